```python
import math
import jax, jax.numpy as jnp
from jax import lax
import numpy as np

D_MODEL = 1024
BATCH = 32
SEQ = 2048
DEPTH = 1

N_META = 16
NORM_EPS = 1e-6
HG_HEADS = 8
HG_DK = D_MODEL // HG_HEADS
HG_DV = D_MODEL // HG_HEADS
HG_WIDTH = HG_HEADS * HG_DK
HG_CHUNK = 16
MLA_HEADS = 8
QK_NOPE = 128
QK_ROPE = 64
V_HEAD = 128
Q_LORA = 256
KV_LORA = 256
ROPE_THETA = 10000.0
ATTN_BLOCK = 128
FFN_HIDDEN = ((8 * D_MODEL + 3 * 256 - 1) // (3 * 256)) * 256
IN_SIZES = (HG_WIDTH, HG_WIDTH, HG_WIDTH, HG_WIDTH,
            Q_LORA, KV_LORA, QK_ROPE,
            D_MODEL, D_MODEL)
IN_COLS = sum(IN_SIZES)

kernel_name = "hybrid_hgrn2_mla_gated_block"


def rms_norm(x, g):
    xf = x.astype(jnp.float32)
    y = xf * lax.rsqrt(jnp.mean(xf * xf, axis=-1, keepdims=True) + NORM_EPS)
    return (y * g.astype(jnp.float32)).astype(x.dtype)


def rope_tables(length):
    pos = jnp.arange(length, dtype=jnp.float32)
    inv_freq = 1.0 / (ROPE_THETA ** (jnp.arange(0, QK_ROPE, 2, dtype=jnp.float32) / QK_ROPE))
    ang = pos[:, None] * inv_freq[None, :]
    return jnp.cos(ang), jnp.sin(ang)


def apply_rope(t, cos, sin):
    t32 = t.astype(jnp.float32)
    t1, t2 = jnp.split(t32, 2, axis=-1)
    return jnp.concatenate([t1 * cos - t2 * sin, t2 * cos + t1 * sin], axis=-1).astype(t.dtype)


def hgrn2_chunk_scan(q, k, v, logf):
    B, L, H, _ = q.shape
    n = L // HG_CHUNK

    def to_chunks(t):
        return t.reshape(B, n, HG_CHUNK, H, t.shape[-1]).transpose(1, 0, 3, 2, 4)

    xs = (to_chunks(q), to_chunks(k), to_chunks(v), to_chunks(logf))
    causal = jnp.tril(jnp.ones((HG_CHUNK, HG_CHUNK), dtype=bool))[:, :, None]

    def step(S, inp):
        qb, kb, vb, gb = inp
        b = jnp.cumsum(gb, axis=2)
        o_inter = jnp.einsum('bhtk,bhkv->bhtv', qb * jnp.exp(b), S)
        diff = b[:, :, :, None, :] - b[:, :, None, :, :]
        decay = jnp.exp(jnp.where(causal, diff, -jnp.inf))
        A = jnp.einsum('bhtsk,bhsk->bhts', decay * qb[:, :, :, None, :], kb)
        o_intra = jnp.einsum('bhts,bhsv->bhtv', A, vb)
        b_last = b[:, :, -1:, :]
        S_new = jnp.exp(b_last[:, :, 0, :])[..., None] * S + jnp.einsum(
            'bhsk,bhsv->bhkv', kb * jnp.exp(b_last - b), vb)
        return S_new, o_inter + o_intra

    S0 = jnp.zeros((B, H, q.shape[-1], v.shape[-1]), jnp.float32)
    _, ys = lax.scan(step, S0, xs)
    return ys.transpose(1, 0, 3, 2, 4).reshape(B, L, H, v.shape[-1])


def hgrn2_mixer(q, f_pre, i, g, lb, norm_g):
    B, L, _ = q.shape
    split = lambda t: t.reshape(B, L, HG_HEADS, -1).astype(jnp.float32)
    qh = jax.nn.silu(split(q))
    lbh = lb.astype(jnp.float32).reshape(HG_HEADS, HG_DK)
    fgate = lbh + (1.0 - lbh) * jax.nn.sigmoid(split(f_pre))
    o = hgrn2_chunk_scan(qh, 1.0 - fgate, split(i), jnp.log(fgate))
    o = rms_norm(o, norm_g) * jax.nn.silu(split(g))
    return o.reshape(B, L, HG_WIDTH).astype(q.dtype)


def mla_mixer(c_q, c_kv, k_pe, q_norm_g, w_q_b, kv_norm_g, w_kv_b, cos, sin):
    B, L, _ = c_q.shape
    q = (rms_norm(c_q, q_norm_g) @ w_q_b).reshape(B, L, MLA_HEADS, QK_NOPE + QK_ROPE)
    q_nope = q[..., :QK_NOPE]
    q_pe = apply_rope(q[..., QK_NOPE:], cos[:, None, :], sin[:, None, :])
    kv = (rms_norm(c_kv, kv_norm_g) @ w_kv_b).reshape(B, L, MLA_HEADS, QK_NOPE + V_HEAD)
    k_nope, v = kv[..., :QK_NOPE], kv[..., QK_NOPE:]
    k_pe = apply_rope(k_pe, cos, sin)
    scale = (QK_NOPE + QK_ROPE) ** -0.5
    bounds = [(0, N_META)] + [(N_META + s, min(N_META + s + ATTN_BLOCK, L))
                              for s in range(0, L - N_META, ATTN_BLOCK)]
    outs = []
    for start, end in bounds:
        s = (jnp.einsum('bqhd,bkhd->bhqk', q_nope[:, start:end], k_nope[:, :end])
             + jnp.einsum('bqhr,bkr->bhqk', q_pe[:, start:end], k_pe[:, :end]))
        s = s.astype(jnp.float32) * scale
        qpos = jnp.arange(start, end)[:, None]
        kpos = jnp.arange(end)[None, :]
        s = jnp.where(kpos <= qpos, s, -jnp.inf)
        p = jax.nn.softmax(s, axis=-1).astype(v.dtype)
        outs.append(jnp.einsum('bhqk,bkhd->bqhd', p, v[:, :end]))
    o = jnp.concatenate(outs, axis=1)
    return o.reshape(B, L, MLA_HEADS * V_HEAD)


def _fwd_setup_inputs(seed: int = 0) -> dict:
    key = jax.random.key(seed)
    ks = jax.random.split(key, 20)
    nrm = lambda k, shape, scale: jax.random.normal(k, shape, jnp.float32) * scale
    gain = lambda k, shape: 1.0 + 0.02 * jax.random.normal(k, shape, jnp.float32)
    return {
        "x": nrm(ks[0], (BATCH, SEQ, D_MODEL), 1.0),
        "meta_tokens": nrm(ks[1], (N_META, D_MODEL), 1.0),
        "w_in": nrm(ks[2], (DEPTH, D_MODEL, IN_COLS), D_MODEL ** -0.5),
        "b_gate": nrm(ks[3], (DEPTH, 2 * D_MODEL), 0.01),
        "lb_logits": nrm(ks[4], (DEPTH + 1, HG_WIDTH), 0.1),
        "hg_norm_g": gain(ks[5], (DEPTH, HG_DV)),
        "w_hg_o": nrm(ks[6], (DEPTH, HG_WIDTH, D_MODEL), HG_WIDTH ** -0.5),
        "q_a_norm_g": gain(ks[7], (DEPTH, Q_LORA)),
        "w_q_b": nrm(ks[8], (DEPTH, Q_LORA, MLA_HEADS * (QK_NOPE + QK_ROPE)), Q_LORA ** -0.5),
        "kv_a_norm_g": gain(ks[9], (DEPTH, KV_LORA)),
        "w_kv_b": nrm(ks[10], (DEPTH, KV_LORA, MLA_HEADS * (QK_NOPE + V_HEAD)), KV_LORA ** -0.5),
        "w_mla_o": nrm(ks[11], (DEPTH, MLA_HEADS * V_HEAD, D_MODEL), (MLA_HEADS * V_HEAD) ** -0.5),
        "w_out": nrm(ks[12], (DEPTH, D_MODEL, D_MODEL), D_MODEL ** -0.5),
        "mix_pre_g": gain(ks[13], (DEPTH, D_MODEL)),
        "mix_post_g": gain(ks[14], (DEPTH, D_MODEL)),
        "ffn_pre_g": gain(ks[15], (DEPTH, D_MODEL)),
        "ffn_post_g": gain(ks[16], (DEPTH, D_MODEL)),
        "w_ffn_in": nrm(ks[17], (DEPTH, D_MODEL, 2 * FFN_HIDDEN), D_MODEL ** -0.5),
        "w_ffn_out": nrm(ks[18], (DEPTH, FFN_HIDDEN, D_MODEL), FFN_HIDDEN ** -0.5),
    }


def _fwd_reference(x, meta_tokens, w_in, b_gate, lb_logits, hg_norm_g, w_hg_o, q_a_norm_g, w_q_b,
              kv_a_norm_g, w_kv_b, w_mla_o, w_out, mix_pre_g, mix_post_g, ffn_pre_g, ffn_post_g,
              w_ffn_in, w_ffn_out):
    B = x.shape[0]
    meta = jnp.broadcast_to(meta_tokens[None].astype(x.dtype), (B, N_META, D_MODEL))
    h = jnp.concatenate([meta, x], axis=1)
    L = h.shape[1]
    cos, sin = rope_tables(L)
    lower_bounds = jnp.cumsum(jax.nn.softmax(lb_logits.astype(jnp.float32), axis=0), axis=0)
    splits = []
    acc = 0
    for sz in IN_SIZES[:-2]:
        acc += sz
        splits.append(acc)
    for l in range(DEPTH):
        u = rms_norm(h, mix_pre_g[l])
        proj = u @ w_in[l]
        hq, hf, hi, hg, cq, ckv, kpe, gates = jnp.split(proj, splits, axis=-1)
        y_a = hgrn2_mixer(hq, hf, hi, hg, lower_bounds[l], hg_norm_g[l]) @ w_hg_o[l]
        y_b = mla_mixer(cq, ckv, kpe, q_a_norm_g[l], w_q_b[l], kv_a_norm_g[l], w_kv_b[l],
                        cos, sin) @ w_mla_o[l]
        gate_a, gate_b = jnp.split(jax.nn.sigmoid(gates + b_gate[l]), 2, axis=-1)
        mixed = (gate_a * y_a + gate_b * y_b) @ w_out[l]
        h = h + rms_norm(mixed, mix_post_g[l])
        u = rms_norm(h, ffn_pre_g[l])
        gt, up = jnp.split(u @ w_ffn_in[l], 2, axis=-1)
        h = h + rms_norm((jax.nn.silu(gt) * up) @ w_ffn_out[l], ffn_post_g[l])
    return h[:, N_META:, :]


import jax as _jax
import jax.numpy as _jnp

TWIN_FORMAT = 'train_step'
FWD_PARAMS = ['x', 'meta_tokens', 'w_in', 'b_gate', 'lb_logits', 'hg_norm_g', 'w_hg_o', 'q_a_norm_g', 'w_q_b', 'kv_a_norm_g', 'w_kv_b', 'w_mla_o', 'w_out', 'mix_pre_g', 'mix_post_g', 'ffn_pre_g', 'ffn_post_g', 'w_ffn_in', 'w_ffn_out']
TWIN_WEIGHTS = ['meta_tokens', 'w_in', 'b_gate', 'lb_logits', 'hg_norm_g', 'w_hg_o', 'q_a_norm_g', 'w_q_b', 'kv_a_norm_g', 'w_kv_b', 'w_mla_o', 'w_out', 'mix_pre_g', 'mix_post_g', 'ffn_pre_g', 'ffn_post_g', 'w_ffn_in', 'w_ffn_out']
TWIN_DIFF_INPUT = 'x'
TWIN_INPUTS = ['x', 'meta_tokens', 'w_in', 'b_gate', 'lb_logits', 'hg_norm_g', 'w_hg_o', 'q_a_norm_g', 'w_q_b', 'kv_a_norm_g', 'w_kv_b', 'w_mla_o', 'w_out', 'mix_pre_g', 'mix_post_g', 'ffn_pre_g', 'ffn_post_g', 'w_ffn_in', 'w_ffn_out', 'loss_target', 'm_meta_tokens', 'm_w_in', 'm_b_gate', 'm_lb_logits', 'm_hg_norm_g', 'm_w_hg_o', 'm_q_a_norm_g', 'm_w_q_b', 'm_kv_a_norm_g', 'm_w_kv_b', 'm_w_mla_o', 'm_w_out', 'm_mix_pre_g', 'm_mix_post_g', 'm_ffn_pre_g', 'm_ffn_post_g', 'm_w_ffn_in', 'm_w_ffn_out', 'v_meta_tokens', 'v_w_in', 'v_b_gate', 'v_lb_logits', 'v_hg_norm_g', 'v_w_hg_o', 'v_q_a_norm_g', 'v_w_q_b', 'v_kv_a_norm_g', 'v_w_kv_b', 'v_w_mla_o', 'v_w_out', 'v_mix_pre_g', 'v_mix_post_g', 'v_ffn_pre_g', 'v_ffn_post_g', 'v_w_ffn_in', 'v_w_ffn_out']
TWIN_OUTPUTS = ['loss', 'grad_x', 'grad_meta_tokens', 'grad_w_in', 'grad_b_gate', 'grad_lb_logits', 'grad_hg_norm_g', 'grad_w_hg_o', 'grad_q_a_norm_g', 'grad_w_q_b', 'grad_kv_a_norm_g', 'grad_w_kv_b', 'grad_w_mla_o', 'grad_w_out', 'grad_mix_pre_g', 'grad_mix_post_g', 'grad_ffn_pre_g', 'grad_ffn_post_g', 'grad_w_ffn_in', 'grad_w_ffn_out', 'delta_meta_tokens', 'delta_w_in', 'delta_b_gate', 'delta_lb_logits', 'delta_hg_norm_g', 'delta_w_hg_o', 'delta_q_a_norm_g', 'delta_w_q_b', 'delta_kv_a_norm_g', 'delta_w_kv_b', 'delta_w_mla_o', 'delta_w_out', 'delta_mix_pre_g', 'delta_mix_post_g', 'delta_ffn_pre_g', 'delta_ffn_post_g', 'delta_w_ffn_in', 'delta_w_ffn_out', 'new_m_meta_tokens', 'new_m_w_in', 'new_m_b_gate', 'new_m_lb_logits', 'new_m_hg_norm_g', 'new_m_w_hg_o', 'new_m_q_a_norm_g', 'new_m_w_q_b', 'new_m_kv_a_norm_g', 'new_m_w_kv_b', 'new_m_w_mla_o', 'new_m_w_out', 'new_m_mix_pre_g', 'new_m_mix_post_g', 'new_m_ffn_pre_g', 'new_m_ffn_post_g', 'new_m_w_ffn_in', 'new_m_w_ffn_out', 'new_v_meta_tokens', 'new_v_w_in', 'new_v_b_gate', 'new_v_lb_logits', 'new_v_hg_norm_g', 'new_v_w_hg_o', 'new_v_q_a_norm_g', 'new_v_w_q_b', 'new_v_kv_a_norm_g', 'new_v_w_kv_b', 'new_v_w_mla_o', 'new_v_w_out', 'new_v_mix_pre_g', 'new_v_mix_post_g', 'new_v_ffn_pre_g', 'new_v_ffn_post_g', 'new_v_w_ffn_in', 'new_v_w_ffn_out']
TWIN_LEAF_KINDS = {'loss': 'loss', 'grad_x': 'grad_x', 'grad_meta_tokens': 'grad_w', 'grad_w_in': 'grad_w', 'grad_b_gate': 'grad_w', 'grad_lb_logits': 'grad_w', 'grad_hg_norm_g': 'grad_w', 'grad_w_hg_o': 'grad_w', 'grad_q_a_norm_g': 'grad_w', 'grad_w_q_b': 'grad_w', 'grad_kv_a_norm_g': 'grad_w', 'grad_w_kv_b': 'grad_w', 'grad_w_mla_o': 'grad_w', 'grad_w_out': 'grad_w', 'grad_mix_pre_g': 'grad_w', 'grad_mix_post_g': 'grad_w', 'grad_ffn_pre_g': 'grad_w', 'grad_ffn_post_g': 'grad_w', 'grad_w_ffn_in': 'grad_w', 'grad_w_ffn_out': 'grad_w', 'delta_meta_tokens': 'delta_w', 'delta_w_in': 'delta_w', 'delta_b_gate': 'delta_w', 'delta_lb_logits': 'delta_w', 'delta_hg_norm_g': 'delta_w', 'delta_w_hg_o': 'delta_w', 'delta_q_a_norm_g': 'delta_w', 'delta_w_q_b': 'delta_w', 'delta_kv_a_norm_g': 'delta_w', 'delta_w_kv_b': 'delta_w', 'delta_w_mla_o': 'delta_w', 'delta_w_out': 'delta_w', 'delta_mix_pre_g': 'delta_w', 'delta_mix_post_g': 'delta_w', 'delta_ffn_pre_g': 'delta_w', 'delta_ffn_post_g': 'delta_w', 'delta_w_ffn_in': 'delta_w', 'delta_w_ffn_out': 'delta_w', 'new_m_meta_tokens': 'new_m', 'new_m_w_in': 'new_m', 'new_m_b_gate': 'new_m', 'new_m_lb_logits': 'new_m', 'new_m_hg_norm_g': 'new_m', 'new_m_w_hg_o': 'new_m', 'new_m_q_a_norm_g': 'new_m', 'new_m_w_q_b': 'new_m', 'new_m_kv_a_norm_g': 'new_m', 'new_m_w_kv_b': 'new_m', 'new_m_w_mla_o': 'new_m', 'new_m_w_out': 'new_m', 'new_m_mix_pre_g': 'new_m', 'new_m_mix_post_g': 'new_m', 'new_m_ffn_pre_g': 'new_m', 'new_m_ffn_post_g': 'new_m', 'new_m_w_ffn_in': 'new_m', 'new_m_w_ffn_out': 'new_m', 'new_v_meta_tokens': 'new_v', 'new_v_w_in': 'new_v', 'new_v_b_gate': 'new_v', 'new_v_lb_logits': 'new_v', 'new_v_hg_norm_g': 'new_v', 'new_v_w_hg_o': 'new_v', 'new_v_q_a_norm_g': 'new_v', 'new_v_w_q_b': 'new_v', 'new_v_kv_a_norm_g': 'new_v', 'new_v_w_kv_b': 'new_v', 'new_v_w_mla_o': 'new_v', 'new_v_w_out': 'new_v', 'new_v_mix_pre_g': 'new_v', 'new_v_mix_post_g': 'new_v', 'new_v_ffn_pre_g': 'new_v', 'new_v_ffn_post_g': 'new_v', 'new_v_w_ffn_in': 'new_v', 'new_v_w_ffn_out': 'new_v'}


def _forward(args):
    return _fwd_reference(*[args[k] for k in FWD_PARAMS])


def _output_shape():
    out = _jax.eval_shape(lambda: _forward(_fwd_setup_inputs(0)))
    return out.shape, out.dtype

N_MICROBATCH = 1
ADAM_LR = 0.001
ADAM_B1 = 0.9
ADAM_B2 = 0.999
ADAM_EPS = 1e-08
ADAM_WD = 0.01
ADAM_STEP = 10
PER_EXAMPLE_BATCH_AXIS = {'x': 0, 'loss_target': 0}
SHARED_INPUTS = []
_WEIGHT_DTYPES = {'meta_tokens': _jnp.float32, 'w_in': _jnp.float32, 'b_gate': _jnp.float32, 'lb_logits': _jnp.float32, 'hg_norm_g': _jnp.float32, 'w_hg_o': _jnp.float32, 'q_a_norm_g': _jnp.float32, 'w_q_b': _jnp.float32, 'kv_a_norm_g': _jnp.float32, 'w_kv_b': _jnp.float32, 'w_mla_o': _jnp.float32, 'w_out': _jnp.float32, 'mix_pre_g': _jnp.float32, 'mix_post_g': _jnp.float32, 'ffn_pre_g': _jnp.float32, 'ffn_post_g': _jnp.float32, 'w_ffn_in': _jnp.float32, 'w_ffn_out': _jnp.float32}
MOMENT_SCALE = {'meta_tokens': 3.278293e-02, 'w_in': 3.704235e-01, 'b_gate': 2.170659e-01, 'lb_logits': 6.096196e-02, 'hg_norm_g': 3.183535e+00, 'w_hg_o': 7.090989e-01, 'q_a_norm_g': 3.299230e-01, 'w_q_b': 1.309382e-01, 'kv_a_norm_g': 4.868009e-01, 'w_kv_b': 1.602561e-01, 'w_mla_o': 1.799178e-01, 'w_out': 7.576645e-01, 'mix_pre_g': 9.973438e-01, 'mix_post_g': 6.358530e+01, 'ffn_pre_g': 7.779534e-01, 'ffn_post_g': 6.406145e+01, 'w_ffn_in': 3.414355e-01, 'w_ffn_out': 6.881411e-01}


def _to_microbatches(a, axis):
    t = _jnp.moveaxis(a, axis, 0)
    t = t.reshape((N_MICROBATCH, t.shape[0] // N_MICROBATCH) + t.shape[1:])
    return _jnp.moveaxis(t, 1, axis + 1)


def setup_inputs(seed: int = 0) -> dict:
    inp = _fwd_setup_inputs(seed)
    key = _jax.random.fold_in(_jax.random.key(seed), 7919)
    shape, _ = _output_shape()
    out = dict(inp)
    out["loss_target"] = _jax.random.normal(_jax.random.fold_in(key, 0), shape, _jnp.float32)
    for i, name in enumerate(TWIN_WEIGHTS):
        w = inp[name].astype(_jnp.float32)
        if MOMENT_SCALE is None:
            s = _jnp.sqrt(_jnp.mean(_jnp.square(w)) + 1e-30)
        else:
            s = MOMENT_SCALE[name]
        km, kv = _jax.random.split(_jax.random.fold_in(key, i + 1))
        out[name] = w
        out["m_" + name] = s * _jax.random.normal(km, w.shape, _jnp.float32)
        out["v_" + name] = (s * s) * _jax.random.uniform(kv, w.shape, _jnp.float32, 0.5, 1.5)
    if N_MICROBATCH > 1:
        for name, axis in PER_EXAMPLE_BATCH_AXIS.items():
            out[name] = _to_microbatches(out[name], axis)
    return {'x': out['x'], 'meta_tokens': out['meta_tokens'], 'w_in': out['w_in'], 'b_gate': out['b_gate'], 'lb_logits': out['lb_logits'], 'hg_norm_g': out['hg_norm_g'], 'w_hg_o': out['w_hg_o'], 'q_a_norm_g': out['q_a_norm_g'], 'w_q_b': out['w_q_b'], 'kv_a_norm_g': out['kv_a_norm_g'], 'w_kv_b': out['w_kv_b'], 'w_mla_o': out['w_mla_o'], 'w_out': out['w_out'], 'mix_pre_g': out['mix_pre_g'], 'mix_post_g': out['mix_post_g'], 'ffn_pre_g': out['ffn_pre_g'], 'ffn_post_g': out['ffn_post_g'], 'w_ffn_in': out['w_ffn_in'], 'w_ffn_out': out['w_ffn_out'], 'loss_target': out['loss_target'], 'm_meta_tokens': out['m_meta_tokens'], 'm_w_in': out['m_w_in'], 'm_b_gate': out['m_b_gate'], 'm_lb_logits': out['m_lb_logits'], 'm_hg_norm_g': out['m_hg_norm_g'], 'm_w_hg_o': out['m_w_hg_o'], 'm_q_a_norm_g': out['m_q_a_norm_g'], 'm_w_q_b': out['m_w_q_b'], 'm_kv_a_norm_g': out['m_kv_a_norm_g'], 'm_w_kv_b': out['m_w_kv_b'], 'm_w_mla_o': out['m_w_mla_o'], 'm_w_out': out['m_w_out'], 'm_mix_pre_g': out['m_mix_pre_g'], 'm_mix_post_g': out['m_mix_post_g'], 'm_ffn_pre_g': out['m_ffn_pre_g'], 'm_ffn_post_g': out['m_ffn_post_g'], 'm_w_ffn_in': out['m_w_ffn_in'], 'm_w_ffn_out': out['m_w_ffn_out'], 'v_meta_tokens': out['v_meta_tokens'], 'v_w_in': out['v_w_in'], 'v_b_gate': out['v_b_gate'], 'v_lb_logits': out['v_lb_logits'], 'v_hg_norm_g': out['v_hg_norm_g'], 'v_w_hg_o': out['v_w_hg_o'], 'v_q_a_norm_g': out['v_q_a_norm_g'], 'v_w_q_b': out['v_w_q_b'], 'v_kv_a_norm_g': out['v_kv_a_norm_g'], 'v_w_kv_b': out['v_w_kv_b'], 'v_w_mla_o': out['v_w_mla_o'], 'v_w_out': out['v_w_out'], 'v_mix_pre_g': out['v_mix_pre_g'], 'v_mix_post_g': out['v_mix_post_g'], 'v_ffn_pre_g': out['v_ffn_pre_g'], 'v_ffn_post_g': out['v_ffn_post_g'], 'v_w_ffn_in': out['v_w_ffn_in'], 'v_w_ffn_out': out['v_w_ffn_out']}


def _loss(weights, diff, rest, loss_target):
    with _jax.named_scope("forward"):
        args = {**rest, TWIN_DIFF_INPUT: diff, **{k: w.astype(_WEIGHT_DTYPES[k]) for k, w in weights.items()}}
        y = _forward(args)
    with _jax.named_scope("loss_head"):
        err = _jnp.square(y.astype(_jnp.float32) - loss_target)
        return 0.5 * _jnp.sum(_jnp.mean(err, axis=-1)) if err.ndim else 0.5 * err


def _adamw(w, g, m, v):
    m = ADAM_B1 * m + (1.0 - ADAM_B1) * g
    v = ADAM_B2 * v + (1.0 - ADAM_B2) * _jnp.square(g)
    m_hat = m / (1.0 - ADAM_B1 ** ADAM_STEP)
    v_hat = v / (1.0 - ADAM_B2 ** ADAM_STEP)
    delta = -ADAM_LR * (m_hat / (_jnp.sqrt(v_hat) + ADAM_EPS) + ADAM_WD * w)
    return delta, m, v


def reference(x, meta_tokens, w_in, b_gate, lb_logits, hg_norm_g, w_hg_o, q_a_norm_g, w_q_b, kv_a_norm_g, w_kv_b, w_mla_o, w_out, mix_pre_g, mix_post_g, ffn_pre_g, ffn_post_g, w_ffn_in, w_ffn_out, loss_target, m_meta_tokens, m_w_in, m_b_gate, m_lb_logits, m_hg_norm_g, m_w_hg_o, m_q_a_norm_g, m_w_q_b, m_kv_a_norm_g, m_w_kv_b, m_w_mla_o, m_w_out, m_mix_pre_g, m_mix_post_g, m_ffn_pre_g, m_ffn_post_g, m_w_ffn_in, m_w_ffn_out, v_meta_tokens, v_w_in, v_b_gate, v_lb_logits, v_hg_norm_g, v_w_hg_o, v_q_a_norm_g, v_w_q_b, v_kv_a_norm_g, v_w_kv_b, v_w_mla_o, v_w_out, v_mix_pre_g, v_mix_post_g, v_ffn_pre_g, v_ffn_post_g, v_w_ffn_in, v_w_ffn_out):
    given = dict(x=x, meta_tokens=meta_tokens, w_in=w_in, b_gate=b_gate, lb_logits=lb_logits, hg_norm_g=hg_norm_g, w_hg_o=w_hg_o, q_a_norm_g=q_a_norm_g, w_q_b=w_q_b, kv_a_norm_g=kv_a_norm_g, w_kv_b=w_kv_b, w_mla_o=w_mla_o, w_out=w_out, mix_pre_g=mix_pre_g, mix_post_g=mix_post_g, ffn_pre_g=ffn_pre_g, ffn_post_g=ffn_post_g, w_ffn_in=w_ffn_in, w_ffn_out=w_ffn_out, loss_target=loss_target, m_meta_tokens=m_meta_tokens, m_w_in=m_w_in, m_b_gate=m_b_gate, m_lb_logits=m_lb_logits, m_hg_norm_g=m_hg_norm_g, m_w_hg_o=m_w_hg_o, m_q_a_norm_g=m_q_a_norm_g, m_w_q_b=m_w_q_b, m_kv_a_norm_g=m_kv_a_norm_g, m_w_kv_b=m_w_kv_b, m_w_mla_o=m_w_mla_o, m_w_out=m_w_out, m_mix_pre_g=m_mix_pre_g, m_mix_post_g=m_mix_post_g, m_ffn_pre_g=m_ffn_pre_g, m_ffn_post_g=m_ffn_post_g, m_w_ffn_in=m_w_ffn_in, m_w_ffn_out=m_w_ffn_out, v_meta_tokens=v_meta_tokens, v_w_in=v_w_in, v_b_gate=v_b_gate, v_lb_logits=v_lb_logits, v_hg_norm_g=v_hg_norm_g, v_w_hg_o=v_w_hg_o, v_q_a_norm_g=v_q_a_norm_g, v_w_q_b=v_w_q_b, v_kv_a_norm_g=v_kv_a_norm_g, v_w_kv_b=v_w_kv_b, v_w_mla_o=v_w_mla_o, v_w_out=v_w_out, v_mix_pre_g=v_mix_pre_g, v_mix_post_g=v_mix_post_g, v_ffn_pre_g=v_ffn_pre_g, v_ffn_post_g=v_ffn_post_g, v_w_ffn_in=v_w_ffn_in, v_w_ffn_out=v_w_ffn_out)
    weights = {n: given[n] for n in TWIN_WEIGHTS}
    shared = {n: given[n] for n in SHARED_INPUTS}
    per_example = {n: given[n] for n in ['x']}
    grad_fn = _jax.value_and_grad(_loss, argnums=(0, 1))

    def one_microbatch(ex, loss_target):
        ex = dict(ex)
        diff = ex.pop(TWIN_DIFF_INPUT)
        return grad_fn(weights, diff, {**shared, **ex}, loss_target)

    if N_MICROBATCH == 1:
        loss, (grad_w, grad_x) = one_microbatch(per_example, given["loss_target"])
    else:
        def body(carry, xs):
            loss_sum, grad_sum = carry
            l_k, (gw_k, gx_k) = one_microbatch(xs[0], xs[1])
            with _jax.named_scope("update"):
                return (loss_sum + l_k, _jax.tree.map(_jnp.add, grad_sum, gw_k)), gx_k

        init = (_jnp.zeros((), _jnp.float32), _jax.tree.map(_jnp.zeros_like, weights))
        (loss, grad_w), grad_x = _jax.lax.scan(body, init, (per_example, given["loss_target"]))
    with _jax.named_scope("update"):
        delta_w, new_m, new_v = {}, {}, {}
        for n in TWIN_WEIGHTS:
            delta_w[n], new_m[n], new_v[n] = _adamw(weights[n], grad_w[n], given["m_" + n], given["v_" + n])
    return (loss, grad_x, *[grad_w[n] for n in TWIN_WEIGHTS], *[delta_w[n] for n in TWIN_WEIGHTS],
            *[new_m[n] for n in TWIN_WEIGHTS], *[new_v[n] for n in TWIN_WEIGHTS])
```

```python
import functools

import jax
import jax.numpy as jnp
from jax import lax
from jax.experimental import pallas as pl
from jax.experimental.pallas import tpu as pltpu

F32 = jnp.float32
_MXU_DTYPE = jnp.bfloat16
_WIRE_DTYPE = jnp.bfloat16
_VMEM_LIMIT_BYTES = 56 * 1024 * 1024
_LANES = 128
_SUBLANES = 8

N_DEV = 8
N_META = 16
NORM_EPS = 1e-6
HEADS = 8
HEAD_DIM = 128
ROPE_DIM = 64
HG_CHUNK = 16
HG_BLOCK = 128
ROPE_THETA = 10000.0
D_MODEL = 1024
Q_LORA = 256
KV_LORA = 256
FFN_HIDDEN = 2816
ATTN_SCALE = (HEAD_DIM + ROPE_DIM) ** -0.5
NEG_BIG = -1e30

ADAM_LR = 0.001
ADAM_B1 = 0.9
ADAM_B2 = 0.999
ADAM_EPS = 1e-08
ADAM_WD = 0.01
ADAM_STEP = 10

CB_HQ, CB_HF, CB_HI, CB_HG, CB_C, CB_GA, CB_GB = range(7)
IN_COLS_PADDED = 7 * 1024


def _params(**kw):
    return pltpu.CompilerParams(vmem_limit_bytes=_VMEM_LIMIT_BYTES, **kw)


def _dot(a, b):
    return lax.dot_general(a, b, (((1,), (0,)), ((), ())), preferred_element_type=F32)


def _dot_nt(a, b):
    return lax.dot_general(a, b, (((1,), (1,)), ((), ())), preferred_element_type=F32)


def _dot_tn(a, b):
    return lax.dot_general(a, b, (((0,), (0,)), ((), ())), preferred_element_type=F32)


def _mx(x):
    return x.astype(_MXU_DTYPE)


def _exact_dot(m01, x):
    if _MXU_DTYPE == jnp.float32:
        return _dot(m01.astype(F32), x)
    m = m01.astype(jnp.bfloat16)
    x1 = x.astype(jnp.bfloat16)
    r1 = x - x1.astype(F32)
    x2 = r1.astype(jnp.bfloat16)
    x3 = (r1 - x2.astype(F32)).astype(jnp.bfloat16)
    return _dot(m, x1) + _dot(m, x2) + _dot(m, x3)


def _sigmoid(x):
    return jax.nn.sigmoid(x)


def _silu_grad(x, s):
    return s * (1.0 + x * (1.0 - s))


def _rms_scale(x):
    return lax.rsqrt(jnp.mean(x * x, axis=-1, keepdims=True) + NORM_EPS)


def _rms_bwd(x, g, dy):
    r = _rms_scale(x)
    xh = x * r
    w = dy * g
    dx = r * (w - xh * jnp.mean(xh * w, axis=-1, keepdims=True))
    return dx, dy * xh


def _heads(fn, *arrays):
    outs = [fn(*[a[:, h * HEAD_DIM:(h + 1) * HEAD_DIM] for a in arrays]) for h in range(HEADS)]
    if isinstance(outs[0], tuple):
        return tuple(jnp.concatenate([o[i] for o in outs], axis=1) for i in range(len(outs[0])))
    return jnp.concatenate(outs, axis=1)


def _matmul(name, a, b, *, out_dtype, tm, tn, tk, c_in=None):
    m, k = a.shape
    _, n = b.shape
    assert m % tm == 0 and n % tn == 0 and k % tk == 0, (name, a.shape, b.shape, tm, tn, tk)
    nk = k // tk
    has_c = c_in is not None

    def body(*refs):
        a_ref, b_ref = refs[0], refs[1]
        c_ref = refs[2] if has_c else None
        o_ref = refs[3] if has_c else refs[2]

        def finish(r):
            if has_c:
                r = r + c_ref[...]
            o_ref[...] = r.astype(o_ref.dtype)

        if nk == 1:
            finish(_dot(a_ref[...], b_ref[...]))
        else:
            acc = refs[-1]
            kk = pl.program_id(2)

            @pl.when(kk == 0)
            def _():
                acc[...] = jnp.zeros_like(acc)

            acc[...] += _dot(a_ref[...], b_ref[...])

            @pl.when(kk == nk - 1)
            def _():
                finish(acc[...])

    in_specs = [pl.BlockSpec((tm, tk), lambda j, i, kk: (i, kk)),
                pl.BlockSpec((tk, tn), lambda j, i, kk: (kk, j))]
    args = [a, b]
    aliases = {}
    if has_c:
        in_specs.append(pl.BlockSpec((tm, tn), lambda j, i, kk: (i, j)))
        args.append(c_in)
        aliases = {2: 0}
    return pl.pallas_call(
        body, name=name,
        out_shape=jax.ShapeDtypeStruct((m, n), out_dtype),
        grid=(n // tn, m // tm, nk),
        in_specs=in_specs,
        out_specs=pl.BlockSpec((tm, tn), lambda j, i, kk: (i, j)),
        scratch_shapes=[pltpu.VMEM((tm, tn), F32)] if nk > 1 else [],
        input_output_aliases=aliases,
        compiler_params=_params(),
    )(*args)


def _matmul_tn(name, x, dy, *, tk, tn, tr):
    r, k = x.shape
    _, n = dy.shape
    assert r % tr == 0 and k % tk == 0 and n % tn == 0, (name, x.shape, dy.shape)

    def body(x_ref, dy_ref, o_ref):
        @pl.when(pl.program_id(2) == 0)
        def _():
            o_ref[...] = jnp.zeros_like(o_ref)

        o_ref[...] += _dot_tn(x_ref[...], dy_ref[...])

    return pl.pallas_call(
        body, name=name,
        out_shape=jax.ShapeDtypeStruct((k, n), F32),
        grid=(k // tk, n // tn, r // tr),
        in_specs=[pl.BlockSpec((tr, tk), lambda kb, nb, rr: (rr, kb)),
                  pl.BlockSpec((tr, tn), lambda kb, nb, rr: (rr, nb))],
        out_specs=pl.BlockSpec((tk, tn), lambda kb, nb, rr: (kb, nb)),
        compiler_params=_params(),
    )(x, dy)


def _rowwise(name, body, *, rows, tr, lp, ins, outs, accs=()):
    assert rows % tr == 0 and lp % tr == 0 and tr % 16 == 0
    tiles_per_example = lp // tr
    in_specs, arrays = [], []
    for spec in ins:
        if spec[0] == "row":
            _, arr, width, cb = spec
            in_specs.append(pl.BlockSpec((tr, width), functools.partial(lambda i, cb: (i, cb), cb=cb)))
        elif spec[0] == "const":
            arr = spec[1]
            in_specs.append(pl.BlockSpec(arr.shape, lambda i: (0, 0)))
        else:
            arr = spec[1]
            in_specs.append(pl.BlockSpec((tr, arr.shape[1]), lambda i: (i % tiles_per_example, 0)))
        arrays.append(arr)
    n_in, n_out = len(ins), len(outs)

    def kern(*refs):
        res_outs, res_accs = body(*[r[...] for r in refs[:n_in]])
        for ref, val in zip(refs[n_in:n_in + n_out], res_outs, strict=True):
            ref[...] = val.astype(ref.dtype)
        acc_refs = refs[n_in + n_out:]
        if acc_refs:
            @pl.when(pl.program_id(0) == 0)
            def _():
                for ref in acc_refs:
                    ref[...] = jnp.zeros_like(ref)

            for ref, val in zip(acc_refs, res_accs, strict=True):
                ref[...] += val.reshape(tr // _SUBLANES, _SUBLANES, val.shape[-1]).sum(axis=0)

    out_shape = ([jax.ShapeDtypeStruct((rows, w), dt) for w, dt in outs]
                 + [jax.ShapeDtypeStruct((_SUBLANES, w), F32) for w in accs])
    out_specs = ([pl.BlockSpec((tr, w), lambda i: (i, 0)) for w, _ in outs]
                 + [pl.BlockSpec((_SUBLANES, w), lambda i: (0, 0)) for w in accs])
    res = pl.pallas_call(
        kern, name=name, out_shape=out_shape, grid=(rows // tr,),
        in_specs=in_specs, out_specs=out_specs, compiler_params=_params(),
    )(*arrays)
    return res[:n_out], [a.sum(axis=0) for a in res[n_out:]]


def _assemble(name, x, head_rows, lp):
    batch, seq, d = x.shape
    tc = 256

    def body(x_ref, m_ref, o_ref):
        o_ref[0:N_META, :] = m_ref[...]
        o_ref[N_META:N_META + seq, :] = x_ref[0]
        if lp > N_META + seq:
            o_ref[N_META + seq:, :] = jnp.zeros((lp - N_META - seq, tc), F32)

    return pl.pallas_call(
        body, name=name,
        out_shape=jax.ShapeDtypeStruct((batch * lp, d), F32),
        grid=(batch, d // tc),
        in_specs=[pl.BlockSpec((1, seq, tc), lambda b, j: (b, 0, j)),
                  pl.BlockSpec((N_META, tc), lambda b, j: (0, j))],
        out_specs=pl.BlockSpec((lp, tc), lambda b, j: (b, j)),
        compiler_params=_params(),
    )(x, head_rows)


def _meta_grad(dh0, batch, lp):
    d = dh0.shape[1]

    def body(g_ref, o_ref):
        @pl.when(pl.program_id(0) == 0)
        def _():
            o_ref[...] = jnp.zeros_like(o_ref)

        o_ref[...] += g_ref[...]

    return pl.pallas_call(
        body, name="meta_grad",
        out_shape=jax.ShapeDtypeStruct((N_META, d), F32),
        grid=(batch,),
        in_specs=[pl.BlockSpec((N_META, d), lambda b: (b * (lp // N_META), 0))],
        out_specs=pl.BlockSpec((N_META, d), lambda b: (0, 0)),
        compiler_params=_params(),
    )(dh0)


def _segment_masks():
    t = lax.broadcasted_iota(jnp.int32, (HG_BLOCK, HG_BLOCK), 0)
    s = lax.broadcasted_iota(jnp.int32, (HG_BLOCK, HG_BLOCK), 1)
    same = lax.shift_right_logical(t, 4) == lax.shift_right_logical(s, 4)
    lower = same & (s <= t)
    upper = same & (s >= t)
    first_half = same & ((s & 15) <= 7)
    return same, lower, upper, first_half


def _hgrn_gates(hq, hf, lb):
    sq = _sigmoid(hq)
    q = hq * sq
    sf = _sigmoid(hf)
    f = lb + (1.0 - lb) * sf
    return q, sq, sf, f


def _hgrn_decays(g, same, lower, first_half):
    b = _exact_dot(lower, g)
    b_last = _exact_dot(same, g)
    b_ref = _exact_dot(first_half, g)
    return b, b_last, b_ref


def _hgrn_fwd(p, lb, gh, *, batch, lp):
    rows = batch * lp
    nb = lp // HG_BLOCK
    n_chunks = HG_BLOCK // HG_CHUNK

    def body(hq_ref, hf_ref, hi_ref, hg_ref, lb_ref, gh_ref, o_ref, z_ref, st_ref,
             s_scr, qt_scr, kh_scr, v_scr, el_scr, o_scr):
        @pl.when(pl.program_id(1) == 0)
        def _():
            s_scr[...] = jnp.zeros_like(s_scr)

        same, lower, _, first_half = _segment_masks()
        v = hi_ref[...]
        q, _, _, f = _hgrn_gates(hq_ref[...], hf_ref[...], lb_ref[...])
        k = 1.0 - f
        b, b_last, b_ref = _hgrn_decays(jnp.log(f), same, lower, first_half)
        qt_scr[...] = _mx(q * jnp.exp(b))
        kh_scr[...] = _mx(k * jnp.exp(b_last - b))
        v_scr[...] = _mx(v)
        el_scr[...] = jnp.exp(b_last)
        qc = _mx(q * jnp.exp(b - b_ref))
        kc = _mx(k * jnp.exp(b_ref - b))

        def intra(qc_h, kc_h, v_h):
            a = jnp.where(lower, _dot_nt(qc_h, kc_h), 0.0)
            return _dot(_mx(a), v_h)

        o_scr[...] = _heads(intra, qc, kc, _mx(v))

        def chunk(c, carry):
            r0 = pl.multiple_of(c * HG_CHUNK, HG_CHUNK)
            rs = pl.ds(r0, HG_CHUNK)
            for h in range(HEADS):
                cs = slice(h * HEAD_DIM, (h + 1) * HEAD_DIM)
                st = s_scr[h]
                st_ref[c, h] = st.astype(st_ref.dtype)
                o_scr[rs, cs] += _dot_nt(qt_scr[rs, cs], _mx(st))
                s_scr[h] = st * el_scr[pl.ds(r0, 1), cs] + _dot_tn(v_scr[rs, cs], kh_scr[rs, cs])
            return carry

        lax.fori_loop(0, n_chunks, chunk, 0)

        o = o_scr[...]
        o_ref[...] = o
        hg = hg_ref[...]
        n = _heads(lambda o_h: o_h * _rms_scale(o_h), o) * gh_ref[...]
        z_ref[...] = (n * hg * _sigmoid(hg)).astype(z_ref.dtype)

    blk = lambda cb: pl.BlockSpec((HG_BLOCK, D_MODEL), functools.partial(lambda b, j, cb: (b * nb + j, cb), cb=cb))
    row_out = pl.BlockSpec((HG_BLOCK, D_MODEL), lambda b, j: (b * nb + j, 0))
    const = pl.BlockSpec((1, D_MODEL), lambda b, j: (0, 0))
    return pl.pallas_call(
        body, name="hgrn_fwd",
        out_shape=[jax.ShapeDtypeStruct((rows, D_MODEL), F32),
                   jax.ShapeDtypeStruct((rows, D_MODEL), _MXU_DTYPE),
                   jax.ShapeDtypeStruct((rows // HG_CHUNK, HEADS, HEAD_DIM, HEAD_DIM), _MXU_DTYPE)],
        grid=(batch, nb),
        in_specs=[blk(CB_HQ), blk(CB_HF), blk(CB_HI), blk(CB_HG), const, const],
        out_specs=[row_out, row_out,
                   pl.BlockSpec((n_chunks, HEADS, HEAD_DIM, HEAD_DIM), lambda b, j: (b * nb + j, 0, 0, 0))],
        scratch_shapes=[pltpu.VMEM((HEADS, HEAD_DIM, HEAD_DIM), F32),
                        pltpu.VMEM((HG_BLOCK, D_MODEL), _MXU_DTYPE),
                        pltpu.VMEM((HG_BLOCK, D_MODEL), _MXU_DTYPE),
                        pltpu.VMEM((HG_BLOCK, D_MODEL), _MXU_DTYPE),
                        pltpu.VMEM((HG_BLOCK, D_MODEL), F32),
                        pltpu.VMEM((HG_BLOCK, D_MODEL), F32)],
        compiler_params=_params(),
    )(p, p, p, p, lb, gh)


def _hgrn_bwd(p, o, dz, states, lb, gh, *, batch, lp):
    rows = batch * lp
    nb = lp // HG_BLOCK
    n_chunks = HG_BLOCK // HG_CHUNK

    def body(hq_ref, hf_ref, hi_ref, hg_ref, o_ref, dz_ref, st_ref, lb_ref, gh_ref,
             dp_ref, dlb_ref, dgh_ref,
             ds_scr, qt_scr, kh_scr, v_scr, do_scr, el_scr, dqt_scr, dkh_scr, dv_scr, dbl_scr):
        first = (pl.program_id(0) == 0) & (pl.program_id(1) == 0)

        @pl.when(first)
        def _():
            dlb_ref[...] = jnp.zeros_like(dlb_ref)
            dgh_ref[...] = jnp.zeros_like(dgh_ref)

        @pl.when(pl.program_id(1) == 0)
        def _():
            ds_scr[...] = jnp.zeros_like(ds_scr)

        same, lower, upper, first_half = _segment_masks()
        lbv = lb_ref[...]
        hq, hf, v, hg = hq_ref[...], hf_ref[...], hi_ref[...], hg_ref[...]
        q, sq, sf, f = _hgrn_gates(hq, hf, lbv)
        k = 1.0 - f
        b, b_last, b_ref = _hgrn_decays(jnp.log(f), same, lower, first_half)
        e_b = jnp.exp(b)
        e_kh = jnp.exp(b_last - b)
        e_qc = jnp.exp(b - b_ref)
        e_kc = jnp.exp(b_ref - b)
        qt, kh, qc, kc = q * e_b, k * e_kh, q * e_qc, k * e_kc

        o = o_ref[...]
        dz = dz_ref[...].astype(F32)
        ghv = gh_ref[...]
        sg = _sigmoid(hg)
        r = _heads(lambda o_h: jnp.broadcast_to(_rms_scale(o_h), o_h.shape), o)
        oh = o * r
        dn = dz * hg * sg
        dhg = dz * oh * ghv * _silu_grad(hg, sg)
        w = dn * ghv
        do = r * (w - oh * _heads(lambda t: jnp.broadcast_to(jnp.mean(t, axis=-1, keepdims=True), t.shape), oh * w))
        dgh_ref[...] += (dn * oh).reshape(HG_BLOCK // _SUBLANES, _SUBLANES, D_MODEL).sum(axis=0)

        qt_scr[...] = _mx(qt)
        kh_scr[...] = _mx(kh)
        v_scr[...] = _mx(v)
        do_scr[...] = _mx(do)
        el_scr[...] = jnp.exp(b_last)

        def intra(qc_h, kc_h, v_h, do_h):
            a = _mx(jnp.where(lower, _dot_nt(qc_h, kc_h), 0.0))
            da = _mx(jnp.where(lower, _dot_nt(do_h, v_h), 0.0))
            return _dot(da, kc_h), _dot_tn(da, qc_h), _dot_tn(a, do_h)

        dqc, dkc, dv_intra = _heads(intra, _mx(qc), _mx(kc), _mx(v), _mx(do))
        dv_scr[...] = dv_intra

        def chunk(i, carry):
            c = n_chunks - 1 - i
            r0 = pl.multiple_of(c * HG_CHUNK, HG_CHUNK)
            rs = pl.ds(r0, HG_CHUNK)
            for h in range(HEADS):
                cs = slice(h * HEAD_DIM, (h + 1) * HEAD_DIM)
                st = st_ref[c, h]
                ds_t = ds_scr[h]
                ds_m = _mx(ds_t)
                el = el_scr[pl.ds(r0, 1), cs]
                dkh_scr[rs, cs] = _dot(v_scr[rs, cs], ds_m)
                dv_scr[rs, cs] += _dot_nt(kh_scr[rs, cs], ds_m)
                dbl = jnp.sum(ds_t * st.astype(F32), axis=0, keepdims=True) * el
                dbl_scr[rs, cs] = jnp.broadcast_to(dbl, (HG_CHUNK, HEAD_DIM))
                dqt_scr[rs, cs] = _dot(do_scr[rs, cs], st)
                ds_scr[h] = ds_t * el + _dot_tn(do_scr[rs, cs], qt_scr[rs, cs])
            return carry

        lax.fori_loop(0, n_chunks, chunk, 0)

        dqt, dkh = dqt_scr[...], dkh_scr[...]
        dq = dqt * e_b + dqc * e_qc
        dk = dkh * e_kh + dkc * e_kc
        t_kh = dkh * kh
        db_rows = dqt * qt + dqc * qc - dkc * kc - t_kh
        dg = _exact_dot(upper, db_rows) + _exact_dot(same, t_kh) + dbl_scr[...]
        df = dg / f - dk
        dhf = df * (1.0 - lbv) * sf * (1.0 - sf)
        dlb_ref[...] += (df * (1.0 - sf)).reshape(HG_BLOCK // _SUBLANES, _SUBLANES, D_MODEL).sum(axis=0)
        dhq = dq * _silu_grad(hq, sq)
        dp_ref[...] = jnp.concatenate([dhq, dhf, dv_scr[...], dhg], axis=1).astype(dp_ref.dtype)

    rev = lambda b, j: b * nb + (nb - 1 - j)
    blk = lambda cb: pl.BlockSpec((HG_BLOCK, D_MODEL), functools.partial(lambda b, j, cb: (rev(b, j), cb), cb=cb))
    row = pl.BlockSpec((HG_BLOCK, D_MODEL), lambda b, j: (rev(b, j), 0))
    const = pl.BlockSpec((1, D_MODEL), lambda b, j: (0, 0))
    acc = pl.BlockSpec((_SUBLANES, D_MODEL), lambda b, j: (0, 0))
    big = lambda dt: pltpu.VMEM((HG_BLOCK, D_MODEL), dt)
    dp, dlb, dgh = pl.pallas_call(
        body, name="hgrn_bwd",
        out_shape=[jax.ShapeDtypeStruct((rows, 4 * D_MODEL), _MXU_DTYPE),
                   jax.ShapeDtypeStruct((_SUBLANES, D_MODEL), F32),
                   jax.ShapeDtypeStruct((_SUBLANES, D_MODEL), F32)],
        grid=(batch, nb),
        in_specs=[blk(CB_HQ), blk(CB_HF), blk(CB_HI), blk(CB_HG), row, row,
                  pl.BlockSpec((n_chunks, HEADS, HEAD_DIM, HEAD_DIM), lambda b, j: (rev(b, j), 0, 0, 0)),
                  const, const],
        out_specs=[pl.BlockSpec((HG_BLOCK, 4 * D_MODEL), lambda b, j: (rev(b, j), 0)), acc, acc],
        scratch_shapes=[pltpu.VMEM((HEADS, HEAD_DIM, HEAD_DIM), F32),
                        big(_MXU_DTYPE), big(_MXU_DTYPE), big(_MXU_DTYPE), big(_MXU_DTYPE),
                        big(F32), big(F32), big(F32), big(F32), big(F32)],
        compiler_params=_params(),
    )(p, p, p, p, o, dz, states, lb, gh)
    return dp, dlb.sum(axis=0), dgh.sum(axis=0)


def _attn_scores(qn, qp, kn_ref, kp_ref, k0, row0, tq):
    kb = pl.ds(k0, _LANES)
    s = (_dot_nt(qn, kn_ref[kb, :]) + _dot_nt(qp, kp_ref[kb, :])) * ATTN_SCALE
    row = row0 + lax.broadcasted_iota(jnp.int32, (tq, _LANES), 0)
    col = k0 + lax.broadcasted_iota(jnp.int32, (tq, _LANES), 1)
    return jnp.where(col <= row, s, NEG_BIG)


def _flash_fwd(qn, qp, kn, kp, v, *, batch, lp, tq):
    rows = batch * lp
    nq = lp // tq

    def body(qn_ref, qp_ref, kn_ref, kp_ref, v_ref, o_ref, lse_ref):
        qi = pl.program_id(2)
        qn_t, qp_t = qn_ref[...], qp_ref[...]
        n_kb = ((qi + 1) * tq + _LANES - 1) // _LANES

        def step(kj, carry):
            m, l, acc = carry
            k0 = pl.multiple_of(kj * _LANES, _LANES)
            s = _attn_scores(qn_t, qp_t, kn_ref, kp_ref, k0, qi * tq, tq)
            m_new = jnp.maximum(m, jnp.max(s, axis=1, keepdims=True))
            alpha = jnp.exp(m - m_new)
            pexp = jnp.exp(s - m_new)
            l = alpha * l + jnp.sum(pexp, axis=1, keepdims=True)
            acc = alpha * acc + _dot(_mx(pexp), v_ref[pl.ds(k0, _LANES), :])
            return m_new, l, acc

        m, l, acc = lax.fori_loop(
            0, n_kb, step,
            (jnp.full((tq, 1), NEG_BIG, F32), jnp.zeros((tq, 1), F32), jnp.zeros((tq, HEAD_DIM), F32)))
        o_ref[...] = (acc / l).astype(o_ref.dtype)
        lse_ref[...] = jnp.broadcast_to(m + jnp.log(l), (tq, HEAD_DIM))

    q_blk = pl.BlockSpec((tq, HEAD_DIM), lambda b, h, qi: (b * nq + qi, h))
    kv_blk = pl.BlockSpec((lp, HEAD_DIM), lambda b, h, qi: (b, h))
    return pl.pallas_call(
        body, name="attn_fwd",
        out_shape=[jax.ShapeDtypeStruct((rows, D_MODEL), _MXU_DTYPE),
                   jax.ShapeDtypeStruct((rows, D_MODEL), F32)],
        grid=(batch, HEADS, nq),
        in_specs=[q_blk, q_blk, kv_blk, pl.BlockSpec((lp, HEAD_DIM), lambda b, h, qi: (b, 0)), kv_blk],
        out_specs=[q_blk, q_blk],
        compiler_params=_params(),
    )(qn, qp, kn, kp, v)


def _flash_bwd(qn, qp, kn, kp, v, o, do, lse, *, batch, lp, tq):
    rows = batch * lp
    nq = lp // tq

    def body(qn_ref, qp_ref, kn_ref, kp_ref, v_ref, o_ref, do_ref, lse_ref,
             dqn_ref, dqp_ref, dkn_ref, dkp_ref, dv_ref, dkn_scr, dv_scr):
        h, qi = pl.program_id(1), pl.program_id(2)

        @pl.when(qi == 0)
        def _():
            dkn_scr[...] = jnp.zeros_like(dkn_scr)
            dv_scr[...] = jnp.zeros_like(dv_scr)

        @pl.when((qi == 0) & (h == 0))
        def _():
            dkp_ref[...] = jnp.zeros_like(dkp_ref)

        qn_t, qp_t, do_t = qn_ref[...], qp_ref[...], do_ref[...]
        delta = jnp.sum(do_t.astype(F32) * o_ref[...].astype(F32), axis=1, keepdims=True)
        lse_t = lse_ref[:, 0:1]
        n_kb = ((qi + 1) * tq + _LANES - 1) // _LANES

        def step(kj, carry):
            dqn, dqp = carry
            k0 = pl.multiple_of(kj * _LANES, _LANES)
            kb = pl.ds(k0, _LANES)
            s = _attn_scores(qn_t, qp_t, kn_ref, kp_ref, k0, qi * tq, tq)
            pexp = jnp.exp(s - lse_t)
            dp = _dot_nt(do_t, v_ref[kb, :])
            ds = _mx(pexp * (dp - delta) * ATTN_SCALE)
            dkn_scr[kb, :] += _dot_tn(ds, qn_t)
            dkp_ref[kb, :] += _dot_tn(ds, qp_t)
            dv_scr[kb, :] += _dot_tn(_mx(pexp), do_t)
            return dqn + _dot(ds, kn_ref[kb, :]), dqp + _dot(ds, kp_ref[kb, :])

        dqn, dqp = lax.fori_loop(0, n_kb, step,
                                 (jnp.zeros((tq, HEAD_DIM), F32), jnp.zeros((tq, HEAD_DIM), F32)))
        dqn_ref[...] = dqn.astype(dqn_ref.dtype)
        dqp_ref[...] = dqp

        @pl.when(qi == nq - 1)
        def _():
            dkn_ref[...] = dkn_scr[...].astype(dkn_ref.dtype)
            dv_ref[...] = dv_scr[...].astype(dv_ref.dtype)

    q_blk = pl.BlockSpec((tq, HEAD_DIM), lambda b, h, qi: (b * nq + qi, h))
    kv_blk = pl.BlockSpec((lp, HEAD_DIM), lambda b, h, qi: (b, h))
    kp_blk = pl.BlockSpec((lp, HEAD_DIM), lambda b, h, qi: (b, 0))
    return pl.pallas_call(
        body, name="attn_bwd",
        out_shape=[jax.ShapeDtypeStruct((rows, D_MODEL), _MXU_DTYPE),
                   jax.ShapeDtypeStruct((rows, D_MODEL), F32),
                   jax.ShapeDtypeStruct((rows, D_MODEL), _MXU_DTYPE),
                   jax.ShapeDtypeStruct((rows, HEAD_DIM), F32),
                   jax.ShapeDtypeStruct((rows, D_MODEL), _MXU_DTYPE)],
        grid=(batch, HEADS, nq),
        in_specs=[q_blk, q_blk, kv_blk, kp_blk, kv_blk, q_blk, q_blk, q_blk],
        out_specs=[q_blk, q_blk, kv_blk, kp_blk, kv_blk],
        scratch_shapes=[pltpu.VMEM((lp, HEAD_DIM), F32), pltpu.VMEM((lp, HEAD_DIM), F32)],
        compiler_params=_params(),
    )(qn, qp, kn, kp, v, o, do, lse)


def _all_gather(name, block):
    rows, lanes = block.shape

    def body(x_ref, out_ref, send_sems, recv_sems, local_sem):
        x, y, c = lax.axis_index("x"), lax.axis_index("y"), lax.axis_index("c")
        me, sibling = (x, y, c), (x, y, 1 - c)
        chips = [(1 - x, y), (x, 1 - y), (1 - x, 1 - y)]

        def slot(px, py, pc):
            return out_ref.at[4 * px + 2 * py + pc]

        def copy(k, blk, to, src=None):
            return pltpu.make_async_remote_copy(
                src_ref=slot(*blk) if src is None else src, dst_ref=slot(*blk),
                send_sem=send_sems.at[k], recv_sem=recv_sems.at[k],
                device_id=to, device_id_type=pl.DeviceIdType.MESH)

        mine = pltpu.make_async_copy(x_ref, slot(*me), local_sem)
        mine.start()
        first = [copy(0, me, sibling, src=x_ref)]
        first += [copy(1 + j, me, (*chip, c), src=x_ref) for j, chip in enumerate(chips)]
        for cp in first:
            cp.start()
        passed = [copy(4 + j, (*chip, c), sibling) for j, chip in enumerate(chips)]
        for j, chip in enumerate(chips):
            copy(1 + j, (*chip, c), me).wait_recv()
            passed[j].start()
        copy(0, sibling, me).wait_recv()
        for j, chip in enumerate(chips):
            copy(4 + j, (*chip, 1 - c), me).wait_recv()
        for cp in first + passed:
            cp.wait_send()
        mine.wait()

    return pl.pallas_call(
        body, name=name,
        out_shape=jax.ShapeDtypeStruct((N_DEV, rows, lanes), block.dtype),
        in_specs=[pl.BlockSpec(memory_space=pl.ANY)],
        out_specs=pl.BlockSpec(memory_space=pl.ANY),
        scratch_shapes=[pltpu.SemaphoreType.DMA((7,)), pltpu.SemaphoreType.DMA((7,)), pltpu.SemaphoreType.DMA],
    )(block)


def _exchange(name, parts):
    _, rows, lanes = parts.shape

    def body(p_ref, out_ref, send_sems, recv_sems, local_sem):
        x, y, c = lax.axis_index("x"), lax.axis_index("y"), lax.axis_index("c")
        me = 4 * x + 2 * y + c
        mine = pltpu.make_async_copy(p_ref.at[me], out_ref.at[me], local_sem)
        mine.start()
        copies = []
        for k in range(1, N_DEV):
            px, py, pc = x ^ (k >> 2), y ^ ((k >> 1) & 1), c ^ (k & 1)
            cp = pltpu.make_async_remote_copy(
                src_ref=p_ref.at[4 * px + 2 * py + pc], dst_ref=out_ref.at[me],
                send_sem=send_sems.at[k - 1], recv_sem=recv_sems.at[k - 1],
                device_id=(px, py, pc), device_id_type=pl.DeviceIdType.MESH)
            cp.start()
            copies.append(cp)
        for cp in copies:
            cp.wait()
        mine.wait()

    return pl.pallas_call(
        body, name=name,
        out_shape=jax.ShapeDtypeStruct(parts.shape, parts.dtype),
        in_specs=[pl.BlockSpec(memory_space=pl.ANY)],
        out_specs=pl.BlockSpec(memory_space=pl.ANY),
        scratch_shapes=[pltpu.SemaphoreType.DMA((7,)), pltpu.SemaphoreType.DMA((7,)), pltpu.SemaphoreType.DMA],
    )(parts)


def _row_tile(rows, align, cap):
    best = rows
    for t in range(align, min(rows, cap) + 1, align):
        if rows % t == 0:
            best = t
    return best


def _sum_parts(name, parts):
    _, rows, lanes = parts.shape
    tr = _row_tile(rows, 16, 2048)

    def body(p_ref, o_ref):
        acc = p_ref[0].astype(F32)
        for d in range(1, N_DEV):
            acc = acc + p_ref[d].astype(F32)
        o_ref[...] = acc

    return pl.pallas_call(
        body, name=name,
        out_shape=jax.ShapeDtypeStruct((rows, lanes), F32),
        grid=(rows // tr,),
        in_specs=[pl.BlockSpec((N_DEV, tr, lanes), lambda i: (0, i, 0))],
        out_specs=pl.BlockSpec((tr, lanes), lambda i: (i, 0)),
        compiler_params=_params(),
    )(parts)


def _adamw(name, w, g, m, v):
    rows, cols = w.shape
    tr = rows // 4 if rows % 32 == 0 and rows * cols > (1 << 16) else rows

    def body(w_ref, g_ref, m_ref, v_ref, d_ref, nm_ref, nv_ref):
        gv = g_ref[...]
        nm = ADAM_B1 * m_ref[...] + (1.0 - ADAM_B1) * gv
        nv = ADAM_B2 * v_ref[...] + (1.0 - ADAM_B2) * (gv * gv)
        m_hat = nm / (1.0 - ADAM_B1 ** ADAM_STEP)
        v_hat = nv / (1.0 - ADAM_B2 ** ADAM_STEP)
        d_ref[...] = -ADAM_LR * (m_hat / (jnp.sqrt(v_hat) + ADAM_EPS) + ADAM_WD * w_ref[...])
        nm_ref[...] = nm
        nv_ref[...] = nv

    spec = pl.BlockSpec((tr, cols), lambda i: (i, 0))
    return pl.pallas_call(
        body, name=name,
        out_shape=[jax.ShapeDtypeStruct((rows, cols), F32)] * 3,
        grid=(rows // tr,),
        in_specs=[spec] * 4, out_specs=[spec] * 3,
        compiler_params=_params(),
    )(w, g, m, v)


def _swap_halves(t):
    half = t.shape[-1] // 2
    return jnp.concatenate([t[..., half:], t[..., :half]], axis=-1)


def _pad_last(t, width):
    return jnp.concatenate([t, jnp.zeros(t.shape[:-1] + (width - t.shape[-1],), t.dtype)], axis=-1)


def _rope_tables(lp):
    pos = jnp.arange(lp, dtype=F32)
    inv_freq = 1.0 / (ROPE_THETA ** (jnp.arange(0, ROPE_DIM, 2, dtype=F32) / ROPE_DIM))
    ang = pos[:, None] * inv_freq[None, :]
    cos, sin = jnp.cos(ang), jnp.sin(ang)
    c128 = _pad_last(jnp.concatenate([cos, cos], axis=1), HEAD_DIM)
    s128 = _pad_last(jnp.concatenate([-sin, sin], axis=1), HEAD_DIM)
    return c128, s128


def _forward_backward(x, target, meta, w, small, *, lp):
    batch, seq, d = x.shape
    rows = batch * lp
    tr = 272 if lp % 272 == 0 else 128
    tm = lp // 2
    tq = 272 if lp % 272 == 0 else 128
    bf = _MXU_DTYPE
    rw = functools.partial(_rowwise, rows=rows, tr=tr, lp=lp)

    c128, s128 = _rope_tables(lp)
    cq_tab, sq_tab = jnp.tile(c128, (1, HEADS)), jnp.tile(s128, (1, HEADS))
    t_idx = jnp.arange(lp)
    real = jnp.broadcast_to(((t_idx >= N_META) & (t_idx < N_META + seq)).astype(F32)[:, None], (lp, _LANES))

    lb_logits = small["lb_logits"]
    lb = jax.nn.softmax(lb_logits, axis=0)[0:1]
    gh = jnp.tile(small["hg_norm_g"], (1, HEADS))

    h0 = _assemble("assemble_x", x, meta, lp)
    tgt = _assemble("assemble_target", target, jnp.zeros_like(meta), lp)

    (u1,), _ = rw("norm_mix_pre", lambda h, g: ([h * _rms_scale(h) * g], []),
                  ins=[("row", h0, d, 0), ("const", small["mix_pre_g"])], outs=[(d, bf)])
    p = _matmul("proj_in", u1, w["w_in"], out_dtype=F32, tm=tm, tn=1024, tk=1024)

    o_hg, z_a, states = _hgrn_fwd(p, lb, gh, batch=batch, lp=lp)
    y_a = _matmul("proj_hg_o", z_a, w["w_hg_o"], out_dtype=F32, tm=tm, tn=1024, tk=1024)

    def mla_pre(pc, gq, gkv, ct, st):
        cq, ckv = pc[:, 0:Q_LORA], pc[:, Q_LORA:Q_LORA + KV_LORA]
        kpe, kpe_sw = pc[:, 512:640], pc[:, 640:768]
        return [cq * _rms_scale(cq) * gq, ckv * _rms_scale(ckv) * gkv, kpe * ct + kpe_sw * st], []

    (cqn, ckvn, kp), _ = rw("mla_pre", mla_pre,
                            ins=[("row", p, 1024, CB_C), ("const", small["q_a_norm_g"]),
                                 ("const", small["kv_a_norm_g"]), ("pos", c128), ("pos", s128)],
                            outs=[(Q_LORA, bf), (KV_LORA, bf), (HEAD_DIM, bf)])
    qf = _matmul("proj_q_b", cqn, w["w_q"], out_dtype=F32, tm=tm, tn=1024, tk=Q_LORA)
    (qn, qp), _ = rw("rope_q", lambda a, pe, pes, ct, st: ([a, pe * ct + pes * st], []),
                     ins=[("row", qf, 1024, 0), ("row", qf, 1024, 1), ("row", qf, 1024, 2),
                          ("pos", cq_tab), ("pos", sq_tab)], outs=[(d, bf), (d, bf)])
    kv = _matmul("proj_kv_b", ckvn, w["w_kv"], out_dtype=bf, tm=tm, tn=1024, tk=KV_LORA)
    kn, vv = kv[:, :d], kv[:, d:]
    o_at, lse = _flash_fwd(qn, qp, kn, kp, vv, batch=batch, lp=lp, tq=tq)
    y_b = _matmul("proj_mla_o", o_at, w["w_mla_o"], out_dtype=F32, tm=tm, tn=1024, tk=1024)

    def merge(pa, pb, ya, yb, bg):
        ga, gb = _sigmoid(pa + bg[:, :d]), _sigmoid(pb + bg[:, d:])
        return [ga * ya + gb * yb], []

    (mix,), _ = rw("merge", merge,
                   ins=[("row", p, 1024, CB_GA), ("row", p, 1024, CB_GB), ("row", y_a, d, 0), ("row", y_b, d, 0),
                        ("const", small["b_gate"])], outs=[(d, bf)])
    mixed = _matmul("proj_out", mix, w["w_out"], out_dtype=F32, tm=tm, tn=1024, tk=1024)

    def post_mix(mx_, h, g2, g3):
        h1_ = h + mx_ * _rms_scale(mx_) * g2
        return [h1_, h1_ * _rms_scale(h1_) * g3], []

    (h1, u2), _ = rw("post_mix", post_mix,
                     ins=[("row", mixed, d, 0), ("row", h0, d, 0), ("const", small["mix_post_g"]),
                          ("const", small["ffn_pre_g"])], outs=[(d, F32), (d, bf)])
    gu = _matmul("ffn_in", u2, w["w_ffn_in"], out_dtype=F32, tm=tm, tn=1408, tk=1024)
    (act,), _ = rw("swiglu", lambda gt, up: ([gt * _sigmoid(gt) * up], []),
                   ins=[("row", gu, FFN_HIDDEN, 0), ("row", gu, FFN_HIDDEN, 1)], outs=[(FFN_HIDDEN, bf)])
    fo = _matmul("ffn_out", act, w["w_ffn_out"], out_dtype=F32, tm=tm, tn=1024, tk=1408)

    def post_ffn(fo_, h1_, t_, mask, g4):
        r = _rms_scale(fo_)
        h2 = h1_ + fo_ * r * g4
        err = (h2 - t_) * mask[:, 0:1]
        dh2 = err * (1.0 / d)
        dfo, dg4 = _rms_bwd(fo_, g4, dh2)
        return [dh2, dfo], [err * err, dg4]

    (dh2, dfo), (loss_vec, dg_ffn_post) = rw(
        "post_ffn_loss", post_ffn,
        ins=[("row", fo, d, 0), ("row", h1, d, 0), ("row", tgt, d, 0), ("pos", real), ("const", small["ffn_post_g"])],
        outs=[(d, F32), (d, bf)], accs=[d, d])
    loss = (0.5 / d) * jnp.sum(loss_vec)

    grads = {}
    d_act = _matmul("d_ffn_out", dfo, w["w_ffn_out_t"], out_dtype=F32, tm=tm, tn=1408, tk=1024)
    grads["w_ffn_out"] = _matmul_tn("dw_ffn_out", act, dfo, tk=1408, tn=1024, tr=tm)

    def swiglu_bwd(gt, up, da):
        s = _sigmoid(gt)
        return [jnp.concatenate([da * up * _silu_grad(gt, s), da * gt * s], axis=1)], []

    (dgu,), _ = rw("swiglu_bwd", swiglu_bwd,
                   ins=[("row", gu, FFN_HIDDEN, 0), ("row", gu, FFN_HIDDEN, 1), ("row", d_act, FFN_HIDDEN, 0)],
                   outs=[(2 * FFN_HIDDEN, bf)])
    du2 = _matmul("d_ffn_in", dgu, w["w_ffn_in_t"], out_dtype=F32, tm=tm, tn=1024, tk=1408)
    grads["w_ffn_in"] = _matmul_tn("dw_ffn_in", u2, dgu, tk=1024, tn=1408, tr=tm)

    def post_mix_bwd(du2_, h1_, dh2_, mx_, g3, g2):
        dx, dg3 = _rms_bwd(h1_, g3, du2_)
        dh1_ = dh2_ + dx
        dmx, dg2 = _rms_bwd(mx_, g2, dh1_)
        return [dh1_, dmx], [dg3, dg2]

    (dh1, dmixed), (dg_ffn_pre, dg_mix_post) = rw(
        "post_mix_bwd", post_mix_bwd,
        ins=[("row", du2, d, 0), ("row", h1, d, 0), ("row", dh2, d, 0), ("row", mixed, d, 0),
             ("const", small["ffn_pre_g"]), ("const", small["mix_post_g"])],
        outs=[(d, F32), (d, bf)], accs=[d, d])
    dmix = _matmul("d_proj_out", dmixed, w["w_out_t"], out_dtype=F32, tm=tm, tn=1024, tk=1024)
    grads["w_out"] = _matmul_tn("dw_out", mix, dmixed, tk=1024, tn=1024, tr=tm)

    def merge_bwd(dm, pa, pb, ya, yb, bg):
        ga, gb = _sigmoid(pa + bg[:, :d]), _sigmoid(pb + bg[:, d:])
        dpg = jnp.concatenate([dm * ya * ga * (1.0 - ga), dm * yb * gb * (1.0 - gb)], axis=1)
        return [dpg, dm * ga, dm * gb], [dpg]

    (dpg, dya, dyb), (db_gate,) = rw(
        "merge_bwd", merge_bwd,
        ins=[("row", dmix, d, 0), ("row", p, 1024, CB_GA), ("row", p, 1024, CB_GB), ("row", y_a, d, 0),
             ("row", y_b, d, 0), ("const", small["b_gate"])],
        outs=[(2 * d, bf), (d, bf), (d, bf)], accs=[2 * d])
    dz_a = _matmul("d_proj_hg_o", dya, w["w_hg_o_t"], out_dtype=F32, tm=tm, tn=1024, tk=1024)
    grads["w_hg_o"] = _matmul_tn("dw_hg_o", z_a, dya, tk=1024, tn=1024, tr=tm)
    do_at = _matmul("d_proj_mla_o", dyb, w["w_mla_o_t"], out_dtype=bf, tm=tm, tn=1024, tk=1024)
    grads["w_mla_o"] = _matmul_tn("dw_mla_o", o_at, dyb, tk=1024, tn=1024, tr=tm)

    dph, dlb, dgh = _hgrn_bwd(p, o_hg, dz_a, states, lb, gh, batch=batch, lp=lp)

    dqn, dqp, dkn, dkp, dvv = _flash_bwd(qn, qp, kn, kp, vv, o_at, do_at, lse, batch=batch, lp=lp, tq=tq)
    (dqf,), _ = rw("rope_q_bwd", lambda a, dpe, ct, st: ([jnp.concatenate([a.astype(F32), dpe * ct, dpe * st], axis=1)], []),
                   ins=[("row", dqn, d, 0), ("row", dqp, d, 0), ("pos", cq_tab), ("pos", sq_tab)],
                   outs=[(3 * d, bf)])
    dcqn = _matmul("d_proj_q_b", dqf, w["w_q_t"], out_dtype=F32, tm=tm, tn=Q_LORA, tk=1024)
    grads["w_q"] = _matmul_tn("dw_q_b", cqn, dqf, tk=Q_LORA, tn=1024, tr=tm)
    dckvn = _matmul("d_proj_k_b", dkn, w["w_k_t"], out_dtype=F32, tm=tm, tn=KV_LORA, tk=1024)
    dckvn = _matmul("d_proj_v_b", dvv, w["w_v_t"], out_dtype=F32, tm=tm, tn=KV_LORA, tk=1024, c_in=dckvn)
    grads["w_k"] = _matmul_tn("dw_k_b", ckvn, dkn, tk=KV_LORA, tn=1024, tr=tm)
    grads["w_v"] = _matmul_tn("dw_v_b", ckvn, dvv, tk=KV_LORA, tn=1024, tr=tm)

    def mla_pre_bwd(pc, dq_, dkv_, dkp_, gq, gkv, ct, st):
        cq, ckv = pc[:, 0:Q_LORA], pc[:, Q_LORA:Q_LORA + KV_LORA]
        dcq, dgq = _rms_bwd(cq, gq, dq_)
        dckv, dgkv = _rms_bwd(ckv, gkv, dkv_)
        dpc = jnp.concatenate([dcq, dckv, dkp_ * ct, dkp_ * st, jnp.zeros((pc.shape[0], 256), F32)], axis=1)
        return [dpc], [dgq, dgkv]

    (dpc,), (dg_q, dg_kv) = rw(
        "mla_pre_bwd", mla_pre_bwd,
        ins=[("row", p, 1024, CB_C), ("row", dcqn, Q_LORA, 0), ("row", dckvn, KV_LORA, 0), ("row", dkp, HEAD_DIM, 0),
             ("const", small["q_a_norm_g"]), ("const", small["kv_a_norm_g"]), ("pos", c128), ("pos", s128)],
        outs=[(1024, bf)], accs=[Q_LORA, KV_LORA])

    du1 = _matmul("d_proj_in_h", dph, w["w_in_t"][:4096], out_dtype=F32, tm=tm, tn=1024, tk=1024)
    du1 = _matmul("d_proj_in_c", dpc, w["w_in_t"][4096:5120], out_dtype=F32, tm=tm, tn=1024, tk=1024, c_in=du1)
    du1 = _matmul("d_proj_in_g", dpg, w["w_in_t"][5120:], out_dtype=F32, tm=tm, tn=1024, tk=1024, c_in=du1)
    grads["w_in"] = jnp.concatenate([
        _matmul_tn("dw_in_h", u1, dph, tk=1024, tn=1024, tr=tm),
        _matmul_tn("dw_in_c", u1, dpc, tk=1024, tn=1024, tr=tm),
        _matmul_tn("dw_in_g", u1, dpg, tk=1024, tn=1024, tr=tm)], axis=1)

    def pre_bwd(du, h, dh, g1):
        dx, dg1 = _rms_bwd(h, g1, du)
        return [dh + dx], [dg1]

    (dh0,), (dg_mix_pre,) = rw("norm_mix_pre_bwd", pre_bwd,
                               ins=[("row", du1, d, 0), ("row", h0, d, 0), ("row", dh1, d, 0), ("const", small["mix_pre_g"])],
                               outs=[(d, F32)], accs=[d])
    grad_x = dh0.reshape(batch, lp, d)[:, N_META:N_META + seq]
    grads["meta_tokens"] = _meta_grad(dh0, batch, lp)

    dl0 = dlb * lb[0] * (1.0 - lb[0])
    grads["lb_logits"] = jnp.stack([dl0, -dl0])
    grads["b_gate"] = db_gate[None]
    grads["hg_norm_g"] = dgh.reshape(HEADS, HEAD_DIM).sum(axis=0)[None]
    grads["q_a_norm_g"] = dg_q[None]
    grads["kv_a_norm_g"] = dg_kv[None]
    grads["mix_pre_g"] = dg_mix_pre[None]
    grads["mix_post_g"] = dg_mix_post[None]
    grads["ffn_pre_g"] = dg_ffn_pre[None]
    grads["ffn_post_g"] = dg_ffn_post[None]
    return loss, grad_x, grads


_BIG = ["w_in", "w_hg_o", "w_q_b", "w_kv_b", "w_mla_o", "w_out", "w_ffn_in", "w_ffn_out"]
_COLUMN_SHARDED = {"w_in", "w_q_b", "w_kv_b", "w_ffn_in"}
_SMALL = ["b_gate", "lb_logits", "hg_norm_g", "q_a_norm_g", "kv_a_norm_g", "mix_pre_g", "mix_post_g",
          "ffn_pre_g", "ffn_post_g"]


def _to_lanes(t):
    return t.reshape(-1, _LANES)


def _gathered_matrix(name, seg, shard_shape):
    k, n = shard_shape
    t = seg.reshape(N_DEV, k, n)
    if name in _COLUMN_SHARDED:
        return t.transpose(1, 0, 2).reshape(k, N_DEV * n)
    return t.reshape(N_DEV * k, n)


def _scatter_layout(name, full, shard_shape):
    k, n = shard_shape
    if name in _COLUMN_SHARDED:
        t = full.reshape(k, N_DEV, n).transpose(1, 0, 2)
    else:
        t = full.reshape(N_DEV, k, n)
    return t.reshape(N_DEV, -1, _LANES)


def _model_weights(full):
    z = lambda *s: jnp.zeros(s, full["w_in"].dtype)
    wi = full["w_in"]
    kpe = wi[:, 4608:4672]
    c_blk = jnp.concatenate([wi[:, 4096:4608], kpe, z(1024, 64), _swap_halves(kpe), z(1024, 64), z(1024, 256)], axis=1)
    w_in = jnp.concatenate([wi[:, :4096], c_blk, wi[:, 4672:]], axis=1)
    wq3 = full["w_q_b"].reshape(Q_LORA, HEADS, HEAD_DIM + ROPE_DIM)
    pe = wq3[:, :, HEAD_DIM:]
    w_q = jnp.concatenate([wq3[:, :, :HEAD_DIM].reshape(Q_LORA, -1),
                           _pad_last(pe, HEAD_DIM).reshape(Q_LORA, -1),
                           _pad_last(_swap_halves(pe), HEAD_DIM).reshape(Q_LORA, -1)], axis=1)
    wkv3 = full["w_kv_b"].reshape(KV_LORA, HEADS, 2 * HEAD_DIM)
    w_k = wkv3[:, :, :HEAD_DIM].reshape(KV_LORA, -1)
    w_v = wkv3[:, :, HEAD_DIM:].reshape(KV_LORA, -1)
    w = {"w_in": w_in, "w_q": w_q, "w_kv": jnp.concatenate([w_k, w_v], axis=1),
         "w_hg_o": full["w_hg_o"], "w_mla_o": full["w_mla_o"], "w_out": full["w_out"],
         "w_ffn_in": full["w_ffn_in"], "w_ffn_out": full["w_ffn_out"]}
    for n in ["w_in", "w_q", "w_hg_o", "w_mla_o", "w_out", "w_ffn_in", "w_ffn_out"]:
        w[n + "_t"] = w[n].T
    w["w_k_t"], w["w_v_t"] = w_k.T, w_v.T
    return {k: v.astype(_MXU_DTYPE) for k, v in w.items()}


def _reference_layout_grads(g):
    gi = g["w_in"]
    d_kpe = gi[:, 4608:4672] + _swap_halves(gi[:, 4736:4800])
    out = {"w_in": jnp.concatenate([gi[:, :4608], d_kpe, gi[:, 5120:]], axis=1)}
    gq = g["w_q"]
    d_pe = (gq[:, 1024:2048].reshape(Q_LORA, HEADS, HEAD_DIM)[:, :, :ROPE_DIM]
            + _swap_halves(gq[:, 2048:].reshape(Q_LORA, HEADS, HEAD_DIM)[:, :, :ROPE_DIM]))
    out["w_q_b"] = jnp.concatenate([gq[:, :1024].reshape(Q_LORA, HEADS, HEAD_DIM), d_pe], axis=2).reshape(Q_LORA, -1)
    out["w_kv_b"] = jnp.concatenate([g["w_k"].reshape(KV_LORA, HEADS, HEAD_DIM),
                                     g["w_v"].reshape(KV_LORA, HEADS, HEAD_DIM)], axis=2).reshape(KV_LORA, -1)
    for n in ["w_hg_o", "w_mla_o", "w_out", "w_ffn_in", "w_ffn_out"]:
        out[n] = g[n]
    return out


def _wire(t):
    if jnp.dtype(_WIRE_DTYPE).itemsize == 4:
        return t.astype(_WIRE_DTYPE)
    return lax.bitcast_convert_type(t, _WIRE_DTYPE)


def _unwire(t):
    if jnp.dtype(_WIRE_DTYPE).itemsize == 4:
        return t.astype(F32)
    return lax.bitcast_convert_type(t.reshape(t.shape[:-1] + (t.shape[-1] // 2, 2)), F32)


def kernel(x, meta_tokens, w_in, b_gate, lb_logits, hg_norm_g, w_hg_o, q_a_norm_g, w_q_b, kv_a_norm_g, w_kv_b, w_mla_o, w_out, mix_pre_g, mix_post_g, ffn_pre_g, ffn_post_g, w_ffn_in, w_ffn_out, loss_target, m_meta_tokens, m_w_in, m_b_gate, m_lb_logits, m_hg_norm_g, m_w_hg_o, m_q_a_norm_g, m_w_q_b, m_kv_a_norm_g, m_w_kv_b, m_w_mla_o, m_w_out, m_mix_pre_g, m_mix_post_g, m_ffn_pre_g, m_ffn_post_g, m_w_ffn_in, m_w_ffn_out, v_meta_tokens, v_w_in, v_b_gate, v_lb_logits, v_hg_norm_g, v_w_hg_o, v_q_a_norm_g, v_w_q_b, v_kv_a_norm_g, v_w_kv_b, v_w_mla_o, v_w_out, v_mix_pre_g, v_mix_post_g, v_ffn_pre_g, v_ffn_post_g, v_w_ffn_in, v_w_ffn_out):
    args = dict(locals())
    batch, seq, d = x.shape
    lp = -(-(N_META + seq) // _LANES) * _LANES
    weight_names = ["meta_tokens", "w_in", "b_gate", "lb_logits", "hg_norm_g", "w_hg_o", "q_a_norm_g", "w_q_b",
                    "kv_a_norm_g", "w_kv_b", "w_mla_o", "w_out", "mix_pre_g", "mix_post_g", "ffn_pre_g",
                    "ffn_post_g", "w_ffn_in", "w_ffn_out"]
    shard = {n: args[n].reshape(args[n].shape[-2:]) for n in _BIG}
    shapes = {n: shard[n].shape for n in _BIG}
    seg_rows = {n: shapes[n][0] * shapes[n][1] // _LANES for n in _BIG}

    packed = jnp.concatenate([_to_lanes(shard[n]).astype(_WIRE_DTYPE) for n in _BIG]
                             + [_to_lanes(_wire(meta_tokens))], axis=0)
    gathered = _all_gather("gather_weights", packed)
    full, off = {}, 0
    for n in _BIG:
        full[n] = _gathered_matrix(n, gathered[:, off:off + seg_rows[n]], shapes[n])
        off += seg_rows[n]
    meta_full = _unwire(gathered[:, off:]).reshape(N_DEV, N_META, d // N_DEV).transpose(1, 0, 2).reshape(N_META, d)
    small = {n: args[n] for n in _SMALL}

    loss, grad_x, g = _forward_backward(x, loss_target, meta_full, _model_weights(full), small, lp=lp)
    loss = lax.psum(loss, ("x", "y", "c"))

    gm = _reference_layout_grads(g)
    parts = jnp.concatenate([_scatter_layout(n, gm[n], shapes[n]) for n in _BIG], axis=1).astype(_WIRE_DTYPE)
    received = _exchange("scatter_grads", parts)
    g_rows = _sum_parts("sum_grads", received)
    grad, off = {}, 0
    for n in _BIG:
        grad[n] = g_rows[off:off + seg_rows[n]].reshape(shapes[n])
        off += seg_rows[n]

    small_parts = jnp.concatenate([_to_lanes(g[n]) for n in _SMALL] + [_to_lanes(g["meta_tokens"])], axis=0)
    n_small_rows = small_parts.shape[0]
    pad = -n_small_rows % _SUBLANES
    small_parts = jnp.concatenate([small_parts, jnp.zeros((pad, _LANES), F32)], axis=0)
    small_sum = _sum_parts("sum_small", _all_gather("gather_small", small_parts))
    off = 0
    for n in _SMALL:
        r = args[n].size // _LANES
        grad[n] = small_sum[off:off + r].reshape(args[n].shape)
        off += r
    me = 4 * lax.axis_index("x") + 2 * lax.axis_index("y") + lax.axis_index("c")
    meta_sum = small_sum[off:off + N_META * d // _LANES].reshape(N_META, d)
    grad["meta_tokens"] = lax.dynamic_slice_in_dim(meta_sum, me * (d // N_DEV), d // N_DEV, axis=1)

    delta, new_m, new_v = {}, {}, {}
    for n in _BIG + ["meta_tokens"]:
        shp = args[n].shape
        two_d = lambda t: t.reshape(shp[-2:])
        res = _adamw("adamw_" + n, two_d(args[n]), two_d(grad[n]), two_d(args["m_" + n]), two_d(args["v_" + n]))
        delta[n], new_m[n], new_v[n] = [t.reshape(shp) for t in res]
        grad[n] = grad[n].reshape(shp)
    cat = lambda prefix: jnp.concatenate(
        [_to_lanes(args[prefix + n]) for n in _SMALL] + [jnp.zeros((pad + N_META * d // _LANES, _LANES), F32)], axis=0)
    res = _adamw("adamw_small", cat(""), small_sum, cat("m_"), cat("v_"))
    off = 0
    for n in _SMALL:
        r = args[n].size // _LANES
        delta[n], new_m[n], new_v[n] = [t[off:off + r].reshape(args[n].shape) for t in res]
        off += r

    return (loss, grad_x, *[grad[n] for n in weight_names], *[delta[n] for n in weight_names],
            *[new_m[n] for n in weight_names], *[new_v[n] for n in weight_names])
```

```python
import functools

import jax
import jax.numpy as jnp
from jax import lax
from jax.experimental import pallas as pl
from jax.experimental.pallas import tpu as pltpu

F32 = jnp.float32
_MXU_DTYPE = jnp.bfloat16
_WIRE_DTYPE = jnp.bfloat16
_VMEM_LIMIT_BYTES = 56 * 1024 * 1024
_LANES = 128
_SUBLANES = 8

N_DEV = 8
N_META = 16
NORM_EPS = 1e-6
HEADS = 8
HEAD_DIM = 128
ROPE_DIM = 64
HG_CHUNK = 16
HG_BLOCK = 128
ROPE_THETA = 10000.0
D_MODEL = 1024
Q_LORA = 256
KV_LORA = 256
FFN_HIDDEN = 2816
ATTN_SCALE = (HEAD_DIM + ROPE_DIM) ** -0.5
NEG_BIG = -1e30

ADAM_LR = 0.001
ADAM_B1 = 0.9
ADAM_B2 = 0.999
ADAM_EPS = 1e-08
ADAM_WD = 0.01
ADAM_STEP = 10

CB_HQ, CB_HF, CB_HI, CB_HG, CB_C, CB_GA, CB_GB = range(7)
IN_COLS_PADDED = 7 * 1024


def _params(**kw):
    return pltpu.CompilerParams(vmem_limit_bytes=_VMEM_LIMIT_BYTES, **kw)


def _dot(a, b):
    return lax.dot_general(a, b, (((1,), (0,)), ((), ())), preferred_element_type=F32)


def _dot_nt(a, b):
    return lax.dot_general(a, b, (((1,), (1,)), ((), ())), preferred_element_type=F32)


def _dot_tn(a, b):
    return lax.dot_general(a, b, (((0,), (0,)), ((), ())), preferred_element_type=F32)


def _mx(x):
    return x.astype(_MXU_DTYPE)


def _exact_dot(m01, x):
    if _MXU_DTYPE == jnp.float32:
        return _dot(m01.astype(F32), x)
    m = m01.astype(jnp.bfloat16)
    x1 = x.astype(jnp.bfloat16)
    r1 = x - x1.astype(F32)
    x2 = r1.astype(jnp.bfloat16)
    x3 = (r1 - x2.astype(F32)).astype(jnp.bfloat16)
    return _dot(m, x1) + _dot(m, x2) + _dot(m, x3)


def _sigmoid(x):
    return jax.nn.sigmoid(x)


def _silu_grad(x, s):
    return s * (1.0 + x * (1.0 - s))


def _rms_scale(x):
    return lax.rsqrt(jnp.mean(x * x, axis=-1, keepdims=True) + NORM_EPS)


def _rms_bwd(x, g, dy):
    r = _rms_scale(x)
    xh = x * r
    w = dy * g
    dx = r * (w - xh * jnp.mean(xh * w, axis=-1, keepdims=True))
    return dx, dy * xh


def _heads(fn, *arrays):
    outs = [fn(*[a[:, h * HEAD_DIM:(h + 1) * HEAD_DIM] for a in arrays]) for h in range(HEADS)]
    if isinstance(outs[0], tuple):
        return tuple(jnp.concatenate([o[i] for o in outs], axis=1) for i in range(len(outs[0])))
    return jnp.concatenate(outs, axis=1)


def _matmul(name, a, b, *, out_dtype, tm, tn, tk, c_in=None):
    m, k = a.shape
    _, n = b.shape
    assert m % tm == 0 and n % tn == 0 and k % tk == 0, (name, a.shape, b.shape, tm, tn, tk)
    nk = k // tk
    has_c = c_in is not None

    def body(*refs):
        a_ref, b_ref = refs[0], refs[1]
        c_ref = refs[2] if has_c else None
        o_ref = refs[3] if has_c else refs[2]

        def finish(r):
            if has_c:
                r = r + c_ref[...]
            o_ref[...] = r.astype(o_ref.dtype)

        if nk == 1:
            finish(_dot(a_ref[...], b_ref[...]))
        else:
            acc = refs[-1]
            kk = pl.program_id(2)

            @pl.when(kk == 0)
            def _():
                acc[...] = jnp.zeros_like(acc)

            acc[...] += _dot(a_ref[...], b_ref[...])

            @pl.when(kk == nk - 1)
            def _():
                finish(acc[...])

    in_specs = [pl.BlockSpec((tm, tk), lambda j, i, kk: (i, kk)),
                pl.BlockSpec((tk, tn), lambda j, i, kk: (kk, j))]
    args = [a, b]
    aliases = {}
    if has_c:
        in_specs.append(pl.BlockSpec((tm, tn), lambda j, i, kk: (i, j)))
        args.append(c_in)
        aliases = {2: 0}
    return pl.pallas_call(
        body, name=name,
        out_shape=jax.ShapeDtypeStruct((m, n), out_dtype),
        grid=(n // tn, m // tm, nk),
        in_specs=in_specs,
        out_specs=pl.BlockSpec((tm, tn), lambda j, i, kk: (i, j)),
        scratch_shapes=[pltpu.VMEM((tm, tn), F32)] if nk > 1 else [],
        input_output_aliases=aliases,
        compiler_params=_params(),
    )(*args)


def _matmul_tn(name, x, dy, *, tk, tn, tr):
    r, k = x.shape
    _, n = dy.shape
    assert r % tr == 0 and k % tk == 0 and n % tn == 0, (name, x.shape, dy.shape)

    def body(x_ref, dy_ref, o_ref):
        @pl.when(pl.program_id(2) == 0)
        def _():
            o_ref[...] = jnp.zeros_like(o_ref)

        o_ref[...] += _dot_tn(x_ref[...], dy_ref[...])

    return pl.pallas_call(
        body, name=name,
        out_shape=jax.ShapeDtypeStruct((k, n), F32),
        grid=(k // tk, n // tn, r // tr),
        in_specs=[pl.BlockSpec((tr, tk), lambda kb, nb, rr: (rr, kb)),
                  pl.BlockSpec((tr, tn), lambda kb, nb, rr: (rr, nb))],
        out_specs=pl.BlockSpec((tk, tn), lambda kb, nb, rr: (kb, nb)),
        compiler_params=_params(),
    )(x, dy)


def _rowwise(name, body, *, rows, tr, lp, ins, outs, accs=()):
    assert rows % tr == 0 and lp % tr == 0 and tr % 16 == 0
    tiles_per_example = lp // tr
    in_specs, arrays = [], []
    for spec in ins:
        if spec[0] == "row":
            _, arr, width, cb = spec
            in_specs.append(pl.BlockSpec((tr, width), functools.partial(lambda i, cb: (i, cb), cb=cb)))
        elif spec[0] == "const":
            arr = spec[1]
            in_specs.append(pl.BlockSpec(arr.shape, lambda i: (0, 0)))
        else:
            arr = spec[1]
            in_specs.append(pl.BlockSpec((tr, arr.shape[1]), lambda i: (i % tiles_per_example, 0)))
        arrays.append(arr)
    n_in, n_out = len(ins), len(outs)

    def kern(*refs):
        res_outs, res_accs = body(*[r[...] for r in refs[:n_in]])
        for ref, val in zip(refs[n_in:n_in + n_out], res_outs, strict=True):
            ref[...] = val.astype(ref.dtype)
        acc_refs = refs[n_in + n_out:]
        if acc_refs:
            @pl.when(pl.program_id(0) == 0)
            def _():
                for ref in acc_refs:
                    ref[...] = jnp.zeros_like(ref)

            for ref, val in zip(acc_refs, res_accs, strict=True):
                ref[...] += val.reshape(tr // _SUBLANES, _SUBLANES, val.shape[-1]).sum(axis=0)

    out_shape = ([jax.ShapeDtypeStruct((rows, w), dt) for w, dt in outs]
                 + [jax.ShapeDtypeStruct((_SUBLANES, w), F32) for w in accs])
    out_specs = ([pl.BlockSpec((tr, w), lambda i: (i, 0)) for w, _ in outs]
                 + [pl.BlockSpec((_SUBLANES, w), lambda i: (0, 0)) for w in accs])
    res = pl.pallas_call(
        kern, name=name, out_shape=out_shape, grid=(rows // tr,),
        in_specs=in_specs, out_specs=out_specs, compiler_params=_params(),
    )(*arrays)
    return res[:n_out], [a.sum(axis=0) for a in res[n_out:]]


def _assemble(name, x, head_rows, lp):
    batch, seq, d = x.shape
    tc = 256

    def body(x_ref, m_ref, o_ref):
        o_ref[0:N_META, :] = m_ref[...]
        o_ref[N_META:N_META + seq, :] = x_ref[0]
        if lp > N_META + seq:
            o_ref[N_META + seq:, :] = jnp.zeros((lp - N_META - seq, tc), F32)

    return pl.pallas_call(
        body, name=name,
        out_shape=jax.ShapeDtypeStruct((batch * lp, d), F32),
        grid=(batch, d // tc),
        in_specs=[pl.BlockSpec((1, seq, tc), lambda b, j: (b, 0, j)),
                  pl.BlockSpec((N_META, tc), lambda b, j: (0, j))],
        out_specs=pl.BlockSpec((lp, tc), lambda b, j: (b, j)),
        compiler_params=_params(),
    )(x, head_rows)


def _meta_grad(dh0, batch, lp):
    d = dh0.shape[1]

    def body(g_ref, o_ref):
        @pl.when(pl.program_id(0) == 0)
        def _():
            o_ref[...] = jnp.zeros_like(o_ref)

        o_ref[...] += g_ref[...]

    return pl.pallas_call(
        body, name="meta_grad",
        out_shape=jax.ShapeDtypeStruct((N_META, d), F32),
        grid=(batch,),
        in_specs=[pl.BlockSpec((N_META, d), lambda b: (b * (lp // N_META), 0))],
        out_specs=pl.BlockSpec((N_META, d), lambda b: (0, 0)),
        compiler_params=_params(),
    )(dh0)


def _segment_masks():
    t = lax.broadcasted_iota(jnp.int32, (HG_BLOCK, HG_BLOCK), 0)
    s = lax.broadcasted_iota(jnp.int32, (HG_BLOCK, HG_BLOCK), 1)
    same = lax.shift_right_logical(t, 4) == lax.shift_right_logical(s, 4)
    lower = same & (s <= t)
    upper = same & (s >= t)
    first_half = same & ((s & 15) <= 7)
    return same, lower, upper, first_half


def _hgrn_gates(hq, hf, lb):
    sq = _sigmoid(hq)
    q = hq * sq
    sf = _sigmoid(hf)
    f = lb + (1.0 - lb) * sf
    return q, sq, sf, f


def _hgrn_decays(g, same, lower, first_half):
    b = _exact_dot(lower, g)
    b_last = _exact_dot(same, g)
    b_ref = _exact_dot(first_half, g)
    return b, b_last, b_ref


def _hgrn_fwd(p, lb, gh, *, batch, lp):
    rows = batch * lp
    nb = lp // HG_BLOCK
    n_chunks = HG_BLOCK // HG_CHUNK

    def body(hq_ref, hf_ref, hi_ref, hg_ref, lb_ref, gh_ref, o_ref, z_ref, st_ref,
             s_scr, qt_scr, kh_scr, v_scr, el_scr, o_scr):
        @pl.when(pl.program_id(1) == 0)
        def _():
            s_scr[...] = jnp.zeros_like(s_scr)

        same, lower, _, first_half = _segment_masks()
        v = hi_ref[...]
        q, _, _, f = _hgrn_gates(hq_ref[...], hf_ref[...], lb_ref[...])
        k = 1.0 - f
        b, b_last, b_ref = _hgrn_decays(jnp.log(f), same, lower, first_half)
        qt_scr[...] = _mx(q * jnp.exp(b))
        kh_scr[...] = _mx(k * jnp.exp(b_last - b))
        v_scr[...] = _mx(v)
        el_scr[...] = jnp.exp(b_last)
        qc = _mx(q * jnp.exp(b - b_ref))
        kc = _mx(k * jnp.exp(b_ref - b))

        def intra(qc_h, kc_h, v_h):
            a = jnp.where(lower, _dot_nt(qc_h, kc_h), 0.0)
            return _dot(_mx(a), v_h)

        o_scr[...] = _heads(intra, qc, kc, _mx(v))

        def chunk(c, carry):
            r0 = pl.multiple_of(c * HG_CHUNK, HG_CHUNK)
            rs = pl.ds(r0, HG_CHUNK)
            for h in range(HEADS):
                cs = slice(h * HEAD_DIM, (h + 1) * HEAD_DIM)
                st = s_scr[h]
                st_ref[c, h] = st.astype(st_ref.dtype)
                o_scr[rs, cs] += _dot_nt(qt_scr[rs, cs], _mx(st))
                s_scr[h] = st * el_scr[pl.ds(r0, 1), cs] + _dot_tn(v_scr[rs, cs], kh_scr[rs, cs])
            return carry

        lax.fori_loop(0, n_chunks, chunk, 0)

        o = o_scr[...]
        o_ref[...] = o
        hg = hg_ref[...]
        n = _heads(lambda o_h: o_h * _rms_scale(o_h), o) * gh_ref[...]
        z_ref[...] = (n * hg * _sigmoid(hg)).astype(z_ref.dtype)

    blk = lambda cb: pl.BlockSpec((HG_BLOCK, D_MODEL), functools.partial(lambda b, j, cb: (b * nb + j, cb), cb=cb))
    row_out = pl.BlockSpec((HG_BLOCK, D_MODEL), lambda b, j: (b * nb + j, 0))
    const = pl.BlockSpec((1, D_MODEL), lambda b, j: (0, 0))
    return pl.pallas_call(
        body, name="hgrn_fwd",
        out_shape=[jax.ShapeDtypeStruct((rows, D_MODEL), F32),
                   jax.ShapeDtypeStruct((rows, D_MODEL), _MXU_DTYPE),
                   jax.ShapeDtypeStruct((rows // HG_CHUNK, HEADS, HEAD_DIM, HEAD_DIM), _MXU_DTYPE)],
        grid=(batch, nb),
        in_specs=[blk(CB_HQ), blk(CB_HF), blk(CB_HI), blk(CB_HG), const, const],
        out_specs=[row_out, row_out,
                   pl.BlockSpec((n_chunks, HEADS, HEAD_DIM, HEAD_DIM), lambda b, j: (b * nb + j, 0, 0, 0))],
        scratch_shapes=[pltpu.VMEM((HEADS, HEAD_DIM, HEAD_DIM), F32),
                        pltpu.VMEM((HG_BLOCK, D_MODEL), _MXU_DTYPE),
                        pltpu.VMEM((HG_BLOCK, D_MODEL), _MXU_DTYPE),
                        pltpu.VMEM((HG_BLOCK, D_MODEL), _MXU_DTYPE),
                        pltpu.VMEM((HG_BLOCK, D_MODEL), F32),
                        pltpu.VMEM((HG_BLOCK, D_MODEL), F32)],
        compiler_params=_params(),
    )(p, p, p, p, lb, gh)


def _hgrn_bwd(p, o, dz, states, lb, gh, *, batch, lp):
    rows = batch * lp
    nb = lp // HG_BLOCK
    n_chunks = HG_BLOCK // HG_CHUNK

    def body(hq_ref, hf_ref, hi_ref, hg_ref, o_ref, dz_ref, st_ref, lb_ref, gh_ref,
             dp_ref, dlb_ref, dgh_ref,
             ds_scr, qt_scr, kh_scr, v_scr, do_scr, el_scr, dqt_scr, dkh_scr, dv_scr, dbl_scr):
        first = (pl.program_id(0) == 0) & (pl.program_id(1) == 0)

        @pl.when(first)
        def _():
            dlb_ref[...] = jnp.zeros_like(dlb_ref)
            dgh_ref[...] = jnp.zeros_like(dgh_ref)

        @pl.when(pl.program_id(1) == 0)
        def _():
            ds_scr[...] = jnp.zeros_like(ds_scr)

        same, lower, upper, first_half = _segment_masks()
        lbv = lb_ref[...]
        hq, hf, v, hg = hq_ref[...], hf_ref[...], hi_ref[...], hg_ref[...]
        q, sq, sf, f = _hgrn_gates(hq, hf, lbv)
        k = 1.0 - f
        b, b_last, b_ref = _hgrn_decays(jnp.log(f), same, lower, first_half)
        e_b = jnp.exp(b)
        e_kh = jnp.exp(b_last - b)
        e_qc = jnp.exp(b - b_ref)
        e_kc = jnp.exp(b_ref - b)
        qt, kh, qc, kc = q * e_b, k * e_kh, q * e_qc, k * e_kc

        o = o_ref[...]
        dz = dz_ref[...].astype(F32)
        ghv = gh_ref[...]
        sg = _sigmoid(hg)
        r = _heads(lambda o_h: jnp.broadcast_to(_rms_scale(o_h), o_h.shape), o)
        oh = o * r
        dn = dz * hg * sg
        dhg = dz * oh * ghv * _silu_grad(hg, sg)
        w = dn * ghv
        do = r * (w - oh * _heads(lambda t: jnp.broadcast_to(jnp.mean(t, axis=-1, keepdims=True), t.shape), oh * w))
        dgh_ref[...] += (dn * oh).reshape(HG_BLOCK // _SUBLANES, _SUBLANES, D_MODEL).sum(axis=0)

        qt_scr[...] = _mx(qt)
        kh_scr[...] = _mx(kh)
        v_scr[...] = _mx(v)
        do_scr[...] = _mx(do)
        el_scr[...] = jnp.exp(b_last)

        def intra(qc_h, kc_h, v_h, do_h):
            a = _mx(jnp.where(lower, _dot_nt(qc_h, kc_h), 0.0))
            da = _mx(jnp.where(lower, _dot_nt(do_h, v_h), 0.0))
            return _dot(da, kc_h), _dot_tn(da, qc_h), _dot_tn(a, do_h)

        dqc, dkc, dv_intra = _heads(intra, _mx(qc), _mx(kc), _mx(v), _mx(do))
        dv_scr[...] = dv_intra

        def chunk(i, carry):
            c = n_chunks - 1 - i
            r0 = pl.multiple_of(c * HG_CHUNK, HG_CHUNK)
            rs = pl.ds(r0, HG_CHUNK)
            for h in range(HEADS):
                cs = slice(h * HEAD_DIM, (h + 1) * HEAD_DIM)
                st = st_ref[c, h]
                ds_t = ds_scr[h]
                ds_m = _mx(ds_t)
                el = el_scr[pl.ds(r0, 1), cs]
                dkh_scr[rs, cs] = _dot(v_scr[rs, cs], ds_m)
                dv_scr[rs, cs] += _dot_nt(kh_scr[rs, cs], ds_m)
                dbl = jnp.sum(ds_t * st.astype(F32), axis=0, keepdims=True) * el
                dbl_scr[rs, cs] = jnp.broadcast_to(dbl, (HG_CHUNK, HEAD_DIM))
                dqt_scr[rs, cs] = _dot(do_scr[rs, cs], st)
                ds_scr[h] = ds_t * el + _dot_tn(do_scr[rs, cs], qt_scr[rs, cs])
            return carry

        lax.fori_loop(0, n_chunks, chunk, 0)

        dqt, dkh = dqt_scr[...], dkh_scr[...]
        dq = dqt * e_b + dqc * e_qc
        dk = dkh * e_kh + dkc * e_kc
        t_kh = dkh * kh
        db_rows = dqt * qt + dqc * qc - dkc * kc - t_kh
        dg = _exact_dot(upper, db_rows) + _exact_dot(same, t_kh) + dbl_scr[...]
        df = dg / f - dk
        dhf = df * (1.0 - lbv) * sf * (1.0 - sf)
        dlb_ref[...] += (df * (1.0 - sf)).reshape(HG_BLOCK // _SUBLANES, _SUBLANES, D_MODEL).sum(axis=0)
        dhq = dq * _silu_grad(hq, sq)
        dp_ref[...] = jnp.concatenate([dhq, dhf, dv_scr[...], dhg], axis=1).astype(dp_ref.dtype)

    rev = lambda b, j: b * nb + (nb - 1 - j)
    blk = lambda cb: pl.BlockSpec((HG_BLOCK, D_MODEL), functools.partial(lambda b, j, cb: (rev(b, j), cb), cb=cb))
    row = pl.BlockSpec((HG_BLOCK, D_MODEL), lambda b, j: (rev(b, j), 0))
    const = pl.BlockSpec((1, D_MODEL), lambda b, j: (0, 0))
    acc = pl.BlockSpec((_SUBLANES, D_MODEL), lambda b, j: (0, 0))
    big = lambda dt: pltpu.VMEM((HG_BLOCK, D_MODEL), dt)
    dp, dlb, dgh = pl.pallas_call(
        body, name="hgrn_bwd",
        out_shape=[jax.ShapeDtypeStruct((rows, 4 * D_MODEL), _MXU_DTYPE),
                   jax.ShapeDtypeStruct((_SUBLANES, D_MODEL), F32),
                   jax.ShapeDtypeStruct((_SUBLANES, D_MODEL), F32)],
        grid=(batch, nb),
        in_specs=[blk(CB_HQ), blk(CB_HF), blk(CB_HI), blk(CB_HG), row, row,
                  pl.BlockSpec((n_chunks, HEADS, HEAD_DIM, HEAD_DIM), lambda b, j: (rev(b, j), 0, 0, 0)),
                  const, const],
        out_specs=[pl.BlockSpec((HG_BLOCK, 4 * D_MODEL), lambda b, j: (rev(b, j), 0)), acc, acc],
        scratch_shapes=[pltpu.VMEM((HEADS, HEAD_DIM, HEAD_DIM), F32),
                        big(_MXU_DTYPE), big(_MXU_DTYPE), big(_MXU_DTYPE), big(_MXU_DTYPE),
                        big(F32), big(F32), big(F32), big(F32), big(F32)],
        compiler_params=_params(),
    )(p, p, p, p, o, dz, states, lb, gh)
    return dp, dlb.sum(axis=0), dgh.sum(axis=0)


QK_DIM = 2 * HEAD_DIM
ATTN_TQ = 256
ATTN_KEY_CHUNK = 512


def _query_tiles(lp):
    return [(r0, min(ATTN_TQ, lp - r0)) for r0 in range(0, lp, ATTN_TQ)]


def _attn_fwd(q_cat, k_t, v, *, batch, lp):
    rows = batch * lp

    def body(q_ref, kt_ref, v_ref, o_ref, lse_ref):
        for r0, tq in _query_tiles(lp):
            q_t = q_ref[r0:r0 + tq, :]
            i = lax.broadcasted_iota(jnp.int32, (tq, tq), 0)
            j = lax.broadcasted_iota(jnp.int32, (tq, tq), 1)
            s_diag = jnp.where(j <= i, _dot(q_t, kt_ref[:, r0:r0 + tq]) * ATTN_SCALE, NEG_BIG)
            m = jnp.max(s_diag, axis=1, keepdims=True)
            if r0:
                s_past = _dot(q_t, kt_ref[:, 0:r0]) * ATTN_SCALE
                m = jnp.maximum(m, jnp.max(s_past, axis=1, keepdims=True))
            p_diag = jnp.exp(s_diag - m)
            l = jnp.sum(p_diag, axis=1, keepdims=True)
            acc = _dot(_mx(p_diag), v_ref[r0:r0 + tq, :])
            if r0:
                p_past = jnp.exp(s_past - m)
                l = l + jnp.sum(p_past, axis=1, keepdims=True)
                acc = acc + _dot(_mx(p_past), v_ref[0:r0, :])
            o_ref[r0:r0 + tq, :] = (acc / l).astype(o_ref.dtype)
            lse_ref[r0:r0 + tq, :] = jnp.broadcast_to(m + jnp.log(l), (tq, HEAD_DIM))

    head_blk = pl.BlockSpec((lp, HEAD_DIM), lambda b, h: (b, h))
    return pl.pallas_call(
        body, name="attn_fwd",
        out_shape=[jax.ShapeDtypeStruct((rows, D_MODEL), _MXU_DTYPE),
                   jax.ShapeDtypeStruct((rows, D_MODEL), F32)],
        grid=(batch, HEADS),
        in_specs=[pl.BlockSpec((lp, QK_DIM), lambda b, h: (b, h)),
                  pl.BlockSpec((QK_DIM, lp), lambda b, h: (b * HEADS + h, 0)),
                  head_blk],
        out_specs=[head_blk, head_blk],
        compiler_params=_params(),
    )(q_cat, k_t, v)


def _attn_bwd(q_cat, q_t, k_cat, k_t, v, do, do_t, o_t, lse_row, *, batch, lp):
    rows = batch * lp

    def body(q_ref, qt_ref, k_ref, kt_ref, v_ref, do_ref, dot_ref, ot_ref, lse_ref,
             dqt_ref, dkn_ref, dkp_ref, dv_ref, dk_acc, dv_acc):
        dk_acc[...] = jnp.zeros_like(dk_acc)
        dv_acc[...] = jnp.zeros_like(dv_acc)
        delta = jnp.sum(dot_ref[...].astype(F32) * ot_ref[...].astype(F32), axis=0, keepdims=True)
        for r0, tq in _query_tiles(lp):
            cols = slice(r0, r0 + tq)
            qt_t, dot_t = qt_ref[:, cols], dot_ref[:, cols]
            q_t_, do_t_ = q_ref[cols, :], do_ref[cols, :]
            lse_t, delta_t = lse_ref[0:1, cols], delta[:, cols]
            chunks = [(c0, min(ATTN_KEY_CHUNK, r0 - c0), False) for c0 in range(0, r0, ATTN_KEY_CHUNK)] + [(r0, tq, True)]
            dq_t = jnp.zeros((QK_DIM, tq), F32)
            for c0, n, diagonal in chunks:
                keys = slice(c0, c0 + n)
                s = _dot(k_ref[keys, :], qt_t) * ATTN_SCALE
                if diagonal:
                    jk = lax.broadcasted_iota(jnp.int32, (n, tq), 0)
                    iq = lax.broadcasted_iota(jnp.int32, (n, tq), 1)
                    s = jnp.where(jk <= iq, s, NEG_BIG)
                pexp = jnp.exp(s - lse_t)
                dp = _dot(v_ref[keys, :], dot_t)
                ds = _mx(pexp * (dp - delta_t) * ATTN_SCALE)
                dk_acc[keys, :] += _dot(ds, q_t_)
                dv_acc[keys, :] += _dot(_mx(pexp), do_t_)
                dq_t = dq_t + _dot(kt_ref[:, keys], ds)
            dqt_ref[:, cols] = dq_t

        dkn_ref[...] = dk_acc[:, 0:HEAD_DIM].astype(dkn_ref.dtype)
        dv_ref[...] = dv_acc[...].astype(dv_ref.dtype)

        @pl.when(pl.program_id(1) == 0)
        def _():
            dkp_ref[...] = jnp.zeros_like(dkp_ref)

        dkp_ref[...] += dk_acc[:, HEAD_DIM:]

    head_blk = pl.BlockSpec((lp, HEAD_DIM), lambda b, h: (b, h))
    cat_blk = pl.BlockSpec((lp, QK_DIM), lambda b, h: (b, h))
    cat_t_blk = pl.BlockSpec((QK_DIM, lp), lambda b, h: (b * HEADS + h, 0))
    head_t_blk = pl.BlockSpec((HEAD_DIM, lp), lambda b, h: (b * HEADS + h, 0))
    return pl.pallas_call(
        body, name="attn_bwd",
        out_shape=[jax.ShapeDtypeStruct((batch * HEADS * QK_DIM, lp), F32),
                   jax.ShapeDtypeStruct((rows, D_MODEL), _MXU_DTYPE),
                   jax.ShapeDtypeStruct((rows, HEAD_DIM), F32),
                   jax.ShapeDtypeStruct((rows, D_MODEL), _MXU_DTYPE)],
        grid=(batch, HEADS),
        in_specs=[cat_blk, cat_t_blk, cat_blk, cat_t_blk, head_blk, head_blk, head_t_blk, head_t_blk,
                  pl.BlockSpec((_SUBLANES, lp), lambda b, h: (b * HEADS + h, 0))],
        out_specs=[cat_t_blk, head_blk, pl.BlockSpec((lp, HEAD_DIM), lambda b, h: (b, 0)), head_blk],
        scratch_shapes=[pltpu.VMEM((lp, QK_DIM), F32), pltpu.VMEM((lp, HEAD_DIM), F32)],
        compiler_params=_params(),
    )(q_cat, q_t, k_cat, k_t, v, do, do_t, o_t, lse_row)


def _all_gather(name, block):
    rows, lanes = block.shape

    def body(x_ref, out_ref, send_sems, recv_sems, local_sem):
        x, y, c = lax.axis_index("x"), lax.axis_index("y"), lax.axis_index("c")
        me, sibling = (x, y, c), (x, y, 1 - c)
        chips = [(1 - x, y), (x, 1 - y), (1 - x, 1 - y)]

        def slot(px, py, pc):
            return out_ref.at[4 * px + 2 * py + pc]

        def copy(k, blk, to, src=None):
            return pltpu.make_async_remote_copy(
                src_ref=slot(*blk) if src is None else src, dst_ref=slot(*blk),
                send_sem=send_sems.at[k], recv_sem=recv_sems.at[k],
                device_id=to, device_id_type=pl.DeviceIdType.MESH)

        mine = pltpu.make_async_copy(x_ref, slot(*me), local_sem)
        mine.start()
        first = [copy(0, me, sibling, src=x_ref)]
        first += [copy(1 + j, me, (*chip, c), src=x_ref) for j, chip in enumerate(chips)]
        for cp in first:
            cp.start()
        passed = [copy(4 + j, (*chip, c), sibling) for j, chip in enumerate(chips)]
        for j, chip in enumerate(chips):
            copy(1 + j, (*chip, c), me).wait_recv()
            passed[j].start()
        copy(0, sibling, me).wait_recv()
        for j, chip in enumerate(chips):
            copy(4 + j, (*chip, 1 - c), me).wait_recv()
        for cp in first + passed:
            cp.wait_send()
        mine.wait()

    return pl.pallas_call(
        body, name=name,
        out_shape=jax.ShapeDtypeStruct((N_DEV, rows, lanes), block.dtype),
        in_specs=[pl.BlockSpec(memory_space=pl.ANY)],
        out_specs=pl.BlockSpec(memory_space=pl.ANY),
        scratch_shapes=[pltpu.SemaphoreType.DMA((7,)), pltpu.SemaphoreType.DMA((7,)), pltpu.SemaphoreType.DMA],
    )(block)


def _exchange(name, parts):
    _, rows, lanes = parts.shape

    def body(p_ref, out_ref, send_sems, recv_sems, local_sem):
        x, y, c = lax.axis_index("x"), lax.axis_index("y"), lax.axis_index("c")
        me = 4 * x + 2 * y + c
        mine = pltpu.make_async_copy(p_ref.at[me], out_ref.at[me], local_sem)
        mine.start()
        copies = []
        for k in range(1, N_DEV):
            px, py, pc = x ^ (k >> 2), y ^ ((k >> 1) & 1), c ^ (k & 1)
            cp = pltpu.make_async_remote_copy(
                src_ref=p_ref.at[4 * px + 2 * py + pc], dst_ref=out_ref.at[me],
                send_sem=send_sems.at[k - 1], recv_sem=recv_sems.at[k - 1],
                device_id=(px, py, pc), device_id_type=pl.DeviceIdType.MESH)
            cp.start()
            copies.append(cp)
        for cp in copies:
            cp.wait()
        mine.wait()

    return pl.pallas_call(
        body, name=name,
        out_shape=jax.ShapeDtypeStruct(parts.shape, parts.dtype),
        in_specs=[pl.BlockSpec(memory_space=pl.ANY)],
        out_specs=pl.BlockSpec(memory_space=pl.ANY),
        scratch_shapes=[pltpu.SemaphoreType.DMA((7,)), pltpu.SemaphoreType.DMA((7,)), pltpu.SemaphoreType.DMA],
    )(parts)


def _row_tile(rows, align, cap):
    best = rows
    for t in range(align, min(rows, cap) + 1, align):
        if rows % t == 0:
            best = t
    return best


def _sum_parts(name, parts):
    _, rows, lanes = parts.shape
    tr = _row_tile(rows, 16, 2048)

    def body(p_ref, o_ref):
        acc = p_ref[0].astype(F32)
        for d in range(1, N_DEV):
            acc = acc + p_ref[d].astype(F32)
        o_ref[...] = acc

    return pl.pallas_call(
        body, name=name,
        out_shape=jax.ShapeDtypeStruct((rows, lanes), F32),
        grid=(rows // tr,),
        in_specs=[pl.BlockSpec((N_DEV, tr, lanes), lambda i: (0, i, 0))],
        out_specs=pl.BlockSpec((tr, lanes), lambda i: (i, 0)),
        compiler_params=_params(),
    )(parts)


def _adamw(name, w, g, m, v):
    rows, cols = w.shape
    tr = rows // 4 if rows % 32 == 0 and rows * cols > (1 << 16) else rows

    def body(w_ref, g_ref, m_ref, v_ref, d_ref, nm_ref, nv_ref):
        gv = g_ref[...]
        nm = ADAM_B1 * m_ref[...] + (1.0 - ADAM_B1) * gv
        nv = ADAM_B2 * v_ref[...] + (1.0 - ADAM_B2) * (gv * gv)
        m_hat = nm / (1.0 - ADAM_B1 ** ADAM_STEP)
        v_hat = nv / (1.0 - ADAM_B2 ** ADAM_STEP)
        d_ref[...] = -ADAM_LR * (m_hat / (jnp.sqrt(v_hat) + ADAM_EPS) + ADAM_WD * w_ref[...])
        nm_ref[...] = nm
        nv_ref[...] = nv

    spec = pl.BlockSpec((tr, cols), lambda i: (i, 0))
    return pl.pallas_call(
        body, name=name,
        out_shape=[jax.ShapeDtypeStruct((rows, cols), F32)] * 3,
        grid=(rows // tr,),
        in_specs=[spec] * 4, out_specs=[spec] * 3,
        compiler_params=_params(),
    )(w, g, m, v)


def _swap_halves(t):
    half = t.shape[-1] // 2
    return jnp.concatenate([t[..., half:], t[..., :half]], axis=-1)


def _pad_last(t, width):
    return jnp.concatenate([t, jnp.zeros(t.shape[:-1] + (width - t.shape[-1],), t.dtype)], axis=-1)


def _rope_tables(lp):
    pos = jnp.arange(lp, dtype=F32)
    inv_freq = 1.0 / (ROPE_THETA ** (jnp.arange(0, ROPE_DIM, 2, dtype=F32) / ROPE_DIM))
    ang = pos[:, None] * inv_freq[None, :]
    cos, sin = jnp.cos(ang), jnp.sin(ang)
    c128 = _pad_last(jnp.concatenate([cos, cos], axis=1), HEAD_DIM)
    s128 = _pad_last(jnp.concatenate([-sin, sin], axis=1), HEAD_DIM)
    return c128, s128


def _forward_backward(x, target, meta, w, small, *, lp):
    batch, seq, d = x.shape
    rows = batch * lp
    tr = 272 if lp % 272 == 0 else 128
    tm = lp // 2
    bf = _MXU_DTYPE
    rw = functools.partial(_rowwise, rows=rows, tr=tr, lp=lp)

    c128, s128 = _rope_tables(lp)
    cq_tab, sq_tab = jnp.tile(c128, (1, HEADS)), jnp.tile(s128, (1, HEADS))
    t_idx = jnp.arange(lp)
    real = jnp.broadcast_to(((t_idx >= N_META) & (t_idx < N_META + seq)).astype(F32)[:, None], (lp, _LANES))

    lb_logits = small["lb_logits"]
    lb = jax.nn.softmax(lb_logits, axis=0)[0:1]
    gh = jnp.tile(small["hg_norm_g"], (1, HEADS))

    h0 = _assemble("assemble_x", x, meta, lp)
    tgt = _assemble("assemble_target", target, jnp.zeros_like(meta), lp)

    (u1,), _ = rw("norm_mix_pre", lambda h, g: ([h * _rms_scale(h) * g], []),
                  ins=[("row", h0, d, 0), ("const", small["mix_pre_g"])], outs=[(d, bf)])
    p = _matmul("proj_in", u1, w["w_in"], out_dtype=F32, tm=tm, tn=1024, tk=1024)

    o_hg, z_a, states = _hgrn_fwd(p, lb, gh, batch=batch, lp=lp)
    y_a = _matmul("proj_hg_o", z_a, w["w_hg_o"], out_dtype=F32, tm=tm, tn=1024, tk=1024)

    def mla_pre(pc, gq, gkv, ct, st):
        cq, ckv = pc[:, 0:Q_LORA], pc[:, Q_LORA:Q_LORA + KV_LORA]
        kpe, kpe_sw = pc[:, 512:640], pc[:, 640:768]
        return [cq * _rms_scale(cq) * gq, ckv * _rms_scale(ckv) * gkv, kpe * ct + kpe_sw * st], []

    (cqn, ckvn, kp), _ = rw("mla_pre", mla_pre,
                            ins=[("row", p, 1024, CB_C), ("const", small["q_a_norm_g"]),
                                 ("const", small["kv_a_norm_g"]), ("pos", c128), ("pos", s128)],
                            outs=[(Q_LORA, bf), (KV_LORA, bf), (HEAD_DIM, bf)])
    qf = _matmul("proj_q_b", cqn, w["w_q"], out_dtype=F32, tm=tm, tn=1024, tk=Q_LORA)
    def rope_q(a, pe, pes, ct, st):
        roped = pe * ct + pes * st
        hs = lambda t, h: t[:, h * HEAD_DIM:(h + 1) * HEAD_DIM]
        return [jnp.concatenate([t for h in range(HEADS) for t in (hs(a, h), hs(roped, h))], axis=1)], []

    (q_cat,), _ = rw("rope_q", rope_q,
                     ins=[("row", qf, 1024, 0), ("row", qf, 1024, 1), ("row", qf, 1024, 2),
                          ("pos", cq_tab), ("pos", sq_tab)], outs=[(HEADS * QK_DIM, bf)])
    kv = _matmul("proj_kv_b", ckvn, w["w_kv"], out_dtype=bf, tm=tm, tn=1024, tk=KV_LORA)
    kn, vv = kv[:, :d], kv[:, d:]
    k_cat = jnp.concatenate([kn.reshape(rows, HEADS, HEAD_DIM),
                             jnp.broadcast_to(kp[:, None, :], (rows, HEADS, HEAD_DIM))], axis=2).reshape(rows, HEADS * QK_DIM)
    head_t = lambda t, wd: t.reshape(batch, lp, HEADS, wd).transpose(0, 2, 3, 1).reshape(batch * HEADS * wd, lp)
    k_t = head_t(k_cat, QK_DIM)
    o_at, lse = _attn_fwd(q_cat, k_t, vv, batch=batch, lp=lp)
    y_b = _matmul("proj_mla_o", o_at, w["w_mla_o"], out_dtype=F32, tm=tm, tn=1024, tk=1024)

    def merge(pa, pb, ya, yb, bg):
        ga, gb = _sigmoid(pa + bg[:, :d]), _sigmoid(pb + bg[:, d:])
        return [ga * ya + gb * yb], []

    (mix,), _ = rw("merge", merge,
                   ins=[("row", p, 1024, CB_GA), ("row", p, 1024, CB_GB), ("row", y_a, d, 0), ("row", y_b, d, 0),
                        ("const", small["b_gate"])], outs=[(d, bf)])
    mixed = _matmul("proj_out", mix, w["w_out"], out_dtype=F32, tm=tm, tn=1024, tk=1024)

    def post_mix(mx_, h, g2, g3):
        h1_ = h + mx_ * _rms_scale(mx_) * g2
        return [h1_, h1_ * _rms_scale(h1_) * g3], []

    (h1, u2), _ = rw("post_mix", post_mix,
                     ins=[("row", mixed, d, 0), ("row", h0, d, 0), ("const", small["mix_post_g"]),
                          ("const", small["ffn_pre_g"])], outs=[(d, F32), (d, bf)])
    gu = _matmul("ffn_in", u2, w["w_ffn_in"], out_dtype=F32, tm=tm, tn=1408, tk=1024)
    (act,), _ = rw("swiglu", lambda gt, up: ([gt * _sigmoid(gt) * up], []),
                   ins=[("row", gu, FFN_HIDDEN, 0), ("row", gu, FFN_HIDDEN, 1)], outs=[(FFN_HIDDEN, bf)])
    fo = _matmul("ffn_out", act, w["w_ffn_out"], out_dtype=F32, tm=tm, tn=1024, tk=1408)

    def post_ffn(fo_, h1_, t_, mask, g4):
        r = _rms_scale(fo_)
        h2 = h1_ + fo_ * r * g4
        err = (h2 - t_) * mask[:, 0:1]
        dh2 = err * (1.0 / d)
        dfo, dg4 = _rms_bwd(fo_, g4, dh2)
        return [dh2, dfo], [err * err, dg4]

    (dh2, dfo), (loss_vec, dg_ffn_post) = rw(
        "post_ffn_loss", post_ffn,
        ins=[("row", fo, d, 0), ("row", h1, d, 0), ("row", tgt, d, 0), ("pos", real), ("const", small["ffn_post_g"])],
        outs=[(d, F32), (d, bf)], accs=[d, d])
    loss = (0.5 / d) * jnp.sum(loss_vec)

    grads = {}
    d_act = _matmul("d_ffn_out", dfo, w["w_ffn_out_t"], out_dtype=F32, tm=tm, tn=1408, tk=1024)
    grads["w_ffn_out"] = _matmul_tn("dw_ffn_out", act, dfo, tk=1408, tn=1024, tr=tm)

    def swiglu_bwd(gt, up, da):
        s = _sigmoid(gt)
        return [jnp.concatenate([da * up * _silu_grad(gt, s), da * gt * s], axis=1)], []

    (dgu,), _ = rw("swiglu_bwd", swiglu_bwd,
                   ins=[("row", gu, FFN_HIDDEN, 0), ("row", gu, FFN_HIDDEN, 1), ("row", d_act, FFN_HIDDEN, 0)],
                   outs=[(2 * FFN_HIDDEN, bf)])
    du2 = _matmul("d_ffn_in", dgu, w["w_ffn_in_t"], out_dtype=F32, tm=tm, tn=1024, tk=1408)
    grads["w_ffn_in"] = _matmul_tn("dw_ffn_in", u2, dgu, tk=1024, tn=1408, tr=tm)

    def post_mix_bwd(du2_, h1_, dh2_, mx_, g3, g2):
        dx, dg3 = _rms_bwd(h1_, g3, du2_)
        dh1_ = dh2_ + dx
        dmx, dg2 = _rms_bwd(mx_, g2, dh1_)
        return [dh1_, dmx], [dg3, dg2]

    (dh1, dmixed), (dg_ffn_pre, dg_mix_post) = rw(
        "post_mix_bwd", post_mix_bwd,
        ins=[("row", du2, d, 0), ("row", h1, d, 0), ("row", dh2, d, 0), ("row", mixed, d, 0),
             ("const", small["ffn_pre_g"]), ("const", small["mix_post_g"])],
        outs=[(d, F32), (d, bf)], accs=[d, d])
    dmix = _matmul("d_proj_out", dmixed, w["w_out_t"], out_dtype=F32, tm=tm, tn=1024, tk=1024)
    grads["w_out"] = _matmul_tn("dw_out", mix, dmixed, tk=1024, tn=1024, tr=tm)

    def merge_bwd(dm, pa, pb, ya, yb, bg):
        ga, gb = _sigmoid(pa + bg[:, :d]), _sigmoid(pb + bg[:, d:])
        dpg = jnp.concatenate([dm * ya * ga * (1.0 - ga), dm * yb * gb * (1.0 - gb)], axis=1)
        return [dpg, dm * ga, dm * gb], [dpg]

    (dpg, dya, dyb), (db_gate,) = rw(
        "merge_bwd", merge_bwd,
        ins=[("row", dmix, d, 0), ("row", p, 1024, CB_GA), ("row", p, 1024, CB_GB), ("row", y_a, d, 0),
             ("row", y_b, d, 0), ("const", small["b_gate"])],
        outs=[(2 * d, bf), (d, bf), (d, bf)], accs=[2 * d])
    dz_a = _matmul("d_proj_hg_o", dya, w["w_hg_o_t"], out_dtype=F32, tm=tm, tn=1024, tk=1024)
    grads["w_hg_o"] = _matmul_tn("dw_hg_o", z_a, dya, tk=1024, tn=1024, tr=tm)
    do_at = _matmul("d_proj_mla_o", dyb, w["w_mla_o_t"], out_dtype=bf, tm=tm, tn=1024, tk=1024)
    grads["w_mla_o"] = _matmul_tn("dw_mla_o", o_at, dyb, tk=1024, tn=1024, tr=tm)

    dph, dlb, dgh = _hgrn_bwd(p, o_hg, dz_a, states, lb, gh, batch=batch, lp=lp)

    lse_row = jnp.broadcast_to(lse[:, ::HEAD_DIM].reshape(batch, lp, HEADS).transpose(0, 2, 1)[:, :, None, :],
                               (batch, HEADS, _SUBLANES, lp)).reshape(batch * HEADS * _SUBLANES, lp)
    dq_t, dkn, dkp, dvv = _attn_bwd(q_cat, head_t(q_cat, QK_DIM), k_cat, k_t, vv, do_at, head_t(do_at, HEAD_DIM),
                                    head_t(o_at, HEAD_DIM), lse_row, batch=batch, lp=lp)
    dq_cat = dq_t.reshape(batch, HEADS, QK_DIM, lp).transpose(0, 3, 1, 2).reshape(rows, HEADS * QK_DIM)

    def rope_q_bwd(dq, ct, st):
        hs = lambda half: jnp.concatenate(
            [dq[:, h * QK_DIM + half * HEAD_DIM:h * QK_DIM + (half + 1) * HEAD_DIM] for h in range(HEADS)], axis=1)
        dpe = hs(1)
        return [jnp.concatenate([hs(0), dpe * ct, dpe * st], axis=1)], []

    (dqf,), _ = rw("rope_q_bwd", rope_q_bwd,
                   ins=[("row", dq_cat, HEADS * QK_DIM, 0), ("pos", cq_tab), ("pos", sq_tab)],
                   outs=[(3 * d, bf)])
    dcqn = _matmul("d_proj_q_b", dqf, w["w_q_t"], out_dtype=F32, tm=tm, tn=Q_LORA, tk=1024)
    grads["w_q"] = _matmul_tn("dw_q_b", cqn, dqf, tk=Q_LORA, tn=1024, tr=tm)
    dckvn = _matmul("d_proj_k_b", dkn, w["w_k_t"], out_dtype=F32, tm=tm, tn=KV_LORA, tk=1024)
    dckvn = _matmul("d_proj_v_b", dvv, w["w_v_t"], out_dtype=F32, tm=tm, tn=KV_LORA, tk=1024, c_in=dckvn)
    grads["w_k"] = _matmul_tn("dw_k_b", ckvn, dkn, tk=KV_LORA, tn=1024, tr=tm)
    grads["w_v"] = _matmul_tn("dw_v_b", ckvn, dvv, tk=KV_LORA, tn=1024, tr=tm)

    def mla_pre_bwd(pc, dq_, dkv_, dkp_, gq, gkv, ct, st):
        cq, ckv = pc[:, 0:Q_LORA], pc[:, Q_LORA:Q_LORA + KV_LORA]
        dcq, dgq = _rms_bwd(cq, gq, dq_)
        dckv, dgkv = _rms_bwd(ckv, gkv, dkv_)
        dpc = jnp.concatenate([dcq, dckv, dkp_ * ct, dkp_ * st, jnp.zeros((pc.shape[0], 256), F32)], axis=1)
        return [dpc], [dgq, dgkv]

    (dpc,), (dg_q, dg_kv) = rw(
        "mla_pre_bwd", mla_pre_bwd,
        ins=[("row", p, 1024, CB_C), ("row", dcqn, Q_LORA, 0), ("row", dckvn, KV_LORA, 0), ("row", dkp, HEAD_DIM, 0),
             ("const", small["q_a_norm_g"]), ("const", small["kv_a_norm_g"]), ("pos", c128), ("pos", s128)],
        outs=[(1024, bf)], accs=[Q_LORA, KV_LORA])

    du1 = _matmul("d_proj_in_h", dph, w["w_in_t"][:4096], out_dtype=F32, tm=tm, tn=1024, tk=1024)
    du1 = _matmul("d_proj_in_c", dpc, w["w_in_t"][4096:5120], out_dtype=F32, tm=tm, tn=1024, tk=1024, c_in=du1)
    du1 = _matmul("d_proj_in_g", dpg, w["w_in_t"][5120:], out_dtype=F32, tm=tm, tn=1024, tk=1024, c_in=du1)
    grads["w_in"] = jnp.concatenate([
        _matmul_tn("dw_in_h", u1, dph, tk=1024, tn=1024, tr=tm),
        _matmul_tn("dw_in_c", u1, dpc, tk=1024, tn=1024, tr=tm),
        _matmul_tn("dw_in_g", u1, dpg, tk=1024, tn=1024, tr=tm)], axis=1)

    def pre_bwd(du, h, dh, g1):
        dx, dg1 = _rms_bwd(h, g1, du)
        return [dh + dx], [dg1]

    (dh0,), (dg_mix_pre,) = rw("norm_mix_pre_bwd", pre_bwd,
                               ins=[("row", du1, d, 0), ("row", h0, d, 0), ("row", dh1, d, 0), ("const", small["mix_pre_g"])],
                               outs=[(d, F32)], accs=[d])
    grad_x = dh0.reshape(batch, lp, d)[:, N_META:N_META + seq]
    grads["meta_tokens"] = _meta_grad(dh0, batch, lp)

    dl0 = dlb * lb[0] * (1.0 - lb[0])
    grads["lb_logits"] = jnp.stack([dl0, -dl0])
    grads["b_gate"] = db_gate[None]
    grads["hg_norm_g"] = dgh.reshape(HEADS, HEAD_DIM).sum(axis=0)[None]
    grads["q_a_norm_g"] = dg_q[None]
    grads["kv_a_norm_g"] = dg_kv[None]
    grads["mix_pre_g"] = dg_mix_pre[None]
    grads["mix_post_g"] = dg_mix_post[None]
    grads["ffn_pre_g"] = dg_ffn_pre[None]
    grads["ffn_post_g"] = dg_ffn_post[None]
    return loss, grad_x, grads


_BIG = ["w_in", "w_hg_o", "w_q_b", "w_kv_b", "w_mla_o", "w_out", "w_ffn_in", "w_ffn_out"]
_COLUMN_SHARDED = {"w_in", "w_q_b", "w_kv_b", "w_ffn_in"}
_SMALL = ["b_gate", "lb_logits", "hg_norm_g", "q_a_norm_g", "kv_a_norm_g", "mix_pre_g", "mix_post_g",
          "ffn_pre_g", "ffn_post_g"]


def _to_lanes(t):
    return t.reshape(-1, _LANES)


def _gathered_matrix(name, seg, shard_shape):
    k, n = shard_shape
    t = seg.reshape(N_DEV, k, n)
    if name in _COLUMN_SHARDED:
        return t.transpose(1, 0, 2).reshape(k, N_DEV * n)
    return t.reshape(N_DEV * k, n)


def _scatter_layout(name, full, shard_shape):
    k, n = shard_shape
    if name in _COLUMN_SHARDED:
        t = full.reshape(k, N_DEV, n).transpose(1, 0, 2)
    else:
        t = full.reshape(N_DEV, k, n)
    return t.reshape(N_DEV, -1, _LANES)


def _model_weights(full):
    z = lambda *s: jnp.zeros(s, full["w_in"].dtype)
    wi = full["w_in"]
    kpe = wi[:, 4608:4672]
    c_blk = jnp.concatenate([wi[:, 4096:4608], kpe, z(1024, 64), _swap_halves(kpe), z(1024, 64), z(1024, 256)], axis=1)
    w_in = jnp.concatenate([wi[:, :4096], c_blk, wi[:, 4672:]], axis=1)
    wq3 = full["w_q_b"].reshape(Q_LORA, HEADS, HEAD_DIM + ROPE_DIM)
    pe = wq3[:, :, HEAD_DIM:]
    w_q = jnp.concatenate([wq3[:, :, :HEAD_DIM].reshape(Q_LORA, -1),
                           _pad_last(pe, HEAD_DIM).reshape(Q_LORA, -1),
                           _pad_last(_swap_halves(pe), HEAD_DIM).reshape(Q_LORA, -1)], axis=1)
    wkv3 = full["w_kv_b"].reshape(KV_LORA, HEADS, 2 * HEAD_DIM)
    w_k = wkv3[:, :, :HEAD_DIM].reshape(KV_LORA, -1)
    w_v = wkv3[:, :, HEAD_DIM:].reshape(KV_LORA, -1)
    w = {"w_in": w_in, "w_q": w_q, "w_kv": jnp.concatenate([w_k, w_v], axis=1),
         "w_hg_o": full["w_hg_o"], "w_mla_o": full["w_mla_o"], "w_out": full["w_out"],
         "w_ffn_in": full["w_ffn_in"], "w_ffn_out": full["w_ffn_out"]}
    for n in ["w_in", "w_q", "w_hg_o", "w_mla_o", "w_out", "w_ffn_in", "w_ffn_out"]:
        w[n + "_t"] = w[n].T
    w["w_k_t"], w["w_v_t"] = w_k.T, w_v.T
    return {k: v.astype(_MXU_DTYPE) for k, v in w.items()}


def _reference_layout_grads(g):
    gi = g["w_in"]
    d_kpe = gi[:, 4608:4672] + _swap_halves(gi[:, 4736:4800])
    out = {"w_in": jnp.concatenate([gi[:, :4608], d_kpe, gi[:, 5120:]], axis=1)}
    gq = g["w_q"]
    d_pe = (gq[:, 1024:2048].reshape(Q_LORA, HEADS, HEAD_DIM)[:, :, :ROPE_DIM]
            + _swap_halves(gq[:, 2048:].reshape(Q_LORA, HEADS, HEAD_DIM)[:, :, :ROPE_DIM]))
    out["w_q_b"] = jnp.concatenate([gq[:, :1024].reshape(Q_LORA, HEADS, HEAD_DIM), d_pe], axis=2).reshape(Q_LORA, -1)
    out["w_kv_b"] = jnp.concatenate([g["w_k"].reshape(KV_LORA, HEADS, HEAD_DIM),
                                     g["w_v"].reshape(KV_LORA, HEADS, HEAD_DIM)], axis=2).reshape(KV_LORA, -1)
    for n in ["w_hg_o", "w_mla_o", "w_out", "w_ffn_in", "w_ffn_out"]:
        out[n] = g[n]
    return out


def _wire(t):
    if jnp.dtype(_WIRE_DTYPE).itemsize == 4:
        return t.astype(_WIRE_DTYPE)
    return lax.bitcast_convert_type(t, _WIRE_DTYPE)


def _unwire(t):
    if jnp.dtype(_WIRE_DTYPE).itemsize == 4:
        return t.astype(F32)
    return lax.bitcast_convert_type(t.reshape(t.shape[:-1] + (t.shape[-1] // 2, 2)), F32)


def kernel(x, meta_tokens, w_in, b_gate, lb_logits, hg_norm_g, w_hg_o, q_a_norm_g, w_q_b, kv_a_norm_g, w_kv_b, w_mla_o, w_out, mix_pre_g, mix_post_g, ffn_pre_g, ffn_post_g, w_ffn_in, w_ffn_out, loss_target, m_meta_tokens, m_w_in, m_b_gate, m_lb_logits, m_hg_norm_g, m_w_hg_o, m_q_a_norm_g, m_w_q_b, m_kv_a_norm_g, m_w_kv_b, m_w_mla_o, m_w_out, m_mix_pre_g, m_mix_post_g, m_ffn_pre_g, m_ffn_post_g, m_w_ffn_in, m_w_ffn_out, v_meta_tokens, v_w_in, v_b_gate, v_lb_logits, v_hg_norm_g, v_w_hg_o, v_q_a_norm_g, v_w_q_b, v_kv_a_norm_g, v_w_kv_b, v_w_mla_o, v_w_out, v_mix_pre_g, v_mix_post_g, v_ffn_pre_g, v_ffn_post_g, v_w_ffn_in, v_w_ffn_out):
    args = dict(locals())
    batch, seq, d = x.shape
    lp = -(-(N_META + seq) // _LANES) * _LANES
    weight_names = ["meta_tokens", "w_in", "b_gate", "lb_logits", "hg_norm_g", "w_hg_o", "q_a_norm_g", "w_q_b",
                    "kv_a_norm_g", "w_kv_b", "w_mla_o", "w_out", "mix_pre_g", "mix_post_g", "ffn_pre_g",
                    "ffn_post_g", "w_ffn_in", "w_ffn_out"]
    shard = {n: args[n].reshape(args[n].shape[-2:]) for n in _BIG}
    shapes = {n: shard[n].shape for n in _BIG}
    seg_rows = {n: shapes[n][0] * shapes[n][1] // _LANES for n in _BIG}

    packed = jnp.concatenate([_to_lanes(shard[n]).astype(_WIRE_DTYPE) for n in _BIG]
                             + [_to_lanes(_wire(meta_tokens))], axis=0)
    gathered = _all_gather("gather_weights", packed)
    full, off = {}, 0
    for n in _BIG:
        full[n] = _gathered_matrix(n, gathered[:, off:off + seg_rows[n]], shapes[n])
        off += seg_rows[n]
    meta_full = _unwire(gathered[:, off:]).reshape(N_DEV, N_META, d // N_DEV).transpose(1, 0, 2).reshape(N_META, d)
    small = {n: args[n] for n in _SMALL}

    loss, grad_x, g = _forward_backward(x, loss_target, meta_full, _model_weights(full), small, lp=lp)
    loss = lax.psum(loss, ("x", "y", "c"))

    gm = _reference_layout_grads(g)
    parts = jnp.concatenate([_scatter_layout(n, gm[n], shapes[n]) for n in _BIG], axis=1).astype(_WIRE_DTYPE)
    received = _exchange("scatter_grads", parts)
    g_rows = _sum_parts("sum_grads", received)
    grad, off = {}, 0
    for n in _BIG:
        grad[n] = g_rows[off:off + seg_rows[n]].reshape(shapes[n])
        off += seg_rows[n]

    small_parts = jnp.concatenate([_to_lanes(g[n]) for n in _SMALL] + [_to_lanes(g["meta_tokens"])], axis=0)
    n_small_rows = small_parts.shape[0]
    pad = -n_small_rows % _SUBLANES
    small_parts = jnp.concatenate([small_parts, jnp.zeros((pad, _LANES), F32)], axis=0)
    small_sum = _sum_parts("sum_small", _all_gather("gather_small", small_parts))
    off = 0
    for n in _SMALL:
        r = args[n].size // _LANES
        grad[n] = small_sum[off:off + r].reshape(args[n].shape)
        off += r
    me = 4 * lax.axis_index("x") + 2 * lax.axis_index("y") + lax.axis_index("c")
    meta_sum = small_sum[off:off + N_META * d // _LANES].reshape(N_META, d)
    grad["meta_tokens"] = lax.dynamic_slice_in_dim(meta_sum, me * (d // N_DEV), d // N_DEV, axis=1)

    delta, new_m, new_v = {}, {}, {}
    for n in _BIG + ["meta_tokens"]:
        shp = args[n].shape
        two_d = lambda t: t.reshape(shp[-2:])
        res = _adamw("adamw_" + n, two_d(args[n]), two_d(grad[n]), two_d(args["m_" + n]), two_d(args["v_" + n]))
        delta[n], new_m[n], new_v[n] = [t.reshape(shp) for t in res]
        grad[n] = grad[n].reshape(shp)
    cat = lambda prefix: jnp.concatenate(
        [_to_lanes(args[prefix + n]) for n in _SMALL] + [jnp.zeros((pad + N_META * d // _LANES, _LANES), F32)], axis=0)
    res = _adamw("adamw_small", cat(""), small_sum, cat("m_"), cat("v_"))
    off = 0
    for n in _SMALL:
        r = args[n].size // _LANES
        delta[n], new_m[n], new_v[n] = [t[off:off + r].reshape(args[n].shape) for t in res]
        off += r

    return (loss, grad_x, *[grad[n] for n in weight_names], *[delta[n] for n in weight_names],
            *[new_m[n] for n in weight_names], *[new_v[n] for n in weight_names])
```

```python
import functools

import jax
import jax.numpy as jnp
from jax import lax
from jax.experimental import pallas as pl
from jax.experimental.pallas import tpu as pltpu

F32 = jnp.float32
_MXU_DTYPE = jnp.bfloat16
_WIRE_DTYPE = jnp.bfloat16
_VMEM_LIMIT_BYTES = 56 * 1024 * 1024
_LANES = 128
_SUBLANES = 8

N_DEV = 8
N_META = 16
NORM_EPS = 1e-6
HEADS = 8
HEAD_DIM = 128
ROPE_DIM = 64
HG_CHUNK = 16
HG_BLOCK = 128
ROPE_THETA = 10000.0
D_MODEL = 1024
Q_LORA = 256
KV_LORA = 256
FFN_HIDDEN = 2816
ATTN_SCALE = (HEAD_DIM + ROPE_DIM) ** -0.5
NEG_BIG = -1e30

ADAM_LR = 0.001
ADAM_B1 = 0.9
ADAM_B2 = 0.999
ADAM_EPS = 1e-08
ADAM_WD = 0.01
ADAM_STEP = 10

CB_HQ, CB_HF, CB_HI, CB_HG, CB_C, CB_GA, CB_GB = range(7)
IN_COLS_PADDED = 7 * 1024


def _params(**kw):
    return pltpu.CompilerParams(vmem_limit_bytes=_VMEM_LIMIT_BYTES, **kw)


def _dot(a, b):
    return lax.dot_general(a, b, (((1,), (0,)), ((), ())), preferred_element_type=F32)


def _dot_nt(a, b):
    return lax.dot_general(a, b, (((1,), (1,)), ((), ())), preferred_element_type=F32)


def _dot_tn(a, b):
    return lax.dot_general(a, b, (((0,), (0,)), ((), ())), preferred_element_type=F32)


def _mx(x):
    return x.astype(_MXU_DTYPE)


def _exact_dot(m01, x):
    if _MXU_DTYPE == jnp.float32:
        return _dot(m01.astype(F32), x)
    m = m01.astype(jnp.bfloat16)
    x1 = x.astype(jnp.bfloat16)
    r1 = x - x1.astype(F32)
    x2 = r1.astype(jnp.bfloat16)
    x3 = (r1 - x2.astype(F32)).astype(jnp.bfloat16)
    return _dot(m, x1) + _dot(m, x2) + _dot(m, x3)


def _sigmoid(x):
    return jax.nn.sigmoid(x)


def _silu_grad(x, s):
    return s * (1.0 + x * (1.0 - s))


def _rms_scale(x):
    return lax.rsqrt(jnp.mean(x * x, axis=-1, keepdims=True) + NORM_EPS)


def _rms_bwd(x, g, dy):
    r = _rms_scale(x)
    xh = x * r
    w = dy * g
    dx = r * (w - xh * jnp.mean(xh * w, axis=-1, keepdims=True))
    return dx, dy * xh


def _heads(fn, *arrays):
    outs = [fn(*[a[:, h * HEAD_DIM:(h + 1) * HEAD_DIM] for a in arrays]) for h in range(HEADS)]
    if isinstance(outs[0], tuple):
        return tuple(jnp.concatenate([o[i] for o in outs], axis=1) for i in range(len(outs[0])))
    return jnp.concatenate(outs, axis=1)


class _Ride:
    def __init__(self, payload, gather):
        self.payload, self.gather, self.args = payload, gather, [payload]
        rows, lanes = payload.shape[-2:]
        self.in_specs = [pl.BlockSpec(memory_space=pl.ANY)]
        self.out_shape = [jax.ShapeDtypeStruct((N_DEV, rows, lanes), payload.dtype)]
        self.out_specs = [pl.BlockSpec(memory_space=pl.ANY)]
        self.scratch = [pltpu.SemaphoreType.DMA((N_DEV - 1,)), pltpu.SemaphoreType.DMA((N_DEV - 1,)),
                        pltpu.SemaphoreType.DMA]

    def _copies(self, p_ref, out_ref, send_sems, recv_sems, local_sem):
        x, y, c = lax.axis_index("x"), lax.axis_index("y"), lax.axis_index("c")
        me = 4 * x + 2 * y + c
        part = (lambda j: p_ref) if self.gather else (lambda j: p_ref.at[j])
        copies = [pltpu.make_async_copy(part(me), out_ref.at[me], local_sem)]
        for k in range(1, N_DEV):
            px, py, pc = x ^ (k >> 2), y ^ ((k >> 1) & 1), c ^ (k & 1)
            copies.append(pltpu.make_async_remote_copy(
                src_ref=part(4 * px + 2 * py + pc), dst_ref=out_ref.at[me],
                send_sem=send_sems.at[k - 1], recv_sem=recv_sems.at[k - 1],
                device_id=(px, py, pc), device_id_type=pl.DeviceIdType.MESH))
        return copies

    def run(self, grid, p_ref, out_ref, send_sems, recv_sems, local_sem):
        ids = [pl.program_id(i) for i in range(len(grid))]
        first = functools.reduce(jnp.logical_and, [i == 0 for i in ids])
        last = functools.reduce(jnp.logical_and, [i == g - 1 for i, g in zip(ids, grid)])

        @pl.when(first)
        def _():
            for cp in self._copies(p_ref, out_ref, send_sems, recv_sems, local_sem):
                cp.start()

        @pl.when(last)
        def _():
            for cp in self._copies(p_ref, out_ref, send_sems, recv_sems, local_sem):
                cp.wait()


class _NoRide:
    in_specs, out_shape, out_specs, scratch, args = [], [], [], [], []


def _matmul(name, a, b, *, out_dtype, tm, tn, tk, c_in=None, ride=None):
    m, k = a.shape
    _, n = b.shape
    assert m % tm == 0 and n % tn == 0 and k % tk == 0, (name, a.shape, b.shape, tm, tn, tk)
    nk = k // tk
    has_c = c_in is not None
    grid = (n // tn, m // tm, nk)
    n_in = 2 + has_c

    def body(*refs):
        a_ref, b_ref = refs[0], refs[1]
        c_ref = refs[2] if has_c else None
        o_ref = refs[n_in + (ride is not None)]
        if ride is not None:
            ride.run(grid, refs[n_in], refs[n_in + 2], *refs[-3:])
        acc_ref = refs[n_in + 1 + 2 * (ride is not None)] if nk > 1 else None

        def finish(r):
            if has_c:
                r = r + c_ref[...]
            o_ref[...] = r.astype(o_ref.dtype)

        if nk == 1:
            finish(_dot(a_ref[...], b_ref[...]))
        else:
            kk = pl.program_id(2)

            @pl.when(kk == 0)
            def _():
                acc_ref[...] = jnp.zeros_like(acc_ref)

            acc_ref[...] += _dot(a_ref[...], b_ref[...])

            @pl.when(kk == nk - 1)
            def _():
                finish(acc_ref[...])

    in_specs = [pl.BlockSpec((tm, tk), lambda j, i, kk: (i, kk)),
                pl.BlockSpec((tk, tn), lambda j, i, kk: (kk, j))]
    args = [a, b]
    aliases = {}
    if has_c:
        in_specs.append(pl.BlockSpec((tm, tn), lambda j, i, kk: (i, j)))
        args.append(c_in)
        aliases = {2: 0}
    out_shape = [jax.ShapeDtypeStruct((m, n), out_dtype)]
    out_specs = [pl.BlockSpec((tm, tn), lambda j, i, kk: (i, j))]
    scratch = [pltpu.VMEM((tm, tn), F32)] if nk > 1 else []
    if ride is not None:
        in_specs, args = in_specs + ride.in_specs, args + [ride.payload]
        out_shape, out_specs, scratch = out_shape + ride.out_shape, out_specs + ride.out_specs, scratch + ride.scratch
    res = pl.pallas_call(
        body, name=name, out_shape=out_shape, grid=grid, in_specs=in_specs, out_specs=out_specs,
        scratch_shapes=scratch, input_output_aliases=aliases, compiler_params=_params(),
    )(*args)
    return res[0] if ride is None else res


def _matmul_tn(name, x, dy, *, tk, tn, tr):
    r, k = x.shape
    _, n = dy.shape
    assert r % tr == 0 and k % tk == 0 and n % tn == 0, (name, x.shape, dy.shape)

    def body(x_ref, dy_ref, o_ref):
        @pl.when(pl.program_id(2) == 0)
        def _():
            o_ref[...] = jnp.zeros_like(o_ref)

        o_ref[...] += _dot_tn(x_ref[...], dy_ref[...])

    return pl.pallas_call(
        body, name=name,
        out_shape=jax.ShapeDtypeStruct((k, n), F32),
        grid=(k // tk, n // tn, r // tr),
        in_specs=[pl.BlockSpec((tr, tk), lambda kb, nb, rr: (rr, kb)),
                  pl.BlockSpec((tr, tn), lambda kb, nb, rr: (rr, nb))],
        out_specs=pl.BlockSpec((tk, tn), lambda kb, nb, rr: (kb, nb)),
        compiler_params=_params(),
    )(x, dy)


def _rowwise(name, body, *, rows, tr, lp, ins, outs, accs=()):
    assert rows % tr == 0 and lp % tr == 0 and tr % 16 == 0
    tiles_per_example = lp // tr
    in_specs, arrays = [], []
    for spec in ins:
        if spec[0] == "row":
            _, arr, width, cb = spec
            in_specs.append(pl.BlockSpec((tr, width), functools.partial(lambda i, cb: (i, cb), cb=cb)))
        elif spec[0] == "const":
            arr = spec[1]
            in_specs.append(pl.BlockSpec(arr.shape, lambda i: (0, 0)))
        else:
            arr = spec[1]
            in_specs.append(pl.BlockSpec((tr, arr.shape[1]), lambda i: (i % tiles_per_example, 0)))
        arrays.append(arr)
    n_in, n_out = len(ins), len(outs)

    def kern(*refs):
        res_outs, res_accs = body(*[r[...] for r in refs[:n_in]])
        for ref, val in zip(refs[n_in:n_in + n_out], res_outs, strict=True):
            ref[...] = val.astype(ref.dtype)
        acc_refs = refs[n_in + n_out:]
        if acc_refs:
            @pl.when(pl.program_id(0) == 0)
            def _():
                for ref in acc_refs:
                    ref[...] = jnp.zeros_like(ref)

            for ref, val in zip(acc_refs, res_accs, strict=True):
                ref[...] += val.reshape(tr // _SUBLANES, _SUBLANES, val.shape[-1]).sum(axis=0)

    out_shape = ([jax.ShapeDtypeStruct((rows, w), dt) for w, dt in outs]
                 + [jax.ShapeDtypeStruct((_SUBLANES, w), F32) for w in accs])
    out_specs = ([pl.BlockSpec((tr, w), lambda i: (i, 0)) for w, _ in outs]
                 + [pl.BlockSpec((_SUBLANES, w), lambda i: (0, 0)) for w in accs])
    res = pl.pallas_call(
        kern, name=name, out_shape=out_shape, grid=(rows // tr,),
        in_specs=in_specs, out_specs=out_specs, compiler_params=_params(),
    )(*arrays)
    return res[:n_out], [a.sum(axis=0) for a in res[n_out:]]


def _assemble(name, x, head_rows, lp):
    batch, seq, d = x.shape
    tc = 256

    def body(x_ref, m_ref, o_ref):
        o_ref[0:N_META, :] = m_ref[...]
        o_ref[N_META:N_META + seq, :] = x_ref[0]
        if lp > N_META + seq:
            o_ref[N_META + seq:, :] = jnp.zeros((lp - N_META - seq, tc), F32)

    return pl.pallas_call(
        body, name=name,
        out_shape=jax.ShapeDtypeStruct((batch * lp, d), F32),
        grid=(batch, d // tc),
        in_specs=[pl.BlockSpec((1, seq, tc), lambda b, j: (b, 0, j)),
                  pl.BlockSpec((N_META, tc), lambda b, j: (0, j))],
        out_specs=pl.BlockSpec((lp, tc), lambda b, j: (b, j)),
        compiler_params=_params(),
    )(x, head_rows)


def _meta_grad(dh0, batch, lp):
    d = dh0.shape[1]

    def body(g_ref, o_ref):
        @pl.when(pl.program_id(0) == 0)
        def _():
            o_ref[...] = jnp.zeros_like(o_ref)

        o_ref[...] += g_ref[...]

    return pl.pallas_call(
        body, name="meta_grad",
        out_shape=jax.ShapeDtypeStruct((N_META, d), F32),
        grid=(batch,),
        in_specs=[pl.BlockSpec((N_META, d), lambda b: (b * (lp // N_META), 0))],
        out_specs=pl.BlockSpec((N_META, d), lambda b: (0, 0)),
        compiler_params=_params(),
    )(dh0)


def _segment_masks():
    t = lax.broadcasted_iota(jnp.int32, (HG_BLOCK, HG_BLOCK), 0)
    s = lax.broadcasted_iota(jnp.int32, (HG_BLOCK, HG_BLOCK), 1)
    same = lax.shift_right_logical(t, 4) == lax.shift_right_logical(s, 4)
    lower = same & (s <= t)
    upper = same & (s >= t)
    first_half = same & ((s & 15) <= 7)
    return same, lower, upper, first_half


def _hgrn_gates(hq, hf, lb):
    sq = _sigmoid(hq)
    q = hq * sq
    sf = _sigmoid(hf)
    f = lb + (1.0 - lb) * sf
    return q, sq, sf, f


def _hgrn_decays(g, same, lower, first_half):
    b = _exact_dot(lower, g)
    b_last = _exact_dot(same, g)
    b_ref = _exact_dot(first_half, g)
    return b, b_last, b_ref


def _hgrn_fwd(p, lb, gh, *, batch, lp, ride=None):
    rows = batch * lp
    nb = lp // HG_BLOCK
    n_chunks = HG_BLOCK // HG_CHUNK

    def body(hq_ref, hf_ref, hi_ref, hg_ref, lb_ref, gh_ref, *rest):
        if ride is not None:
            ride.run((batch, nb), rest[0], rest[4], *rest[-3:])
            rest = rest[1:4] + rest[5:-3]
        o_ref, z_ref, st_ref, s_scr, qt_scr, kh_scr, v_scr, el_scr, o_scr = rest

        @pl.when(pl.program_id(1) == 0)
        def _():
            s_scr[...] = jnp.zeros_like(s_scr)

        same, lower, _, first_half = _segment_masks()
        v = hi_ref[...]
        q, _, _, f = _hgrn_gates(hq_ref[...], hf_ref[...], lb_ref[...])
        k = 1.0 - f
        b, b_last, b_ref = _hgrn_decays(jnp.log(f), same, lower, first_half)
        qt_scr[...] = _mx(q * jnp.exp(b))
        kh_scr[...] = _mx(k * jnp.exp(b_last - b))
        v_scr[...] = _mx(v)
        el_scr[...] = jnp.exp(b_last)
        qc = _mx(q * jnp.exp(b - b_ref))
        kc = _mx(k * jnp.exp(b_ref - b))

        def intra(qc_h, kc_h, v_h):
            a = jnp.where(lower, _dot_nt(qc_h, kc_h), 0.0)
            return _dot(_mx(a), v_h)

        o_scr[...] = _heads(intra, qc, kc, _mx(v))

        def chunk(c, carry):
            r0 = pl.multiple_of(c * HG_CHUNK, HG_CHUNK)
            rs = pl.ds(r0, HG_CHUNK)
            for h in range(HEADS):
                cs = slice(h * HEAD_DIM, (h + 1) * HEAD_DIM)
                st = s_scr[h]
                st_ref[c, h] = st.astype(st_ref.dtype)
                o_scr[rs, cs] += _dot_nt(qt_scr[rs, cs], _mx(st))
                s_scr[h] = st * el_scr[pl.ds(r0, 1), cs] + _dot_tn(v_scr[rs, cs], kh_scr[rs, cs])
            return carry

        lax.fori_loop(0, n_chunks, chunk, 0)

        o = o_scr[...]
        o_ref[...] = o
        hg = hg_ref[...]
        n = _heads(lambda o_h: o_h * _rms_scale(o_h), o) * gh_ref[...]
        z_ref[...] = (n * hg * _sigmoid(hg)).astype(z_ref.dtype)

    blk = lambda cb: pl.BlockSpec((HG_BLOCK, D_MODEL), functools.partial(lambda b, j, cb: (b * nb + j, cb), cb=cb))
    row_out = pl.BlockSpec((HG_BLOCK, D_MODEL), lambda b, j: (b * nb + j, 0))
    const = pl.BlockSpec((1, D_MODEL), lambda b, j: (0, 0))
    extra = ride if ride is not None else _NoRide
    return pl.pallas_call(
        body, name="hgrn_fwd",
        out_shape=[jax.ShapeDtypeStruct((rows, D_MODEL), F32),
                   jax.ShapeDtypeStruct((rows, D_MODEL), _MXU_DTYPE),
                   jax.ShapeDtypeStruct((rows // HG_CHUNK, HEADS, HEAD_DIM, HEAD_DIM), _MXU_DTYPE)] + extra.out_shape,
        grid=(batch, nb),
        in_specs=[blk(CB_HQ), blk(CB_HF), blk(CB_HI), blk(CB_HG), const, const] + extra.in_specs,
        out_specs=[row_out, row_out,
                   pl.BlockSpec((n_chunks, HEADS, HEAD_DIM, HEAD_DIM), lambda b, j: (b * nb + j, 0, 0, 0))]
        + extra.out_specs,
        scratch_shapes=[pltpu.VMEM((HEADS, HEAD_DIM, HEAD_DIM), F32),
                        pltpu.VMEM((HG_BLOCK, D_MODEL), _MXU_DTYPE),
                        pltpu.VMEM((HG_BLOCK, D_MODEL), _MXU_DTYPE),
                        pltpu.VMEM((HG_BLOCK, D_MODEL), _MXU_DTYPE),
                        pltpu.VMEM((HG_BLOCK, D_MODEL), F32),
                        pltpu.VMEM((HG_BLOCK, D_MODEL), F32)] + extra.scratch,
        compiler_params=_params(),
    )(p, p, p, p, lb, gh, *extra.args)


def _hgrn_bwd(p, o, dz, states, lb, gh, *, batch, lp, ride=None):
    rows = batch * lp
    nb = lp // HG_BLOCK
    n_chunks = HG_BLOCK // HG_CHUNK

    def body(hq_ref, hf_ref, hi_ref, hg_ref, o_ref, dz_ref, st_ref, lb_ref, gh_ref, *rest):
        if ride is not None:
            ride.run((batch, nb), rest[0], rest[4], *rest[-3:])
            rest = rest[1:4] + rest[5:-3]
        (dp_ref, dlb_ref, dgh_ref,
         ds_scr, qt_scr, kh_scr, v_scr, do_scr, el_scr, dqt_scr, dkh_scr, dv_scr, dbl_scr) = rest
        first = (pl.program_id(0) == 0) & (pl.program_id(1) == 0)

        @pl.when(first)
        def _():
            dlb_ref[...] = jnp.zeros_like(dlb_ref)
            dgh_ref[...] = jnp.zeros_like(dgh_ref)

        @pl.when(pl.program_id(1) == 0)
        def _():
            ds_scr[...] = jnp.zeros_like(ds_scr)

        same, lower, upper, first_half = _segment_masks()
        lbv = lb_ref[...]
        hq, hf, v, hg = hq_ref[...], hf_ref[...], hi_ref[...], hg_ref[...]
        q, sq, sf, f = _hgrn_gates(hq, hf, lbv)
        k = 1.0 - f
        b, b_last, b_ref = _hgrn_decays(jnp.log(f), same, lower, first_half)
        e_b = jnp.exp(b)
        e_kh = jnp.exp(b_last - b)
        e_qc = jnp.exp(b - b_ref)
        e_kc = jnp.exp(b_ref - b)
        qt, kh, qc, kc = q * e_b, k * e_kh, q * e_qc, k * e_kc

        o = o_ref[...]
        dz = dz_ref[...].astype(F32)
        ghv = gh_ref[...]
        sg = _sigmoid(hg)
        r = _heads(lambda o_h: jnp.broadcast_to(_rms_scale(o_h), o_h.shape), o)
        oh = o * r
        dn = dz * hg * sg
        dhg = dz * oh * ghv * _silu_grad(hg, sg)
        w = dn * ghv
        do = r * (w - oh * _heads(lambda t: jnp.broadcast_to(jnp.mean(t, axis=-1, keepdims=True), t.shape), oh * w))
        dgh_ref[...] += (dn * oh).reshape(HG_BLOCK // _SUBLANES, _SUBLANES, D_MODEL).sum(axis=0)

        qt_scr[...] = _mx(qt)
        kh_scr[...] = _mx(kh)
        v_scr[...] = _mx(v)
        do_scr[...] = _mx(do)
        el_scr[...] = jnp.exp(b_last)

        def intra(qc_h, kc_h, v_h, do_h):
            a = _mx(jnp.where(lower, _dot_nt(qc_h, kc_h), 0.0))
            da = _mx(jnp.where(lower, _dot_nt(do_h, v_h), 0.0))
            return _dot(da, kc_h), _dot_tn(da, qc_h), _dot_tn(a, do_h)

        dqc, dkc, dv_intra = _heads(intra, _mx(qc), _mx(kc), _mx(v), _mx(do))
        dv_scr[...] = dv_intra

        def chunk(i, carry):
            c = n_chunks - 1 - i
            r0 = pl.multiple_of(c * HG_CHUNK, HG_CHUNK)
            rs = pl.ds(r0, HG_CHUNK)
            for h in range(HEADS):
                cs = slice(h * HEAD_DIM, (h + 1) * HEAD_DIM)
                st = st_ref[c, h]
                ds_t = ds_scr[h]
                ds_m = _mx(ds_t)
                el = el_scr[pl.ds(r0, 1), cs]
                dkh_scr[rs, cs] = _dot(v_scr[rs, cs], ds_m)
                dv_scr[rs, cs] += _dot_nt(kh_scr[rs, cs], ds_m)
                dbl = jnp.sum(ds_t * st.astype(F32), axis=0, keepdims=True) * el
                dbl_scr[rs, cs] = jnp.broadcast_to(dbl, (HG_CHUNK, HEAD_DIM))
                dqt_scr[rs, cs] = _dot(do_scr[rs, cs], st)
                ds_scr[h] = ds_t * el + _dot_tn(do_scr[rs, cs], qt_scr[rs, cs])
            return carry

        lax.fori_loop(0, n_chunks, chunk, 0)

        dqt, dkh = dqt_scr[...], dkh_scr[...]
        dq = dqt * e_b + dqc * e_qc
        dk = dkh * e_kh + dkc * e_kc
        t_kh = dkh * kh
        db_rows = dqt * qt + dqc * qc - dkc * kc - t_kh
        dg = _exact_dot(upper, db_rows) + _exact_dot(same, t_kh) + dbl_scr[...]
        df = dg / f - dk
        dhf = df * (1.0 - lbv) * sf * (1.0 - sf)
        dlb_ref[...] += (df * (1.0 - sf)).reshape(HG_BLOCK // _SUBLANES, _SUBLANES, D_MODEL).sum(axis=0)
        dhq = dq * _silu_grad(hq, sq)
        dp_ref[...] = jnp.concatenate([dhq, dhf, dv_scr[...], dhg], axis=1).astype(dp_ref.dtype)

    rev = lambda b, j: b * nb + (nb - 1 - j)
    blk = lambda cb: pl.BlockSpec((HG_BLOCK, D_MODEL), functools.partial(lambda b, j, cb: (rev(b, j), cb), cb=cb))
    row = pl.BlockSpec((HG_BLOCK, D_MODEL), lambda b, j: (rev(b, j), 0))
    const = pl.BlockSpec((1, D_MODEL), lambda b, j: (0, 0))
    acc = pl.BlockSpec((_SUBLANES, D_MODEL), lambda b, j: (0, 0))
    big = lambda dt: pltpu.VMEM((HG_BLOCK, D_MODEL), dt)
    extra = ride if ride is not None else _NoRide
    dp, dlb, dgh, *exchanged = pl.pallas_call(
        body, name="hgrn_bwd",
        out_shape=[jax.ShapeDtypeStruct((rows, 4 * D_MODEL), _MXU_DTYPE),
                   jax.ShapeDtypeStruct((_SUBLANES, D_MODEL), F32),
                   jax.ShapeDtypeStruct((_SUBLANES, D_MODEL), F32)] + extra.out_shape,
        grid=(batch, nb),
        in_specs=[blk(CB_HQ), blk(CB_HF), blk(CB_HI), blk(CB_HG), row, row,
                  pl.BlockSpec((n_chunks, HEADS, HEAD_DIM, HEAD_DIM), lambda b, j: (rev(b, j), 0, 0, 0)),
                  const, const] + extra.in_specs,
        out_specs=[pl.BlockSpec((HG_BLOCK, 4 * D_MODEL), lambda b, j: (rev(b, j), 0)), acc, acc] + extra.out_specs,
        scratch_shapes=[pltpu.VMEM((HEADS, HEAD_DIM, HEAD_DIM), F32),
                        big(_MXU_DTYPE), big(_MXU_DTYPE), big(_MXU_DTYPE), big(_MXU_DTYPE),
                        big(F32), big(F32), big(F32), big(F32), big(F32)] + extra.scratch,
        compiler_params=_params(),
    )(p, p, p, p, o, dz, states, lb, gh, *extra.args)
    return (dp, dlb.sum(axis=0), dgh.sum(axis=0), *exchanged)


QK_DIM = 2 * HEAD_DIM
ATTN_TQ = 256
ATTN_KEY_CHUNK = 512


def _query_tiles(lp):
    return [(r0, min(ATTN_TQ, lp - r0)) for r0 in range(0, lp, ATTN_TQ)]


def _attn_fwd(q_cat, k_t, v, *, batch, lp):
    rows = batch * lp

    def body(q_ref, kt_ref, v_ref, o_ref, lse_ref):
        for r0, tq in _query_tiles(lp):
            q_t = q_ref[r0:r0 + tq, :]
            i = lax.broadcasted_iota(jnp.int32, (tq, tq), 0)
            j = lax.broadcasted_iota(jnp.int32, (tq, tq), 1)
            s_diag = jnp.where(j <= i, _dot(q_t, kt_ref[:, r0:r0 + tq]) * ATTN_SCALE, NEG_BIG)
            m = jnp.max(s_diag, axis=1, keepdims=True)
            if r0:
                s_past = _dot(q_t, kt_ref[:, 0:r0]) * ATTN_SCALE
                m = jnp.maximum(m, jnp.max(s_past, axis=1, keepdims=True))
            p_diag = jnp.exp(s_diag - m)
            l = jnp.sum(p_diag, axis=1, keepdims=True)
            acc = _dot(_mx(p_diag), v_ref[r0:r0 + tq, :])
            if r0:
                p_past = jnp.exp(s_past - m)
                l = l + jnp.sum(p_past, axis=1, keepdims=True)
                acc = acc + _dot(_mx(p_past), v_ref[0:r0, :])
            o_ref[r0:r0 + tq, :] = (acc / l).astype(o_ref.dtype)
            lse_ref[r0:r0 + tq, :] = jnp.broadcast_to(m + jnp.log(l), (tq, HEAD_DIM))

    head_blk = pl.BlockSpec((lp, HEAD_DIM), lambda b, h: (b, h))
    return pl.pallas_call(
        body, name="attn_fwd",
        out_shape=[jax.ShapeDtypeStruct((rows, D_MODEL), _MXU_DTYPE),
                   jax.ShapeDtypeStruct((rows, D_MODEL), F32)],
        grid=(batch, HEADS),
        in_specs=[pl.BlockSpec((lp, QK_DIM), lambda b, h: (b, h)),
                  pl.BlockSpec((QK_DIM, lp), lambda b, h: (b * HEADS + h, 0)),
                  head_blk],
        out_specs=[head_blk, head_blk],
        compiler_params=_params(),
    )(q_cat, k_t, v)


def _attn_bwd(q_cat, q_t, k_cat, k_t, v, do, do_t, o_t, lse_row, *, batch, lp, ride=None):
    rows = batch * lp

    def body(q_ref, qt_ref, k_ref, kt_ref, v_ref, do_ref, dot_ref, ot_ref, lse_ref, *rest):
        if ride is not None:
            ride.run((batch, HEADS), rest[0], rest[5], *rest[-3:])
            rest = rest[1:5] + rest[6:-3]
        dqt_ref, dkn_ref, dkp_ref, dv_ref, dk_acc, dv_acc = rest
        dk_acc[...] = jnp.zeros_like(dk_acc)
        dv_acc[...] = jnp.zeros_like(dv_acc)
        delta = jnp.sum(dot_ref[...].astype(F32) * ot_ref[...].astype(F32), axis=0, keepdims=True)
        for r0, tq in _query_tiles(lp):
            cols = slice(r0, r0 + tq)
            qt_t, dot_t = qt_ref[:, cols], dot_ref[:, cols]
            q_t_, do_t_ = q_ref[cols, :], do_ref[cols, :]
            lse_t, delta_t = lse_ref[0:1, cols], delta[:, cols]
            chunks = [(c0, min(ATTN_KEY_CHUNK, r0 - c0), False) for c0 in range(0, r0, ATTN_KEY_CHUNK)] + [(r0, tq, True)]
            dq_t = jnp.zeros((QK_DIM, tq), F32)
            for c0, n, diagonal in chunks:
                keys = slice(c0, c0 + n)
                s = _dot(k_ref[keys, :], qt_t) * ATTN_SCALE
                if diagonal:
                    jk = lax.broadcasted_iota(jnp.int32, (n, tq), 0)
                    iq = lax.broadcasted_iota(jnp.int32, (n, tq), 1)
                    s = jnp.where(jk <= iq, s, NEG_BIG)
                pexp = jnp.exp(s - lse_t)
                dp = _dot(v_ref[keys, :], dot_t)
                ds = _mx(pexp * (dp - delta_t) * ATTN_SCALE)
                dk_acc[keys, :] += _dot(ds, q_t_)
                dv_acc[keys, :] += _dot(_mx(pexp), do_t_)
                dq_t = dq_t + _dot(kt_ref[:, keys], ds)
            dqt_ref[:, cols] = dq_t

        dkn_ref[...] = dk_acc[:, 0:HEAD_DIM].astype(dkn_ref.dtype)
        dv_ref[...] = dv_acc[...].astype(dv_ref.dtype)

        @pl.when(pl.program_id(1) == 0)
        def _():
            dkp_ref[...] = jnp.zeros_like(dkp_ref)

        dkp_ref[...] += dk_acc[:, HEAD_DIM:]

    head_blk = pl.BlockSpec((lp, HEAD_DIM), lambda b, h: (b, h))
    cat_blk = pl.BlockSpec((lp, QK_DIM), lambda b, h: (b, h))
    cat_t_blk = pl.BlockSpec((QK_DIM, lp), lambda b, h: (b * HEADS + h, 0))
    head_t_blk = pl.BlockSpec((HEAD_DIM, lp), lambda b, h: (b * HEADS + h, 0))
    extra = ride if ride is not None else _NoRide
    return pl.pallas_call(
        body, name="attn_bwd",
        out_shape=[jax.ShapeDtypeStruct((batch * HEADS * QK_DIM, lp), F32),
                   jax.ShapeDtypeStruct((rows, D_MODEL), _MXU_DTYPE),
                   jax.ShapeDtypeStruct((rows, HEAD_DIM), F32),
                   jax.ShapeDtypeStruct((rows, D_MODEL), _MXU_DTYPE)] + extra.out_shape,
        grid=(batch, HEADS),
        in_specs=[cat_blk, cat_t_blk, cat_blk, cat_t_blk, head_blk, head_blk, head_t_blk, head_t_blk,
                  pl.BlockSpec((_SUBLANES, lp), lambda b, h: (b * HEADS + h, 0))] + extra.in_specs,
        out_specs=[cat_t_blk, head_blk, pl.BlockSpec((lp, HEAD_DIM), lambda b, h: (b, 0)), head_blk] + extra.out_specs,
        scratch_shapes=[pltpu.VMEM((lp, QK_DIM), F32), pltpu.VMEM((lp, HEAD_DIM), F32)] + extra.scratch,
        compiler_params=_params(),
    )(q_cat, q_t, k_cat, k_t, v, do, do_t, o_t, lse_row, *extra.args)


def _all_gather(name, block):
    rows, lanes = block.shape

    def body(x_ref, out_ref, send_sems, recv_sems, local_sem):
        x, y, c = lax.axis_index("x"), lax.axis_index("y"), lax.axis_index("c")
        me, sibling = (x, y, c), (x, y, 1 - c)
        chips = [(1 - x, y), (x, 1 - y), (1 - x, 1 - y)]

        def slot(px, py, pc):
            return out_ref.at[4 * px + 2 * py + pc]

        def copy(k, blk, to, src=None):
            return pltpu.make_async_remote_copy(
                src_ref=slot(*blk) if src is None else src, dst_ref=slot(*blk),
                send_sem=send_sems.at[k], recv_sem=recv_sems.at[k],
                device_id=to, device_id_type=pl.DeviceIdType.MESH)

        mine = pltpu.make_async_copy(x_ref, slot(*me), local_sem)
        mine.start()
        first = [copy(0, me, sibling, src=x_ref)]
        first += [copy(1 + j, me, (*chip, c), src=x_ref) for j, chip in enumerate(chips)]
        for cp in first:
            cp.start()
        passed = [copy(4 + j, (*chip, c), sibling) for j, chip in enumerate(chips)]
        for j, chip in enumerate(chips):
            copy(1 + j, (*chip, c), me).wait_recv()
            passed[j].start()
        copy(0, sibling, me).wait_recv()
        for j, chip in enumerate(chips):
            copy(4 + j, (*chip, 1 - c), me).wait_recv()
        for cp in first + passed:
            cp.wait_send()
        mine.wait()

    return pl.pallas_call(
        body, name=name,
        out_shape=jax.ShapeDtypeStruct((N_DEV, rows, lanes), block.dtype),
        in_specs=[pl.BlockSpec(memory_space=pl.ANY)],
        out_specs=pl.BlockSpec(memory_space=pl.ANY),
        scratch_shapes=[pltpu.SemaphoreType.DMA((7,)), pltpu.SemaphoreType.DMA((7,)), pltpu.SemaphoreType.DMA],
    )(block)


def _row_tile(rows, align, cap):
    best = rows
    for t in range(align, min(rows, cap) + 1, align):
        if rows % t == 0:
            best = t
    return best


def _sum_parts(name, parts):
    _, rows, lanes = parts.shape
    tr = _row_tile(rows, 16, 2048)

    def body(p_ref, o_ref):
        acc = p_ref[0].astype(F32)
        for d in range(1, N_DEV):
            acc = acc + p_ref[d].astype(F32)
        o_ref[...] = acc

    return pl.pallas_call(
        body, name=name,
        out_shape=jax.ShapeDtypeStruct((rows, lanes), F32),
        grid=(rows // tr,),
        in_specs=[pl.BlockSpec((N_DEV, tr, lanes), lambda i: (0, i, 0))],
        out_specs=pl.BlockSpec((tr, lanes), lambda i: (i, 0)),
        compiler_params=_params(),
    )(parts)


def _adamw(name, w, g, m, v):
    rows, cols = w.shape
    tr = rows // 4 if rows % 32 == 0 and rows * cols > (1 << 16) else rows

    def body(w_ref, g_ref, m_ref, v_ref, d_ref, nm_ref, nv_ref):
        gv = g_ref[...]
        nm = ADAM_B1 * m_ref[...] + (1.0 - ADAM_B1) * gv
        nv = ADAM_B2 * v_ref[...] + (1.0 - ADAM_B2) * (gv * gv)
        m_hat = nm / (1.0 - ADAM_B1 ** ADAM_STEP)
        v_hat = nv / (1.0 - ADAM_B2 ** ADAM_STEP)
        d_ref[...] = -ADAM_LR * (m_hat / (jnp.sqrt(v_hat) + ADAM_EPS) + ADAM_WD * w_ref[...])
        nm_ref[...] = nm
        nv_ref[...] = nv

    spec = pl.BlockSpec((tr, cols), lambda i: (i, 0))
    return pl.pallas_call(
        body, name=name,
        out_shape=[jax.ShapeDtypeStruct((rows, cols), F32)] * 3,
        grid=(rows // tr,),
        in_specs=[spec] * 4, out_specs=[spec] * 3,
        compiler_params=_params(),
    )(w, g, m, v)


def _swap_halves(t):
    half = t.shape[-1] // 2
    return jnp.concatenate([t[..., half:], t[..., :half]], axis=-1)


def _pad_last(t, width):
    return jnp.concatenate([t, jnp.zeros(t.shape[:-1] + (width - t.shape[-1],), t.dtype)], axis=-1)


def _rope_tables(lp):
    pos = jnp.arange(lp, dtype=F32)
    inv_freq = 1.0 / (ROPE_THETA ** (jnp.arange(0, ROPE_DIM, 2, dtype=F32) / ROPE_DIM))
    ang = pos[:, None] * inv_freq[None, :]
    cos, sin = jnp.cos(ang), jnp.sin(ang)
    c128 = _pad_last(jnp.concatenate([cos, cos], axis=1), HEAD_DIM)
    s128 = _pad_last(jnp.concatenate([-sin, sin], axis=1), HEAD_DIM)
    return c128, s128


def _forward_backward(x, target, meta, w, small, *, lp, comm=None):
    batch, seq, d = x.shape
    rows = batch * lp
    tr = 272 if lp % 272 == 0 else 128
    tm = lp // 2
    bf = _MXU_DTYPE
    rw = functools.partial(_rowwise, rows=rows, tr=tr, lp=lp)

    c128, s128 = _rope_tables(lp)
    cq_tab, sq_tab = jnp.tile(c128, (1, HEADS)), jnp.tile(s128, (1, HEADS))
    t_idx = jnp.arange(lp)
    real = jnp.broadcast_to(((t_idx >= N_META) & (t_idx < N_META + seq)).astype(F32)[:, None], (lp, _LANES))

    lb_logits = small["lb_logits"]
    lb = jax.nn.softmax(lb_logits, axis=0)[0:1]
    gh = jnp.tile(small["hg_norm_g"], (1, HEADS))

    h0 = _assemble("assemble_x", x, meta, lp)
    tgt = _assemble("assemble_target", target, jnp.zeros_like(meta), lp)

    (u1,), _ = rw("norm_mix_pre", lambda h, g: ([h * _rms_scale(h) * g], []),
                  ins=[("row", h0, d, 0), ("const", small["mix_pre_g"])], outs=[(d, bf)])
    p = _matmul("proj_in", u1, w["w_in"], out_dtype=F32, tm=tm, tn=1024, tk=1024)

    if comm is None:
        o_hg, z_a, states = _hgrn_fwd(p, lb, gh, batch=batch, lp=lp)
    else:
        o_hg, z_a, states, gathered = _hgrn_fwd(p, lb, gh, batch=batch, lp=lp, ride=_Ride(comm.rest_payload, True))
        w = {**w, **comm.rest_weights(gathered)}
    received = []
    scatter = lambda names: _Ride(comm.grad_parts(names, grads), False) if comm is not None else None
    y_a = _matmul("proj_hg_o", z_a, w["w_hg_o"], out_dtype=F32, tm=tm, tn=1024, tk=1024)

    def mla_pre(pc, gq, gkv, ct, st):
        cq, ckv = pc[:, 0:Q_LORA], pc[:, Q_LORA:Q_LORA + KV_LORA]
        kpe, kpe_sw = pc[:, 512:640], pc[:, 640:768]
        return [cq * _rms_scale(cq) * gq, ckv * _rms_scale(ckv) * gkv, kpe * ct + kpe_sw * st], []

    (cqn, ckvn, kp), _ = rw("mla_pre", mla_pre,
                            ins=[("row", p, 1024, CB_C), ("const", small["q_a_norm_g"]),
                                 ("const", small["kv_a_norm_g"]), ("pos", c128), ("pos", s128)],
                            outs=[(Q_LORA, bf), (KV_LORA, bf), (HEAD_DIM, bf)])
    qf = _matmul("proj_q_b", cqn, w["w_q"], out_dtype=F32, tm=tm, tn=1024, tk=Q_LORA)
    def rope_q(a, pe, pes, ct, st):
        roped = pe * ct + pes * st
        hs = lambda t, h: t[:, h * HEAD_DIM:(h + 1) * HEAD_DIM]
        return [jnp.concatenate([t for h in range(HEADS) for t in (hs(a, h), hs(roped, h))], axis=1)], []

    (q_cat,), _ = rw("rope_q", rope_q,
                     ins=[("row", qf, 1024, 0), ("row", qf, 1024, 1), ("row", qf, 1024, 2),
                          ("pos", cq_tab), ("pos", sq_tab)], outs=[(HEADS * QK_DIM, bf)])
    kv = _matmul("proj_kv_b", ckvn, w["w_kv"], out_dtype=bf, tm=tm, tn=1024, tk=KV_LORA)
    kn, vv = kv[:, :d], kv[:, d:]
    k_cat = jnp.concatenate([kn.reshape(rows, HEADS, HEAD_DIM),
                             jnp.broadcast_to(kp[:, None, :], (rows, HEADS, HEAD_DIM))], axis=2).reshape(rows, HEADS * QK_DIM)
    head_t = lambda t, wd: t.reshape(batch, lp, HEADS, wd).transpose(0, 2, 3, 1).reshape(batch * HEADS * wd, lp)
    k_t = head_t(k_cat, QK_DIM)
    o_at, lse = _attn_fwd(q_cat, k_t, vv, batch=batch, lp=lp)
    y_b = _matmul("proj_mla_o", o_at, w["w_mla_o"], out_dtype=F32, tm=tm, tn=1024, tk=1024)

    def merge(pa, pb, ya, yb, bg):
        ga, gb = _sigmoid(pa + bg[:, :d]), _sigmoid(pb + bg[:, d:])
        return [ga * ya + gb * yb], []

    (mix,), _ = rw("merge", merge,
                   ins=[("row", p, 1024, CB_GA), ("row", p, 1024, CB_GB), ("row", y_a, d, 0), ("row", y_b, d, 0),
                        ("const", small["b_gate"])], outs=[(d, bf)])
    mixed = _matmul("proj_out", mix, w["w_out"], out_dtype=F32, tm=tm, tn=1024, tk=1024)

    def post_mix(mx_, h, g2, g3):
        h1_ = h + mx_ * _rms_scale(mx_) * g2
        return [h1_, h1_ * _rms_scale(h1_) * g3], []

    (h1, u2), _ = rw("post_mix", post_mix,
                     ins=[("row", mixed, d, 0), ("row", h0, d, 0), ("const", small["mix_post_g"]),
                          ("const", small["ffn_pre_g"])], outs=[(d, F32), (d, bf)])
    gu = _matmul("ffn_in", u2, w["w_ffn_in"], out_dtype=F32, tm=tm, tn=1408, tk=1024)
    (act,), _ = rw("swiglu", lambda gt, up: ([gt * _sigmoid(gt) * up], []),
                   ins=[("row", gu, FFN_HIDDEN, 0), ("row", gu, FFN_HIDDEN, 1)], outs=[(FFN_HIDDEN, bf)])
    fo = _matmul("ffn_out", act, w["w_ffn_out"], out_dtype=F32, tm=tm, tn=1024, tk=1408)

    def post_ffn(fo_, h1_, t_, mask, g4):
        r = _rms_scale(fo_)
        h2 = h1_ + fo_ * r * g4
        err = (h2 - t_) * mask[:, 0:1]
        dh2 = err * (1.0 / d)
        dfo, dg4 = _rms_bwd(fo_, g4, dh2)
        return [dh2, dfo], [err * err, dg4]

    (dh2, dfo), (loss_vec, dg_ffn_post) = rw(
        "post_ffn_loss", post_ffn,
        ins=[("row", fo, d, 0), ("row", h1, d, 0), ("row", tgt, d, 0), ("pos", real), ("const", small["ffn_post_g"])],
        outs=[(d, F32), (d, bf)], accs=[d, d])
    loss = (0.5 / d) * jnp.sum(loss_vec)

    grads = {}
    d_act = _matmul("d_ffn_out", dfo, w["w_ffn_out_t"], out_dtype=F32, tm=tm, tn=1408, tk=1024)
    grads["w_ffn_out"] = _matmul_tn("dw_ffn_out", act, dfo, tk=1408, tn=1024, tr=tm)

    def swiglu_bwd(gt, up, da):
        s = _sigmoid(gt)
        return [jnp.concatenate([da * up * _silu_grad(gt, s), da * gt * s], axis=1)], []

    (dgu,), _ = rw("swiglu_bwd", swiglu_bwd,
                   ins=[("row", gu, FFN_HIDDEN, 0), ("row", gu, FFN_HIDDEN, 1), ("row", d_act, FFN_HIDDEN, 0)],
                   outs=[(2 * FFN_HIDDEN, bf)])
    du2 = _matmul("d_ffn_in", dgu, w["w_ffn_in_t"], out_dtype=F32, tm=tm, tn=1024, tk=1408)
    grads["w_ffn_in"] = _matmul_tn("dw_ffn_in", u2, dgu, tk=1024, tn=1408, tr=tm)

    def post_mix_bwd(du2_, h1_, dh2_, mx_, g3, g2):
        dx, dg3 = _rms_bwd(h1_, g3, du2_)
        dh1_ = dh2_ + dx
        dmx, dg2 = _rms_bwd(mx_, g2, dh1_)
        return [dh1_, dmx], [dg3, dg2]

    (dh1, dmixed), (dg_ffn_pre, dg_mix_post) = rw(
        "post_mix_bwd", post_mix_bwd,
        ins=[("row", du2, d, 0), ("row", h1, d, 0), ("row", dh2, d, 0), ("row", mixed, d, 0),
             ("const", small["ffn_pre_g"]), ("const", small["mix_post_g"])],
        outs=[(d, F32), (d, bf)], accs=[d, d])
    dmix = _matmul("d_proj_out", dmixed, w["w_out_t"], out_dtype=F32, tm=tm, tn=1024, tk=1024)
    grads["w_out"] = _matmul_tn("dw_out", mix, dmixed, tk=1024, tn=1024, tr=tm)

    def merge_bwd(dm, pa, pb, ya, yb, bg):
        ga, gb = _sigmoid(pa + bg[:, :d]), _sigmoid(pb + bg[:, d:])
        dpg = jnp.concatenate([dm * ya * ga * (1.0 - ga), dm * yb * gb * (1.0 - gb)], axis=1)
        return [dpg, dm * ga, dm * gb], [dpg]

    (dpg, dya, dyb), (db_gate,) = rw(
        "merge_bwd", merge_bwd,
        ins=[("row", dmix, d, 0), ("row", p, 1024, CB_GA), ("row", p, 1024, CB_GB), ("row", y_a, d, 0),
             ("row", y_b, d, 0), ("const", small["b_gate"])],
        outs=[(2 * d, bf), (d, bf), (d, bf)], accs=[2 * d])
    dz_a = _matmul("d_proj_hg_o", dya, w["w_hg_o_t"], out_dtype=F32, tm=tm, tn=1024, tk=1024)
    grads["w_hg_o"] = _matmul_tn("dw_hg_o", z_a, dya, tk=1024, tn=1024, tr=tm)
    do_at = _matmul("d_proj_mla_o", dyb, w["w_mla_o_t"], out_dtype=bf, tm=tm, tn=1024, tk=1024)
    grads["w_mla_o"] = _matmul_tn("dw_mla_o", o_at, dyb, tk=1024, tn=1024, tr=tm)

    dph, dlb, dgh, *got = _hgrn_bwd(p, o_hg, dz_a, states, lb, gh, batch=batch, lp=lp,
                                    ride=scatter(_GRAD_GROUPS[0]))
    received += got

    lse_row = jnp.broadcast_to(lse[:, ::HEAD_DIM].reshape(batch, lp, HEADS).transpose(0, 2, 1)[:, :, None, :],
                               (batch, HEADS, _SUBLANES, lp)).reshape(batch * HEADS * _SUBLANES, lp)
    dq_t, dkn, dkp, dvv, *got = _attn_bwd(q_cat, head_t(q_cat, QK_DIM), k_cat, k_t, vv, do_at, head_t(do_at, HEAD_DIM),
                                          head_t(o_at, HEAD_DIM), lse_row, batch=batch, lp=lp,
                                          ride=scatter(_GRAD_GROUPS[1]))
    received += got
    dq_cat = dq_t.reshape(batch, HEADS, QK_DIM, lp).transpose(0, 3, 1, 2).reshape(rows, HEADS * QK_DIM)

    def rope_q_bwd(dq, ct, st):
        hs = lambda half: jnp.concatenate(
            [dq[:, h * QK_DIM + half * HEAD_DIM:h * QK_DIM + (half + 1) * HEAD_DIM] for h in range(HEADS)], axis=1)
        dpe = hs(1)
        return [jnp.concatenate([hs(0), dpe * ct, dpe * st], axis=1)], []

    (dqf,), _ = rw("rope_q_bwd", rope_q_bwd,
                   ins=[("row", dq_cat, HEADS * QK_DIM, 0), ("pos", cq_tab), ("pos", sq_tab)],
                   outs=[(3 * d, bf)])
    dcqn = _matmul("d_proj_q_b", dqf, w["w_q_t"], out_dtype=F32, tm=tm, tn=Q_LORA, tk=1024)
    grads["w_q"] = _matmul_tn("dw_q_b", cqn, dqf, tk=Q_LORA, tn=1024, tr=tm)
    dckvn = _matmul("d_proj_k_b", dkn, w["w_k_t"], out_dtype=F32, tm=tm, tn=KV_LORA, tk=1024)
    dckvn = _matmul("d_proj_v_b", dvv, w["w_v_t"], out_dtype=F32, tm=tm, tn=KV_LORA, tk=1024, c_in=dckvn)
    grads["w_k"] = _matmul_tn("dw_k_b", ckvn, dkn, tk=KV_LORA, tn=1024, tr=tm)
    grads["w_v"] = _matmul_tn("dw_v_b", ckvn, dvv, tk=KV_LORA, tn=1024, tr=tm)

    def mla_pre_bwd(pc, dq_, dkv_, dkp_, gq, gkv, ct, st):
        cq, ckv = pc[:, 0:Q_LORA], pc[:, Q_LORA:Q_LORA + KV_LORA]
        dcq, dgq = _rms_bwd(cq, gq, dq_)
        dckv, dgkv = _rms_bwd(ckv, gkv, dkv_)
        dpc = jnp.concatenate([dcq, dckv, dkp_ * ct, dkp_ * st, jnp.zeros((pc.shape[0], 256), F32)], axis=1)
        return [dpc], [dgq, dgkv]

    (dpc,), (dg_q, dg_kv) = rw(
        "mla_pre_bwd", mla_pre_bwd,
        ins=[("row", p, 1024, CB_C), ("row", dcqn, Q_LORA, 0), ("row", dckvn, KV_LORA, 0), ("row", dkp, HEAD_DIM, 0),
             ("const", small["q_a_norm_g"]), ("const", small["kv_a_norm_g"]), ("pos", c128), ("pos", s128)],
        outs=[(1024, bf)], accs=[Q_LORA, KV_LORA])

    grads["w_in"] = jnp.concatenate([
        _matmul_tn("dw_in_h", u1, dph, tk=1024, tn=1024, tr=tm),
        _matmul_tn("dw_in_c", u1, dpc, tk=1024, tn=1024, tr=tm),
        _matmul_tn("dw_in_g", u1, dpg, tk=1024, tn=1024, tr=tm)], axis=1)
    du1 = _matmul("d_proj_in_h", dph, w["w_in_t"][:4096], out_dtype=F32, tm=tm, tn=1024, tk=1024,
                  ride=scatter(_GRAD_GROUPS[2]))
    if comm is not None:
        du1, got = du1
        received.append(got)
    du1 = _matmul("d_proj_in_c", dpc, w["w_in_t"][4096:5120], out_dtype=F32, tm=tm, tn=1024, tk=1024, c_in=du1)
    du1 = _matmul("d_proj_in_g", dpg, w["w_in_t"][5120:], out_dtype=F32, tm=tm, tn=1024, tk=1024, c_in=du1)

    def pre_bwd(du, h, dh, g1):
        dx, dg1 = _rms_bwd(h, g1, du)
        return [dh + dx], [dg1]

    (dh0,), (dg_mix_pre,) = rw("norm_mix_pre_bwd", pre_bwd,
                               ins=[("row", du1, d, 0), ("row", h0, d, 0), ("row", dh1, d, 0), ("const", small["mix_pre_g"])],
                               outs=[(d, F32)], accs=[d])
    grad_x = dh0.reshape(batch, lp, d)[:, N_META:N_META + seq]
    grads["meta_tokens"] = _meta_grad(dh0, batch, lp)

    dl0 = dlb * lb[0] * (1.0 - lb[0])
    grads["lb_logits"] = jnp.stack([dl0, -dl0])
    grads["b_gate"] = db_gate[None]
    grads["hg_norm_g"] = dgh.reshape(HEADS, HEAD_DIM).sum(axis=0)[None]
    grads["q_a_norm_g"] = dg_q[None]
    grads["kv_a_norm_g"] = dg_kv[None]
    grads["mix_pre_g"] = dg_mix_pre[None]
    grads["mix_post_g"] = dg_mix_post[None]
    grads["ffn_pre_g"] = dg_ffn_pre[None]
    grads["ffn_post_g"] = dg_ffn_post[None]
    return loss, grad_x, grads, received


_BIG = ["w_in", "w_hg_o", "w_q_b", "w_kv_b", "w_mla_o", "w_out", "w_ffn_in", "w_ffn_out"]
_COLUMN_SHARDED = {"w_in", "w_q_b", "w_kv_b", "w_ffn_in"}
_GRAD_GROUPS = [["w_ffn_in", "w_ffn_out"], ["w_out", "w_hg_o", "w_mla_o"], ["w_in", "w_q_b", "w_kv_b"]]
_SMALL = ["b_gate", "lb_logits", "hg_norm_g", "q_a_norm_g", "kv_a_norm_g", "mix_pre_g", "mix_post_g",
          "ffn_pre_g", "ffn_post_g"]


def _to_lanes(t):
    return t.reshape(-1, _LANES)


def _gathered_matrix(name, seg, shard_shape):
    k, n = shard_shape
    t = seg.reshape(N_DEV, k, n)
    if name in _COLUMN_SHARDED:
        return t.transpose(1, 0, 2).reshape(k, N_DEV * n)
    return t.reshape(N_DEV * k, n)


def _scatter_layout(name, full, shard_shape):
    k, n = shard_shape
    if name in _COLUMN_SHARDED:
        t = full.reshape(k, N_DEV, n).transpose(1, 0, 2)
    else:
        t = full.reshape(N_DEV, k, n)
    return t.reshape(N_DEV, -1, _LANES)


def _model_w_in(wi):
    z = lambda *s: jnp.zeros(s, wi.dtype)
    kpe = wi[:, 4608:4672]
    c_blk = jnp.concatenate([wi[:, 4096:4608], kpe, z(1024, 64), _swap_halves(kpe), z(1024, 64), z(1024, 256)], axis=1)
    w_in = jnp.concatenate([wi[:, :4096], c_blk, wi[:, 4672:]], axis=1).astype(_MXU_DTYPE)
    return {"w_in": w_in, "w_in_t": w_in.T}


def _model_weights(full):
    return {**_model_w_in(full["w_in"]), **_model_rest(full)}


def _model_rest(full):
    wq3 = full["w_q_b"].reshape(Q_LORA, HEADS, HEAD_DIM + ROPE_DIM)
    pe = wq3[:, :, HEAD_DIM:]
    w_q = jnp.concatenate([wq3[:, :, :HEAD_DIM].reshape(Q_LORA, -1),
                           _pad_last(pe, HEAD_DIM).reshape(Q_LORA, -1),
                           _pad_last(_swap_halves(pe), HEAD_DIM).reshape(Q_LORA, -1)], axis=1)
    wkv3 = full["w_kv_b"].reshape(KV_LORA, HEADS, 2 * HEAD_DIM)
    w_k = wkv3[:, :, :HEAD_DIM].reshape(KV_LORA, -1)
    w_v = wkv3[:, :, HEAD_DIM:].reshape(KV_LORA, -1)
    w = {"w_q": w_q, "w_kv": jnp.concatenate([w_k, w_v], axis=1),
         "w_hg_o": full["w_hg_o"], "w_mla_o": full["w_mla_o"], "w_out": full["w_out"],
         "w_ffn_in": full["w_ffn_in"], "w_ffn_out": full["w_ffn_out"]}
    for n in ["w_q", "w_hg_o", "w_mla_o", "w_out", "w_ffn_in", "w_ffn_out"]:
        w[n + "_t"] = w[n].T
    w["w_k_t"], w["w_v_t"] = w_k.T, w_v.T
    return {k: v.astype(_MXU_DTYPE) for k, v in w.items()}


def _reference_layout_grad(name, g):
    if name == "w_in":
        gi = g["w_in"]
        d_kpe = gi[:, 4608:4672] + _swap_halves(gi[:, 4736:4800])
        return jnp.concatenate([gi[:, :4608], d_kpe, gi[:, 5120:]], axis=1)
    if name == "w_q_b":
        gq = g["w_q"]
        d_pe = (gq[:, 1024:2048].reshape(Q_LORA, HEADS, HEAD_DIM)[:, :, :ROPE_DIM]
                + _swap_halves(gq[:, 2048:].reshape(Q_LORA, HEADS, HEAD_DIM)[:, :, :ROPE_DIM]))
        return jnp.concatenate([gq[:, :1024].reshape(Q_LORA, HEADS, HEAD_DIM), d_pe], axis=2).reshape(Q_LORA, -1)
    if name == "w_kv_b":
        return jnp.concatenate([g["w_k"].reshape(KV_LORA, HEADS, HEAD_DIM),
                                g["w_v"].reshape(KV_LORA, HEADS, HEAD_DIM)], axis=2).reshape(KV_LORA, -1)
    return g[name]


def _reference_layout_grads(g):
    return {n: _reference_layout_grad(n, g) for n in _BIG}


class _Comm:
    def __init__(self, shard):
        self.shapes = {n: shard[n].shape for n in _BIG}
        self.seg_rows = {n: shard[n].size // _LANES for n in _BIG}
        self.rest_payload = jnp.concatenate([_to_lanes(shard[n]).astype(_WIRE_DTYPE) for n in _BIG[1:]], axis=0)

    def unpack(self, names, buf, layout):
        out, off = {}, 0
        for n in names:
            out[n] = layout(n, buf[..., off:off + self.seg_rows[n], :], self.shapes[n])
            off += self.seg_rows[n]
        return out

    def rest_weights(self, gathered):
        return _model_rest(self.unpack(_BIG[1:], gathered, _gathered_matrix))

    def grad_parts(self, names, g):
        return jnp.concatenate([_scatter_layout(n, _reference_layout_grad(n, g), self.shapes[n]) for n in names],
                               axis=1).astype(_WIRE_DTYPE)


def _wire(t):
    if jnp.dtype(_WIRE_DTYPE).itemsize == 4:
        return t.astype(_WIRE_DTYPE)
    return lax.bitcast_convert_type(t, _WIRE_DTYPE)


def _unwire(t):
    if jnp.dtype(_WIRE_DTYPE).itemsize == 4:
        return t.astype(F32)
    return lax.bitcast_convert_type(t.reshape(t.shape[:-1] + (t.shape[-1] // 2, 2)), F32)


def kernel(x, meta_tokens, w_in, b_gate, lb_logits, hg_norm_g, w_hg_o, q_a_norm_g, w_q_b, kv_a_norm_g, w_kv_b, w_mla_o, w_out, mix_pre_g, mix_post_g, ffn_pre_g, ffn_post_g, w_ffn_in, w_ffn_out, loss_target, m_meta_tokens, m_w_in, m_b_gate, m_lb_logits, m_hg_norm_g, m_w_hg_o, m_q_a_norm_g, m_w_q_b, m_kv_a_norm_g, m_w_kv_b, m_w_mla_o, m_w_out, m_mix_pre_g, m_mix_post_g, m_ffn_pre_g, m_ffn_post_g, m_w_ffn_in, m_w_ffn_out, v_meta_tokens, v_w_in, v_b_gate, v_lb_logits, v_hg_norm_g, v_w_hg_o, v_q_a_norm_g, v_w_q_b, v_kv_a_norm_g, v_w_kv_b, v_w_mla_o, v_w_out, v_mix_pre_g, v_mix_post_g, v_ffn_pre_g, v_ffn_post_g, v_w_ffn_in, v_w_ffn_out):
    args = dict(locals())
    batch, seq, d = x.shape
    lp = -(-(N_META + seq) // _LANES) * _LANES
    weight_names = ["meta_tokens", "w_in", "b_gate", "lb_logits", "hg_norm_g", "w_hg_o", "q_a_norm_g", "w_q_b",
                    "kv_a_norm_g", "w_kv_b", "w_mla_o", "w_out", "mix_pre_g", "mix_post_g", "ffn_pre_g",
                    "ffn_post_g", "w_ffn_in", "w_ffn_out"]
    shard = {n: args[n].reshape(args[n].shape[-2:]) for n in _BIG}
    comm = _Comm(shard)

    packed = jnp.concatenate([_to_lanes(shard["w_in"]).astype(_WIRE_DTYPE), _to_lanes(_wire(meta_tokens))], axis=0)
    gathered = _all_gather("gather_first", packed)
    n_in = comm.seg_rows["w_in"]
    w_first = _model_w_in(_gathered_matrix("w_in", gathered[:, :n_in], comm.shapes["w_in"]))
    meta_full = _unwire(gathered[:, n_in:]).reshape(N_DEV, N_META, d // N_DEV).transpose(1, 0, 2).reshape(N_META, d)
    small = {n: args[n] for n in _SMALL}

    loss, grad_x, g, received = _forward_backward(x, loss_target, meta_full, w_first, small, lp=lp, comm=comm)
    loss = lax.psum(loss, ("x", "y", "c"))
    grad = {}
    for i, (names, buf) in enumerate(zip(_GRAD_GROUPS, received, strict=True)):
        grad.update(comm.unpack(names, _sum_parts(f"sum_grads_{i}", buf), lambda n, seg, shape: seg.reshape(shape)))

    small_parts = jnp.concatenate([_to_lanes(g[n]) for n in _SMALL] + [_to_lanes(g["meta_tokens"])], axis=0)
    n_small_rows = small_parts.shape[0]
    pad = -n_small_rows % _SUBLANES
    small_parts = jnp.concatenate([small_parts, jnp.zeros((pad, _LANES), F32)], axis=0)
    small_sum = _sum_parts("sum_small", _all_gather("gather_small", small_parts))
    off = 0
    for n in _SMALL:
        r = args[n].size // _LANES
        grad[n] = small_sum[off:off + r].reshape(args[n].shape)
        off += r
    me = 4 * lax.axis_index("x") + 2 * lax.axis_index("y") + lax.axis_index("c")
    meta_sum = small_sum[off:off + N_META * d // _LANES].reshape(N_META, d)
    grad["meta_tokens"] = lax.dynamic_slice_in_dim(meta_sum, me * (d // N_DEV), d // N_DEV, axis=1)

    delta, new_m, new_v = {}, {}, {}
    for n in _BIG + ["meta_tokens"]:
        shp = args[n].shape
        two_d = lambda t: t.reshape(shp[-2:])
        res = _adamw("adamw_" + n, two_d(args[n]), two_d(grad[n]), two_d(args["m_" + n]), two_d(args["v_" + n]))
        delta[n], new_m[n], new_v[n] = [t.reshape(shp) for t in res]
        grad[n] = grad[n].reshape(shp)
    cat = lambda prefix: jnp.concatenate(
        [_to_lanes(args[prefix + n]) for n in _SMALL] + [jnp.zeros((pad + N_META * d // _LANES, _LANES), F32)], axis=0)
    res = _adamw("adamw_small", cat(""), small_sum, cat("m_"), cat("v_"))
    off = 0
    for n in _SMALL:
        r = args[n].size // _LANES
        delta[n], new_m[n], new_v[n] = [t[off:off + r].reshape(args[n].shape) for t in res]
        off += r

    return (loss, grad_x, *[grad[n] for n in weight_names], *[delta[n] for n in weight_names],
            *[new_m[n] for n in weight_names], *[new_v[n] for n in weight_names])
```

```python
import functools

import jax
import jax.numpy as jnp
from jax import lax
from jax.experimental import pallas as pl
from jax.experimental.pallas import tpu as pltpu

F32 = jnp.float32
_MXU_DTYPE = jnp.bfloat16
_WIRE_DTYPE = jnp.bfloat16
_VMEM_LIMIT_BYTES = 56 * 1024 * 1024
_LANES = 128
_SUBLANES = 8

N_DEV = 8
N_META = 16
NORM_EPS = 1e-6
HEADS = 8
HEAD_DIM = 128
ROPE_DIM = 64
HG_CHUNK = 16
HG_BLOCK = 128
ROPE_THETA = 10000.0
D_MODEL = 1024
Q_LORA = 256
KV_LORA = 256
FFN_HIDDEN = 2816
ATTN_SCALE = (HEAD_DIM + ROPE_DIM) ** -0.5
NEG_BIG = -1e30

ADAM_LR = 0.001
ADAM_B1 = 0.9
ADAM_B2 = 0.999
ADAM_EPS = 1e-08
ADAM_WD = 0.01
ADAM_STEP = 10

CB_HQ, CB_HF, CB_HI, CB_HG, CB_C, CB_GA, CB_GB = range(7)
IN_COLS_PADDED = 7 * 1024


def _params(**kw):
    return pltpu.CompilerParams(vmem_limit_bytes=_VMEM_LIMIT_BYTES, **kw)


def _dot(a, b):
    return lax.dot_general(a, b, (((1,), (0,)), ((), ())), preferred_element_type=F32)


def _dot_nt(a, b):
    return lax.dot_general(a, b, (((1,), (1,)), ((), ())), preferred_element_type=F32)


def _dot_tn(a, b):
    return lax.dot_general(a, b, (((0,), (0,)), ((), ())), preferred_element_type=F32)


def _mx(x):
    return x.astype(_MXU_DTYPE)


def _exact_dot(m01, x, dot=_dot):
    if _MXU_DTYPE == jnp.float32:
        return dot(m01.astype(F32), x)
    m = m01.astype(jnp.bfloat16)
    x1 = x.astype(jnp.bfloat16)
    r1 = x - x1.astype(F32)
    x2 = r1.astype(jnp.bfloat16)
    x3 = (r1 - x2.astype(F32)).astype(jnp.bfloat16)
    return dot(m, x1) + dot(m, x2) + dot(m, x3)


def _exact_dot_nt(m01, x):
    return _exact_dot(m01, x, dot=_dot_nt)


def _sigmoid(x):
    return jax.nn.sigmoid(x)


def _silu_grad(x, s):
    return s * (1.0 + x * (1.0 - s))


def _rms_scale(x):
    return lax.rsqrt(jnp.mean(x * x, axis=-1, keepdims=True) + NORM_EPS)


def _rms_bwd(x, g, dy):
    r = _rms_scale(x)
    xh = x * r
    w = dy * g
    dx = r * (w - xh * jnp.mean(xh * w, axis=-1, keepdims=True))
    return dx, dy * xh


def _heads(fn, *arrays):
    outs = [fn(*[a[:, h * HEAD_DIM:(h + 1) * HEAD_DIM] for a in arrays]) for h in range(HEADS)]
    if isinstance(outs[0], tuple):
        return tuple(jnp.concatenate([o[i] for o in outs], axis=1) for i in range(len(outs[0])))
    return jnp.concatenate(outs, axis=1)


class _Ride:
    def __init__(self, payloads, gather):
        self.gather, self.args, self.n = gather, list(payloads), len(payloads)
        self.in_specs = [pl.BlockSpec(memory_space=pl.ANY)] * self.n
        self.out_shape = [jax.ShapeDtypeStruct((N_DEV, *p.shape[-2:]), p.dtype) for p in payloads]
        self.out_specs = [pl.BlockSpec(memory_space=pl.ANY)] * self.n
        self.scratch = [pltpu.SemaphoreType.DMA((self.n, N_DEV - 1)), pltpu.SemaphoreType.DMA((self.n, N_DEV - 1)),
                        pltpu.SemaphoreType.DMA((self.n,))]

    def split(self, rest, n_outs):
        n = self.n
        mine = (rest[:n], rest[n + n_outs:2 * n + n_outs], rest[-3:])
        return rest[n:n + n_outs] + rest[2 * n + n_outs:-3], mine

    def _copies(self, p_refs, out_refs, sems):
        send_sems, recv_sems, local_sems = sems
        x, y, c = lax.axis_index("x"), lax.axis_index("y"), lax.axis_index("c")
        me = 4 * x + 2 * y + c
        copies = []
        for i, (p_ref, out_ref) in enumerate(zip(p_refs, out_refs, strict=True)):
            part = (lambda j, p_ref=p_ref: p_ref) if self.gather else (lambda j, p_ref=p_ref: p_ref.at[j])
            copies.append(pltpu.make_async_copy(part(me), out_ref.at[me], local_sems.at[i]))
            for k in range(1, N_DEV):
                px, py, pc = x ^ (k >> 2), y ^ ((k >> 1) & 1), c ^ (k & 1)
                copies.append(pltpu.make_async_remote_copy(
                    src_ref=part(4 * px + 2 * py + pc), dst_ref=out_ref.at[me],
                    send_sem=send_sems.at[i, k - 1], recv_sem=recv_sems.at[i, k - 1],
                    device_id=(px, py, pc), device_id_type=pl.DeviceIdType.MESH))
        return copies

    def run(self, grid, refs):
        ids = [pl.program_id(i) for i in range(len(grid))]
        first = functools.reduce(jnp.logical_and, [i == 0 for i in ids])
        last = functools.reduce(jnp.logical_and, [i == g - 1 for i, g in zip(ids, grid)])

        @pl.when(first)
        def _():
            for cp in self._copies(*refs):
                cp.start()

        @pl.when(last)
        def _():
            for cp in self._copies(*refs):
                cp.wait()


class _NoRide:
    in_specs, out_shape, out_specs, scratch, args = [], [], [], [], []


def _matmul(name, a, b, *, out_dtype, tm, tn, tk, c_in=None, ride=None):
    m, k = a.shape
    _, n = b.shape
    assert m % tm == 0 and n % tn == 0 and k % tk == 0, (name, a.shape, b.shape, tm, tn, tk)
    nk = k // tk
    has_c = c_in is not None
    grid = (n // tn, m // tm, nk)
    n_in = 2 + has_c

    def body(*refs):
        a_ref, b_ref = refs[0], refs[1]
        c_ref = refs[2] if has_c else None
        rest = refs[n_in:]
        if ride is not None:
            rest, exchange = ride.split(rest, 1)
            ride.run(grid, exchange)
        o_ref = rest[0]
        acc_ref = rest[1] if nk > 1 else None

        def finish(r):
            if has_c:
                r = r + c_ref[...]
            o_ref[...] = r.astype(o_ref.dtype)

        if nk == 1:
            finish(_dot(a_ref[...], b_ref[...]))
        else:
            kk = pl.program_id(2)

            @pl.when(kk == 0)
            def _():
                acc_ref[...] = jnp.zeros_like(acc_ref)

            acc_ref[...] += _dot(a_ref[...], b_ref[...])

            @pl.when(kk == nk - 1)
            def _():
                finish(acc_ref[...])

    in_specs = [pl.BlockSpec((tm, tk), lambda j, i, kk: (i, kk)),
                pl.BlockSpec((tk, tn), lambda j, i, kk: (kk, j))]
    args = [a, b]
    aliases = {}
    if has_c:
        in_specs.append(pl.BlockSpec((tm, tn), lambda j, i, kk: (i, j)))
        args.append(c_in)
        aliases = {2: 0}
    out_shape = [jax.ShapeDtypeStruct((m, n), out_dtype)]
    out_specs = [pl.BlockSpec((tm, tn), lambda j, i, kk: (i, j))]
    scratch = [pltpu.VMEM((tm, tn), F32)] if nk > 1 else []
    if ride is not None:
        in_specs, args = in_specs + ride.in_specs, args + ride.args
        out_shape, out_specs, scratch = out_shape + ride.out_shape, out_specs + ride.out_specs, scratch + ride.scratch
    res = pl.pallas_call(
        body, name=name, out_shape=out_shape, grid=grid, in_specs=in_specs, out_specs=out_specs,
        scratch_shapes=scratch, input_output_aliases=aliases, compiler_params=_params(),
    )(*args)
    return res[0] if ride is None else res


def _matmul_tn(name, x, dy, *, tk, tn, tr, out_dtype=F32):
    r, k = x.shape
    _, n = dy.shape
    assert r % tr == 0 and k % tk == 0 and n % tn == 0, (name, x.shape, dy.shape)
    n_r = r // tr
    direct = out_dtype == F32

    def body(x_ref, dy_ref, o_ref, *scratch):
        acc_ref = o_ref if direct else scratch[0]

        @pl.when(pl.program_id(2) == 0)
        def _():
            acc_ref[...] = jnp.zeros_like(acc_ref)

        acc_ref[...] += _dot_tn(x_ref[...], dy_ref[...])
        if not direct:
            @pl.when(pl.program_id(2) == n_r - 1)
            def _():
                o_ref[...] = acc_ref[...].astype(o_ref.dtype)

    return pl.pallas_call(
        body, name=name,
        out_shape=jax.ShapeDtypeStruct((k, n), out_dtype),
        grid=(k // tk, n // tn, n_r),
        in_specs=[pl.BlockSpec((tr, tk), lambda kb, nb, rr: (rr, kb)),
                  pl.BlockSpec((tr, tn), lambda kb, nb, rr: (rr, nb))],
        out_specs=pl.BlockSpec((tk, tn), lambda kb, nb, rr: (kb, nb)),
        scratch_shapes=[] if direct else [pltpu.VMEM((tk, tn), F32)],
        compiler_params=_params(),
    )(x, dy)


def _rowwise(name, body, *, rows, tr, lp, ins, outs, accs=()):
    assert rows % tr == 0 and lp % tr == 0 and tr % 16 == 0
    tiles_per_example = lp // tr
    in_specs, arrays = [], []
    for spec in ins:
        if spec[0] == "row":
            _, arr, width, cb = spec
            in_specs.append(pl.BlockSpec((tr, width), functools.partial(lambda i, cb: (i, cb), cb=cb)))
        elif spec[0] == "const":
            arr = spec[1]
            in_specs.append(pl.BlockSpec(arr.shape, lambda i: (0, 0)))
        else:
            arr = spec[1]
            in_specs.append(pl.BlockSpec((tr, arr.shape[1]), lambda i: (i % tiles_per_example, 0)))
        arrays.append(arr)
    n_in, n_out = len(ins), len(outs)

    def kern(*refs):
        res_outs, res_accs = body(*[r[...] for r in refs[:n_in]])
        for ref, val in zip(refs[n_in:n_in + n_out], res_outs, strict=True):
            ref[...] = val.astype(ref.dtype)
        acc_refs = refs[n_in + n_out:]
        if acc_refs:
            @pl.when(pl.program_id(0) == 0)
            def _():
                for ref in acc_refs:
                    ref[...] = jnp.zeros_like(ref)

            for ref, val in zip(acc_refs, res_accs, strict=True):
                ref[...] += val.reshape(tr // _SUBLANES, _SUBLANES, val.shape[-1]).sum(axis=0)

    out_shape = ([jax.ShapeDtypeStruct((rows, w), dt) for w, dt in outs]
                 + [jax.ShapeDtypeStruct((_SUBLANES, w), F32) for w in accs])
    out_specs = ([pl.BlockSpec((tr, w), lambda i: (i, 0)) for w, _ in outs]
                 + [pl.BlockSpec((_SUBLANES, w), lambda i: (0, 0)) for w in accs])
    res = pl.pallas_call(
        kern, name=name, out_shape=out_shape, grid=(rows // tr,),
        in_specs=in_specs, out_specs=out_specs, compiler_params=_params(),
    )(*arrays)
    return res[:n_out], list(res[n_out:])


def _assemble(name, x, head_rows, lp):
    batch, seq, d = x.shape
    tc = 256

    def body(x_ref, m_ref, o_ref):
        o_ref[0:N_META, :] = m_ref[...]
        o_ref[N_META:N_META + seq, :] = x_ref[0]
        if lp > N_META + seq:
            o_ref[N_META + seq:, :] = jnp.zeros((lp - N_META - seq, tc), F32)

    return pl.pallas_call(
        body, name=name,
        out_shape=jax.ShapeDtypeStruct((batch * lp, d), F32),
        grid=(batch, d // tc),
        in_specs=[pl.BlockSpec((1, seq, tc), lambda b, j: (b, 0, j)),
                  pl.BlockSpec((N_META, tc), lambda b, j: (0, j))],
        out_specs=pl.BlockSpec((lp, tc), lambda b, j: (b, j)),
        compiler_params=_params(),
    )(x, head_rows)


def _meta_grad(dh0, batch, lp):
    d = dh0.shape[1]

    def body(g_ref, o_ref):
        @pl.when(pl.program_id(0) == 0)
        def _():
            o_ref[...] = jnp.zeros_like(o_ref)

        o_ref[...] += g_ref[...]

    return pl.pallas_call(
        body, name="meta_grad",
        out_shape=jax.ShapeDtypeStruct((N_META, d), F32),
        grid=(batch,),
        in_specs=[pl.BlockSpec((N_META, d), lambda b: (b * (lp // N_META), 0))],
        out_specs=pl.BlockSpec((N_META, d), lambda b: (0, 0)),
        compiler_params=_params(),
    )(dh0)


def _segment_masks():
    t = lax.broadcasted_iota(jnp.int32, (HG_BLOCK, HG_BLOCK), 0)
    s = lax.broadcasted_iota(jnp.int32, (HG_BLOCK, HG_BLOCK), 1)
    same = lax.shift_right_logical(t, 4) == lax.shift_right_logical(s, 4)
    lower = same & (s <= t)
    upper = same & (s >= t)
    first_half = same & ((s & 15) <= 7)
    return same, lower, upper, first_half


def _hgrn_gates(hq, hf, lb):
    sq = _sigmoid(hq)
    q = hq * sq
    sf = _sigmoid(hf)
    f = lb + (1.0 - lb) * sf
    return q, sq, sf, f


def _hgrn_decays(g, same, lower, first_half):
    b = _exact_dot(lower, g)
    b_last = _exact_dot(same, g)
    b_ref = _exact_dot(first_half, g)
    return b, b_last, b_ref


def _hgrn_fwd(p, lb, gh, *, batch, lp, ride=None):
    rows = batch * lp
    nb = lp // HG_BLOCK
    n_chunks = HG_BLOCK // HG_CHUNK

    def body(hq_ref, hf_ref, hi_ref, hg_ref, lb_ref, gh_ref, *rest):
        if ride is not None:
            rest, exchange = ride.split(rest, 3)
            ride.run((batch, nb), exchange)
        o_ref, z_ref, st_ref, s_scr, qt_scr, kh_scr, v_scr, el_scr, o_scr = rest

        @pl.when(pl.program_id(1) == 0)
        def _():
            s_scr[...] = jnp.zeros_like(s_scr)

        same, lower, _, first_half = _segment_masks()
        v = hi_ref[...]
        q, _, _, f = _hgrn_gates(hq_ref[...], hf_ref[...], lb_ref[...])
        k = 1.0 - f
        b, b_last, b_ref = _hgrn_decays(jnp.log(f), same, lower, first_half)
        qt_scr[...] = _mx(q * jnp.exp(b))
        kh_scr[...] = _mx(k * jnp.exp(b_last - b))
        v_scr[...] = _mx(v)
        el_scr[...] = jnp.exp(b_last)
        qc = _mx(q * jnp.exp(b - b_ref))
        kc = _mx(k * jnp.exp(b_ref - b))

        def intra(qc_h, kc_h, v_h):
            a = jnp.where(lower, _dot_nt(qc_h, kc_h), 0.0)
            return _dot(_mx(a), v_h)

        o_scr[...] = _heads(intra, qc, kc, _mx(v))

        def chunk(c, carry):
            r0 = pl.multiple_of(c * HG_CHUNK, HG_CHUNK)
            rs = pl.ds(r0, HG_CHUNK)
            for h in range(HEADS):
                cs = slice(h * HEAD_DIM, (h + 1) * HEAD_DIM)
                st = s_scr[h]
                st_ref[c, h] = st.astype(st_ref.dtype)
                o_scr[rs, cs] += _dot_nt(qt_scr[rs, cs], _mx(st))
                s_scr[h] = st * el_scr[pl.ds(r0, 1), cs] + _dot_tn(v_scr[rs, cs], kh_scr[rs, cs])
            return carry

        lax.fori_loop(0, n_chunks, chunk, 0)

        o = o_scr[...]
        o_ref[...] = o
        hg = hg_ref[...]
        n = _heads(lambda o_h: o_h * _rms_scale(o_h), o) * gh_ref[...]
        z_ref[...] = (n * hg * _sigmoid(hg)).astype(z_ref.dtype)

    blk = lambda cb: pl.BlockSpec((HG_BLOCK, D_MODEL), functools.partial(lambda b, j, cb: (b * nb + j, cb), cb=cb))
    row_out = pl.BlockSpec((HG_BLOCK, D_MODEL), lambda b, j: (b * nb + j, 0))
    const = pl.BlockSpec((1, D_MODEL), lambda b, j: (0, 0))
    extra = ride if ride is not None else _NoRide
    return pl.pallas_call(
        body, name="hgrn_fwd",
        out_shape=[jax.ShapeDtypeStruct((rows, D_MODEL), F32),
                   jax.ShapeDtypeStruct((rows, D_MODEL), _MXU_DTYPE),
                   jax.ShapeDtypeStruct((rows // HG_CHUNK, HEADS, HEAD_DIM, HEAD_DIM), _MXU_DTYPE)] + extra.out_shape,
        grid=(batch, nb),
        in_specs=[blk(CB_HQ), blk(CB_HF), blk(CB_HI), blk(CB_HG), const, const] + extra.in_specs,
        out_specs=[row_out, row_out,
                   pl.BlockSpec((n_chunks, HEADS, HEAD_DIM, HEAD_DIM), lambda b, j: (b * nb + j, 0, 0, 0))]
        + extra.out_specs,
        scratch_shapes=[pltpu.VMEM((HEADS, HEAD_DIM, HEAD_DIM), F32),
                        pltpu.VMEM((HG_BLOCK, D_MODEL), _MXU_DTYPE),
                        pltpu.VMEM((HG_BLOCK, D_MODEL), _MXU_DTYPE),
                        pltpu.VMEM((HG_BLOCK, D_MODEL), _MXU_DTYPE),
                        pltpu.VMEM((HG_BLOCK, D_MODEL), F32),
                        pltpu.VMEM((HG_BLOCK, D_MODEL), F32)] + extra.scratch,
        compiler_params=_params(),
    )(p, p, p, p, lb, gh, *extra.args)


def _hgrn_bwd(p, o, dz, states, lb, gh, *, batch, lp, ride=None):
    rows = batch * lp
    nb = lp // HG_BLOCK
    n_chunks = HG_BLOCK // HG_CHUNK

    def body(hq_ref, hf_ref, hi_ref, hg_ref, o_ref, dz_ref, st_ref, lb_ref, gh_ref, *rest):
        if ride is not None:
            rest, exchange = ride.split(rest, 3)
            ride.run((batch, nb), exchange)
        (dp_ref, dlb_ref, dgh_ref,
         ds_scr, qt_scr, kh_scr, v_scr, do_scr, el_scr, dqt_scr, dkh_scr, dv_scr, dbl_scr) = rest
        first = (pl.program_id(0) == 0) & (pl.program_id(1) == 0)

        @pl.when(first)
        def _():
            dlb_ref[...] = jnp.zeros_like(dlb_ref)
            dgh_ref[...] = jnp.zeros_like(dgh_ref)

        @pl.when(pl.program_id(1) == 0)
        def _():
            ds_scr[...] = jnp.zeros_like(ds_scr)

        same, lower, upper, first_half = _segment_masks()
        lbv = lb_ref[...]
        hq, hf, v, hg = hq_ref[...], hf_ref[...], hi_ref[...], hg_ref[...]
        q, sq, sf, f = _hgrn_gates(hq, hf, lbv)
        k = 1.0 - f
        b, b_last, b_ref = _hgrn_decays(jnp.log(f), same, lower, first_half)
        e_b = jnp.exp(b)
        e_kh = jnp.exp(b_last - b)
        e_qc = jnp.exp(b - b_ref)
        e_kc = jnp.exp(b_ref - b)
        qt, kh, qc, kc = q * e_b, k * e_kh, q * e_qc, k * e_kc

        o = o_ref[...]
        dz = dz_ref[...].astype(F32)
        ghv = gh_ref[...]
        sg = _sigmoid(hg)
        r = _heads(lambda o_h: jnp.broadcast_to(_rms_scale(o_h), o_h.shape), o)
        oh = o * r
        dn = dz * hg * sg
        dhg = dz * oh * ghv * _silu_grad(hg, sg)
        w = dn * ghv
        do = r * (w - oh * _heads(lambda t: jnp.broadcast_to(jnp.mean(t, axis=-1, keepdims=True), t.shape), oh * w))
        dgh_ref[...] += (dn * oh).reshape(HG_BLOCK // _SUBLANES, _SUBLANES, D_MODEL).sum(axis=0)

        qt_scr[...] = _mx(qt)
        kh_scr[...] = _mx(kh)
        v_scr[...] = _mx(v)
        do_scr[...] = _mx(do)
        el_scr[...] = jnp.exp(b_last)

        def intra(qc_h, kc_h, v_h, do_h):
            a = _mx(jnp.where(lower, _dot_nt(qc_h, kc_h), 0.0))
            da = _mx(jnp.where(lower, _dot_nt(do_h, v_h), 0.0))
            return _dot(da, kc_h), _dot_tn(da, qc_h), _dot_tn(a, do_h)

        dqc, dkc, dv_intra = _heads(intra, _mx(qc), _mx(kc), _mx(v), _mx(do))
        dv_scr[...] = dv_intra

        def chunk(i, carry):
            c = n_chunks - 1 - i
            r0 = pl.multiple_of(c * HG_CHUNK, HG_CHUNK)
            rs = pl.ds(r0, HG_CHUNK)
            for h in range(HEADS):
                cs = slice(h * HEAD_DIM, (h + 1) * HEAD_DIM)
                st = st_ref[c, h]
                ds_t = ds_scr[h]
                ds_m = _mx(ds_t)
                el = el_scr[pl.ds(r0, 1), cs]
                dkh_scr[rs, cs] = _dot(v_scr[rs, cs], ds_m)
                dv_scr[rs, cs] += _dot_nt(kh_scr[rs, cs], ds_m)
                dbl = jnp.sum(ds_t * st.astype(F32), axis=0, keepdims=True) * el
                dbl_scr[rs, cs] = jnp.broadcast_to(dbl, (HG_CHUNK, HEAD_DIM))
                dqt_scr[rs, cs] = _dot(do_scr[rs, cs], st)
                ds_scr[h] = ds_t * el + _dot_tn(do_scr[rs, cs], qt_scr[rs, cs])
            return carry

        lax.fori_loop(0, n_chunks, chunk, 0)

        dqt, dkh = dqt_scr[...], dkh_scr[...]
        dq = dqt * e_b + dqc * e_qc
        dk = dkh * e_kh + dkc * e_kc
        t_kh = dkh * kh
        db_rows = dqt * qt + dqc * qc - dkc * kc - t_kh
        dg = _exact_dot(upper, db_rows) + _exact_dot(same, t_kh) + dbl_scr[...]
        df = dg / f - dk
        dhf = df * (1.0 - lbv) * sf * (1.0 - sf)
        dlb_ref[...] += (df * (1.0 - sf)).reshape(HG_BLOCK // _SUBLANES, _SUBLANES, D_MODEL).sum(axis=0)
        dhq = dq * _silu_grad(hq, sq)
        dp_ref[...] = jnp.concatenate([dhq, dhf, dv_scr[...], dhg], axis=1).astype(dp_ref.dtype)

    rev = lambda b, j: b * nb + (nb - 1 - j)
    blk = lambda cb: pl.BlockSpec((HG_BLOCK, D_MODEL), functools.partial(lambda b, j, cb: (rev(b, j), cb), cb=cb))
    row = pl.BlockSpec((HG_BLOCK, D_MODEL), lambda b, j: (rev(b, j), 0))
    const = pl.BlockSpec((1, D_MODEL), lambda b, j: (0, 0))
    acc = pl.BlockSpec((_SUBLANES, D_MODEL), lambda b, j: (0, 0))
    big = lambda dt: pltpu.VMEM((HG_BLOCK, D_MODEL), dt)
    extra = ride if ride is not None else _NoRide
    dp, dlb, dgh, *exchanged = pl.pallas_call(
        body, name="hgrn_bwd",
        out_shape=[jax.ShapeDtypeStruct((rows, 4 * D_MODEL), _MXU_DTYPE),
                   jax.ShapeDtypeStruct((_SUBLANES, D_MODEL), F32),
                   jax.ShapeDtypeStruct((_SUBLANES, D_MODEL), F32)] + extra.out_shape,
        grid=(batch, nb),
        in_specs=[blk(CB_HQ), blk(CB_HF), blk(CB_HI), blk(CB_HG), row, row,
                  pl.BlockSpec((n_chunks, HEADS, HEAD_DIM, HEAD_DIM), lambda b, j: (rev(b, j), 0, 0, 0)),
                  const, const] + extra.in_specs,
        out_specs=[pl.BlockSpec((HG_BLOCK, 4 * D_MODEL), lambda b, j: (rev(b, j), 0)), acc, acc] + extra.out_specs,
        scratch_shapes=[pltpu.VMEM((HEADS, HEAD_DIM, HEAD_DIM), F32),
                        big(_MXU_DTYPE), big(_MXU_DTYPE), big(_MXU_DTYPE), big(_MXU_DTYPE),
                        big(F32), big(F32), big(F32), big(F32), big(F32)] + extra.scratch,
        compiler_params=_params(),
    )(p, p, p, p, o, dz, states, lb, gh, *extra.args)
    return (dp, dlb, dgh, *exchanged)


QK_DIM = 2 * HEAD_DIM
ATTN_TQ = 256
ATTN_KEY_CHUNK = 512


def _query_tiles(lp):
    return [(r0, min(ATTN_TQ, lp - r0)) for r0 in range(0, lp, ATTN_TQ)]


def _attn_fwd(q_cat, kv, kp, *, batch, lp):
    rows = batch * lp

    def body(q_ref, kn_ref, kp_ref, v_ref, o_ref, lse_ref):
        k_cat = jnp.concatenate([kn_ref[...], kp_ref[...]], axis=1)
        for r0, tq in _query_tiles(lp):
            q_t = q_ref[r0:r0 + tq, :]
            i = lax.broadcasted_iota(jnp.int32, (tq, tq), 0)
            j = lax.broadcasted_iota(jnp.int32, (tq, tq), 1)
            s_diag = jnp.where(j <= i, _dot_nt(q_t, k_cat[r0:r0 + tq]) * ATTN_SCALE, NEG_BIG)
            m = jnp.max(s_diag, axis=1, keepdims=True)
            if r0:
                s_past = _dot_nt(q_t, k_cat[0:r0]) * ATTN_SCALE
                m = jnp.maximum(m, jnp.max(s_past, axis=1, keepdims=True))
            p_diag = jnp.exp(s_diag - m)
            l = jnp.sum(p_diag, axis=1, keepdims=True)
            acc = _dot(_mx(p_diag), v_ref[r0:r0 + tq, :])
            if r0:
                p_past = jnp.exp(s_past - m)
                l = l + jnp.sum(p_past, axis=1, keepdims=True)
                acc = acc + _dot(_mx(p_past), v_ref[0:r0, :])
            o_ref[r0:r0 + tq, :] = (acc / l).astype(o_ref.dtype)
            lse_ref[r0:r0 + tq, :] = jnp.broadcast_to(m + jnp.log(l), (tq, HEAD_DIM))

    head_blk = pl.BlockSpec((lp, HEAD_DIM), lambda b, h: (b, h))
    return pl.pallas_call(
        body, name="attn_fwd",
        out_shape=[jax.ShapeDtypeStruct((rows, D_MODEL), _MXU_DTYPE),
                   jax.ShapeDtypeStruct((rows, D_MODEL), F32)],
        grid=(batch, HEADS),
        in_specs=[pl.BlockSpec((lp, QK_DIM), lambda b, h: (b, h)), head_blk,
                  pl.BlockSpec((lp, HEAD_DIM), lambda b, h: (b, 0)),
                  pl.BlockSpec((lp, HEAD_DIM), lambda b, h: (b, HEADS + h))],
        out_specs=[head_blk, head_blk],
        compiler_params=_params(),
    )(q_cat, kv, kp, kv)


def _attn_bwd(q_cat, kv, kp, do, o, lse, *, batch, lp, ride=None):
    rows = batch * lp

    def body(q_ref, kn_ref, kp_ref, v_ref, do_ref, o_ref, lse_ref, *rest):
        if ride is not None:
            rest, exchange = ride.split(rest, 4)
            ride.run((batch, HEADS), exchange)
        dq_ref, dkn_ref, dkp_ref, dv_ref, dk_acc, dv_acc = rest
        dk_acc[...] = jnp.zeros_like(dk_acc)
        dv_acc[...] = jnp.zeros_like(dv_acc)
        k_cat = jnp.concatenate([kn_ref[...], kp_ref[...]], axis=1)
        k_t = k_cat.T
        lane = lax.broadcasted_iota(jnp.int32, (_SUBLANES, HEAD_DIM), 1)
        lse_row = _exact_dot_nt(lane == 0, lse_ref[...])
        delta = _exact_dot_nt(lane >= 0, do_ref[...].astype(F32) * o_ref[...].astype(F32))
        for r0, tq in _query_tiles(lp):
            cols = slice(r0, r0 + tq)
            q_t_, do_t_ = q_ref[cols, :], do_ref[cols, :]
            lse_t, delta_t = lse_row[0:1, cols], delta[0:1, cols]
            chunks = [(c0, min(ATTN_KEY_CHUNK, r0 - c0), False) for c0 in range(0, r0, ATTN_KEY_CHUNK)] + [(r0, tq, True)]
            dq_t = jnp.zeros((QK_DIM, tq), F32)
            for c0, n, diagonal in chunks:
                keys = slice(c0, c0 + n)
                s = _dot_nt(k_cat[keys], q_t_) * ATTN_SCALE
                if diagonal:
                    jk = lax.broadcasted_iota(jnp.int32, (n, tq), 0)
                    iq = lax.broadcasted_iota(jnp.int32, (n, tq), 1)
                    s = jnp.where(jk <= iq, s, NEG_BIG)
                pexp = jnp.exp(s - lse_t)
                dp = _dot_nt(v_ref[keys, :], do_t_)
                ds = _mx(pexp * (dp - delta_t) * ATTN_SCALE)
                dk_acc[keys, :] += _dot(ds, q_t_)
                dv_acc[keys, :] += _dot(_mx(pexp), do_t_)
                dq_t = dq_t + _dot(k_t[:, keys], ds)
            dq_ref[cols, :] = dq_t.T

        dkn_ref[...] = dk_acc[:, 0:HEAD_DIM].astype(dkn_ref.dtype)
        dv_ref[...] = dv_acc[...].astype(dv_ref.dtype)

        @pl.when(pl.program_id(1) == 0)
        def _():
            dkp_ref[...] = jnp.zeros_like(dkp_ref)

        dkp_ref[...] += dk_acc[:, HEAD_DIM:]

    head_blk = pl.BlockSpec((lp, HEAD_DIM), lambda b, h: (b, h))
    cat_blk = pl.BlockSpec((lp, QK_DIM), lambda b, h: (b, h))
    shared_blk = pl.BlockSpec((lp, HEAD_DIM), lambda b, h: (b, 0))
    extra = ride if ride is not None else _NoRide
    return pl.pallas_call(
        body, name="attn_bwd",
        out_shape=[jax.ShapeDtypeStruct((rows, HEADS * QK_DIM), F32),
                   jax.ShapeDtypeStruct((rows, D_MODEL), _MXU_DTYPE),
                   jax.ShapeDtypeStruct((rows, HEAD_DIM), F32),
                   jax.ShapeDtypeStruct((rows, D_MODEL), _MXU_DTYPE)] + extra.out_shape,
        grid=(batch, HEADS),
        in_specs=[cat_blk, head_blk, shared_blk, pl.BlockSpec((lp, HEAD_DIM), lambda b, h: (b, HEADS + h)),
                  head_blk, head_blk, head_blk] + extra.in_specs,
        out_specs=[cat_blk, head_blk, shared_blk, head_blk] + extra.out_specs,
        scratch_shapes=[pltpu.VMEM((lp, QK_DIM), F32), pltpu.VMEM((lp, HEAD_DIM), F32)] + extra.scratch,
        compiler_params=_params(),
    )(q_cat, kv, kp, kv, do, o, lse, *extra.args)


def _all_gather(name, blocks):
    n = len(blocks)

    def body(*refs):
        x_refs, out_refs, (send_sems, recv_sems, local_sems) = refs[:n], refs[n:2 * n], refs[2 * n:]
        x, y, c = lax.axis_index("x"), lax.axis_index("y"), lax.axis_index("c")
        me, sibling = (x, y, c), (x, y, 1 - c)
        chips = [(1 - x, y), (x, 1 - y), (1 - x, 1 - y)]

        def slot(i, px, py, pc):
            return out_refs[i].at[4 * px + 2 * py + pc]

        def copy(i, k, blk, to, src=None):
            return pltpu.make_async_remote_copy(
                src_ref=slot(i, *blk) if src is None else src, dst_ref=slot(i, *blk),
                send_sem=send_sems.at[i, k], recv_sem=recv_sems.at[i, k],
                device_id=to, device_id_type=pl.DeviceIdType.MESH)

        mine = [pltpu.make_async_copy(x_refs[i], slot(i, *me), local_sems.at[i]) for i in range(n)]
        first = [copy(i, 0, me, sibling, src=x_refs[i]) for i in range(n)]
        first += [copy(i, 1 + j, me, (*chip, c), src=x_refs[i]) for i in range(n) for j, chip in enumerate(chips)]
        for cp in mine + first:
            cp.start()
        passed = []
        for i in range(n):
            for j, chip in enumerate(chips):
                copy(i, 1 + j, (*chip, c), me).wait_recv()
                passed.append(copy(i, 4 + j, (*chip, c), sibling))
                passed[-1].start()
        for i in range(n):
            copy(i, 0, sibling, me).wait_recv()
            for j, chip in enumerate(chips):
                copy(i, 4 + j, (*chip, 1 - c), me).wait_recv()
        for cp in first + passed:
            cp.wait_send()
        for cp in mine:
            cp.wait()

    return pl.pallas_call(
        body, name=name,
        out_shape=[jax.ShapeDtypeStruct((N_DEV, *b.shape), b.dtype) for b in blocks],
        in_specs=[pl.BlockSpec(memory_space=pl.ANY)] * n,
        out_specs=[pl.BlockSpec(memory_space=pl.ANY)] * n,
        scratch_shapes=[pltpu.SemaphoreType.DMA((n, 7)), pltpu.SemaphoreType.DMA((n, 7)),
                        pltpu.SemaphoreType.DMA((n,))],
    )(*blocks)


def _adamw_math(w, g, m, v):
    nm = ADAM_B1 * m + (1.0 - ADAM_B1) * g
    nv = ADAM_B2 * v + (1.0 - ADAM_B2) * (g * g)
    m_hat = nm / (1.0 - ADAM_B1 ** ADAM_STEP)
    v_hat = nv / (1.0 - ADAM_B2 ** ADAM_STEP)
    return -ADAM_LR * (m_hat / (jnp.sqrt(v_hat) + ADAM_EPS) + ADAM_WD * w), nm, nv


def _sum_adamw(name, parts, w, m, v):
    rows, cols = w.shape
    tr = rows // 4 if rows % 64 == 0 and rows * cols > (1 << 16) else rows

    def body(p_ref, w_ref, m_ref, v_ref, g_ref, d_ref, nm_ref, nv_ref):
        g = p_ref[0].astype(F32)
        for dev in range(1, N_DEV):
            g = g + p_ref[dev].astype(F32)
        g_ref[...] = g
        d_ref[...], nm_ref[...], nv_ref[...] = _adamw_math(w_ref[...], g, m_ref[...], v_ref[...])

    spec = pl.BlockSpec((tr, cols), lambda i: (i, 0))
    return pl.pallas_call(
        body, name=name,
        out_shape=[jax.ShapeDtypeStruct((rows, cols), F32)] * 4,
        grid=(rows // tr,),
        in_specs=[pl.BlockSpec((N_DEV, tr, cols), lambda i: (0, i, 0))] + [spec] * 3, out_specs=[spec] * 4,
        compiler_params=_params(),
    )(parts, w, m, v)


def _finish_vectors(gathered, lb, params):
    names = list(params)
    n = len(names)

    def body(*refs):
        g_refs, lb_ref = refs[:n], refs[n]
        wmv_refs = refs[n + 1:4 * n + 1]
        out_refs = refs[4 * n + 1:]
        me = 4 * lax.axis_index("x") + 2 * lax.axis_index("y") + lax.axis_index("c")
        for i, name in enumerate(names):
            g_ref = g_refs[i]
            w_ref, m_ref, v_ref = wmv_refs[3 * i:3 * i + 3]
            if name == "meta_tokens":
                width = w_ref.shape[1]
                mine = pl.ds(pl.multiple_of(me * width, width), width)
                g = g_ref[0, :, mine]
                for dev in range(1, N_DEV):
                    g = g + g_ref[dev, :, mine]
            else:
                g = g_ref[0]
                for dev in range(1, N_DEV):
                    g = g + g_ref[dev]
                g = jnp.sum(g, axis=0, keepdims=True)
                if name == "hg_norm_g":
                    g = functools.reduce(jnp.add, [g[:, h * HEAD_DIM:(h + 1) * HEAD_DIM] for h in range(HEADS)])
                if name == "lb_logits":
                    lbv = lb_ref[...]
                    g = g * lbv * (1.0 - lbv)
                    g = jnp.concatenate([g, -g], axis=0)
            outs = (g, *_adamw_math(w_ref[...], g, m_ref[...], v_ref[...]))
            for ref, val in zip(out_refs[4 * i:4 * i + 4], outs, strict=True):
                ref[...] = val

    args = [gathered[k] for k in names] + [lb] + [t for k in names for t in params[k]]
    res = pl.pallas_call(
        body, name="finish_vectors",
        out_shape=[jax.ShapeDtypeStruct(params[k][0].shape, F32) for k in names for _ in range(4)],
        compiler_params=_params(),
    )(*args)
    return {k: res[4 * i:4 * i + 4] for i, k in enumerate(names)}


def _swap_halves(t):
    half = t.shape[-1] // 2
    return jnp.concatenate([t[..., half:], t[..., :half]], axis=-1)


def _pad_last(t, width):
    return jnp.concatenate([t, jnp.zeros(t.shape[:-1] + (width - t.shape[-1],), t.dtype)], axis=-1)


def _rope_tables(lp):
    pos = jnp.arange(lp, dtype=F32)
    inv_freq = 1.0 / (ROPE_THETA ** (jnp.arange(0, ROPE_DIM, 2, dtype=F32) / ROPE_DIM))
    ang = pos[:, None] * inv_freq[None, :]
    cos, sin = jnp.cos(ang), jnp.sin(ang)
    c128 = _pad_last(jnp.concatenate([cos, cos], axis=1), HEAD_DIM)
    s128 = _pad_last(jnp.concatenate([-sin, sin], axis=1), HEAD_DIM)
    return c128, s128


def _forward_backward(x, target, meta, w, small, *, lp, comm=None):
    batch, seq, d = x.shape
    rows = batch * lp
    tr = 272 if lp % 272 == 0 else 128
    tm = lp // 2
    bf = _MXU_DTYPE
    rw = functools.partial(_rowwise, rows=rows, tr=tr, lp=lp)

    c128, s128 = _rope_tables(lp)
    cq_tab, sq_tab = jnp.tile(c128, (1, HEADS)), jnp.tile(s128, (1, HEADS))
    t_idx = jnp.arange(lp)
    real = jnp.broadcast_to(((t_idx >= N_META) & (t_idx < N_META + seq)).astype(F32)[:, None], (lp, _LANES))

    lb_logits = small["lb_logits"]
    lb = jax.nn.softmax(lb_logits, axis=0)[0:1]
    gh = jnp.tile(small["hg_norm_g"], (1, HEADS))

    h0 = _assemble("assemble_x", x, meta, lp)
    tgt = _assemble("assemble_target", target, jnp.zeros_like(meta), lp)

    (u1,), _ = rw("norm_mix_pre", lambda h, g: ([h * _rms_scale(h) * g], []),
                  ins=[("row", h0, d, 0), ("const", small["mix_pre_g"])], outs=[(d, bf)])
    p = _matmul("proj_in", u1, w["w_in"], out_dtype=F32, tm=tm, tn=1024, tk=1024)

    if comm is None:
        o_hg, z_a, states = _hgrn_fwd(p, lb, gh, batch=batch, lp=lp)
    else:
        o_hg, z_a, states, *gathered = _hgrn_fwd(p, lb, gh, batch=batch, lp=lp, ride=_Ride(comm.rest_payloads, True))
        w = {**w, **comm.rest_weights(gathered)}
    received = []
    scatter = lambda names: _Ride(comm.grad_parts(names, grads), False) if comm is not None else None
    y_a = _matmul("proj_hg_o", z_a, w["w_hg_o"], out_dtype=F32, tm=tm, tn=1024, tk=1024)

    def mla_pre(pc, gq, gkv, ct, st):
        cq, ckv = pc[:, 0:Q_LORA], pc[:, Q_LORA:Q_LORA + KV_LORA]
        kpe, kpe_sw = pc[:, 512:640], pc[:, 640:768]
        return [cq * _rms_scale(cq) * gq, ckv * _rms_scale(ckv) * gkv, kpe * ct + kpe_sw * st], []

    (cqn, ckvn, kp), _ = rw("mla_pre", mla_pre,
                            ins=[("row", p, 1024, CB_C), ("const", small["q_a_norm_g"]),
                                 ("const", small["kv_a_norm_g"]), ("pos", c128), ("pos", s128)],
                            outs=[(Q_LORA, bf), (KV_LORA, bf), (HEAD_DIM, bf)])
    qf = _matmul("proj_q_b", cqn, w["w_q"], out_dtype=F32, tm=tm, tn=1024, tk=Q_LORA)
    def rope_q(a, pe, pes, ct, st):
        roped = pe * ct + pes * st
        hs = lambda t, h: t[:, h * HEAD_DIM:(h + 1) * HEAD_DIM]
        return [jnp.concatenate([t for h in range(HEADS) for t in (hs(a, h), hs(roped, h))], axis=1)], []

    (q_cat,), _ = rw("rope_q", rope_q,
                     ins=[("row", qf, 1024, 0), ("row", qf, 1024, 1), ("row", qf, 1024, 2),
                          ("pos", cq_tab), ("pos", sq_tab)], outs=[(HEADS * QK_DIM, bf)])
    kv = _matmul("proj_kv_b", ckvn, w["w_kv"], out_dtype=bf, tm=tm, tn=1024, tk=KV_LORA)
    o_at, lse = _attn_fwd(q_cat, kv, kp, batch=batch, lp=lp)
    y_b = _matmul("proj_mla_o", o_at, w["w_mla_o"], out_dtype=F32, tm=tm, tn=1024, tk=1024)

    def merge(pa, pb, ya, yb, bg):
        ga, gb = _sigmoid(pa + bg[:, :d]), _sigmoid(pb + bg[:, d:])
        return [ga * ya + gb * yb], []

    (mix,), _ = rw("merge", merge,
                   ins=[("row", p, 1024, CB_GA), ("row", p, 1024, CB_GB), ("row", y_a, d, 0), ("row", y_b, d, 0),
                        ("const", small["b_gate"])], outs=[(d, bf)])
    mixed = _matmul("proj_out", mix, w["w_out"], out_dtype=F32, tm=tm, tn=1024, tk=1024)

    def post_mix(mx_, h, g2, g3):
        h1_ = h + mx_ * _rms_scale(mx_) * g2
        return [h1_, h1_ * _rms_scale(h1_) * g3], []

    (h1, u2), _ = rw("post_mix", post_mix,
                     ins=[("row", mixed, d, 0), ("row", h0, d, 0), ("const", small["mix_post_g"]),
                          ("const", small["ffn_pre_g"])], outs=[(d, F32), (d, bf)])
    gu = _matmul("ffn_in", u2, w["w_ffn_in"], out_dtype=F32, tm=tm, tn=1408, tk=1024)
    (act,), _ = rw("swiglu", lambda gt, up: ([gt * _sigmoid(gt) * up], []),
                   ins=[("row", gu, FFN_HIDDEN, 0), ("row", gu, FFN_HIDDEN, 1)], outs=[(FFN_HIDDEN, bf)])
    fo = _matmul("ffn_out", act, w["w_ffn_out"], out_dtype=F32, tm=tm, tn=1024, tk=1408)

    def post_ffn(fo_, h1_, t_, mask, g4):
        r = _rms_scale(fo_)
        h2 = h1_ + fo_ * r * g4
        err = (h2 - t_) * mask[:, 0:1]
        dh2 = err * (1.0 / d)
        dfo, dg4 = _rms_bwd(fo_, g4, dh2)
        return [dh2, dfo], [err * err, dg4]

    (dh2, dfo), (loss_vec, dg_ffn_post) = rw(
        "post_ffn_loss", post_ffn,
        ins=[("row", fo, d, 0), ("row", h1, d, 0), ("row", tgt, d, 0), ("pos", real), ("const", small["ffn_post_g"])],
        outs=[(d, F32), (d, bf)], accs=[d, d])
    loss = (0.5 / d) * jnp.sum(loss_vec)

    grads = {}
    dw_dt = F32 if comm is None else _WIRE_DTYPE
    d_act = _matmul("d_ffn_out", dfo, w["w_ffn_out_t"], out_dtype=F32, tm=tm, tn=1408, tk=1024)
    grads["w_ffn_out"] = _matmul_tn("dw_ffn_out", act, dfo, tk=1408, tn=1024, tr=tm, out_dtype=dw_dt)

    def swiglu_bwd(gt, up, da):
        s = _sigmoid(gt)
        return [jnp.concatenate([da * up * _silu_grad(gt, s), da * gt * s], axis=1)], []

    (dgu,), _ = rw("swiglu_bwd", swiglu_bwd,
                   ins=[("row", gu, FFN_HIDDEN, 0), ("row", gu, FFN_HIDDEN, 1), ("row", d_act, FFN_HIDDEN, 0)],
                   outs=[(2 * FFN_HIDDEN, bf)])
    du2 = _matmul("d_ffn_in", dgu, w["w_ffn_in_t"], out_dtype=F32, tm=tm, tn=1024, tk=1408)
    grads["w_ffn_in"] = _matmul_tn("dw_ffn_in", u2, dgu, tk=1024, tn=1408, tr=tm)

    def post_mix_bwd(du2_, h1_, dh2_, mx_, g3, g2):
        dx, dg3 = _rms_bwd(h1_, g3, du2_)
        dh1_ = dh2_ + dx
        dmx, dg2 = _rms_bwd(mx_, g2, dh1_)
        return [dh1_, dmx], [dg3, dg2]

    (dh1, dmixed), (dg_ffn_pre, dg_mix_post) = rw(
        "post_mix_bwd", post_mix_bwd,
        ins=[("row", du2, d, 0), ("row", h1, d, 0), ("row", dh2, d, 0), ("row", mixed, d, 0),
             ("const", small["ffn_pre_g"]), ("const", small["mix_post_g"])],
        outs=[(d, F32), (d, bf)], accs=[d, d])
    dmix = _matmul("d_proj_out", dmixed, w["w_out_t"], out_dtype=F32, tm=tm, tn=1024, tk=1024)
    grads["w_out"] = _matmul_tn("dw_out", mix, dmixed, tk=1024, tn=1024, tr=tm, out_dtype=dw_dt)

    def merge_bwd(dm, pa, pb, ya, yb, bg):
        ga, gb = _sigmoid(pa + bg[:, :d]), _sigmoid(pb + bg[:, d:])
        dpg = jnp.concatenate([dm * ya * ga * (1.0 - ga), dm * yb * gb * (1.0 - gb)], axis=1)
        return [dpg, dm * ga, dm * gb], [dpg]

    (dpg, dya, dyb), (db_gate,) = rw(
        "merge_bwd", merge_bwd,
        ins=[("row", dmix, d, 0), ("row", p, 1024, CB_GA), ("row", p, 1024, CB_GB), ("row", y_a, d, 0),
             ("row", y_b, d, 0), ("const", small["b_gate"])],
        outs=[(2 * d, bf), (d, bf), (d, bf)], accs=[2 * d])
    dz_a = _matmul("d_proj_hg_o", dya, w["w_hg_o_t"], out_dtype=F32, tm=tm, tn=1024, tk=1024)
    grads["w_hg_o"] = _matmul_tn("dw_hg_o", z_a, dya, tk=1024, tn=1024, tr=tm, out_dtype=dw_dt)
    do_at = _matmul("d_proj_mla_o", dyb, w["w_mla_o_t"], out_dtype=bf, tm=tm, tn=1024, tk=1024)
    grads["w_mla_o"] = _matmul_tn("dw_mla_o", o_at, dyb, tk=1024, tn=1024, tr=tm, out_dtype=dw_dt)

    dph, dlb, dgh, *got = _hgrn_bwd(p, o_hg, dz_a, states, lb, gh, batch=batch, lp=lp,
                                    ride=scatter(_GRAD_GROUPS[0]))
    received.append(got)

    dq_cat, dkn, dkp, dvv, *got = _attn_bwd(q_cat, kv, kp, do_at, o_at, lse, batch=batch, lp=lp,
                                            ride=scatter(_GRAD_GROUPS[1]))
    received.append(got)

    def rope_q_bwd(dq, ct, st):
        hs = lambda half: jnp.concatenate(
            [dq[:, h * QK_DIM + half * HEAD_DIM:h * QK_DIM + (half + 1) * HEAD_DIM] for h in range(HEADS)], axis=1)
        dpe = hs(1)
        return [jnp.concatenate([hs(0), dpe * ct, dpe * st], axis=1)], []

    (dqf,), _ = rw("rope_q_bwd", rope_q_bwd,
                   ins=[("row", dq_cat, HEADS * QK_DIM, 0), ("pos", cq_tab), ("pos", sq_tab)],
                   outs=[(3 * d, bf)])
    dcqn = _matmul("d_proj_q_b", dqf, w["w_q_t"], out_dtype=F32, tm=tm, tn=Q_LORA, tk=1024)
    grads["w_q"] = _matmul_tn("dw_q_b", cqn, dqf, tk=Q_LORA, tn=1024, tr=tm)
    dckvn = _matmul("d_proj_k_b", dkn, w["w_k_t"], out_dtype=F32, tm=tm, tn=KV_LORA, tk=1024)
    dckvn = _matmul("d_proj_v_b", dvv, w["w_v_t"], out_dtype=F32, tm=tm, tn=KV_LORA, tk=1024, c_in=dckvn)
    grads["w_k"] = _matmul_tn("dw_k_b", ckvn, dkn, tk=KV_LORA, tn=1024, tr=tm)
    grads["w_v"] = _matmul_tn("dw_v_b", ckvn, dvv, tk=KV_LORA, tn=1024, tr=tm)

    def mla_pre_bwd(pc, dq_, dkv_, dkp_, gq, gkv, ct, st):
        cq, ckv = pc[:, 0:Q_LORA], pc[:, Q_LORA:Q_LORA + KV_LORA]
        dcq, dgq = _rms_bwd(cq, gq, dq_)
        dckv, dgkv = _rms_bwd(ckv, gkv, dkv_)
        dpc = jnp.concatenate([dcq, dckv, dkp_ * ct, dkp_ * st, jnp.zeros((pc.shape[0], 256), F32)], axis=1)
        return [dpc], [dgq, dgkv]

    (dpc,), (dg_q, dg_kv) = rw(
        "mla_pre_bwd", mla_pre_bwd,
        ins=[("row", p, 1024, CB_C), ("row", dcqn, Q_LORA, 0), ("row", dckvn, KV_LORA, 0), ("row", dkp, HEAD_DIM, 0),
             ("const", small["q_a_norm_g"]), ("const", small["kv_a_norm_g"]), ("pos", c128), ("pos", s128)],
        outs=[(1024, bf)], accs=[Q_LORA, KV_LORA])

    grads["w_in"] = jnp.concatenate([
        _matmul_tn("dw_in_h", u1, dph, tk=1024, tn=1024, tr=tm),
        _matmul_tn("dw_in_c", u1, dpc, tk=1024, tn=1024, tr=tm),
        _matmul_tn("dw_in_g", u1, dpg, tk=1024, tn=1024, tr=tm)], axis=1)
    du1 = _matmul("d_proj_in_h", dph, w["w_in_t"][:4096], out_dtype=F32, tm=tm, tn=1024, tk=1024,
                  ride=scatter(_GRAD_GROUPS[2]))
    if comm is not None:
        du1, *got = du1
        received.append(got)
    du1 = _matmul("d_proj_in_c", dpc, w["w_in_t"][4096:5120], out_dtype=F32, tm=tm, tn=1024, tk=1024, c_in=du1)
    du1 = _matmul("d_proj_in_g", dpg, w["w_in_t"][5120:], out_dtype=F32, tm=tm, tn=1024, tk=1024, c_in=du1)

    def pre_bwd(du, h, dh, g1):
        dx, dg1 = _rms_bwd(h, g1, du)
        return [dh + dx], [dg1]

    (dh0,), (dg_mix_pre,) = rw("norm_mix_pre_bwd", pre_bwd,
                               ins=[("row", du1, d, 0), ("row", h0, d, 0), ("row", dh1, d, 0), ("const", small["mix_pre_g"])],
                               outs=[(d, F32)], accs=[d])
    grad_x = dh0.reshape(batch, lp, d)[:, N_META:N_META + seq]
    partial = {"meta_tokens": _meta_grad(dh0, batch, lp), "lb_logits": dlb, "b_gate": db_gate, "hg_norm_g": dgh,
               "q_a_norm_g": dg_q, "kv_a_norm_g": dg_kv, "mix_pre_g": dg_mix_pre, "mix_post_g": dg_mix_post,
               "ffn_pre_g": dg_ffn_pre, "ffn_post_g": dg_ffn_post}
    return loss, grad_x, grads, partial, lb, received


_BIG = ["w_in", "w_hg_o", "w_q_b", "w_kv_b", "w_mla_o", "w_out", "w_ffn_in", "w_ffn_out"]
_COLUMN_SHARDED = {"w_in", "w_q_b", "w_kv_b", "w_ffn_in"}
_GRAD_GROUPS = [["w_ffn_in", "w_ffn_out"], ["w_out", "w_hg_o", "w_mla_o"], ["w_in", "w_q_b", "w_kv_b"]]
_SMALL = ["b_gate", "lb_logits", "hg_norm_g", "q_a_norm_g", "kv_a_norm_g", "mix_pre_g", "mix_post_g",
          "ffn_pre_g", "ffn_post_g"]


def _gathered_matrix(name, t):
    _, k, n = t.shape
    if name in _COLUMN_SHARDED:
        return t.transpose(1, 0, 2).reshape(k, N_DEV * n)
    return t.reshape(N_DEV * k, n)


def _scatter_layout(name, full):
    kk, nn = full.shape
    if name in _COLUMN_SHARDED:
        t = full.reshape(kk, N_DEV, nn // N_DEV).transpose(1, 0, 2)
    else:
        t = full.reshape(N_DEV, kk // N_DEV, nn)
    return t.astype(_WIRE_DTYPE)


def _model_w_in(wi):
    z = lambda *s: jnp.zeros(s, wi.dtype)
    kpe = wi[:, 4608:4672]
    c_blk = jnp.concatenate([wi[:, 4096:4608], kpe, z(1024, 64), _swap_halves(kpe), z(1024, 64), z(1024, 256)], axis=1)
    w_in = jnp.concatenate([wi[:, :4096], c_blk, wi[:, 4672:]], axis=1).astype(_MXU_DTYPE)
    return {"w_in": w_in, "w_in_t": w_in.T}


def _model_weights(full):
    return {**_model_w_in(full["w_in"]), **_model_rest(full)}


def _model_rest(full):
    wq3 = full["w_q_b"].reshape(Q_LORA, HEADS, HEAD_DIM + ROPE_DIM)
    pe = wq3[:, :, HEAD_DIM:]
    w_q = jnp.concatenate([wq3[:, :, :HEAD_DIM].reshape(Q_LORA, -1),
                           _pad_last(pe, HEAD_DIM).reshape(Q_LORA, -1),
                           _pad_last(_swap_halves(pe), HEAD_DIM).reshape(Q_LORA, -1)], axis=1)
    wkv3 = full["w_kv_b"].reshape(KV_LORA, HEADS, 2 * HEAD_DIM)
    w_k = wkv3[:, :, :HEAD_DIM].reshape(KV_LORA, -1)
    w_v = wkv3[:, :, HEAD_DIM:].reshape(KV_LORA, -1)
    w = {"w_q": w_q, "w_kv": jnp.concatenate([w_k, w_v], axis=1),
         "w_hg_o": full["w_hg_o"], "w_mla_o": full["w_mla_o"], "w_out": full["w_out"],
         "w_ffn_in": full["w_ffn_in"], "w_ffn_out": full["w_ffn_out"]}
    for n in ["w_q", "w_hg_o", "w_mla_o", "w_out", "w_ffn_in", "w_ffn_out"]:
        w[n + "_t"] = w[n].T
    w["w_k_t"], w["w_v_t"] = w_k.T, w_v.T
    return {k: v.astype(_MXU_DTYPE) for k, v in w.items()}


def _reference_layout_grad(name, g):
    if name == "w_in":
        gi = g["w_in"]
        d_kpe = gi[:, 4608:4672] + _swap_halves(gi[:, 4736:4800])
        return jnp.concatenate([gi[:, :4608], d_kpe, gi[:, 5120:]], axis=1)
    if name == "w_q_b":
        gq = g["w_q"]
        d_pe = (gq[:, 1024:2048].reshape(Q_LORA, HEADS, HEAD_DIM)[:, :, :ROPE_DIM]
                + _swap_halves(gq[:, 2048:].reshape(Q_LORA, HEADS, HEAD_DIM)[:, :, :ROPE_DIM]))
        return jnp.concatenate([gq[:, :1024].reshape(Q_LORA, HEADS, HEAD_DIM), d_pe], axis=2).reshape(Q_LORA, -1)
    if name == "w_kv_b":
        return jnp.concatenate([g["w_k"].reshape(KV_LORA, HEADS, HEAD_DIM),
                                g["w_v"].reshape(KV_LORA, HEADS, HEAD_DIM)], axis=2).reshape(KV_LORA, -1)
    return g[name]


def _reference_layout_grads(g):
    return {n: _reference_layout_grad(n, g) for n in _BIG}


class _Comm:
    def __init__(self, shard):
        self.rest_payloads = [shard[n].astype(_WIRE_DTYPE) for n in _BIG[1:]]

    def rest_weights(self, gathered):
        return _model_rest({n: _gathered_matrix(n, t) for n, t in zip(_BIG[1:], gathered, strict=True)})

    def grad_parts(self, names, g):
        return [_scatter_layout(n, _reference_layout_grad(n, g)) for n in names]


def kernel(x, meta_tokens, w_in, b_gate, lb_logits, hg_norm_g, w_hg_o, q_a_norm_g, w_q_b, kv_a_norm_g, w_kv_b, w_mla_o, w_out, mix_pre_g, mix_post_g, ffn_pre_g, ffn_post_g, w_ffn_in, w_ffn_out, loss_target, m_meta_tokens, m_w_in, m_b_gate, m_lb_logits, m_hg_norm_g, m_w_hg_o, m_q_a_norm_g, m_w_q_b, m_kv_a_norm_g, m_w_kv_b, m_w_mla_o, m_w_out, m_mix_pre_g, m_mix_post_g, m_ffn_pre_g, m_ffn_post_g, m_w_ffn_in, m_w_ffn_out, v_meta_tokens, v_w_in, v_b_gate, v_lb_logits, v_hg_norm_g, v_w_hg_o, v_q_a_norm_g, v_w_q_b, v_kv_a_norm_g, v_w_kv_b, v_w_mla_o, v_w_out, v_mix_pre_g, v_mix_post_g, v_ffn_pre_g, v_ffn_post_g, v_w_ffn_in, v_w_ffn_out):
    args = dict(locals())
    batch, seq, d = x.shape
    lp = -(-(N_META + seq) // _LANES) * _LANES
    weight_names = ["meta_tokens", "w_in", "b_gate", "lb_logits", "hg_norm_g", "w_hg_o", "q_a_norm_g", "w_q_b",
                    "kv_a_norm_g", "w_kv_b", "w_mla_o", "w_out", "mix_pre_g", "mix_post_g", "ffn_pre_g",
                    "ffn_post_g", "w_ffn_in", "w_ffn_out"]
    shard = {n: args[n].reshape(args[n].shape[-2:]) for n in _BIG}
    comm = _Comm(shard)

    w_in_all, meta_all = _all_gather("gather_first", [shard["w_in"].astype(_WIRE_DTYPE), meta_tokens])
    w_first = _model_w_in(_gathered_matrix("w_in", w_in_all))
    meta_full = meta_all.transpose(1, 0, 2).reshape(N_META, d)
    small = {n: args[n] for n in _SMALL}

    loss, grad_x, _, partial, lb, received = _forward_backward(x, loss_target, meta_full, w_first, small, lp=lp,
                                                               comm=comm)
    loss = lax.psum(loss, ("x", "y", "c"))
    out = {}
    for names, bufs in zip(_GRAD_GROUPS, received, strict=True):
        for n, buf in zip(names, bufs, strict=True):
            two_d = lambda t: t.reshape(t.shape[-2:])
            res = _sum_adamw("adamw_" + n, buf, shard[n], two_d(args["m_" + n]), two_d(args["v_" + n]))
            out[n] = [t.reshape(args[n].shape) for t in res]

    vec_names = _SMALL + ["meta_tokens"]
    gathered = _all_gather("gather_vectors", [partial[n] for n in vec_names])
    out.update(_finish_vectors(dict(zip(vec_names, gathered, strict=True)), lb,
                               {n: (args[n], args["m_" + n], args["v_" + n]) for n in vec_names}))
    return (loss, grad_x, *[out[n][i] for i in range(4) for n in weight_names])
```

```python
import functools

import jax
import jax.numpy as jnp
from jax import lax
from jax.experimental import pallas as pl
from jax.experimental.pallas import tpu as pltpu

F32 = jnp.float32
_MXU_DTYPE = jnp.bfloat16
_WIRE_DTYPE = jnp.bfloat16
_VMEM_LIMIT_BYTES = 56 * 1024 * 1024
_LANES = 128
_SUBLANES = 8

N_DEV = 8
N_META = 16
NORM_EPS = 1e-6
HEADS = 8
HEAD_DIM = 128
ROPE_DIM = 64
HG_CHUNK = 16
HG_BLOCK = 128
ROPE_THETA = 10000.0
D_MODEL = 1024
Q_LORA = 256
KV_LORA = 256
FFN_HIDDEN = 2816
ATTN_SCALE = (HEAD_DIM + ROPE_DIM) ** -0.5
NEG_BIG = -1e30

ADAM_LR = 0.001
ADAM_B1 = 0.9
ADAM_B2 = 0.999
ADAM_EPS = 1e-08
ADAM_WD = 0.01
ADAM_STEP = 10

CB_HQ, CB_HF, CB_HI, CB_HG, CB_C, CB_GA, CB_GB = range(7)
IN_COLS_PADDED = 7 * 1024


def _params(**kw):
    return pltpu.CompilerParams(vmem_limit_bytes=_VMEM_LIMIT_BYTES, **kw)


def _dot(a, b):
    return lax.dot_general(a, b, (((1,), (0,)), ((), ())), preferred_element_type=F32)


def _dot_nt(a, b):
    return lax.dot_general(a, b, (((1,), (1,)), ((), ())), preferred_element_type=F32)


def _dot_tn(a, b):
    return lax.dot_general(a, b, (((0,), (0,)), ((), ())), preferred_element_type=F32)


def _mx(x):
    return x.astype(_MXU_DTYPE)


def _exact_dot(m01, x, dot=_dot):
    if _MXU_DTYPE == jnp.float32:
        return dot(m01.astype(F32), x)
    m = m01.astype(jnp.bfloat16)
    x1 = x.astype(jnp.bfloat16)
    r1 = x - x1.astype(F32)
    x2 = r1.astype(jnp.bfloat16)
    x3 = (r1 - x2.astype(F32)).astype(jnp.bfloat16)
    return dot(m, x1) + dot(m, x2) + dot(m, x3)


def _exact_dot_nt(m01, x):
    return _exact_dot(m01, x, dot=_dot_nt)


def _sigmoid(x):
    return jax.nn.sigmoid(x)


def _silu_grad(x, s):
    return s * (1.0 + x * (1.0 - s))


def _rms_scale(x):
    return lax.rsqrt(jnp.mean(x * x, axis=-1, keepdims=True) + NORM_EPS)


def _rms_bwd(x, g, dy):
    r = _rms_scale(x)
    xh = x * r
    w = dy * g
    dx = r * (w - xh * jnp.mean(xh * w, axis=-1, keepdims=True))
    return dx, dy * xh


def _heads(fn, *arrays):
    outs = [fn(*[a[:, h * HEAD_DIM:(h + 1) * HEAD_DIM] for a in arrays]) for h in range(HEADS)]
    if isinstance(outs[0], tuple):
        return tuple(jnp.concatenate([o[i] for o in outs], axis=1) for i in range(len(outs[0])))
    return jnp.concatenate(outs, axis=1)


class _Ride:
    def __init__(self, payloads, gather):
        self.gather, self.args, self.n = gather, list(payloads), len(payloads)
        self.in_specs = [pl.BlockSpec(memory_space=pl.ANY)] * self.n
        self.out_shape = [jax.ShapeDtypeStruct((N_DEV, *p.shape[-2:]), p.dtype) for p in payloads]
        self.out_specs = [pl.BlockSpec(memory_space=pl.ANY)] * self.n
        self.scratch = [pltpu.SemaphoreType.DMA((self.n, N_DEV - 1)), pltpu.SemaphoreType.DMA((self.n, N_DEV - 1)),
                        pltpu.SemaphoreType.DMA((self.n,))]

    def split(self, rest, n_outs):
        n = self.n
        mine = (rest[:n], rest[n + n_outs:2 * n + n_outs], rest[-3:])
        return rest[n:n + n_outs] + rest[2 * n + n_outs:-3], mine

    def _copies(self, p_refs, out_refs, sems):
        send_sems, recv_sems, local_sems = sems
        x, y, c = lax.axis_index("x"), lax.axis_index("y"), lax.axis_index("c")
        me = 4 * x + 2 * y + c
        copies = []
        for i, (p_ref, out_ref) in enumerate(zip(p_refs, out_refs, strict=True)):
            part = (lambda j, p_ref=p_ref: p_ref) if self.gather else (lambda j, p_ref=p_ref: p_ref.at[j])
            copies.append(pltpu.make_async_copy(part(me), out_ref.at[me], local_sems.at[i]))
            for k in range(1, N_DEV):
                px, py, pc = x ^ (k >> 2), y ^ ((k >> 1) & 1), c ^ (k & 1)
                copies.append(pltpu.make_async_remote_copy(
                    src_ref=part(4 * px + 2 * py + pc), dst_ref=out_ref.at[me],
                    send_sem=send_sems.at[i, k - 1], recv_sem=recv_sems.at[i, k - 1],
                    device_id=(px, py, pc), device_id_type=pl.DeviceIdType.MESH))
        return copies

    def run(self, grid, refs):
        ids = [pl.program_id(i) for i in range(len(grid))]
        first = functools.reduce(jnp.logical_and, [i == 0 for i in ids])
        last = functools.reduce(jnp.logical_and, [i == g - 1 for i, g in zip(ids, grid)])

        @pl.when(first)
        def _():
            for cp in self._copies(*refs):
                cp.start()

        @pl.when(last)
        def _():
            for cp in self._copies(*refs):
                cp.wait()


class _NoRide:
    in_specs, out_shape, out_specs, scratch, args = [], [], [], [], []


def _matmul(name, a, b, *, out_dtype, tm, tn, tk, c_in=None, ride=None):
    m, k = a.shape
    _, n = b.shape
    assert m % tm == 0 and n % tn == 0 and k % tk == 0, (name, a.shape, b.shape, tm, tn, tk)
    nk = k // tk
    has_c = c_in is not None
    grid = (n // tn, m // tm, nk)
    n_in = 2 + has_c

    def body(*refs):
        a_ref, b_ref = refs[0], refs[1]
        c_ref = refs[2] if has_c else None
        rest = refs[n_in:]
        if ride is not None:
            rest, exchange = ride.split(rest, 1)
            ride.run(grid, exchange)
        o_ref = rest[0]
        acc_ref = rest[1] if nk > 1 else None

        def finish(r):
            if has_c:
                r = r + c_ref[...]
            o_ref[...] = r.astype(o_ref.dtype)

        if nk == 1:
            finish(_dot(a_ref[...], b_ref[...]))
        else:
            kk = pl.program_id(2)

            @pl.when(kk == 0)
            def _():
                acc_ref[...] = jnp.zeros_like(acc_ref)

            acc_ref[...] += _dot(a_ref[...], b_ref[...])

            @pl.when(kk == nk - 1)
            def _():
                finish(acc_ref[...])

    in_specs = [pl.BlockSpec((tm, tk), lambda j, i, kk: (i, kk)),
                pl.BlockSpec((tk, tn), lambda j, i, kk: (kk, j))]
    args = [a, b]
    aliases = {}
    if has_c:
        in_specs.append(pl.BlockSpec((tm, tn), lambda j, i, kk: (i, j)))
        args.append(c_in)
        aliases = {2: 0}
    out_shape = [jax.ShapeDtypeStruct((m, n), out_dtype)]
    out_specs = [pl.BlockSpec((tm, tn), lambda j, i, kk: (i, j))]
    scratch = [pltpu.VMEM((tm, tn), F32)] if nk > 1 else []
    if ride is not None:
        in_specs, args = in_specs + ride.in_specs, args + ride.args
        out_shape, out_specs, scratch = out_shape + ride.out_shape, out_specs + ride.out_specs, scratch + ride.scratch
    res = pl.pallas_call(
        body, name=name, out_shape=out_shape, grid=grid, in_specs=in_specs, out_specs=out_specs,
        scratch_shapes=scratch, input_output_aliases=aliases, compiler_params=_params(),
    )(*args)
    return res[0] if ride is None else res


def _matmul_tn(name, x, dy, *, tk, tn, tr, out_dtype=F32):
    r, k = x.shape
    _, n = dy.shape
    assert r % tr == 0 and k % tk == 0 and n % tn == 0, (name, x.shape, dy.shape)
    n_r = r // tr
    direct = out_dtype == F32

    def body(x_ref, dy_ref, o_ref, *scratch):
        acc_ref = o_ref if direct else scratch[0]

        @pl.when(pl.program_id(2) == 0)
        def _():
            acc_ref[...] = jnp.zeros_like(acc_ref)

        acc_ref[...] += _dot_tn(x_ref[...], dy_ref[...])
        if not direct:
            @pl.when(pl.program_id(2) == n_r - 1)
            def _():
                o_ref[...] = acc_ref[...].astype(o_ref.dtype)

    return pl.pallas_call(
        body, name=name,
        out_shape=jax.ShapeDtypeStruct((k, n), out_dtype),
        grid=(k // tk, n // tn, n_r),
        in_specs=[pl.BlockSpec((tr, tk), lambda kb, nb, rr: (rr, kb)),
                  pl.BlockSpec((tr, tn), lambda kb, nb, rr: (rr, nb))],
        out_specs=pl.BlockSpec((tk, tn), lambda kb, nb, rr: (kb, nb)),
        scratch_shapes=[] if direct else [pltpu.VMEM((tk, tn), F32)],
        compiler_params=_params(),
    )(x, dy)


def _ffn_in_swiglu(u, w, *, tm, tn):
    r, k = u.shape
    h = w.shape[1] // 2
    assert r % tm == 0 and h % tn == 0
    nj = h // tn

    def body(u_ref, wg_ref, wu_ref, act_ref, gt_ref, up_ref):
        uu = u_ref[...]
        gt, up = _dot(uu, wg_ref[...]), _dot(uu, wu_ref[...])
        act_ref[...] = (gt * _sigmoid(gt) * up).astype(act_ref.dtype)
        gt_ref[...] = gt.astype(gt_ref.dtype)
        up_ref[...] = up.astype(up_ref.dtype)

    tile = pl.BlockSpec((tm, tn), lambda j, i: (i, j))
    return pl.pallas_call(
        body, name="ffn_in_swiglu",
        out_shape=[jax.ShapeDtypeStruct((r, h), _MXU_DTYPE)] * 3,
        grid=(nj, r // tm),
        in_specs=[pl.BlockSpec((tm, k), lambda j, i: (i, 0)),
                  pl.BlockSpec((k, tn), lambda j, i: (0, j)),
                  pl.BlockSpec((k, tn), lambda j, i: (0, nj + j))],
        out_specs=[tile] * 3,
        compiler_params=_params(),
    )(u, w, w)


def _d_ffn_out_swiglu(dy, w_t, gt, up, *, tm, tn):
    r, k = dy.shape
    h = w_t.shape[1]
    assert r % tm == 0 and h % tn == 0

    def body(dy_ref, w_ref, gt_ref, up_ref, dgt_ref, dup_ref):
        da = _dot(dy_ref[...], w_ref[...])
        g, u_ = gt_ref[...].astype(F32), up_ref[...].astype(F32)
        s = _sigmoid(g)
        dgt_ref[...] = (da * u_ * _silu_grad(g, s)).astype(dgt_ref.dtype)
        dup_ref[...] = (da * g * s).astype(dup_ref.dtype)

    tile = pl.BlockSpec((tm, tn), lambda j, i: (i, j))
    return pl.pallas_call(
        body, name="d_ffn_out_swiglu",
        out_shape=[jax.ShapeDtypeStruct((r, h), _MXU_DTYPE)] * 2,
        grid=(h // tn, r // tm),
        in_specs=[pl.BlockSpec((tm, k), lambda j, i: (i, 0)), pl.BlockSpec((k, tn), lambda j, i: (0, j)), tile, tile],
        out_specs=[tile] * 2,
        compiler_params=_params(),
    )(dy, w_t, gt, up)


def _rowwise(name, body, *, rows, tr, lp, ins, outs, accs=()):
    assert rows % tr == 0 and lp % tr == 0 and tr % 16 == 0
    tiles_per_example = lp // tr
    in_specs, arrays = [], []
    for spec in ins:
        if spec[0] == "row":
            _, arr, width, cb = spec
            in_specs.append(pl.BlockSpec((tr, width), functools.partial(lambda i, cb: (i, cb), cb=cb)))
        elif spec[0] == "const":
            arr = spec[1]
            in_specs.append(pl.BlockSpec(arr.shape, lambda i: (0, 0)))
        else:
            arr = spec[1]
            in_specs.append(pl.BlockSpec((tr, arr.shape[1]), lambda i: (i % tiles_per_example, 0)))
        arrays.append(arr)
    n_in, n_out = len(ins), len(outs)

    def kern(*refs):
        res_outs, res_accs = body(*[r[...] for r in refs[:n_in]])
        for ref, val in zip(refs[n_in:n_in + n_out], res_outs, strict=True):
            ref[...] = val.astype(ref.dtype)
        acc_refs = refs[n_in + n_out:]
        if acc_refs:
            @pl.when(pl.program_id(0) == 0)
            def _():
                for ref in acc_refs:
                    ref[...] = jnp.zeros_like(ref)

            for ref, val in zip(acc_refs, res_accs, strict=True):
                ref[...] += val.reshape(tr // _SUBLANES, _SUBLANES, val.shape[-1]).sum(axis=0)

    out_shape = ([jax.ShapeDtypeStruct((rows, w), dt) for w, dt in outs]
                 + [jax.ShapeDtypeStruct((_SUBLANES, w), F32) for w in accs])
    out_specs = ([pl.BlockSpec((tr, w), lambda i: (i, 0)) for w, _ in outs]
                 + [pl.BlockSpec((_SUBLANES, w), lambda i: (0, 0)) for w in accs])
    res = pl.pallas_call(
        kern, name=name, out_shape=out_shape, grid=(rows // tr,),
        in_specs=in_specs, out_specs=out_specs, compiler_params=_params(),
    )(*arrays)
    return res[:n_out], list(res[n_out:])


def _assemble(name, x, head_rows, lp):
    batch, seq, d = x.shape
    tc = 256

    def body(x_ref, m_ref, o_ref):
        o_ref[0:N_META, :] = m_ref[...]
        o_ref[N_META:N_META + seq, :] = x_ref[0]
        if lp > N_META + seq:
            o_ref[N_META + seq:, :] = jnp.zeros((lp - N_META - seq, tc), F32)

    return pl.pallas_call(
        body, name=name,
        out_shape=jax.ShapeDtypeStruct((batch * lp, d), F32),
        grid=(batch, d // tc),
        in_specs=[pl.BlockSpec((1, seq, tc), lambda b, j: (b, 0, j)),
                  pl.BlockSpec((N_META, tc), lambda b, j: (0, j))],
        out_specs=pl.BlockSpec((lp, tc), lambda b, j: (b, j)),
        compiler_params=_params(),
    )(x, head_rows)


def _meta_grad(dh0, batch, lp):
    d = dh0.shape[1]

    def body(g_ref, o_ref):
        @pl.when(pl.program_id(0) == 0)
        def _():
            o_ref[...] = jnp.zeros_like(o_ref)

        o_ref[...] += g_ref[...]

    return pl.pallas_call(
        body, name="meta_grad",
        out_shape=jax.ShapeDtypeStruct((N_META, d), F32),
        grid=(batch,),
        in_specs=[pl.BlockSpec((N_META, d), lambda b: (b * (lp // N_META), 0))],
        out_specs=pl.BlockSpec((N_META, d), lambda b: (0, 0)),
        compiler_params=_params(),
    )(dh0)


def _segment_masks():
    t = lax.broadcasted_iota(jnp.int32, (HG_BLOCK, HG_BLOCK), 0)
    s = lax.broadcasted_iota(jnp.int32, (HG_BLOCK, HG_BLOCK), 1)
    same = lax.shift_right_logical(t, 4) == lax.shift_right_logical(s, 4)
    lower = same & (s <= t)
    upper = same & (s >= t)
    first_half = same & ((s & 15) <= 7)
    return same, lower, upper, first_half


def _hgrn_gates(hq, hf, lb):
    sq = _sigmoid(hq)
    q = hq * sq
    sf = _sigmoid(hf)
    f = lb + (1.0 - lb) * sf
    return q, sq, sf, f


def _hgrn_decays(g, same, lower, first_half):
    b = _exact_dot(lower, g)
    b_last = _exact_dot(same, g)
    b_ref = _exact_dot(first_half, g)
    return b, b_last, b_ref


def _hgrn_fwd(p, lb, gh, *, batch, lp, ride=None):
    rows = batch * lp
    nb = lp // HG_BLOCK
    n_chunks = HG_BLOCK // HG_CHUNK

    def body(hq_ref, hf_ref, hi_ref, hg_ref, lb_ref, gh_ref, *rest):
        if ride is not None:
            rest, exchange = ride.split(rest, 3)
            ride.run((batch, nb), exchange)
        o_ref, z_ref, st_ref, s_scr, qt_scr, kh_scr, v_scr, el_scr, o_scr = rest

        @pl.when(pl.program_id(1) == 0)
        def _():
            s_scr[...] = jnp.zeros_like(s_scr)

        same, lower, _, first_half = _segment_masks()
        v = hi_ref[...]
        q, _, _, f = _hgrn_gates(hq_ref[...], hf_ref[...], lb_ref[...])
        k = 1.0 - f
        b, b_last, b_ref = _hgrn_decays(jnp.log(f), same, lower, first_half)
        qt_scr[...] = _mx(q * jnp.exp(b))
        kh_scr[...] = _mx(k * jnp.exp(b_last - b))
        v_scr[...] = _mx(v)
        el_scr[...] = jnp.exp(b_last)
        qc = _mx(q * jnp.exp(b - b_ref))
        kc = _mx(k * jnp.exp(b_ref - b))

        def intra(qc_h, kc_h, v_h):
            a = jnp.where(lower, _dot_nt(qc_h, kc_h), 0.0)
            return _dot(_mx(a), v_h)

        o_scr[...] = _heads(intra, qc, kc, _mx(v))

        for c in range(n_chunks):
            rs = slice(c * HG_CHUNK, (c + 1) * HG_CHUNK)
            for h in range(HEADS):
                cs = slice(h * HEAD_DIM, (h + 1) * HEAD_DIM)
                st = s_scr[h]
                st_m = _mx(st)
                st_ref[c, h] = st_m
                o_scr[rs, cs] += _dot_nt(qt_scr[rs, cs], st_m)
                s_scr[h] = st * el_scr[c * HG_CHUNK:c * HG_CHUNK + 1, cs] + _dot_tn(v_scr[rs, cs], kh_scr[rs, cs])

        o = o_scr[...]
        o_ref[...] = o
        hg = hg_ref[...]
        n = _heads(lambda o_h: o_h * _rms_scale(o_h), o) * gh_ref[...]
        z_ref[...] = (n * hg * _sigmoid(hg)).astype(z_ref.dtype)

    blk = lambda cb: pl.BlockSpec((HG_BLOCK, D_MODEL), functools.partial(lambda b, j, cb: (b * nb + j, cb), cb=cb))
    row_out = pl.BlockSpec((HG_BLOCK, D_MODEL), lambda b, j: (b * nb + j, 0))
    const = pl.BlockSpec((1, D_MODEL), lambda b, j: (0, 0))
    extra = ride if ride is not None else _NoRide
    return pl.pallas_call(
        body, name="hgrn_fwd",
        out_shape=[jax.ShapeDtypeStruct((rows, D_MODEL), F32),
                   jax.ShapeDtypeStruct((rows, D_MODEL), _MXU_DTYPE),
                   jax.ShapeDtypeStruct((rows // HG_CHUNK, HEADS, HEAD_DIM, HEAD_DIM), _MXU_DTYPE)] + extra.out_shape,
        grid=(batch, nb),
        in_specs=[blk(CB_HQ), blk(CB_HF), blk(CB_HI), blk(CB_HG), const, const] + extra.in_specs,
        out_specs=[row_out, row_out,
                   pl.BlockSpec((n_chunks, HEADS, HEAD_DIM, HEAD_DIM), lambda b, j: (b * nb + j, 0, 0, 0))]
        + extra.out_specs,
        scratch_shapes=[pltpu.VMEM((HEADS, HEAD_DIM, HEAD_DIM), F32),
                        pltpu.VMEM((HG_BLOCK, D_MODEL), _MXU_DTYPE),
                        pltpu.VMEM((HG_BLOCK, D_MODEL), _MXU_DTYPE),
                        pltpu.VMEM((HG_BLOCK, D_MODEL), _MXU_DTYPE),
                        pltpu.VMEM((HG_BLOCK, D_MODEL), F32),
                        pltpu.VMEM((HG_BLOCK, D_MODEL), F32)] + extra.scratch,
        compiler_params=_params(),
    )(p, p, p, p, lb, gh, *extra.args)


def _hgrn_bwd(p, o, dz, states, lb, gh, *, batch, lp, ride=None):
    rows = batch * lp
    nb = lp // HG_BLOCK
    n_chunks = HG_BLOCK // HG_CHUNK

    def body(hq_ref, hf_ref, hi_ref, hg_ref, o_ref, dz_ref, st_ref, lb_ref, gh_ref, *rest):
        if ride is not None:
            rest, exchange = ride.split(rest, 3)
            ride.run((batch, nb), exchange)
        (dp_ref, dlb_ref, dgh_ref,
         ds_scr, qt_scr, kh_scr, v_scr, do_scr, el_scr, dqt_scr, dkh_scr, dv_scr, dbl_scr) = rest
        first = (pl.program_id(0) == 0) & (pl.program_id(1) == 0)

        @pl.when(first)
        def _():
            dlb_ref[...] = jnp.zeros_like(dlb_ref)
            dgh_ref[...] = jnp.zeros_like(dgh_ref)

        @pl.when(pl.program_id(1) == 0)
        def _():
            ds_scr[...] = jnp.zeros_like(ds_scr)

        same, lower, upper, first_half = _segment_masks()
        lbv = lb_ref[...]
        hq, hf, v, hg = hq_ref[...], hf_ref[...], hi_ref[...], hg_ref[...]
        q, sq, sf, f = _hgrn_gates(hq, hf, lbv)
        k = 1.0 - f
        b, b_last, b_ref = _hgrn_decays(jnp.log(f), same, lower, first_half)
        e_b = jnp.exp(b)
        e_kh = jnp.exp(b_last - b)
        e_qc = jnp.exp(b - b_ref)
        e_kc = jnp.exp(b_ref - b)
        qt, kh, qc, kc = q * e_b, k * e_kh, q * e_qc, k * e_kc

        o = o_ref[...]
        dz = dz_ref[...].astype(F32)
        ghv = gh_ref[...]
        sg = _sigmoid(hg)
        r = _heads(lambda o_h: jnp.broadcast_to(_rms_scale(o_h), o_h.shape), o)
        oh = o * r
        dn = dz * hg * sg
        dhg = dz * oh * ghv * _silu_grad(hg, sg)
        w = dn * ghv
        do = r * (w - oh * _heads(lambda t: jnp.broadcast_to(jnp.mean(t, axis=-1, keepdims=True), t.shape), oh * w))
        dgh_ref[...] += (dn * oh).reshape(HG_BLOCK // _SUBLANES, _SUBLANES, D_MODEL).sum(axis=0)

        qt_scr[...] = _mx(qt)
        kh_scr[...] = _mx(kh)
        v_scr[...] = _mx(v)
        do_scr[...] = _mx(do)
        el_scr[...] = jnp.exp(b_last)

        def intra(qc_h, kc_h, v_h, do_h):
            a = _mx(jnp.where(lower, _dot_nt(qc_h, kc_h), 0.0))
            da = _mx(jnp.where(lower, _dot_nt(do_h, v_h), 0.0))
            return _dot(da, kc_h), _dot_tn(da, qc_h), _dot_tn(a, do_h)

        dqc, dkc, dv_intra = _heads(intra, _mx(qc), _mx(kc), _mx(v), _mx(do))
        dv_scr[...] = dv_intra

        for c in reversed(range(n_chunks)):
            rs = slice(c * HG_CHUNK, (c + 1) * HG_CHUNK)
            for h in range(HEADS):
                cs = slice(h * HEAD_DIM, (h + 1) * HEAD_DIM)
                st = st_ref[c, h]
                ds_t = ds_scr[h]
                ds_m = _mx(ds_t)
                el = el_scr[c * HG_CHUNK:c * HG_CHUNK + 1, cs]
                dkh_scr[rs, cs] = _dot(v_scr[rs, cs], ds_m)
                dv_scr[rs, cs] += _dot_nt(kh_scr[rs, cs], ds_m)
                dbl = jnp.sum(ds_t * st.astype(F32), axis=0, keepdims=True) * el
                dbl_scr[rs, cs] = jnp.broadcast_to(dbl, (HG_CHUNK, HEAD_DIM))
                dqt_scr[rs, cs] = _dot(do_scr[rs, cs], st)
                ds_scr[h] = ds_t * el + _dot_tn(do_scr[rs, cs], qt_scr[rs, cs])

        dqt, dkh = dqt_scr[...], dkh_scr[...]
        dq = dqt * e_b + dqc * e_qc
        dk = dkh * e_kh + dkc * e_kc
        t_kh = dkh * kh
        db_rows = dqt * qt + dqc * qc - dkc * kc - t_kh
        dg = _exact_dot(upper, db_rows) + _exact_dot(same, t_kh) + dbl_scr[...]
        df = dg / f - dk
        dhf = df * (1.0 - lbv) * sf * (1.0 - sf)
        dlb_ref[...] += (df * (1.0 - sf)).reshape(HG_BLOCK // _SUBLANES, _SUBLANES, D_MODEL).sum(axis=0)
        dhq = dq * _silu_grad(hq, sq)
        dp_ref[...] = jnp.concatenate([dhq, dhf, dv_scr[...], dhg], axis=1).astype(dp_ref.dtype)

    rev = lambda b, j: b * nb + (nb - 1 - j)
    blk = lambda cb: pl.BlockSpec((HG_BLOCK, D_MODEL), functools.partial(lambda b, j, cb: (rev(b, j), cb), cb=cb))
    row = pl.BlockSpec((HG_BLOCK, D_MODEL), lambda b, j: (rev(b, j), 0))
    const = pl.BlockSpec((1, D_MODEL), lambda b, j: (0, 0))
    acc = pl.BlockSpec((_SUBLANES, D_MODEL), lambda b, j: (0, 0))
    big = lambda dt: pltpu.VMEM((HG_BLOCK, D_MODEL), dt)
    extra = ride if ride is not None else _NoRide
    dp, dlb, dgh, *exchanged = pl.pallas_call(
        body, name="hgrn_bwd",
        out_shape=[jax.ShapeDtypeStruct((rows, 4 * D_MODEL), _MXU_DTYPE),
                   jax.ShapeDtypeStruct((_SUBLANES, D_MODEL), F32),
                   jax.ShapeDtypeStruct((_SUBLANES, D_MODEL), F32)] + extra.out_shape,
        grid=(batch, nb),
        in_specs=[blk(CB_HQ), blk(CB_HF), blk(CB_HI), blk(CB_HG), row, row,
                  pl.BlockSpec((n_chunks, HEADS, HEAD_DIM, HEAD_DIM), lambda b, j: (rev(b, j), 0, 0, 0)),
                  const, const] + extra.in_specs,
        out_specs=[pl.BlockSpec((HG_BLOCK, 4 * D_MODEL), lambda b, j: (rev(b, j), 0)), acc, acc] + extra.out_specs,
        scratch_shapes=[pltpu.VMEM((HEADS, HEAD_DIM, HEAD_DIM), F32),
                        big(_MXU_DTYPE), big(_MXU_DTYPE), big(_MXU_DTYPE), big(_MXU_DTYPE),
                        big(F32), big(F32), big(F32), big(F32), big(F32)] + extra.scratch,
        compiler_params=_params(),
    )(p, p, p, p, o, dz, states, lb, gh, *extra.args)
    return (dp, dlb, dgh, *exchanged)


QK_DIM = 2 * HEAD_DIM
ATTN_TQ = 256
ATTN_KEY_CHUNK = 512


def _query_tiles(lp):
    return [(r0, min(ATTN_TQ, lp - r0)) for r0 in range(0, lp, ATTN_TQ)]


def _attn_fwd(q_cat, kv, kp, *, batch, lp):
    rows = batch * lp

    def body(q_ref, kn_ref, kp_ref, v_ref, o_ref, lse_ref):
        k_cat = jnp.concatenate([kn_ref[...], kp_ref[...]], axis=1)
        for r0, tq in _query_tiles(lp):
            q_t = q_ref[r0:r0 + tq, :]
            i = lax.broadcasted_iota(jnp.int32, (tq, tq), 0)
            j = lax.broadcasted_iota(jnp.int32, (tq, tq), 1)
            s_diag = jnp.where(j <= i, _dot_nt(q_t, k_cat[r0:r0 + tq]) * ATTN_SCALE, NEG_BIG)
            m = jnp.max(s_diag, axis=1, keepdims=True)
            if r0:
                s_past = _dot_nt(q_t, k_cat[0:r0]) * ATTN_SCALE
                m = jnp.maximum(m, jnp.max(s_past, axis=1, keepdims=True))
            p_diag = jnp.exp(s_diag - m)
            l = jnp.sum(p_diag, axis=1, keepdims=True)
            acc = _dot(_mx(p_diag), v_ref[r0:r0 + tq, :])
            if r0:
                p_past = jnp.exp(s_past - m)
                l = l + jnp.sum(p_past, axis=1, keepdims=True)
                acc = acc + _dot(_mx(p_past), v_ref[0:r0, :])
            o_ref[r0:r0 + tq, :] = (acc / l).astype(o_ref.dtype)
            lse_ref[r0:r0 + tq, :] = jnp.broadcast_to(m + jnp.log(l), (tq, HEAD_DIM))

    head_blk = pl.BlockSpec((lp, HEAD_DIM), lambda b, h: (b, h))
    return pl.pallas_call(
        body, name="attn_fwd",
        out_shape=[jax.ShapeDtypeStruct((rows, D_MODEL), _MXU_DTYPE),
                   jax.ShapeDtypeStruct((rows, D_MODEL), F32)],
        grid=(batch, HEADS),
        in_specs=[pl.BlockSpec((lp, QK_DIM), lambda b, h: (b, h)), head_blk,
                  pl.BlockSpec((lp, HEAD_DIM), lambda b, h: (b, 0)),
                  pl.BlockSpec((lp, HEAD_DIM), lambda b, h: (b, HEADS + h))],
        out_specs=[head_blk, head_blk],
        compiler_params=_params(),
    )(q_cat, kv, kp, kv)


def _attn_bwd(q_cat, kv, kp, do, o, lse, *, batch, lp, ride=None):
    rows = batch * lp

    def body(q_ref, kn_ref, kp_ref, v_ref, do_ref, o_ref, lse_ref, *rest):
        if ride is not None:
            rest, exchange = ride.split(rest, 4)
            ride.run((batch, HEADS), exchange)
        dq_ref, dkn_ref, dkp_ref, dv_ref, dk_acc, dv_acc = rest
        dk_acc[...] = jnp.zeros_like(dk_acc)
        dv_acc[...] = jnp.zeros_like(dv_acc)
        k_cat = jnp.concatenate([kn_ref[...], kp_ref[...]], axis=1)
        k_t = k_cat.T
        lane = lax.broadcasted_iota(jnp.int32, (_SUBLANES, HEAD_DIM), 1)
        lse_row = _exact_dot_nt(lane == 0, lse_ref[...])
        delta = _exact_dot_nt(lane >= 0, do_ref[...].astype(F32) * o_ref[...].astype(F32))
        for r0, tq in _query_tiles(lp):
            cols = slice(r0, r0 + tq)
            q_t_, do_t_ = q_ref[cols, :], do_ref[cols, :]
            lse_t, delta_t = lse_row[0:1, cols], delta[0:1, cols]
            chunks = [(c0, min(ATTN_KEY_CHUNK, r0 - c0), False) for c0 in range(0, r0, ATTN_KEY_CHUNK)] + [(r0, tq, True)]
            dq_t = jnp.zeros((QK_DIM, tq), F32)
            for c0, n, diagonal in chunks:
                keys = slice(c0, c0 + n)
                s = _dot_nt(k_cat[keys], q_t_) * ATTN_SCALE
                if diagonal:
                    jk = lax.broadcasted_iota(jnp.int32, (n, tq), 0)
                    iq = lax.broadcasted_iota(jnp.int32, (n, tq), 1)
                    s = jnp.where(jk <= iq, s, NEG_BIG)
                pexp = jnp.exp(s - lse_t)
                dp = _dot_nt(v_ref[keys, :], do_t_)
                ds = _mx(pexp * (dp - delta_t) * ATTN_SCALE)
                dk_acc[keys, :] += _dot(ds, q_t_)
                dv_acc[keys, :] += _dot(_mx(pexp), do_t_)
                dq_t = dq_t + _dot(k_t[:, keys], ds)
            dq_ref[cols, :] = dq_t.T

        dkn_ref[...] = dk_acc[:, 0:HEAD_DIM].astype(dkn_ref.dtype)
        dv_ref[...] = dv_acc[...].astype(dv_ref.dtype)

        @pl.when(pl.program_id(1) == 0)
        def _():
            dkp_ref[...] = jnp.zeros_like(dkp_ref)

        dkp_ref[...] += dk_acc[:, HEAD_DIM:]

    head_blk = pl.BlockSpec((lp, HEAD_DIM), lambda b, h: (b, h))
    cat_blk = pl.BlockSpec((lp, QK_DIM), lambda b, h: (b, h))
    shared_blk = pl.BlockSpec((lp, HEAD_DIM), lambda b, h: (b, 0))
    extra = ride if ride is not None else _NoRide
    return pl.pallas_call(
        body, name="attn_bwd",
        out_shape=[jax.ShapeDtypeStruct((rows, HEADS * QK_DIM), F32),
                   jax.ShapeDtypeStruct((rows, D_MODEL), _MXU_DTYPE),
                   jax.ShapeDtypeStruct((rows, HEAD_DIM), F32),
                   jax.ShapeDtypeStruct((rows, D_MODEL), _MXU_DTYPE)] + extra.out_shape,
        grid=(batch, HEADS),
        in_specs=[cat_blk, head_blk, shared_blk, pl.BlockSpec((lp, HEAD_DIM), lambda b, h: (b, HEADS + h)),
                  head_blk, head_blk, head_blk] + extra.in_specs,
        out_specs=[cat_blk, head_blk, shared_blk, head_blk] + extra.out_specs,
        scratch_shapes=[pltpu.VMEM((lp, QK_DIM), F32), pltpu.VMEM((lp, HEAD_DIM), F32)] + extra.scratch,
        compiler_params=_params(),
    )(q_cat, kv, kp, kv, do, o, lse, *extra.args)


def _all_gather(name, blocks):
    n = len(blocks)

    def body(*refs):
        x_refs, out_refs, (send_sems, recv_sems, local_sems) = refs[:n], refs[n:2 * n], refs[2 * n:]
        x, y, c = lax.axis_index("x"), lax.axis_index("y"), lax.axis_index("c")
        me, sibling = (x, y, c), (x, y, 1 - c)
        chips = [(1 - x, y), (x, 1 - y), (1 - x, 1 - y)]

        def slot(i, px, py, pc):
            return out_refs[i].at[4 * px + 2 * py + pc]

        def copy(i, k, blk, to, src=None):
            return pltpu.make_async_remote_copy(
                src_ref=slot(i, *blk) if src is None else src, dst_ref=slot(i, *blk),
                send_sem=send_sems.at[i, k], recv_sem=recv_sems.at[i, k],
                device_id=to, device_id_type=pl.DeviceIdType.MESH)

        mine = [pltpu.make_async_copy(x_refs[i], slot(i, *me), local_sems.at[i]) for i in range(n)]
        first = [copy(i, 0, me, sibling, src=x_refs[i]) for i in range(n)]
        first += [copy(i, 1 + j, me, (*chip, c), src=x_refs[i]) for i in range(n) for j, chip in enumerate(chips)]
        for cp in mine + first:
            cp.start()
        passed = []
        for i in range(n):
            for j, chip in enumerate(chips):
                copy(i, 1 + j, (*chip, c), me).wait_recv()
                passed.append(copy(i, 4 + j, (*chip, c), sibling))
                passed[-1].start()
        for i in range(n):
            copy(i, 0, sibling, me).wait_recv()
            for j, chip in enumerate(chips):
                copy(i, 4 + j, (*chip, 1 - c), me).wait_recv()
        for cp in first + passed:
            cp.wait_send()
        for cp in mine:
            cp.wait()

    return pl.pallas_call(
        body, name=name,
        out_shape=[jax.ShapeDtypeStruct((N_DEV, *b.shape), b.dtype) for b in blocks],
        in_specs=[pl.BlockSpec(memory_space=pl.ANY)] * n,
        out_specs=[pl.BlockSpec(memory_space=pl.ANY)] * n,
        scratch_shapes=[pltpu.SemaphoreType.DMA((n, 7)), pltpu.SemaphoreType.DMA((n, 7)),
                        pltpu.SemaphoreType.DMA((n,))],
    )(*blocks)


def _adamw_math(w, g, m, v):
    nm = ADAM_B1 * m + (1.0 - ADAM_B1) * g
    nv = ADAM_B2 * v + (1.0 - ADAM_B2) * (g * g)
    m_hat = nm / (1.0 - ADAM_B1 ** ADAM_STEP)
    v_hat = nv / (1.0 - ADAM_B2 ** ADAM_STEP)
    return -ADAM_LR * (m_hat / (jnp.sqrt(v_hat) + ADAM_EPS) + ADAM_WD * w), nm, nv


def _sum_adamw(name, parts, w, m, v):
    rows, cols = w.shape
    tr = rows // 4 if rows % 64 == 0 and rows * cols > (1 << 16) else rows

    def body(p_ref, w_ref, m_ref, v_ref, g_ref, d_ref, nm_ref, nv_ref):
        g = p_ref[0].astype(F32)
        for dev in range(1, N_DEV):
            g = g + p_ref[dev].astype(F32)
        g_ref[...] = g
        d_ref[...], nm_ref[...], nv_ref[...] = _adamw_math(w_ref[...], g, m_ref[...], v_ref[...])

    spec = pl.BlockSpec((tr, cols), lambda i: (i, 0))
    return pl.pallas_call(
        body, name=name,
        out_shape=[jax.ShapeDtypeStruct((rows, cols), F32)] * 4,
        grid=(rows // tr,),
        in_specs=[pl.BlockSpec((N_DEV, tr, cols), lambda i: (0, i, 0))] + [spec] * 3, out_specs=[spec] * 4,
        compiler_params=_params(),
    )(parts, w, m, v)


def _finish_vectors(gathered, lb, params, loss_parts):
    names = list(params)
    n = len(names)

    def body(*refs):
        g_refs, lb_ref, loss_ref = refs[:n], refs[n], refs[n + 1]
        wmv_refs = refs[n + 2:4 * n + 2]
        out_refs, loss_out = refs[4 * n + 2:-1], refs[-1]
        sq = loss_ref[0]
        for dev in range(1, N_DEV):
            sq = sq + loss_ref[dev]
        sq = jnp.sum(jnp.sum(sq, axis=0, keepdims=True), axis=1, keepdims=True)
        loss_out[...] = sq * (0.5 / D_MODEL)
        me = 4 * lax.axis_index("x") + 2 * lax.axis_index("y") + lax.axis_index("c")
        for i, name in enumerate(names):
            g_ref = g_refs[i]
            w_ref, m_ref, v_ref = wmv_refs[3 * i:3 * i + 3]
            if name == "meta_tokens":
                width = w_ref.shape[1]
                mine = pl.ds(pl.multiple_of(me * width, width), width)
                g = g_ref[0, :, mine]
                for dev in range(1, N_DEV):
                    g = g + g_ref[dev, :, mine]
            else:
                g = g_ref[0]
                for dev in range(1, N_DEV):
                    g = g + g_ref[dev]
                g = jnp.sum(g, axis=0, keepdims=True)
                if name == "hg_norm_g":
                    g = functools.reduce(jnp.add, [g[:, h * HEAD_DIM:(h + 1) * HEAD_DIM] for h in range(HEADS)])
                if name == "lb_logits":
                    lbv = lb_ref[...]
                    g = g * lbv * (1.0 - lbv)
                    g = jnp.concatenate([g, -g], axis=0)
            outs = (g, *_adamw_math(w_ref[...], g, m_ref[...], v_ref[...]))
            for ref, val in zip(out_refs[4 * i:4 * i + 4], outs, strict=True):
                ref[...] = val

    args = [gathered[k] for k in names] + [lb, loss_parts] + [t for k in names for t in params[k]]
    res = pl.pallas_call(
        body, name="finish_vectors",
        out_shape=[jax.ShapeDtypeStruct(params[k][0].shape, F32) for k in names for _ in range(4)]
        + [jax.ShapeDtypeStruct((1, 1), F32)],
        compiler_params=_params(),
    )(*args)
    return {k: res[4 * i:4 * i + 4] for i, k in enumerate(names)}, res[-1].reshape(())


def _swap_halves(t):
    half = t.shape[-1] // 2
    return jnp.concatenate([t[..., half:], t[..., :half]], axis=-1)


def _pad_last(t, width):
    return jnp.concatenate([t, jnp.zeros(t.shape[:-1] + (width - t.shape[-1],), t.dtype)], axis=-1)


def _rope_tables(lp):
    pos = jnp.arange(lp, dtype=F32)
    inv_freq = 1.0 / (ROPE_THETA ** (jnp.arange(0, ROPE_DIM, 2, dtype=F32) / ROPE_DIM))
    ang = pos[:, None] * inv_freq[None, :]
    cos, sin = jnp.cos(ang), jnp.sin(ang)
    c128 = _pad_last(jnp.concatenate([cos, cos], axis=1), HEAD_DIM)
    s128 = _pad_last(jnp.concatenate([-sin, sin], axis=1), HEAD_DIM)
    return c128, s128


def _forward_backward(x, target, meta, w, small, *, lp, comm=None):
    batch, seq, d = x.shape
    rows = batch * lp
    tr = 272 if lp % 272 == 0 else 128
    tm = lp // 2
    bf = _MXU_DTYPE
    rw = functools.partial(_rowwise, rows=rows, tr=tr, lp=lp)

    c128, s128 = _rope_tables(lp)
    cq_tab, sq_tab = jnp.tile(c128, (1, HEADS)), jnp.tile(s128, (1, HEADS))
    t_idx = jnp.arange(lp)
    real = jnp.broadcast_to(((t_idx >= N_META) & (t_idx < N_META + seq)).astype(F32)[:, None], (lp, _LANES))

    lb_logits = small["lb_logits"]
    lb = jax.nn.softmax(lb_logits, axis=0)[0:1]
    gh = jnp.tile(small["hg_norm_g"], (1, HEADS))

    h0 = _assemble("assemble_x", x, meta, lp)
    tgt = _assemble("assemble_target", target, jnp.zeros_like(meta), lp)

    (u1,), _ = rw("norm_mix_pre", lambda h, g: ([h * _rms_scale(h) * g], []),
                  ins=[("row", h0, d, 0), ("const", small["mix_pre_g"])], outs=[(d, bf)])
    p = _matmul("proj_in", u1, w["w_in"], out_dtype=F32, tm=tm, tn=1024, tk=1024)

    if comm is None:
        o_hg, z_a, states = _hgrn_fwd(p, lb, gh, batch=batch, lp=lp)
    else:
        o_hg, z_a, states, *gathered = _hgrn_fwd(p, lb, gh, batch=batch, lp=lp, ride=_Ride(comm.rest_payloads, True))
        w = {**w, **comm.rest_weights(gathered)}
    received = []
    scatter = lambda names: _Ride(comm.grad_parts(names, grads), False) if comm is not None else None
    y_a = _matmul("proj_hg_o", z_a, w["w_hg_o"], out_dtype=F32, tm=tm, tn=1024, tk=1024)

    def mla_pre(pc, gq, gkv, ct, st):
        cq, ckv = pc[:, 0:Q_LORA], pc[:, Q_LORA:Q_LORA + KV_LORA]
        kpe, kpe_sw = pc[:, 512:640], pc[:, 640:768]
        return [cq * _rms_scale(cq) * gq, ckv * _rms_scale(ckv) * gkv, kpe * ct + kpe_sw * st], []

    (cqn, ckvn, kp), _ = rw("mla_pre", mla_pre,
                            ins=[("row", p, 1024, CB_C), ("const", small["q_a_norm_g"]),
                                 ("const", small["kv_a_norm_g"]), ("pos", c128), ("pos", s128)],
                            outs=[(Q_LORA, bf), (KV_LORA, bf), (HEAD_DIM, bf)])
    qf = _matmul("proj_q_b", cqn, w["w_q"], out_dtype=F32, tm=tm, tn=1024, tk=Q_LORA)
    def rope_q(a, pe, pes, ct, st):
        roped = pe * ct + pes * st
        hs = lambda t, h: t[:, h * HEAD_DIM:(h + 1) * HEAD_DIM]
        return [jnp.concatenate([t for h in range(HEADS) for t in (hs(a, h), hs(roped, h))], axis=1)], []

    (q_cat,), _ = rw("rope_q", rope_q,
                     ins=[("row", qf, 1024, 0), ("row", qf, 1024, 1), ("row", qf, 1024, 2),
                          ("pos", cq_tab), ("pos", sq_tab)], outs=[(HEADS * QK_DIM, bf)])
    kv = _matmul("proj_kv_b", ckvn, w["w_kv"], out_dtype=bf, tm=tm, tn=1024, tk=KV_LORA)
    o_at, lse = _attn_fwd(q_cat, kv, kp, batch=batch, lp=lp)
    y_b = _matmul("proj_mla_o", o_at, w["w_mla_o"], out_dtype=F32, tm=tm, tn=1024, tk=1024)

    def merge(pa, pb, ya, yb, bg):
        ga, gb = _sigmoid(pa + bg[:, :d]), _sigmoid(pb + bg[:, d:])
        return [ga * ya + gb * yb], []

    (mix,), _ = rw("merge", merge,
                   ins=[("row", p, 1024, CB_GA), ("row", p, 1024, CB_GB), ("row", y_a, d, 0), ("row", y_b, d, 0),
                        ("const", small["b_gate"])], outs=[(d, bf)])
    mixed = _matmul("proj_out", mix, w["w_out"], out_dtype=F32, tm=tm, tn=1024, tk=1024)

    def post_mix(mx_, h, g2, g3):
        h1_ = h + mx_ * _rms_scale(mx_) * g2
        return [h1_, h1_ * _rms_scale(h1_) * g3], []

    (h1, u2), _ = rw("post_mix", post_mix,
                     ins=[("row", mixed, d, 0), ("row", h0, d, 0), ("const", small["mix_post_g"]),
                          ("const", small["ffn_pre_g"])], outs=[(d, F32), (d, bf)])
    act, gt, up = _ffn_in_swiglu(u2, w["w_ffn_in"], tm=tm, tn=1408)
    fo = _matmul("ffn_out", act, w["w_ffn_out"], out_dtype=F32, tm=tm, tn=1024, tk=1408)

    def post_ffn(fo_, h1_, t_, mask, g4):
        r = _rms_scale(fo_)
        h2 = h1_ + fo_ * r * g4
        err = (h2 - t_) * mask[:, 0:1]
        dh2 = err * (1.0 / d)
        dfo, dg4 = _rms_bwd(fo_, g4, dh2)
        return [dh2, dfo], [err * err, dg4]

    (dh2, dfo), (loss_vec, dg_ffn_post) = rw(
        "post_ffn_loss", post_ffn,
        ins=[("row", fo, d, 0), ("row", h1, d, 0), ("row", tgt, d, 0), ("pos", real), ("const", small["ffn_post_g"])],
        outs=[(d, F32), (d, bf)], accs=[d, d])
    loss = (0.5 / d) * jnp.sum(loss_vec)

    grads = {}
    dw_dt = F32 if comm is None else _WIRE_DTYPE
    dgt, dup = _d_ffn_out_swiglu(dfo, w["w_ffn_out_t"], gt, up, tm=tm, tn=1408)
    grads["w_ffn_out"] = _matmul_tn("dw_ffn_out", act, dfo, tk=1408, tn=1024, tr=tm, out_dtype=dw_dt)
    du2 = _matmul("d_ffn_in_gate", dgt, w["w_ffn_in_t"][:FFN_HIDDEN], out_dtype=F32, tm=tm, tn=1024, tk=1408)
    du2 = _matmul("d_ffn_in_up", dup, w["w_ffn_in_t"][FFN_HIDDEN:], out_dtype=F32, tm=tm, tn=1024, tk=1408, c_in=du2)
    grads["w_ffn_in"] = jnp.concatenate([_matmul_tn("dw_ffn_in_gate", u2, dgt, tk=1024, tn=1408, tr=tm),
                                         _matmul_tn("dw_ffn_in_up", u2, dup, tk=1024, tn=1408, tr=tm)], axis=1)

    def post_mix_bwd(du2_, h1_, dh2_, mx_, g3, g2):
        dx, dg3 = _rms_bwd(h1_, g3, du2_)
        dh1_ = dh2_ + dx
        dmx, dg2 = _rms_bwd(mx_, g2, dh1_)
        return [dh1_, dmx], [dg3, dg2]

    (dh1, dmixed), (dg_ffn_pre, dg_mix_post) = rw(
        "post_mix_bwd", post_mix_bwd,
        ins=[("row", du2, d, 0), ("row", h1, d, 0), ("row", dh2, d, 0), ("row", mixed, d, 0),
             ("const", small["ffn_pre_g"]), ("const", small["mix_post_g"])],
        outs=[(d, F32), (d, bf)], accs=[d, d])
    dmix = _matmul("d_proj_out", dmixed, w["w_out_t"], out_dtype=F32, tm=tm, tn=1024, tk=1024)
    grads["w_out"] = _matmul_tn("dw_out", mix, dmixed, tk=1024, tn=1024, tr=tm, out_dtype=dw_dt)

    def merge_bwd(dm, pa, pb, ya, yb, bg):
        ga, gb = _sigmoid(pa + bg[:, :d]), _sigmoid(pb + bg[:, d:])
        dpg = jnp.concatenate([dm * ya * ga * (1.0 - ga), dm * yb * gb * (1.0 - gb)], axis=1)
        return [dpg, dm * ga, dm * gb], [dpg]

    (dpg, dya, dyb), (db_gate,) = rw(
        "merge_bwd", merge_bwd,
        ins=[("row", dmix, d, 0), ("row", p, 1024, CB_GA), ("row", p, 1024, CB_GB), ("row", y_a, d, 0),
             ("row", y_b, d, 0), ("const", small["b_gate"])],
        outs=[(2 * d, bf), (d, bf), (d, bf)], accs=[2 * d])
    dz_a = _matmul("d_proj_hg_o", dya, w["w_hg_o_t"], out_dtype=F32, tm=tm, tn=1024, tk=1024)
    grads["w_hg_o"] = _matmul_tn("dw_hg_o", z_a, dya, tk=1024, tn=1024, tr=tm, out_dtype=dw_dt)
    do_at = _matmul("d_proj_mla_o", dyb, w["w_mla_o_t"], out_dtype=bf, tm=tm, tn=1024, tk=1024)
    grads["w_mla_o"] = _matmul_tn("dw_mla_o", o_at, dyb, tk=1024, tn=1024, tr=tm, out_dtype=dw_dt)

    dph, dlb, dgh, *got = _hgrn_bwd(p, o_hg, dz_a, states, lb, gh, batch=batch, lp=lp,
                                    ride=scatter(_GRAD_GROUPS[0]))
    received.append(got)

    dq_cat, dkn, dkp, dvv, *got = _attn_bwd(q_cat, kv, kp, do_at, o_at, lse, batch=batch, lp=lp,
                                            ride=scatter(_GRAD_GROUPS[1]))
    received.append(got)

    def rope_q_bwd(dq, ct, st):
        hs = lambda half: jnp.concatenate(
            [dq[:, h * QK_DIM + half * HEAD_DIM:h * QK_DIM + (half + 1) * HEAD_DIM] for h in range(HEADS)], axis=1)
        dpe = hs(1)
        return [jnp.concatenate([hs(0), dpe * ct, dpe * st], axis=1)], []

    (dqf,), _ = rw("rope_q_bwd", rope_q_bwd,
                   ins=[("row", dq_cat, HEADS * QK_DIM, 0), ("pos", cq_tab), ("pos", sq_tab)],
                   outs=[(3 * d, bf)])
    dcqn = _matmul("d_proj_q_b", dqf, w["w_q_t"], out_dtype=F32, tm=tm, tn=Q_LORA, tk=1024)
    grads["w_q"] = _matmul_tn("dw_q_b", cqn, dqf, tk=Q_LORA, tn=1024, tr=tm)
    dckvn = _matmul("d_proj_k_b", dkn, w["w_k_t"], out_dtype=F32, tm=tm, tn=KV_LORA, tk=1024)
    dckvn = _matmul("d_proj_v_b", dvv, w["w_v_t"], out_dtype=F32, tm=tm, tn=KV_LORA, tk=1024, c_in=dckvn)
    grads["w_k"] = _matmul_tn("dw_k_b", ckvn, dkn, tk=KV_LORA, tn=1024, tr=tm)
    grads["w_v"] = _matmul_tn("dw_v_b", ckvn, dvv, tk=KV_LORA, tn=1024, tr=tm)

    def mla_pre_bwd(pc, dq_, dkv_, dkp_, gq, gkv, ct, st):
        cq, ckv = pc[:, 0:Q_LORA], pc[:, Q_LORA:Q_LORA + KV_LORA]
        dcq, dgq = _rms_bwd(cq, gq, dq_)
        dckv, dgkv = _rms_bwd(ckv, gkv, dkv_)
        dpc = jnp.concatenate([dcq, dckv, dkp_ * ct, dkp_ * st, jnp.zeros((pc.shape[0], 256), F32)], axis=1)
        return [dpc], [dgq, dgkv]

    (dpc,), (dg_q, dg_kv) = rw(
        "mla_pre_bwd", mla_pre_bwd,
        ins=[("row", p, 1024, CB_C), ("row", dcqn, Q_LORA, 0), ("row", dckvn, KV_LORA, 0), ("row", dkp, HEAD_DIM, 0),
             ("const", small["q_a_norm_g"]), ("const", small["kv_a_norm_g"]), ("pos", c128), ("pos", s128)],
        outs=[(1024, bf)], accs=[Q_LORA, KV_LORA])

    grads["w_in"] = jnp.concatenate([
        _matmul_tn("dw_in_h", u1, dph, tk=1024, tn=1024, tr=tm),
        _matmul_tn("dw_in_c", u1, dpc, tk=1024, tn=1024, tr=tm),
        _matmul_tn("dw_in_g", u1, dpg, tk=1024, tn=1024, tr=tm)], axis=1)
    du1 = _matmul("d_proj_in_h", dph, w["w_in_t"][:4096], out_dtype=F32, tm=tm, tn=1024, tk=1024,
                  ride=scatter(_GRAD_GROUPS[2]))
    if comm is not None:
        du1, *got = du1
        received.append(got)
    du1 = _matmul("d_proj_in_c", dpc, w["w_in_t"][4096:5120], out_dtype=F32, tm=tm, tn=1024, tk=1024, c_in=du1)
    du1 = _matmul("d_proj_in_g", dpg, w["w_in_t"][5120:], out_dtype=F32, tm=tm, tn=1024, tk=1024, c_in=du1)

    def pre_bwd(du, h, dh, g1):
        dx, dg1 = _rms_bwd(h, g1, du)
        return [dh + dx], [dg1]

    (dh0,), (dg_mix_pre,) = rw("norm_mix_pre_bwd", pre_bwd,
                               ins=[("row", du1, d, 0), ("row", h0, d, 0), ("row", dh1, d, 0), ("const", small["mix_pre_g"])],
                               outs=[(d, F32)], accs=[d])
    grad_x = dh0.reshape(batch, lp, d)[:, N_META:N_META + seq]
    partial = {"meta_tokens": _meta_grad(dh0, batch, lp), "lb_logits": dlb, "b_gate": db_gate, "hg_norm_g": dgh,
               "q_a_norm_g": dg_q, "kv_a_norm_g": dg_kv, "mix_pre_g": dg_mix_pre, "mix_post_g": dg_mix_post,
               "ffn_pre_g": dg_ffn_pre, "ffn_post_g": dg_ffn_post, "loss": loss_vec}
    return loss, grad_x, grads, partial, lb, received


_BIG = ["w_in", "w_hg_o", "w_q_b", "w_kv_b", "w_mla_o", "w_out", "w_ffn_in", "w_ffn_out"]
_COLUMN_SHARDED = {"w_in", "w_q_b", "w_kv_b", "w_ffn_in"}
_GRAD_GROUPS = [["w_ffn_in", "w_ffn_out"], ["w_out", "w_hg_o", "w_mla_o"], ["w_in", "w_q_b", "w_kv_b"]]
_SMALL = ["b_gate", "lb_logits", "hg_norm_g", "q_a_norm_g", "kv_a_norm_g", "mix_pre_g", "mix_post_g",
          "ffn_pre_g", "ffn_post_g"]


def _gathered_matrix(name, t):
    _, k, n = t.shape
    if name in _COLUMN_SHARDED:
        return t.transpose(1, 0, 2).reshape(k, N_DEV * n)
    return t.reshape(N_DEV * k, n)


def _scatter_layout(name, full):
    kk, nn = full.shape
    if name in _COLUMN_SHARDED:
        t = full.reshape(kk, N_DEV, nn // N_DEV).transpose(1, 0, 2)
    else:
        t = full.reshape(N_DEV, kk // N_DEV, nn)
    return t.astype(_WIRE_DTYPE)


def _model_w_in(wi):
    z = lambda *s: jnp.zeros(s, wi.dtype)
    kpe = wi[:, 4608:4672]
    c_blk = jnp.concatenate([wi[:, 4096:4608], kpe, z(1024, 64), _swap_halves(kpe), z(1024, 64), z(1024, 256)], axis=1)
    w_in = jnp.concatenate([wi[:, :4096], c_blk, wi[:, 4672:]], axis=1).astype(_MXU_DTYPE)
    return {"w_in": w_in, "w_in_t": w_in.T}


def _model_weights(full):
    return {**_model_w_in(full["w_in"]), **_model_rest(full)}


def _model_rest(full):
    wq3 = full["w_q_b"].reshape(Q_LORA, HEADS, HEAD_DIM + ROPE_DIM)
    pe = wq3[:, :, HEAD_DIM:]
    w_q = jnp.concatenate([wq3[:, :, :HEAD_DIM].reshape(Q_LORA, -1),
                           _pad_last(pe, HEAD_DIM).reshape(Q_LORA, -1),
                           _pad_last(_swap_halves(pe), HEAD_DIM).reshape(Q_LORA, -1)], axis=1)
    wkv3 = full["w_kv_b"].reshape(KV_LORA, HEADS, 2 * HEAD_DIM)
    w_k = wkv3[:, :, :HEAD_DIM].reshape(KV_LORA, -1)
    w_v = wkv3[:, :, HEAD_DIM:].reshape(KV_LORA, -1)
    w = {"w_q": w_q, "w_kv": jnp.concatenate([w_k, w_v], axis=1),
         "w_hg_o": full["w_hg_o"], "w_mla_o": full["w_mla_o"], "w_out": full["w_out"],
         "w_ffn_in": full["w_ffn_in"], "w_ffn_out": full["w_ffn_out"]}
    for n in ["w_q", "w_hg_o", "w_mla_o", "w_out", "w_ffn_in", "w_ffn_out"]:
        w[n + "_t"] = w[n].T
    w["w_k_t"], w["w_v_t"] = w_k.T, w_v.T
    return {k: v.astype(_MXU_DTYPE) for k, v in w.items()}


def _reference_layout_grad(name, g):
    if name == "w_in":
        gi = g["w_in"]
        d_kpe = gi[:, 4608:4672] + _swap_halves(gi[:, 4736:4800])
        return jnp.concatenate([gi[:, :4608], d_kpe, gi[:, 5120:]], axis=1)
    if name == "w_q_b":
        gq = g["w_q"]
        d_pe = (gq[:, 1024:2048].reshape(Q_LORA, HEADS, HEAD_DIM)[:, :, :ROPE_DIM]
                + _swap_halves(gq[:, 2048:].reshape(Q_LORA, HEADS, HEAD_DIM)[:, :, :ROPE_DIM]))
        return jnp.concatenate([gq[:, :1024].reshape(Q_LORA, HEADS, HEAD_DIM), d_pe], axis=2).reshape(Q_LORA, -1)
    if name == "w_kv_b":
        return jnp.concatenate([g["w_k"].reshape(KV_LORA, HEADS, HEAD_DIM),
                                g["w_v"].reshape(KV_LORA, HEADS, HEAD_DIM)], axis=2).reshape(KV_LORA, -1)
    return g[name]


def _reference_layout_grads(g):
    return {n: _reference_layout_grad(n, g) for n in _BIG}


class _Comm:
    def __init__(self, shard):
        self.rest_payloads = [shard[n].astype(_WIRE_DTYPE) for n in _BIG[1:]]

    def rest_weights(self, gathered):
        return _model_rest({n: _gathered_matrix(n, t) for n, t in zip(_BIG[1:], gathered, strict=True)})

    def grad_parts(self, names, g):
        return [_scatter_layout(n, _reference_layout_grad(n, g)) for n in names]


def kernel(x, meta_tokens, w_in, b_gate, lb_logits, hg_norm_g, w_hg_o, q_a_norm_g, w_q_b, kv_a_norm_g, w_kv_b, w_mla_o, w_out, mix_pre_g, mix_post_g, ffn_pre_g, ffn_post_g, w_ffn_in, w_ffn_out, loss_target, m_meta_tokens, m_w_in, m_b_gate, m_lb_logits, m_hg_norm_g, m_w_hg_o, m_q_a_norm_g, m_w_q_b, m_kv_a_norm_g, m_w_kv_b, m_w_mla_o, m_w_out, m_mix_pre_g, m_mix_post_g, m_ffn_pre_g, m_ffn_post_g, m_w_ffn_in, m_w_ffn_out, v_meta_tokens, v_w_in, v_b_gate, v_lb_logits, v_hg_norm_g, v_w_hg_o, v_q_a_norm_g, v_w_q_b, v_kv_a_norm_g, v_w_kv_b, v_w_mla_o, v_w_out, v_mix_pre_g, v_mix_post_g, v_ffn_pre_g, v_ffn_post_g, v_w_ffn_in, v_w_ffn_out):
    args = dict(locals())
    batch, seq, d = x.shape
    lp = -(-(N_META + seq) // _LANES) * _LANES
    weight_names = ["meta_tokens", "w_in", "b_gate", "lb_logits", "hg_norm_g", "w_hg_o", "q_a_norm_g", "w_q_b",
                    "kv_a_norm_g", "w_kv_b", "w_mla_o", "w_out", "mix_pre_g", "mix_post_g", "ffn_pre_g",
                    "ffn_post_g", "w_ffn_in", "w_ffn_out"]
    shard = {n: args[n].reshape(args[n].shape[-2:]) for n in _BIG}
    comm = _Comm(shard)

    w_in_all, meta_all = _all_gather("gather_first", [shard["w_in"].astype(_WIRE_DTYPE), meta_tokens])
    w_first = _model_w_in(_gathered_matrix("w_in", w_in_all))
    meta_full = meta_all.transpose(1, 0, 2).reshape(N_META, d)
    small = {n: args[n] for n in _SMALL}

    _, grad_x, _, partial, lb, received = _forward_backward(x, loss_target, meta_full, w_first, small, lp=lp, comm=comm)
    out = {}
    for names, bufs in zip(_GRAD_GROUPS, received, strict=True):
        for n, buf in zip(names, bufs, strict=True):
            two_d = lambda t: t.reshape(t.shape[-2:])
            res = _sum_adamw("adamw_" + n, buf, shard[n], two_d(args["m_" + n]), two_d(args["v_" + n]))
            out[n] = [t.reshape(args[n].shape) for t in res]

    vec_names = _SMALL + ["meta_tokens"]
    *gathered, loss_parts = _all_gather("gather_vectors", [partial[n] for n in vec_names + ["loss"]])
    finished, loss = _finish_vectors(dict(zip(vec_names, gathered, strict=True)), lb,
                                     {n: (args[n], args["m_" + n], args["v_" + n]) for n in vec_names}, loss_parts)
    out.update(finished)
    return (loss, grad_x, *[out[n][i] for i in range(4) for n in weight_names])
```

```python
import functools

import jax
import jax.numpy as jnp
from jax import lax
from jax.experimental import pallas as pl
from jax.experimental.pallas import tpu as pltpu

F32 = jnp.float32
_MXU_DTYPE = jnp.bfloat16
_WIRE_DTYPE = jnp.bfloat16
_VMEM_LIMIT_BYTES = 56 * 1024 * 1024
_LANES = 128
_SUBLANES = 8

N_DEV = 8
N_META = 16
NORM_EPS = 1e-6
HEADS = 8
HEAD_DIM = 128
ROPE_DIM = 64
HG_CHUNK = 16
HG_BLOCK = 128
ROPE_THETA = 10000.0
D_MODEL = 1024
Q_LORA = 256
KV_LORA = 256
FFN_HIDDEN = 2816
ATTN_SCALE = (HEAD_DIM + ROPE_DIM) ** -0.5
NEG_BIG = -1e30

ADAM_LR = 0.001
ADAM_B1 = 0.9
ADAM_B2 = 0.999
ADAM_EPS = 1e-08
ADAM_WD = 0.01
ADAM_STEP = 10

CB_HQ, CB_HF, CB_HI, CB_HG, CB_C, CB_GA, CB_GB = range(7)
IN_COLS_PADDED = 7 * 1024


def _params(**kw):
    return pltpu.CompilerParams(vmem_limit_bytes=_VMEM_LIMIT_BYTES, **kw)


def _dot(a, b):
    return lax.dot_general(a, b, (((1,), (0,)), ((), ())), preferred_element_type=F32)


def _dot_nt(a, b):
    return lax.dot_general(a, b, (((1,), (1,)), ((), ())), preferred_element_type=F32)


def _dot_tn(a, b):
    return lax.dot_general(a, b, (((0,), (0,)), ((), ())), preferred_element_type=F32)


def _mx(x):
    return x.astype(_MXU_DTYPE)


def _exact_dot(m01, x, dot=_dot):
    if _MXU_DTYPE == jnp.float32:
        return dot(m01.astype(F32), x)
    m = m01.astype(jnp.bfloat16)
    x1 = x.astype(jnp.bfloat16)
    r1 = x - x1.astype(F32)
    x2 = r1.astype(jnp.bfloat16)
    x3 = (r1 - x2.astype(F32)).astype(jnp.bfloat16)
    return dot(m, x1) + dot(m, x2) + dot(m, x3)


def _exact_dot_nt(m01, x):
    return _exact_dot(m01, x, dot=_dot_nt)


def _sigmoid(x):
    return jax.nn.sigmoid(x)


def _silu_grad(x, s):
    return s * (1.0 + x * (1.0 - s))


def _rms_scale(x):
    return lax.rsqrt(jnp.mean(x * x, axis=-1, keepdims=True) + NORM_EPS)


def _rms_bwd(x, g, dy):
    r = _rms_scale(x)
    xh = x * r
    w = dy * g
    dx = r * (w - xh * jnp.mean(xh * w, axis=-1, keepdims=True))
    return dx, dy * xh


def _heads(fn, *arrays):
    outs = [fn(*[a[:, h * HEAD_DIM:(h + 1) * HEAD_DIM] for a in arrays]) for h in range(HEADS)]
    if isinstance(outs[0], tuple):
        return tuple(jnp.concatenate([o[i] for o in outs], axis=1) for i in range(len(outs[0])))
    return jnp.concatenate(outs, axis=1)


class _Ride:
    def __init__(self, payloads, gather):
        self.gather, self.args, self.n = gather, list(payloads), len(payloads)
        self.in_specs = [pl.BlockSpec(memory_space=pl.ANY)] * self.n
        self.out_shape = [jax.ShapeDtypeStruct((N_DEV, *p.shape[-2:]), p.dtype) for p in payloads]
        self.out_specs = [pl.BlockSpec(memory_space=pl.ANY)] * self.n
        self.scratch = [pltpu.SemaphoreType.DMA((self.n, N_DEV - 1)), pltpu.SemaphoreType.DMA((self.n, N_DEV - 1)),
                        pltpu.SemaphoreType.DMA((self.n,))]

    def split(self, rest, n_outs):
        n = self.n
        mine = (rest[:n], rest[n + n_outs:2 * n + n_outs], rest[-3:])
        return rest[n:n + n_outs] + rest[2 * n + n_outs:-3], mine

    def _copies(self, p_refs, out_refs, sems):
        send_sems, recv_sems, local_sems = sems
        x, y, c = lax.axis_index("x"), lax.axis_index("y"), lax.axis_index("c")
        me = 4 * x + 2 * y + c
        copies = []
        for i, (p_ref, out_ref) in enumerate(zip(p_refs, out_refs, strict=True)):
            part = (lambda j, p_ref=p_ref: p_ref) if self.gather else (lambda j, p_ref=p_ref: p_ref.at[j])
            copies.append(pltpu.make_async_copy(part(me), out_ref.at[me], local_sems.at[i]))
            for k in range(1, N_DEV):
                px, py, pc = x ^ (k >> 2), y ^ ((k >> 1) & 1), c ^ (k & 1)
                copies.append(pltpu.make_async_remote_copy(
                    src_ref=part(4 * px + 2 * py + pc), dst_ref=out_ref.at[me],
                    send_sem=send_sems.at[i, k - 1], recv_sem=recv_sems.at[i, k - 1],
                    device_id=(px, py, pc), device_id_type=pl.DeviceIdType.MESH))
        return copies

    def run(self, grid, refs):
        ids = [pl.program_id(i) for i in range(len(grid))]
        first = functools.reduce(jnp.logical_and, [i == 0 for i in ids])
        last = functools.reduce(jnp.logical_and, [i == g - 1 for i, g in zip(ids, grid)])

        @pl.when(first)
        def _():
            for cp in self._copies(*refs):
                cp.start()

        @pl.when(last)
        def _():
            for cp in self._copies(*refs):
                cp.wait()


class _NoRide:
    in_specs, out_shape, out_specs, scratch, args = [], [], [], [], []


def _matmul(name, a, b, *, out_dtype, tm, tn, tk, c_in=None, ride=None):
    m, k = a.shape
    _, n = b.shape
    assert m % tm == 0 and n % tn == 0 and k % tk == 0, (name, a.shape, b.shape, tm, tn, tk)
    nk = k // tk
    has_c = c_in is not None
    grid = (n // tn, m // tm, nk)
    n_in = 2 + has_c

    def body(*refs):
        a_ref, b_ref = refs[0], refs[1]
        c_ref = refs[2] if has_c else None
        rest = refs[n_in:]
        if ride is not None:
            rest, exchange = ride.split(rest, 1)
            ride.run(grid, exchange)
        o_ref = rest[0]
        acc_ref = rest[1] if nk > 1 else None

        def finish(r):
            if has_c:
                r = r + c_ref[...]
            o_ref[...] = r.astype(o_ref.dtype)

        if nk == 1:
            finish(_dot(a_ref[...], b_ref[...]))
        else:
            kk = pl.program_id(2)

            @pl.when(kk == 0)
            def _():
                acc_ref[...] = jnp.zeros_like(acc_ref)

            acc_ref[...] += _dot(a_ref[...], b_ref[...])

            @pl.when(kk == nk - 1)
            def _():
                finish(acc_ref[...])

    in_specs = [pl.BlockSpec((tm, tk), lambda j, i, kk: (i, kk)),
                pl.BlockSpec((tk, tn), lambda j, i, kk: (kk, j))]
    args = [a, b]
    aliases = {}
    if has_c:
        in_specs.append(pl.BlockSpec((tm, tn), lambda j, i, kk: (i, j)))
        args.append(c_in)
        aliases = {2: 0}
    out_shape = [jax.ShapeDtypeStruct((m, n), out_dtype)]
    out_specs = [pl.BlockSpec((tm, tn), lambda j, i, kk: (i, j))]
    scratch = [pltpu.VMEM((tm, tn), F32)] if nk > 1 else []
    if ride is not None:
        in_specs, args = in_specs + ride.in_specs, args + ride.args
        out_shape, out_specs, scratch = out_shape + ride.out_shape, out_specs + ride.out_specs, scratch + ride.scratch
    res = pl.pallas_call(
        body, name=name, out_shape=out_shape, grid=grid, in_specs=in_specs, out_specs=out_specs,
        scratch_shapes=scratch, input_output_aliases=aliases, compiler_params=_params(),
    )(*args)
    return res[0] if ride is None else res


def _matmul_segments(name, a_list, b, *, out_dtype, tm, tn, tk, ride=None):
    m = a_list[0].shape[0]
    k, n = b.shape
    steps = [a.shape[1] // tk for a in a_list]
    offs = [sum(steps[:s]) for s in range(len(steps))]
    nk = sum(steps)
    assert nk * tk == k and m % tm == 0 and n % tn == 0 and all(a.shape[1] % tk == 0 for a in a_list), name
    grid = (n // tn, m // tm, nk)
    n_seg = len(a_list)

    def body(*refs):
        a_refs, b_ref, rest = refs[:n_seg], refs[n_seg], refs[n_seg + 1:]
        if ride is not None:
            rest, exchange = ride.split(rest, 1)
            ride.run(grid, exchange)
        o_ref, acc_ref = rest
        kk = pl.program_id(2)

        @pl.when(kk == 0)
        def _():
            acc_ref[...] = jnp.zeros_like(acc_ref)

        for s in range(n_seg):
            @pl.when((kk >= offs[s]) & (kk < offs[s] + steps[s]))
            def _(s=s):
                acc_ref[...] += _dot(a_refs[s][...], b_ref[...])

        @pl.when(kk == nk - 1)
        def _():
            o_ref[...] = acc_ref[...].astype(o_ref.dtype)

    seg_spec = lambda s: pl.BlockSpec(
        (tm, tk), functools.partial(lambda j, i, kk, off, ns: (i, jnp.clip(kk - off, 0, ns - 1)), off=offs[s], ns=steps[s]))
    in_specs = [seg_spec(s) for s in range(n_seg)] + [pl.BlockSpec((tk, tn), lambda j, i, kk: (kk, j))]
    args = list(a_list) + [b]
    out_shape = [jax.ShapeDtypeStruct((m, n), out_dtype)]
    out_specs = [pl.BlockSpec((tm, tn), lambda j, i, kk: (i, j))]
    scratch = [pltpu.VMEM((tm, tn), F32)]
    if ride is not None:
        in_specs, args = in_specs + ride.in_specs, args + ride.args
        out_shape, out_specs, scratch = out_shape + ride.out_shape, out_specs + ride.out_specs, scratch + ride.scratch
    res = pl.pallas_call(
        body, name=name, out_shape=out_shape, grid=grid, in_specs=in_specs, out_specs=out_specs,
        scratch_shapes=scratch, compiler_params=_params(),
    )(*args)
    return res[0] if ride is None else res


def _matmul_tn(name, x, dy, *, tk, tn, tr, out_dtype=F32):
    r, k = x.shape
    _, n = dy.shape
    assert r % tr == 0 and k % tk == 0 and n % tn == 0, (name, x.shape, dy.shape)
    n_r = r // tr
    direct = out_dtype == F32

    def body(x_ref, dy_ref, o_ref, *scratch):
        acc_ref = o_ref if direct else scratch[0]

        @pl.when(pl.program_id(2) == 0)
        def _():
            acc_ref[...] = jnp.zeros_like(acc_ref)

        acc_ref[...] += _dot_tn(x_ref[...], dy_ref[...])
        if not direct:
            @pl.when(pl.program_id(2) == n_r - 1)
            def _():
                o_ref[...] = acc_ref[...].astype(o_ref.dtype)

    return pl.pallas_call(
        body, name=name,
        out_shape=jax.ShapeDtypeStruct((k, n), out_dtype),
        grid=(k // tk, n // tn, n_r),
        in_specs=[pl.BlockSpec((tr, tk), lambda kb, nb, rr: (rr, kb)),
                  pl.BlockSpec((tr, tn), lambda kb, nb, rr: (rr, nb))],
        out_specs=pl.BlockSpec((tk, tn), lambda kb, nb, rr: (kb, nb)),
        scratch_shapes=[] if direct else [pltpu.VMEM((tk, tn), F32)],
        compiler_params=_params(),
    )(x, dy)


def _ffn_in_swiglu(u, w, *, tm, tn):
    r, k = u.shape
    h = w.shape[1] // 2
    assert r % tm == 0 and h % tn == 0
    nj = h // tn

    def body(u_ref, wg_ref, wu_ref, act_ref, gt_ref, up_ref):
        uu = u_ref[...]
        gt, up = _dot(uu, wg_ref[...]), _dot(uu, wu_ref[...])
        act_ref[...] = (gt * _sigmoid(gt) * up).astype(act_ref.dtype)
        gt_ref[...] = gt.astype(gt_ref.dtype)
        up_ref[...] = up.astype(up_ref.dtype)

    tile = pl.BlockSpec((tm, tn), lambda j, i: (i, j))
    return pl.pallas_call(
        body, name="ffn_in_swiglu",
        out_shape=[jax.ShapeDtypeStruct((r, h), _MXU_DTYPE)] * 3,
        grid=(nj, r // tm),
        in_specs=[pl.BlockSpec((tm, k), lambda j, i: (i, 0)),
                  pl.BlockSpec((k, tn), lambda j, i: (0, j)),
                  pl.BlockSpec((k, tn), lambda j, i: (0, nj + j))],
        out_specs=[tile] * 3,
        compiler_params=_params(),
    )(u, w, w)


def _d_ffn_out_swiglu(dy, w_t, gt, up, *, tm, tn):
    r, k = dy.shape
    h = w_t.shape[1]
    assert r % tm == 0 and h % tn == 0

    def body(dy_ref, w_ref, gt_ref, up_ref, dgt_ref, dup_ref):
        da = _dot(dy_ref[...], w_ref[...])
        g, u_ = gt_ref[...].astype(F32), up_ref[...].astype(F32)
        s = _sigmoid(g)
        dgt_ref[...] = (da * u_ * _silu_grad(g, s)).astype(dgt_ref.dtype)
        dup_ref[...] = (da * g * s).astype(dup_ref.dtype)

    tile = pl.BlockSpec((tm, tn), lambda j, i: (i, j))
    return pl.pallas_call(
        body, name="d_ffn_out_swiglu",
        out_shape=[jax.ShapeDtypeStruct((r, h), _MXU_DTYPE)] * 2,
        grid=(h // tn, r // tm),
        in_specs=[pl.BlockSpec((tm, k), lambda j, i: (i, 0)), pl.BlockSpec((k, tn), lambda j, i: (0, j)), tile, tile],
        out_specs=[tile] * 2,
        compiler_params=_params(),
    )(dy, w_t, gt, up)


def _proj_q_rope(cqn, w_q, c_tab, s_tab, *, tm, lp):
    r, k = cqn.shape
    tiles_per_example = lp // tm

    def body(x_ref, wn_ref, wp_ref, ws_ref, c_ref, s_ref, o_ref):
        x = x_ref[...]
        roped = _dot(x, wp_ref[...]) * c_ref[...] + _dot(x, ws_ref[...]) * s_ref[...]
        o_ref[...] = jnp.concatenate([_dot(x, wn_ref[...]), roped], axis=1).astype(o_ref.dtype)

    w_blk = lambda part: pl.BlockSpec((k, HEAD_DIM), functools.partial(lambda h, i, part: (0, part * HEADS + h), part=part))
    tab = pl.BlockSpec((tm, HEAD_DIM), lambda h, i: (i % tiles_per_example, 0))
    return pl.pallas_call(
        body, name="proj_q_rope",
        out_shape=jax.ShapeDtypeStruct((r, HEADS * QK_DIM), _MXU_DTYPE),
        grid=(HEADS, r // tm),
        in_specs=[pl.BlockSpec((tm, k), lambda h, i: (i, 0)), w_blk(0), w_blk(1), w_blk(2), tab, tab],
        out_specs=pl.BlockSpec((tm, QK_DIM), lambda h, i: (i, h)),
        compiler_params=_params(),
    )(cqn, w_q, w_q, w_q, c_tab, s_tab)


def _rowwise(name, body, *, rows, tr, lp, ins, outs, accs=()):
    assert rows % tr == 0 and lp % tr == 0 and tr % 16 == 0
    tiles_per_example = lp // tr
    in_specs, arrays = [], []
    for spec in ins:
        if spec[0] == "row":
            _, arr, width, cb = spec
            in_specs.append(pl.BlockSpec((tr, width), functools.partial(lambda i, cb: (i, cb), cb=cb)))
        elif spec[0] == "const":
            arr = spec[1]
            in_specs.append(pl.BlockSpec(arr.shape, lambda i: (0, 0)))
        else:
            arr = spec[1]
            in_specs.append(pl.BlockSpec((tr, arr.shape[1]), lambda i: (i % tiles_per_example, 0)))
        arrays.append(arr)
    n_in, n_out = len(ins), len(outs)

    def kern(*refs):
        res_outs, res_accs = body(*[r[...] for r in refs[:n_in]])
        for ref, val in zip(refs[n_in:n_in + n_out], res_outs, strict=True):
            ref[...] = val.astype(ref.dtype)
        acc_refs = refs[n_in + n_out:]
        if acc_refs:
            @pl.when(pl.program_id(0) == 0)
            def _():
                for ref in acc_refs:
                    ref[...] = jnp.zeros_like(ref)

            for ref, val in zip(acc_refs, res_accs, strict=True):
                ref[...] += val.reshape(tr // _SUBLANES, _SUBLANES, val.shape[-1]).sum(axis=0)

    out_shape = ([jax.ShapeDtypeStruct((rows, w), dt) for w, dt in outs]
                 + [jax.ShapeDtypeStruct((_SUBLANES, w), F32) for w in accs])
    out_specs = ([pl.BlockSpec((tr, w), lambda i: (i, 0)) for w, _ in outs]
                 + [pl.BlockSpec((_SUBLANES, w), lambda i: (0, 0)) for w in accs])
    res = pl.pallas_call(
        kern, name=name, out_shape=out_shape, grid=(rows // tr,),
        in_specs=in_specs, out_specs=out_specs, compiler_params=_params(),
    )(*arrays)
    return res[:n_out], list(res[n_out:])


def _assemble(name, x, head_rows, lp):
    batch, seq, d = x.shape
    tc = 256

    def body(x_ref, m_ref, o_ref):
        o_ref[0:N_META, :] = m_ref[...]
        o_ref[N_META:N_META + seq, :] = x_ref[0]
        if lp > N_META + seq:
            o_ref[N_META + seq:, :] = jnp.zeros((lp - N_META - seq, tc), F32)

    return pl.pallas_call(
        body, name=name,
        out_shape=jax.ShapeDtypeStruct((batch * lp, d), F32),
        grid=(batch, d // tc),
        in_specs=[pl.BlockSpec((1, seq, tc), lambda b, j: (b, 0, j)),
                  pl.BlockSpec((N_META, tc), lambda b, j: (0, j))],
        out_specs=pl.BlockSpec((lp, tc), lambda b, j: (b, j)),
        compiler_params=_params(),
    )(x, head_rows)


def _meta_grad(dh0, batch, lp):
    d = dh0.shape[1]

    def body(g_ref, o_ref):
        @pl.when(pl.program_id(0) == 0)
        def _():
            o_ref[...] = jnp.zeros_like(o_ref)

        o_ref[...] += g_ref[...]

    return pl.pallas_call(
        body, name="meta_grad",
        out_shape=jax.ShapeDtypeStruct((N_META, d), F32),
        grid=(batch,),
        in_specs=[pl.BlockSpec((N_META, d), lambda b: (b * (lp // N_META), 0))],
        out_specs=pl.BlockSpec((N_META, d), lambda b: (0, 0)),
        compiler_params=_params(),
    )(dh0)


def _segment_masks():
    t = lax.broadcasted_iota(jnp.int32, (HG_BLOCK, HG_BLOCK), 0)
    s = lax.broadcasted_iota(jnp.int32, (HG_BLOCK, HG_BLOCK), 1)
    same = lax.shift_right_logical(t, 4) == lax.shift_right_logical(s, 4)
    lower = same & (s <= t)
    upper = same & (s >= t)
    first_half = same & ((s & 15) <= 7)
    return same, lower, upper, first_half


def _hgrn_gates(hq, hf, lb):
    sq = _sigmoid(hq)
    q = hq * sq
    sf = _sigmoid(hf)
    f = lb + (1.0 - lb) * sf
    return q, sq, sf, f


def _hgrn_decays(g, same, lower, first_half):
    b = _exact_dot(lower, g)
    b_last = _exact_dot(same, g)
    b_ref = _exact_dot(first_half, g)
    return b, b_last, b_ref


def _hgrn_fwd(p, lb, gh, *, batch, lp, ride=None):
    rows = batch * lp
    nb = lp // HG_BLOCK
    n_chunks = HG_BLOCK // HG_CHUNK

    def body(hq_ref, hf_ref, hi_ref, hg_ref, lb_ref, gh_ref, *rest):
        if ride is not None:
            rest, exchange = ride.split(rest, 3)
            ride.run((batch, nb), exchange)
        o_ref, z_ref, st_ref, s_scr, qt_scr, kh_scr, v_scr, el_scr, o_scr = rest

        @pl.when(pl.program_id(1) == 0)
        def _():
            s_scr[...] = jnp.zeros_like(s_scr)

        same, lower, _, first_half = _segment_masks()
        v = hi_ref[...]
        q, _, _, f = _hgrn_gates(hq_ref[...], hf_ref[...], lb_ref[...])
        k = 1.0 - f
        b, b_last, b_ref = _hgrn_decays(jnp.log(f), same, lower, first_half)
        qt_scr[...] = _mx(q * jnp.exp(b))
        kh_scr[...] = _mx(k * jnp.exp(b_last - b))
        v_scr[...] = _mx(v)
        el_scr[...] = jnp.exp(b_last)
        qc = _mx(q * jnp.exp(b - b_ref))
        kc = _mx(k * jnp.exp(b_ref - b))

        def intra(qc_h, kc_h, v_h):
            a = jnp.where(lower, _dot_nt(qc_h, kc_h), 0.0)
            return _dot(_mx(a), v_h)

        o_scr[...] = _heads(intra, qc, kc, _mx(v))

        for c in range(n_chunks):
            rs = slice(c * HG_CHUNK, (c + 1) * HG_CHUNK)
            for h in range(HEADS):
                cs = slice(h * HEAD_DIM, (h + 1) * HEAD_DIM)
                st = s_scr[h]
                st_m = _mx(st)
                st_ref[c, h] = st_m
                o_scr[rs, cs] += _dot_nt(qt_scr[rs, cs], st_m)
                s_scr[h] = st * el_scr[c * HG_CHUNK:c * HG_CHUNK + 1, cs] + _dot_tn(v_scr[rs, cs], kh_scr[rs, cs])

        o = o_scr[...]
        o_ref[...] = o
        hg = hg_ref[...]
        n = _heads(lambda o_h: o_h * _rms_scale(o_h), o) * gh_ref[...]
        z_ref[...] = (n * hg * _sigmoid(hg)).astype(z_ref.dtype)

    blk = lambda cb: pl.BlockSpec((HG_BLOCK, D_MODEL), functools.partial(lambda b, j, cb: (b * nb + j, cb), cb=cb))
    row_out = pl.BlockSpec((HG_BLOCK, D_MODEL), lambda b, j: (b * nb + j, 0))
    const = pl.BlockSpec((1, D_MODEL), lambda b, j: (0, 0))
    extra = ride if ride is not None else _NoRide
    return pl.pallas_call(
        body, name="hgrn_fwd",
        out_shape=[jax.ShapeDtypeStruct((rows, D_MODEL), F32),
                   jax.ShapeDtypeStruct((rows, D_MODEL), _MXU_DTYPE),
                   jax.ShapeDtypeStruct((rows // HG_CHUNK, HEADS, HEAD_DIM, HEAD_DIM), _MXU_DTYPE)] + extra.out_shape,
        grid=(batch, nb),
        in_specs=[blk(CB_HQ), blk(CB_HF), blk(CB_HI), blk(CB_HG), const, const] + extra.in_specs,
        out_specs=[row_out, row_out,
                   pl.BlockSpec((n_chunks, HEADS, HEAD_DIM, HEAD_DIM), lambda b, j: (b * nb + j, 0, 0, 0))]
        + extra.out_specs,
        scratch_shapes=[pltpu.VMEM((HEADS, HEAD_DIM, HEAD_DIM), F32),
                        pltpu.VMEM((HG_BLOCK, D_MODEL), _MXU_DTYPE),
                        pltpu.VMEM((HG_BLOCK, D_MODEL), _MXU_DTYPE),
                        pltpu.VMEM((HG_BLOCK, D_MODEL), _MXU_DTYPE),
                        pltpu.VMEM((HG_BLOCK, D_MODEL), F32),
                        pltpu.VMEM((HG_BLOCK, D_MODEL), F32)] + extra.scratch,
        compiler_params=_params(),
    )(p, p, p, p, lb, gh, *extra.args)


def _hgrn_bwd(p, o, dz, states, lb, gh, *, batch, lp, ride=None):
    rows = batch * lp
    nb = lp // HG_BLOCK
    n_chunks = HG_BLOCK // HG_CHUNK

    def body(hq_ref, hf_ref, hi_ref, hg_ref, o_ref, dz_ref, st_ref, lb_ref, gh_ref, *rest):
        if ride is not None:
            rest, exchange = ride.split(rest, 3)
            ride.run((batch, nb), exchange)
        (dp_ref, dlb_ref, dgh_ref,
         ds_scr, qt_scr, kh_scr, v_scr, do_scr, el_scr, dqt_scr, dkh_scr, dv_scr, dbl_scr) = rest
        first = (pl.program_id(0) == 0) & (pl.program_id(1) == 0)

        @pl.when(first)
        def _():
            dlb_ref[...] = jnp.zeros_like(dlb_ref)
            dgh_ref[...] = jnp.zeros_like(dgh_ref)

        @pl.when(pl.program_id(1) == 0)
        def _():
            ds_scr[...] = jnp.zeros_like(ds_scr)

        same, lower, upper, first_half = _segment_masks()
        lbv = lb_ref[...]
        hq, hf, v, hg = hq_ref[...], hf_ref[...], hi_ref[...], hg_ref[...]
        q, sq, sf, f = _hgrn_gates(hq, hf, lbv)
        k = 1.0 - f
        b, b_last, b_ref = _hgrn_decays(jnp.log(f), same, lower, first_half)
        e_b = jnp.exp(b)
        e_kh = jnp.exp(b_last - b)
        e_qc = jnp.exp(b - b_ref)
        e_kc = jnp.exp(b_ref - b)
        qt, kh, qc, kc = q * e_b, k * e_kh, q * e_qc, k * e_kc

        o = o_ref[...]
        dz = dz_ref[...].astype(F32)
        ghv = gh_ref[...]
        sg = _sigmoid(hg)
        r = _heads(lambda o_h: jnp.broadcast_to(_rms_scale(o_h), o_h.shape), o)
        oh = o * r
        dn = dz * hg * sg
        dhg = dz * oh * ghv * _silu_grad(hg, sg)
        w = dn * ghv
        do = r * (w - oh * _heads(lambda t: jnp.broadcast_to(jnp.mean(t, axis=-1, keepdims=True), t.shape), oh * w))
        dgh_ref[...] += (dn * oh).reshape(HG_BLOCK // _SUBLANES, _SUBLANES, D_MODEL).sum(axis=0)

        qt_scr[...] = _mx(qt)
        kh_scr[...] = _mx(kh)
        v_scr[...] = _mx(v)
        do_scr[...] = _mx(do)
        el_scr[...] = jnp.exp(b_last)

        def intra(qc_h, kc_h, v_h, do_h):
            a = _mx(jnp.where(lower, _dot_nt(qc_h, kc_h), 0.0))
            da = _mx(jnp.where(lower, _dot_nt(do_h, v_h), 0.0))
            return _dot(da, kc_h), _dot_tn(da, qc_h), _dot_tn(a, do_h)

        dqc, dkc, dv_intra = _heads(intra, _mx(qc), _mx(kc), _mx(v), _mx(do))
        dv_scr[...] = dv_intra

        for c in reversed(range(n_chunks)):
            rs = slice(c * HG_CHUNK, (c + 1) * HG_CHUNK)
            for h in range(HEADS):
                cs = slice(h * HEAD_DIM, (h + 1) * HEAD_DIM)
                st = st_ref[c, h]
                ds_t = ds_scr[h]
                ds_m = _mx(ds_t)
                el = el_scr[c * HG_CHUNK:c * HG_CHUNK + 1, cs]
                dkh_scr[rs, cs] = _dot(v_scr[rs, cs], ds_m)
                dv_scr[rs, cs] += _dot_nt(kh_scr[rs, cs], ds_m)
                dbl = jnp.sum(ds_t * st.astype(F32), axis=0, keepdims=True) * el
                dbl_scr[rs, cs] = jnp.broadcast_to(dbl, (HG_CHUNK, HEAD_DIM))
                dqt_scr[rs, cs] = _dot(do_scr[rs, cs], st)
                ds_scr[h] = ds_t * el + _dot_tn(do_scr[rs, cs], qt_scr[rs, cs])

        dqt, dkh = dqt_scr[...], dkh_scr[...]
        dq = dqt * e_b + dqc * e_qc
        dk = dkh * e_kh + dkc * e_kc
        t_kh = dkh * kh
        db_rows = dqt * qt + dqc * qc - dkc * kc - t_kh
        dg = _exact_dot(upper, db_rows) + _exact_dot(same, t_kh) + dbl_scr[...]
        df = dg / f - dk
        dhf = df * (1.0 - lbv) * sf * (1.0 - sf)
        dlb_ref[...] += (df * (1.0 - sf)).reshape(HG_BLOCK // _SUBLANES, _SUBLANES, D_MODEL).sum(axis=0)
        dhq = dq * _silu_grad(hq, sq)
        dp_ref[...] = jnp.concatenate([dhq, dhf, dv_scr[...], dhg], axis=1).astype(dp_ref.dtype)

    rev = lambda b, j: b * nb + (nb - 1 - j)
    blk = lambda cb: pl.BlockSpec((HG_BLOCK, D_MODEL), functools.partial(lambda b, j, cb: (rev(b, j), cb), cb=cb))
    row = pl.BlockSpec((HG_BLOCK, D_MODEL), lambda b, j: (rev(b, j), 0))
    const = pl.BlockSpec((1, D_MODEL), lambda b, j: (0, 0))
    acc = pl.BlockSpec((_SUBLANES, D_MODEL), lambda b, j: (0, 0))
    big = lambda dt: pltpu.VMEM((HG_BLOCK, D_MODEL), dt)
    extra = ride if ride is not None else _NoRide
    dp, dlb, dgh, *exchanged = pl.pallas_call(
        body, name="hgrn_bwd",
        out_shape=[jax.ShapeDtypeStruct((rows, 4 * D_MODEL), _MXU_DTYPE),
                   jax.ShapeDtypeStruct((_SUBLANES, D_MODEL), F32),
                   jax.ShapeDtypeStruct((_SUBLANES, D_MODEL), F32)] + extra.out_shape,
        grid=(batch, nb),
        in_specs=[blk(CB_HQ), blk(CB_HF), blk(CB_HI), blk(CB_HG), row, row,
                  pl.BlockSpec((n_chunks, HEADS, HEAD_DIM, HEAD_DIM), lambda b, j: (rev(b, j), 0, 0, 0)),
                  const, const] + extra.in_specs,
        out_specs=[pl.BlockSpec((HG_BLOCK, 4 * D_MODEL), lambda b, j: (rev(b, j), 0)), acc, acc] + extra.out_specs,
        scratch_shapes=[pltpu.VMEM((HEADS, HEAD_DIM, HEAD_DIM), F32),
                        big(_MXU_DTYPE), big(_MXU_DTYPE), big(_MXU_DTYPE), big(_MXU_DTYPE),
                        big(F32), big(F32), big(F32), big(F32), big(F32)] + extra.scratch,
        compiler_params=_params(),
    )(p, p, p, p, o, dz, states, lb, gh, *extra.args)
    return (dp, dlb, dgh, *exchanged)


QK_DIM = 2 * HEAD_DIM
ATTN_TQ = 256
ATTN_KEY_CHUNK = 512


def _query_tiles(lp):
    return [(r0, min(ATTN_TQ, lp - r0)) for r0 in range(0, lp, ATTN_TQ)]


def _attn_fwd(q_cat, kv, kp, *, batch, lp):
    rows = batch * lp

    def body(q_ref, kn_ref, kp_ref, v_ref, o_ref, lse_ref):
        k_cat = jnp.concatenate([kn_ref[...], kp_ref[...]], axis=1)
        for r0, tq in _query_tiles(lp):
            q_t = q_ref[r0:r0 + tq, :]
            i = lax.broadcasted_iota(jnp.int32, (tq, tq), 0)
            j = lax.broadcasted_iota(jnp.int32, (tq, tq), 1)
            s_diag = jnp.where(j <= i, _dot_nt(q_t, k_cat[r0:r0 + tq]) * ATTN_SCALE, NEG_BIG)
            m = jnp.max(s_diag, axis=1, keepdims=True)
            if r0:
                s_past = _dot_nt(q_t, k_cat[0:r0]) * ATTN_SCALE
                m = jnp.maximum(m, jnp.max(s_past, axis=1, keepdims=True))
            p_diag = jnp.exp(s_diag - m)
            l = jnp.sum(p_diag, axis=1, keepdims=True)
            acc = _dot(_mx(p_diag), v_ref[r0:r0 + tq, :])
            if r0:
                p_past = jnp.exp(s_past - m)
                l = l + jnp.sum(p_past, axis=1, keepdims=True)
                acc = acc + _dot(_mx(p_past), v_ref[0:r0, :])
            o_ref[r0:r0 + tq, :] = (acc / l).astype(o_ref.dtype)
            lse_ref[r0:r0 + tq, :] = jnp.broadcast_to(m + jnp.log(l), (tq, HEAD_DIM))

    head_blk = pl.BlockSpec((lp, HEAD_DIM), lambda b, h: (b, h))
    return pl.pallas_call(
        body, name="attn_fwd",
        out_shape=[jax.ShapeDtypeStruct((rows, D_MODEL), _MXU_DTYPE),
                   jax.ShapeDtypeStruct((rows, D_MODEL), F32)],
        grid=(batch, HEADS),
        in_specs=[pl.BlockSpec((lp, QK_DIM), lambda b, h: (b, h)), head_blk,
                  pl.BlockSpec((lp, HEAD_DIM), lambda b, h: (b, 0)),
                  pl.BlockSpec((lp, HEAD_DIM), lambda b, h: (b, HEADS + h))],
        out_specs=[head_blk, head_blk],
        compiler_params=_params(),
    )(q_cat, kv, kp, kv)


def _attn_bwd(q_cat, kv, kp, do, o, lse, *, batch, lp, ride=None):
    rows = batch * lp

    def body(q_ref, kn_ref, kp_ref, v_ref, do_ref, o_ref, lse_ref, *rest):
        if ride is not None:
            rest, exchange = ride.split(rest, 4)
            ride.run((batch, HEADS), exchange)
        dq_ref, dkn_ref, dkp_ref, dv_ref, dk_acc, dv_acc = rest
        dk_acc[...] = jnp.zeros_like(dk_acc)
        dv_acc[...] = jnp.zeros_like(dv_acc)
        k_cat = jnp.concatenate([kn_ref[...], kp_ref[...]], axis=1)
        k_t = k_cat.T
        lane = lax.broadcasted_iota(jnp.int32, (_SUBLANES, HEAD_DIM), 1)
        lse_row = _exact_dot_nt(lane == 0, lse_ref[...])
        delta = _exact_dot_nt(lane >= 0, do_ref[...].astype(F32) * o_ref[...].astype(F32))
        for r0, tq in _query_tiles(lp):
            cols = slice(r0, r0 + tq)
            q_t_, do_t_ = q_ref[cols, :], do_ref[cols, :]
            lse_t, delta_t = lse_row[0:1, cols], delta[0:1, cols]
            chunks = [(c0, min(ATTN_KEY_CHUNK, r0 - c0), False) for c0 in range(0, r0, ATTN_KEY_CHUNK)] + [(r0, tq, True)]
            dq_t = jnp.zeros((QK_DIM, tq), F32)
            for c0, n, diagonal in chunks:
                keys = slice(c0, c0 + n)
                s = _dot_nt(k_cat[keys], q_t_) * ATTN_SCALE
                if diagonal:
                    jk = lax.broadcasted_iota(jnp.int32, (n, tq), 0)
                    iq = lax.broadcasted_iota(jnp.int32, (n, tq), 1)
                    s = jnp.where(jk <= iq, s, NEG_BIG)
                pexp = jnp.exp(s - lse_t)
                dp = _dot_nt(v_ref[keys, :], do_t_)
                ds = _mx(pexp * (dp - delta_t) * ATTN_SCALE)
                dk_acc[keys, :] += _dot(ds, q_t_)
                dv_acc[keys, :] += _dot(_mx(pexp), do_t_)
                dq_t = dq_t + _dot(k_t[:, keys], ds)
            dq_ref[cols, :] = dq_t.T

        dkn_ref[...] = dk_acc[:, 0:HEAD_DIM].astype(dkn_ref.dtype)
        dv_ref[...] = dv_acc[...].astype(dv_ref.dtype)

        @pl.when(pl.program_id(1) == 0)
        def _():
            dkp_ref[...] = jnp.zeros_like(dkp_ref)

        dkp_ref[...] += dk_acc[:, HEAD_DIM:]

    head_blk = pl.BlockSpec((lp, HEAD_DIM), lambda b, h: (b, h))
    cat_blk = pl.BlockSpec((lp, QK_DIM), lambda b, h: (b, h))
    shared_blk = pl.BlockSpec((lp, HEAD_DIM), lambda b, h: (b, 0))
    extra = ride if ride is not None else _NoRide
    return pl.pallas_call(
        body, name="attn_bwd",
        out_shape=[jax.ShapeDtypeStruct((rows, HEADS * QK_DIM), F32),
                   jax.ShapeDtypeStruct((rows, D_MODEL), _MXU_DTYPE),
                   jax.ShapeDtypeStruct((rows, HEAD_DIM), F32),
                   jax.ShapeDtypeStruct((rows, D_MODEL), _MXU_DTYPE)] + extra.out_shape,
        grid=(batch, HEADS),
        in_specs=[cat_blk, head_blk, shared_blk, pl.BlockSpec((lp, HEAD_DIM), lambda b, h: (b, HEADS + h)),
                  head_blk, head_blk, head_blk] + extra.in_specs,
        out_specs=[cat_blk, head_blk, shared_blk, head_blk] + extra.out_specs,
        scratch_shapes=[pltpu.VMEM((lp, QK_DIM), F32), pltpu.VMEM((lp, HEAD_DIM), F32)] + extra.scratch,
        compiler_params=_params(),
    )(q_cat, kv, kp, kv, do, o, lse, *extra.args)


def _all_gather(name, blocks):
    n = len(blocks)

    def body(*refs):
        x_refs, out_refs, (send_sems, recv_sems, local_sems) = refs[:n], refs[n:2 * n], refs[2 * n:]
        x, y, c = lax.axis_index("x"), lax.axis_index("y"), lax.axis_index("c")
        me, sibling = (x, y, c), (x, y, 1 - c)
        chips = [(1 - x, y), (x, 1 - y), (1 - x, 1 - y)]

        def slot(i, px, py, pc):
            return out_refs[i].at[4 * px + 2 * py + pc]

        def copy(i, k, blk, to, src=None):
            return pltpu.make_async_remote_copy(
                src_ref=slot(i, *blk) if src is None else src, dst_ref=slot(i, *blk),
                send_sem=send_sems.at[i, k], recv_sem=recv_sems.at[i, k],
                device_id=to, device_id_type=pl.DeviceIdType.MESH)

        mine = [pltpu.make_async_copy(x_refs[i], slot(i, *me), local_sems.at[i]) for i in range(n)]
        first = [copy(i, 0, me, sibling, src=x_refs[i]) for i in range(n)]
        first += [copy(i, 1 + j, me, (*chip, c), src=x_refs[i]) for i in range(n) for j, chip in enumerate(chips)]
        for cp in mine + first:
            cp.start()
        passed = []
        for i in range(n):
            for j, chip in enumerate(chips):
                copy(i, 1 + j, (*chip, c), me).wait_recv()
                passed.append(copy(i, 4 + j, (*chip, c), sibling))
                passed[-1].start()
        for i in range(n):
            copy(i, 0, sibling, me).wait_recv()
            for j, chip in enumerate(chips):
                copy(i, 4 + j, (*chip, 1 - c), me).wait_recv()
        for cp in first + passed:
            cp.wait_send()
        for cp in mine:
            cp.wait()

    return pl.pallas_call(
        body, name=name,
        out_shape=[jax.ShapeDtypeStruct((N_DEV, *b.shape), b.dtype) for b in blocks],
        in_specs=[pl.BlockSpec(memory_space=pl.ANY)] * n,
        out_specs=[pl.BlockSpec(memory_space=pl.ANY)] * n,
        scratch_shapes=[pltpu.SemaphoreType.DMA((n, 7)), pltpu.SemaphoreType.DMA((n, 7)),
                        pltpu.SemaphoreType.DMA((n,))],
    )(*blocks)


def _adamw_math(w, g, m, v):
    nm = ADAM_B1 * m + (1.0 - ADAM_B1) * g
    nv = ADAM_B2 * v + (1.0 - ADAM_B2) * (g * g)
    m_hat = nm / (1.0 - ADAM_B1 ** ADAM_STEP)
    v_hat = nv / (1.0 - ADAM_B2 ** ADAM_STEP)
    return -ADAM_LR * (m_hat / (jnp.sqrt(v_hat) + ADAM_EPS) + ADAM_WD * w), nm, nv


def _sum_adamw(name, parts, w, m, v):
    rows, cols = w.shape
    tr = rows // 4 if rows % 64 == 0 and rows * cols > (1 << 16) else rows

    def body(p_ref, w_ref, m_ref, v_ref, g_ref, d_ref, nm_ref, nv_ref):
        g = p_ref[0].astype(F32)
        for dev in range(1, N_DEV):
            g = g + p_ref[dev].astype(F32)
        g_ref[...] = g
        d_ref[...], nm_ref[...], nv_ref[...] = _adamw_math(w_ref[...], g, m_ref[...], v_ref[...])

    spec = pl.BlockSpec((tr, cols), lambda i: (i, 0))
    return pl.pallas_call(
        body, name=name,
        out_shape=[jax.ShapeDtypeStruct((rows, cols), F32)] * 4,
        grid=(rows // tr,),
        in_specs=[pl.BlockSpec((N_DEV, tr, cols), lambda i: (0, i, 0))] + [spec] * 3, out_specs=[spec] * 4,
        compiler_params=_params(),
    )(parts, w, m, v)


def _finish_vectors(gathered, lb, params, loss_parts):
    names = list(params)
    n = len(names)

    def body(*refs):
        g_refs, lb_ref, loss_ref = refs[:n], refs[n], refs[n + 1]
        wmv_refs = refs[n + 2:4 * n + 2]
        out_refs, loss_out = refs[4 * n + 2:-1], refs[-1]
        sq = loss_ref[0]
        for dev in range(1, N_DEV):
            sq = sq + loss_ref[dev]
        sq = jnp.sum(jnp.sum(sq, axis=0, keepdims=True), axis=1, keepdims=True)
        loss_out[...] = sq * (0.5 / D_MODEL)
        me = 4 * lax.axis_index("x") + 2 * lax.axis_index("y") + lax.axis_index("c")
        for i, name in enumerate(names):
            g_ref = g_refs[i]
            w_ref, m_ref, v_ref = wmv_refs[3 * i:3 * i + 3]
            if name == "meta_tokens":
                width = w_ref.shape[1]
                mine = pl.ds(pl.multiple_of(me * width, width), width)
                g = g_ref[0, :, mine]
                for dev in range(1, N_DEV):
                    g = g + g_ref[dev, :, mine]
            else:
                g = g_ref[0]
                for dev in range(1, N_DEV):
                    g = g + g_ref[dev]
                g = jnp.sum(g, axis=0, keepdims=True)
                if name == "hg_norm_g":
                    g = functools.reduce(jnp.add, [g[:, h * HEAD_DIM:(h + 1) * HEAD_DIM] for h in range(HEADS)])
                if name == "lb_logits":
                    lbv = lb_ref[...]
                    g = g * lbv * (1.0 - lbv)
                    g = jnp.concatenate([g, -g], axis=0)
            outs = (g, *_adamw_math(w_ref[...], g, m_ref[...], v_ref[...]))
            for ref, val in zip(out_refs[4 * i:4 * i + 4], outs, strict=True):
                ref[...] = val

    args = [gathered[k] for k in names] + [lb, loss_parts] + [t for k in names for t in params[k]]
    res = pl.pallas_call(
        body, name="finish_vectors",
        out_shape=[jax.ShapeDtypeStruct(params[k][0].shape, F32) for k in names for _ in range(4)]
        + [jax.ShapeDtypeStruct((1, 1), F32)],
        compiler_params=_params(),
    )(*args)
    return {k: res[4 * i:4 * i + 4] for i, k in enumerate(names)}, res[-1].reshape(())


def _swap_halves(t):
    half = t.shape[-1] // 2
    return jnp.concatenate([t[..., half:], t[..., :half]], axis=-1)


def _pad_last(t, width):
    return jnp.concatenate([t, jnp.zeros(t.shape[:-1] + (width - t.shape[-1],), t.dtype)], axis=-1)


def _rope_tables(lp):
    pos = jnp.arange(lp, dtype=F32)
    inv_freq = 1.0 / (ROPE_THETA ** (jnp.arange(0, ROPE_DIM, 2, dtype=F32) / ROPE_DIM))
    ang = pos[:, None] * inv_freq[None, :]
    cos, sin = jnp.cos(ang), jnp.sin(ang)
    c128 = _pad_last(jnp.concatenate([cos, cos], axis=1), HEAD_DIM)
    s128 = _pad_last(jnp.concatenate([-sin, sin], axis=1), HEAD_DIM)
    return c128, s128


def _forward_backward(x, target, meta, w, small, *, lp, comm=None):
    batch, seq, d = x.shape
    rows = batch * lp
    tr = 272 if lp % 272 == 0 else 128
    tm = lp // 2
    bf = _MXU_DTYPE
    rw = functools.partial(_rowwise, rows=rows, tr=tr, lp=lp)

    c128, s128 = _rope_tables(lp)
    cq_tab, sq_tab = jnp.tile(c128, (1, HEADS)), jnp.tile(s128, (1, HEADS))
    t_idx = jnp.arange(lp)
    real = jnp.broadcast_to(((t_idx >= N_META) & (t_idx < N_META + seq)).astype(F32)[:, None], (lp, _LANES))

    lb_logits = small["lb_logits"]
    lb = jax.nn.softmax(lb_logits, axis=0)[0:1]
    gh = jnp.tile(small["hg_norm_g"], (1, HEADS))

    h0 = _assemble("assemble_x", x, meta, lp)
    tgt = _assemble("assemble_target", target, jnp.zeros_like(meta), lp)

    (u1,), _ = rw("norm_mix_pre", lambda h, g: ([h * _rms_scale(h) * g], []),
                  ins=[("row", h0, d, 0), ("const", small["mix_pre_g"])], outs=[(d, bf)])
    p = _matmul("proj_in", u1, w["w_in"], out_dtype=F32, tm=tm, tn=1024, tk=1024)

    if comm is None:
        o_hg, z_a, states = _hgrn_fwd(p, lb, gh, batch=batch, lp=lp)
    else:
        o_hg, z_a, states, *gathered = _hgrn_fwd(p, lb, gh, batch=batch, lp=lp, ride=_Ride(comm.rest_payloads, True))
        w = {**w, **comm.rest_weights(gathered)}
    received = []
    scatter = lambda names: _Ride(comm.grad_parts(names, grads), False) if comm is not None else None
    y_a = _matmul("proj_hg_o", z_a, w["w_hg_o"], out_dtype=F32, tm=tm, tn=1024, tk=1024)

    def mla_pre(pc, gq, gkv, ct, st):
        cq, ckv = pc[:, 0:Q_LORA], pc[:, Q_LORA:Q_LORA + KV_LORA]
        kpe, kpe_sw = pc[:, 512:640], pc[:, 640:768]
        return [cq * _rms_scale(cq) * gq, ckv * _rms_scale(ckv) * gkv, kpe * ct + kpe_sw * st], []

    (cqn, ckvn, kp), _ = rw("mla_pre", mla_pre,
                            ins=[("row", p, 1024, CB_C), ("const", small["q_a_norm_g"]),
                                 ("const", small["kv_a_norm_g"]), ("pos", c128), ("pos", s128)],
                            outs=[(Q_LORA, bf), (KV_LORA, bf), (HEAD_DIM, bf)])
    q_cat = _proj_q_rope(cqn, w["w_q"], c128, s128, tm=tm, lp=lp)
    kv = _matmul("proj_kv_b", ckvn, w["w_kv"], out_dtype=bf, tm=tm, tn=1024, tk=KV_LORA)
    o_at, lse = _attn_fwd(q_cat, kv, kp, batch=batch, lp=lp)
    y_b = _matmul("proj_mla_o", o_at, w["w_mla_o"], out_dtype=F32, tm=tm, tn=1024, tk=1024)

    def merge(pa, pb, ya, yb, bg):
        ga, gb = _sigmoid(pa + bg[:, :d]), _sigmoid(pb + bg[:, d:])
        return [ga * ya + gb * yb], []

    (mix,), _ = rw("merge", merge,
                   ins=[("row", p, 1024, CB_GA), ("row", p, 1024, CB_GB), ("row", y_a, d, 0), ("row", y_b, d, 0),
                        ("const", small["b_gate"])], outs=[(d, bf)])
    mixed = _matmul("proj_out", mix, w["w_out"], out_dtype=F32, tm=tm, tn=1024, tk=1024)

    def post_mix(mx_, h, g2, g3):
        h1_ = h + mx_ * _rms_scale(mx_) * g2
        return [h1_, h1_ * _rms_scale(h1_) * g3], []

    (h1, u2), _ = rw("post_mix", post_mix,
                     ins=[("row", mixed, d, 0), ("row", h0, d, 0), ("const", small["mix_post_g"]),
                          ("const", small["ffn_pre_g"])], outs=[(d, F32), (d, bf)])
    act, gt, up = _ffn_in_swiglu(u2, w["w_ffn_in"], tm=tm, tn=1408)
    fo = _matmul("ffn_out", act, w["w_ffn_out"], out_dtype=F32, tm=tm, tn=1024, tk=1408)

    def post_ffn(fo_, h1_, t_, mask, g4):
        r = _rms_scale(fo_)
        h2 = h1_ + fo_ * r * g4
        err = (h2 - t_) * mask[:, 0:1]
        dh2 = err * (1.0 / d)
        dfo, dg4 = _rms_bwd(fo_, g4, dh2)
        return [dh2, dfo], [err * err, dg4]

    (dh2, dfo), (loss_vec, dg_ffn_post) = rw(
        "post_ffn_loss", post_ffn,
        ins=[("row", fo, d, 0), ("row", h1, d, 0), ("row", tgt, d, 0), ("pos", real), ("const", small["ffn_post_g"])],
        outs=[(d, F32), (d, bf)], accs=[d, d])
    loss = (0.5 / d) * jnp.sum(loss_vec)

    grads = {}
    dw_dt = F32 if comm is None else _WIRE_DTYPE
    dgt, dup = _d_ffn_out_swiglu(dfo, w["w_ffn_out_t"], gt, up, tm=tm, tn=1408)
    grads["w_ffn_out"] = _matmul_tn("dw_ffn_out", act, dfo, tk=1408, tn=1024, tr=tm, out_dtype=dw_dt)
    du2 = _matmul_segments("d_ffn_in", [dgt, dup], w["w_ffn_in_t"], out_dtype=F32, tm=tm, tn=1024, tk=1408)
    grads["w_ffn_in"] = jnp.concatenate([_matmul_tn("dw_ffn_in_gate", u2, dgt, tk=1024, tn=1408, tr=tm),
                                         _matmul_tn("dw_ffn_in_up", u2, dup, tk=1024, tn=1408, tr=tm)], axis=1)

    def post_mix_bwd(du2_, h1_, dh2_, mx_, g3, g2):
        dx, dg3 = _rms_bwd(h1_, g3, du2_)
        dh1_ = dh2_ + dx
        dmx, dg2 = _rms_bwd(mx_, g2, dh1_)
        return [dh1_, dmx], [dg3, dg2]

    (dh1, dmixed), (dg_ffn_pre, dg_mix_post) = rw(
        "post_mix_bwd", post_mix_bwd,
        ins=[("row", du2, d, 0), ("row", h1, d, 0), ("row", dh2, d, 0), ("row", mixed, d, 0),
             ("const", small["ffn_pre_g"]), ("const", small["mix_post_g"])],
        outs=[(d, F32), (d, bf)], accs=[d, d])
    dmix = _matmul("d_proj_out", dmixed, w["w_out_t"], out_dtype=F32, tm=tm, tn=1024, tk=1024)
    grads["w_out"] = _matmul_tn("dw_out", mix, dmixed, tk=1024, tn=1024, tr=tm, out_dtype=dw_dt)

    def merge_bwd(dm, pa, pb, ya, yb, bg):
        ga, gb = _sigmoid(pa + bg[:, :d]), _sigmoid(pb + bg[:, d:])
        dpg = jnp.concatenate([dm * ya * ga * (1.0 - ga), dm * yb * gb * (1.0 - gb)], axis=1)
        return [dpg, dm * ga, dm * gb], [dpg]

    (dpg, dya, dyb), (db_gate,) = rw(
        "merge_bwd", merge_bwd,
        ins=[("row", dmix, d, 0), ("row", p, 1024, CB_GA), ("row", p, 1024, CB_GB), ("row", y_a, d, 0),
             ("row", y_b, d, 0), ("const", small["b_gate"])],
        outs=[(2 * d, bf), (d, bf), (d, bf)], accs=[2 * d])
    dz_a = _matmul("d_proj_hg_o", dya, w["w_hg_o_t"], out_dtype=F32, tm=tm, tn=1024, tk=1024)
    grads["w_hg_o"] = _matmul_tn("dw_hg_o", z_a, dya, tk=1024, tn=1024, tr=tm, out_dtype=dw_dt)
    do_at = _matmul("d_proj_mla_o", dyb, w["w_mla_o_t"], out_dtype=bf, tm=tm, tn=1024, tk=1024)
    grads["w_mla_o"] = _matmul_tn("dw_mla_o", o_at, dyb, tk=1024, tn=1024, tr=tm, out_dtype=dw_dt)

    dph, dlb, dgh, *got = _hgrn_bwd(p, o_hg, dz_a, states, lb, gh, batch=batch, lp=lp,
                                    ride=scatter(_GRAD_GROUPS[0]))
    received.append(got)

    dq_cat, dkn, dkp, dvv, *got = _attn_bwd(q_cat, kv, kp, do_at, o_at, lse, batch=batch, lp=lp,
                                            ride=scatter(_GRAD_GROUPS[1]))
    received.append(got)

    def rope_q_bwd(dq, ct, st):
        hs = lambda half: jnp.concatenate(
            [dq[:, h * QK_DIM + half * HEAD_DIM:h * QK_DIM + (half + 1) * HEAD_DIM] for h in range(HEADS)], axis=1)
        dpe = hs(1)
        return [jnp.concatenate([hs(0), dpe * ct, dpe * st], axis=1)], []

    (dqf,), _ = rw("rope_q_bwd", rope_q_bwd,
                   ins=[("row", dq_cat, HEADS * QK_DIM, 0), ("pos", cq_tab), ("pos", sq_tab)],
                   outs=[(3 * d, bf)])
    dcqn = _matmul("d_proj_q_b", dqf, w["w_q_t"], out_dtype=F32, tm=tm, tn=Q_LORA, tk=1024)
    grads["w_q"] = _matmul_tn("dw_q_b", cqn, dqf, tk=Q_LORA, tn=1024, tr=tm)
    dckvn = _matmul_segments("d_proj_kv_b", [dkn, dvv], w["w_kv_t"], out_dtype=F32, tm=tm, tn=KV_LORA, tk=1024)
    grads["w_k"] = _matmul_tn("dw_k_b", ckvn, dkn, tk=KV_LORA, tn=1024, tr=tm)
    grads["w_v"] = _matmul_tn("dw_v_b", ckvn, dvv, tk=KV_LORA, tn=1024, tr=tm)

    def mla_pre_bwd(pc, dq_, dkv_, dkp_, gq, gkv, ct, st):
        cq, ckv = pc[:, 0:Q_LORA], pc[:, Q_LORA:Q_LORA + KV_LORA]
        dcq, dgq = _rms_bwd(cq, gq, dq_)
        dckv, dgkv = _rms_bwd(ckv, gkv, dkv_)
        dpc = jnp.concatenate([dcq, dckv, dkp_ * ct, dkp_ * st, jnp.zeros((pc.shape[0], 256), F32)], axis=1)
        return [dpc], [dgq, dgkv]

    (dpc,), (dg_q, dg_kv) = rw(
        "mla_pre_bwd", mla_pre_bwd,
        ins=[("row", p, 1024, CB_C), ("row", dcqn, Q_LORA, 0), ("row", dckvn, KV_LORA, 0), ("row", dkp, HEAD_DIM, 0),
             ("const", small["q_a_norm_g"]), ("const", small["kv_a_norm_g"]), ("pos", c128), ("pos", s128)],
        outs=[(1024, bf)], accs=[Q_LORA, KV_LORA])

    grads["w_in"] = jnp.concatenate([
        _matmul_tn("dw_in_h", u1, dph, tk=1024, tn=1024, tr=tm),
        _matmul_tn("dw_in_c", u1, dpc, tk=1024, tn=1024, tr=tm),
        _matmul_tn("dw_in_g", u1, dpg, tk=1024, tn=1024, tr=tm)], axis=1)
    du1 = _matmul_segments("d_proj_in", [dph, dpc, dpg], w["w_in_t"], out_dtype=F32, tm=tm, tn=1024, tk=1024,
                           ride=scatter(_GRAD_GROUPS[2]))
    if comm is not None:
        du1, *got = du1
        received.append(got)

    def pre_bwd(du, h, dh, g1):
        dx, dg1 = _rms_bwd(h, g1, du)
        return [dh + dx], [dg1]

    (dh0,), (dg_mix_pre,) = rw("norm_mix_pre_bwd", pre_bwd,
                               ins=[("row", du1, d, 0), ("row", h0, d, 0), ("row", dh1, d, 0), ("const", small["mix_pre_g"])],
                               outs=[(d, F32)], accs=[d])
    grad_x = dh0.reshape(batch, lp, d)[:, N_META:N_META + seq]
    partial = {"meta_tokens": _meta_grad(dh0, batch, lp), "lb_logits": dlb, "b_gate": db_gate, "hg_norm_g": dgh,
               "q_a_norm_g": dg_q, "kv_a_norm_g": dg_kv, "mix_pre_g": dg_mix_pre, "mix_post_g": dg_mix_post,
               "ffn_pre_g": dg_ffn_pre, "ffn_post_g": dg_ffn_post, "loss": loss_vec}
    return loss, grad_x, grads, partial, lb, received


_BIG = ["w_in", "w_hg_o", "w_q_b", "w_kv_b", "w_mla_o", "w_out", "w_ffn_in", "w_ffn_out"]
_COLUMN_SHARDED = {"w_in", "w_q_b", "w_kv_b", "w_ffn_in"}
_GRAD_GROUPS = [["w_ffn_in", "w_ffn_out"], ["w_out", "w_hg_o", "w_mla_o"], ["w_in", "w_q_b", "w_kv_b"]]
_SMALL = ["b_gate", "lb_logits", "hg_norm_g", "q_a_norm_g", "kv_a_norm_g", "mix_pre_g", "mix_post_g",
          "ffn_pre_g", "ffn_post_g"]


def _gathered_matrix(name, t):
    _, k, n = t.shape
    if name in _COLUMN_SHARDED:
        return t.transpose(1, 0, 2).reshape(k, N_DEV * n)
    return t.reshape(N_DEV * k, n)


def _scatter_layout(name, full):
    kk, nn = full.shape
    if name in _COLUMN_SHARDED:
        t = full.reshape(kk, N_DEV, nn // N_DEV).transpose(1, 0, 2)
    else:
        t = full.reshape(N_DEV, kk // N_DEV, nn)
    return t.astype(_WIRE_DTYPE)


def _model_w_in(wi):
    z = lambda *s: jnp.zeros(s, wi.dtype)
    kpe = wi[:, 4608:4672]
    c_blk = jnp.concatenate([wi[:, 4096:4608], kpe, z(1024, 64), _swap_halves(kpe), z(1024, 64), z(1024, 256)], axis=1)
    w_in = jnp.concatenate([wi[:, :4096], c_blk, wi[:, 4672:]], axis=1).astype(_MXU_DTYPE)
    return {"w_in": w_in, "w_in_t": w_in.T}


def _model_weights(full):
    return {**_model_w_in(full["w_in"]), **_model_rest(full)}


def _model_rest(full):
    wq3 = full["w_q_b"].reshape(Q_LORA, HEADS, HEAD_DIM + ROPE_DIM)
    pe = wq3[:, :, HEAD_DIM:]
    w_q = jnp.concatenate([wq3[:, :, :HEAD_DIM].reshape(Q_LORA, -1),
                           _pad_last(pe, HEAD_DIM).reshape(Q_LORA, -1),
                           _pad_last(_swap_halves(pe), HEAD_DIM).reshape(Q_LORA, -1)], axis=1)
    wkv3 = full["w_kv_b"].reshape(KV_LORA, HEADS, 2 * HEAD_DIM)
    w_k = wkv3[:, :, :HEAD_DIM].reshape(KV_LORA, -1)
    w_v = wkv3[:, :, HEAD_DIM:].reshape(KV_LORA, -1)
    w = {"w_q": w_q, "w_kv": jnp.concatenate([w_k, w_v], axis=1),
         "w_hg_o": full["w_hg_o"], "w_mla_o": full["w_mla_o"], "w_out": full["w_out"],
         "w_ffn_in": full["w_ffn_in"], "w_ffn_out": full["w_ffn_out"]}
    for n in ["w_q", "w_kv", "w_hg_o", "w_mla_o", "w_out", "w_ffn_in", "w_ffn_out"]:
        w[n + "_t"] = w[n].T
    return {k: v.astype(_MXU_DTYPE) for k, v in w.items()}


def _reference_layout_grad(name, g):
    if name == "w_in":
        gi = g["w_in"]
        d_kpe = gi[:, 4608:4672] + _swap_halves(gi[:, 4736:4800])
        return jnp.concatenate([gi[:, :4608], d_kpe, gi[:, 5120:]], axis=1)
    if name == "w_q_b":
        gq = g["w_q"]
        d_pe = (gq[:, 1024:2048].reshape(Q_LORA, HEADS, HEAD_DIM)[:, :, :ROPE_DIM]
                + _swap_halves(gq[:, 2048:].reshape(Q_LORA, HEADS, HEAD_DIM)[:, :, :ROPE_DIM]))
        return jnp.concatenate([gq[:, :1024].reshape(Q_LORA, HEADS, HEAD_DIM), d_pe], axis=2).reshape(Q_LORA, -1)
    if name == "w_kv_b":
        return jnp.concatenate([g["w_k"].reshape(KV_LORA, HEADS, HEAD_DIM),
                                g["w_v"].reshape(KV_LORA, HEADS, HEAD_DIM)], axis=2).reshape(KV_LORA, -1)
    return g[name]


def _reference_layout_grads(g):
    return {n: _reference_layout_grad(n, g) for n in _BIG}


class _Comm:
    def __init__(self, shard):
        self.rest_payloads = [shard[n].astype(_WIRE_DTYPE) for n in _BIG[1:]]

    def rest_weights(self, gathered):
        return _model_rest({n: _gathered_matrix(n, t) for n, t in zip(_BIG[1:], gathered, strict=True)})

    def grad_parts(self, names, g):
        return [_scatter_layout(n, _reference_layout_grad(n, g)) for n in names]


def kernel(x, meta_tokens, w_in, b_gate, lb_logits, hg_norm_g, w_hg_o, q_a_norm_g, w_q_b, kv_a_norm_g, w_kv_b, w_mla_o, w_out, mix_pre_g, mix_post_g, ffn_pre_g, ffn_post_g, w_ffn_in, w_ffn_out, loss_target, m_meta_tokens, m_w_in, m_b_gate, m_lb_logits, m_hg_norm_g, m_w_hg_o, m_q_a_norm_g, m_w_q_b, m_kv_a_norm_g, m_w_kv_b, m_w_mla_o, m_w_out, m_mix_pre_g, m_mix_post_g, m_ffn_pre_g, m_ffn_post_g, m_w_ffn_in, m_w_ffn_out, v_meta_tokens, v_w_in, v_b_gate, v_lb_logits, v_hg_norm_g, v_w_hg_o, v_q_a_norm_g, v_w_q_b, v_kv_a_norm_g, v_w_kv_b, v_w_mla_o, v_w_out, v_mix_pre_g, v_mix_post_g, v_ffn_pre_g, v_ffn_post_g, v_w_ffn_in, v_w_ffn_out):
    args = dict(locals())
    batch, seq, d = x.shape
    lp = -(-(N_META + seq) // _LANES) * _LANES
    weight_names = ["meta_tokens", "w_in", "b_gate", "lb_logits", "hg_norm_g", "w_hg_o", "q_a_norm_g", "w_q_b",
                    "kv_a_norm_g", "w_kv_b", "w_mla_o", "w_out", "mix_pre_g", "mix_post_g", "ffn_pre_g",
                    "ffn_post_g", "w_ffn_in", "w_ffn_out"]
    shard = {n: args[n].reshape(args[n].shape[-2:]) for n in _BIG}
    comm = _Comm(shard)

    w_in_all, meta_all = _all_gather("gather_first", [shard["w_in"].astype(_WIRE_DTYPE), meta_tokens])
    w_first = _model_w_in(_gathered_matrix("w_in", w_in_all))
    meta_full = meta_all.transpose(1, 0, 2).reshape(N_META, d)
    small = {n: args[n] for n in _SMALL}

    _, grad_x, _, partial, lb, received = _forward_backward(x, loss_target, meta_full, w_first, small, lp=lp, comm=comm)
    out = {}
    for names, bufs in zip(_GRAD_GROUPS, received, strict=True):
        for n, buf in zip(names, bufs, strict=True):
            two_d = lambda t: t.reshape(t.shape[-2:])
            res = _sum_adamw("adamw_" + n, buf, shard[n], two_d(args["m_" + n]), two_d(args["v_" + n]))
            out[n] = [t.reshape(args[n].shape) for t in res]

    vec_names = _SMALL + ["meta_tokens"]
    *gathered, loss_parts = _all_gather("gather_vectors", [partial[n] for n in vec_names + ["loss"]])
    finished, loss = _finish_vectors(dict(zip(vec_names, gathered, strict=True)), lb,
                                     {n: (args[n], args["m_" + n], args["v_" + n]) for n in vec_names}, loss_parts)
    out.update(finished)
    return (loss, grad_x, *[out[n][i] for i in range(4) for n in weight_names])
```

```python
import functools

import jax
import jax.numpy as jnp
from jax import lax
from jax.experimental import pallas as pl
from jax.experimental.pallas import tpu as pltpu

F32 = jnp.float32
_MXU_DTYPE = jnp.bfloat16
_WIRE_DTYPE = jnp.bfloat16
_VMEM_LIMIT_BYTES = 56 * 1024 * 1024
_LANES = 128
_SUBLANES = 8

N_DEV = 8
N_META = 16
NORM_EPS = 1e-6
HEADS = 8
HEAD_DIM = 128
ROPE_DIM = 64
HG_CHUNK = 16
HG_BLOCK = 128
ROPE_THETA = 10000.0
D_MODEL = 1024
Q_LORA = 256
KV_LORA = 256
FFN_HIDDEN = 2816
ATTN_SCALE = (HEAD_DIM + ROPE_DIM) ** -0.5
NEG_BIG = -1e30

ADAM_LR = 0.001
ADAM_B1 = 0.9
ADAM_B2 = 0.999
ADAM_EPS = 1e-08
ADAM_WD = 0.01
ADAM_STEP = 10

CB_HQ, CB_HF, CB_HI, CB_HG, CB_C, CB_GA, CB_GB = range(7)
IN_COLS_PADDED = 7 * 1024


def _params(**kw):
    return pltpu.CompilerParams(vmem_limit_bytes=_VMEM_LIMIT_BYTES, **kw)


def _dot(a, b):
    return lax.dot_general(a, b, (((1,), (0,)), ((), ())), preferred_element_type=F32)


def _dot_nt(a, b):
    return lax.dot_general(a, b, (((1,), (1,)), ((), ())), preferred_element_type=F32)


def _dot_tn(a, b):
    return lax.dot_general(a, b, (((0,), (0,)), ((), ())), preferred_element_type=F32)


def _mx(x):
    return x.astype(_MXU_DTYPE)


def _exact_dot(m01, x, dot=_dot):
    if _MXU_DTYPE == jnp.float32:
        return dot(m01.astype(F32), x)
    m = m01.astype(jnp.bfloat16)
    x1 = x.astype(jnp.bfloat16)
    r1 = x - x1.astype(F32)
    x2 = r1.astype(jnp.bfloat16)
    x3 = (r1 - x2.astype(F32)).astype(jnp.bfloat16)
    return dot(m, x1) + dot(m, x2) + dot(m, x3)


def _exact_dot_nt(m01, x):
    return _exact_dot(m01, x, dot=_dot_nt)


def _sigmoid(x):
    return jax.nn.sigmoid(x)


def _silu_grad(x, s):
    return s * (1.0 + x * (1.0 - s))


def _rms_scale(x):
    return lax.rsqrt(jnp.mean(x * x, axis=-1, keepdims=True) + NORM_EPS)


def _rms_bwd(x, g, dy):
    r = _rms_scale(x)
    xh = x * r
    w = dy * g
    dx = r * (w - xh * jnp.mean(xh * w, axis=-1, keepdims=True))
    return dx, dy * xh


def _heads(fn, *arrays):
    outs = [fn(*[a[:, h * HEAD_DIM:(h + 1) * HEAD_DIM] for a in arrays]) for h in range(HEADS)]
    if isinstance(outs[0], tuple):
        return tuple(jnp.concatenate([o[i] for o in outs], axis=1) for i in range(len(outs[0])))
    return jnp.concatenate(outs, axis=1)


class _Ride:
    def __init__(self, payloads, gather):
        self.gather, self.args, self.n = gather, list(payloads), len(payloads)
        self.in_specs = [pl.BlockSpec(memory_space=pl.ANY)] * self.n
        self.out_shape = [jax.ShapeDtypeStruct((N_DEV, *p.shape[-2:]), p.dtype) for p in payloads]
        self.out_specs = [pl.BlockSpec(memory_space=pl.ANY)] * self.n
        self.scratch = [pltpu.SemaphoreType.DMA((self.n, N_DEV - 1)), pltpu.SemaphoreType.DMA((self.n, N_DEV - 1)),
                        pltpu.SemaphoreType.DMA((self.n,))]

    def split(self, rest, n_outs):
        n = self.n
        mine = (rest[:n], rest[n + n_outs:2 * n + n_outs], rest[-3:])
        return rest[n:n + n_outs] + rest[2 * n + n_outs:-3], mine

    def _copies(self, p_refs, out_refs, sems):
        send_sems, recv_sems, local_sems = sems
        x, y, c = lax.axis_index("x"), lax.axis_index("y"), lax.axis_index("c")
        me = 4 * x + 2 * y + c
        copies = []
        for i, (p_ref, out_ref) in enumerate(zip(p_refs, out_refs, strict=True)):
            part = (lambda j, p_ref=p_ref: p_ref) if self.gather else (lambda j, p_ref=p_ref: p_ref.at[j])
            copies.append(pltpu.make_async_copy(part(me), out_ref.at[me], local_sems.at[i]))
            for k in range(1, N_DEV):
                px, py, pc = x ^ (k >> 2), y ^ ((k >> 1) & 1), c ^ (k & 1)
                copies.append(pltpu.make_async_remote_copy(
                    src_ref=part(4 * px + 2 * py + pc), dst_ref=out_ref.at[me],
                    send_sem=send_sems.at[i, k - 1], recv_sem=recv_sems.at[i, k - 1],
                    device_id=(px, py, pc), device_id_type=pl.DeviceIdType.MESH))
        return copies

    def run(self, grid, refs):
        ids = [pl.program_id(i) for i in range(len(grid))]
        first = functools.reduce(jnp.logical_and, [i == 0 for i in ids])
        last = functools.reduce(jnp.logical_and, [i == g - 1 for i, g in zip(ids, grid)])

        @pl.when(first)
        def _():
            for cp in self._copies(*refs):
                cp.start()

        @pl.when(last)
        def _():
            for cp in self._copies(*refs):
                cp.wait()


class _NoRide:
    in_specs, out_shape, out_specs, scratch, args = [], [], [], [], []


def _matmul(name, a, b, *, out_dtype, tm, tn, tk, c_in=None, ride=None, b_transposed=False):
    m, k = a.shape
    n = b.shape[0] if b_transposed else b.shape[1]
    assert m % tm == 0 and n % tn == 0 and k % tk == 0, (name, a.shape, b.shape, tm, tn, tk)
    nk = k // tk
    has_c = c_in is not None
    dot = _dot_nt if b_transposed else _dot
    grid = (n // tn, m // tm, nk)
    n_in = 2 + has_c

    def body(*refs):
        a_ref, b_ref = refs[0], refs[1]
        c_ref = refs[2] if has_c else None
        rest = refs[n_in:]
        if ride is not None:
            rest, exchange = ride.split(rest, 1)
            ride.run(grid, exchange)
        o_ref = rest[0]
        acc_ref = rest[1] if nk > 1 else None

        def finish(r):
            if has_c:
                r = r + c_ref[...]
            o_ref[...] = r.astype(o_ref.dtype)

        if nk == 1:
            finish(dot(a_ref[...], b_ref[...]))
        else:
            kk = pl.program_id(2)

            @pl.when(kk == 0)
            def _():
                acc_ref[...] = jnp.zeros_like(acc_ref)

            acc_ref[...] += dot(a_ref[...], b_ref[...])

            @pl.when(kk == nk - 1)
            def _():
                finish(acc_ref[...])

    in_specs = [pl.BlockSpec((tm, tk), lambda j, i, kk: (i, kk)),
                pl.BlockSpec((tn, tk), lambda j, i, kk: (j, kk)) if b_transposed
                else pl.BlockSpec((tk, tn), lambda j, i, kk: (kk, j))]
    args = [a, b]
    aliases = {}
    if has_c:
        in_specs.append(pl.BlockSpec((tm, tn), lambda j, i, kk: (i, j)))
        args.append(c_in)
        aliases = {2: 0}
    out_shape = [jax.ShapeDtypeStruct((m, n), out_dtype)]
    out_specs = [pl.BlockSpec((tm, tn), lambda j, i, kk: (i, j))]
    scratch = [pltpu.VMEM((tm, tn), F32)] if nk > 1 else []
    if ride is not None:
        in_specs, args = in_specs + ride.in_specs, args + ride.args
        out_shape, out_specs, scratch = out_shape + ride.out_shape, out_specs + ride.out_specs, scratch + ride.scratch
    res = pl.pallas_call(
        body, name=name, out_shape=out_shape, grid=grid, in_specs=in_specs, out_specs=out_specs,
        scratch_shapes=scratch, input_output_aliases=aliases, compiler_params=_params(),
    )(*args)
    return res[0] if ride is None else res


class _Epilogue:
    def __init__(self, fn, *, rows=(), consts=(), pos=(), outs=(), accs=(), lp=None):
        self.fn, self.rows, self.consts, self.pos = fn, list(rows), list(consts), list(pos)
        self.outs, self.accs, self.lp = list(outs), list(accs), lp


def _matmul_segments(name, a_list, b, *, out_dtype=F32, tm, tn, tk, ride=None, b_transposed=False, epilogue=None):
    m = a_list[0].shape[0]
    n, k = b.shape if b_transposed else b.shape[::-1]
    steps = [a.shape[1] // tk for a in a_list]
    offs = [sum(steps[:s]) for s in range(len(steps))]
    nk = sum(steps)
    assert nk * tk == k and m % tm == 0 and n % tn == 0 and all(a.shape[1] % tk == 0 for a in a_list), name
    grid = (n // tn, m // tm, nk)
    n_seg = len(a_list)
    dot = _dot_nt if b_transposed else _dot
    ep = epilogue
    assert ep is None or tn == n, name
    n_extra = 0 if ep is None else len(ep.rows) + len(ep.consts) + len(ep.pos)
    n_outs = 1 if ep is None else len(ep.outs) + len(ep.accs)

    def body(*refs):
        a_refs, b_ref = refs[:n_seg], refs[n_seg]
        extra_refs, rest = refs[n_seg + 1:n_seg + 1 + n_extra], refs[n_seg + 1 + n_extra:]
        if ride is not None:
            rest, exchange = ride.split(rest, n_outs)
            ride.run(grid, exchange)
        out_refs, acc_ref = rest[:n_outs], rest[n_outs]
        i, kk = pl.program_id(1), pl.program_id(2)

        @pl.when(kk == 0)
        def _():
            acc_ref[...] = jnp.zeros_like(acc_ref)

        for s in range(n_seg):
            @pl.when((kk >= offs[s]) & (kk < offs[s] + steps[s]))
            def _(s=s):
                acc_ref[...] += dot(a_refs[s][...], b_ref[...])

        if ep is None:
            @pl.when(kk == nk - 1)
            def _():
                out_refs[0][...] = acc_ref[...].astype(out_refs[0].dtype)
        else:
            sum_refs = out_refs[len(ep.outs):]

            @pl.when((kk == 0) & (i == 0))
            def _():
                for ref in sum_refs:
                    ref[...] = jnp.zeros_like(ref)

            @pl.when(kk == nk - 1)
            def _():
                res_outs, res_sums = ep.fn(acc_ref[...], *[r[...] for r in extra_refs])
                for ref, val in zip(out_refs[:len(ep.outs)], res_outs, strict=True):
                    ref[...] = val.astype(ref.dtype)
                for ref, val in zip(sum_refs, res_sums, strict=True):
                    ref[...] += val.reshape(tm // _SUBLANES, _SUBLANES, val.shape[-1]).sum(axis=0)

    seg_spec = lambda s: pl.BlockSpec(
        (tm, tk), functools.partial(lambda j, i, kk, off, ns: (i, jnp.clip(kk - off, 0, ns - 1)), off=offs[s], ns=steps[s]))
    b_spec = (pl.BlockSpec((tn, tk), lambda j, i, kk: (j, kk)) if b_transposed
              else pl.BlockSpec((tk, tn), lambda j, i, kk: (kk, j)))
    in_specs = [seg_spec(s) for s in range(n_seg)] + [b_spec]
    args = list(a_list) + [b]
    row_spec = lambda w: pl.BlockSpec((tm, w), lambda j, i, kk: (i, 0))
    if ep is None:
        out_shape = [jax.ShapeDtypeStruct((m, n), out_dtype)]
        out_specs = [pl.BlockSpec((tm, tn), lambda j, i, kk: (i, j))]
    else:
        tiles_per_example = ep.lp // tm
        row_ins = [r if isinstance(r, tuple) else (r, r.shape[1], 0) for r in ep.rows]
        in_specs += ([pl.BlockSpec((tm, wd), functools.partial(lambda j, i, kk, cb: (i, cb), cb=cb)) for _, wd, cb in row_ins]
                     + [pl.BlockSpec(c.shape, lambda j, i, kk: (0, 0)) for c in ep.consts]
                     + [pl.BlockSpec((tm, p.shape[1]), lambda j, i, kk: (i % tiles_per_example, 0)) for p in ep.pos])
        args += [arr for arr, _, _ in row_ins] + ep.consts + ep.pos
        out_shape = ([jax.ShapeDtypeStruct((m, w), dt) for w, dt in ep.outs]
                     + [jax.ShapeDtypeStruct((_SUBLANES, w), F32) for w in ep.accs])
        out_specs = ([row_spec(w) for w, _ in ep.outs]
                     + [pl.BlockSpec((_SUBLANES, w), lambda j, i, kk: (0, 0)) for w in ep.accs])
    scratch = [pltpu.VMEM((tm, tn), F32)]
    if ride is not None:
        in_specs, args = in_specs + ride.in_specs, args + ride.args
        out_shape, out_specs, scratch = out_shape + ride.out_shape, out_specs + ride.out_specs, scratch + ride.scratch
    res = pl.pallas_call(
        body, name=name, out_shape=out_shape, grid=grid, in_specs=in_specs, out_specs=out_specs,
        scratch_shapes=scratch, compiler_params=_params(),
    )(*args)
    if ep is None:
        return res[0] if ride is None else res
    n_o = len(ep.outs)
    return (res[:n_o], res[n_o:n_outs], *res[n_outs:])


def _matmul_tn(name, x, dy, *, tk, tn, tr, out_dtype=F32):
    r, k = x.shape
    _, n = dy.shape
    assert r % tr == 0 and k % tk == 0 and n % tn == 0, (name, x.shape, dy.shape)
    n_r = r // tr
    direct = out_dtype == F32

    def body(x_ref, dy_ref, o_ref, *scratch):
        acc_ref = o_ref if direct else scratch[0]

        @pl.when(pl.program_id(2) == 0)
        def _():
            acc_ref[...] = jnp.zeros_like(acc_ref)

        acc_ref[...] += _dot_tn(x_ref[...], dy_ref[...])
        if not direct:
            @pl.when(pl.program_id(2) == n_r - 1)
            def _():
                o_ref[...] = acc_ref[...].astype(o_ref.dtype)

    return pl.pallas_call(
        body, name=name,
        out_shape=jax.ShapeDtypeStruct((k, n), out_dtype),
        grid=(k // tk, n // tn, n_r),
        in_specs=[pl.BlockSpec((tr, tk), lambda kb, nb, rr: (rr, kb)),
                  pl.BlockSpec((tr, tn), lambda kb, nb, rr: (rr, nb))],
        out_specs=pl.BlockSpec((tk, tn), lambda kb, nb, rr: (kb, nb)),
        scratch_shapes=[] if direct else [pltpu.VMEM((tk, tn), F32)],
        compiler_params=_params(),
    )(x, dy)


def _ffn_in_swiglu(u, w, *, tm, tn):
    r, k = u.shape
    h = w.shape[1] // 2
    assert r % tm == 0 and h % tn == 0
    nj = h // tn

    def body(u_ref, wg_ref, wu_ref, act_ref, gt_ref, up_ref):
        uu = u_ref[...]
        gt, up = _dot(uu, wg_ref[...]), _dot(uu, wu_ref[...])
        act_ref[...] = (gt * _sigmoid(gt) * up).astype(act_ref.dtype)
        gt_ref[...] = gt.astype(gt_ref.dtype)
        up_ref[...] = up.astype(up_ref.dtype)

    tile = pl.BlockSpec((tm, tn), lambda j, i: (i, j))
    return pl.pallas_call(
        body, name="ffn_in_swiglu",
        out_shape=[jax.ShapeDtypeStruct((r, h), _MXU_DTYPE)] * 3,
        grid=(nj, r // tm),
        in_specs=[pl.BlockSpec((tm, k), lambda j, i: (i, 0)),
                  pl.BlockSpec((k, tn), lambda j, i: (0, j)),
                  pl.BlockSpec((k, tn), lambda j, i: (0, nj + j))],
        out_specs=[tile] * 3,
        compiler_params=_params(),
    )(u, w, w)


def _d_ffn_out_swiglu(dy, w, gt, up, *, tm, tn):
    r, k = dy.shape
    h = w.shape[0]
    assert r % tm == 0 and h % tn == 0

    def body(dy_ref, w_ref, gt_ref, up_ref, dgt_ref, dup_ref):
        da = _dot_nt(dy_ref[...], w_ref[...])
        g, u_ = gt_ref[...].astype(F32), up_ref[...].astype(F32)
        s = _sigmoid(g)
        dgt_ref[...] = (da * u_ * _silu_grad(g, s)).astype(dgt_ref.dtype)
        dup_ref[...] = (da * g * s).astype(dup_ref.dtype)

    tile = pl.BlockSpec((tm, tn), lambda j, i: (i, j))
    return pl.pallas_call(
        body, name="d_ffn_out_swiglu",
        out_shape=[jax.ShapeDtypeStruct((r, h), _MXU_DTYPE)] * 2,
        grid=(h // tn, r // tm),
        in_specs=[pl.BlockSpec((tm, k), lambda j, i: (i, 0)), pl.BlockSpec((tn, k), lambda j, i: (j, 0)), tile, tile],
        out_specs=[tile] * 2,
        compiler_params=_params(),
    )(dy, w, gt, up)


def _proj_q_rope(cqn, w_q, c_tab, s_tab, *, tm, lp):
    r, k = cqn.shape
    tiles_per_example = lp // tm

    def body(x_ref, wn_ref, wp_ref, ws_ref, c_ref, s_ref, o_ref):
        x = x_ref[...]
        roped = _dot(x, wp_ref[...]) * c_ref[...] + _dot(x, ws_ref[...]) * s_ref[...]
        o_ref[...] = jnp.concatenate([_dot(x, wn_ref[...]), roped], axis=1).astype(o_ref.dtype)

    w_blk = lambda part: pl.BlockSpec((k, HEAD_DIM), functools.partial(lambda h, i, part: (0, part * HEADS + h), part=part))
    tab = pl.BlockSpec((tm, HEAD_DIM), lambda h, i: (i % tiles_per_example, 0))
    return pl.pallas_call(
        body, name="proj_q_rope",
        out_shape=jax.ShapeDtypeStruct((r, HEADS * QK_DIM), _MXU_DTYPE),
        grid=(HEADS, r // tm),
        in_specs=[pl.BlockSpec((tm, k), lambda h, i: (i, 0)), w_blk(0), w_blk(1), w_blk(2), tab, tab],
        out_specs=pl.BlockSpec((tm, QK_DIM), lambda h, i: (i, h)),
        compiler_params=_params(),
    )(cqn, w_q, w_q, w_q, c_tab, s_tab)


def _rowwise(name, body, *, rows, tr, lp, ins, outs, accs=()):
    assert rows % tr == 0 and lp % tr == 0 and tr % 16 == 0
    tiles_per_example = lp // tr
    in_specs, arrays = [], []
    for spec in ins:
        if spec[0] == "row":
            _, arr, width, cb = spec
            in_specs.append(pl.BlockSpec((tr, width), functools.partial(lambda i, cb: (i, cb), cb=cb)))
        elif spec[0] == "const":
            arr = spec[1]
            in_specs.append(pl.BlockSpec(arr.shape, lambda i: (0, 0)))
        else:
            arr = spec[1]
            in_specs.append(pl.BlockSpec((tr, arr.shape[1]), lambda i: (i % tiles_per_example, 0)))
        arrays.append(arr)
    n_in, n_out = len(ins), len(outs)

    def kern(*refs):
        res_outs, res_accs = body(*[r[...] for r in refs[:n_in]])
        for ref, val in zip(refs[n_in:n_in + n_out], res_outs, strict=True):
            ref[...] = val.astype(ref.dtype)
        acc_refs = refs[n_in + n_out:]
        if acc_refs:
            @pl.when(pl.program_id(0) == 0)
            def _():
                for ref in acc_refs:
                    ref[...] = jnp.zeros_like(ref)

            for ref, val in zip(acc_refs, res_accs, strict=True):
                ref[...] += val.reshape(tr // _SUBLANES, _SUBLANES, val.shape[-1]).sum(axis=0)

    out_shape = ([jax.ShapeDtypeStruct((rows, w), dt) for w, dt in outs]
                 + [jax.ShapeDtypeStruct((_SUBLANES, w), F32) for w in accs])
    out_specs = ([pl.BlockSpec((tr, w), lambda i: (i, 0)) for w, _ in outs]
                 + [pl.BlockSpec((_SUBLANES, w), lambda i: (0, 0)) for w in accs])
    res = pl.pallas_call(
        kern, name=name, out_shape=out_shape, grid=(rows // tr,),
        in_specs=in_specs, out_specs=out_specs, compiler_params=_params(),
    )(*arrays)
    return res[:n_out], list(res[n_out:])


def _assemble(name, x, head_rows, lp):
    batch, seq, d = x.shape
    tc = 256

    def body(x_ref, m_ref, o_ref):
        o_ref[0:N_META, :] = m_ref[...]
        o_ref[N_META:N_META + seq, :] = x_ref[0]
        if lp > N_META + seq:
            o_ref[N_META + seq:, :] = jnp.zeros((lp - N_META - seq, tc), F32)

    return pl.pallas_call(
        body, name=name,
        out_shape=jax.ShapeDtypeStruct((batch * lp, d), F32),
        grid=(batch, d // tc),
        in_specs=[pl.BlockSpec((1, seq, tc), lambda b, j: (b, 0, j)),
                  pl.BlockSpec((N_META, tc), lambda b, j: (0, j))],
        out_specs=pl.BlockSpec((lp, tc), lambda b, j: (b, j)),
        compiler_params=_params(),
    )(x, head_rows)


def _meta_grad(dh0, batch, lp):
    d = dh0.shape[1]

    def body(g_ref, o_ref):
        @pl.when(pl.program_id(0) == 0)
        def _():
            o_ref[...] = jnp.zeros_like(o_ref)

        o_ref[...] += g_ref[...]

    return pl.pallas_call(
        body, name="meta_grad",
        out_shape=jax.ShapeDtypeStruct((N_META, d), F32),
        grid=(batch,),
        in_specs=[pl.BlockSpec((N_META, d), lambda b: (b * (lp // N_META), 0))],
        out_specs=pl.BlockSpec((N_META, d), lambda b: (0, 0)),
        compiler_params=_params(),
    )(dh0)


def _segment_masks():
    t = lax.broadcasted_iota(jnp.int32, (HG_BLOCK, HG_BLOCK), 0)
    s = lax.broadcasted_iota(jnp.int32, (HG_BLOCK, HG_BLOCK), 1)
    same = lax.shift_right_logical(t, 4) == lax.shift_right_logical(s, 4)
    lower = same & (s <= t)
    upper = same & (s >= t)
    first_half = same & ((s & 15) <= 7)
    return same, lower, upper, first_half


def _hgrn_gates(hq, hf, lb):
    sq = _sigmoid(hq)
    q = hq * sq
    sf = _sigmoid(hf)
    f = lb + (1.0 - lb) * sf
    return q, sq, sf, f


def _hgrn_decays(g, same, lower, first_half):
    b = _exact_dot(lower, g)
    b_last = _exact_dot(same, g)
    b_ref = _exact_dot(first_half, g)
    return b, b_last, b_ref


def _hgrn_fwd(p, lb, gh, *, batch, lp, ride=None):
    rows = batch * lp
    nb = lp // HG_BLOCK
    n_chunks = HG_BLOCK // HG_CHUNK

    def body(hq_ref, hf_ref, hi_ref, hg_ref, lb_ref, gh_ref, *rest):
        if ride is not None:
            rest, exchange = ride.split(rest, 3)
            ride.run((batch, nb), exchange)
        o_ref, z_ref, st_ref, s_scr, qt_scr, kh_scr, v_scr, el_scr, o_scr = rest

        @pl.when(pl.program_id(1) == 0)
        def _():
            s_scr[...] = jnp.zeros_like(s_scr)

        same, lower, _, first_half = _segment_masks()
        v = hi_ref[...]
        q, _, _, f = _hgrn_gates(hq_ref[...], hf_ref[...], lb_ref[...])
        k = 1.0 - f
        b, b_last, b_ref = _hgrn_decays(jnp.log(f), same, lower, first_half)
        qt_scr[...] = _mx(q * jnp.exp(b))
        kh_scr[...] = _mx(k * jnp.exp(b_last - b))
        v_scr[...] = _mx(v)
        el_scr[...] = jnp.exp(b_last)
        qc = _mx(q * jnp.exp(b - b_ref))
        kc = _mx(k * jnp.exp(b_ref - b))

        def intra(qc_h, kc_h, v_h):
            a = jnp.where(lower, _dot_nt(qc_h, kc_h), 0.0)
            return _dot(_mx(a), v_h)

        o_scr[...] = _heads(intra, qc, kc, _mx(v))

        for c in range(n_chunks):
            rs = slice(c * HG_CHUNK, (c + 1) * HG_CHUNK)
            for h in range(HEADS):
                cs = slice(h * HEAD_DIM, (h + 1) * HEAD_DIM)
                st = s_scr[h]
                st_m = _mx(st)
                st_ref[c, h] = st_m
                o_scr[rs, cs] += _dot_nt(qt_scr[rs, cs], st_m)
                s_scr[h] = st * el_scr[c * HG_CHUNK:c * HG_CHUNK + 1, cs] + _dot_tn(v_scr[rs, cs], kh_scr[rs, cs])

        o = o_scr[...]
        o_ref[...] = o
        hg = hg_ref[...]
        n = _heads(lambda o_h: o_h * _rms_scale(o_h), o) * gh_ref[...]
        z_ref[...] = (n * hg * _sigmoid(hg)).astype(z_ref.dtype)

    blk = lambda cb: pl.BlockSpec((HG_BLOCK, D_MODEL), functools.partial(lambda b, j, cb: (b * nb + j, cb), cb=cb))
    row_out = pl.BlockSpec((HG_BLOCK, D_MODEL), lambda b, j: (b * nb + j, 0))
    const = pl.BlockSpec((1, D_MODEL), lambda b, j: (0, 0))
    extra = ride if ride is not None else _NoRide
    return pl.pallas_call(
        body, name="hgrn_fwd",
        out_shape=[jax.ShapeDtypeStruct((rows, D_MODEL), F32),
                   jax.ShapeDtypeStruct((rows, D_MODEL), _MXU_DTYPE),
                   jax.ShapeDtypeStruct((rows // HG_CHUNK, HEADS, HEAD_DIM, HEAD_DIM), _MXU_DTYPE)] + extra.out_shape,
        grid=(batch, nb),
        in_specs=[blk(CB_HQ), blk(CB_HF), blk(CB_HI), blk(CB_HG), const, const] + extra.in_specs,
        out_specs=[row_out, row_out,
                   pl.BlockSpec((n_chunks, HEADS, HEAD_DIM, HEAD_DIM), lambda b, j: (b * nb + j, 0, 0, 0))]
        + extra.out_specs,
        scratch_shapes=[pltpu.VMEM((HEADS, HEAD_DIM, HEAD_DIM), F32),
                        pltpu.VMEM((HG_BLOCK, D_MODEL), _MXU_DTYPE),
                        pltpu.VMEM((HG_BLOCK, D_MODEL), _MXU_DTYPE),
                        pltpu.VMEM((HG_BLOCK, D_MODEL), _MXU_DTYPE),
                        pltpu.VMEM((HG_BLOCK, D_MODEL), F32),
                        pltpu.VMEM((HG_BLOCK, D_MODEL), F32)] + extra.scratch,
        compiler_params=_params(),
    )(p, p, p, p, lb, gh, *extra.args)


def _hgrn_bwd(p, o, dz, states, lb, gh, *, batch, lp, ride=None):
    rows = batch * lp
    nb = lp // HG_BLOCK
    n_chunks = HG_BLOCK // HG_CHUNK

    def body(hq_ref, hf_ref, hi_ref, hg_ref, o_ref, dz_ref, st_ref, lb_ref, gh_ref, *rest):
        if ride is not None:
            rest, exchange = ride.split(rest, 3)
            ride.run((batch, nb), exchange)
        (dp_ref, dlb_ref, dgh_ref,
         ds_scr, qt_scr, kh_scr, v_scr, do_scr, el_scr, dqt_scr, dkh_scr, dv_scr, dbl_scr) = rest
        first = (pl.program_id(0) == 0) & (pl.program_id(1) == 0)

        @pl.when(first)
        def _():
            dlb_ref[...] = jnp.zeros_like(dlb_ref)
            dgh_ref[...] = jnp.zeros_like(dgh_ref)

        @pl.when(pl.program_id(1) == 0)
        def _():
            ds_scr[...] = jnp.zeros_like(ds_scr)

        same, lower, upper, first_half = _segment_masks()
        lbv = lb_ref[...]
        hq, hf, v, hg = hq_ref[...], hf_ref[...], hi_ref[...], hg_ref[...]
        q, sq, sf, f = _hgrn_gates(hq, hf, lbv)
        k = 1.0 - f
        b, b_last, b_ref = _hgrn_decays(jnp.log(f), same, lower, first_half)
        e_b = jnp.exp(b)
        e_kh = jnp.exp(b_last - b)
        e_qc = jnp.exp(b - b_ref)
        e_kc = jnp.exp(b_ref - b)
        qt, kh, qc, kc = q * e_b, k * e_kh, q * e_qc, k * e_kc

        o = o_ref[...]
        dz = dz_ref[...].astype(F32)
        ghv = gh_ref[...]
        sg = _sigmoid(hg)
        r = _heads(lambda o_h: jnp.broadcast_to(_rms_scale(o_h), o_h.shape), o)
        oh = o * r
        dn = dz * hg * sg
        dhg = dz * oh * ghv * _silu_grad(hg, sg)
        w = dn * ghv
        do = r * (w - oh * _heads(lambda t: jnp.broadcast_to(jnp.mean(t, axis=-1, keepdims=True), t.shape), oh * w))
        dgh_ref[...] += (dn * oh).reshape(HG_BLOCK // _SUBLANES, _SUBLANES, D_MODEL).sum(axis=0)

        qt_scr[...] = _mx(qt)
        kh_scr[...] = _mx(kh)
        v_scr[...] = _mx(v)
        do_scr[...] = _mx(do)
        el_scr[...] = jnp.exp(b_last)

        def intra(qc_h, kc_h, v_h, do_h):
            a = _mx(jnp.where(lower, _dot_nt(qc_h, kc_h), 0.0))
            da = _mx(jnp.where(lower, _dot_nt(do_h, v_h), 0.0))
            return _dot(da, kc_h), _dot_tn(da, qc_h), _dot_tn(a, do_h)

        dqc, dkc, dv_intra = _heads(intra, _mx(qc), _mx(kc), _mx(v), _mx(do))
        dv_scr[...] = dv_intra

        for c in reversed(range(n_chunks)):
            rs = slice(c * HG_CHUNK, (c + 1) * HG_CHUNK)
            for h in range(HEADS):
                cs = slice(h * HEAD_DIM, (h + 1) * HEAD_DIM)
                st = st_ref[c, h]
                ds_t = ds_scr[h]
                ds_m = _mx(ds_t)
                el = el_scr[c * HG_CHUNK:c * HG_CHUNK + 1, cs]
                dkh_scr[rs, cs] = _dot(v_scr[rs, cs], ds_m)
                dv_scr[rs, cs] += _dot_nt(kh_scr[rs, cs], ds_m)
                dbl = jnp.sum(ds_t * st.astype(F32), axis=0, keepdims=True) * el
                dbl_scr[rs, cs] = jnp.broadcast_to(dbl, (HG_CHUNK, HEAD_DIM))
                dqt_scr[rs, cs] = _dot(do_scr[rs, cs], st)
                ds_scr[h] = ds_t * el + _dot_tn(do_scr[rs, cs], qt_scr[rs, cs])

        dqt, dkh = dqt_scr[...], dkh_scr[...]
        dq = dqt * e_b + dqc * e_qc
        dk = dkh * e_kh + dkc * e_kc
        t_kh = dkh * kh
        db_rows = dqt * qt + dqc * qc - dkc * kc - t_kh
        dg = _exact_dot(upper, db_rows) + _exact_dot(same, t_kh) + dbl_scr[...]
        df = dg / f - dk
        dhf = df * (1.0 - lbv) * sf * (1.0 - sf)
        dlb_ref[...] += (df * (1.0 - sf)).reshape(HG_BLOCK // _SUBLANES, _SUBLANES, D_MODEL).sum(axis=0)
        dhq = dq * _silu_grad(hq, sq)
        dp_ref[...] = jnp.concatenate([dhq, dhf, dv_scr[...], dhg], axis=1).astype(dp_ref.dtype)

    rev = lambda b, j: b * nb + (nb - 1 - j)
    blk = lambda cb: pl.BlockSpec((HG_BLOCK, D_MODEL), functools.partial(lambda b, j, cb: (rev(b, j), cb), cb=cb))
    row = pl.BlockSpec((HG_BLOCK, D_MODEL), lambda b, j: (rev(b, j), 0))
    const = pl.BlockSpec((1, D_MODEL), lambda b, j: (0, 0))
    acc = pl.BlockSpec((_SUBLANES, D_MODEL), lambda b, j: (0, 0))
    big = lambda dt: pltpu.VMEM((HG_BLOCK, D_MODEL), dt)
    extra = ride if ride is not None else _NoRide
    dp, dlb, dgh, *exchanged = pl.pallas_call(
        body, name="hgrn_bwd",
        out_shape=[jax.ShapeDtypeStruct((rows, 4 * D_MODEL), _MXU_DTYPE),
                   jax.ShapeDtypeStruct((_SUBLANES, D_MODEL), F32),
                   jax.ShapeDtypeStruct((_SUBLANES, D_MODEL), F32)] + extra.out_shape,
        grid=(batch, nb),
        in_specs=[blk(CB_HQ), blk(CB_HF), blk(CB_HI), blk(CB_HG), row, row,
                  pl.BlockSpec((n_chunks, HEADS, HEAD_DIM, HEAD_DIM), lambda b, j: (rev(b, j), 0, 0, 0)),
                  const, const] + extra.in_specs,
        out_specs=[pl.BlockSpec((HG_BLOCK, 4 * D_MODEL), lambda b, j: (rev(b, j), 0)), acc, acc] + extra.out_specs,
        scratch_shapes=[pltpu.VMEM((HEADS, HEAD_DIM, HEAD_DIM), F32),
                        big(_MXU_DTYPE), big(_MXU_DTYPE), big(_MXU_DTYPE), big(_MXU_DTYPE),
                        big(F32), big(F32), big(F32), big(F32), big(F32)] + extra.scratch,
        compiler_params=_params(),
    )(p, p, p, p, o, dz, states, lb, gh, *extra.args)
    return (dp, dlb, dgh, *exchanged)


QK_DIM = 2 * HEAD_DIM
ATTN_TQ = 256
ATTN_KEY_CHUNK = 512


def _query_tiles(lp):
    return [(r0, min(ATTN_TQ, lp - r0)) for r0 in range(0, lp, ATTN_TQ)]


def _attn_fwd(q_cat, kv, kp, *, batch, lp):
    rows = batch * lp

    def body(q_ref, kn_ref, kp_ref, v_ref, o_ref, lse_ref):
        k_cat = jnp.concatenate([kn_ref[...], kp_ref[...]], axis=1)
        for r0, tq in _query_tiles(lp):
            q_t = q_ref[r0:r0 + tq, :]
            i = lax.broadcasted_iota(jnp.int32, (tq, tq), 0)
            j = lax.broadcasted_iota(jnp.int32, (tq, tq), 1)
            s_diag = jnp.where(j <= i, _dot_nt(q_t, k_cat[r0:r0 + tq]) * ATTN_SCALE, NEG_BIG)
            m = jnp.max(s_diag, axis=1, keepdims=True)
            if r0:
                s_past = _dot_nt(q_t, k_cat[0:r0]) * ATTN_SCALE
                m = jnp.maximum(m, jnp.max(s_past, axis=1, keepdims=True))
            p_diag = jnp.exp(s_diag - m)
            l = jnp.sum(p_diag, axis=1, keepdims=True)
            acc = _dot(_mx(p_diag), v_ref[r0:r0 + tq, :])
            if r0:
                p_past = jnp.exp(s_past - m)
                l = l + jnp.sum(p_past, axis=1, keepdims=True)
                acc = acc + _dot(_mx(p_past), v_ref[0:r0, :])
            o_ref[r0:r0 + tq, :] = (acc / l).astype(o_ref.dtype)
            lse_ref[r0:r0 + tq, :] = jnp.broadcast_to(m + jnp.log(l), (tq, HEAD_DIM))

    head_blk = pl.BlockSpec((lp, HEAD_DIM), lambda b, h: (b, h))
    return pl.pallas_call(
        body, name="attn_fwd",
        out_shape=[jax.ShapeDtypeStruct((rows, D_MODEL), _MXU_DTYPE),
                   jax.ShapeDtypeStruct((rows, D_MODEL), F32)],
        grid=(batch, HEADS),
        in_specs=[pl.BlockSpec((lp, QK_DIM), lambda b, h: (b, h)), head_blk,
                  pl.BlockSpec((lp, HEAD_DIM), lambda b, h: (b, 0)),
                  pl.BlockSpec((lp, HEAD_DIM), lambda b, h: (b, HEADS + h))],
        out_specs=[head_blk, head_blk],
        compiler_params=_params(),
    )(q_cat, kv, kp, kv)


def _attn_bwd(q_cat, kv, kp, do, o, lse, *, batch, lp, ride=None):
    rows = batch * lp

    def body(q_ref, kn_ref, kp_ref, v_ref, do_ref, o_ref, lse_ref, *rest):
        if ride is not None:
            rest, exchange = ride.split(rest, 4)
            ride.run((batch, HEADS), exchange)
        dq_ref, dkn_ref, dkp_ref, dv_ref, dk_acc, dv_acc = rest
        dk_acc[...] = jnp.zeros_like(dk_acc)
        dv_acc[...] = jnp.zeros_like(dv_acc)
        k_cat = jnp.concatenate([kn_ref[...], kp_ref[...]], axis=1)
        k_t = k_cat.T
        lane = lax.broadcasted_iota(jnp.int32, (_SUBLANES, HEAD_DIM), 1)
        lse_row = _exact_dot_nt(lane == 0, lse_ref[...])
        delta = _exact_dot_nt(lane >= 0, do_ref[...].astype(F32) * o_ref[...].astype(F32))
        for r0, tq in _query_tiles(lp):
            cols = slice(r0, r0 + tq)
            q_t_, do_t_ = q_ref[cols, :], do_ref[cols, :]
            lse_t, delta_t = lse_row[0:1, cols], delta[0:1, cols]
            chunks = [(c0, min(ATTN_KEY_CHUNK, r0 - c0), False) for c0 in range(0, r0, ATTN_KEY_CHUNK)] + [(r0, tq, True)]
            dq_t = jnp.zeros((QK_DIM, tq), F32)
            for c0, n, diagonal in chunks:
                keys = slice(c0, c0 + n)
                s = _dot_nt(k_cat[keys], q_t_) * ATTN_SCALE
                if diagonal:
                    jk = lax.broadcasted_iota(jnp.int32, (n, tq), 0)
                    iq = lax.broadcasted_iota(jnp.int32, (n, tq), 1)
                    s = jnp.where(jk <= iq, s, NEG_BIG)
                pexp = jnp.exp(s - lse_t)
                dp = _dot_nt(v_ref[keys, :], do_t_)
                ds = _mx(pexp * (dp - delta_t) * ATTN_SCALE)
                dk_acc[keys, :] += _dot(ds, q_t_)
                dv_acc[keys, :] += _dot(_mx(pexp), do_t_)
                dq_t = dq_t + _dot(k_t[:, keys], ds)
            dq_ref[cols, :] = dq_t.T

        dkn_ref[...] = dk_acc[:, 0:HEAD_DIM].astype(dkn_ref.dtype)
        dv_ref[...] = dv_acc[...].astype(dv_ref.dtype)

        @pl.when(pl.program_id(1) == 0)
        def _():
            dkp_ref[...] = jnp.zeros_like(dkp_ref)

        dkp_ref[...] += dk_acc[:, HEAD_DIM:]

    head_blk = pl.BlockSpec((lp, HEAD_DIM), lambda b, h: (b, h))
    cat_blk = pl.BlockSpec((lp, QK_DIM), lambda b, h: (b, h))
    shared_blk = pl.BlockSpec((lp, HEAD_DIM), lambda b, h: (b, 0))
    extra = ride if ride is not None else _NoRide
    return pl.pallas_call(
        body, name="attn_bwd",
        out_shape=[jax.ShapeDtypeStruct((rows, HEADS * QK_DIM), F32),
                   jax.ShapeDtypeStruct((rows, D_MODEL), _MXU_DTYPE),
                   jax.ShapeDtypeStruct((rows, HEAD_DIM), F32),
                   jax.ShapeDtypeStruct((rows, D_MODEL), _MXU_DTYPE)] + extra.out_shape,
        grid=(batch, HEADS),
        in_specs=[cat_blk, head_blk, shared_blk, pl.BlockSpec((lp, HEAD_DIM), lambda b, h: (b, HEADS + h)),
                  head_blk, head_blk, head_blk] + extra.in_specs,
        out_specs=[cat_blk, head_blk, shared_blk, head_blk] + extra.out_specs,
        scratch_shapes=[pltpu.VMEM((lp, QK_DIM), F32), pltpu.VMEM((lp, HEAD_DIM), F32)] + extra.scratch,
        compiler_params=_params(),
    )(q_cat, kv, kp, kv, do, o, lse, *extra.args)


def _all_gather(name, blocks):
    n = len(blocks)

    def body(*refs):
        x_refs, out_refs, (send_sems, recv_sems, local_sems) = refs[:n], refs[n:2 * n], refs[2 * n:]
        x, y, c = lax.axis_index("x"), lax.axis_index("y"), lax.axis_index("c")
        me, sibling = (x, y, c), (x, y, 1 - c)
        chips = [(1 - x, y), (x, 1 - y), (1 - x, 1 - y)]

        def slot(i, px, py, pc):
            return out_refs[i].at[4 * px + 2 * py + pc]

        def copy(i, k, blk, to, src=None):
            return pltpu.make_async_remote_copy(
                src_ref=slot(i, *blk) if src is None else src, dst_ref=slot(i, *blk),
                send_sem=send_sems.at[i, k], recv_sem=recv_sems.at[i, k],
                device_id=to, device_id_type=pl.DeviceIdType.MESH)

        mine = [pltpu.make_async_copy(x_refs[i], slot(i, *me), local_sems.at[i]) for i in range(n)]
        first = [copy(i, 0, me, sibling, src=x_refs[i]) for i in range(n)]
        first += [copy(i, 1 + j, me, (*chip, c), src=x_refs[i]) for i in range(n) for j, chip in enumerate(chips)]
        for cp in mine + first:
            cp.start()
        passed = []
        for i in range(n):
            for j, chip in enumerate(chips):
                copy(i, 1 + j, (*chip, c), me).wait_recv()
                passed.append(copy(i, 4 + j, (*chip, c), sibling))
                passed[-1].start()
        for i in range(n):
            copy(i, 0, sibling, me).wait_recv()
            for j, chip in enumerate(chips):
                copy(i, 4 + j, (*chip, 1 - c), me).wait_recv()
        for cp in first + passed:
            cp.wait_send()
        for cp in mine:
            cp.wait()

    return pl.pallas_call(
        body, name=name,
        out_shape=[jax.ShapeDtypeStruct((N_DEV, *b.shape), b.dtype) for b in blocks],
        in_specs=[pl.BlockSpec(memory_space=pl.ANY)] * n,
        out_specs=[pl.BlockSpec(memory_space=pl.ANY)] * n,
        scratch_shapes=[pltpu.SemaphoreType.DMA((n, 7)), pltpu.SemaphoreType.DMA((n, 7)),
                        pltpu.SemaphoreType.DMA((n,))],
    )(*blocks)


def _adamw_math(w, g, m, v):
    nm = ADAM_B1 * m + (1.0 - ADAM_B1) * g
    nv = ADAM_B2 * v + (1.0 - ADAM_B2) * (g * g)
    m_hat = nm / (1.0 - ADAM_B1 ** ADAM_STEP)
    v_hat = nv / (1.0 - ADAM_B2 ** ADAM_STEP)
    return -ADAM_LR * (m_hat / (jnp.sqrt(v_hat) + ADAM_EPS) + ADAM_WD * w), nm, nv


def _sum_adamw(name, parts, w, m, v):
    rows, cols = w.shape
    tr = rows // 4 if rows % 64 == 0 and rows * cols > (1 << 16) else rows

    def body(p_ref, w_ref, m_ref, v_ref, g_ref, d_ref, nm_ref, nv_ref):
        g = p_ref[0].astype(F32)
        for dev in range(1, N_DEV):
            g = g + p_ref[dev].astype(F32)
        g_ref[...] = g
        d_ref[...], nm_ref[...], nv_ref[...] = _adamw_math(w_ref[...], g, m_ref[...], v_ref[...])

    spec = pl.BlockSpec((tr, cols), lambda i: (i, 0))
    return pl.pallas_call(
        body, name=name,
        out_shape=[jax.ShapeDtypeStruct((rows, cols), F32)] * 4,
        grid=(rows // tr,),
        in_specs=[pl.BlockSpec((N_DEV, tr, cols), lambda i: (0, i, 0))] + [spec] * 3, out_specs=[spec] * 4,
        compiler_params=_params(),
    )(parts, w, m, v)


def _finish_vectors(gathered, lb, params, loss_parts):
    names = list(params)
    n = len(names)

    def body(*refs):
        g_refs, lb_ref, loss_ref = refs[:n], refs[n], refs[n + 1]
        wmv_refs = refs[n + 2:4 * n + 2]
        out_refs, loss_out = refs[4 * n + 2:-1], refs[-1]
        sq = loss_ref[0]
        for dev in range(1, N_DEV):
            sq = sq + loss_ref[dev]
        sq = jnp.sum(jnp.sum(sq, axis=0, keepdims=True), axis=1, keepdims=True)
        loss_out[...] = sq * (0.5 / D_MODEL)
        me = 4 * lax.axis_index("x") + 2 * lax.axis_index("y") + lax.axis_index("c")
        for i, name in enumerate(names):
            g_ref = g_refs[i]
            w_ref, m_ref, v_ref = wmv_refs[3 * i:3 * i + 3]
            if name == "meta_tokens":
                width = w_ref.shape[1]
                mine = pl.ds(pl.multiple_of(me * width, width), width)
                g = g_ref[0, :, mine]
                for dev in range(1, N_DEV):
                    g = g + g_ref[dev, :, mine]
            else:
                g = g_ref[0]
                for dev in range(1, N_DEV):
                    g = g + g_ref[dev]
                g = jnp.sum(g, axis=0, keepdims=True)
                if name == "hg_norm_g":
                    g = functools.reduce(jnp.add, [g[:, h * HEAD_DIM:(h + 1) * HEAD_DIM] for h in range(HEADS)])
                if name == "lb_logits":
                    lbv = lb_ref[...]
                    g = g * lbv * (1.0 - lbv)
                    g = jnp.concatenate([g, -g], axis=0)
            outs = (g, *_adamw_math(w_ref[...], g, m_ref[...], v_ref[...]))
            for ref, val in zip(out_refs[4 * i:4 * i + 4], outs, strict=True):
                ref[...] = val

    args = [gathered[k] for k in names] + [lb, loss_parts] + [t for k in names for t in params[k]]
    res = pl.pallas_call(
        body, name="finish_vectors",
        out_shape=[jax.ShapeDtypeStruct(params[k][0].shape, F32) for k in names for _ in range(4)]
        + [jax.ShapeDtypeStruct((1, 1), F32)],
        compiler_params=_params(),
    )(*args)
    return {k: res[4 * i:4 * i + 4] for i, k in enumerate(names)}, res[-1].reshape(())


def _swap_halves(t):
    half = t.shape[-1] // 2
    return jnp.concatenate([t[..., half:], t[..., :half]], axis=-1)


def _pad_last(t, width):
    return jnp.concatenate([t, jnp.zeros(t.shape[:-1] + (width - t.shape[-1],), t.dtype)], axis=-1)


def _rope_tables(lp):
    pos = jnp.arange(lp, dtype=F32)
    inv_freq = 1.0 / (ROPE_THETA ** (jnp.arange(0, ROPE_DIM, 2, dtype=F32) / ROPE_DIM))
    ang = pos[:, None] * inv_freq[None, :]
    cos, sin = jnp.cos(ang), jnp.sin(ang)
    c128 = _pad_last(jnp.concatenate([cos, cos], axis=1), HEAD_DIM)
    s128 = _pad_last(jnp.concatenate([-sin, sin], axis=1), HEAD_DIM)
    return c128, s128


def _forward_backward(x, target, meta, w, small, *, lp, comm=None):
    batch, seq, d = x.shape
    rows = batch * lp
    tr = 272 if lp % 272 == 0 else 128
    tm = lp // 2
    bf = _MXU_DTYPE
    rw = functools.partial(_rowwise, rows=rows, tr=tr, lp=lp)

    c128, s128 = _rope_tables(lp)
    cq_tab, sq_tab = jnp.tile(c128, (1, HEADS)), jnp.tile(s128, (1, HEADS))
    t_idx = jnp.arange(lp)
    real = jnp.broadcast_to(((t_idx >= N_META) & (t_idx < N_META + seq)).astype(F32)[:, None], (lp, _LANES))

    lb_logits = small["lb_logits"]
    lb = jax.nn.softmax(lb_logits, axis=0)[0:1]
    gh = jnp.tile(small["hg_norm_g"], (1, HEADS))

    h0 = _assemble("assemble_x", x, meta, lp)
    tgt = _assemble("assemble_target", target, jnp.zeros_like(meta), lp)

    (u1,), _ = rw("norm_mix_pre", lambda h, g: ([h * _rms_scale(h) * g], []),
                  ins=[("row", h0, d, 0), ("const", small["mix_pre_g"])], outs=[(d, bf)])
    p = _matmul("proj_in", u1, w["w_in"], out_dtype=F32, tm=tm, tn=1024, tk=1024)

    if comm is None:
        o_hg, z_a, states = _hgrn_fwd(p, lb, gh, batch=batch, lp=lp)
    else:
        o_hg, z_a, states, *gathered = _hgrn_fwd(p, lb, gh, batch=batch, lp=lp, ride=_Ride(comm.rest_payloads, True))
        w = {**w, **comm.rest_weights(gathered)}
    received = []
    scatter = lambda names: _Ride(comm.grad_parts(names, grads), False) if comm is not None else None
    y_a = _matmul("proj_hg_o", z_a, w["w_hg_o"], out_dtype=F32, tm=tm, tn=1024, tk=1024)

    def mla_pre(pc, gq, gkv, ct, st):
        cq, ckv = pc[:, 0:Q_LORA], pc[:, Q_LORA:Q_LORA + KV_LORA]
        kpe, kpe_sw = pc[:, 512:640], pc[:, 640:768]
        return [cq * _rms_scale(cq) * gq, ckv * _rms_scale(ckv) * gkv, kpe * ct + kpe_sw * st], []

    (cqn, ckvn, kp), _ = rw("mla_pre", mla_pre,
                            ins=[("row", p, 1024, CB_C), ("const", small["q_a_norm_g"]),
                                 ("const", small["kv_a_norm_g"]), ("pos", c128), ("pos", s128)],
                            outs=[(Q_LORA, bf), (KV_LORA, bf), (HEAD_DIM, bf)])
    q_cat = _proj_q_rope(cqn, w["w_q"], c128, s128, tm=tm, lp=lp)
    kv = _matmul("proj_kv_b", ckvn, w["w_kv"], out_dtype=bf, tm=tm, tn=1024, tk=KV_LORA)
    o_at, lse = _attn_fwd(q_cat, kv, kp, batch=batch, lp=lp)
    y_b = _matmul("proj_mla_o", o_at, w["w_mla_o"], out_dtype=F32, tm=tm, tn=1024, tk=1024)

    def merge(pa, pb, ya, yb, bg):
        ga, gb = _sigmoid(pa + bg[:, :d]), _sigmoid(pb + bg[:, d:])
        return [ga * ya + gb * yb], []

    (mix,), _ = rw("merge", merge,
                   ins=[("row", p, 1024, CB_GA), ("row", p, 1024, CB_GB), ("row", y_a, d, 0), ("row", y_b, d, 0),
                        ("const", small["b_gate"])], outs=[(d, bf)])
    te = lp // 4

    def post_mix(mx_, h, g2, g3):
        h1_ = h + mx_ * _rms_scale(mx_) * g2
        return [mx_, h1_, h1_ * _rms_scale(h1_) * g3], []

    (mixed, h1, u2), _ = _matmul_segments(
        "proj_out", [mix], w["w_out"], tm=te, tn=d, tk=1024,
        epilogue=_Epilogue(post_mix, rows=[h0], consts=[small["mix_post_g"], small["ffn_pre_g"]],
                           outs=[(d, F32), (d, F32), (d, bf)], lp=lp))
    act, gt, up = _ffn_in_swiglu(u2, w["w_ffn_in"], tm=tm, tn=1408)

    def post_ffn(fo_, h1_, t_, g4, mask):
        r = _rms_scale(fo_)
        h2 = h1_ + fo_ * r * g4
        err = (h2 - t_) * mask[:, 0:1]
        dh2 = err * (1.0 / d)
        dfo, dg4 = _rms_bwd(fo_, g4, dh2)
        return [dh2, dfo], [err * err, dg4]

    (dh2, dfo), (loss_vec, dg_ffn_post) = _matmul_segments(
        "ffn_out", [act], w["w_ffn_out"], tm=te, tn=d, tk=1408,
        epilogue=_Epilogue(post_ffn, rows=[h1, tgt], consts=[small["ffn_post_g"]], pos=[real],
                           outs=[(d, F32), (d, bf)], accs=[d, d], lp=lp))
    loss = (0.5 / d) * jnp.sum(loss_vec)

    grads = {}
    dw_dt = F32 if comm is None else _WIRE_DTYPE
    dgt, dup = _d_ffn_out_swiglu(dfo, w["w_ffn_out"], gt, up, tm=tm, tn=1408)
    grads["w_ffn_out"] = _matmul_tn("dw_ffn_out", act, dfo, tk=1408, tn=1024, tr=tm, out_dtype=dw_dt)
    grads["w_ffn_in"] = jnp.concatenate([_matmul_tn("dw_ffn_in_gate", u2, dgt, tk=1024, tn=1408, tr=tm),
                                         _matmul_tn("dw_ffn_in_up", u2, dup, tk=1024, tn=1408, tr=tm)], axis=1)

    def post_mix_bwd(du2_, h1_, dh2_, mx_, g3, g2):
        dx, dg3 = _rms_bwd(h1_, g3, du2_)
        dh1_ = dh2_ + dx
        dmx, dg2 = _rms_bwd(mx_, g2, dh1_)
        return [dh1_, dmx], [dg3, dg2]

    (dh1, dmixed), (dg_ffn_pre, dg_mix_post) = _matmul_segments(
        "d_ffn_in", [dgt, dup], w["w_ffn_in"], tm=te, tn=d, tk=1408, b_transposed=True,
        epilogue=_Epilogue(post_mix_bwd, rows=[h1, dh2, mixed], consts=[small["ffn_pre_g"], small["mix_post_g"]],
                           outs=[(d, F32), (d, bf)], accs=[d, d], lp=lp))
    grads["w_out"] = _matmul_tn("dw_out", mix, dmixed, tk=1024, tn=1024, tr=tm, out_dtype=dw_dt)

    def merge_bwd(dm, pa, pb, ya, yb, bg):
        ga, gb = _sigmoid(pa + bg[:, :d]), _sigmoid(pb + bg[:, d:])
        dpg = jnp.concatenate([dm * ya * ga * (1.0 - ga), dm * yb * gb * (1.0 - gb)], axis=1)
        return [dpg, dm * ga, dm * gb], [dpg]

    (dpg, dya, dyb), (db_gate,) = _matmul_segments(
        "d_proj_out", [dmixed], w["w_out"], tm=te, tn=d, tk=1024, b_transposed=True,
        epilogue=_Epilogue(merge_bwd, rows=[(p, 1024, CB_GA), (p, 1024, CB_GB), y_a, y_b], consts=[small["b_gate"]],
                           outs=[(2 * d, bf), (d, bf), (d, bf)], accs=[2 * d], lp=lp))
    dz_a = _matmul("d_proj_hg_o", dya, w["w_hg_o"], out_dtype=F32, tm=tm, tn=1024, tk=1024, b_transposed=True)
    grads["w_hg_o"] = _matmul_tn("dw_hg_o", z_a, dya, tk=1024, tn=1024, tr=tm, out_dtype=dw_dt)
    do_at = _matmul("d_proj_mla_o", dyb, w["w_mla_o"], out_dtype=bf, tm=tm, tn=1024, tk=1024, b_transposed=True)
    grads["w_mla_o"] = _matmul_tn("dw_mla_o", o_at, dyb, tk=1024, tn=1024, tr=tm, out_dtype=dw_dt)

    dph, dlb, dgh, *got = _hgrn_bwd(p, o_hg, dz_a, states, lb, gh, batch=batch, lp=lp,
                                    ride=scatter(_GRAD_GROUPS[0]))
    received.append(got)

    dq_cat, dkn, dkp, dvv, *got = _attn_bwd(q_cat, kv, kp, do_at, o_at, lse, batch=batch, lp=lp,
                                            ride=scatter(_GRAD_GROUPS[1]))
    received.append(got)

    def rope_q_bwd(dq, ct, st):
        hs = lambda half: jnp.concatenate(
            [dq[:, h * QK_DIM + half * HEAD_DIM:h * QK_DIM + (half + 1) * HEAD_DIM] for h in range(HEADS)], axis=1)
        dpe = hs(1)
        return [jnp.concatenate([hs(0), dpe * ct, dpe * st], axis=1)], []

    (dqf,), _ = rw("rope_q_bwd", rope_q_bwd,
                   ins=[("row", dq_cat, HEADS * QK_DIM, 0), ("pos", cq_tab), ("pos", sq_tab)],
                   outs=[(3 * d, bf)])
    dcqn = _matmul("d_proj_q_b", dqf, w["w_q"], out_dtype=F32, tm=tm, tn=Q_LORA, tk=1024, b_transposed=True)
    grads["w_q"] = _matmul_tn("dw_q_b", cqn, dqf, tk=Q_LORA, tn=1024, tr=tm)
    dckvn = _matmul_segments("d_proj_kv_b", [dkn, dvv], w["w_kv"], tm=tm, tn=KV_LORA, tk=1024, b_transposed=True)
    grads["w_k"] = _matmul_tn("dw_k_b", ckvn, dkn, tk=KV_LORA, tn=1024, tr=tm)
    grads["w_v"] = _matmul_tn("dw_v_b", ckvn, dvv, tk=KV_LORA, tn=1024, tr=tm)

    def mla_pre_bwd(pc, dq_, dkv_, dkp_, gq, gkv, ct, st):
        cq, ckv = pc[:, 0:Q_LORA], pc[:, Q_LORA:Q_LORA + KV_LORA]
        dcq, dgq = _rms_bwd(cq, gq, dq_)
        dckv, dgkv = _rms_bwd(ckv, gkv, dkv_)
        dpc = jnp.concatenate([dcq, dckv, dkp_ * ct, dkp_ * st, jnp.zeros((pc.shape[0], 256), F32)], axis=1)
        return [dpc], [dgq, dgkv]

    (dpc,), (dg_q, dg_kv) = rw(
        "mla_pre_bwd", mla_pre_bwd,
        ins=[("row", p, 1024, CB_C), ("row", dcqn, Q_LORA, 0), ("row", dckvn, KV_LORA, 0), ("row", dkp, HEAD_DIM, 0),
             ("const", small["q_a_norm_g"]), ("const", small["kv_a_norm_g"]), ("pos", c128), ("pos", s128)],
        outs=[(1024, bf)], accs=[Q_LORA, KV_LORA])

    grads["w_in"] = jnp.concatenate([
        _matmul_tn("dw_in_h", u1, dph, tk=1024, tn=1024, tr=tm),
        _matmul_tn("dw_in_c", u1, dpc, tk=1024, tn=1024, tr=tm),
        _matmul_tn("dw_in_g", u1, dpg, tk=1024, tn=1024, tr=tm)], axis=1)
    def pre_bwd(du, h, dh, g1):
        dx, dg1 = _rms_bwd(h, g1, du)
        return [dh + dx], [dg1]

    (dh0,), (dg_mix_pre,), *got = _matmul_segments(
        "d_proj_in", [dph, dpc, dpg], w["w_in"], tm=te, tn=d, tk=1024, b_transposed=True,
        ride=scatter(_GRAD_GROUPS[2]),
        epilogue=_Epilogue(pre_bwd, rows=[h0, dh1], consts=[small["mix_pre_g"]], outs=[(d, F32)], accs=[d], lp=lp))
    if comm is not None:
        received.append(got)
    grad_x = dh0.reshape(batch, lp, d)[:, N_META:N_META + seq]
    partial = {"meta_tokens": _meta_grad(dh0, batch, lp), "lb_logits": dlb, "b_gate": db_gate, "hg_norm_g": dgh,
               "q_a_norm_g": dg_q, "kv_a_norm_g": dg_kv, "mix_pre_g": dg_mix_pre, "mix_post_g": dg_mix_post,
               "ffn_pre_g": dg_ffn_pre, "ffn_post_g": dg_ffn_post, "loss": loss_vec}
    return loss, grad_x, grads, partial, lb, received


_BIG = ["w_in", "w_hg_o", "w_q_b", "w_kv_b", "w_mla_o", "w_out", "w_ffn_in", "w_ffn_out"]
_COLUMN_SHARDED = {"w_in", "w_q_b", "w_kv_b", "w_ffn_in"}
_GRAD_GROUPS = [["w_ffn_in", "w_ffn_out"], ["w_out", "w_hg_o", "w_mla_o"], ["w_in", "w_q_b", "w_kv_b"]]
_SMALL = ["b_gate", "lb_logits", "hg_norm_g", "q_a_norm_g", "kv_a_norm_g", "mix_pre_g", "mix_post_g",
          "ffn_pre_g", "ffn_post_g"]


def _gathered_matrix(name, t):
    _, k, n = t.shape
    if name in _COLUMN_SHARDED:
        return t.transpose(1, 0, 2).reshape(k, N_DEV * n)
    return t.reshape(N_DEV * k, n)


def _scatter_layout(name, full):
    kk, nn = full.shape
    if name in _COLUMN_SHARDED:
        t = full.reshape(kk, N_DEV, nn // N_DEV).transpose(1, 0, 2)
    else:
        t = full.reshape(N_DEV, kk // N_DEV, nn)
    return t.astype(_WIRE_DTYPE)


def _model_w_in(wi):
    z = lambda *s: jnp.zeros(s, wi.dtype)
    kpe = wi[:, 4608:4672]
    c_blk = jnp.concatenate([wi[:, 4096:4608], kpe, z(1024, 64), _swap_halves(kpe), z(1024, 64), z(1024, 256)], axis=1)
    return {"w_in": jnp.concatenate([wi[:, :4096], c_blk, wi[:, 4672:]], axis=1).astype(_MXU_DTYPE)}


def _model_weights(full):
    return {**_model_w_in(full["w_in"]), **_model_rest(full)}


def _model_rest(full):
    wq3 = full["w_q_b"].reshape(Q_LORA, HEADS, HEAD_DIM + ROPE_DIM)
    pe = wq3[:, :, HEAD_DIM:]
    w_q = jnp.concatenate([wq3[:, :, :HEAD_DIM].reshape(Q_LORA, -1),
                           _pad_last(pe, HEAD_DIM).reshape(Q_LORA, -1),
                           _pad_last(_swap_halves(pe), HEAD_DIM).reshape(Q_LORA, -1)], axis=1)
    wkv3 = full["w_kv_b"].reshape(KV_LORA, HEADS, 2 * HEAD_DIM)
    w_k = wkv3[:, :, :HEAD_DIM].reshape(KV_LORA, -1)
    w_v = wkv3[:, :, HEAD_DIM:].reshape(KV_LORA, -1)
    w = {"w_q": w_q, "w_kv": jnp.concatenate([w_k, w_v], axis=1),
         "w_hg_o": full["w_hg_o"], "w_mla_o": full["w_mla_o"], "w_out": full["w_out"],
         "w_ffn_in": full["w_ffn_in"], "w_ffn_out": full["w_ffn_out"]}
    return {k: v.astype(_MXU_DTYPE) for k, v in w.items()}


def _reference_layout_grad(name, g):
    if name == "w_in":
        gi = g["w_in"]
        d_kpe = gi[:, 4608:4672] + _swap_halves(gi[:, 4736:4800])
        return jnp.concatenate([gi[:, :4608], d_kpe, gi[:, 5120:]], axis=1)
    if name == "w_q_b":
        gq = g["w_q"]
        d_pe = (gq[:, 1024:2048].reshape(Q_LORA, HEADS, HEAD_DIM)[:, :, :ROPE_DIM]
                + _swap_halves(gq[:, 2048:].reshape(Q_LORA, HEADS, HEAD_DIM)[:, :, :ROPE_DIM]))
        return jnp.concatenate([gq[:, :1024].reshape(Q_LORA, HEADS, HEAD_DIM), d_pe], axis=2).reshape(Q_LORA, -1)
    if name == "w_kv_b":
        return jnp.concatenate([g["w_k"].reshape(KV_LORA, HEADS, HEAD_DIM),
                                g["w_v"].reshape(KV_LORA, HEADS, HEAD_DIM)], axis=2).reshape(KV_LORA, -1)
    return g[name]


def _reference_layout_grads(g):
    return {n: _reference_layout_grad(n, g) for n in _BIG}


class _Comm:
    def __init__(self, shard):
        self.rest_payloads = [shard[n].astype(_WIRE_DTYPE) for n in _BIG[1:]]

    def rest_weights(self, gathered):
        return _model_rest({n: _gathered_matrix(n, t) for n, t in zip(_BIG[1:], gathered, strict=True)})

    def grad_parts(self, names, g):
        return [_scatter_layout(n, _reference_layout_grad(n, g)) for n in names]


def kernel(x, meta_tokens, w_in, b_gate, lb_logits, hg_norm_g, w_hg_o, q_a_norm_g, w_q_b, kv_a_norm_g, w_kv_b, w_mla_o, w_out, mix_pre_g, mix_post_g, ffn_pre_g, ffn_post_g, w_ffn_in, w_ffn_out, loss_target, m_meta_tokens, m_w_in, m_b_gate, m_lb_logits, m_hg_norm_g, m_w_hg_o, m_q_a_norm_g, m_w_q_b, m_kv_a_norm_g, m_w_kv_b, m_w_mla_o, m_w_out, m_mix_pre_g, m_mix_post_g, m_ffn_pre_g, m_ffn_post_g, m_w_ffn_in, m_w_ffn_out, v_meta_tokens, v_w_in, v_b_gate, v_lb_logits, v_hg_norm_g, v_w_hg_o, v_q_a_norm_g, v_w_q_b, v_kv_a_norm_g, v_w_kv_b, v_w_mla_o, v_w_out, v_mix_pre_g, v_mix_post_g, v_ffn_pre_g, v_ffn_post_g, v_w_ffn_in, v_w_ffn_out):
    args = dict(locals())
    batch, seq, d = x.shape
    lp = -(-(N_META + seq) // _LANES) * _LANES
    weight_names = ["meta_tokens", "w_in", "b_gate", "lb_logits", "hg_norm_g", "w_hg_o", "q_a_norm_g", "w_q_b",
                    "kv_a_norm_g", "w_kv_b", "w_mla_o", "w_out", "mix_pre_g", "mix_post_g", "ffn_pre_g",
                    "ffn_post_g", "w_ffn_in", "w_ffn_out"]
    shard = {n: args[n].reshape(args[n].shape[-2:]) for n in _BIG}
    comm = _Comm(shard)

    w_in_all, meta_all = _all_gather("gather_first", [shard["w_in"].astype(_WIRE_DTYPE), meta_tokens])
    w_first = _model_w_in(_gathered_matrix("w_in", w_in_all))
    meta_full = meta_all.transpose(1, 0, 2).reshape(N_META, d)
    small = {n: args[n] for n in _SMALL}

    _, grad_x, _, partial, lb, received = _forward_backward(x, loss_target, meta_full, w_first, small, lp=lp, comm=comm)
    out = {}
    for names, bufs in zip(_GRAD_GROUPS, received, strict=True):
        for n, buf in zip(names, bufs, strict=True):
            two_d = lambda t: t.reshape(t.shape[-2:])
            res = _sum_adamw("adamw_" + n, buf, shard[n], two_d(args["m_" + n]), two_d(args["v_" + n]))
            out[n] = [t.reshape(args[n].shape) for t in res]

    vec_names = _SMALL + ["meta_tokens"]
    *gathered, loss_parts = _all_gather("gather_vectors", [partial[n] for n in vec_names + ["loss"]])
    finished, loss = _finish_vectors(dict(zip(vec_names, gathered, strict=True)), lb,
                                     {n: (args[n], args["m_" + n], args["v_" + n]) for n in vec_names}, loss_parts)
    out.update(finished)
    return (loss, grad_x, *[out[n][i] for i in range(4) for n in weight_names])
```

```python
import functools

import jax
import jax.numpy as jnp
from jax import lax
from jax.experimental import pallas as pl
from jax.experimental.pallas import tpu as pltpu

F32 = jnp.float32
_MXU_DTYPE = jnp.bfloat16
_WIRE_DTYPE = jnp.bfloat16
_VMEM_LIMIT_BYTES = 56 * 1024 * 1024
_LANES = 128
_SUBLANES = 8

N_DEV = 8
N_META = 16
NORM_EPS = 1e-6
HEADS = 8
HEAD_DIM = 128
ROPE_DIM = 64
HG_CHUNK = 16
HG_BLOCK = 128
ROPE_THETA = 10000.0
D_MODEL = 1024
Q_LORA = 256
KV_LORA = 256
FFN_HIDDEN = 2816
ATTN_SCALE = (HEAD_DIM + ROPE_DIM) ** -0.5
NEG_BIG = -1e30

ADAM_LR = 0.001
ADAM_B1 = 0.9
ADAM_B2 = 0.999
ADAM_EPS = 1e-08
ADAM_WD = 0.01
ADAM_STEP = 10

CB_HQ, CB_HF, CB_HI, CB_HG, CB_C, CB_GA, CB_GB = range(7)
IN_COLS_PADDED = 7 * 1024


def _params(**kw):
    return pltpu.CompilerParams(vmem_limit_bytes=_VMEM_LIMIT_BYTES, **kw)


def _dot(a, b):
    return lax.dot_general(a, b, (((1,), (0,)), ((), ())), preferred_element_type=F32)


def _dot_nt(a, b):
    return lax.dot_general(a, b, (((1,), (1,)), ((), ())), preferred_element_type=F32)


def _dot_tn(a, b):
    return lax.dot_general(a, b, (((0,), (0,)), ((), ())), preferred_element_type=F32)


def _mx(x):
    return x.astype(_MXU_DTYPE)


def _exact_dot(m01, x, dot=_dot):
    if _MXU_DTYPE == jnp.float32:
        return dot(m01.astype(F32), x)
    m = m01.astype(jnp.bfloat16)
    x1 = x.astype(jnp.bfloat16)
    x2 = (x - x1.astype(F32)).astype(jnp.bfloat16)
    return dot(m, x1) + dot(m, x2)


def _exact_dot_nt(m01, x):
    return _exact_dot(m01, x, dot=_dot_nt)


def _sigmoid(x):
    return jax.nn.sigmoid(x)


def _silu_grad(x, s):
    return s * (1.0 + x * (1.0 - s))


def _rms_scale(x):
    return lax.rsqrt(jnp.mean(x * x, axis=-1, keepdims=True) + NORM_EPS)


def _rms_bwd(x, g, dy):
    r = _rms_scale(x)
    xh = x * r
    w = dy * g
    dx = r * (w - xh * jnp.mean(xh * w, axis=-1, keepdims=True))
    return dx, dy * xh


def _heads(fn, *arrays):
    outs = [fn(*[a[:, h * HEAD_DIM:(h + 1) * HEAD_DIM] for a in arrays]) for h in range(HEADS)]
    if isinstance(outs[0], tuple):
        return tuple(jnp.concatenate([o[i] for o in outs], axis=1) for i in range(len(outs[0])))
    return jnp.concatenate(outs, axis=1)


class _Ride:
    def __init__(self, payloads, gather):
        self.gather, self.args, self.n = gather, list(payloads), len(payloads)
        self.in_specs = [pl.BlockSpec(memory_space=pl.ANY)] * self.n
        self.out_shape = [jax.ShapeDtypeStruct((N_DEV, *p.shape[-2:]), p.dtype) for p in payloads]
        self.out_specs = [pl.BlockSpec(memory_space=pl.ANY)] * self.n
        self.scratch = [pltpu.SemaphoreType.DMA((self.n, N_DEV - 1)), pltpu.SemaphoreType.DMA((self.n, N_DEV - 1)),
                        pltpu.SemaphoreType.DMA((self.n,))]

    def split(self, rest, n_outs):
        n = self.n
        mine = (rest[:n], rest[n + n_outs:2 * n + n_outs], rest[-3:])
        return rest[n:n + n_outs] + rest[2 * n + n_outs:-3], mine

    def _copies(self, p_refs, out_refs, sems):
        send_sems, recv_sems, local_sems = sems
        x, y, c = lax.axis_index("x"), lax.axis_index("y"), lax.axis_index("c")
        me = 4 * x + 2 * y + c
        copies = []
        for i, (p_ref, out_ref) in enumerate(zip(p_refs, out_refs, strict=True)):
            part = (lambda j, p_ref=p_ref: p_ref) if self.gather else (lambda j, p_ref=p_ref: p_ref.at[j])
            copies.append(pltpu.make_async_copy(part(me), out_ref.at[me], local_sems.at[i]))
            for k in range(1, N_DEV):
                px, py, pc = x ^ (k >> 2), y ^ ((k >> 1) & 1), c ^ (k & 1)
                copies.append(pltpu.make_async_remote_copy(
                    src_ref=part(4 * px + 2 * py + pc), dst_ref=out_ref.at[me],
                    send_sem=send_sems.at[i, k - 1], recv_sem=recv_sems.at[i, k - 1],
                    device_id=(px, py, pc), device_id_type=pl.DeviceIdType.MESH))
        return copies

    def run(self, grid, refs):
        ids = [pl.program_id(i) for i in range(len(grid))]
        first = functools.reduce(jnp.logical_and, [i == 0 for i in ids])
        last = functools.reduce(jnp.logical_and, [i == g - 1 for i, g in zip(ids, grid)])

        @pl.when(first)
        def _():
            for cp in self._copies(*refs):
                cp.start()

        @pl.when(last)
        def _():
            for cp in self._copies(*refs):
                cp.wait()


class _NoRide:
    in_specs, out_shape, out_specs, scratch, args = [], [], [], [], []


def _matmul(name, a, b, *, out_dtype, tm, tn, tk, c_in=None, ride=None, b_transposed=False):
    m, k = a.shape
    n = b.shape[0] if b_transposed else b.shape[1]
    assert m % tm == 0 and n % tn == 0 and k % tk == 0, (name, a.shape, b.shape, tm, tn, tk)
    nk = k // tk
    has_c = c_in is not None
    dot = _dot_nt if b_transposed else _dot
    grid = (n // tn, m // tm, nk)
    n_in = 2 + has_c

    def body(*refs):
        a_ref, b_ref = refs[0], refs[1]
        c_ref = refs[2] if has_c else None
        rest = refs[n_in:]
        if ride is not None:
            rest, exchange = ride.split(rest, 1)
            ride.run(grid, exchange)
        o_ref = rest[0]
        acc_ref = rest[1] if nk > 1 else None

        def finish(r):
            if has_c:
                r = r + c_ref[...]
            o_ref[...] = r.astype(o_ref.dtype)

        if nk == 1:
            finish(dot(a_ref[...], b_ref[...]))
        else:
            kk = pl.program_id(2)

            @pl.when(kk == 0)
            def _():
                acc_ref[...] = jnp.zeros_like(acc_ref)

            acc_ref[...] += dot(a_ref[...], b_ref[...])

            @pl.when(kk == nk - 1)
            def _():
                finish(acc_ref[...])

    in_specs = [pl.BlockSpec((tm, tk), lambda j, i, kk: (i, kk)),
                pl.BlockSpec((tn, tk), lambda j, i, kk: (j, kk)) if b_transposed
                else pl.BlockSpec((tk, tn), lambda j, i, kk: (kk, j))]
    args = [a, b]
    aliases = {}
    if has_c:
        in_specs.append(pl.BlockSpec((tm, tn), lambda j, i, kk: (i, j)))
        args.append(c_in)
        aliases = {2: 0}
    out_shape = [jax.ShapeDtypeStruct((m, n), out_dtype)]
    out_specs = [pl.BlockSpec((tm, tn), lambda j, i, kk: (i, j))]
    scratch = [pltpu.VMEM((tm, tn), F32)] if nk > 1 else []
    if ride is not None:
        in_specs, args = in_specs + ride.in_specs, args + ride.args
        out_shape, out_specs, scratch = out_shape + ride.out_shape, out_specs + ride.out_specs, scratch + ride.scratch
    res = pl.pallas_call(
        body, name=name, out_shape=out_shape, grid=grid, in_specs=in_specs, out_specs=out_specs,
        scratch_shapes=scratch, input_output_aliases=aliases, compiler_params=_params(),
    )(*args)
    return res[0] if ride is None else res


class _Epilogue:
    def __init__(self, fn, *, rows=(), consts=(), pos=(), outs=(), accs=(), lp=None):
        self.fn, self.rows, self.consts, self.pos = fn, list(rows), list(consts), list(pos)
        self.outs, self.accs, self.lp = list(outs), list(accs), lp


def _matmul_segments(name, a_list, b, *, out_dtype=F32, tm, tn, tk, ride=None, b_transposed=False, epilogue=None):
    m = a_list[0].shape[0]
    n, k = b.shape if b_transposed else b.shape[::-1]
    steps = [a.shape[1] // tk for a in a_list]
    offs = [sum(steps[:s]) for s in range(len(steps))]
    nk = sum(steps)
    assert nk * tk == k and m % tm == 0 and n % tn == 0 and all(a.shape[1] % tk == 0 for a in a_list), name
    grid = (n // tn, m // tm, nk)
    n_seg = len(a_list)
    dot = _dot_nt if b_transposed else _dot
    ep = epilogue
    assert ep is None or tn == n, name
    n_extra = 0 if ep is None else len(ep.rows) + len(ep.consts) + len(ep.pos)
    n_outs = 1 if ep is None else len(ep.outs) + len(ep.accs)

    def body(*refs):
        a_refs, b_ref = refs[:n_seg], refs[n_seg]
        extra_refs, rest = refs[n_seg + 1:n_seg + 1 + n_extra], refs[n_seg + 1 + n_extra:]
        if ride is not None:
            rest, exchange = ride.split(rest, n_outs)
            ride.run(grid, exchange)
        out_refs, acc_ref = rest[:n_outs], rest[n_outs]
        i, kk = pl.program_id(1), pl.program_id(2)

        @pl.when(kk == 0)
        def _():
            acc_ref[...] = jnp.zeros_like(acc_ref)

        for s in range(n_seg):
            @pl.when((kk >= offs[s]) & (kk < offs[s] + steps[s]))
            def _(s=s):
                acc_ref[...] += dot(a_refs[s][...], b_ref[...])

        if ep is None:
            @pl.when(kk == nk - 1)
            def _():
                out_refs[0][...] = acc_ref[...].astype(out_refs[0].dtype)
        else:
            sum_refs = out_refs[len(ep.outs):]

            @pl.when((kk == 0) & (i == 0))
            def _():
                for ref in sum_refs:
                    ref[...] = jnp.zeros_like(ref)

            @pl.when(kk == nk - 1)
            def _():
                res_outs, res_sums = ep.fn(acc_ref[...], *[r[...] for r in extra_refs])
                for ref, val in zip(out_refs[:len(ep.outs)], res_outs, strict=True):
                    ref[...] = val.astype(ref.dtype)
                for ref, val in zip(sum_refs, res_sums, strict=True):
                    ref[...] += val.reshape(tm // _SUBLANES, _SUBLANES, val.shape[-1]).sum(axis=0)

    seg_spec = lambda s: pl.BlockSpec(
        (tm, tk), functools.partial(lambda j, i, kk, off, ns: (i, jnp.clip(kk - off, 0, ns - 1)), off=offs[s], ns=steps[s]))
    b_spec = (pl.BlockSpec((tn, tk), lambda j, i, kk: (j, kk)) if b_transposed
              else pl.BlockSpec((tk, tn), lambda j, i, kk: (kk, j)))
    in_specs = [seg_spec(s) for s in range(n_seg)] + [b_spec]
    args = list(a_list) + [b]
    row_spec = lambda w: pl.BlockSpec((tm, w), lambda j, i, kk: (i, 0))
    if ep is None:
        out_shape = [jax.ShapeDtypeStruct((m, n), out_dtype)]
        out_specs = [pl.BlockSpec((tm, tn), lambda j, i, kk: (i, j))]
    else:
        tiles_per_example = ep.lp // tm
        row_ins = [r if isinstance(r, tuple) else (r, r.shape[1], 0) for r in ep.rows]
        in_specs += ([pl.BlockSpec((tm, wd), functools.partial(lambda j, i, kk, cb: (i, cb), cb=cb)) for _, wd, cb in row_ins]
                     + [pl.BlockSpec(c.shape, lambda j, i, kk: (0, 0)) for c in ep.consts]
                     + [pl.BlockSpec((tm, p.shape[1]), lambda j, i, kk: (i % tiles_per_example, 0)) for p in ep.pos])
        args += [arr for arr, _, _ in row_ins] + ep.consts + ep.pos
        out_shape = ([jax.ShapeDtypeStruct((m, w), dt) for w, dt in ep.outs]
                     + [jax.ShapeDtypeStruct((_SUBLANES, w), F32) for w in ep.accs])
        out_specs = ([row_spec(w) for w, _ in ep.outs]
                     + [pl.BlockSpec((_SUBLANES, w), lambda j, i, kk: (0, 0)) for w in ep.accs])
    scratch = [pltpu.VMEM((tm, tn), F32)]
    if ride is not None:
        in_specs, args = in_specs + ride.in_specs, args + ride.args
        out_shape, out_specs, scratch = out_shape + ride.out_shape, out_specs + ride.out_specs, scratch + ride.scratch
    res = pl.pallas_call(
        body, name=name, out_shape=out_shape, grid=grid, in_specs=in_specs, out_specs=out_specs,
        scratch_shapes=scratch, compiler_params=_params(),
    )(*args)
    if ep is None:
        return res[0] if ride is None else res
    n_o = len(ep.outs)
    return (res[:n_o], res[n_o:n_outs], *res[n_outs:])


def _matmul_tn(name, x, dy, *, tk, tn, tr, out_dtype=F32):
    r, k = x.shape
    _, n = dy.shape
    assert r % tr == 0 and k % tk == 0 and n % tn == 0, (name, x.shape, dy.shape)
    n_r = r // tr
    direct = out_dtype == F32

    def body(x_ref, dy_ref, o_ref, *scratch):
        acc_ref = o_ref if direct else scratch[0]

        @pl.when(pl.program_id(2) == 0)
        def _():
            acc_ref[...] = jnp.zeros_like(acc_ref)

        acc_ref[...] += _dot_tn(x_ref[...], dy_ref[...])
        if not direct:
            @pl.when(pl.program_id(2) == n_r - 1)
            def _():
                o_ref[...] = acc_ref[...].astype(o_ref.dtype)

    return pl.pallas_call(
        body, name=name,
        out_shape=jax.ShapeDtypeStruct((k, n), out_dtype),
        grid=(k // tk, n // tn, n_r),
        in_specs=[pl.BlockSpec((tr, tk), lambda kb, nb, rr: (rr, kb)),
                  pl.BlockSpec((tr, tn), lambda kb, nb, rr: (rr, nb))],
        out_specs=pl.BlockSpec((tk, tn), lambda kb, nb, rr: (kb, nb)),
        scratch_shapes=[] if direct else [pltpu.VMEM((tk, tn), F32)],
        compiler_params=_params(),
    )(x, dy)


def _ffn_in_swiglu(u, w, *, tm, tn):
    r, k = u.shape
    h = w.shape[1] // 2
    assert r % tm == 0 and h % tn == 0
    nj = h // tn

    def body(u_ref, wg_ref, wu_ref, act_ref, gt_ref, up_ref):
        uu = u_ref[...]
        gt, up = _dot(uu, wg_ref[...]), _dot(uu, wu_ref[...])
        act_ref[...] = (gt * _sigmoid(gt) * up).astype(act_ref.dtype)
        gt_ref[...] = gt.astype(gt_ref.dtype)
        up_ref[...] = up.astype(up_ref.dtype)

    tile = pl.BlockSpec((tm, tn), lambda j, i: (i, j))
    return pl.pallas_call(
        body, name="ffn_in_swiglu",
        out_shape=[jax.ShapeDtypeStruct((r, h), _MXU_DTYPE)] * 3,
        grid=(nj, r // tm),
        in_specs=[pl.BlockSpec((tm, k), lambda j, i: (i, 0)),
                  pl.BlockSpec((k, tn), lambda j, i: (0, j)),
                  pl.BlockSpec((k, tn), lambda j, i: (0, nj + j))],
        out_specs=[tile] * 3,
        compiler_params=_params(),
    )(u, w, w)


def _d_ffn_out_swiglu(dy, w, gt, up, *, tm, tn):
    r, k = dy.shape
    h = w.shape[0]
    assert r % tm == 0 and h % tn == 0

    def body(dy_ref, w_ref, gt_ref, up_ref, dgt_ref, dup_ref):
        da = _dot_nt(dy_ref[...], w_ref[...])
        g, u_ = gt_ref[...].astype(F32), up_ref[...].astype(F32)
        s = _sigmoid(g)
        dgt_ref[...] = (da * u_ * _silu_grad(g, s)).astype(dgt_ref.dtype)
        dup_ref[...] = (da * g * s).astype(dup_ref.dtype)

    tile = pl.BlockSpec((tm, tn), lambda j, i: (i, j))
    return pl.pallas_call(
        body, name="d_ffn_out_swiglu",
        out_shape=[jax.ShapeDtypeStruct((r, h), _MXU_DTYPE)] * 2,
        grid=(h // tn, r // tm),
        in_specs=[pl.BlockSpec((tm, k), lambda j, i: (i, 0)), pl.BlockSpec((tn, k), lambda j, i: (j, 0)), tile, tile],
        out_specs=[tile] * 2,
        compiler_params=_params(),
    )(dy, w, gt, up)


def _proj_q_rope(cqn, w_q, c_tab, s_tab, *, tm, lp):
    r, k = cqn.shape
    tiles_per_example = lp // tm

    def body(x_ref, wn_ref, wp_ref, ws_ref, c_ref, s_ref, o_ref):
        x = x_ref[...]
        roped = _dot(x, wp_ref[...]) * c_ref[...] + _dot(x, ws_ref[...]) * s_ref[...]
        o_ref[...] = jnp.concatenate([_dot(x, wn_ref[...]), roped], axis=1).astype(o_ref.dtype)

    w_blk = lambda part: pl.BlockSpec((k, HEAD_DIM), functools.partial(lambda h, i, part: (0, part * HEADS + h), part=part))
    tab = pl.BlockSpec((tm, HEAD_DIM), lambda h, i: (i % tiles_per_example, 0))
    return pl.pallas_call(
        body, name="proj_q_rope",
        out_shape=jax.ShapeDtypeStruct((r, HEADS * QK_DIM), _MXU_DTYPE),
        grid=(HEADS, r // tm),
        in_specs=[pl.BlockSpec((tm, k), lambda h, i: (i, 0)), w_blk(0), w_blk(1), w_blk(2), tab, tab],
        out_specs=pl.BlockSpec((tm, QK_DIM), lambda h, i: (i, h)),
        compiler_params=_params(),
    )(cqn, w_q, w_q, w_q, c_tab, s_tab)


def _rowwise(name, body, *, rows, tr, lp, ins, outs, accs=()):
    assert rows % tr == 0 and lp % tr == 0 and tr % 16 == 0
    tiles_per_example = lp // tr
    in_specs, arrays = [], []
    for spec in ins:
        if spec[0] == "row":
            _, arr, width, cb = spec
            in_specs.append(pl.BlockSpec((tr, width), functools.partial(lambda i, cb: (i, cb), cb=cb)))
        elif spec[0] == "const":
            arr = spec[1]
            in_specs.append(pl.BlockSpec(arr.shape, lambda i: (0, 0)))
        else:
            arr = spec[1]
            in_specs.append(pl.BlockSpec((tr, arr.shape[1]), lambda i: (i % tiles_per_example, 0)))
        arrays.append(arr)
    n_in, n_out = len(ins), len(outs)

    def kern(*refs):
        res_outs, res_accs = body(*[r[...] for r in refs[:n_in]])
        for ref, val in zip(refs[n_in:n_in + n_out], res_outs, strict=True):
            ref[...] = val.astype(ref.dtype)
        acc_refs = refs[n_in + n_out:]
        if acc_refs:
            @pl.when(pl.program_id(0) == 0)
            def _():
                for ref in acc_refs:
                    ref[...] = jnp.zeros_like(ref)

            for ref, val in zip(acc_refs, res_accs, strict=True):
                ref[...] += val.reshape(tr // _SUBLANES, _SUBLANES, val.shape[-1]).sum(axis=0)

    out_shape = ([jax.ShapeDtypeStruct((rows, w), dt) for w, dt in outs]
                 + [jax.ShapeDtypeStruct((_SUBLANES, w), F32) for w in accs])
    out_specs = ([pl.BlockSpec((tr, w), lambda i: (i, 0)) for w, _ in outs]
                 + [pl.BlockSpec((_SUBLANES, w), lambda i: (0, 0)) for w in accs])
    res = pl.pallas_call(
        kern, name=name, out_shape=out_shape, grid=(rows // tr,),
        in_specs=in_specs, out_specs=out_specs, compiler_params=_params(),
    )(*arrays)
    return res[:n_out], list(res[n_out:])


def _assemble(name, x, head_rows, lp):
    batch, seq, d = x.shape
    tc = 256

    def body(x_ref, m_ref, o_ref):
        o_ref[0:N_META, :] = m_ref[...]
        o_ref[N_META:N_META + seq, :] = x_ref[0]
        if lp > N_META + seq:
            o_ref[N_META + seq:, :] = jnp.zeros((lp - N_META - seq, tc), F32)

    return pl.pallas_call(
        body, name=name,
        out_shape=jax.ShapeDtypeStruct((batch * lp, d), F32),
        grid=(batch, d // tc),
        in_specs=[pl.BlockSpec((1, seq, tc), lambda b, j: (b, 0, j)),
                  pl.BlockSpec((N_META, tc), lambda b, j: (0, j))],
        out_specs=pl.BlockSpec((lp, tc), lambda b, j: (b, j)),
        compiler_params=_params(),
    )(x, head_rows)


def _meta_grad(dh0, batch, lp):
    d = dh0.shape[1]

    def body(g_ref, o_ref):
        @pl.when(pl.program_id(0) == 0)
        def _():
            o_ref[...] = jnp.zeros_like(o_ref)

        o_ref[...] += g_ref[...]

    return pl.pallas_call(
        body, name="meta_grad",
        out_shape=jax.ShapeDtypeStruct((N_META, d), F32),
        grid=(batch,),
        in_specs=[pl.BlockSpec((N_META, d), lambda b: (b * (lp // N_META), 0))],
        out_specs=pl.BlockSpec((N_META, d), lambda b: (0, 0)),
        compiler_params=_params(),
    )(dh0)


def _segment_masks():
    t = lax.broadcasted_iota(jnp.int32, (HG_BLOCK, HG_BLOCK), 0)
    s = lax.broadcasted_iota(jnp.int32, (HG_BLOCK, HG_BLOCK), 1)
    same = lax.shift_right_logical(t, 4) == lax.shift_right_logical(s, 4)
    lower = same & (s <= t)
    upper = same & (s >= t)
    first_half = same & ((s & 15) <= 7)
    return same, lower, upper, first_half


def _hgrn_gates(hq, hf, lb):
    sq = _sigmoid(hq)
    q = hq * sq
    sf = _sigmoid(hf)
    f = lb + (1.0 - lb) * sf
    return q, sq, sf, f


def _hgrn_decays(g, same, lower, first_half):
    b = _exact_dot(lower, g)
    b_last = _exact_dot(same, g)
    b_ref = _exact_dot(first_half, g)
    return b, b_last, b_ref


def _hgrn_fwd(p, lb, gh, *, batch, lp, ride=None):
    rows = batch * lp
    nb = lp // HG_BLOCK
    n_chunks = HG_BLOCK // HG_CHUNK

    def body(hq_ref, hf_ref, hi_ref, hg_ref, lb_ref, gh_ref, *rest):
        if ride is not None:
            rest, exchange = ride.split(rest, 3)
            ride.run((batch, nb), exchange)
        o_ref, z_ref, st_ref, s_scr, qt_scr, kh_scr, v_scr, el_scr, o_scr = rest

        @pl.when(pl.program_id(1) == 0)
        def _():
            s_scr[...] = jnp.zeros_like(s_scr)

        same, lower, _, first_half = _segment_masks()
        v = hi_ref[...]
        q, _, _, f = _hgrn_gates(hq_ref[...], hf_ref[...], lb_ref[...])
        k = 1.0 - f
        b, b_last, b_ref = _hgrn_decays(jnp.log(f), same, lower, first_half)
        qt_scr[...] = _mx(q * jnp.exp(b))
        kh_scr[...] = _mx(k * jnp.exp(b_last - b))
        v_scr[...] = _mx(v)
        el_scr[...] = jnp.exp(b_last)
        qc = _mx(q * jnp.exp(b - b_ref))
        kc = _mx(k * jnp.exp(b_ref - b))

        def intra(qc_h, kc_h, v_h):
            a = jnp.where(lower, _dot_nt(qc_h, kc_h), 0.0)
            return _dot(_mx(a), v_h)

        o_scr[...] = _heads(intra, qc, kc, _mx(v))

        for c in range(n_chunks):
            rs = slice(c * HG_CHUNK, (c + 1) * HG_CHUNK)
            for h in range(HEADS):
                cs = slice(h * HEAD_DIM, (h + 1) * HEAD_DIM)
                st = s_scr[h]
                st_m = _mx(st)
                st_ref[c, h] = st_m
                o_scr[rs, cs] += _dot_nt(qt_scr[rs, cs], st_m)
                s_scr[h] = st * el_scr[c * HG_CHUNK:c * HG_CHUNK + 1, cs] + _dot_tn(v_scr[rs, cs], kh_scr[rs, cs])

        o = o_scr[...]
        o_ref[...] = o
        hg = hg_ref[...]
        n = _heads(lambda o_h: o_h * _rms_scale(o_h), o) * gh_ref[...]
        z_ref[...] = (n * hg * _sigmoid(hg)).astype(z_ref.dtype)

    blk = lambda cb: pl.BlockSpec((HG_BLOCK, D_MODEL), functools.partial(lambda b, j, cb: (b * nb + j, cb), cb=cb))
    row_out = pl.BlockSpec((HG_BLOCK, D_MODEL), lambda b, j: (b * nb + j, 0))
    const = pl.BlockSpec((1, D_MODEL), lambda b, j: (0, 0))
    extra = ride if ride is not None else _NoRide
    return pl.pallas_call(
        body, name="hgrn_fwd",
        out_shape=[jax.ShapeDtypeStruct((rows, D_MODEL), F32),
                   jax.ShapeDtypeStruct((rows, D_MODEL), _MXU_DTYPE),
                   jax.ShapeDtypeStruct((rows // HG_CHUNK, HEADS, HEAD_DIM, HEAD_DIM), _MXU_DTYPE)] + extra.out_shape,
        grid=(batch, nb),
        in_specs=[blk(CB_HQ), blk(CB_HF), blk(CB_HI), blk(CB_HG), const, const] + extra.in_specs,
        out_specs=[row_out, row_out,
                   pl.BlockSpec((n_chunks, HEADS, HEAD_DIM, HEAD_DIM), lambda b, j: (b * nb + j, 0, 0, 0))]
        + extra.out_specs,
        scratch_shapes=[pltpu.VMEM((HEADS, HEAD_DIM, HEAD_DIM), F32),
                        pltpu.VMEM((HG_BLOCK, D_MODEL), _MXU_DTYPE),
                        pltpu.VMEM((HG_BLOCK, D_MODEL), _MXU_DTYPE),
                        pltpu.VMEM((HG_BLOCK, D_MODEL), _MXU_DTYPE),
                        pltpu.VMEM((HG_BLOCK, D_MODEL), F32),
                        pltpu.VMEM((HG_BLOCK, D_MODEL), F32)] + extra.scratch,
        compiler_params=_params(),
    )(p, p, p, p, lb, gh, *extra.args)


def _hgrn_bwd(p, o, dz, states, lb, gh, *, batch, lp, ride=None):
    rows = batch * lp
    nb = lp // HG_BLOCK
    n_chunks = HG_BLOCK // HG_CHUNK

    def body(hq_ref, hf_ref, hi_ref, hg_ref, o_ref, dz_ref, st_ref, lb_ref, gh_ref, *rest):
        if ride is not None:
            rest, exchange = ride.split(rest, 3)
            ride.run((batch, nb), exchange)
        (dp_ref, dlb_ref, dgh_ref,
         ds_scr, qt_scr, kh_scr, v_scr, do_scr, el_scr, dqt_scr, dkh_scr, dv_scr, dbl_scr) = rest
        first = (pl.program_id(0) == 0) & (pl.program_id(1) == 0)

        @pl.when(first)
        def _():
            dlb_ref[...] = jnp.zeros_like(dlb_ref)
            dgh_ref[...] = jnp.zeros_like(dgh_ref)

        @pl.when(pl.program_id(1) == 0)
        def _():
            ds_scr[...] = jnp.zeros_like(ds_scr)

        same, lower, upper, first_half = _segment_masks()
        lbv = lb_ref[...]
        hq, hf, v, hg = hq_ref[...], hf_ref[...], hi_ref[...], hg_ref[...]
        q, sq, sf, f = _hgrn_gates(hq, hf, lbv)
        k = 1.0 - f
        b, b_last, b_ref = _hgrn_decays(jnp.log(f), same, lower, first_half)
        e_b = jnp.exp(b)
        e_kh = jnp.exp(b_last - b)
        e_qc = jnp.exp(b - b_ref)
        e_kc = jnp.exp(b_ref - b)
        qt, kh, qc, kc = q * e_b, k * e_kh, q * e_qc, k * e_kc

        o = o_ref[...]
        dz = dz_ref[...].astype(F32)
        ghv = gh_ref[...]
        sg = _sigmoid(hg)
        r = _heads(lambda o_h: jnp.broadcast_to(_rms_scale(o_h), o_h.shape), o)
        oh = o * r
        dn = dz * hg * sg
        dhg = dz * oh * ghv * _silu_grad(hg, sg)
        w = dn * ghv
        do = r * (w - oh * _heads(lambda t: jnp.broadcast_to(jnp.mean(t, axis=-1, keepdims=True), t.shape), oh * w))
        dgh_ref[...] += (dn * oh).reshape(HG_BLOCK // _SUBLANES, _SUBLANES, D_MODEL).sum(axis=0)

        qt_scr[...] = _mx(qt)
        kh_scr[...] = _mx(kh)
        v_scr[...] = _mx(v)
        do_scr[...] = _mx(do)
        el_scr[...] = jnp.exp(b_last)

        def intra(qc_h, kc_h, v_h, do_h):
            a = _mx(jnp.where(lower, _dot_nt(qc_h, kc_h), 0.0))
            da = _mx(jnp.where(lower, _dot_nt(do_h, v_h), 0.0))
            return _dot(da, kc_h), _dot_tn(da, qc_h), _dot_tn(a, do_h)

        dqc, dkc, dv_intra = _heads(intra, _mx(qc), _mx(kc), _mx(v), _mx(do))
        dv_scr[...] = dv_intra

        for c in reversed(range(n_chunks)):
            rs = slice(c * HG_CHUNK, (c + 1) * HG_CHUNK)
            for h in range(HEADS):
                cs = slice(h * HEAD_DIM, (h + 1) * HEAD_DIM)
                st = st_ref[c, h]
                ds_t = ds_scr[h]
                ds_m = _mx(ds_t)
                el = el_scr[c * HG_CHUNK:c * HG_CHUNK + 1, cs]
                dkh_scr[rs, cs] = _dot(v_scr[rs, cs], ds_m)
                dv_scr[rs, cs] += _dot_nt(kh_scr[rs, cs], ds_m)
                dbl = jnp.sum(ds_t * st.astype(F32), axis=0, keepdims=True) * el
                dbl_scr[rs, cs] = jnp.broadcast_to(dbl, (HG_CHUNK, HEAD_DIM))
                dqt_scr[rs, cs] = _dot(do_scr[rs, cs], st)
                ds_scr[h] = ds_t * el + _dot_tn(do_scr[rs, cs], qt_scr[rs, cs])

        dqt, dkh = dqt_scr[...], dkh_scr[...]
        dq = dqt * e_b + dqc * e_qc
        dk = dkh * e_kh + dkc * e_kc
        t_kh = dkh * kh
        db_rows = dqt * qt + dqc * qc - dkc * kc - t_kh
        dg = _exact_dot(upper, db_rows) + _exact_dot(same, t_kh) + dbl_scr[...]
        df = dg / f - dk
        dhf = df * (1.0 - lbv) * sf * (1.0 - sf)
        dlb_ref[...] += (df * (1.0 - sf)).reshape(HG_BLOCK // _SUBLANES, _SUBLANES, D_MODEL).sum(axis=0)
        dhq = dq * _silu_grad(hq, sq)
        dp_ref[...] = jnp.concatenate([dhq, dhf, dv_scr[...], dhg], axis=1).astype(dp_ref.dtype)

    rev = lambda b, j: b * nb + (nb - 1 - j)
    blk = lambda cb: pl.BlockSpec((HG_BLOCK, D_MODEL), functools.partial(lambda b, j, cb: (rev(b, j), cb), cb=cb))
    row = pl.BlockSpec((HG_BLOCK, D_MODEL), lambda b, j: (rev(b, j), 0))
    const = pl.BlockSpec((1, D_MODEL), lambda b, j: (0, 0))
    acc = pl.BlockSpec((_SUBLANES, D_MODEL), lambda b, j: (0, 0))
    big = lambda dt: pltpu.VMEM((HG_BLOCK, D_MODEL), dt)
    extra = ride if ride is not None else _NoRide
    dp, dlb, dgh, *exchanged = pl.pallas_call(
        body, name="hgrn_bwd",
        out_shape=[jax.ShapeDtypeStruct((rows, 4 * D_MODEL), _MXU_DTYPE),
                   jax.ShapeDtypeStruct((_SUBLANES, D_MODEL), F32),
                   jax.ShapeDtypeStruct((_SUBLANES, D_MODEL), F32)] + extra.out_shape,
        grid=(batch, nb),
        in_specs=[blk(CB_HQ), blk(CB_HF), blk(CB_HI), blk(CB_HG), row, row,
                  pl.BlockSpec((n_chunks, HEADS, HEAD_DIM, HEAD_DIM), lambda b, j: (rev(b, j), 0, 0, 0)),
                  const, const] + extra.in_specs,
        out_specs=[pl.BlockSpec((HG_BLOCK, 4 * D_MODEL), lambda b, j: (rev(b, j), 0)), acc, acc] + extra.out_specs,
        scratch_shapes=[pltpu.VMEM((HEADS, HEAD_DIM, HEAD_DIM), F32),
                        big(_MXU_DTYPE), big(_MXU_DTYPE), big(_MXU_DTYPE), big(_MXU_DTYPE),
                        big(F32), big(F32), big(F32), big(F32), big(F32)] + extra.scratch,
        compiler_params=_params(),
    )(p, p, p, p, o, dz, states, lb, gh, *extra.args)
    return (dp, dlb, dgh, *exchanged)


QK_DIM = 2 * HEAD_DIM
ATTN_TQ = 256
ATTN_KEY_CHUNK = 512


def _query_tiles(lp):
    return [(r0, min(ATTN_TQ, lp - r0)) for r0 in range(0, lp, ATTN_TQ)]


def _attn_fwd(q_cat, kv, kp, *, batch, lp):
    rows = batch * lp

    def body(q_ref, kn_ref, kp_ref, v_ref, o_ref, lse_ref):
        k_cat = jnp.concatenate([kn_ref[...], kp_ref[...]], axis=1)
        for r0, tq in _query_tiles(lp):
            q_t = q_ref[r0:r0 + tq, :]
            i = lax.broadcasted_iota(jnp.int32, (tq, tq), 0)
            j = lax.broadcasted_iota(jnp.int32, (tq, tq), 1)
            s_diag = jnp.where(j <= i, _dot_nt(q_t, k_cat[r0:r0 + tq]) * ATTN_SCALE, NEG_BIG)
            m = jnp.max(s_diag, axis=1, keepdims=True)
            if r0:
                s_past = _dot_nt(q_t, k_cat[0:r0]) * ATTN_SCALE
                m = jnp.maximum(m, jnp.max(s_past, axis=1, keepdims=True))
            p_diag = jnp.exp(s_diag - m)
            l = jnp.sum(p_diag, axis=1, keepdims=True)
            acc = _dot(_mx(p_diag), v_ref[r0:r0 + tq, :])
            if r0:
                p_past = jnp.exp(s_past - m)
                l = l + jnp.sum(p_past, axis=1, keepdims=True)
                acc = acc + _dot(_mx(p_past), v_ref[0:r0, :])
            o_ref[r0:r0 + tq, :] = (acc / l).astype(o_ref.dtype)
            lse_ref[r0:r0 + tq, :] = jnp.broadcast_to(m + jnp.log(l), (tq, HEAD_DIM))

    head_blk = pl.BlockSpec((lp, HEAD_DIM), lambda b, h: (b, h))
    return pl.pallas_call(
        body, name="attn_fwd",
        out_shape=[jax.ShapeDtypeStruct((rows, D_MODEL), _MXU_DTYPE),
                   jax.ShapeDtypeStruct((rows, D_MODEL), F32)],
        grid=(batch, HEADS),
        in_specs=[pl.BlockSpec((lp, QK_DIM), lambda b, h: (b, h)), head_blk,
                  pl.BlockSpec((lp, HEAD_DIM), lambda b, h: (b, 0)),
                  pl.BlockSpec((lp, HEAD_DIM), lambda b, h: (b, HEADS + h))],
        out_specs=[head_blk, head_blk],
        compiler_params=_params(),
    )(q_cat, kv, kp, kv)


def _attn_bwd(q_cat, kv, kp, do, o, lse, c_tab, s_tab, *, batch, lp, ride=None):
    rows = batch * lp

    def body(q_ref, kn_ref, kp_ref, v_ref, do_ref, o_ref, lse_ref, c_ref, s_ref, *rest):
        if ride is not None:
            rest, exchange = ride.split(rest, 6)
            ride.run((batch, HEADS), exchange)
        dqn_ref, dqc_ref, dqs_ref, dkn_ref, dkp_ref, dv_ref, dk_acc, dv_acc = rest
        dk_acc[...] = jnp.zeros_like(dk_acc)
        dv_acc[...] = jnp.zeros_like(dv_acc)
        k_cat = jnp.concatenate([kn_ref[...], kp_ref[...]], axis=1)
        k_t = k_cat.T
        lane = lax.broadcasted_iota(jnp.int32, (_SUBLANES, HEAD_DIM), 1)
        lse_row = _exact_dot_nt(lane == 0, lse_ref[...])
        delta = _exact_dot_nt(lane >= 0, do_ref[...].astype(F32) * o_ref[...].astype(F32))
        for r0, tq in _query_tiles(lp):
            cols = slice(r0, r0 + tq)
            q_t_, do_t_ = q_ref[cols, :], do_ref[cols, :]
            lse_t, delta_t = lse_row[0:1, cols], delta[0:1, cols]
            chunks = [(c0, min(ATTN_KEY_CHUNK, r0 - c0), False) for c0 in range(0, r0, ATTN_KEY_CHUNK)] + [(r0, tq, True)]
            dq_t = jnp.zeros((QK_DIM, tq), F32)
            for c0, n, diagonal in chunks:
                keys = slice(c0, c0 + n)
                s = _dot_nt(k_cat[keys], q_t_) * ATTN_SCALE
                if diagonal:
                    jk = lax.broadcasted_iota(jnp.int32, (n, tq), 0)
                    iq = lax.broadcasted_iota(jnp.int32, (n, tq), 1)
                    s = jnp.where(jk <= iq, s, NEG_BIG)
                pexp = jnp.exp(s - lse_t)
                dp = _dot_nt(v_ref[keys, :], do_t_)
                ds = _mx(pexp * (dp - delta_t) * ATTN_SCALE)
                dk_acc[keys, :] += _dot(ds, q_t_)
                dv_acc[keys, :] += _dot(_mx(pexp), do_t_)
                dq_t = dq_t + _dot(k_t[:, keys], ds)
            dq = dq_t.T
            d_rope = dq[:, HEAD_DIM:]
            dqn_ref[cols, :] = dq[:, :HEAD_DIM].astype(dqn_ref.dtype)
            dqc_ref[cols, :] = (d_rope * c_ref[cols, :]).astype(dqc_ref.dtype)
            dqs_ref[cols, :] = (d_rope * s_ref[cols, :]).astype(dqs_ref.dtype)

        dkn_ref[...] = dk_acc[:, 0:HEAD_DIM].astype(dkn_ref.dtype)
        dv_ref[...] = dv_acc[...].astype(dv_ref.dtype)

        @pl.when(pl.program_id(1) == 0)
        def _():
            dkp_ref[...] = jnp.zeros_like(dkp_ref)

        dkp_ref[...] += dk_acc[:, HEAD_DIM:]

    head_blk = pl.BlockSpec((lp, HEAD_DIM), lambda b, h: (b, h))
    cat_blk = pl.BlockSpec((lp, QK_DIM), lambda b, h: (b, h))
    shared_blk = pl.BlockSpec((lp, HEAD_DIM), lambda b, h: (b, 0))
    table_blk = pl.BlockSpec((lp, HEAD_DIM), lambda b, h: (0, 0))
    extra = ride if ride is not None else _NoRide
    return pl.pallas_call(
        body, name="attn_bwd",
        out_shape=[jax.ShapeDtypeStruct((rows, D_MODEL), _MXU_DTYPE)] * 3 + [
                   jax.ShapeDtypeStruct((rows, D_MODEL), _MXU_DTYPE),
                   jax.ShapeDtypeStruct((rows, HEAD_DIM), F32),
                   jax.ShapeDtypeStruct((rows, D_MODEL), _MXU_DTYPE)] + extra.out_shape,
        grid=(batch, HEADS),
        in_specs=[cat_blk, head_blk, shared_blk, pl.BlockSpec((lp, HEAD_DIM), lambda b, h: (b, HEADS + h)),
                  head_blk, head_blk, head_blk, table_blk, table_blk] + extra.in_specs,
        out_specs=[head_blk, head_blk, head_blk, head_blk, shared_blk, head_blk] + extra.out_specs,
        scratch_shapes=[pltpu.VMEM((lp, QK_DIM), F32), pltpu.VMEM((lp, HEAD_DIM), F32)] + extra.scratch,
        compiler_params=_params(),
    )(q_cat, kv, kp, kv, do, o, lse, c_tab, s_tab, *extra.args)


def _all_gather(name, blocks):
    n = len(blocks)

    def body(*refs):
        x_refs, out_refs, (send_sems, recv_sems, local_sems) = refs[:n], refs[n:2 * n], refs[2 * n:]
        x, y, c = lax.axis_index("x"), lax.axis_index("y"), lax.axis_index("c")
        me, sibling = (x, y, c), (x, y, 1 - c)
        chips = [(1 - x, y), (x, 1 - y), (1 - x, 1 - y)]

        def slot(i, px, py, pc):
            return out_refs[i].at[4 * px + 2 * py + pc]

        def copy(i, k, blk, to, src=None):
            return pltpu.make_async_remote_copy(
                src_ref=slot(i, *blk) if src is None else src, dst_ref=slot(i, *blk),
                send_sem=send_sems.at[i, k], recv_sem=recv_sems.at[i, k],
                device_id=to, device_id_type=pl.DeviceIdType.MESH)

        mine = [pltpu.make_async_copy(x_refs[i], slot(i, *me), local_sems.at[i]) for i in range(n)]
        first = [copy(i, 0, me, sibling, src=x_refs[i]) for i in range(n)]
        first += [copy(i, 1 + j, me, (*chip, c), src=x_refs[i]) for i in range(n) for j, chip in enumerate(chips)]
        for cp in mine + first:
            cp.start()
        passed = []
        for i in range(n):
            for j, chip in enumerate(chips):
                copy(i, 1 + j, (*chip, c), me).wait_recv()
                passed.append(copy(i, 4 + j, (*chip, c), sibling))
                passed[-1].start()
        for i in range(n):
            copy(i, 0, sibling, me).wait_recv()
            for j, chip in enumerate(chips):
                copy(i, 4 + j, (*chip, 1 - c), me).wait_recv()
        for cp in first + passed:
            cp.wait_send()
        for cp in mine:
            cp.wait()

    return pl.pallas_call(
        body, name=name,
        out_shape=[jax.ShapeDtypeStruct((N_DEV, *b.shape), b.dtype) for b in blocks],
        in_specs=[pl.BlockSpec(memory_space=pl.ANY)] * n,
        out_specs=[pl.BlockSpec(memory_space=pl.ANY)] * n,
        scratch_shapes=[pltpu.SemaphoreType.DMA((n, 7)), pltpu.SemaphoreType.DMA((n, 7)),
                        pltpu.SemaphoreType.DMA((n,))],
    )(*blocks)


def _adamw_math(w, g, m, v):
    nm = ADAM_B1 * m + (1.0 - ADAM_B1) * g
    nv = ADAM_B2 * v + (1.0 - ADAM_B2) * (g * g)
    m_hat = nm / (1.0 - ADAM_B1 ** ADAM_STEP)
    v_hat = nv / (1.0 - ADAM_B2 ** ADAM_STEP)
    return -ADAM_LR * (m_hat / (jnp.sqrt(v_hat) + ADAM_EPS) + ADAM_WD * w), nm, nv


def _sum_adamw(name, parts, w, m, v):
    rows, cols = w.shape
    tr = rows // 4 if rows % 64 == 0 and rows * cols > (1 << 16) else rows

    def body(p_ref, w_ref, m_ref, v_ref, g_ref, d_ref, nm_ref, nv_ref):
        g = p_ref[0].astype(F32)
        for dev in range(1, N_DEV):
            g = g + p_ref[dev].astype(F32)
        g_ref[...] = g
        d_ref[...], nm_ref[...], nv_ref[...] = _adamw_math(w_ref[...], g, m_ref[...], v_ref[...])

    spec = pl.BlockSpec((tr, cols), lambda i: (i, 0))
    return pl.pallas_call(
        body, name=name,
        out_shape=[jax.ShapeDtypeStruct((rows, cols), F32)] * 4,
        grid=(rows // tr,),
        in_specs=[pl.BlockSpec((N_DEV, tr, cols), lambda i: (0, i, 0))] + [spec] * 3, out_specs=[spec] * 4,
        compiler_params=_params(),
    )(parts, w, m, v)


def _finish_vectors(gathered, lb, params, loss_parts):
    names = list(params)
    n = len(names)

    def body(*refs):
        g_refs, lb_ref, loss_ref = refs[:n], refs[n], refs[n + 1]
        wmv_refs = refs[n + 2:4 * n + 2]
        out_refs, loss_out = refs[4 * n + 2:-1], refs[-1]
        sq = loss_ref[0]
        for dev in range(1, N_DEV):
            sq = sq + loss_ref[dev]
        sq = jnp.sum(jnp.sum(sq, axis=0, keepdims=True), axis=1, keepdims=True)
        loss_out[...] = sq * (0.5 / D_MODEL)
        me = 4 * lax.axis_index("x") + 2 * lax.axis_index("y") + lax.axis_index("c")
        for i, name in enumerate(names):
            g_ref = g_refs[i]
            w_ref, m_ref, v_ref = wmv_refs[3 * i:3 * i + 3]
            if name == "meta_tokens":
                width = w_ref.shape[1]
                mine = pl.ds(pl.multiple_of(me * width, width), width)
                g = g_ref[0, :, mine]
                for dev in range(1, N_DEV):
                    g = g + g_ref[dev, :, mine]
            else:
                g = g_ref[0]
                for dev in range(1, N_DEV):
                    g = g + g_ref[dev]
                g = jnp.sum(g, axis=0, keepdims=True)
                if name == "hg_norm_g":
                    g = functools.reduce(jnp.add, [g[:, h * HEAD_DIM:(h + 1) * HEAD_DIM] for h in range(HEADS)])
                if name == "lb_logits":
                    lbv = lb_ref[...]
                    g = g * lbv * (1.0 - lbv)
                    g = jnp.concatenate([g, -g], axis=0)
            outs = (g, *_adamw_math(w_ref[...], g, m_ref[...], v_ref[...]))
            for ref, val in zip(out_refs[4 * i:4 * i + 4], outs, strict=True):
                ref[...] = val

    args = [gathered[k] for k in names] + [lb, loss_parts] + [t for k in names for t in params[k]]
    res = pl.pallas_call(
        body, name="finish_vectors",
        out_shape=[jax.ShapeDtypeStruct(params[k][0].shape, F32) for k in names for _ in range(4)]
        + [jax.ShapeDtypeStruct((1, 1), F32)],
        compiler_params=_params(),
    )(*args)
    return {k: res[4 * i:4 * i + 4] for i, k in enumerate(names)}, res[-1].reshape(())


def _swap_halves(t):
    half = t.shape[-1] // 2
    return jnp.concatenate([t[..., half:], t[..., :half]], axis=-1)


def _pad_last(t, width):
    return jnp.concatenate([t, jnp.zeros(t.shape[:-1] + (width - t.shape[-1],), t.dtype)], axis=-1)


def _rope_tables(lp):
    pos = jnp.arange(lp, dtype=F32)
    inv_freq = 1.0 / (ROPE_THETA ** (jnp.arange(0, ROPE_DIM, 2, dtype=F32) / ROPE_DIM))
    ang = pos[:, None] * inv_freq[None, :]
    cos, sin = jnp.cos(ang), jnp.sin(ang)
    c128 = _pad_last(jnp.concatenate([cos, cos], axis=1), HEAD_DIM)
    s128 = _pad_last(jnp.concatenate([-sin, sin], axis=1), HEAD_DIM)
    return c128, s128


def _forward_backward(x, target, meta, w, small, *, lp, comm=None):
    batch, seq, d = x.shape
    rows = batch * lp
    tr = 272 if lp % 272 == 0 else 128
    tm = lp // 2
    bf = _MXU_DTYPE
    rw = functools.partial(_rowwise, rows=rows, tr=tr, lp=lp)

    c128, s128 = _rope_tables(lp)
    t_idx = jnp.arange(lp)
    real = jnp.broadcast_to(((t_idx >= N_META) & (t_idx < N_META + seq)).astype(F32)[:, None], (lp, _LANES))

    lb_logits = small["lb_logits"]
    lb = jax.nn.softmax(lb_logits, axis=0)[0:1]
    gh = jnp.tile(small["hg_norm_g"], (1, HEADS))

    h0 = _assemble("assemble_x", x, meta, lp)
    tgt = _assemble("assemble_target", target, jnp.zeros_like(meta), lp)

    (u1,), _ = rw("norm_mix_pre", lambda h, g: ([h * _rms_scale(h) * g], []),
                  ins=[("row", h0, d, 0), ("const", small["mix_pre_g"])], outs=[(d, bf)])
    p = _matmul("proj_in", u1, w["w_in"], out_dtype=F32, tm=tm, tn=1024, tk=1024)

    if comm is None:
        o_hg, z_a, states = _hgrn_fwd(p, lb, gh, batch=batch, lp=lp)
    else:
        o_hg, z_a, states, *gathered = _hgrn_fwd(p, lb, gh, batch=batch, lp=lp, ride=_Ride(comm.rest_payloads, True))
        w = {**w, **comm.rest_weights(gathered)}
    received = []
    scatter = lambda names: _Ride(comm.grad_parts(names, grads), False) if comm is not None else None
    y_a = _matmul("proj_hg_o", z_a, w["w_hg_o"], out_dtype=F32, tm=tm, tn=1024, tk=1024)

    def mla_pre(pc, gq, gkv, ct, st):
        cq, ckv = pc[:, 0:Q_LORA], pc[:, Q_LORA:Q_LORA + KV_LORA]
        kpe, kpe_sw = pc[:, 512:640], pc[:, 640:768]
        return [cq * _rms_scale(cq) * gq, ckv * _rms_scale(ckv) * gkv, kpe * ct + kpe_sw * st], []

    (cqn, ckvn, kp), _ = rw("mla_pre", mla_pre,
                            ins=[("row", p, 1024, CB_C), ("const", small["q_a_norm_g"]),
                                 ("const", small["kv_a_norm_g"]), ("pos", c128), ("pos", s128)],
                            outs=[(Q_LORA, bf), (KV_LORA, bf), (HEAD_DIM, bf)])
    q_cat = _proj_q_rope(cqn, w["w_q"], c128, s128, tm=tm, lp=lp)
    kv = _matmul("proj_kv_b", ckvn, w["w_kv"], out_dtype=bf, tm=tm, tn=1024, tk=KV_LORA)
    o_at, lse = _attn_fwd(q_cat, kv, kp, batch=batch, lp=lp)
    y_b = _matmul("proj_mla_o", o_at, w["w_mla_o"], out_dtype=F32, tm=tm, tn=1024, tk=1024)

    def merge(pa, pb, ya, yb, bg):
        ga, gb = _sigmoid(pa + bg[:, :d]), _sigmoid(pb + bg[:, d:])
        return [ga * ya + gb * yb], []

    (mix,), _ = rw("merge", merge,
                   ins=[("row", p, 1024, CB_GA), ("row", p, 1024, CB_GB), ("row", y_a, d, 0), ("row", y_b, d, 0),
                        ("const", small["b_gate"])], outs=[(d, bf)])
    te = lp // 4

    def post_mix(mx_, h, g2, g3):
        h1_ = h + mx_ * _rms_scale(mx_) * g2
        return [mx_, h1_, h1_ * _rms_scale(h1_) * g3], []

    (mixed, h1, u2), _ = _matmul_segments(
        "proj_out", [mix], w["w_out"], tm=te, tn=d, tk=1024,
        epilogue=_Epilogue(post_mix, rows=[h0], consts=[small["mix_post_g"], small["ffn_pre_g"]],
                           outs=[(d, F32), (d, F32), (d, bf)], lp=lp))
    act, gt, up = _ffn_in_swiglu(u2, w["w_ffn_in"], tm=tm, tn=1408)

    def post_ffn(fo_, h1_, t_, g4, mask):
        r = _rms_scale(fo_)
        h2 = h1_ + fo_ * r * g4
        err = (h2 - t_) * mask[:, 0:1]
        dh2 = err * (1.0 / d)
        dfo, dg4 = _rms_bwd(fo_, g4, dh2)
        return [dh2, dfo], [err * err, dg4]

    (dh2, dfo), (loss_vec, dg_ffn_post) = _matmul_segments(
        "ffn_out", [act], w["w_ffn_out"], tm=te, tn=d, tk=1408,
        epilogue=_Epilogue(post_ffn, rows=[h1, tgt], consts=[small["ffn_post_g"]], pos=[real],
                           outs=[(d, F32), (d, bf)], accs=[d, d], lp=lp))
    loss = (0.5 / d) * jnp.sum(loss_vec)

    grads = {}
    dw_dt = F32 if comm is None else _WIRE_DTYPE
    dgt, dup = _d_ffn_out_swiglu(dfo, w["w_ffn_out"], gt, up, tm=tm, tn=1408)
    grads["w_ffn_out"] = _matmul_tn("dw_ffn_out", act, dfo, tk=1408, tn=1024, tr=tm, out_dtype=dw_dt)
    grads["w_ffn_in"] = jnp.concatenate([_matmul_tn("dw_ffn_in_gate", u2, dgt, tk=1024, tn=1408, tr=tm),
                                         _matmul_tn("dw_ffn_in_up", u2, dup, tk=1024, tn=1408, tr=tm)], axis=1)

    def post_mix_bwd(du2_, h1_, dh2_, mx_, g3, g2):
        dx, dg3 = _rms_bwd(h1_, g3, du2_)
        dh1_ = dh2_ + dx
        dmx, dg2 = _rms_bwd(mx_, g2, dh1_)
        return [dh1_, dmx], [dg3, dg2]

    (dh1, dmixed), (dg_ffn_pre, dg_mix_post) = _matmul_segments(
        "d_ffn_in", [dgt, dup], w["w_ffn_in"], tm=te, tn=d, tk=1408, b_transposed=True,
        epilogue=_Epilogue(post_mix_bwd, rows=[h1, dh2, mixed], consts=[small["ffn_pre_g"], small["mix_post_g"]],
                           outs=[(d, F32), (d, bf)], accs=[d, d], lp=lp))
    grads["w_out"] = _matmul_tn("dw_out", mix, dmixed, tk=1024, tn=1024, tr=tm, out_dtype=dw_dt)

    def merge_bwd(dm, pa, pb, ya, yb, bg):
        ga, gb = _sigmoid(pa + bg[:, :d]), _sigmoid(pb + bg[:, d:])
        dpg = jnp.concatenate([dm * ya * ga * (1.0 - ga), dm * yb * gb * (1.0 - gb)], axis=1)
        return [dpg, dm * ga, dm * gb], [dpg]

    (dpg, dya, dyb), (db_gate,) = _matmul_segments(
        "d_proj_out", [dmixed], w["w_out"], tm=te, tn=d, tk=1024, b_transposed=True,
        epilogue=_Epilogue(merge_bwd, rows=[(p, 1024, CB_GA), (p, 1024, CB_GB), y_a, y_b], consts=[small["b_gate"]],
                           outs=[(2 * d, bf), (d, bf), (d, bf)], accs=[2 * d], lp=lp))
    dz_a = _matmul("d_proj_hg_o", dya, w["w_hg_o"], out_dtype=F32, tm=tm, tn=1024, tk=1024, b_transposed=True)
    grads["w_hg_o"] = _matmul_tn("dw_hg_o", z_a, dya, tk=1024, tn=1024, tr=tm, out_dtype=dw_dt)
    do_at = _matmul("d_proj_mla_o", dyb, w["w_mla_o"], out_dtype=bf, tm=tm, tn=1024, tk=1024, b_transposed=True)
    grads["w_mla_o"] = _matmul_tn("dw_mla_o", o_at, dyb, tk=1024, tn=1024, tr=tm, out_dtype=dw_dt)

    dph, dlb, dgh, *got = _hgrn_bwd(p, o_hg, dz_a, states, lb, gh, batch=batch, lp=lp,
                                    ride=scatter(_GRAD_GROUPS[0]))
    received.append(got)

    res = _attn_bwd(q_cat, kv, kp, do_at, o_at, lse, c128, s128, batch=batch, lp=lp, ride=scatter(_GRAD_GROUPS[1]))
    dq_parts, (dkn, dkp, dvv) = list(res[:3]), res[3:6]
    received.append(list(res[6:]))
    dcqn = _matmul_segments("d_proj_q_b", dq_parts, w["w_q"], tm=tm, tn=Q_LORA, tk=1024, b_transposed=True)
    grads["w_q"] = jnp.concatenate([_matmul_tn(f"dw_q_b_{i}", cqn, part, tk=Q_LORA, tn=1024, tr=tm)
                                    for i, part in enumerate(dq_parts)], axis=1)
    dckvn = _matmul_segments("d_proj_kv_b", [dkn, dvv], w["w_kv"], tm=tm, tn=KV_LORA, tk=1024, b_transposed=True)
    grads["w_k"] = _matmul_tn("dw_k_b", ckvn, dkn, tk=KV_LORA, tn=1024, tr=tm)
    grads["w_v"] = _matmul_tn("dw_v_b", ckvn, dvv, tk=KV_LORA, tn=1024, tr=tm)

    def mla_pre_bwd(pc, dq_, dkv_, dkp_, gq, gkv, ct, st):
        cq, ckv = pc[:, 0:Q_LORA], pc[:, Q_LORA:Q_LORA + KV_LORA]
        dcq, dgq = _rms_bwd(cq, gq, dq_)
        dckv, dgkv = _rms_bwd(ckv, gkv, dkv_)
        dpc = jnp.concatenate([dcq, dckv, dkp_ * ct, dkp_ * st, jnp.zeros((pc.shape[0], 256), F32)], axis=1)
        return [dpc], [dgq, dgkv]

    (dpc,), (dg_q, dg_kv) = rw(
        "mla_pre_bwd", mla_pre_bwd,
        ins=[("row", p, 1024, CB_C), ("row", dcqn, Q_LORA, 0), ("row", dckvn, KV_LORA, 0), ("row", dkp, HEAD_DIM, 0),
             ("const", small["q_a_norm_g"]), ("const", small["kv_a_norm_g"]), ("pos", c128), ("pos", s128)],
        outs=[(1024, bf)], accs=[Q_LORA, KV_LORA])

    grads["w_in"] = jnp.concatenate([
        _matmul_tn("dw_in_h", u1, dph, tk=1024, tn=1024, tr=tm),
        _matmul_tn("dw_in_c", u1, dpc, tk=1024, tn=1024, tr=tm),
        _matmul_tn("dw_in_g", u1, dpg, tk=1024, tn=1024, tr=tm)], axis=1)
    def pre_bwd(du, h, dh, g1):
        dx, dg1 = _rms_bwd(h, g1, du)
        return [dh + dx], [dg1]

    (dh0,), (dg_mix_pre,), *got = _matmul_segments(
        "d_proj_in", [dph, dpc, dpg], w["w_in"], tm=te, tn=d, tk=1024, b_transposed=True,
        ride=scatter(_GRAD_GROUPS[2]),
        epilogue=_Epilogue(pre_bwd, rows=[h0, dh1], consts=[small["mix_pre_g"]], outs=[(d, F32)], accs=[d], lp=lp))
    if comm is not None:
        received.append(got)
    grad_x = dh0.reshape(batch, lp, d)[:, N_META:N_META + seq]
    partial = {"meta_tokens": _meta_grad(dh0, batch, lp), "lb_logits": dlb, "b_gate": db_gate, "hg_norm_g": dgh,
               "q_a_norm_g": dg_q, "kv_a_norm_g": dg_kv, "mix_pre_g": dg_mix_pre, "mix_post_g": dg_mix_post,
               "ffn_pre_g": dg_ffn_pre, "ffn_post_g": dg_ffn_post, "loss": loss_vec}
    return loss, grad_x, grads, partial, lb, received


_BIG = ["w_in", "w_hg_o", "w_q_b", "w_kv_b", "w_mla_o", "w_out", "w_ffn_in", "w_ffn_out"]
_COLUMN_SHARDED = {"w_in", "w_q_b", "w_kv_b", "w_ffn_in"}
_GRAD_GROUPS = [["w_ffn_in", "w_ffn_out"], ["w_out", "w_hg_o", "w_mla_o"], ["w_in", "w_q_b", "w_kv_b"]]
_SMALL = ["b_gate", "lb_logits", "hg_norm_g", "q_a_norm_g", "kv_a_norm_g", "mix_pre_g", "mix_post_g",
          "ffn_pre_g", "ffn_post_g"]


def _gathered_matrix(name, t):
    _, k, n = t.shape
    if name in _COLUMN_SHARDED:
        return t.transpose(1, 0, 2).reshape(k, N_DEV * n)
    return t.reshape(N_DEV * k, n)


def _scatter_layout(name, full):
    kk, nn = full.shape
    if name in _COLUMN_SHARDED:
        t = full.reshape(kk, N_DEV, nn // N_DEV).transpose(1, 0, 2)
    else:
        t = full.reshape(N_DEV, kk // N_DEV, nn)
    return t.astype(_WIRE_DTYPE)


def _model_w_in(wi):
    z = lambda *s: jnp.zeros(s, wi.dtype)
    kpe = wi[:, 4608:4672]
    c_blk = jnp.concatenate([wi[:, 4096:4608], kpe, z(1024, 64), _swap_halves(kpe), z(1024, 64), z(1024, 256)], axis=1)
    return {"w_in": jnp.concatenate([wi[:, :4096], c_blk, wi[:, 4672:]], axis=1).astype(_MXU_DTYPE)}


def _model_weights(full):
    return {**_model_w_in(full["w_in"]), **_model_rest(full)}


def _model_rest(full):
    wq3 = full["w_q_b"].reshape(Q_LORA, HEADS, HEAD_DIM + ROPE_DIM)
    pe = wq3[:, :, HEAD_DIM:]
    w_q = jnp.concatenate([wq3[:, :, :HEAD_DIM].reshape(Q_LORA, -1),
                           _pad_last(pe, HEAD_DIM).reshape(Q_LORA, -1),
                           _pad_last(_swap_halves(pe), HEAD_DIM).reshape(Q_LORA, -1)], axis=1)
    wkv3 = full["w_kv_b"].reshape(KV_LORA, HEADS, 2 * HEAD_DIM)
    w_k = wkv3[:, :, :HEAD_DIM].reshape(KV_LORA, -1)
    w_v = wkv3[:, :, HEAD_DIM:].reshape(KV_LORA, -1)
    w = {"w_q": w_q, "w_kv": jnp.concatenate([w_k, w_v], axis=1),
         "w_hg_o": full["w_hg_o"], "w_mla_o": full["w_mla_o"], "w_out": full["w_out"],
         "w_ffn_in": full["w_ffn_in"], "w_ffn_out": full["w_ffn_out"]}
    return {k: v.astype(_MXU_DTYPE) for k, v in w.items()}


def _reference_layout_grad(name, g):
    if name == "w_in":
        gi = g["w_in"]
        d_kpe = gi[:, 4608:4672] + _swap_halves(gi[:, 4736:4800])
        return jnp.concatenate([gi[:, :4608], d_kpe, gi[:, 5120:]], axis=1)
    if name == "w_q_b":
        gq = g["w_q"]
        d_pe = (gq[:, 1024:2048].reshape(Q_LORA, HEADS, HEAD_DIM)[:, :, :ROPE_DIM]
                + _swap_halves(gq[:, 2048:].reshape(Q_LORA, HEADS, HEAD_DIM)[:, :, :ROPE_DIM]))
        return jnp.concatenate([gq[:, :1024].reshape(Q_LORA, HEADS, HEAD_DIM), d_pe], axis=2).reshape(Q_LORA, -1)
    if name == "w_kv_b":
        return jnp.concatenate([g["w_k"].reshape(KV_LORA, HEADS, HEAD_DIM),
                                g["w_v"].reshape(KV_LORA, HEADS, HEAD_DIM)], axis=2).reshape(KV_LORA, -1)
    return g[name]


def _reference_layout_grads(g):
    return {n: _reference_layout_grad(n, g) for n in _BIG}


class _Comm:
    def __init__(self, shard):
        self.rest_payloads = [shard[n].astype(_WIRE_DTYPE) for n in _BIG[1:]]

    def rest_weights(self, gathered):
        return _model_rest({n: _gathered_matrix(n, t) for n, t in zip(_BIG[1:], gathered, strict=True)})

    def grad_parts(self, names, g):
        return [_scatter_layout(n, _reference_layout_grad(n, g)) for n in names]


def kernel(x, meta_tokens, w_in, b_gate, lb_logits, hg_norm_g, w_hg_o, q_a_norm_g, w_q_b, kv_a_norm_g, w_kv_b, w_mla_o, w_out, mix_pre_g, mix_post_g, ffn_pre_g, ffn_post_g, w_ffn_in, w_ffn_out, loss_target, m_meta_tokens, m_w_in, m_b_gate, m_lb_logits, m_hg_norm_g, m_w_hg_o, m_q_a_norm_g, m_w_q_b, m_kv_a_norm_g, m_w_kv_b, m_w_mla_o, m_w_out, m_mix_pre_g, m_mix_post_g, m_ffn_pre_g, m_ffn_post_g, m_w_ffn_in, m_w_ffn_out, v_meta_tokens, v_w_in, v_b_gate, v_lb_logits, v_hg_norm_g, v_w_hg_o, v_q_a_norm_g, v_w_q_b, v_kv_a_norm_g, v_w_kv_b, v_w_mla_o, v_w_out, v_mix_pre_g, v_mix_post_g, v_ffn_pre_g, v_ffn_post_g, v_w_ffn_in, v_w_ffn_out):
    args = dict(locals())
    batch, seq, d = x.shape
    lp = -(-(N_META + seq) // _LANES) * _LANES
    weight_names = ["meta_tokens", "w_in", "b_gate", "lb_logits", "hg_norm_g", "w_hg_o", "q_a_norm_g", "w_q_b",
                    "kv_a_norm_g", "w_kv_b", "w_mla_o", "w_out", "mix_pre_g", "mix_post_g", "ffn_pre_g",
                    "ffn_post_g", "w_ffn_in", "w_ffn_out"]
    shard = {n: args[n].reshape(args[n].shape[-2:]) for n in _BIG}
    comm = _Comm(shard)

    w_in_all, meta_all = _all_gather("gather_first", [shard["w_in"].astype(_WIRE_DTYPE), meta_tokens])
    w_first = _model_w_in(_gathered_matrix("w_in", w_in_all))
    meta_full = meta_all.transpose(1, 0, 2).reshape(N_META, d)
    small = {n: args[n] for n in _SMALL}

    _, grad_x, _, partial, lb, received = _forward_backward(x, loss_target, meta_full, w_first, small, lp=lp, comm=comm)
    out = {}
    for names, bufs in zip(_GRAD_GROUPS, received, strict=True):
        for n, buf in zip(names, bufs, strict=True):
            two_d = lambda t: t.reshape(t.shape[-2:])
            res = _sum_adamw("adamw_" + n, buf, shard[n], two_d(args["m_" + n]), two_d(args["v_" + n]))
            out[n] = [t.reshape(args[n].shape) for t in res]

    vec_names = _SMALL + ["meta_tokens"]
    *gathered, loss_parts = _all_gather("gather_vectors", [partial[n] for n in vec_names + ["loss"]])
    finished, loss = _finish_vectors(dict(zip(vec_names, gathered, strict=True)), lb,
                                     {n: (args[n], args["m_" + n], args["v_" + n]) for n in vec_names}, loss_parts)
    out.update(finished)
    return (loss, grad_x, *[out[n][i] for i in range(4) for n in weight_names])
```

```python
import functools

import jax
import jax.numpy as jnp
from jax import lax
from jax.experimental import pallas as pl
from jax.experimental.pallas import tpu as pltpu

F32 = jnp.float32
_MXU_DTYPE = jnp.bfloat16
_WIRE_DTYPE = jnp.bfloat16
_VMEM_LIMIT_BYTES = 56 * 1024 * 1024
_LANES = 128
_SUBLANES = 8

N_DEV = 8
N_META = 16
NORM_EPS = 1e-6
HEADS = 8
HEAD_DIM = 128
ROPE_DIM = 64
HG_CHUNK = 16
HG_BLOCK = 128
ROPE_THETA = 10000.0
D_MODEL = 1024
Q_LORA = 256
KV_LORA = 256
FFN_HIDDEN = 2816
ATTN_SCALE = (HEAD_DIM + ROPE_DIM) ** -0.5
NEG_BIG = -1e30

ADAM_LR = 0.001
ADAM_B1 = 0.9
ADAM_B2 = 0.999
ADAM_EPS = 1e-08
ADAM_WD = 0.01
ADAM_STEP = 10

CB_HQ, CB_HF, CB_HI, CB_HG, CB_C, CB_GA, CB_GB = range(7)
IN_COLS_PADDED = 7 * 1024


def _params(**kw):
    return pltpu.CompilerParams(vmem_limit_bytes=_VMEM_LIMIT_BYTES, **kw)


def _dot(a, b):
    return lax.dot_general(a, b, (((1,), (0,)), ((), ())), preferred_element_type=F32)


def _dot_nt(a, b):
    return lax.dot_general(a, b, (((1,), (1,)), ((), ())), preferred_element_type=F32)


def _dot_tn(a, b):
    return lax.dot_general(a, b, (((0,), (0,)), ((), ())), preferred_element_type=F32)


def _mx(x):
    return x.astype(_MXU_DTYPE)


def _exact_dot(m01, x, dot=_dot):
    if _MXU_DTYPE == jnp.float32:
        return dot(m01.astype(F32), x)
    m = m01.astype(jnp.bfloat16)
    x1 = x.astype(jnp.bfloat16)
    x2 = (x - x1.astype(F32)).astype(jnp.bfloat16)
    return dot(m, x1) + dot(m, x2)


def _exact_dot_nt(m01, x):
    return _exact_dot(m01, x, dot=_dot_nt)


def _sigmoid(x):
    return jax.nn.sigmoid(x)


def _silu_grad(x, s):
    return s * (1.0 + x * (1.0 - s))


def _rms_scale(x):
    return lax.rsqrt(jnp.mean(x * x, axis=-1, keepdims=True) + NORM_EPS)


def _rms_bwd(x, g, dy):
    r = _rms_scale(x)
    xh = x * r
    w = dy * g
    dx = r * (w - xh * jnp.mean(xh * w, axis=-1, keepdims=True))
    return dx, dy * xh


def _heads(fn, *arrays):
    outs = [fn(*[a[:, h * HEAD_DIM:(h + 1) * HEAD_DIM] for a in arrays]) for h in range(HEADS)]
    if isinstance(outs[0], tuple):
        return tuple(jnp.concatenate([o[i] for o in outs], axis=1) for i in range(len(outs[0])))
    return jnp.concatenate(outs, axis=1)


class _Ride:
    def __init__(self, payloads, gather):
        self.gather, self.args, self.n = gather, list(payloads), len(payloads)
        self.in_specs = [pl.BlockSpec(memory_space=pl.ANY)] * self.n
        self.out_shape = [jax.ShapeDtypeStruct((N_DEV, *p.shape[-2:]), p.dtype) for p in payloads]
        self.out_specs = [pl.BlockSpec(memory_space=pl.ANY)] * self.n
        self.scratch = [pltpu.SemaphoreType.DMA((self.n, N_DEV - 1)), pltpu.SemaphoreType.DMA((self.n, N_DEV - 1)),
                        pltpu.SemaphoreType.DMA((self.n,))]

    def split(self, rest, n_outs):
        n = self.n
        mine = (rest[:n], rest[n + n_outs:2 * n + n_outs], rest[-3:])
        return rest[n:n + n_outs] + rest[2 * n + n_outs:-3], mine

    def _copies(self, p_refs, out_refs, sems):
        send_sems, recv_sems, local_sems = sems
        x, y, c = lax.axis_index("x"), lax.axis_index("y"), lax.axis_index("c")
        me = 4 * x + 2 * y + c
        copies = []
        for i, (p_ref, out_ref) in enumerate(zip(p_refs, out_refs, strict=True)):
            part = (lambda j, p_ref=p_ref: p_ref) if self.gather else (lambda j, p_ref=p_ref: p_ref.at[j])
            copies.append(pltpu.make_async_copy(part(me), out_ref.at[me], local_sems.at[i]))
            for k in range(1, N_DEV):
                px, py, pc = x ^ (k >> 2), y ^ ((k >> 1) & 1), c ^ (k & 1)
                copies.append(pltpu.make_async_remote_copy(
                    src_ref=part(4 * px + 2 * py + pc), dst_ref=out_ref.at[me],
                    send_sem=send_sems.at[i, k - 1], recv_sem=recv_sems.at[i, k - 1],
                    device_id=(px, py, pc), device_id_type=pl.DeviceIdType.MESH))
        return copies

    def run(self, grid, refs):
        ids = [pl.program_id(i) for i in range(len(grid))]
        first = functools.reduce(jnp.logical_and, [i == 0 for i in ids])
        last = functools.reduce(jnp.logical_and, [i == g - 1 for i, g in zip(ids, grid)])

        @pl.when(first)
        def _():
            for cp in self._copies(*refs):
                cp.start()

        @pl.when(last)
        def _():
            for cp in self._copies(*refs):
                cp.wait()


class _NoRide:
    in_specs, out_shape, out_specs, scratch, args = [], [], [], [], []


def _matmul(name, a, b, *, out_dtype, tm, tn, tk, c_in=None, ride=None, b_transposed=False):
    m, k = a.shape
    n = b.shape[0] if b_transposed else b.shape[1]
    assert m % tm == 0 and n % tn == 0 and k % tk == 0, (name, a.shape, b.shape, tm, tn, tk)
    nk = k // tk
    has_c = c_in is not None
    dot = _dot_nt if b_transposed else _dot
    grid = (n // tn, m // tm, nk)
    n_in = 2 + has_c

    def body(*refs):
        a_ref, b_ref = refs[0], refs[1]
        c_ref = refs[2] if has_c else None
        rest = refs[n_in:]
        if ride is not None:
            rest, exchange = ride.split(rest, 1)
            ride.run(grid, exchange)
        o_ref = rest[0]
        acc_ref = rest[1] if nk > 1 else None

        def finish(r):
            if has_c:
                r = r + c_ref[...]
            o_ref[...] = r.astype(o_ref.dtype)

        if nk == 1:
            finish(dot(a_ref[...], b_ref[...]))
        else:
            kk = pl.program_id(2)

            @pl.when(kk == 0)
            def _():
                acc_ref[...] = jnp.zeros_like(acc_ref)

            acc_ref[...] += dot(a_ref[...], b_ref[...])

            @pl.when(kk == nk - 1)
            def _():
                finish(acc_ref[...])

    in_specs = [pl.BlockSpec((tm, tk), lambda j, i, kk: (i, kk)),
                pl.BlockSpec((tn, tk), lambda j, i, kk: (j, kk)) if b_transposed
                else pl.BlockSpec((tk, tn), lambda j, i, kk: (kk, j))]
    args = [a, b]
    aliases = {}
    if has_c:
        in_specs.append(pl.BlockSpec((tm, tn), lambda j, i, kk: (i, j)))
        args.append(c_in)
        aliases = {2: 0}
    out_shape = [jax.ShapeDtypeStruct((m, n), out_dtype)]
    out_specs = [pl.BlockSpec((tm, tn), lambda j, i, kk: (i, j))]
    scratch = [pltpu.VMEM((tm, tn), F32)] if nk > 1 else []
    if ride is not None:
        in_specs, args = in_specs + ride.in_specs, args + ride.args
        out_shape, out_specs, scratch = out_shape + ride.out_shape, out_specs + ride.out_specs, scratch + ride.scratch
    res = pl.pallas_call(
        body, name=name, out_shape=out_shape, grid=grid, in_specs=in_specs, out_specs=out_specs,
        scratch_shapes=scratch, input_output_aliases=aliases, compiler_params=_params(),
    )(*args)
    return res[0] if ride is None else res


class _Epilogue:
    def __init__(self, fn, *, rows=(), consts=(), pos=(), outs=(), accs=(), lp=None):
        self.fn, self.rows, self.consts, self.pos = fn, list(rows), list(consts), list(pos)
        self.outs, self.accs, self.lp = list(outs), list(accs), lp


def _matmul_segments(name, a_list, b, *, out_dtype=F32, tm, tn, tk, ride=None, b_transposed=False, epilogue=None):
    m = a_list[0].shape[0]
    n, k = b.shape if b_transposed else b.shape[::-1]
    steps = [a.shape[1] // tk for a in a_list]
    offs = [sum(steps[:s]) for s in range(len(steps))]
    nk = sum(steps)
    assert nk * tk == k and m % tm == 0 and n % tn == 0 and all(a.shape[1] % tk == 0 for a in a_list), name
    grid = (n // tn, m // tm, nk)
    n_seg = len(a_list)
    dot = _dot_nt if b_transposed else _dot
    ep = epilogue
    assert ep is None or tn == n, name
    n_extra = 0 if ep is None else len(ep.rows) + len(ep.consts) + len(ep.pos)
    n_outs = 1 if ep is None else len(ep.outs) + len(ep.accs)

    def body(*refs):
        a_refs, b_ref = refs[:n_seg], refs[n_seg]
        extra_refs, rest = refs[n_seg + 1:n_seg + 1 + n_extra], refs[n_seg + 1 + n_extra:]
        if ride is not None:
            rest, exchange = ride.split(rest, n_outs)
            ride.run(grid, exchange)
        out_refs, acc_ref = rest[:n_outs], rest[n_outs]
        i, kk = pl.program_id(1), pl.program_id(2)

        @pl.when(kk == 0)
        def _():
            acc_ref[...] = jnp.zeros_like(acc_ref)

        for s in range(n_seg):
            @pl.when((kk >= offs[s]) & (kk < offs[s] + steps[s]))
            def _(s=s):
                acc_ref[...] += dot(a_refs[s][...], b_ref[...])

        if ep is None:
            @pl.when(kk == nk - 1)
            def _():
                out_refs[0][...] = acc_ref[...].astype(out_refs[0].dtype)
        else:
            sum_refs = out_refs[len(ep.outs):]

            @pl.when((kk == 0) & (i == 0))
            def _():
                for ref in sum_refs:
                    ref[...] = jnp.zeros_like(ref)

            @pl.when(kk == nk - 1)
            def _():
                res_outs, res_sums = ep.fn(acc_ref[...], *[r[...] for r in extra_refs])
                for ref, val in zip(out_refs[:len(ep.outs)], res_outs, strict=True):
                    ref[...] = val.astype(ref.dtype)
                for ref, val in zip(sum_refs, res_sums, strict=True):
                    ref[...] += val.reshape(tm // _SUBLANES, _SUBLANES, val.shape[-1]).sum(axis=0)

    seg_spec = lambda s: pl.BlockSpec(
        (tm, tk), functools.partial(lambda j, i, kk, off, ns: (i, jnp.clip(kk - off, 0, ns - 1)), off=offs[s], ns=steps[s]))
    b_spec = (pl.BlockSpec((tn, tk), lambda j, i, kk: (j, kk)) if b_transposed
              else pl.BlockSpec((tk, tn), lambda j, i, kk: (kk, j)))
    in_specs = [seg_spec(s) for s in range(n_seg)] + [b_spec]
    args = list(a_list) + [b]
    row_spec = lambda w: pl.BlockSpec((tm, w), lambda j, i, kk: (i, 0))
    if ep is None:
        out_shape = [jax.ShapeDtypeStruct((m, n), out_dtype)]
        out_specs = [pl.BlockSpec((tm, tn), lambda j, i, kk: (i, j))]
    else:
        tiles_per_example = ep.lp // tm
        row_ins = [r if isinstance(r, tuple) else (r, r.shape[1], 0) for r in ep.rows]
        in_specs += ([pl.BlockSpec((tm, wd), functools.partial(lambda j, i, kk, cb: (i, cb), cb=cb)) for _, wd, cb in row_ins]
                     + [pl.BlockSpec(c.shape, lambda j, i, kk: (0, 0)) for c in ep.consts]
                     + [pl.BlockSpec((tm, p.shape[1]), lambda j, i, kk: (i % tiles_per_example, 0)) for p in ep.pos])
        args += [arr for arr, _, _ in row_ins] + ep.consts + ep.pos
        out_shape = ([jax.ShapeDtypeStruct((m, w), dt) for w, dt in ep.outs]
                     + [jax.ShapeDtypeStruct((_SUBLANES, w), F32) for w in ep.accs])
        out_specs = ([row_spec(w) for w, _ in ep.outs]
                     + [pl.BlockSpec((_SUBLANES, w), lambda j, i, kk: (0, 0)) for w in ep.accs])
    scratch = [pltpu.VMEM((tm, tn), F32)]
    if ride is not None:
        in_specs, args = in_specs + ride.in_specs, args + ride.args
        out_shape, out_specs, scratch = out_shape + ride.out_shape, out_specs + ride.out_specs, scratch + ride.scratch
    res = pl.pallas_call(
        body, name=name, out_shape=out_shape, grid=grid, in_specs=in_specs, out_specs=out_specs,
        scratch_shapes=scratch, compiler_params=_params(),
    )(*args)
    if ep is None:
        return res[0] if ride is None else res
    n_o = len(ep.outs)
    return (res[:n_o], res[n_o:n_outs], *res[n_outs:])


def _matmul_tn(name, x, dy, *, tk, tn, tr, out_dtype=F32):
    r, k = x.shape
    _, n = dy.shape
    assert r % tr == 0 and k % tk == 0 and n % tn == 0, (name, x.shape, dy.shape)
    n_r = r // tr
    direct = out_dtype == F32

    def body(x_ref, dy_ref, o_ref, *scratch):
        acc_ref = o_ref if direct else scratch[0]

        @pl.when(pl.program_id(2) == 0)
        def _():
            acc_ref[...] = jnp.zeros_like(acc_ref)

        acc_ref[...] += _dot_tn(x_ref[...], dy_ref[...])
        if not direct:
            @pl.when(pl.program_id(2) == n_r - 1)
            def _():
                o_ref[...] = acc_ref[...].astype(o_ref.dtype)

    return pl.pallas_call(
        body, name=name,
        out_shape=jax.ShapeDtypeStruct((k, n), out_dtype),
        grid=(k // tk, n // tn, n_r),
        in_specs=[pl.BlockSpec((tr, tk), lambda kb, nb, rr: (rr, kb)),
                  pl.BlockSpec((tr, tn), lambda kb, nb, rr: (rr, nb))],
        out_specs=pl.BlockSpec((tk, tn), lambda kb, nb, rr: (kb, nb)),
        scratch_shapes=[] if direct else [pltpu.VMEM((tk, tn), F32)],
        compiler_params=_params(),
    )(x, dy)


def _ffn_in_swiglu(u, w, *, tm, tn):
    r, k = u.shape
    h = w.shape[1] // 2
    assert r % tm == 0 and h % tn == 0
    nj = h // tn

    def body(u_ref, wg_ref, wu_ref, act_ref, gt_ref, up_ref):
        uu = u_ref[...]
        gt, up = _dot(uu, wg_ref[...]), _dot(uu, wu_ref[...])
        act_ref[...] = (gt * _sigmoid(gt) * up).astype(act_ref.dtype)
        gt_ref[...] = gt.astype(gt_ref.dtype)
        up_ref[...] = up.astype(up_ref.dtype)

    tile = pl.BlockSpec((tm, tn), lambda j, i: (i, j))
    return pl.pallas_call(
        body, name="ffn_in_swiglu",
        out_shape=[jax.ShapeDtypeStruct((r, h), _MXU_DTYPE)] * 3,
        grid=(nj, r // tm),
        in_specs=[pl.BlockSpec((tm, k), lambda j, i: (i, 0)),
                  pl.BlockSpec((k, tn), lambda j, i: (0, j)),
                  pl.BlockSpec((k, tn), lambda j, i: (0, nj + j))],
        out_specs=[tile] * 3,
        compiler_params=_params(),
    )(u, w, w)


def _d_ffn_out_swiglu(dy, w, gt, up, *, tm, tn):
    r, k = dy.shape
    h = w.shape[0]
    assert r % tm == 0 and h % tn == 0

    def body(dy_ref, w_ref, gt_ref, up_ref, dgt_ref, dup_ref):
        da = _dot_nt(dy_ref[...], w_ref[...])
        g, u_ = gt_ref[...].astype(F32), up_ref[...].astype(F32)
        s = _sigmoid(g)
        dgt_ref[...] = (da * u_ * _silu_grad(g, s)).astype(dgt_ref.dtype)
        dup_ref[...] = (da * g * s).astype(dup_ref.dtype)

    tile = pl.BlockSpec((tm, tn), lambda j, i: (i, j))
    return pl.pallas_call(
        body, name="d_ffn_out_swiglu",
        out_shape=[jax.ShapeDtypeStruct((r, h), _MXU_DTYPE)] * 2,
        grid=(h // tn, r // tm),
        in_specs=[pl.BlockSpec((tm, k), lambda j, i: (i, 0)), pl.BlockSpec((tn, k), lambda j, i: (j, 0)), tile, tile],
        out_specs=[tile] * 2,
        compiler_params=_params(),
    )(dy, w, gt, up)


def _proj_q_rope(cqn, w_q, c_tab, s_tab, *, tm, lp):
    r, k = cqn.shape
    tiles_per_example = lp // tm

    def body(x_ref, wn_ref, wp_ref, ws_ref, c_ref, s_ref, o_ref):
        x = x_ref[...]
        roped = _dot(x, wp_ref[...]) * c_ref[...] + _dot(x, ws_ref[...]) * s_ref[...]
        o_ref[...] = jnp.concatenate([_dot(x, wn_ref[...]), roped], axis=1).astype(o_ref.dtype)

    w_blk = lambda part: pl.BlockSpec((k, HEAD_DIM), functools.partial(lambda h, i, part: (0, part * HEADS + h), part=part))
    tab = pl.BlockSpec((tm, HEAD_DIM), lambda h, i: (i % tiles_per_example, 0))
    return pl.pallas_call(
        body, name="proj_q_rope",
        out_shape=jax.ShapeDtypeStruct((r, HEADS * QK_DIM), _MXU_DTYPE),
        grid=(HEADS, r // tm),
        in_specs=[pl.BlockSpec((tm, k), lambda h, i: (i, 0)), w_blk(0), w_blk(1), w_blk(2), tab, tab],
        out_specs=pl.BlockSpec((tm, QK_DIM), lambda h, i: (i, h)),
        compiler_params=_params(),
    )(cqn, w_q, w_q, w_q, c_tab, s_tab)


def _rowwise(name, body, *, rows, tr, lp, ins, outs, accs=()):
    assert rows % tr == 0 and lp % tr == 0 and tr % 16 == 0
    tiles_per_example = lp // tr
    in_specs, arrays = [], []
    for spec in ins:
        if spec[0] == "row":
            _, arr, width, cb = spec
            in_specs.append(pl.BlockSpec((tr, width), functools.partial(lambda i, cb: (i, cb), cb=cb)))
        elif spec[0] == "const":
            arr = spec[1]
            in_specs.append(pl.BlockSpec(arr.shape, lambda i: (0, 0)))
        else:
            arr = spec[1]
            in_specs.append(pl.BlockSpec((tr, arr.shape[1]), lambda i: (i % tiles_per_example, 0)))
        arrays.append(arr)
    n_in, n_out = len(ins), len(outs)

    def kern(*refs):
        res_outs, res_accs = body(*[r[...] for r in refs[:n_in]])
        for ref, val in zip(refs[n_in:n_in + n_out], res_outs, strict=True):
            ref[...] = val.astype(ref.dtype)
        acc_refs = refs[n_in + n_out:]
        if acc_refs:
            @pl.when(pl.program_id(0) == 0)
            def _():
                for ref in acc_refs:
                    ref[...] = jnp.zeros_like(ref)

            for ref, val in zip(acc_refs, res_accs, strict=True):
                ref[...] += val.reshape(tr // _SUBLANES, _SUBLANES, val.shape[-1]).sum(axis=0)

    out_shape = ([jax.ShapeDtypeStruct((rows, w), dt) for w, dt in outs]
                 + [jax.ShapeDtypeStruct((_SUBLANES, w), F32) for w in accs])
    out_specs = ([pl.BlockSpec((tr, w), lambda i: (i, 0)) for w, _ in outs]
                 + [pl.BlockSpec((_SUBLANES, w), lambda i: (0, 0)) for w in accs])
    res = pl.pallas_call(
        kern, name=name, out_shape=out_shape, grid=(rows // tr,),
        in_specs=in_specs, out_specs=out_specs, compiler_params=_params(),
    )(*arrays)
    return res[:n_out], list(res[n_out:])


def _assemble(name, x, head_rows, lp):
    batch, seq, d = x.shape
    tc = 256

    def body(x_ref, m_ref, o_ref):
        o_ref[0:N_META, :] = m_ref[...]
        o_ref[N_META:N_META + seq, :] = x_ref[0]
        if lp > N_META + seq:
            o_ref[N_META + seq:, :] = jnp.zeros((lp - N_META - seq, tc), F32)

    return pl.pallas_call(
        body, name=name,
        out_shape=jax.ShapeDtypeStruct((batch * lp, d), F32),
        grid=(batch, d // tc),
        in_specs=[pl.BlockSpec((1, seq, tc), lambda b, j: (b, 0, j)),
                  pl.BlockSpec((N_META, tc), lambda b, j: (0, j))],
        out_specs=pl.BlockSpec((lp, tc), lambda b, j: (b, j)),
        compiler_params=_params(),
    )(x, head_rows)


def _meta_grad(dh0, batch, lp):
    d = dh0.shape[1]

    def body(g_ref, o_ref):
        @pl.when(pl.program_id(0) == 0)
        def _():
            o_ref[...] = jnp.zeros_like(o_ref)

        o_ref[...] += g_ref[...]

    return pl.pallas_call(
        body, name="meta_grad",
        out_shape=jax.ShapeDtypeStruct((N_META, d), F32),
        grid=(batch,),
        in_specs=[pl.BlockSpec((N_META, d), lambda b: (b * (lp // N_META), 0))],
        out_specs=pl.BlockSpec((N_META, d), lambda b: (0, 0)),
        compiler_params=_params(),
    )(dh0)


def _segment_masks():
    t = lax.broadcasted_iota(jnp.int32, (HG_BLOCK, HG_BLOCK), 0)
    s = lax.broadcasted_iota(jnp.int32, (HG_BLOCK, HG_BLOCK), 1)
    same = lax.shift_right_logical(t, 4) == lax.shift_right_logical(s, 4)
    lower = same & (s <= t)
    upper = same & (s >= t)
    first_half = same & ((s & 15) <= 7)
    return same, lower, upper, first_half


def _hgrn_gates(hq, hf, lb):
    sq = _sigmoid(hq)
    q = hq * sq
    sf = _sigmoid(hf)
    f = lb + (1.0 - lb) * sf
    return q, sq, sf, f


def _hgrn_decays(g, same, lower, first_half):
    b = _exact_dot(lower, g)
    b_last = _exact_dot(same, g)
    b_ref = _exact_dot(first_half, g)
    return b, b_last, b_ref


def _hgrn_fwd(p, lb, gh, *, batch, lp, ride=None):
    rows = batch * lp
    nb = lp // HG_BLOCK
    n_chunks = HG_BLOCK // HG_CHUNK

    def body(hq_ref, hf_ref, hi_ref, hg_ref, lb_ref, gh_ref, *rest):
        if ride is not None:
            rest, exchange = ride.split(rest, 3)
            ride.run((batch, nb), exchange)
        o_ref, z_ref, st_ref, s_scr = rest

        @pl.when(pl.program_id(1) == 0)
        def _():
            s_scr[...] = jnp.zeros_like(s_scr)

        same, lower, _, first_half = _segment_masks()
        v = hi_ref[...]
        q, _, _, f = _hgrn_gates(hq_ref[...], hf_ref[...], lb_ref[...])
        k = 1.0 - f
        b, b_last, b_ref = _hgrn_decays(jnp.log(f), same, lower, first_half)
        qt = _mx(q * jnp.exp(b))
        kh = _mx(k * jnp.exp(b_last - b))
        vm = _mx(v)
        el = jnp.exp(b_last)
        qc = _mx(q * jnp.exp(b - b_ref))
        kc = _mx(k * jnp.exp(b_ref - b))

        def intra(qc_h, kc_h, v_h):
            a = jnp.where(lower, _dot_nt(qc_h, kc_h), 0.0)
            return _dot(_mx(a), v_h)

        o_intra = _heads(intra, qc, kc, vm)

        states = [s_scr[h] for h in range(HEADS)]
        o_inter = [[None] * HEADS for _ in range(n_chunks)]
        for c in range(n_chunks):
            rs = slice(c * HG_CHUNK, (c + 1) * HG_CHUNK)
            for h in range(HEADS):
                cs = slice(h * HEAD_DIM, (h + 1) * HEAD_DIM)
                st_m = _mx(states[h])
                st_ref[c, h] = st_m
                o_inter[c][h] = _dot_nt(qt[rs, cs], st_m)
                states[h] = states[h] * el[c * HG_CHUNK:c * HG_CHUNK + 1, cs] + _dot_tn(vm[rs, cs], kh[rs, cs])
        for h in range(HEADS):
            s_scr[h] = states[h]

        o = o_intra + jnp.concatenate([jnp.concatenate(row, axis=1) for row in o_inter], axis=0)
        o_ref[...] = o
        hg = hg_ref[...]
        n = _heads(lambda o_h: o_h * _rms_scale(o_h), o) * gh_ref[...]
        z_ref[...] = (n * hg * _sigmoid(hg)).astype(z_ref.dtype)

    blk = lambda cb: pl.BlockSpec((HG_BLOCK, D_MODEL), functools.partial(lambda b, j, cb: (b * nb + j, cb), cb=cb))
    row_out = pl.BlockSpec((HG_BLOCK, D_MODEL), lambda b, j: (b * nb + j, 0))
    const = pl.BlockSpec((1, D_MODEL), lambda b, j: (0, 0))
    extra = ride if ride is not None else _NoRide
    return pl.pallas_call(
        body, name="hgrn_fwd",
        out_shape=[jax.ShapeDtypeStruct((rows, D_MODEL), F32),
                   jax.ShapeDtypeStruct((rows, D_MODEL), _MXU_DTYPE),
                   jax.ShapeDtypeStruct((rows // HG_CHUNK, HEADS, HEAD_DIM, HEAD_DIM), _MXU_DTYPE)] + extra.out_shape,
        grid=(batch, nb),
        in_specs=[blk(CB_HQ), blk(CB_HF), blk(CB_HI), blk(CB_HG), const, const] + extra.in_specs,
        out_specs=[row_out, row_out,
                   pl.BlockSpec((n_chunks, HEADS, HEAD_DIM, HEAD_DIM), lambda b, j: (b * nb + j, 0, 0, 0))]
        + extra.out_specs,
        scratch_shapes=[pltpu.VMEM((HEADS, HEAD_DIM, HEAD_DIM), F32)] + extra.scratch,
        compiler_params=_params(),
    )(p, p, p, p, lb, gh, *extra.args)


def _hgrn_bwd(p, o, dz, states, lb, gh, *, batch, lp, ride=None):
    rows = batch * lp
    nb = lp // HG_BLOCK
    n_chunks = HG_BLOCK // HG_CHUNK

    def body(hq_ref, hf_ref, hi_ref, hg_ref, o_ref, dz_ref, st_ref, lb_ref, gh_ref, *rest):
        if ride is not None:
            rest, exchange = ride.split(rest, 3)
            ride.run((batch, nb), exchange)
        dp_ref, dlb_ref, dgh_ref, ds_scr = rest
        first = (pl.program_id(0) == 0) & (pl.program_id(1) == 0)

        @pl.when(first)
        def _():
            dlb_ref[...] = jnp.zeros_like(dlb_ref)
            dgh_ref[...] = jnp.zeros_like(dgh_ref)

        @pl.when(pl.program_id(1) == 0)
        def _():
            ds_scr[...] = jnp.zeros_like(ds_scr)

        same, lower, upper, first_half = _segment_masks()
        lbv = lb_ref[...]
        hq, hf, v, hg = hq_ref[...], hf_ref[...], hi_ref[...], hg_ref[...]
        q, sq, sf, f = _hgrn_gates(hq, hf, lbv)
        k = 1.0 - f
        b, b_last, b_ref = _hgrn_decays(jnp.log(f), same, lower, first_half)
        e_b = jnp.exp(b)
        e_kh = jnp.exp(b_last - b)
        e_qc = jnp.exp(b - b_ref)
        e_kc = jnp.exp(b_ref - b)
        qt, kh, qc, kc = q * e_b, k * e_kh, q * e_qc, k * e_kc

        o = o_ref[...]
        dz = dz_ref[...].astype(F32)
        ghv = gh_ref[...]
        sg = _sigmoid(hg)
        r = _heads(lambda o_h: jnp.broadcast_to(_rms_scale(o_h), o_h.shape), o)
        oh = o * r
        dn = dz * hg * sg
        dhg = dz * oh * ghv * _silu_grad(hg, sg)
        w = dn * ghv
        do = r * (w - oh * _heads(lambda t: jnp.broadcast_to(jnp.mean(t, axis=-1, keepdims=True), t.shape), oh * w))
        dgh_ref[...] += (dn * oh).reshape(HG_BLOCK // _SUBLANES, _SUBLANES, D_MODEL).sum(axis=0)

        qt_m, kh_m, v_m, do_m = _mx(qt), _mx(kh), _mx(v), _mx(do)
        el_all = jnp.exp(b_last)

        def intra(qc_h, kc_h, v_h, do_h):
            a = _mx(jnp.where(lower, _dot_nt(qc_h, kc_h), 0.0))
            da = _mx(jnp.where(lower, _dot_nt(do_h, v_h), 0.0))
            return _dot(da, kc_h), _dot_tn(da, qc_h), _dot_tn(a, do_h)

        dqc, dkc, dv_intra = _heads(intra, _mx(qc), _mx(kc), v_m, do_m)

        d_states = [ds_scr[h] for h in range(HEADS)]
        grid_of = lambda: [[None] * HEADS for _ in range(n_chunks)]
        dkh_p, dv_p, dbl_p, dqt_p = grid_of(), grid_of(), grid_of(), grid_of()
        for c in reversed(range(n_chunks)):
            rs = slice(c * HG_CHUNK, (c + 1) * HG_CHUNK)
            for h in range(HEADS):
                cs = slice(h * HEAD_DIM, (h + 1) * HEAD_DIM)
                st = st_ref[c, h]
                ds_t = d_states[h]
                ds_m = _mx(ds_t)
                el = el_all[c * HG_CHUNK:c * HG_CHUNK + 1, cs]
                dkh_p[c][h] = _dot(v_m[rs, cs], ds_m)
                dv_p[c][h] = _dot_nt(kh_m[rs, cs], ds_m)
                dbl = jnp.sum(ds_t * st.astype(F32), axis=0, keepdims=True) * el
                dbl_p[c][h] = jnp.broadcast_to(dbl, (HG_CHUNK, HEAD_DIM))
                dqt_p[c][h] = _dot(do_m[rs, cs], st)
                d_states[h] = ds_t * el + _dot_tn(do_m[rs, cs], qt_m[rs, cs])
        for h in range(HEADS):
            ds_scr[h] = d_states[h]
        whole = lambda parts: jnp.concatenate([jnp.concatenate(row, axis=1) for row in parts], axis=0)

        dqt, dkh = whole(dqt_p), whole(dkh_p)
        dq = dqt * e_b + dqc * e_qc
        dk = dkh * e_kh + dkc * e_kc
        t_kh = dkh * kh
        db_rows = dqt * qt + dqc * qc - dkc * kc - t_kh
        dg = _exact_dot(upper, db_rows) + _exact_dot(same, t_kh) + whole(dbl_p)
        df = dg / f - dk
        dhf = df * (1.0 - lbv) * sf * (1.0 - sf)
        dlb_ref[...] += (df * (1.0 - sf)).reshape(HG_BLOCK // _SUBLANES, _SUBLANES, D_MODEL).sum(axis=0)
        dhq = dq * _silu_grad(hq, sq)
        dp_ref[...] = jnp.concatenate([dhq, dhf, dv_intra + whole(dv_p), dhg], axis=1).astype(dp_ref.dtype)

    rev = lambda b, j: b * nb + (nb - 1 - j)
    blk = lambda cb: pl.BlockSpec((HG_BLOCK, D_MODEL), functools.partial(lambda b, j, cb: (rev(b, j), cb), cb=cb))
    row = pl.BlockSpec((HG_BLOCK, D_MODEL), lambda b, j: (rev(b, j), 0))
    const = pl.BlockSpec((1, D_MODEL), lambda b, j: (0, 0))
    acc = pl.BlockSpec((_SUBLANES, D_MODEL), lambda b, j: (0, 0))
    extra = ride if ride is not None else _NoRide
    dp, dlb, dgh, *exchanged = pl.pallas_call(
        body, name="hgrn_bwd",
        out_shape=[jax.ShapeDtypeStruct((rows, 4 * D_MODEL), _MXU_DTYPE),
                   jax.ShapeDtypeStruct((_SUBLANES, D_MODEL), F32),
                   jax.ShapeDtypeStruct((_SUBLANES, D_MODEL), F32)] + extra.out_shape,
        grid=(batch, nb),
        in_specs=[blk(CB_HQ), blk(CB_HF), blk(CB_HI), blk(CB_HG), row, row,
                  pl.BlockSpec((n_chunks, HEADS, HEAD_DIM, HEAD_DIM), lambda b, j: (rev(b, j), 0, 0, 0)),
                  const, const] + extra.in_specs,
        out_specs=[pl.BlockSpec((HG_BLOCK, 4 * D_MODEL), lambda b, j: (rev(b, j), 0)), acc, acc] + extra.out_specs,
        scratch_shapes=[pltpu.VMEM((HEADS, HEAD_DIM, HEAD_DIM), F32)] + extra.scratch,
        compiler_params=_params(),
    )(p, p, p, p, o, dz, states, lb, gh, *extra.args)
    return (dp, dlb, dgh, *exchanged)


QK_DIM = 2 * HEAD_DIM
ATTN_TQ = 256
ATTN_KEY_CHUNK = 512


def _query_tiles(lp):
    return [(r0, min(ATTN_TQ, lp - r0)) for r0 in range(0, lp, ATTN_TQ)]


def _attn_fwd(q_cat, kv, kp, *, batch, lp):
    rows = batch * lp

    def body(q_ref, kn_ref, kp_ref, v_ref, o_ref, lse_ref):
        k_cat = jnp.concatenate([kn_ref[...], kp_ref[...]], axis=1)
        for r0, tq in _query_tiles(lp):
            q_t = q_ref[r0:r0 + tq, :]
            i = lax.broadcasted_iota(jnp.int32, (tq, tq), 0)
            j = lax.broadcasted_iota(jnp.int32, (tq, tq), 1)
            s_diag = jnp.where(j <= i, _dot_nt(q_t, k_cat[r0:r0 + tq]) * ATTN_SCALE, NEG_BIG)
            m = jnp.max(s_diag, axis=1, keepdims=True)
            if r0:
                s_past = _dot_nt(q_t, k_cat[0:r0]) * ATTN_SCALE
                m = jnp.maximum(m, jnp.max(s_past, axis=1, keepdims=True))
            p_diag = jnp.exp(s_diag - m)
            l = jnp.sum(p_diag, axis=1, keepdims=True)
            acc = _dot(_mx(p_diag), v_ref[r0:r0 + tq, :])
            if r0:
                p_past = jnp.exp(s_past - m)
                l = l + jnp.sum(p_past, axis=1, keepdims=True)
                acc = acc + _dot(_mx(p_past), v_ref[0:r0, :])
            o_ref[r0:r0 + tq, :] = (acc / l).astype(o_ref.dtype)
            lse_ref[r0:r0 + tq, :] = jnp.broadcast_to(m + jnp.log(l), (tq, HEAD_DIM))

    head_blk = pl.BlockSpec((lp, HEAD_DIM), lambda b, h: (b, h))
    return pl.pallas_call(
        body, name="attn_fwd",
        out_shape=[jax.ShapeDtypeStruct((rows, D_MODEL), _MXU_DTYPE),
                   jax.ShapeDtypeStruct((rows, D_MODEL), F32)],
        grid=(batch, HEADS),
        in_specs=[pl.BlockSpec((lp, QK_DIM), lambda b, h: (b, h)), head_blk,
                  pl.BlockSpec((lp, HEAD_DIM), lambda b, h: (b, 0)),
                  pl.BlockSpec((lp, HEAD_DIM), lambda b, h: (b, HEADS + h))],
        out_specs=[head_blk, head_blk],
        compiler_params=_params(),
    )(q_cat, kv, kp, kv)


def _attn_bwd(q_cat, kv, kp, do, o, lse, c_tab, s_tab, *, batch, lp, ride=None):
    rows = batch * lp

    def body(q_ref, kn_ref, kp_ref, v_ref, do_ref, o_ref, lse_ref, c_ref, s_ref, *rest):
        if ride is not None:
            rest, exchange = ride.split(rest, 6)
            ride.run((batch, HEADS), exchange)
        dqn_ref, dqc_ref, dqs_ref, dkn_ref, dkp_ref, dv_ref, dk_acc, dv_acc = rest
        dk_acc[...] = jnp.zeros_like(dk_acc)
        dv_acc[...] = jnp.zeros_like(dv_acc)
        k_cat = jnp.concatenate([kn_ref[...], kp_ref[...]], axis=1)
        k_t = k_cat.T
        lane = lax.broadcasted_iota(jnp.int32, (_SUBLANES, HEAD_DIM), 1)
        lse_row = _exact_dot_nt(lane == 0, lse_ref[...])
        delta = _exact_dot_nt(lane >= 0, do_ref[...].astype(F32) * o_ref[...].astype(F32))
        for r0, tq in _query_tiles(lp):
            cols = slice(r0, r0 + tq)
            q_t_, do_t_ = q_ref[cols, :], do_ref[cols, :]
            lse_t, delta_t = lse_row[0:1, cols], delta[0:1, cols]
            chunks = [(c0, min(ATTN_KEY_CHUNK, r0 - c0), False) for c0 in range(0, r0, ATTN_KEY_CHUNK)] + [(r0, tq, True)]
            dq_t = jnp.zeros((QK_DIM, tq), F32)
            for c0, n, diagonal in chunks:
                keys = slice(c0, c0 + n)
                s = _dot_nt(k_cat[keys], q_t_) * ATTN_SCALE
                if diagonal:
                    jk = lax.broadcasted_iota(jnp.int32, (n, tq), 0)
                    iq = lax.broadcasted_iota(jnp.int32, (n, tq), 1)
                    s = jnp.where(jk <= iq, s, NEG_BIG)
                pexp = jnp.exp(s - lse_t)
                dp = _dot_nt(v_ref[keys, :], do_t_)
                ds = _mx(pexp * (dp - delta_t) * ATTN_SCALE)
                dk_acc[keys, :] += _dot(ds, q_t_)
                dv_acc[keys, :] += _dot(_mx(pexp), do_t_)
                dq_t = dq_t + _dot(k_t[:, keys], ds)
            dq = dq_t.T
            d_rope = dq[:, HEAD_DIM:]
            dqn_ref[cols, :] = dq[:, :HEAD_DIM].astype(dqn_ref.dtype)
            dqc_ref[cols, :] = (d_rope * c_ref[cols, :]).astype(dqc_ref.dtype)
            dqs_ref[cols, :] = (d_rope * s_ref[cols, :]).astype(dqs_ref.dtype)

        dkn_ref[...] = dk_acc[:, 0:HEAD_DIM].astype(dkn_ref.dtype)
        dv_ref[...] = dv_acc[...].astype(dv_ref.dtype)

        @pl.when(pl.program_id(1) == 0)
        def _():
            dkp_ref[...] = jnp.zeros_like(dkp_ref)

        dkp_ref[...] += dk_acc[:, HEAD_DIM:]

    head_blk = pl.BlockSpec((lp, HEAD_DIM), lambda b, h: (b, h))
    cat_blk = pl.BlockSpec((lp, QK_DIM), lambda b, h: (b, h))
    shared_blk = pl.BlockSpec((lp, HEAD_DIM), lambda b, h: (b, 0))
    table_blk = pl.BlockSpec((lp, HEAD_DIM), lambda b, h: (0, 0))
    extra = ride if ride is not None else _NoRide
    return pl.pallas_call(
        body, name="attn_bwd",
        out_shape=[jax.ShapeDtypeStruct((rows, D_MODEL), _MXU_DTYPE)] * 3 + [
                   jax.ShapeDtypeStruct((rows, D_MODEL), _MXU_DTYPE),
                   jax.ShapeDtypeStruct((rows, HEAD_DIM), F32),
                   jax.ShapeDtypeStruct((rows, D_MODEL), _MXU_DTYPE)] + extra.out_shape,
        grid=(batch, HEADS),
        in_specs=[cat_blk, head_blk, shared_blk, pl.BlockSpec((lp, HEAD_DIM), lambda b, h: (b, HEADS + h)),
                  head_blk, head_blk, head_blk, table_blk, table_blk] + extra.in_specs,
        out_specs=[head_blk, head_blk, head_blk, head_blk, shared_blk, head_blk] + extra.out_specs,
        scratch_shapes=[pltpu.VMEM((lp, QK_DIM), F32), pltpu.VMEM((lp, HEAD_DIM), F32)] + extra.scratch,
        compiler_params=_params(),
    )(q_cat, kv, kp, kv, do, o, lse, c_tab, s_tab, *extra.args)


def _all_gather(name, blocks):
    n = len(blocks)

    def body(*refs):
        x_refs, out_refs, (send_sems, recv_sems, local_sems) = refs[:n], refs[n:2 * n], refs[2 * n:]
        x, y, c = lax.axis_index("x"), lax.axis_index("y"), lax.axis_index("c")
        me, sibling = (x, y, c), (x, y, 1 - c)
        chips = [(1 - x, y), (x, 1 - y), (1 - x, 1 - y)]

        def slot(i, px, py, pc):
            return out_refs[i].at[4 * px + 2 * py + pc]

        def copy(i, k, blk, to, src=None):
            return pltpu.make_async_remote_copy(
                src_ref=slot(i, *blk) if src is None else src, dst_ref=slot(i, *blk),
                send_sem=send_sems.at[i, k], recv_sem=recv_sems.at[i, k],
                device_id=to, device_id_type=pl.DeviceIdType.MESH)

        mine = [pltpu.make_async_copy(x_refs[i], slot(i, *me), local_sems.at[i]) for i in range(n)]
        first = [copy(i, 0, me, sibling, src=x_refs[i]) for i in range(n)]
        first += [copy(i, 1 + j, me, (*chip, c), src=x_refs[i]) for i in range(n) for j, chip in enumerate(chips)]
        for cp in mine + first:
            cp.start()
        passed = []
        for i in range(n):
            for j, chip in enumerate(chips):
                copy(i, 1 + j, (*chip, c), me).wait_recv()
                passed.append(copy(i, 4 + j, (*chip, c), sibling))
                passed[-1].start()
        for i in range(n):
            copy(i, 0, sibling, me).wait_recv()
            for j, chip in enumerate(chips):
                copy(i, 4 + j, (*chip, 1 - c), me).wait_recv()
        for cp in first + passed:
            cp.wait_send()
        for cp in mine:
            cp.wait()

    return pl.pallas_call(
        body, name=name,
        out_shape=[jax.ShapeDtypeStruct((N_DEV, *b.shape), b.dtype) for b in blocks],
        in_specs=[pl.BlockSpec(memory_space=pl.ANY)] * n,
        out_specs=[pl.BlockSpec(memory_space=pl.ANY)] * n,
        scratch_shapes=[pltpu.SemaphoreType.DMA((n, 7)), pltpu.SemaphoreType.DMA((n, 7)),
                        pltpu.SemaphoreType.DMA((n,))],
    )(*blocks)


def _adamw_math(w, g, m, v):
    nm = ADAM_B1 * m + (1.0 - ADAM_B1) * g
    nv = ADAM_B2 * v + (1.0 - ADAM_B2) * (g * g)
    m_hat = nm / (1.0 - ADAM_B1 ** ADAM_STEP)
    v_hat = nv / (1.0 - ADAM_B2 ** ADAM_STEP)
    return -ADAM_LR * (m_hat / (jnp.sqrt(v_hat) + ADAM_EPS) + ADAM_WD * w), nm, nv


def _sum_adamw(name, parts, w, m, v):
    rows, cols = w.shape
    tr = rows // 4 if rows % 64 == 0 and rows * cols > (1 << 16) else rows

    def body(p_ref, w_ref, m_ref, v_ref, g_ref, d_ref, nm_ref, nv_ref):
        g = p_ref[0].astype(F32)
        for dev in range(1, N_DEV):
            g = g + p_ref[dev].astype(F32)
        g_ref[...] = g
        d_ref[...], nm_ref[...], nv_ref[...] = _adamw_math(w_ref[...], g, m_ref[...], v_ref[...])

    spec = pl.BlockSpec((tr, cols), lambda i: (i, 0))
    return pl.pallas_call(
        body, name=name,
        out_shape=[jax.ShapeDtypeStruct((rows, cols), F32)] * 4,
        grid=(rows // tr,),
        in_specs=[pl.BlockSpec((N_DEV, tr, cols), lambda i: (0, i, 0))] + [spec] * 3, out_specs=[spec] * 4,
        compiler_params=_params(),
    )(parts, w, m, v)


def _finish_vectors(gathered, lb, params, loss_parts):
    names = list(params)
    n = len(names)

    def body(*refs):
        g_refs, lb_ref, loss_ref = refs[:n], refs[n], refs[n + 1]
        wmv_refs = refs[n + 2:4 * n + 2]
        out_refs, loss_out = refs[4 * n + 2:-1], refs[-1]
        sq = loss_ref[0]
        for dev in range(1, N_DEV):
            sq = sq + loss_ref[dev]
        sq = jnp.sum(jnp.sum(sq, axis=0, keepdims=True), axis=1, keepdims=True)
        loss_out[...] = sq * (0.5 / D_MODEL)
        me = 4 * lax.axis_index("x") + 2 * lax.axis_index("y") + lax.axis_index("c")
        for i, name in enumerate(names):
            g_ref = g_refs[i]
            w_ref, m_ref, v_ref = wmv_refs[3 * i:3 * i + 3]
            if name == "meta_tokens":
                width = w_ref.shape[1]
                mine = pl.ds(pl.multiple_of(me * width, width), width)
                g = g_ref[0, :, mine]
                for dev in range(1, N_DEV):
                    g = g + g_ref[dev, :, mine]
            else:
                g = g_ref[0]
                for dev in range(1, N_DEV):
                    g = g + g_ref[dev]
                g = jnp.sum(g, axis=0, keepdims=True)
                if name == "hg_norm_g":
                    g = functools.reduce(jnp.add, [g[:, h * HEAD_DIM:(h + 1) * HEAD_DIM] for h in range(HEADS)])
                if name == "lb_logits":
                    lbv = lb_ref[...]
                    g = g * lbv * (1.0 - lbv)
                    g = jnp.concatenate([g, -g], axis=0)
            outs = (g, *_adamw_math(w_ref[...], g, m_ref[...], v_ref[...]))
            for ref, val in zip(out_refs[4 * i:4 * i + 4], outs, strict=True):
                ref[...] = val

    args = [gathered[k] for k in names] + [lb, loss_parts] + [t for k in names for t in params[k]]
    res = pl.pallas_call(
        body, name="finish_vectors",
        out_shape=[jax.ShapeDtypeStruct(params[k][0].shape, F32) for k in names for _ in range(4)]
        + [jax.ShapeDtypeStruct((1, 1), F32)],
        compiler_params=_params(),
    )(*args)
    return {k: res[4 * i:4 * i + 4] for i, k in enumerate(names)}, res[-1].reshape(())


def _swap_halves(t):
    half = t.shape[-1] // 2
    return jnp.concatenate([t[..., half:], t[..., :half]], axis=-1)


def _pad_last(t, width):
    return jnp.concatenate([t, jnp.zeros(t.shape[:-1] + (width - t.shape[-1],), t.dtype)], axis=-1)


def _rope_tables(lp):
    pos = jnp.arange(lp, dtype=F32)
    inv_freq = 1.0 / (ROPE_THETA ** (jnp.arange(0, ROPE_DIM, 2, dtype=F32) / ROPE_DIM))
    ang = pos[:, None] * inv_freq[None, :]
    cos, sin = jnp.cos(ang), jnp.sin(ang)
    c128 = _pad_last(jnp.concatenate([cos, cos], axis=1), HEAD_DIM)
    s128 = _pad_last(jnp.concatenate([-sin, sin], axis=1), HEAD_DIM)
    return c128, s128


def _forward_backward(x, target, meta, w, small, *, lp, comm=None):
    batch, seq, d = x.shape
    rows = batch * lp
    tr = 272 if lp % 272 == 0 else 128
    tm = lp // 2
    bf = _MXU_DTYPE
    rw = functools.partial(_rowwise, rows=rows, tr=tr, lp=lp)

    c128, s128 = _rope_tables(lp)
    t_idx = jnp.arange(lp)
    real = jnp.broadcast_to(((t_idx >= N_META) & (t_idx < N_META + seq)).astype(F32)[:, None], (lp, _LANES))

    lb_logits = small["lb_logits"]
    lb = jax.nn.softmax(lb_logits, axis=0)[0:1]
    gh = jnp.tile(small["hg_norm_g"], (1, HEADS))

    h0 = _assemble("assemble_x", x, meta, lp)
    tgt = _assemble("assemble_target", target, jnp.zeros_like(meta), lp)

    (u1,), _ = rw("norm_mix_pre", lambda h, g: ([h * _rms_scale(h) * g], []),
                  ins=[("row", h0, d, 0), ("const", small["mix_pre_g"])], outs=[(d, bf)])
    p = _matmul("proj_in", u1, w["w_in"], out_dtype=F32, tm=tm, tn=1024, tk=1024)

    if comm is None:
        o_hg, z_a, states = _hgrn_fwd(p, lb, gh, batch=batch, lp=lp)
    else:
        o_hg, z_a, states, *gathered = _hgrn_fwd(p, lb, gh, batch=batch, lp=lp, ride=_Ride(comm.rest_payloads, True))
        w = {**w, **comm.rest_weights(gathered)}
    received = []
    scatter = lambda names: _Ride(comm.grad_parts(names, grads), False) if comm is not None else None
    y_a = _matmul("proj_hg_o", z_a, w["w_hg_o"], out_dtype=F32, tm=tm, tn=1024, tk=1024)

    def mla_pre(pc, gq, gkv, ct, st):
        cq, ckv = pc[:, 0:Q_LORA], pc[:, Q_LORA:Q_LORA + KV_LORA]
        kpe, kpe_sw = pc[:, 512:640], pc[:, 640:768]
        return [cq * _rms_scale(cq) * gq, ckv * _rms_scale(ckv) * gkv, kpe * ct + kpe_sw * st], []

    (cqn, ckvn, kp), _ = rw("mla_pre", mla_pre,
                            ins=[("row", p, 1024, CB_C), ("const", small["q_a_norm_g"]),
                                 ("const", small["kv_a_norm_g"]), ("pos", c128), ("pos", s128)],
                            outs=[(Q_LORA, bf), (KV_LORA, bf), (HEAD_DIM, bf)])
    q_cat = _proj_q_rope(cqn, w["w_q"], c128, s128, tm=tm, lp=lp)
    kv = _matmul("proj_kv_b", ckvn, w["w_kv"], out_dtype=bf, tm=tm, tn=1024, tk=KV_LORA)
    o_at, lse = _attn_fwd(q_cat, kv, kp, batch=batch, lp=lp)
    te = lp // 4

    def merge(yb, pa, pb, ya, bg):
        ga, gb = _sigmoid(pa + bg[:, :d]), _sigmoid(pb + bg[:, d:])
        return [yb, ga * ya + gb * yb], []

    (y_b, mix), _ = _matmul_segments(
        "proj_mla_o", [o_at], w["w_mla_o"], tm=te, tn=d, tk=1024,
        epilogue=_Epilogue(merge, rows=[(p, 1024, CB_GA), (p, 1024, CB_GB), y_a], consts=[small["b_gate"]],
                           outs=[(d, F32), (d, bf)], lp=lp))
    def post_mix(mx_, h, g2, g3):
        h1_ = h + mx_ * _rms_scale(mx_) * g2
        return [mx_, h1_, h1_ * _rms_scale(h1_) * g3], []

    (mixed, h1, u2), _ = _matmul_segments(
        "proj_out", [mix], w["w_out"], tm=te, tn=d, tk=1024,
        epilogue=_Epilogue(post_mix, rows=[h0], consts=[small["mix_post_g"], small["ffn_pre_g"]],
                           outs=[(d, F32), (d, F32), (d, bf)], lp=lp))
    act, gt, up = _ffn_in_swiglu(u2, w["w_ffn_in"], tm=tm, tn=1408)

    def post_ffn(fo_, h1_, t_, g4, mask):
        r = _rms_scale(fo_)
        h2 = h1_ + fo_ * r * g4
        err = (h2 - t_) * mask[:, 0:1]
        dh2 = err * (1.0 / d)
        dfo, dg4 = _rms_bwd(fo_, g4, dh2)
        return [dh2, dfo], [err * err, dg4]

    (dh2, dfo), (loss_vec, dg_ffn_post) = _matmul_segments(
        "ffn_out", [act], w["w_ffn_out"], tm=te, tn=d, tk=1408,
        epilogue=_Epilogue(post_ffn, rows=[h1, tgt], consts=[small["ffn_post_g"]], pos=[real],
                           outs=[(d, F32), (d, bf)], accs=[d, d], lp=lp))
    loss = (0.5 / d) * jnp.sum(loss_vec)

    grads = {}
    dw_dt = F32 if comm is None else _WIRE_DTYPE
    dgt, dup = _d_ffn_out_swiglu(dfo, w["w_ffn_out"], gt, up, tm=tm, tn=1408)
    grads["w_ffn_out"] = _matmul_tn("dw_ffn_out", act, dfo, tk=1408, tn=1024, tr=tm, out_dtype=dw_dt)
    grads["w_ffn_in"] = jnp.concatenate([_matmul_tn("dw_ffn_in_gate", u2, dgt, tk=1024, tn=1408, tr=tm),
                                         _matmul_tn("dw_ffn_in_up", u2, dup, tk=1024, tn=1408, tr=tm)], axis=1)

    def post_mix_bwd(du2_, h1_, dh2_, mx_, g3, g2):
        dx, dg3 = _rms_bwd(h1_, g3, du2_)
        dh1_ = dh2_ + dx
        dmx, dg2 = _rms_bwd(mx_, g2, dh1_)
        return [dh1_, dmx], [dg3, dg2]

    (dh1, dmixed), (dg_ffn_pre, dg_mix_post) = _matmul_segments(
        "d_ffn_in", [dgt, dup], w["w_ffn_in"], tm=te, tn=d, tk=1408, b_transposed=True,
        epilogue=_Epilogue(post_mix_bwd, rows=[h1, dh2, mixed], consts=[small["ffn_pre_g"], small["mix_post_g"]],
                           outs=[(d, F32), (d, bf)], accs=[d, d], lp=lp))
    grads["w_out"] = _matmul_tn("dw_out", mix, dmixed, tk=1024, tn=1024, tr=tm, out_dtype=dw_dt)

    def merge_bwd(dm, pa, pb, ya, yb, bg):
        ga, gb = _sigmoid(pa + bg[:, :d]), _sigmoid(pb + bg[:, d:])
        dpg = jnp.concatenate([dm * ya * ga * (1.0 - ga), dm * yb * gb * (1.0 - gb)], axis=1)
        return [dpg, dm * ga, dm * gb], [dpg]

    (dpg, dya, dyb), (db_gate,) = _matmul_segments(
        "d_proj_out", [dmixed], w["w_out"], tm=te, tn=d, tk=1024, b_transposed=True,
        epilogue=_Epilogue(merge_bwd, rows=[(p, 1024, CB_GA), (p, 1024, CB_GB), y_a, y_b], consts=[small["b_gate"]],
                           outs=[(2 * d, bf), (d, bf), (d, bf)], accs=[2 * d], lp=lp))
    dz_a = _matmul("d_proj_hg_o", dya, w["w_hg_o"], out_dtype=F32, tm=tm, tn=1024, tk=1024, b_transposed=True)
    grads["w_hg_o"] = _matmul_tn("dw_hg_o", z_a, dya, tk=1024, tn=1024, tr=tm, out_dtype=dw_dt)
    do_at = _matmul("d_proj_mla_o", dyb, w["w_mla_o"], out_dtype=bf, tm=tm, tn=1024, tk=1024, b_transposed=True)
    grads["w_mla_o"] = _matmul_tn("dw_mla_o", o_at, dyb, tk=1024, tn=1024, tr=tm, out_dtype=dw_dt)

    dph, dlb, dgh, *got = _hgrn_bwd(p, o_hg, dz_a, states, lb, gh, batch=batch, lp=lp,
                                    ride=scatter(_GRAD_GROUPS[0]))
    received.append(got)

    res = _attn_bwd(q_cat, kv, kp, do_at, o_at, lse, c128, s128, batch=batch, lp=lp, ride=scatter(_GRAD_GROUPS[1]))
    dq_parts, (dkn, dkp, dvv) = list(res[:3]), res[3:6]
    received.append(list(res[6:]))
    dcqn = _matmul_segments("d_proj_q_b", dq_parts, w["w_q"], tm=tm, tn=Q_LORA, tk=1024, b_transposed=True)
    grads["w_q"] = jnp.concatenate([_matmul_tn(f"dw_q_b_{i}", cqn, part, tk=Q_LORA, tn=1024, tr=tm)
                                    for i, part in enumerate(dq_parts)], axis=1)
    dckvn = _matmul_segments("d_proj_kv_b", [dkn, dvv], w["w_kv"], tm=tm, tn=KV_LORA, tk=1024, b_transposed=True)
    grads["w_k"] = _matmul_tn("dw_k_b", ckvn, dkn, tk=KV_LORA, tn=1024, tr=tm)
    grads["w_v"] = _matmul_tn("dw_v_b", ckvn, dvv, tk=KV_LORA, tn=1024, tr=tm)

    def mla_pre_bwd(pc, dq_, dkv_, dkp_, gq, gkv, ct, st):
        cq, ckv = pc[:, 0:Q_LORA], pc[:, Q_LORA:Q_LORA + KV_LORA]
        dcq, dgq = _rms_bwd(cq, gq, dq_)
        dckv, dgkv = _rms_bwd(ckv, gkv, dkv_)
        dpc = jnp.concatenate([dcq, dckv, dkp_ * ct, dkp_ * st, jnp.zeros((pc.shape[0], 256), F32)], axis=1)
        return [dpc], [dgq, dgkv]

    (dpc,), (dg_q, dg_kv) = rw(
        "mla_pre_bwd", mla_pre_bwd,
        ins=[("row", p, 1024, CB_C), ("row", dcqn, Q_LORA, 0), ("row", dckvn, KV_LORA, 0), ("row", dkp, HEAD_DIM, 0),
             ("const", small["q_a_norm_g"]), ("const", small["kv_a_norm_g"]), ("pos", c128), ("pos", s128)],
        outs=[(1024, bf)], accs=[Q_LORA, KV_LORA])

    grads["w_in"] = (_matmul_tn("dw_in_h", u1, dph, tk=1024, tn=1024, tr=tm),
                     _matmul_tn("dw_in_c", u1, dpc, tk=1024, tn=1024, tr=tm),
                     _matmul_tn("dw_in_g", u1, dpg, tk=1024, tn=1024, tr=tm))
    def pre_bwd(du, h, dh, g1):
        dx, dg1 = _rms_bwd(h, g1, du)
        return [dh + dx], [dg1]

    (dh0,), (dg_mix_pre,), *got = _matmul_segments(
        "d_proj_in", [dph, dpc, dpg], w["w_in"], tm=te, tn=d, tk=1024, b_transposed=True,
        ride=scatter(_GRAD_GROUPS[2]),
        epilogue=_Epilogue(pre_bwd, rows=[h0, dh1], consts=[small["mix_pre_g"]], outs=[(d, F32)], accs=[d], lp=lp))
    if comm is not None:
        received.append(got)
    grad_x = dh0.reshape(batch, lp, d)[:, N_META:N_META + seq]
    partial = {"meta_tokens": _meta_grad(dh0, batch, lp), "lb_logits": dlb, "b_gate": db_gate, "hg_norm_g": dgh,
               "q_a_norm_g": dg_q, "kv_a_norm_g": dg_kv, "mix_pre_g": dg_mix_pre, "mix_post_g": dg_mix_post,
               "ffn_pre_g": dg_ffn_pre, "ffn_post_g": dg_ffn_post, "loss": loss_vec}
    return loss, grad_x, grads, partial, lb, received


_BIG = ["w_in", "w_hg_o", "w_q_b", "w_kv_b", "w_mla_o", "w_out", "w_ffn_in", "w_ffn_out"]
_COLUMN_SHARDED = {"w_in", "w_q_b", "w_kv_b", "w_ffn_in"}
_GRAD_GROUPS = [["w_ffn_in", "w_ffn_out"], ["w_out", "w_hg_o", "w_mla_o"], ["w_in", "w_q_b", "w_kv_b"]]
_SMALL = ["b_gate", "lb_logits", "hg_norm_g", "q_a_norm_g", "kv_a_norm_g", "mix_pre_g", "mix_post_g",
          "ffn_pre_g", "ffn_post_g"]


def _gathered_matrix(name, t):
    _, k, n = t.shape
    if name in _COLUMN_SHARDED:
        return t.transpose(1, 0, 2).reshape(k, N_DEV * n)
    return t.reshape(N_DEV * k, n)


def _scatter_layout(name, full):
    kk, nn = full.shape
    if name in _COLUMN_SHARDED:
        t = full.reshape(kk, N_DEV, nn // N_DEV).transpose(1, 0, 2)
    else:
        t = full.reshape(N_DEV, kk // N_DEV, nn)
    return t.astype(_WIRE_DTYPE)


def _model_w_in(wi):
    z = lambda *s: jnp.zeros(s, wi.dtype)
    kpe = wi[:, 4608:4672]
    c_blk = jnp.concatenate([wi[:, 4096:4608], kpe, z(1024, 64), _swap_halves(kpe), z(1024, 64), z(1024, 256)], axis=1)
    return {"w_in": jnp.concatenate([wi[:, :4096], c_blk, wi[:, 4672:]], axis=1).astype(_MXU_DTYPE)}


def _model_weights(full):
    return {**_model_w_in(full["w_in"]), **_model_rest(full)}


def _model_rest(full):
    wq3 = full["w_q_b"].reshape(Q_LORA, HEADS, HEAD_DIM + ROPE_DIM)
    pe = wq3[:, :, HEAD_DIM:]
    w_q = jnp.concatenate([wq3[:, :, :HEAD_DIM].reshape(Q_LORA, -1),
                           _pad_last(pe, HEAD_DIM).reshape(Q_LORA, -1),
                           _pad_last(_swap_halves(pe), HEAD_DIM).reshape(Q_LORA, -1)], axis=1)
    wkv3 = full["w_kv_b"].reshape(KV_LORA, HEADS, 2 * HEAD_DIM)
    w_k = wkv3[:, :, :HEAD_DIM].reshape(KV_LORA, -1)
    w_v = wkv3[:, :, HEAD_DIM:].reshape(KV_LORA, -1)
    w = {"w_q": w_q, "w_kv": jnp.concatenate([w_k, w_v], axis=1),
         "w_hg_o": full["w_hg_o"], "w_mla_o": full["w_mla_o"], "w_out": full["w_out"],
         "w_ffn_in": full["w_ffn_in"], "w_ffn_out": full["w_ffn_out"]}
    return {k: v.astype(_MXU_DTYPE) for k, v in w.items()}


def _reference_layout_grad(name, g):
    if name == "w_in":
        g_h, g_c, g_g = g["w_in"]
        d_kpe = g_c[:, 512:576] + _swap_halves(g_c[:, 640:704])
        return jnp.concatenate([g_h, g_c[:, :512], d_kpe, g_g], axis=1)
    if name == "w_q_b":
        gq = g["w_q"]
        d_pe = (gq[:, 1024:2048].reshape(Q_LORA, HEADS, HEAD_DIM)[:, :, :ROPE_DIM]
                + _swap_halves(gq[:, 2048:].reshape(Q_LORA, HEADS, HEAD_DIM)[:, :, :ROPE_DIM]))
        return jnp.concatenate([gq[:, :1024].reshape(Q_LORA, HEADS, HEAD_DIM), d_pe], axis=2).reshape(Q_LORA, -1)
    if name == "w_kv_b":
        return jnp.concatenate([g["w_k"].reshape(KV_LORA, HEADS, HEAD_DIM),
                                g["w_v"].reshape(KV_LORA, HEADS, HEAD_DIM)], axis=2).reshape(KV_LORA, -1)
    return g[name]


def _reference_layout_grads(g):
    return {n: _reference_layout_grad(n, g) for n in _BIG}


class _Comm:
    def __init__(self, shard):
        self.rest_payloads = [shard[n].astype(_WIRE_DTYPE) for n in _BIG[1:]]

    def rest_weights(self, gathered):
        return _model_rest({n: _gathered_matrix(n, t) for n, t in zip(_BIG[1:], gathered, strict=True)})

    def grad_parts(self, names, g):
        return [_scatter_layout(n, _reference_layout_grad(n, g)) for n in names]


def kernel(x, meta_tokens, w_in, b_gate, lb_logits, hg_norm_g, w_hg_o, q_a_norm_g, w_q_b, kv_a_norm_g, w_kv_b, w_mla_o, w_out, mix_pre_g, mix_post_g, ffn_pre_g, ffn_post_g, w_ffn_in, w_ffn_out, loss_target, m_meta_tokens, m_w_in, m_b_gate, m_lb_logits, m_hg_norm_g, m_w_hg_o, m_q_a_norm_g, m_w_q_b, m_kv_a_norm_g, m_w_kv_b, m_w_mla_o, m_w_out, m_mix_pre_g, m_mix_post_g, m_ffn_pre_g, m_ffn_post_g, m_w_ffn_in, m_w_ffn_out, v_meta_tokens, v_w_in, v_b_gate, v_lb_logits, v_hg_norm_g, v_w_hg_o, v_q_a_norm_g, v_w_q_b, v_kv_a_norm_g, v_w_kv_b, v_w_mla_o, v_w_out, v_mix_pre_g, v_mix_post_g, v_ffn_pre_g, v_ffn_post_g, v_w_ffn_in, v_w_ffn_out):
    args = dict(locals())
    batch, seq, d = x.shape
    lp = -(-(N_META + seq) // _LANES) * _LANES
    weight_names = ["meta_tokens", "w_in", "b_gate", "lb_logits", "hg_norm_g", "w_hg_o", "q_a_norm_g", "w_q_b",
                    "kv_a_norm_g", "w_kv_b", "w_mla_o", "w_out", "mix_pre_g", "mix_post_g", "ffn_pre_g",
                    "ffn_post_g", "w_ffn_in", "w_ffn_out"]
    shard = {n: args[n].reshape(args[n].shape[-2:]) for n in _BIG}
    comm = _Comm(shard)

    w_in_all, meta_all = _all_gather("gather_first", [shard["w_in"].astype(_WIRE_DTYPE), meta_tokens])
    w_first = _model_w_in(_gathered_matrix("w_in", w_in_all))
    meta_full = meta_all.transpose(1, 0, 2).reshape(N_META, d)
    small = {n: args[n] for n in _SMALL}

    _, grad_x, _, partial, lb, received = _forward_backward(x, loss_target, meta_full, w_first, small, lp=lp, comm=comm)
    out = {}
    for names, bufs in zip(_GRAD_GROUPS, received, strict=True):
        for n, buf in zip(names, bufs, strict=True):
            two_d = lambda t: t.reshape(t.shape[-2:])
            res = _sum_adamw("adamw_" + n, buf, shard[n], two_d(args["m_" + n]), two_d(args["v_" + n]))
            out[n] = [t.reshape(args[n].shape) for t in res]

    vec_names = _SMALL + ["meta_tokens"]
    *gathered, loss_parts = _all_gather("gather_vectors", [partial[n] for n in vec_names + ["loss"]])
    finished, loss = _finish_vectors(dict(zip(vec_names, gathered, strict=True)), lb,
                                     {n: (args[n], args["m_" + n], args["v_" + n]) for n in vec_names}, loss_parts)
    out.update(finished)
    return (loss, grad_x, *[out[n][i] for i in range(4) for n in weight_names])
```

```python
import functools

import jax
import jax.numpy as jnp
from jax import lax
from jax.experimental import pallas as pl
from jax.experimental.pallas import tpu as pltpu

F32 = jnp.float32
_MXU_DTYPE = jnp.bfloat16
_WIRE_DTYPE = jnp.bfloat16
_VMEM_LIMIT_BYTES = 56 * 1024 * 1024
_LANES = 128
_SUBLANES = 8

N_DEV = 8
N_META = 16
NORM_EPS = 1e-6
HEADS = 8
HEAD_DIM = 128
ROPE_DIM = 64
HG_CHUNK = 16
HG_BLOCK = 128
ROPE_THETA = 10000.0
D_MODEL = 1024
Q_LORA = 256
KV_LORA = 256
FFN_HIDDEN = 2816
ATTN_SCALE = (HEAD_DIM + ROPE_DIM) ** -0.5
NEG_BIG = -1e30

ADAM_LR = 0.001
ADAM_B1 = 0.9
ADAM_B2 = 0.999
ADAM_EPS = 1e-08
ADAM_WD = 0.01
ADAM_STEP = 10

CB_HQ, CB_HF, CB_HI, CB_HG, CB_C, CB_GA, CB_GB = range(7)
IN_COLS_PADDED = 7 * 1024


def _params(**kw):
    return pltpu.CompilerParams(vmem_limit_bytes=_VMEM_LIMIT_BYTES, **kw)


def _dot(a, b):
    return lax.dot_general(a, b, (((1,), (0,)), ((), ())), preferred_element_type=F32)


def _dot_nt(a, b):
    return lax.dot_general(a, b, (((1,), (1,)), ((), ())), preferred_element_type=F32)


def _dot_tn(a, b):
    return lax.dot_general(a, b, (((0,), (0,)), ((), ())), preferred_element_type=F32)


def _mx(x):
    return x.astype(_MXU_DTYPE)


def _exact_dot(m01, x, dot=_dot):
    if _MXU_DTYPE == jnp.float32:
        return dot(m01.astype(F32), x)
    m = m01.astype(jnp.bfloat16)
    x1 = x.astype(jnp.bfloat16)
    x2 = (x - x1.astype(F32)).astype(jnp.bfloat16)
    return dot(m, x1) + dot(m, x2)


def _exact_dot_nt(m01, x):
    return _exact_dot(m01, x, dot=_dot_nt)


def _sigmoid(x):
    return jax.nn.sigmoid(x)


def _silu_grad(x, s):
    return s * (1.0 + x * (1.0 - s))


def _rms_scale(x):
    return lax.rsqrt(jnp.mean(x * x, axis=-1, keepdims=True) + NORM_EPS)


def _rms_bwd(x, g, dy):
    r = _rms_scale(x)
    xh = x * r
    w = dy * g
    dx = r * (w - xh * jnp.mean(xh * w, axis=-1, keepdims=True))
    return dx, dy * xh


def _heads(fn, *arrays):
    outs = [fn(*[a[:, h * HEAD_DIM:(h + 1) * HEAD_DIM] for a in arrays]) for h in range(HEADS)]
    if isinstance(outs[0], tuple):
        return tuple(jnp.concatenate([o[i] for o in outs], axis=1) for i in range(len(outs[0])))
    return jnp.concatenate(outs, axis=1)


class _Ride:
    def __init__(self, payloads, gather):
        self.gather, self.args, self.n = gather, list(payloads), len(payloads)
        self.in_specs = [pl.BlockSpec(memory_space=pl.ANY)] * self.n
        self.out_shape = [jax.ShapeDtypeStruct((N_DEV, *p.shape[-2:]), p.dtype) for p in payloads]
        self.out_specs = [pl.BlockSpec(memory_space=pl.ANY)] * self.n
        self.scratch = [pltpu.SemaphoreType.DMA((self.n, N_DEV - 1)), pltpu.SemaphoreType.DMA((self.n, N_DEV - 1)),
                        pltpu.SemaphoreType.DMA((self.n,))]

    def split(self, rest, n_outs):
        n = self.n
        mine = (rest[:n], rest[n + n_outs:2 * n + n_outs], rest[-3:])
        return rest[n:n + n_outs] + rest[2 * n + n_outs:-3], mine

    def _copies(self, p_refs, out_refs, sems):
        send_sems, recv_sems, local_sems = sems
        x, y, c = lax.axis_index("x"), lax.axis_index("y"), lax.axis_index("c")
        me = 4 * x + 2 * y + c
        copies = []
        for i, (p_ref, out_ref) in enumerate(zip(p_refs, out_refs, strict=True)):
            part = (lambda j, p_ref=p_ref: p_ref) if self.gather else (lambda j, p_ref=p_ref: p_ref.at[j])
            copies.append(pltpu.make_async_copy(part(me), out_ref.at[me], local_sems.at[i]))
            for k in range(1, N_DEV):
                px, py, pc = x ^ (k >> 2), y ^ ((k >> 1) & 1), c ^ (k & 1)
                copies.append(pltpu.make_async_remote_copy(
                    src_ref=part(4 * px + 2 * py + pc), dst_ref=out_ref.at[me],
                    send_sem=send_sems.at[i, k - 1], recv_sem=recv_sems.at[i, k - 1],
                    device_id=(px, py, pc), device_id_type=pl.DeviceIdType.MESH))
        return copies

    def run(self, grid, refs):
        ids = [pl.program_id(i) for i in range(len(grid))]
        first = functools.reduce(jnp.logical_and, [i == 0 for i in ids])
        last = functools.reduce(jnp.logical_and, [i == g - 1 for i, g in zip(ids, grid)])

        @pl.when(first)
        def _():
            for cp in self._copies(*refs):
                cp.start()

        @pl.when(last)
        def _():
            for cp in self._copies(*refs):
                cp.wait()


class _NoRide:
    in_specs, out_shape, out_specs, scratch, args = [], [], [], [], []


def _matmul(name, a, b, *, out_dtype, tm, tn, tk, c_in=None, ride=None, b_transposed=False):
    m, k = a.shape
    n = b.shape[0] if b_transposed else b.shape[1]
    assert m % tm == 0 and n % tn == 0 and k % tk == 0, (name, a.shape, b.shape, tm, tn, tk)
    nk = k // tk
    has_c = c_in is not None
    dot = _dot_nt if b_transposed else _dot
    grid = (n // tn, m // tm, nk)
    n_in = 2 + has_c

    def body(*refs):
        a_ref, b_ref = refs[0], refs[1]
        c_ref = refs[2] if has_c else None
        rest = refs[n_in:]
        if ride is not None:
            rest, exchange = ride.split(rest, 1)
            ride.run(grid, exchange)
        o_ref = rest[0]
        acc_ref = rest[1] if nk > 1 else None

        def finish(r):
            if has_c:
                r = r + c_ref[...]
            o_ref[...] = r.astype(o_ref.dtype)

        if nk == 1:
            finish(dot(a_ref[...], b_ref[...]))
        else:
            kk = pl.program_id(2)

            @pl.when(kk == 0)
            def _():
                acc_ref[...] = jnp.zeros_like(acc_ref)

            acc_ref[...] += dot(a_ref[...], b_ref[...])

            @pl.when(kk == nk - 1)
            def _():
                finish(acc_ref[...])

    in_specs = [pl.BlockSpec((tm, tk), lambda j, i, kk: (i, kk)),
                pl.BlockSpec((tn, tk), lambda j, i, kk: (j, kk)) if b_transposed
                else pl.BlockSpec((tk, tn), lambda j, i, kk: (kk, j))]
    args = [a, b]
    aliases = {}
    if has_c:
        in_specs.append(pl.BlockSpec((tm, tn), lambda j, i, kk: (i, j)))
        args.append(c_in)
        aliases = {2: 0}
    out_shape = [jax.ShapeDtypeStruct((m, n), out_dtype)]
    out_specs = [pl.BlockSpec((tm, tn), lambda j, i, kk: (i, j))]
    scratch = [pltpu.VMEM((tm, tn), F32)] if nk > 1 else []
    if ride is not None:
        in_specs, args = in_specs + ride.in_specs, args + ride.args
        out_shape, out_specs, scratch = out_shape + ride.out_shape, out_specs + ride.out_specs, scratch + ride.scratch
    res = pl.pallas_call(
        body, name=name, out_shape=out_shape, grid=grid, in_specs=in_specs, out_specs=out_specs,
        scratch_shapes=scratch, input_output_aliases=aliases, compiler_params=_params(),
    )(*args)
    return res[0] if ride is None else res


class _Epilogue:
    def __init__(self, fn, *, rows=(), consts=(), pos=(), outs=(), accs=(), lp=None):
        self.fn, self.rows, self.consts, self.pos = fn, list(rows), list(consts), list(pos)
        self.outs, self.accs, self.lp = list(outs), list(accs), lp


def _matmul_segments(name, a_list, b, *, out_dtype=F32, tm, tn, tk, ride=None, b_transposed=False, epilogue=None):
    m = a_list[0].shape[0]
    n, k = b.shape if b_transposed else b.shape[::-1]
    steps = [a.shape[1] // tk for a in a_list]
    offs = [sum(steps[:s]) for s in range(len(steps))]
    nk = sum(steps)
    assert nk * tk == k and m % tm == 0 and n % tn == 0 and all(a.shape[1] % tk == 0 for a in a_list), name
    grid = (n // tn, m // tm, nk)
    n_seg = len(a_list)
    dot = _dot_nt if b_transposed else _dot
    ep = epilogue
    assert ep is None or tn == n, name
    n_extra = 0 if ep is None else len(ep.rows) + len(ep.consts) + len(ep.pos)
    n_outs = 1 if ep is None else len(ep.outs) + len(ep.accs)

    def body(*refs):
        a_refs, b_ref = refs[:n_seg], refs[n_seg]
        extra_refs, rest = refs[n_seg + 1:n_seg + 1 + n_extra], refs[n_seg + 1 + n_extra:]
        if ride is not None:
            rest, exchange = ride.split(rest, n_outs)
            ride.run(grid, exchange)
        out_refs, acc_ref = rest[:n_outs], rest[n_outs]
        i, kk = pl.program_id(1), pl.program_id(2)

        @pl.when(kk == 0)
        def _():
            acc_ref[...] = jnp.zeros_like(acc_ref)

        for s in range(n_seg):
            @pl.when((kk >= offs[s]) & (kk < offs[s] + steps[s]))
            def _(s=s):
                acc_ref[...] += dot(a_refs[s][...], b_ref[...])

        if ep is None:
            @pl.when(kk == nk - 1)
            def _():
                out_refs[0][...] = acc_ref[...].astype(out_refs[0].dtype)
        else:
            sum_refs = out_refs[len(ep.outs):]

            @pl.when((kk == 0) & (i == 0))
            def _():
                for ref in sum_refs:
                    ref[...] = jnp.zeros_like(ref)

            @pl.when(kk == nk - 1)
            def _():
                res_outs, res_sums = ep.fn(acc_ref[...], *[r[...] for r in extra_refs])
                for ref, val in zip(out_refs[:len(ep.outs)], res_outs, strict=True):
                    ref[...] = val.astype(ref.dtype)
                for ref, val in zip(sum_refs, res_sums, strict=True):
                    ref[...] += val.reshape(tm // _SUBLANES, _SUBLANES, val.shape[-1]).sum(axis=0)

    seg_spec = lambda s: pl.BlockSpec(
        (tm, tk), functools.partial(lambda j, i, kk, off, ns: (i, jnp.clip(kk - off, 0, ns - 1)), off=offs[s], ns=steps[s]))
    b_spec = (pl.BlockSpec((tn, tk), lambda j, i, kk: (j, kk)) if b_transposed
              else pl.BlockSpec((tk, tn), lambda j, i, kk: (kk, j)))
    in_specs = [seg_spec(s) for s in range(n_seg)] + [b_spec]
    args = list(a_list) + [b]
    row_spec = lambda w: pl.BlockSpec((tm, w), lambda j, i, kk: (i, 0))
    if ep is None:
        out_shape = [jax.ShapeDtypeStruct((m, n), out_dtype)]
        out_specs = [pl.BlockSpec((tm, tn), lambda j, i, kk: (i, j))]
    else:
        tiles_per_example = ep.lp // tm
        row_ins = [r if isinstance(r, tuple) else (r, r.shape[1], 0) for r in ep.rows]
        in_specs += ([pl.BlockSpec((tm, wd), functools.partial(lambda j, i, kk, cb: (i, cb), cb=cb)) for _, wd, cb in row_ins]
                     + [pl.BlockSpec(c.shape, lambda j, i, kk: (0, 0)) for c in ep.consts]
                     + [pl.BlockSpec((tm, p.shape[1]), lambda j, i, kk: (i % tiles_per_example, 0)) for p in ep.pos])
        args += [arr for arr, _, _ in row_ins] + ep.consts + ep.pos
        out_shape = ([jax.ShapeDtypeStruct((m, w), dt) for w, dt in ep.outs]
                     + [jax.ShapeDtypeStruct((_SUBLANES, w), F32) for w in ep.accs])
        out_specs = ([row_spec(w) for w, _ in ep.outs]
                     + [pl.BlockSpec((_SUBLANES, w), lambda j, i, kk: (0, 0)) for w in ep.accs])
    scratch = [pltpu.VMEM((tm, tn), F32)]
    if ride is not None:
        in_specs, args = in_specs + ride.in_specs, args + ride.args
        out_shape, out_specs, scratch = out_shape + ride.out_shape, out_specs + ride.out_specs, scratch + ride.scratch
    res = pl.pallas_call(
        body, name=name, out_shape=out_shape, grid=grid, in_specs=in_specs, out_specs=out_specs,
        scratch_shapes=scratch, compiler_params=_params(),
    )(*args)
    if ep is None:
        return res[0] if ride is None else res
    n_o = len(ep.outs)
    return (res[:n_o], res[n_o:n_outs], *res[n_outs:])


EPILOGUE_SLICES = 4


def _matmul_epilogue(name, a_list, b, epilogue, *, tm, tk, ride=None, b_transposed=False):
    ep = epilogue
    m = a_list[0].shape[0]
    n, k = b.shape if b_transposed else b.shape[::-1]
    steps = [a.shape[1] // tk for a in a_list]
    offs = [sum(steps[:s]) for s in range(len(steps))]
    nk = sum(steps)
    n_tiles = m // tm
    n_sl = min(nk, EPILOGUE_SLICES)
    rs = tm // n_sl
    assert nk * tk == k and m % tm == 0 and rs * n_sl == tm and rs % 16 == 0 and not ep.pos, name
    grid = (n_tiles + 1, nk)
    n_seg = len(a_list)
    dot = _dot_nt if b_transposed else _dot
    row_ins = [r if isinstance(r, tuple) else (r, r.shape[1], 0) for r in ep.rows]
    n_rows, n_consts = len(row_ins), len(ep.consts)
    n_outs = len(ep.outs) + len(ep.accs)

    def body(*refs):
        a_refs, b_ref = refs[:n_seg], refs[n_seg]
        row_refs = refs[n_seg + 1:n_seg + 1 + n_rows]
        const_refs = refs[n_seg + 1 + n_rows:n_seg + 1 + n_rows + n_consts]
        rest = refs[n_seg + 1 + n_rows + n_consts:]
        if ride is not None:
            rest, exchange = ride.split(rest, n_outs)
            ride.run(grid, exchange)
        out_refs, sum_refs, acc_ref = rest[:len(ep.outs)], rest[len(ep.outs):n_outs], rest[n_outs]
        i, kk = pl.program_id(0), pl.program_id(1)
        cur = i % 2

        @pl.when((i == 0) & (kk == 0))
        def _():
            acc_ref[...] = jnp.zeros_like(acc_ref)
            for ref in sum_refs:
                ref[...] = jnp.zeros_like(ref)

        def accumulate():
            a = a_refs[n_seg - 1][...]
            for s in reversed(range(n_seg - 1)):
                a = jnp.where(kk < offs[s + 1], a_refs[s][...], a)
            part = dot(a, b_ref[...])
            acc_ref[cur] = jnp.where(kk == 0, part, acc_ref[cur] + part)

        def finish_slice():
            sl = pl.ds(pl.multiple_of(kk * rs, rs), rs)
            res_outs, res_sums = ep.fn(acc_ref[1 - cur, sl, :], *[r[sl, :] for r in row_refs],
                                       *[c[...] for c in const_refs])
            for ref, val in zip(out_refs, res_outs, strict=True):
                ref[sl, :] = val.astype(ref.dtype)
            for ref, val in zip(sum_refs, res_sums, strict=True):
                ref[...] += jnp.where(i > 0, val.reshape(rs // _SUBLANES, _SUBLANES, val.shape[-1]).sum(axis=0), 0.0)

        @pl.when(kk < n_sl)
        def _():
            accumulate()
            finish_slice()

        if nk > n_sl:
            @pl.when(kk >= n_sl)
            def _():
                accumulate()

    this_tile = lambda i: jnp.minimum(i, n_tiles - 1)
    last_tile = lambda i: jnp.maximum(i - 1, 0)
    seg_spec = lambda s: pl.BlockSpec(
        (tm, tk), functools.partial(lambda i, kk, off, ns: (this_tile(i), jnp.clip(kk - off, 0, ns - 1)),
                                    off=offs[s], ns=steps[s]))
    b_spec = (pl.BlockSpec((n, tk), lambda i, kk: (0, kk)) if b_transposed
              else pl.BlockSpec((tk, n), lambda i, kk: (kk, 0)))
    in_specs = ([seg_spec(s) for s in range(n_seg)] + [b_spec]
                + [pl.BlockSpec((tm, wd), functools.partial(lambda i, kk, cb: (last_tile(i), cb), cb=cb))
                   for _, wd, cb in row_ins]
                + [pl.BlockSpec(c.shape, lambda i, kk: (0, 0)) for c in ep.consts])
    args = list(a_list) + [b] + [arr for arr, _, _ in row_ins] + ep.consts
    out_shape = ([jax.ShapeDtypeStruct((m, w), dt) for w, dt in ep.outs]
                 + [jax.ShapeDtypeStruct((_SUBLANES, w), F32) for w in ep.accs])
    out_specs = ([pl.BlockSpec((tm, w), lambda i, kk: (last_tile(i), 0)) for w, _ in ep.outs]
                 + [pl.BlockSpec((_SUBLANES, w), lambda i, kk: (0, 0)) for w in ep.accs])
    scratch = [pltpu.VMEM((2, tm, n), F32)]
    if ride is not None:
        in_specs, args = in_specs + ride.in_specs, args + ride.args
        out_shape, out_specs, scratch = out_shape + ride.out_shape, out_specs + ride.out_specs, scratch + ride.scratch
    res = pl.pallas_call(
        body, name=name, out_shape=out_shape, grid=grid, in_specs=in_specs, out_specs=out_specs,
        scratch_shapes=scratch, compiler_params=_params(),
    )(*args)
    n_o = len(ep.outs)
    return (res[:n_o], res[n_o:n_outs], *res[n_outs:])


def _matmul_tn(name, x, dy, *, tk, tn, tr, out_dtype=F32):
    r, k = x.shape
    _, n = dy.shape
    assert r % tr == 0 and k % tk == 0 and n % tn == 0, (name, x.shape, dy.shape)
    n_r = r // tr
    direct = out_dtype == F32

    def body(x_ref, dy_ref, o_ref, *scratch):
        acc_ref = o_ref if direct else scratch[0]

        @pl.when(pl.program_id(2) == 0)
        def _():
            acc_ref[...] = jnp.zeros_like(acc_ref)

        acc_ref[...] += _dot_tn(x_ref[...], dy_ref[...])
        if not direct:
            @pl.when(pl.program_id(2) == n_r - 1)
            def _():
                o_ref[...] = acc_ref[...].astype(o_ref.dtype)

    return pl.pallas_call(
        body, name=name,
        out_shape=jax.ShapeDtypeStruct((k, n), out_dtype),
        grid=(k // tk, n // tn, n_r),
        in_specs=[pl.BlockSpec((tr, tk), lambda kb, nb, rr: (rr, kb)),
                  pl.BlockSpec((tr, tn), lambda kb, nb, rr: (rr, nb))],
        out_specs=pl.BlockSpec((tk, tn), lambda kb, nb, rr: (kb, nb)),
        scratch_shapes=[] if direct else [pltpu.VMEM((tk, tn), F32)],
        compiler_params=_params(),
    )(x, dy)


def _ffn_in_swiglu(u, w, *, tm, tn):
    r, k = u.shape
    h = w.shape[1] // 2
    assert r % tm == 0 and h % tn == 0
    nj = h // tn

    def body(u_ref, wg_ref, wu_ref, act_ref, gt_ref, up_ref):
        uu = u_ref[...]
        gt, up = _dot(uu, wg_ref[...]), _dot(uu, wu_ref[...])
        act_ref[...] = (gt * _sigmoid(gt) * up).astype(act_ref.dtype)
        gt_ref[...] = gt.astype(gt_ref.dtype)
        up_ref[...] = up.astype(up_ref.dtype)

    tile = pl.BlockSpec((tm, tn), lambda j, i: (i, j))
    return pl.pallas_call(
        body, name="ffn_in_swiglu",
        out_shape=[jax.ShapeDtypeStruct((r, h), _MXU_DTYPE)] * 3,
        grid=(nj, r // tm),
        in_specs=[pl.BlockSpec((tm, k), lambda j, i: (i, 0)),
                  pl.BlockSpec((k, tn), lambda j, i: (0, j)),
                  pl.BlockSpec((k, tn), lambda j, i: (0, nj + j))],
        out_specs=[tile] * 3,
        compiler_params=_params(),
    )(u, w, w)


def _d_ffn_out_swiglu(dy, w, gt, up, *, tm, tn):
    r, k = dy.shape
    h = w.shape[0]
    assert r % tm == 0 and h % tn == 0

    def body(dy_ref, w_ref, gt_ref, up_ref, dgt_ref, dup_ref):
        da = _dot_nt(dy_ref[...], w_ref[...])
        g, u_ = gt_ref[...].astype(F32), up_ref[...].astype(F32)
        s = _sigmoid(g)
        dgt_ref[...] = (da * u_ * _silu_grad(g, s)).astype(dgt_ref.dtype)
        dup_ref[...] = (da * g * s).astype(dup_ref.dtype)

    tile = pl.BlockSpec((tm, tn), lambda j, i: (i, j))
    return pl.pallas_call(
        body, name="d_ffn_out_swiglu",
        out_shape=[jax.ShapeDtypeStruct((r, h), _MXU_DTYPE)] * 2,
        grid=(h // tn, r // tm),
        in_specs=[pl.BlockSpec((tm, k), lambda j, i: (i, 0)), pl.BlockSpec((tn, k), lambda j, i: (j, 0)), tile, tile],
        out_specs=[tile] * 2,
        compiler_params=_params(),
    )(dy, w, gt, up)


def _proj_q_rope(cqn, w_q, c_tab, s_tab, *, tm, lp):
    r, k = cqn.shape
    tiles_per_example = lp // tm

    def body(x_ref, wn_ref, wp_ref, ws_ref, c_ref, s_ref, o_ref):
        x = x_ref[...]
        roped = _dot(x, wp_ref[...]) * c_ref[...] + _dot(x, ws_ref[...]) * s_ref[...]
        o_ref[...] = jnp.concatenate([_dot(x, wn_ref[...]), roped], axis=1).astype(o_ref.dtype)

    w_blk = lambda part: pl.BlockSpec((k, HEAD_DIM), functools.partial(lambda h, i, part: (0, part * HEADS + h), part=part))
    tab = pl.BlockSpec((tm, HEAD_DIM), lambda h, i: (i % tiles_per_example, 0))
    return pl.pallas_call(
        body, name="proj_q_rope",
        out_shape=jax.ShapeDtypeStruct((r, HEADS * QK_DIM), _MXU_DTYPE),
        grid=(HEADS, r // tm),
        in_specs=[pl.BlockSpec((tm, k), lambda h, i: (i, 0)), w_blk(0), w_blk(1), w_blk(2), tab, tab],
        out_specs=pl.BlockSpec((tm, QK_DIM), lambda h, i: (i, h)),
        compiler_params=_params(),
    )(cqn, w_q, w_q, w_q, c_tab, s_tab)


def _rowwise(name, body, *, rows, tr, lp, ins, outs, accs=()):
    assert rows % tr == 0 and lp % tr == 0 and tr % 16 == 0
    tiles_per_example = lp // tr
    in_specs, arrays = [], []
    for spec in ins:
        if spec[0] == "row":
            _, arr, width, cb = spec
            in_specs.append(pl.BlockSpec((tr, width), functools.partial(lambda i, cb: (i, cb), cb=cb)))
        elif spec[0] == "const":
            arr = spec[1]
            in_specs.append(pl.BlockSpec(arr.shape, lambda i: (0, 0)))
        else:
            arr = spec[1]
            in_specs.append(pl.BlockSpec((tr, arr.shape[1]), lambda i: (i % tiles_per_example, 0)))
        arrays.append(arr)
    n_in, n_out = len(ins), len(outs)

    def kern(*refs):
        res_outs, res_accs = body(*[r[...] for r in refs[:n_in]])
        for ref, val in zip(refs[n_in:n_in + n_out], res_outs, strict=True):
            ref[...] = val.astype(ref.dtype)
        acc_refs = refs[n_in + n_out:]
        if acc_refs:
            @pl.when(pl.program_id(0) == 0)
            def _():
                for ref in acc_refs:
                    ref[...] = jnp.zeros_like(ref)

            for ref, val in zip(acc_refs, res_accs, strict=True):
                ref[...] += val.reshape(tr // _SUBLANES, _SUBLANES, val.shape[-1]).sum(axis=0)

    out_shape = ([jax.ShapeDtypeStruct((rows, w), dt) for w, dt in outs]
                 + [jax.ShapeDtypeStruct((_SUBLANES, w), F32) for w in accs])
    out_specs = ([pl.BlockSpec((tr, w), lambda i: (i, 0)) for w, _ in outs]
                 + [pl.BlockSpec((_SUBLANES, w), lambda i: (0, 0)) for w in accs])
    res = pl.pallas_call(
        kern, name=name, out_shape=out_shape, grid=(rows // tr,),
        in_specs=in_specs, out_specs=out_specs, compiler_params=_params(),
    )(*arrays)
    return res[:n_out], list(res[n_out:])


def _assemble(name, x, head_rows, lp):
    batch, seq, d = x.shape
    tc = 256

    def body(x_ref, m_ref, o_ref):
        o_ref[0:N_META, :] = m_ref[...]
        o_ref[N_META:N_META + seq, :] = x_ref[0]
        if lp > N_META + seq:
            o_ref[N_META + seq:, :] = jnp.zeros((lp - N_META - seq, tc), F32)

    return pl.pallas_call(
        body, name=name,
        out_shape=jax.ShapeDtypeStruct((batch * lp, d), F32),
        grid=(batch, d // tc),
        in_specs=[pl.BlockSpec((1, seq, tc), lambda b, j: (b, 0, j)),
                  pl.BlockSpec((N_META, tc), lambda b, j: (0, j))],
        out_specs=pl.BlockSpec((lp, tc), lambda b, j: (b, j)),
        compiler_params=_params(),
    )(x, head_rows)


def _meta_grad(dh0, batch, lp):
    d = dh0.shape[1]

    def body(g_ref, o_ref):
        @pl.when(pl.program_id(0) == 0)
        def _():
            o_ref[...] = jnp.zeros_like(o_ref)

        o_ref[...] += g_ref[...]

    return pl.pallas_call(
        body, name="meta_grad",
        out_shape=jax.ShapeDtypeStruct((N_META, d), F32),
        grid=(batch,),
        in_specs=[pl.BlockSpec((N_META, d), lambda b: (b * (lp // N_META), 0))],
        out_specs=pl.BlockSpec((N_META, d), lambda b: (0, 0)),
        compiler_params=_params(),
    )(dh0)


def _segment_masks():
    t = lax.broadcasted_iota(jnp.int32, (HG_BLOCK, HG_BLOCK), 0)
    s = lax.broadcasted_iota(jnp.int32, (HG_BLOCK, HG_BLOCK), 1)
    same = lax.shift_right_logical(t, 4) == lax.shift_right_logical(s, 4)
    lower = same & (s <= t)
    upper = same & (s >= t)
    first_half = same & ((s & 15) <= 7)
    return same, lower, upper, first_half


def _hgrn_gates(hq, hf, lb):
    sq = _sigmoid(hq)
    q = hq * sq
    sf = _sigmoid(hf)
    f = lb + (1.0 - lb) * sf
    return q, sq, sf, f


def _hgrn_decays(g, same, lower, first_half):
    b = _exact_dot(lower, g)
    b_last = _exact_dot(same, g)
    b_ref = _exact_dot(first_half, g)
    return b, b_last, b_ref


def _hgrn_fwd(p, lb, gh, *, batch, lp, ride=None):
    rows = batch * lp
    nb = lp // HG_BLOCK
    n_chunks = HG_BLOCK // HG_CHUNK

    def body(hq_ref, hf_ref, hi_ref, hg_ref, lb_ref, gh_ref, *rest):
        if ride is not None:
            rest, exchange = ride.split(rest, 3)
            ride.run((batch, nb), exchange)
        o_ref, z_ref, st_ref, s_scr = rest

        @pl.when(pl.program_id(1) == 0)
        def _():
            s_scr[...] = jnp.zeros_like(s_scr)

        same, lower, _, first_half = _segment_masks()
        v = hi_ref[...]
        q, _, _, f = _hgrn_gates(hq_ref[...], hf_ref[...], lb_ref[...])
        k = 1.0 - f
        b, b_last, b_ref = _hgrn_decays(jnp.log(f), same, lower, first_half)
        qt = _mx(q * jnp.exp(b))
        kh = _mx(k * jnp.exp(b_last - b))
        vm = _mx(v)
        el = jnp.exp(b_last)
        qc = _mx(q * jnp.exp(b - b_ref))
        kc = _mx(k * jnp.exp(b_ref - b))

        def intra(qc_h, kc_h, v_h):
            a = jnp.where(lower, _dot_nt(qc_h, kc_h), 0.0)
            return _dot(_mx(a), v_h)

        o_intra = _heads(intra, qc, kc, vm)

        states = [s_scr[h] for h in range(HEADS)]
        o_inter = [[None] * HEADS for _ in range(n_chunks)]
        for c in range(n_chunks):
            rs = slice(c * HG_CHUNK, (c + 1) * HG_CHUNK)
            for h in range(HEADS):
                cs = slice(h * HEAD_DIM, (h + 1) * HEAD_DIM)
                st_m = _mx(states[h])
                st_ref[c, h] = st_m
                o_inter[c][h] = _dot_nt(qt[rs, cs], st_m)
                states[h] = states[h] * el[c * HG_CHUNK:c * HG_CHUNK + 1, cs] + _dot_tn(vm[rs, cs], kh[rs, cs])
        for h in range(HEADS):
            s_scr[h] = states[h]

        o = o_intra + jnp.concatenate([jnp.concatenate(row, axis=1) for row in o_inter], axis=0)
        o_ref[...] = o
        hg = hg_ref[...]
        n = _heads(lambda o_h: o_h * _rms_scale(o_h), o) * gh_ref[...]
        z_ref[...] = (n * hg * _sigmoid(hg)).astype(z_ref.dtype)

    blk = lambda cb: pl.BlockSpec((HG_BLOCK, D_MODEL), functools.partial(lambda b, j, cb: (b * nb + j, cb), cb=cb))
    row_out = pl.BlockSpec((HG_BLOCK, D_MODEL), lambda b, j: (b * nb + j, 0))
    const = pl.BlockSpec((1, D_MODEL), lambda b, j: (0, 0))
    extra = ride if ride is not None else _NoRide
    return pl.pallas_call(
        body, name="hgrn_fwd",
        out_shape=[jax.ShapeDtypeStruct((rows, D_MODEL), F32),
                   jax.ShapeDtypeStruct((rows, D_MODEL), _MXU_DTYPE),
                   jax.ShapeDtypeStruct((rows // HG_CHUNK, HEADS, HEAD_DIM, HEAD_DIM), _MXU_DTYPE)] + extra.out_shape,
        grid=(batch, nb),
        in_specs=[blk(CB_HQ), blk(CB_HF), blk(CB_HI), blk(CB_HG), const, const] + extra.in_specs,
        out_specs=[row_out, row_out,
                   pl.BlockSpec((n_chunks, HEADS, HEAD_DIM, HEAD_DIM), lambda b, j: (b * nb + j, 0, 0, 0))]
        + extra.out_specs,
        scratch_shapes=[pltpu.VMEM((HEADS, HEAD_DIM, HEAD_DIM), F32)] + extra.scratch,
        compiler_params=_params(),
    )(p, p, p, p, lb, gh, *extra.args)


def _hgrn_bwd(p, o, dz, states, lb, gh, *, batch, lp, ride=None):
    rows = batch * lp
    nb = lp // HG_BLOCK
    n_chunks = HG_BLOCK // HG_CHUNK

    def body(hq_ref, hf_ref, hi_ref, hg_ref, o_ref, dz_ref, st_ref, lb_ref, gh_ref, *rest):
        if ride is not None:
            rest, exchange = ride.split(rest, 3)
            ride.run((batch, nb), exchange)
        dp_ref, dlb_ref, dgh_ref, ds_scr = rest
        first = (pl.program_id(0) == 0) & (pl.program_id(1) == 0)

        @pl.when(first)
        def _():
            dlb_ref[...] = jnp.zeros_like(dlb_ref)
            dgh_ref[...] = jnp.zeros_like(dgh_ref)

        @pl.when(pl.program_id(1) == 0)
        def _():
            ds_scr[...] = jnp.zeros_like(ds_scr)

        same, lower, upper, first_half = _segment_masks()
        lbv = lb_ref[...]
        hq, hf, v, hg = hq_ref[...], hf_ref[...], hi_ref[...], hg_ref[...]
        q, sq, sf, f = _hgrn_gates(hq, hf, lbv)
        k = 1.0 - f
        b, b_last, b_ref = _hgrn_decays(jnp.log(f), same, lower, first_half)
        e_b = jnp.exp(b)
        e_kh = jnp.exp(b_last - b)
        e_qc = jnp.exp(b - b_ref)
        e_kc = jnp.exp(b_ref - b)
        qt, kh, qc, kc = q * e_b, k * e_kh, q * e_qc, k * e_kc

        o = o_ref[...]
        dz = dz_ref[...].astype(F32)
        ghv = gh_ref[...]
        sg = _sigmoid(hg)
        r = _heads(lambda o_h: jnp.broadcast_to(_rms_scale(o_h), o_h.shape), o)
        oh = o * r
        dn = dz * hg * sg
        dhg = dz * oh * ghv * _silu_grad(hg, sg)
        w = dn * ghv
        do = r * (w - oh * _heads(lambda t: jnp.broadcast_to(jnp.mean(t, axis=-1, keepdims=True), t.shape), oh * w))
        dgh_ref[...] += (dn * oh).reshape(HG_BLOCK // _SUBLANES, _SUBLANES, D_MODEL).sum(axis=0)

        qt_m, kh_m, v_m, do_m = _mx(qt), _mx(kh), _mx(v), _mx(do)
        el_all = jnp.exp(b_last)

        def intra(qc_h, kc_h, v_h, do_h):
            a = _mx(jnp.where(lower, _dot_nt(qc_h, kc_h), 0.0))
            da = _mx(jnp.where(lower, _dot_nt(do_h, v_h), 0.0))
            return _dot(da, kc_h), _dot_tn(da, qc_h), _dot_tn(a, do_h)

        dqc, dkc, dv_intra = _heads(intra, _mx(qc), _mx(kc), v_m, do_m)

        d_states = [ds_scr[h] for h in range(HEADS)]
        grid_of = lambda: [[None] * HEADS for _ in range(n_chunks)]
        dkh_p, dv_p, dbl_p, dqt_p = grid_of(), grid_of(), grid_of(), grid_of()
        for c in reversed(range(n_chunks)):
            rs = slice(c * HG_CHUNK, (c + 1) * HG_CHUNK)
            for h in range(HEADS):
                cs = slice(h * HEAD_DIM, (h + 1) * HEAD_DIM)
                st = st_ref[c, h]
                ds_t = d_states[h]
                ds_m = _mx(ds_t)
                el = el_all[c * HG_CHUNK:c * HG_CHUNK + 1, cs]
                dkh_p[c][h] = _dot(v_m[rs, cs], ds_m)
                dv_p[c][h] = _dot_nt(kh_m[rs, cs], ds_m)
                dbl = jnp.sum(ds_t * st.astype(F32), axis=0, keepdims=True) * el
                dbl_p[c][h] = jnp.broadcast_to(dbl, (HG_CHUNK, HEAD_DIM))
                dqt_p[c][h] = _dot(do_m[rs, cs], st)
                d_states[h] = ds_t * el + _dot_tn(do_m[rs, cs], qt_m[rs, cs])
        for h in range(HEADS):
            ds_scr[h] = d_states[h]
        whole = lambda parts: jnp.concatenate([jnp.concatenate(row, axis=1) for row in parts], axis=0)

        dqt, dkh = whole(dqt_p), whole(dkh_p)
        dq = dqt * e_b + dqc * e_qc
        dk = dkh * e_kh + dkc * e_kc
        t_kh = dkh * kh
        db_rows = dqt * qt + dqc * qc - dkc * kc - t_kh
        dg = _exact_dot(upper, db_rows) + _exact_dot(same, t_kh) + whole(dbl_p)
        df = dg / f - dk
        dhf = df * (1.0 - lbv) * sf * (1.0 - sf)
        dlb_ref[...] += (df * (1.0 - sf)).reshape(HG_BLOCK // _SUBLANES, _SUBLANES, D_MODEL).sum(axis=0)
        dhq = dq * _silu_grad(hq, sq)
        dp_ref[...] = jnp.concatenate([dhq, dhf, dv_intra + whole(dv_p), dhg], axis=1).astype(dp_ref.dtype)

    rev = lambda b, j: b * nb + (nb - 1 - j)
    blk = lambda cb: pl.BlockSpec((HG_BLOCK, D_MODEL), functools.partial(lambda b, j, cb: (rev(b, j), cb), cb=cb))
    row = pl.BlockSpec((HG_BLOCK, D_MODEL), lambda b, j: (rev(b, j), 0))
    const = pl.BlockSpec((1, D_MODEL), lambda b, j: (0, 0))
    acc = pl.BlockSpec((_SUBLANES, D_MODEL), lambda b, j: (0, 0))
    extra = ride if ride is not None else _NoRide
    dp, dlb, dgh, *exchanged = pl.pallas_call(
        body, name="hgrn_bwd",
        out_shape=[jax.ShapeDtypeStruct((rows, 4 * D_MODEL), _MXU_DTYPE),
                   jax.ShapeDtypeStruct((_SUBLANES, D_MODEL), F32),
                   jax.ShapeDtypeStruct((_SUBLANES, D_MODEL), F32)] + extra.out_shape,
        grid=(batch, nb),
        in_specs=[blk(CB_HQ), blk(CB_HF), blk(CB_HI), blk(CB_HG), row, row,
                  pl.BlockSpec((n_chunks, HEADS, HEAD_DIM, HEAD_DIM), lambda b, j: (rev(b, j), 0, 0, 0)),
                  const, const] + extra.in_specs,
        out_specs=[pl.BlockSpec((HG_BLOCK, 4 * D_MODEL), lambda b, j: (rev(b, j), 0)), acc, acc] + extra.out_specs,
        scratch_shapes=[pltpu.VMEM((HEADS, HEAD_DIM, HEAD_DIM), F32)] + extra.scratch,
        compiler_params=_params(),
    )(p, p, p, p, o, dz, states, lb, gh, *extra.args)
    return (dp, dlb, dgh, *exchanged)


QK_DIM = 2 * HEAD_DIM
ATTN_TQ = 256
ATTN_KEY_CHUNK = 512


def _query_tiles(lp):
    return [(r0, min(ATTN_TQ, lp - r0)) for r0 in range(0, lp, ATTN_TQ)]


def _attn_fwd(q_cat, kv, kp, *, batch, lp):
    rows = batch * lp

    def body(q_ref, kn_ref, kp_ref, v_ref, o_ref, lse_ref):
        k_cat = jnp.concatenate([kn_ref[...], kp_ref[...]], axis=1)
        for r0, tq in _query_tiles(lp):
            q_t = q_ref[r0:r0 + tq, :]
            i = lax.broadcasted_iota(jnp.int32, (tq, tq), 0)
            j = lax.broadcasted_iota(jnp.int32, (tq, tq), 1)
            s_diag = jnp.where(j <= i, _dot_nt(q_t, k_cat[r0:r0 + tq]) * ATTN_SCALE, NEG_BIG)
            m = jnp.max(s_diag, axis=1, keepdims=True)
            if r0:
                s_past = _dot_nt(q_t, k_cat[0:r0]) * ATTN_SCALE
                m = jnp.maximum(m, jnp.max(s_past, axis=1, keepdims=True))
            p_diag = jnp.exp(s_diag - m)
            l = jnp.sum(p_diag, axis=1, keepdims=True)
            acc = _dot(_mx(p_diag), v_ref[r0:r0 + tq, :])
            if r0:
                p_past = jnp.exp(s_past - m)
                l = l + jnp.sum(p_past, axis=1, keepdims=True)
                acc = acc + _dot(_mx(p_past), v_ref[0:r0, :])
            o_ref[r0:r0 + tq, :] = (acc / l).astype(o_ref.dtype)
            lse_ref[r0:r0 + tq, :] = jnp.broadcast_to(m + jnp.log(l), (tq, HEAD_DIM))

    head_blk = pl.BlockSpec((lp, HEAD_DIM), lambda b, h: (b, h))
    return pl.pallas_call(
        body, name="attn_fwd",
        out_shape=[jax.ShapeDtypeStruct((rows, D_MODEL), _MXU_DTYPE),
                   jax.ShapeDtypeStruct((rows, D_MODEL), F32)],
        grid=(batch, HEADS),
        in_specs=[pl.BlockSpec((lp, QK_DIM), lambda b, h: (b, h)), head_blk,
                  pl.BlockSpec((lp, HEAD_DIM), lambda b, h: (b, 0)),
                  pl.BlockSpec((lp, HEAD_DIM), lambda b, h: (b, HEADS + h))],
        out_specs=[head_blk, head_blk],
        compiler_params=_params(),
    )(q_cat, kv, kp, kv)


def _attn_bwd(q_cat, kv, kp, do, o, lse, c_tab, s_tab, *, batch, lp, ride=None):
    rows = batch * lp

    def body(q_ref, kn_ref, kp_ref, v_ref, do_ref, o_ref, lse_ref, c_ref, s_ref, *rest):
        if ride is not None:
            rest, exchange = ride.split(rest, 6)
            ride.run((batch, HEADS), exchange)
        dqn_ref, dqc_ref, dqs_ref, dkn_ref, dkp_ref, dv_ref, dk_acc, dv_acc = rest
        dk_acc[...] = jnp.zeros_like(dk_acc)
        dv_acc[...] = jnp.zeros_like(dv_acc)
        k_cat = jnp.concatenate([kn_ref[...], kp_ref[...]], axis=1)
        k_t = k_cat.T
        lane = lax.broadcasted_iota(jnp.int32, (_SUBLANES, HEAD_DIM), 1)
        lse_row = _exact_dot_nt(lane == 0, lse_ref[...])
        delta = _exact_dot_nt(lane >= 0, do_ref[...].astype(F32) * o_ref[...].astype(F32))
        for r0, tq in _query_tiles(lp):
            cols = slice(r0, r0 + tq)
            q_t_, do_t_ = q_ref[cols, :], do_ref[cols, :]
            lse_t, delta_t = lse_row[0:1, cols], delta[0:1, cols]
            chunks = [(c0, min(ATTN_KEY_CHUNK, r0 - c0), False) for c0 in range(0, r0, ATTN_KEY_CHUNK)] + [(r0, tq, True)]
            dq_t = jnp.zeros((QK_DIM, tq), F32)
            for c0, n, diagonal in chunks:
                keys = slice(c0, c0 + n)
                s = _dot_nt(k_cat[keys], q_t_) * ATTN_SCALE
                if diagonal:
                    jk = lax.broadcasted_iota(jnp.int32, (n, tq), 0)
                    iq = lax.broadcasted_iota(jnp.int32, (n, tq), 1)
                    s = jnp.where(jk <= iq, s, NEG_BIG)
                pexp = jnp.exp(s - lse_t)
                dp = _dot_nt(v_ref[keys, :], do_t_)
                ds = _mx(pexp * (dp - delta_t) * ATTN_SCALE)
                dk_acc[keys, :] += _dot(ds, q_t_)
                dv_acc[keys, :] += _dot(_mx(pexp), do_t_)
                dq_t = dq_t + _dot(k_t[:, keys], ds)
            dq = dq_t.T
            d_rope = dq[:, HEAD_DIM:]
            dqn_ref[cols, :] = dq[:, :HEAD_DIM].astype(dqn_ref.dtype)
            dqc_ref[cols, :] = (d_rope * c_ref[cols, :]).astype(dqc_ref.dtype)
            dqs_ref[cols, :] = (d_rope * s_ref[cols, :]).astype(dqs_ref.dtype)

        dkn_ref[...] = dk_acc[:, 0:HEAD_DIM].astype(dkn_ref.dtype)
        dv_ref[...] = dv_acc[...].astype(dv_ref.dtype)

        @pl.when(pl.program_id(1) == 0)
        def _():
            dkp_ref[...] = jnp.zeros_like(dkp_ref)

        dkp_ref[...] += dk_acc[:, HEAD_DIM:]

    head_blk = pl.BlockSpec((lp, HEAD_DIM), lambda b, h: (b, h))
    cat_blk = pl.BlockSpec((lp, QK_DIM), lambda b, h: (b, h))
    shared_blk = pl.BlockSpec((lp, HEAD_DIM), lambda b, h: (b, 0))
    table_blk = pl.BlockSpec((lp, HEAD_DIM), lambda b, h: (0, 0))
    extra = ride if ride is not None else _NoRide
    return pl.pallas_call(
        body, name="attn_bwd",
        out_shape=[jax.ShapeDtypeStruct((rows, D_MODEL), _MXU_DTYPE)] * 3 + [
                   jax.ShapeDtypeStruct((rows, D_MODEL), _MXU_DTYPE),
                   jax.ShapeDtypeStruct((rows, HEAD_DIM), F32),
                   jax.ShapeDtypeStruct((rows, D_MODEL), _MXU_DTYPE)] + extra.out_shape,
        grid=(batch, HEADS),
        in_specs=[cat_blk, head_blk, shared_blk, pl.BlockSpec((lp, HEAD_DIM), lambda b, h: (b, HEADS + h)),
                  head_blk, head_blk, head_blk, table_blk, table_blk] + extra.in_specs,
        out_specs=[head_blk, head_blk, head_blk, head_blk, shared_blk, head_blk] + extra.out_specs,
        scratch_shapes=[pltpu.VMEM((lp, QK_DIM), F32), pltpu.VMEM((lp, HEAD_DIM), F32)] + extra.scratch,
        compiler_params=_params(),
    )(q_cat, kv, kp, kv, do, o, lse, c_tab, s_tab, *extra.args)


def _all_gather(name, blocks):
    n = len(blocks)

    def body(*refs):
        x_refs, out_refs, (send_sems, recv_sems, local_sems) = refs[:n], refs[n:2 * n], refs[2 * n:]
        x, y, c = lax.axis_index("x"), lax.axis_index("y"), lax.axis_index("c")
        me, sibling = (x, y, c), (x, y, 1 - c)
        chips = [(1 - x, y), (x, 1 - y), (1 - x, 1 - y)]

        def slot(i, px, py, pc):
            return out_refs[i].at[4 * px + 2 * py + pc]

        def copy(i, k, blk, to, src=None):
            return pltpu.make_async_remote_copy(
                src_ref=slot(i, *blk) if src is None else src, dst_ref=slot(i, *blk),
                send_sem=send_sems.at[i, k], recv_sem=recv_sems.at[i, k],
                device_id=to, device_id_type=pl.DeviceIdType.MESH)

        mine = [pltpu.make_async_copy(x_refs[i], slot(i, *me), local_sems.at[i]) for i in range(n)]
        first = [copy(i, 0, me, sibling, src=x_refs[i]) for i in range(n)]
        first += [copy(i, 1 + j, me, (*chip, c), src=x_refs[i]) for i in range(n) for j, chip in enumerate(chips)]
        for cp in mine + first:
            cp.start()
        passed = []
        for i in range(n):
            for j, chip in enumerate(chips):
                copy(i, 1 + j, (*chip, c), me).wait_recv()
                passed.append(copy(i, 4 + j, (*chip, c), sibling))
                passed[-1].start()
        for i in range(n):
            copy(i, 0, sibling, me).wait_recv()
            for j, chip in enumerate(chips):
                copy(i, 4 + j, (*chip, 1 - c), me).wait_recv()
        for cp in first + passed:
            cp.wait_send()
        for cp in mine:
            cp.wait()

    return pl.pallas_call(
        body, name=name,
        out_shape=[jax.ShapeDtypeStruct((N_DEV, *b.shape), b.dtype) for b in blocks],
        in_specs=[pl.BlockSpec(memory_space=pl.ANY)] * n,
        out_specs=[pl.BlockSpec(memory_space=pl.ANY)] * n,
        scratch_shapes=[pltpu.SemaphoreType.DMA((n, 7)), pltpu.SemaphoreType.DMA((n, 7)),
                        pltpu.SemaphoreType.DMA((n,))],
    )(*blocks)


def _adamw_math(w, g, m, v):
    nm = ADAM_B1 * m + (1.0 - ADAM_B1) * g
    nv = ADAM_B2 * v + (1.0 - ADAM_B2) * (g * g)
    m_hat = nm / (1.0 - ADAM_B1 ** ADAM_STEP)
    v_hat = nv / (1.0 - ADAM_B2 ** ADAM_STEP)
    return -ADAM_LR * (m_hat / (jnp.sqrt(v_hat) + ADAM_EPS) + ADAM_WD * w), nm, nv


def _sum_adamw(name, parts, w, m, v):
    rows, cols = w.shape
    tr = rows // 4 if rows % 64 == 0 and rows * cols > (1 << 16) else rows

    def body(p_ref, w_ref, m_ref, v_ref, g_ref, d_ref, nm_ref, nv_ref):
        g = p_ref[0].astype(F32)
        for dev in range(1, N_DEV):
            g = g + p_ref[dev].astype(F32)
        g_ref[...] = g
        d_ref[...], nm_ref[...], nv_ref[...] = _adamw_math(w_ref[...], g, m_ref[...], v_ref[...])

    spec = pl.BlockSpec((tr, cols), lambda i: (i, 0))
    return pl.pallas_call(
        body, name=name,
        out_shape=[jax.ShapeDtypeStruct((rows, cols), F32)] * 4,
        grid=(rows // tr,),
        in_specs=[pl.BlockSpec((N_DEV, tr, cols), lambda i: (0, i, 0))] + [spec] * 3, out_specs=[spec] * 4,
        compiler_params=_params(),
    )(parts, w, m, v)


def _finish_vectors(gathered, lb, params, loss_parts):
    names = list(params)
    n = len(names)

    def body(*refs):
        g_refs, lb_ref, loss_ref = refs[:n], refs[n], refs[n + 1]
        wmv_refs = refs[n + 2:4 * n + 2]
        out_refs, loss_out = refs[4 * n + 2:-1], refs[-1]
        sq = loss_ref[0]
        for dev in range(1, N_DEV):
            sq = sq + loss_ref[dev]
        sq = jnp.sum(jnp.sum(sq, axis=0, keepdims=True), axis=1, keepdims=True)
        loss_out[...] = sq * (0.5 / D_MODEL)
        me = 4 * lax.axis_index("x") + 2 * lax.axis_index("y") + lax.axis_index("c")
        for i, name in enumerate(names):
            g_ref = g_refs[i]
            w_ref, m_ref, v_ref = wmv_refs[3 * i:3 * i + 3]
            if name == "meta_tokens":
                width = w_ref.shape[1]
                mine = pl.ds(pl.multiple_of(me * width, width), width)
                g = g_ref[0, :, mine]
                for dev in range(1, N_DEV):
                    g = g + g_ref[dev, :, mine]
            else:
                g = g_ref[0]
                for dev in range(1, N_DEV):
                    g = g + g_ref[dev]
                g = jnp.sum(g, axis=0, keepdims=True)
                if name == "hg_norm_g":
                    g = functools.reduce(jnp.add, [g[:, h * HEAD_DIM:(h + 1) * HEAD_DIM] for h in range(HEADS)])
                if name == "lb_logits":
                    lbv = lb_ref[...]
                    g = g * lbv * (1.0 - lbv)
                    g = jnp.concatenate([g, -g], axis=0)
            outs = (g, *_adamw_math(w_ref[...], g, m_ref[...], v_ref[...]))
            for ref, val in zip(out_refs[4 * i:4 * i + 4], outs, strict=True):
                ref[...] = val

    args = [gathered[k] for k in names] + [lb, loss_parts] + [t for k in names for t in params[k]]
    res = pl.pallas_call(
        body, name="finish_vectors",
        out_shape=[jax.ShapeDtypeStruct(params[k][0].shape, F32) for k in names for _ in range(4)]
        + [jax.ShapeDtypeStruct((1, 1), F32)],
        compiler_params=_params(),
    )(*args)
    return {k: res[4 * i:4 * i + 4] for i, k in enumerate(names)}, res[-1].reshape(())


def _swap_halves(t):
    half = t.shape[-1] // 2
    return jnp.concatenate([t[..., half:], t[..., :half]], axis=-1)


def _pad_last(t, width):
    return jnp.concatenate([t, jnp.zeros(t.shape[:-1] + (width - t.shape[-1],), t.dtype)], axis=-1)


def _rope_tables(lp):
    pos = jnp.arange(lp, dtype=F32)
    inv_freq = 1.0 / (ROPE_THETA ** (jnp.arange(0, ROPE_DIM, 2, dtype=F32) / ROPE_DIM))
    ang = pos[:, None] * inv_freq[None, :]
    cos, sin = jnp.cos(ang), jnp.sin(ang)
    c128 = _pad_last(jnp.concatenate([cos, cos], axis=1), HEAD_DIM)
    s128 = _pad_last(jnp.concatenate([-sin, sin], axis=1), HEAD_DIM)
    return c128, s128


def _forward_backward(x, target, meta, w, small, *, lp, comm=None):
    batch, seq, d = x.shape
    rows = batch * lp
    tr = 272 if lp % 272 == 0 else 128
    tm = lp // 2
    bf = _MXU_DTYPE
    rw = functools.partial(_rowwise, rows=rows, tr=tr, lp=lp)

    c128, s128 = _rope_tables(lp)
    t_idx = jnp.arange(lp)
    real = jnp.broadcast_to(((t_idx >= N_META) & (t_idx < N_META + seq)).astype(F32)[:, None], (lp, _LANES))

    lb_logits = small["lb_logits"]
    lb = jax.nn.softmax(lb_logits, axis=0)[0:1]
    gh = jnp.tile(small["hg_norm_g"], (1, HEADS))

    h0 = _assemble("assemble_x", x, meta, lp)
    tgt = _assemble("assemble_target", target, jnp.zeros_like(meta), lp)

    (u1,), _ = rw("norm_mix_pre", lambda h, g: ([h * _rms_scale(h) * g], []),
                  ins=[("row", h0, d, 0), ("const", small["mix_pre_g"])], outs=[(d, bf)])
    p = _matmul("proj_in", u1, w["w_in"], out_dtype=F32, tm=tm, tn=1024, tk=1024)

    if comm is None:
        o_hg, z_a, states = _hgrn_fwd(p, lb, gh, batch=batch, lp=lp)
    else:
        o_hg, z_a, states, *gathered = _hgrn_fwd(p, lb, gh, batch=batch, lp=lp, ride=_Ride(comm.rest_payloads, True))
        w = {**w, **comm.rest_weights(gathered)}
    received = []
    scatter = lambda names: _Ride(comm.grad_parts(names, grads), False) if comm is not None else None
    y_a = _matmul("proj_hg_o", z_a, w["w_hg_o"], out_dtype=F32, tm=tm, tn=1024, tk=1024)

    def mla_pre(pc, gq, gkv, ct, st):
        cq, ckv = pc[:, 0:Q_LORA], pc[:, Q_LORA:Q_LORA + KV_LORA]
        kpe, kpe_sw = pc[:, 512:640], pc[:, 640:768]
        return [cq * _rms_scale(cq) * gq, ckv * _rms_scale(ckv) * gkv, kpe * ct + kpe_sw * st], []

    (cqn, ckvn, kp), _ = rw("mla_pre", mla_pre,
                            ins=[("row", p, 1024, CB_C), ("const", small["q_a_norm_g"]),
                                 ("const", small["kv_a_norm_g"]), ("pos", c128), ("pos", s128)],
                            outs=[(Q_LORA, bf), (KV_LORA, bf), (HEAD_DIM, bf)])
    q_cat = _proj_q_rope(cqn, w["w_q"], c128, s128, tm=tm, lp=lp)
    kv = _matmul("proj_kv_b", ckvn, w["w_kv"], out_dtype=bf, tm=tm, tn=1024, tk=KV_LORA)
    o_at, lse = _attn_fwd(q_cat, kv, kp, batch=batch, lp=lp)
    te = 512 if rows % 512 == 0 else 256
    real_rows = jnp.tile(real, (batch, 1))

    def merge(yb, pa, pb, ya, bg):
        ga, gb = _sigmoid(pa + bg[:, :d]), _sigmoid(pb + bg[:, d:])
        return [yb, ga * ya + gb * yb], []

    (y_b, mix), _ = _matmul_epilogue(
        "proj_mla_o", [o_at], w["w_mla_o"],
        _Epilogue(merge, rows=[(p, 1024, CB_GA), (p, 1024, CB_GB), y_a], consts=[small["b_gate"]],
                  outs=[(d, F32), (d, bf)]), tm=te, tk=1024)

    def post_mix(mx_, h, g2, g3):
        h1_ = h + mx_ * _rms_scale(mx_) * g2
        return [mx_, h1_, h1_ * _rms_scale(h1_) * g3], []

    (mixed, h1, u2), _ = _matmul_epilogue(
        "proj_out", [mix], w["w_out"],
        _Epilogue(post_mix, rows=[h0], consts=[small["mix_post_g"], small["ffn_pre_g"]],
                  outs=[(d, F32), (d, F32), (d, bf)]), tm=te, tk=1024)
    act, gt, up = _ffn_in_swiglu(u2, w["w_ffn_in"], tm=tm, tn=1408)

    def post_ffn(fo_, h1_, t_, mask, g4):
        r = _rms_scale(fo_)
        h2 = h1_ + fo_ * r * g4
        err = (h2 - t_) * mask[:, 0:1]
        dh2 = err * (1.0 / d)
        dfo, dg4 = _rms_bwd(fo_, g4, dh2)
        return [dh2, dfo], [err * err, dg4]

    (dh2, dfo), (loss_vec, dg_ffn_post) = _matmul_epilogue(
        "ffn_out", [act], w["w_ffn_out"],
        _Epilogue(post_ffn, rows=[h1, tgt, real_rows], consts=[small["ffn_post_g"]],
                  outs=[(d, F32), (d, bf)], accs=[d, d]), tm=te, tk=1408)
    loss = (0.5 / d) * jnp.sum(loss_vec)

    grads = {}
    dw_dt = F32 if comm is None else _WIRE_DTYPE
    dgt, dup = _d_ffn_out_swiglu(dfo, w["w_ffn_out"], gt, up, tm=tm, tn=1408)
    grads["w_ffn_out"] = _matmul_tn("dw_ffn_out", act, dfo, tk=1408, tn=1024, tr=tm, out_dtype=dw_dt)
    grads["w_ffn_in"] = jnp.concatenate([_matmul_tn("dw_ffn_in_gate", u2, dgt, tk=1024, tn=1408, tr=tm),
                                         _matmul_tn("dw_ffn_in_up", u2, dup, tk=1024, tn=1408, tr=tm)], axis=1)

    def post_mix_bwd(du2_, h1_, dh2_, mx_, g3, g2):
        dx, dg3 = _rms_bwd(h1_, g3, du2_)
        dh1_ = dh2_ + dx
        dmx, dg2 = _rms_bwd(mx_, g2, dh1_)
        return [dh1_, dmx], [dg3, dg2]

    (dh1, dmixed), (dg_ffn_pre, dg_mix_post) = _matmul_epilogue(
        "d_ffn_in", [dgt, dup], w["w_ffn_in"],
        _Epilogue(post_mix_bwd, rows=[h1, dh2, mixed], consts=[small["ffn_pre_g"], small["mix_post_g"]],
                  outs=[(d, F32), (d, bf)], accs=[d, d]), tm=te, tk=1408, b_transposed=True)
    grads["w_out"] = _matmul_tn("dw_out", mix, dmixed, tk=1024, tn=1024, tr=tm, out_dtype=dw_dt)

    def merge_bwd(dm, pa, pb, ya, yb, bg):
        ga, gb = _sigmoid(pa + bg[:, :d]), _sigmoid(pb + bg[:, d:])
        dpg = jnp.concatenate([dm * ya * ga * (1.0 - ga), dm * yb * gb * (1.0 - gb)], axis=1)
        return [dpg, dm * ga, dm * gb], [dpg]

    (dpg, dya, dyb), (db_gate,) = _matmul_epilogue(
        "d_proj_out", [dmixed], w["w_out"],
        _Epilogue(merge_bwd, rows=[(p, 1024, CB_GA), (p, 1024, CB_GB), y_a, y_b], consts=[small["b_gate"]],
                  outs=[(2 * d, bf), (d, bf), (d, bf)], accs=[2 * d]), tm=te, tk=1024, b_transposed=True)
    dz_a = _matmul("d_proj_hg_o", dya, w["w_hg_o"], out_dtype=F32, tm=tm, tn=1024, tk=1024, b_transposed=True)
    grads["w_hg_o"] = _matmul_tn("dw_hg_o", z_a, dya, tk=1024, tn=1024, tr=tm, out_dtype=dw_dt)
    do_at = _matmul("d_proj_mla_o", dyb, w["w_mla_o"], out_dtype=bf, tm=tm, tn=1024, tk=1024, b_transposed=True)
    grads["w_mla_o"] = _matmul_tn("dw_mla_o", o_at, dyb, tk=1024, tn=1024, tr=tm, out_dtype=dw_dt)

    dph, dlb, dgh, *got = _hgrn_bwd(p, o_hg, dz_a, states, lb, gh, batch=batch, lp=lp,
                                    ride=scatter(_GRAD_GROUPS[0]))
    received.append(got)

    res = _attn_bwd(q_cat, kv, kp, do_at, o_at, lse, c128, s128, batch=batch, lp=lp, ride=scatter(_GRAD_GROUPS[1]))
    dq_parts, (dkn, dkp, dvv) = list(res[:3]), res[3:6]
    received.append(list(res[6:]))
    dcqn = _matmul_segments("d_proj_q_b", dq_parts, w["w_q"], tm=tm, tn=Q_LORA, tk=1024, b_transposed=True)
    grads["w_q"] = jnp.concatenate([_matmul_tn(f"dw_q_b_{i}", cqn, part, tk=Q_LORA, tn=1024, tr=tm)
                                    for i, part in enumerate(dq_parts)], axis=1)
    dckvn = _matmul_segments("d_proj_kv_b", [dkn, dvv], w["w_kv"], tm=tm, tn=KV_LORA, tk=1024, b_transposed=True)
    grads["w_k"] = _matmul_tn("dw_k_b", ckvn, dkn, tk=KV_LORA, tn=1024, tr=tm)
    grads["w_v"] = _matmul_tn("dw_v_b", ckvn, dvv, tk=KV_LORA, tn=1024, tr=tm)

    def mla_pre_bwd(pc, dq_, dkv_, dkp_, gq, gkv, ct, st):
        cq, ckv = pc[:, 0:Q_LORA], pc[:, Q_LORA:Q_LORA + KV_LORA]
        dcq, dgq = _rms_bwd(cq, gq, dq_)
        dckv, dgkv = _rms_bwd(ckv, gkv, dkv_)
        dpc = jnp.concatenate([dcq, dckv, dkp_ * ct, dkp_ * st, jnp.zeros((pc.shape[0], 256), F32)], axis=1)
        return [dpc], [dgq, dgkv]

    (dpc,), (dg_q, dg_kv) = rw(
        "mla_pre_bwd", mla_pre_bwd,
        ins=[("row", p, 1024, CB_C), ("row", dcqn, Q_LORA, 0), ("row", dckvn, KV_LORA, 0), ("row", dkp, HEAD_DIM, 0),
             ("const", small["q_a_norm_g"]), ("const", small["kv_a_norm_g"]), ("pos", c128), ("pos", s128)],
        outs=[(1024, bf)], accs=[Q_LORA, KV_LORA])

    grads["w_in"] = (_matmul_tn("dw_in_h", u1, dph, tk=1024, tn=1024, tr=tm),
                     _matmul_tn("dw_in_c", u1, dpc, tk=1024, tn=1024, tr=tm),
                     _matmul_tn("dw_in_g", u1, dpg, tk=1024, tn=1024, tr=tm))
    def pre_bwd(du, h, dh, g1):
        dx, dg1 = _rms_bwd(h, g1, du)
        return [dh + dx], [dg1]

    (dh0,), (dg_mix_pre,), *got = _matmul_epilogue(
        "d_proj_in", [dph, dpc, dpg], w["w_in"],
        _Epilogue(pre_bwd, rows=[h0, dh1], consts=[small["mix_pre_g"]], outs=[(d, F32)], accs=[d]),
        tm=te, tk=1024, b_transposed=True, ride=scatter(_GRAD_GROUPS[2]))
    if comm is not None:
        received.append(got)
    grad_x = dh0.reshape(batch, lp, d)[:, N_META:N_META + seq]
    partial = {"meta_tokens": _meta_grad(dh0, batch, lp), "lb_logits": dlb, "b_gate": db_gate, "hg_norm_g": dgh,
               "q_a_norm_g": dg_q, "kv_a_norm_g": dg_kv, "mix_pre_g": dg_mix_pre, "mix_post_g": dg_mix_post,
               "ffn_pre_g": dg_ffn_pre, "ffn_post_g": dg_ffn_post, "loss": loss_vec}
    return loss, grad_x, grads, partial, lb, received


_BIG = ["w_in", "w_hg_o", "w_q_b", "w_kv_b", "w_mla_o", "w_out", "w_ffn_in", "w_ffn_out"]
_COLUMN_SHARDED = {"w_in", "w_q_b", "w_kv_b", "w_ffn_in"}
_GRAD_GROUPS = [["w_ffn_in", "w_ffn_out"], ["w_out", "w_hg_o", "w_mla_o"], ["w_in", "w_q_b", "w_kv_b"]]
_SMALL = ["b_gate", "lb_logits", "hg_norm_g", "q_a_norm_g", "kv_a_norm_g", "mix_pre_g", "mix_post_g",
          "ffn_pre_g", "ffn_post_g"]


def _gathered_matrix(name, t):
    _, k, n = t.shape
    if name in _COLUMN_SHARDED:
        return t.transpose(1, 0, 2).reshape(k, N_DEV * n)
    return t.reshape(N_DEV * k, n)


def _scatter_layout(name, full):
    kk, nn = full.shape
    if name in _COLUMN_SHARDED:
        t = full.reshape(kk, N_DEV, nn // N_DEV).transpose(1, 0, 2)
    else:
        t = full.reshape(N_DEV, kk // N_DEV, nn)
    return t.astype(_WIRE_DTYPE)


def _model_w_in(wi):
    z = lambda *s: jnp.zeros(s, wi.dtype)
    kpe = wi[:, 4608:4672]
    c_blk = jnp.concatenate([wi[:, 4096:4608], kpe, z(1024, 64), _swap_halves(kpe), z(1024, 64), z(1024, 256)], axis=1)
    return {"w_in": jnp.concatenate([wi[:, :4096], c_blk, wi[:, 4672:]], axis=1).astype(_MXU_DTYPE)}


def _model_weights(full):
    return {**_model_w_in(full["w_in"]), **_model_rest(full)}


def _model_rest(full):
    wq3 = full["w_q_b"].reshape(Q_LORA, HEADS, HEAD_DIM + ROPE_DIM)
    pe = wq3[:, :, HEAD_DIM:]
    w_q = jnp.concatenate([wq3[:, :, :HEAD_DIM].reshape(Q_LORA, -1),
                           _pad_last(pe, HEAD_DIM).reshape(Q_LORA, -1),
                           _pad_last(_swap_halves(pe), HEAD_DIM).reshape(Q_LORA, -1)], axis=1)
    wkv3 = full["w_kv_b"].reshape(KV_LORA, HEADS, 2 * HEAD_DIM)
    w_k = wkv3[:, :, :HEAD_DIM].reshape(KV_LORA, -1)
    w_v = wkv3[:, :, HEAD_DIM:].reshape(KV_LORA, -1)
    w = {"w_q": w_q, "w_kv": jnp.concatenate([w_k, w_v], axis=1),
         "w_hg_o": full["w_hg_o"], "w_mla_o": full["w_mla_o"], "w_out": full["w_out"],
         "w_ffn_in": full["w_ffn_in"], "w_ffn_out": full["w_ffn_out"]}
    return {k: v.astype(_MXU_DTYPE) for k, v in w.items()}


def _reference_layout_grad(name, g):
    if name == "w_in":
        g_h, g_c, g_g = g["w_in"]
        d_kpe = g_c[:, 512:576] + _swap_halves(g_c[:, 640:704])
        return jnp.concatenate([g_h, g_c[:, :512], d_kpe, g_g], axis=1)
    if name == "w_q_b":
        gq = g["w_q"]
        d_pe = (gq[:, 1024:2048].reshape(Q_LORA, HEADS, HEAD_DIM)[:, :, :ROPE_DIM]
                + _swap_halves(gq[:, 2048:].reshape(Q_LORA, HEADS, HEAD_DIM)[:, :, :ROPE_DIM]))
        return jnp.concatenate([gq[:, :1024].reshape(Q_LORA, HEADS, HEAD_DIM), d_pe], axis=2).reshape(Q_LORA, -1)
    if name == "w_kv_b":
        return jnp.concatenate([g["w_k"].reshape(KV_LORA, HEADS, HEAD_DIM),
                                g["w_v"].reshape(KV_LORA, HEADS, HEAD_DIM)], axis=2).reshape(KV_LORA, -1)
    return g[name]


def _reference_layout_grads(g):
    return {n: _reference_layout_grad(n, g) for n in _BIG}


class _Comm:
    def __init__(self, shard):
        self.rest_payloads = [shard[n].astype(_WIRE_DTYPE) for n in _BIG[1:]]

    def rest_weights(self, gathered):
        return _model_rest({n: _gathered_matrix(n, t) for n, t in zip(_BIG[1:], gathered, strict=True)})

    def grad_parts(self, names, g):
        return [_scatter_layout(n, _reference_layout_grad(n, g)) for n in names]


def kernel(x, meta_tokens, w_in, b_gate, lb_logits, hg_norm_g, w_hg_o, q_a_norm_g, w_q_b, kv_a_norm_g, w_kv_b, w_mla_o, w_out, mix_pre_g, mix_post_g, ffn_pre_g, ffn_post_g, w_ffn_in, w_ffn_out, loss_target, m_meta_tokens, m_w_in, m_b_gate, m_lb_logits, m_hg_norm_g, m_w_hg_o, m_q_a_norm_g, m_w_q_b, m_kv_a_norm_g, m_w_kv_b, m_w_mla_o, m_w_out, m_mix_pre_g, m_mix_post_g, m_ffn_pre_g, m_ffn_post_g, m_w_ffn_in, m_w_ffn_out, v_meta_tokens, v_w_in, v_b_gate, v_lb_logits, v_hg_norm_g, v_w_hg_o, v_q_a_norm_g, v_w_q_b, v_kv_a_norm_g, v_w_kv_b, v_w_mla_o, v_w_out, v_mix_pre_g, v_mix_post_g, v_ffn_pre_g, v_ffn_post_g, v_w_ffn_in, v_w_ffn_out):
    args = dict(locals())
    batch, seq, d = x.shape
    lp = -(-(N_META + seq) // _LANES) * _LANES
    weight_names = ["meta_tokens", "w_in", "b_gate", "lb_logits", "hg_norm_g", "w_hg_o", "q_a_norm_g", "w_q_b",
                    "kv_a_norm_g", "w_kv_b", "w_mla_o", "w_out", "mix_pre_g", "mix_post_g", "ffn_pre_g",
                    "ffn_post_g", "w_ffn_in", "w_ffn_out"]
    shard = {n: args[n].reshape(args[n].shape[-2:]) for n in _BIG}
    comm = _Comm(shard)

    w_in_all, meta_all = _all_gather("gather_first", [shard["w_in"].astype(_WIRE_DTYPE), meta_tokens])
    w_first = _model_w_in(_gathered_matrix("w_in", w_in_all))
    meta_full = meta_all.transpose(1, 0, 2).reshape(N_META, d)
    small = {n: args[n] for n in _SMALL}

    _, grad_x, _, partial, lb, received = _forward_backward(x, loss_target, meta_full, w_first, small, lp=lp, comm=comm)
    out = {}
    for names, bufs in zip(_GRAD_GROUPS, received, strict=True):
        for n, buf in zip(names, bufs, strict=True):
            two_d = lambda t: t.reshape(t.shape[-2:])
            res = _sum_adamw("adamw_" + n, buf, shard[n], two_d(args["m_" + n]), two_d(args["v_" + n]))
            out[n] = [t.reshape(args[n].shape) for t in res]

    vec_names = _SMALL + ["meta_tokens"]
    *gathered, loss_parts = _all_gather("gather_vectors", [partial[n] for n in vec_names + ["loss"]])
    finished, loss = _finish_vectors(dict(zip(vec_names, gathered, strict=True)), lb,
                                     {n: (args[n], args["m_" + n], args["v_" + n]) for n in vec_names}, loss_parts)
    out.update(finished)
    return (loss, grad_x, *[out[n][i] for i in range(4) for n in weight_names])
```

```python
import functools

import jax
import jax.numpy as jnp
from jax import lax
from jax.experimental import pallas as pl
from jax.experimental.pallas import tpu as pltpu

F32 = jnp.float32
_MXU_DTYPE = jnp.bfloat16
_WIRE_DTYPE = jnp.bfloat16
_VMEM_LIMIT_BYTES = 56 * 1024 * 1024
_LANES = 128
_SUBLANES = 8

N_DEV = 8
N_META = 16
NORM_EPS = 1e-6
HEADS = 8
HEAD_DIM = 128
ROPE_DIM = 64
HG_CHUNK = 16
HG_BLOCK = 128
ROPE_THETA = 10000.0
D_MODEL = 1024
Q_LORA = 256
KV_LORA = 256
FFN_HIDDEN = 2816
ATTN_SCALE = (HEAD_DIM + ROPE_DIM) ** -0.5
NEG_BIG = -1e30

ADAM_LR = 0.001
ADAM_B1 = 0.9
ADAM_B2 = 0.999
ADAM_EPS = 1e-08
ADAM_WD = 0.01
ADAM_STEP = 10

CB_HQ, CB_HF, CB_HI, CB_HG, CB_C, CB_GA, CB_GB = range(7)
IN_COLS_PADDED = 7 * 1024


def _params(**kw):
    return pltpu.CompilerParams(vmem_limit_bytes=_VMEM_LIMIT_BYTES, **kw)


def _dot(a, b):
    return lax.dot_general(a, b, (((1,), (0,)), ((), ())), preferred_element_type=F32)


def _dot_nt(a, b):
    return lax.dot_general(a, b, (((1,), (1,)), ((), ())), preferred_element_type=F32)


def _dot_tn(a, b):
    return lax.dot_general(a, b, (((0,), (0,)), ((), ())), preferred_element_type=F32)


def _mx(x):
    return x.astype(_MXU_DTYPE)


def _exact_dot(m01, x, dot=_dot):
    if _MXU_DTYPE == jnp.float32:
        return dot(m01.astype(F32), x)
    m = m01.astype(jnp.bfloat16)
    x1 = x.astype(jnp.bfloat16)
    x2 = (x - x1.astype(F32)).astype(jnp.bfloat16)
    return dot(m, x1) + dot(m, x2)


def _exact_dot_nt(m01, x):
    return _exact_dot(m01, x, dot=_dot_nt)


def _sigmoid(x):
    return jax.nn.sigmoid(x)


def _silu_grad(x, s):
    return s * (1.0 + x * (1.0 - s))


def _rms_scale(x):
    return lax.rsqrt(jnp.mean(x * x, axis=-1, keepdims=True) + NORM_EPS)


def _rms_bwd(x, g, dy):
    r = _rms_scale(x)
    xh = x * r
    w = dy * g
    dx = r * (w - xh * jnp.mean(xh * w, axis=-1, keepdims=True))
    return dx, dy * xh


def _heads(fn, *arrays):
    outs = [fn(*[a[:, h * HEAD_DIM:(h + 1) * HEAD_DIM] for a in arrays]) for h in range(HEADS)]
    if isinstance(outs[0], tuple):
        return tuple(jnp.concatenate([o[i] for o in outs], axis=1) for i in range(len(outs[0])))
    return jnp.concatenate(outs, axis=1)


class _Ride:
    def __init__(self, payloads, gather):
        self.gather, self.args, self.n = gather, list(payloads), len(payloads)
        self.in_specs = [pl.BlockSpec(memory_space=pl.ANY)] * self.n
        self.out_shape = [jax.ShapeDtypeStruct((N_DEV, *p.shape[-2:]), p.dtype) for p in payloads]
        self.out_specs = [pl.BlockSpec(memory_space=pl.ANY)] * self.n
        self.scratch = [pltpu.SemaphoreType.DMA((self.n, N_DEV - 1)), pltpu.SemaphoreType.DMA((self.n, N_DEV - 1)),
                        pltpu.SemaphoreType.DMA((self.n,))]

    def split(self, rest, n_outs):
        n = self.n
        mine = (rest[:n], rest[n + n_outs:2 * n + n_outs], rest[-3:])
        return rest[n:n + n_outs] + rest[2 * n + n_outs:-3], mine

    def _copies(self, p_refs, out_refs, sems):
        send_sems, recv_sems, local_sems = sems
        x, y, c = lax.axis_index("x"), lax.axis_index("y"), lax.axis_index("c")
        me = 4 * x + 2 * y + c
        copies = []
        for i, (p_ref, out_ref) in enumerate(zip(p_refs, out_refs, strict=True)):
            part = (lambda j, p_ref=p_ref: p_ref) if self.gather else (lambda j, p_ref=p_ref: p_ref.at[j])
            copies.append(pltpu.make_async_copy(part(me), out_ref.at[me], local_sems.at[i]))
            for k in range(1, N_DEV):
                px, py, pc = x ^ (k >> 2), y ^ ((k >> 1) & 1), c ^ (k & 1)
                copies.append(pltpu.make_async_remote_copy(
                    src_ref=part(4 * px + 2 * py + pc), dst_ref=out_ref.at[me],
                    send_sem=send_sems.at[i, k - 1], recv_sem=recv_sems.at[i, k - 1],
                    device_id=(px, py, pc), device_id_type=pl.DeviceIdType.MESH))
        return copies

    def run(self, grid, refs):
        ids = [pl.program_id(i) for i in range(len(grid))]
        first = functools.reduce(jnp.logical_and, [i == 0 for i in ids])
        last = functools.reduce(jnp.logical_and, [i == g - 1 for i, g in zip(ids, grid)])

        @pl.when(first)
        def _():
            for cp in self._copies(*refs):
                cp.start()

        @pl.when(last)
        def _():
            for cp in self._copies(*refs):
                cp.wait()


class _NoRide:
    in_specs, out_shape, out_specs, scratch, args = [], [], [], [], []


def _matmul(name, a, b, *, out_dtype, tm, tn, tk, c_in=None, ride=None, b_transposed=False):
    m, k = a.shape
    n = b.shape[0] if b_transposed else b.shape[1]
    assert m % tm == 0 and n % tn == 0 and k % tk == 0, (name, a.shape, b.shape, tm, tn, tk)
    nk = k // tk
    has_c = c_in is not None
    dot = _dot_nt if b_transposed else _dot
    grid = (n // tn, m // tm, nk)
    n_in = 2 + has_c

    def body(*refs):
        a_ref, b_ref = refs[0], refs[1]
        c_ref = refs[2] if has_c else None
        rest = refs[n_in:]
        if ride is not None:
            rest, exchange = ride.split(rest, 1)
            ride.run(grid, exchange)
        o_ref = rest[0]
        acc_ref = rest[1] if nk > 1 else None

        def finish(r):
            if has_c:
                r = r + c_ref[...]
            o_ref[...] = r.astype(o_ref.dtype)

        if nk == 1:
            finish(dot(a_ref[...], b_ref[...]))
        else:
            kk = pl.program_id(2)

            @pl.when(kk == 0)
            def _():
                acc_ref[...] = jnp.zeros_like(acc_ref)

            acc_ref[...] += dot(a_ref[...], b_ref[...])

            @pl.when(kk == nk - 1)
            def _():
                finish(acc_ref[...])

    in_specs = [pl.BlockSpec((tm, tk), lambda j, i, kk: (i, kk)),
                pl.BlockSpec((tn, tk), lambda j, i, kk: (j, kk)) if b_transposed
                else pl.BlockSpec((tk, tn), lambda j, i, kk: (kk, j))]
    args = [a, b]
    aliases = {}
    if has_c:
        in_specs.append(pl.BlockSpec((tm, tn), lambda j, i, kk: (i, j)))
        args.append(c_in)
        aliases = {2: 0}
    out_shape = [jax.ShapeDtypeStruct((m, n), out_dtype)]
    out_specs = [pl.BlockSpec((tm, tn), lambda j, i, kk: (i, j))]
    scratch = [pltpu.VMEM((tm, tn), F32)] if nk > 1 else []
    if ride is not None:
        in_specs, args = in_specs + ride.in_specs, args + ride.args
        out_shape, out_specs, scratch = out_shape + ride.out_shape, out_specs + ride.out_specs, scratch + ride.scratch
    res = pl.pallas_call(
        body, name=name, out_shape=out_shape, grid=grid, in_specs=in_specs, out_specs=out_specs,
        scratch_shapes=scratch, input_output_aliases=aliases, compiler_params=_params(),
    )(*args)
    return res[0] if ride is None else res


EPILOGUE_ROWS = 272


class _Epilogue:
    def __init__(self, fn, *, rows=(), consts=(), pos=(), outs=(), accs=(), lp=None):
        self.fn, self.rows, self.consts, self.pos = fn, list(rows), list(consts), list(pos)
        self.outs, self.accs, self.lp = list(outs), list(accs), lp


def _matmul_segments(name, a_list, b, *, out_dtype=F32, tm, tn, tk, ride=None, b_transposed=False, epilogue=None):
    m = a_list[0].shape[0]
    n, k = b.shape if b_transposed else b.shape[::-1]
    steps = [a.shape[1] // tk for a in a_list]
    offs = [sum(steps[:s]) for s in range(len(steps))]
    nk = sum(steps)
    assert nk * tk == k and m % tm == 0 and n % tn == 0 and all(a.shape[1] % tk == 0 for a in a_list), name
    grid = (n // tn, m // tm, nk)
    n_seg = len(a_list)
    dot = _dot_nt if b_transposed else _dot
    ep = epilogue
    assert ep is None or tn == n, name
    n_extra = 0 if ep is None else len(ep.rows) + len(ep.consts) + len(ep.pos)
    n_outs = 1 if ep is None else len(ep.outs) + len(ep.accs)

    def body(*refs):
        a_refs, b_ref = refs[:n_seg], refs[n_seg]
        extra_refs, rest = refs[n_seg + 1:n_seg + 1 + n_extra], refs[n_seg + 1 + n_extra:]
        if ride is not None:
            rest, exchange = ride.split(rest, n_outs)
            ride.run(grid, exchange)
        out_refs, acc_ref = rest[:n_outs], rest[n_outs]
        i, kk = pl.program_id(1), pl.program_id(2)

        @pl.when(kk == 0)
        def _():
            acc_ref[...] = jnp.zeros_like(acc_ref)

        for s in range(n_seg):
            @pl.when((kk >= offs[s]) & (kk < offs[s] + steps[s]))
            def _(s=s):
                acc_ref[...] += dot(a_refs[s][...], b_ref[...])

        if ep is None:
            @pl.when(kk == nk - 1)
            def _():
                out_refs[0][...] = acc_ref[...].astype(out_refs[0].dtype)
        else:
            sum_refs = out_refs[len(ep.outs):]

            @pl.when((kk == 0) & (i == 0))
            def _():
                for ref in sum_refs:
                    ref[...] = jnp.zeros_like(ref)

            @pl.when(kk == nk - 1)
            def _():
                rs = EPILOGUE_ROWS if tm % EPILOGUE_ROWS == 0 else tm
                n_r, n_c = len(ep.rows), len(ep.consts)
                for r0 in range(0, tm, rs):
                    sl = slice(r0, r0 + rs)
                    tiles = ([r[sl, :] for r in extra_refs[:n_r]] + [c[...] for c in extra_refs[n_r:n_r + n_c]]
                             + [t[sl, :] for t in extra_refs[n_r + n_c:]])
                    res_outs, res_sums = ep.fn(acc_ref[sl, :], *tiles)
                    for ref, val in zip(out_refs[:len(ep.outs)], res_outs, strict=True):
                        ref[sl, :] = val.astype(ref.dtype)
                    for ref, val in zip(sum_refs, res_sums, strict=True):
                        ref[...] += val.reshape(rs // _SUBLANES, _SUBLANES, val.shape[-1]).sum(axis=0)

    seg_spec = lambda s: pl.BlockSpec(
        (tm, tk), functools.partial(lambda j, i, kk, off, ns: (i, jnp.clip(kk - off, 0, ns - 1)), off=offs[s], ns=steps[s]))
    b_spec = (pl.BlockSpec((tn, tk), lambda j, i, kk: (j, kk)) if b_transposed
              else pl.BlockSpec((tk, tn), lambda j, i, kk: (kk, j)))
    in_specs = [seg_spec(s) for s in range(n_seg)] + [b_spec]
    args = list(a_list) + [b]
    row_spec = lambda w: pl.BlockSpec((tm, w), lambda j, i, kk: (i, 0))
    if ep is None:
        out_shape = [jax.ShapeDtypeStruct((m, n), out_dtype)]
        out_specs = [pl.BlockSpec((tm, tn), lambda j, i, kk: (i, j))]
    else:
        tiles_per_example = ep.lp // tm
        row_ins = [r if isinstance(r, tuple) else (r, r.shape[1], 0) for r in ep.rows]
        in_specs += ([pl.BlockSpec((tm, wd), functools.partial(lambda j, i, kk, cb: (i, cb), cb=cb)) for _, wd, cb in row_ins]
                     + [pl.BlockSpec(c.shape, lambda j, i, kk: (0, 0)) for c in ep.consts]
                     + [pl.BlockSpec((tm, p.shape[1]), lambda j, i, kk: (i % tiles_per_example, 0)) for p in ep.pos])
        args += [arr for arr, _, _ in row_ins] + ep.consts + ep.pos
        out_shape = ([jax.ShapeDtypeStruct((m, w), dt) for w, dt in ep.outs]
                     + [jax.ShapeDtypeStruct((_SUBLANES, w), F32) for w in ep.accs])
        out_specs = ([row_spec(w) for w, _ in ep.outs]
                     + [pl.BlockSpec((_SUBLANES, w), lambda j, i, kk: (0, 0)) for w in ep.accs])
    scratch = [pltpu.VMEM((tm, tn), F32)]
    if ride is not None:
        in_specs, args = in_specs + ride.in_specs, args + ride.args
        out_shape, out_specs, scratch = out_shape + ride.out_shape, out_specs + ride.out_specs, scratch + ride.scratch
    res = pl.pallas_call(
        body, name=name, out_shape=out_shape, grid=grid, in_specs=in_specs, out_specs=out_specs,
        scratch_shapes=scratch, compiler_params=_params(),
    )(*args)
    if ep is None:
        return res[0] if ride is None else res
    n_o = len(ep.outs)
    return (res[:n_o], res[n_o:n_outs], *res[n_outs:])


def _matmul_tn(name, x, dy, *, tk, tn, tr, out_dtype=F32):
    r, k = x.shape
    _, n = dy.shape
    assert r % tr == 0 and k % tk == 0 and n % tn == 0, (name, x.shape, dy.shape)
    n_r = r // tr
    direct = out_dtype == F32

    def body(x_ref, dy_ref, o_ref, *scratch):
        acc_ref = o_ref if direct else scratch[0]

        @pl.when(pl.program_id(2) == 0)
        def _():
            acc_ref[...] = jnp.zeros_like(acc_ref)

        acc_ref[...] += _dot_tn(x_ref[...], dy_ref[...])
        if not direct:
            @pl.when(pl.program_id(2) == n_r - 1)
            def _():
                o_ref[...] = acc_ref[...].astype(o_ref.dtype)

    return pl.pallas_call(
        body, name=name,
        out_shape=jax.ShapeDtypeStruct((k, n), out_dtype),
        grid=(k // tk, n // tn, n_r),
        in_specs=[pl.BlockSpec((tr, tk), lambda kb, nb, rr: (rr, kb)),
                  pl.BlockSpec((tr, tn), lambda kb, nb, rr: (rr, nb))],
        out_specs=pl.BlockSpec((tk, tn), lambda kb, nb, rr: (kb, nb)),
        scratch_shapes=[] if direct else [pltpu.VMEM((tk, tn), F32)],
        compiler_params=_params(),
    )(x, dy)


def _ffn_in_swiglu(u, w, *, tm, tn):
    r, k = u.shape
    h = w.shape[1] // 2
    assert r % tm == 0 and h % tn == 0
    nj = h // tn

    def body(u_ref, wg_ref, wu_ref, act_ref, gt_ref, up_ref):
        uu = u_ref[...]
        gt, up = _dot(uu, wg_ref[...]), _dot(uu, wu_ref[...])
        act_ref[...] = (gt * _sigmoid(gt) * up).astype(act_ref.dtype)
        gt_ref[...] = gt.astype(gt_ref.dtype)
        up_ref[...] = up.astype(up_ref.dtype)

    tile = pl.BlockSpec((tm, tn), lambda j, i: (i, j))
    return pl.pallas_call(
        body, name="ffn_in_swiglu",
        out_shape=[jax.ShapeDtypeStruct((r, h), _MXU_DTYPE)] * 3,
        grid=(nj, r // tm),
        in_specs=[pl.BlockSpec((tm, k), lambda j, i: (i, 0)),
                  pl.BlockSpec((k, tn), lambda j, i: (0, j)),
                  pl.BlockSpec((k, tn), lambda j, i: (0, nj + j))],
        out_specs=[tile] * 3,
        compiler_params=_params(),
    )(u, w, w)


def _d_ffn_out_swiglu(dy, w, gt, up, *, tm, tn):
    r, k = dy.shape
    h = w.shape[0]
    assert r % tm == 0 and h % tn == 0

    def body(dy_ref, w_ref, gt_ref, up_ref, dgt_ref, dup_ref):
        da = _dot_nt(dy_ref[...], w_ref[...])
        g, u_ = gt_ref[...].astype(F32), up_ref[...].astype(F32)
        s = _sigmoid(g)
        dgt_ref[...] = (da * u_ * _silu_grad(g, s)).astype(dgt_ref.dtype)
        dup_ref[...] = (da * g * s).astype(dup_ref.dtype)

    tile = pl.BlockSpec((tm, tn), lambda j, i: (i, j))
    return pl.pallas_call(
        body, name="d_ffn_out_swiglu",
        out_shape=[jax.ShapeDtypeStruct((r, h), _MXU_DTYPE)] * 2,
        grid=(h // tn, r // tm),
        in_specs=[pl.BlockSpec((tm, k), lambda j, i: (i, 0)), pl.BlockSpec((tn, k), lambda j, i: (j, 0)), tile, tile],
        out_specs=[tile] * 2,
        compiler_params=_params(),
    )(dy, w, gt, up)


def _proj_q_rope(cqn, w_q, c_tab, s_tab, *, tm, lp):
    r, k = cqn.shape
    tiles_per_example = lp // tm
    pair = 2 * HEAD_DIM

    def body(x_ref, wn_ref, wp_ref, ws_ref, c_ref, s_ref, o_ref):
        x = x_ref[...]
        c2, s2 = jnp.tile(c_ref[...], (1, 2)), jnp.tile(s_ref[...], (1, 2))
        nope = _dot(x, wn_ref[...])
        roped = _dot(x, wp_ref[...]) * c2 + _dot(x, ws_ref[...]) * s2
        hs = lambda t, h: t[:, h * HEAD_DIM:(h + 1) * HEAD_DIM]
        o_ref[...] = jnp.concatenate([hs(nope, 0), hs(roped, 0), hs(nope, 1), hs(roped, 1)], axis=1).astype(o_ref.dtype)

    w_blk = lambda part: pl.BlockSpec((k, pair), functools.partial(lambda h, i, part: (0, part * (HEADS // 2) + h), part=part))
    tab = pl.BlockSpec((tm, HEAD_DIM), lambda h, i: (i % tiles_per_example, 0))
    return pl.pallas_call(
        body, name="proj_q_rope",
        out_shape=jax.ShapeDtypeStruct((r, HEADS * QK_DIM), _MXU_DTYPE),
        grid=(HEADS // 2, r // tm),
        in_specs=[pl.BlockSpec((tm, k), lambda h, i: (i, 0)), w_blk(0), w_blk(1), w_blk(2), tab, tab],
        out_specs=pl.BlockSpec((tm, 2 * QK_DIM), lambda h, i: (i, h)),
        compiler_params=_params(),
    )(cqn, w_q, w_q, w_q, c_tab, s_tab)


def _rowwise(name, body, *, rows, tr, lp, ins, outs, accs=()):
    assert rows % tr == 0 and lp % tr == 0 and tr % 16 == 0
    tiles_per_example = lp // tr
    in_specs, arrays = [], []
    for spec in ins:
        if spec[0] == "row":
            _, arr, width, cb = spec
            in_specs.append(pl.BlockSpec((tr, width), functools.partial(lambda i, cb: (i, cb), cb=cb)))
        elif spec[0] == "const":
            arr = spec[1]
            in_specs.append(pl.BlockSpec(arr.shape, lambda i: (0, 0)))
        else:
            arr = spec[1]
            in_specs.append(pl.BlockSpec((tr, arr.shape[1]), lambda i: (i % tiles_per_example, 0)))
        arrays.append(arr)
    n_in, n_out = len(ins), len(outs)

    def kern(*refs):
        res_outs, res_accs = body(*[r[...] for r in refs[:n_in]])
        for ref, val in zip(refs[n_in:n_in + n_out], res_outs, strict=True):
            ref[...] = val.astype(ref.dtype)
        acc_refs = refs[n_in + n_out:]
        if acc_refs:
            @pl.when(pl.program_id(0) == 0)
            def _():
                for ref in acc_refs:
                    ref[...] = jnp.zeros_like(ref)

            for ref, val in zip(acc_refs, res_accs, strict=True):
                ref[...] += val.reshape(tr // _SUBLANES, _SUBLANES, val.shape[-1]).sum(axis=0)

    out_shape = ([jax.ShapeDtypeStruct((rows, w), dt) for w, dt in outs]
                 + [jax.ShapeDtypeStruct((_SUBLANES, w), F32) for w in accs])
    out_specs = ([pl.BlockSpec((tr, w), lambda i: (i, 0)) for w, _ in outs]
                 + [pl.BlockSpec((_SUBLANES, w), lambda i: (0, 0)) for w in accs])
    res = pl.pallas_call(
        kern, name=name, out_shape=out_shape, grid=(rows // tr,),
        in_specs=in_specs, out_specs=out_specs, compiler_params=_params(),
    )(*arrays)
    return res[:n_out], list(res[n_out:])


def _assemble(name, x, head_rows, lp):
    batch, seq, d = x.shape
    tc = 256

    def body(x_ref, m_ref, o_ref):
        o_ref[0:N_META, :] = m_ref[...]
        o_ref[N_META:N_META + seq, :] = x_ref[0]
        if lp > N_META + seq:
            o_ref[N_META + seq:, :] = jnp.zeros((lp - N_META - seq, tc), F32)

    return pl.pallas_call(
        body, name=name,
        out_shape=jax.ShapeDtypeStruct((batch * lp, d), F32),
        grid=(batch, d // tc),
        in_specs=[pl.BlockSpec((1, seq, tc), lambda b, j: (b, 0, j)),
                  pl.BlockSpec((N_META, tc), lambda b, j: (0, j))],
        out_specs=pl.BlockSpec((lp, tc), lambda b, j: (b, j)),
        compiler_params=_params(),
    )(x, head_rows)


def _meta_grad(dh0, batch, lp):
    d = dh0.shape[1]

    def body(g_ref, o_ref):
        @pl.when(pl.program_id(0) == 0)
        def _():
            o_ref[...] = jnp.zeros_like(o_ref)

        o_ref[...] += g_ref[...]

    return pl.pallas_call(
        body, name="meta_grad",
        out_shape=jax.ShapeDtypeStruct((N_META, d), F32),
        grid=(batch,),
        in_specs=[pl.BlockSpec((N_META, d), lambda b: (b * (lp // N_META), 0))],
        out_specs=pl.BlockSpec((N_META, d), lambda b: (0, 0)),
        compiler_params=_params(),
    )(dh0)


def _segment_masks():
    t = lax.broadcasted_iota(jnp.int32, (HG_BLOCK, HG_BLOCK), 0)
    s = lax.broadcasted_iota(jnp.int32, (HG_BLOCK, HG_BLOCK), 1)
    same = lax.shift_right_logical(t, 4) == lax.shift_right_logical(s, 4)
    lower = same & (s <= t)
    upper = same & (s >= t)
    first_half = same & ((s & 15) <= 7)
    return same, lower, upper, first_half


def _hgrn_gates(hq, hf, lb):
    sq = _sigmoid(hq)
    q = hq * sq
    sf = _sigmoid(hf)
    f = lb + (1.0 - lb) * sf
    return q, sq, sf, f


def _hgrn_decays(g, same, lower, first_half):
    b = _exact_dot(lower, g)
    b_last = _exact_dot(same, g)
    b_ref = _exact_dot(first_half, g)
    return b, b_last, b_ref


def _hgrn_fwd(p, lb, gh, *, batch, lp, ride=None):
    rows = batch * lp
    nb = lp // HG_BLOCK
    n_chunks = HG_BLOCK // HG_CHUNK

    def body(hq_ref, hf_ref, hi_ref, hg_ref, lb_ref, gh_ref, *rest):
        if ride is not None:
            rest, exchange = ride.split(rest, 3)
            ride.run((batch, nb), exchange)
        o_ref, z_ref, st_ref, s_scr = rest

        @pl.when(pl.program_id(1) == 0)
        def _():
            s_scr[...] = jnp.zeros_like(s_scr)

        same, lower, _, first_half = _segment_masks()
        v = hi_ref[...]
        q, _, _, f = _hgrn_gates(hq_ref[...], hf_ref[...], lb_ref[...])
        k = 1.0 - f
        b, b_last, b_ref = _hgrn_decays(jnp.log(f), same, lower, first_half)
        qt = _mx(q * jnp.exp(b))
        kh = _mx(k * jnp.exp(b_last - b))
        vm = _mx(v)
        el = jnp.exp(b_last)
        qc = _mx(q * jnp.exp(b - b_ref))
        kc = _mx(k * jnp.exp(b_ref - b))

        def intra(qc_h, kc_h, v_h):
            a = jnp.where(lower, _dot_nt(qc_h, kc_h), 0.0)
            return _dot(_mx(a), v_h)

        o_intra = _heads(intra, qc, kc, vm)

        states = [s_scr[h] for h in range(HEADS)]
        o_inter = [[None] * HEADS for _ in range(n_chunks)]
        for c in range(n_chunks):
            rs = slice(c * HG_CHUNK, (c + 1) * HG_CHUNK)
            for h in range(HEADS):
                cs = slice(h * HEAD_DIM, (h + 1) * HEAD_DIM)
                st_m = _mx(states[h])
                st_ref[c, h] = st_m
                o_inter[c][h] = _dot_nt(qt[rs, cs], st_m)
                states[h] = states[h] * el[c * HG_CHUNK:c * HG_CHUNK + 1, cs] + _dot_tn(vm[rs, cs], kh[rs, cs])
        for h in range(HEADS):
            s_scr[h] = states[h]

        o = o_intra + jnp.concatenate([jnp.concatenate(row, axis=1) for row in o_inter], axis=0)
        o_ref[...] = o
        hg = hg_ref[...]
        n = _heads(lambda o_h: o_h * _rms_scale(o_h), o) * gh_ref[...]
        z_ref[...] = (n * hg * _sigmoid(hg)).astype(z_ref.dtype)

    blk = lambda cb: pl.BlockSpec((HG_BLOCK, D_MODEL), functools.partial(lambda b, j, cb: (b * nb + j, cb), cb=cb))
    row_out = pl.BlockSpec((HG_BLOCK, D_MODEL), lambda b, j: (b * nb + j, 0))
    const = pl.BlockSpec((1, D_MODEL), lambda b, j: (0, 0))
    extra = ride if ride is not None else _NoRide
    return pl.pallas_call(
        body, name="hgrn_fwd",
        out_shape=[jax.ShapeDtypeStruct((rows, D_MODEL), F32),
                   jax.ShapeDtypeStruct((rows, D_MODEL), _MXU_DTYPE),
                   jax.ShapeDtypeStruct((rows // HG_CHUNK, HEADS, HEAD_DIM, HEAD_DIM), _MXU_DTYPE)] + extra.out_shape,
        grid=(batch, nb),
        in_specs=[blk(CB_HQ), blk(CB_HF), blk(CB_HI), blk(CB_HG), const, const] + extra.in_specs,
        out_specs=[row_out, row_out,
                   pl.BlockSpec((n_chunks, HEADS, HEAD_DIM, HEAD_DIM), lambda b, j: (b * nb + j, 0, 0, 0))]
        + extra.out_specs,
        scratch_shapes=[pltpu.VMEM((HEADS, HEAD_DIM, HEAD_DIM), F32)] + extra.scratch,
        compiler_params=_params(),
    )(p, p, p, p, lb, gh, *extra.args)


def _hgrn_bwd(p, o, dz, states, lb, gh, *, batch, lp, ride=None):
    rows = batch * lp
    nb = lp // HG_BLOCK
    n_chunks = HG_BLOCK // HG_CHUNK

    def body(hq_ref, hf_ref, hi_ref, hg_ref, o_ref, dz_ref, st_ref, lb_ref, gh_ref, *rest):
        if ride is not None:
            rest, exchange = ride.split(rest, 3)
            ride.run((batch, nb), exchange)
        dp_ref, dlb_ref, dgh_ref, ds_scr = rest
        first = (pl.program_id(0) == 0) & (pl.program_id(1) == 0)

        @pl.when(first)
        def _():
            dlb_ref[...] = jnp.zeros_like(dlb_ref)
            dgh_ref[...] = jnp.zeros_like(dgh_ref)

        @pl.when(pl.program_id(1) == 0)
        def _():
            ds_scr[...] = jnp.zeros_like(ds_scr)

        same, lower, upper, first_half = _segment_masks()
        lbv = lb_ref[...]
        hq, hf, v, hg = hq_ref[...], hf_ref[...], hi_ref[...], hg_ref[...]
        q, sq, sf, f = _hgrn_gates(hq, hf, lbv)
        k = 1.0 - f
        b, b_last, b_ref = _hgrn_decays(jnp.log(f), same, lower, first_half)
        e_b = jnp.exp(b)
        e_kh = jnp.exp(b_last - b)
        e_qc = jnp.exp(b - b_ref)
        e_kc = jnp.exp(b_ref - b)
        qt, kh, qc, kc = q * e_b, k * e_kh, q * e_qc, k * e_kc

        o = o_ref[...]
        dz = dz_ref[...].astype(F32)
        ghv = gh_ref[...]
        sg = _sigmoid(hg)
        r = _heads(lambda o_h: jnp.broadcast_to(_rms_scale(o_h), o_h.shape), o)
        oh = o * r
        dn = dz * hg * sg
        dhg = dz * oh * ghv * _silu_grad(hg, sg)
        w = dn * ghv
        do = r * (w - oh * _heads(lambda t: jnp.broadcast_to(jnp.mean(t, axis=-1, keepdims=True), t.shape), oh * w))
        dgh_ref[...] += (dn * oh).reshape(HG_BLOCK // _SUBLANES, _SUBLANES, D_MODEL).sum(axis=0)

        qt_m, kh_m, v_m, do_m = _mx(qt), _mx(kh), _mx(v), _mx(do)
        el_all = jnp.exp(b_last)

        def intra(qc_h, kc_h, v_h, do_h):
            a = _mx(jnp.where(lower, _dot_nt(qc_h, kc_h), 0.0))
            da = _mx(jnp.where(lower, _dot_nt(do_h, v_h), 0.0))
            return _dot(da, kc_h), _dot_tn(da, qc_h), _dot_tn(a, do_h)

        dqc, dkc, dv_intra = _heads(intra, _mx(qc), _mx(kc), v_m, do_m)

        d_states = [ds_scr[h] for h in range(HEADS)]
        grid_of = lambda: [[None] * HEADS for _ in range(n_chunks)]
        dkh_p, dv_p, dbl_p, dqt_p = grid_of(), grid_of(), grid_of(), grid_of()
        for c in reversed(range(n_chunks)):
            rs = slice(c * HG_CHUNK, (c + 1) * HG_CHUNK)
            for h in range(HEADS):
                cs = slice(h * HEAD_DIM, (h + 1) * HEAD_DIM)
                st = st_ref[c, h]
                ds_t = d_states[h]
                ds_m = _mx(ds_t)
                el = el_all[c * HG_CHUNK:c * HG_CHUNK + 1, cs]
                dkh_p[c][h] = _dot(v_m[rs, cs], ds_m)
                dv_p[c][h] = _dot_nt(kh_m[rs, cs], ds_m)
                dbl = jnp.sum(ds_t * st.astype(F32), axis=0, keepdims=True) * el
                dbl_p[c][h] = jnp.broadcast_to(dbl, (HG_CHUNK, HEAD_DIM))
                dqt_p[c][h] = _dot(do_m[rs, cs], st)
                d_states[h] = ds_t * el + _dot_tn(do_m[rs, cs], qt_m[rs, cs])
        for h in range(HEADS):
            ds_scr[h] = d_states[h]
        whole = lambda parts: jnp.concatenate([jnp.concatenate(row, axis=1) for row in parts], axis=0)

        dqt, dkh = whole(dqt_p), whole(dkh_p)
        dq = dqt * e_b + dqc * e_qc
        dk = dkh * e_kh + dkc * e_kc
        t_kh = dkh * kh
        db_rows = dqt * qt + dqc * qc - dkc * kc - t_kh
        dg = _exact_dot(upper, db_rows) + _exact_dot(same, t_kh) + whole(dbl_p)
        df = dg / f - dk
        dhf = df * (1.0 - lbv) * sf * (1.0 - sf)
        dlb_ref[...] += (df * (1.0 - sf)).reshape(HG_BLOCK // _SUBLANES, _SUBLANES, D_MODEL).sum(axis=0)
        dhq = dq * _silu_grad(hq, sq)
        dp_ref[...] = jnp.concatenate([dhq, dhf, dv_intra + whole(dv_p), dhg], axis=1).astype(dp_ref.dtype)

    rev = lambda b, j: b * nb + (nb - 1 - j)
    blk = lambda cb: pl.BlockSpec((HG_BLOCK, D_MODEL), functools.partial(lambda b, j, cb: (rev(b, j), cb), cb=cb))
    row = pl.BlockSpec((HG_BLOCK, D_MODEL), lambda b, j: (rev(b, j), 0))
    const = pl.BlockSpec((1, D_MODEL), lambda b, j: (0, 0))
    acc = pl.BlockSpec((_SUBLANES, D_MODEL), lambda b, j: (0, 0))
    extra = ride if ride is not None else _NoRide
    dp, dlb, dgh, *exchanged = pl.pallas_call(
        body, name="hgrn_bwd",
        out_shape=[jax.ShapeDtypeStruct((rows, 4 * D_MODEL), _MXU_DTYPE),
                   jax.ShapeDtypeStruct((_SUBLANES, D_MODEL), F32),
                   jax.ShapeDtypeStruct((_SUBLANES, D_MODEL), F32)] + extra.out_shape,
        grid=(batch, nb),
        in_specs=[blk(CB_HQ), blk(CB_HF), blk(CB_HI), blk(CB_HG), row, row,
                  pl.BlockSpec((n_chunks, HEADS, HEAD_DIM, HEAD_DIM), lambda b, j: (rev(b, j), 0, 0, 0)),
                  const, const] + extra.in_specs,
        out_specs=[pl.BlockSpec((HG_BLOCK, 4 * D_MODEL), lambda b, j: (rev(b, j), 0)), acc, acc] + extra.out_specs,
        scratch_shapes=[pltpu.VMEM((HEADS, HEAD_DIM, HEAD_DIM), F32)] + extra.scratch,
        compiler_params=_params(),
    )(p, p, p, p, o, dz, states, lb, gh, *extra.args)
    return (dp, dlb, dgh, *exchanged)


QK_DIM = 2 * HEAD_DIM
ATTN_TQ = 256
ATTN_KEY_CHUNK = 512


def _query_tiles(lp):
    return [(r0, min(ATTN_TQ, lp - r0)) for r0 in range(0, lp, ATTN_TQ)]


def _attn_fwd(q_cat, kv, kp, *, batch, lp):
    rows = batch * lp

    def body(q_ref, kn_ref, kp_ref, v_ref, o_ref, lse_ref):
        k_cat = jnp.concatenate([kn_ref[...], kp_ref[...]], axis=1)
        for r0, tq in _query_tiles(lp):
            q_t = q_ref[r0:r0 + tq, :]
            i = lax.broadcasted_iota(jnp.int32, (tq, tq), 0)
            j = lax.broadcasted_iota(jnp.int32, (tq, tq), 1)
            s_diag = jnp.where(j <= i, _dot_nt(q_t, k_cat[r0:r0 + tq]) * ATTN_SCALE, NEG_BIG)
            m = jnp.max(s_diag, axis=1, keepdims=True)
            if r0:
                s_past = _dot_nt(q_t, k_cat[0:r0]) * ATTN_SCALE
                m = jnp.maximum(m, jnp.max(s_past, axis=1, keepdims=True))
            p_diag = jnp.exp(s_diag - m)
            l = jnp.sum(p_diag, axis=1, keepdims=True)
            acc = _dot(_mx(p_diag), v_ref[r0:r0 + tq, :])
            if r0:
                p_past = jnp.exp(s_past - m)
                l = l + jnp.sum(p_past, axis=1, keepdims=True)
                acc = acc + _dot(_mx(p_past), v_ref[0:r0, :])
            o_ref[r0:r0 + tq, :] = (acc / l).astype(o_ref.dtype)
            lse_ref[r0:r0 + tq, :] = jnp.broadcast_to(m + jnp.log(l), (tq, HEAD_DIM))

    head_blk = pl.BlockSpec((lp, HEAD_DIM), lambda b, h: (b, h))
    return pl.pallas_call(
        body, name="attn_fwd",
        out_shape=[jax.ShapeDtypeStruct((rows, D_MODEL), _MXU_DTYPE),
                   jax.ShapeDtypeStruct((rows, D_MODEL), F32)],
        grid=(batch, HEADS),
        in_specs=[pl.BlockSpec((lp, QK_DIM), lambda b, h: (b, h)), head_blk,
                  pl.BlockSpec((lp, HEAD_DIM), lambda b, h: (b, 0)),
                  pl.BlockSpec((lp, HEAD_DIM), lambda b, h: (b, HEADS + h))],
        out_specs=[head_blk, head_blk],
        compiler_params=_params(),
    )(q_cat, kv, kp, kv)


def _attn_bwd(q_cat, kv, kp, do, o, lse, c_tab, s_tab, *, batch, lp, ride=None):
    rows = batch * lp

    def body(q_ref, kn_ref, kp_ref, v_ref, do_ref, o_ref, lse_ref, c_ref, s_ref, *rest):
        if ride is not None:
            rest, exchange = ride.split(rest, 6)
            ride.run((batch, HEADS), exchange)
        dqn_ref, dqc_ref, dqs_ref, dkn_ref, dkp_ref, dv_ref, dk_acc, dv_acc = rest
        dk_acc[...] = jnp.zeros_like(dk_acc)
        dv_acc[...] = jnp.zeros_like(dv_acc)
        k_cat = jnp.concatenate([kn_ref[...], kp_ref[...]], axis=1)
        k_t = k_cat.T
        lane = lax.broadcasted_iota(jnp.int32, (_SUBLANES, HEAD_DIM), 1)
        lse_row = _exact_dot_nt(lane == 0, lse_ref[...])
        delta = _exact_dot_nt(lane >= 0, do_ref[...].astype(F32) * o_ref[...].astype(F32))
        for r0, tq in _query_tiles(lp):
            cols = slice(r0, r0 + tq)
            q_t_, do_t_ = q_ref[cols, :], do_ref[cols, :]
            lse_t, delta_t = lse_row[0:1, cols], delta[0:1, cols]
            chunks = [(c0, min(ATTN_KEY_CHUNK, r0 - c0), False) for c0 in range(0, r0, ATTN_KEY_CHUNK)] + [(r0, tq, True)]
            dq_t = jnp.zeros((QK_DIM, tq), F32)
            for c0, n, diagonal in chunks:
                keys = slice(c0, c0 + n)
                s = _dot_nt(k_cat[keys], q_t_) * ATTN_SCALE
                if diagonal:
                    jk = lax.broadcasted_iota(jnp.int32, (n, tq), 0)
                    iq = lax.broadcasted_iota(jnp.int32, (n, tq), 1)
                    s = jnp.where(jk <= iq, s, NEG_BIG)
                pexp = jnp.exp(s - lse_t)
                dp = _dot_nt(v_ref[keys, :], do_t_)
                ds = _mx(pexp * (dp - delta_t) * ATTN_SCALE)
                dk_acc[keys, :] += _dot(ds, q_t_)
                dv_acc[keys, :] += _dot(_mx(pexp), do_t_)
                dq_t = dq_t + _dot(k_t[:, keys], ds)
            dq = dq_t.T
            d_rope = dq[:, HEAD_DIM:]
            dqn_ref[cols, :] = dq[:, :HEAD_DIM].astype(dqn_ref.dtype)
            dqc_ref[cols, :] = (d_rope * c_ref[cols, :]).astype(dqc_ref.dtype)
            dqs_ref[cols, :] = (d_rope * s_ref[cols, :]).astype(dqs_ref.dtype)

        dkn_ref[...] = dk_acc[:, 0:HEAD_DIM].astype(dkn_ref.dtype)
        dv_ref[...] = dv_acc[...].astype(dv_ref.dtype)

        @pl.when(pl.program_id(1) == 0)
        def _():
            dkp_ref[...] = jnp.zeros_like(dkp_ref)

        dkp_ref[...] += dk_acc[:, HEAD_DIM:]

    head_blk = pl.BlockSpec((lp, HEAD_DIM), lambda b, h: (b, h))
    cat_blk = pl.BlockSpec((lp, QK_DIM), lambda b, h: (b, h))
    shared_blk = pl.BlockSpec((lp, HEAD_DIM), lambda b, h: (b, 0))
    table_blk = pl.BlockSpec((lp, HEAD_DIM), lambda b, h: (0, 0))
    extra = ride if ride is not None else _NoRide
    return pl.pallas_call(
        body, name="attn_bwd",
        out_shape=[jax.ShapeDtypeStruct((rows, D_MODEL), _MXU_DTYPE)] * 3 + [
                   jax.ShapeDtypeStruct((rows, D_MODEL), _MXU_DTYPE),
                   jax.ShapeDtypeStruct((rows, HEAD_DIM), F32),
                   jax.ShapeDtypeStruct((rows, D_MODEL), _MXU_DTYPE)] + extra.out_shape,
        grid=(batch, HEADS),
        in_specs=[cat_blk, head_blk, shared_blk, pl.BlockSpec((lp, HEAD_DIM), lambda b, h: (b, HEADS + h)),
                  head_blk, head_blk, head_blk, table_blk, table_blk] + extra.in_specs,
        out_specs=[head_blk, head_blk, head_blk, head_blk, shared_blk, head_blk] + extra.out_specs,
        scratch_shapes=[pltpu.VMEM((lp, QK_DIM), F32), pltpu.VMEM((lp, HEAD_DIM), F32)] + extra.scratch,
        compiler_params=_params(),
    )(q_cat, kv, kp, kv, do, o, lse, c_tab, s_tab, *extra.args)


def _all_gather(name, blocks):
    n = len(blocks)

    def body(*refs):
        x_refs, out_refs, (send_sems, recv_sems, local_sems) = refs[:n], refs[n:2 * n], refs[2 * n:]
        x, y, c = lax.axis_index("x"), lax.axis_index("y"), lax.axis_index("c")
        me, sibling = (x, y, c), (x, y, 1 - c)
        chips = [(1 - x, y), (x, 1 - y), (1 - x, 1 - y)]

        def slot(i, px, py, pc):
            return out_refs[i].at[4 * px + 2 * py + pc]

        def copy(i, k, blk, to, src=None):
            return pltpu.make_async_remote_copy(
                src_ref=slot(i, *blk) if src is None else src, dst_ref=slot(i, *blk),
                send_sem=send_sems.at[i, k], recv_sem=recv_sems.at[i, k],
                device_id=to, device_id_type=pl.DeviceIdType.MESH)

        mine = [pltpu.make_async_copy(x_refs[i], slot(i, *me), local_sems.at[i]) for i in range(n)]
        first = [copy(i, 0, me, sibling, src=x_refs[i]) for i in range(n)]
        first += [copy(i, 1 + j, me, (*chip, c), src=x_refs[i]) for i in range(n) for j, chip in enumerate(chips)]
        for cp in mine + first:
            cp.start()
        passed = []
        for i in range(n):
            for j, chip in enumerate(chips):
                copy(i, 1 + j, (*chip, c), me).wait_recv()
                passed.append(copy(i, 4 + j, (*chip, c), sibling))
                passed[-1].start()
        for i in range(n):
            copy(i, 0, sibling, me).wait_recv()
            for j, chip in enumerate(chips):
                copy(i, 4 + j, (*chip, 1 - c), me).wait_recv()
        for cp in first + passed:
            cp.wait_send()
        for cp in mine:
            cp.wait()

    return pl.pallas_call(
        body, name=name,
        out_shape=[jax.ShapeDtypeStruct((N_DEV, *b.shape), b.dtype) for b in blocks],
        in_specs=[pl.BlockSpec(memory_space=pl.ANY)] * n,
        out_specs=[pl.BlockSpec(memory_space=pl.ANY)] * n,
        scratch_shapes=[pltpu.SemaphoreType.DMA((n, 7)), pltpu.SemaphoreType.DMA((n, 7)),
                        pltpu.SemaphoreType.DMA((n,))],
    )(*blocks)


def _adamw_math(w, g, m, v):
    nm = ADAM_B1 * m + (1.0 - ADAM_B1) * g
    nv = ADAM_B2 * v + (1.0 - ADAM_B2) * (g * g)
    m_hat = nm / (1.0 - ADAM_B1 ** ADAM_STEP)
    v_hat = nv / (1.0 - ADAM_B2 ** ADAM_STEP)
    return -ADAM_LR * (m_hat / (jnp.sqrt(v_hat) + ADAM_EPS) + ADAM_WD * w), nm, nv


def _sum_adamw(name, parts, w, m, v):
    rows, cols = w.shape
    tr = rows // 4 if rows % 64 == 0 and rows * cols > (1 << 16) else rows

    def body(p_ref, w_ref, m_ref, v_ref, g_ref, d_ref, nm_ref, nv_ref):
        g = p_ref[0].astype(F32)
        for dev in range(1, N_DEV):
            g = g + p_ref[dev].astype(F32)
        g_ref[...] = g
        d_ref[...], nm_ref[...], nv_ref[...] = _adamw_math(w_ref[...], g, m_ref[...], v_ref[...])

    spec = pl.BlockSpec((tr, cols), lambda i: (i, 0))
    return pl.pallas_call(
        body, name=name,
        out_shape=[jax.ShapeDtypeStruct((rows, cols), F32)] * 4,
        grid=(rows // tr,),
        in_specs=[pl.BlockSpec((N_DEV, tr, cols), lambda i: (0, i, 0))] + [spec] * 3, out_specs=[spec] * 4,
        compiler_params=_params(),
    )(parts, w, m, v)


def _finish_vectors(gathered, lb, params, loss_parts):
    names = list(params)
    n = len(names)

    def body(*refs):
        g_refs, lb_ref, loss_ref = refs[:n], refs[n], refs[n + 1]
        wmv_refs = refs[n + 2:4 * n + 2]
        out_refs, loss_out = refs[4 * n + 2:-1], refs[-1]
        sq = loss_ref[0]
        for dev in range(1, N_DEV):
            sq = sq + loss_ref[dev]
        sq = jnp.sum(jnp.sum(sq, axis=0, keepdims=True), axis=1, keepdims=True)
        loss_out[...] = sq * (0.5 / D_MODEL)
        me = 4 * lax.axis_index("x") + 2 * lax.axis_index("y") + lax.axis_index("c")
        for i, name in enumerate(names):
            g_ref = g_refs[i]
            w_ref, m_ref, v_ref = wmv_refs[3 * i:3 * i + 3]
            if name == "meta_tokens":
                width = w_ref.shape[1]
                mine = pl.ds(pl.multiple_of(me * width, width), width)
                g = g_ref[0, :, mine]
                for dev in range(1, N_DEV):
                    g = g + g_ref[dev, :, mine]
            else:
                g = g_ref[0]
                for dev in range(1, N_DEV):
                    g = g + g_ref[dev]
                g = jnp.sum(g, axis=0, keepdims=True)
                if name == "hg_norm_g":
                    g = functools.reduce(jnp.add, [g[:, h * HEAD_DIM:(h + 1) * HEAD_DIM] for h in range(HEADS)])
                if name == "lb_logits":
                    lbv = lb_ref[...]
                    g = g * lbv * (1.0 - lbv)
                    g = jnp.concatenate([g, -g], axis=0)
            outs = (g, *_adamw_math(w_ref[...], g, m_ref[...], v_ref[...]))
            for ref, val in zip(out_refs[4 * i:4 * i + 4], outs, strict=True):
                ref[...] = val

    args = [gathered[k] for k in names] + [lb, loss_parts] + [t for k in names for t in params[k]]
    res = pl.pallas_call(
        body, name="finish_vectors",
        out_shape=[jax.ShapeDtypeStruct(params[k][0].shape, F32) for k in names for _ in range(4)]
        + [jax.ShapeDtypeStruct((1, 1), F32)],
        compiler_params=_params(),
    )(*args)
    return {k: res[4 * i:4 * i + 4] for i, k in enumerate(names)}, res[-1].reshape(())


def _swap_halves(t):
    half = t.shape[-1] // 2
    return jnp.concatenate([t[..., half:], t[..., :half]], axis=-1)


def _pad_last(t, width):
    return jnp.concatenate([t, jnp.zeros(t.shape[:-1] + (width - t.shape[-1],), t.dtype)], axis=-1)


def _rope_tables(lp):
    pos = jnp.arange(lp, dtype=F32)
    inv_freq = 1.0 / (ROPE_THETA ** (jnp.arange(0, ROPE_DIM, 2, dtype=F32) / ROPE_DIM))
    ang = pos[:, None] * inv_freq[None, :]
    cos, sin = jnp.cos(ang), jnp.sin(ang)
    c128 = _pad_last(jnp.concatenate([cos, cos], axis=1), HEAD_DIM)
    s128 = _pad_last(jnp.concatenate([-sin, sin], axis=1), HEAD_DIM)
    return c128, s128


def _forward_backward(x, target, meta, w, small, *, lp, comm=None):
    batch, seq, d = x.shape
    rows = batch * lp
    tr = 272 if lp % 272 == 0 else 128
    tm = lp // 2
    bf = _MXU_DTYPE
    rw = functools.partial(_rowwise, rows=rows, tr=tr, lp=lp)

    c128, s128 = _rope_tables(lp)
    t_idx = jnp.arange(lp)
    real = jnp.broadcast_to(((t_idx >= N_META) & (t_idx < N_META + seq)).astype(F32)[:, None], (lp, _LANES))

    lb_logits = small["lb_logits"]
    lb = jax.nn.softmax(lb_logits, axis=0)[0:1]
    gh = jnp.tile(small["hg_norm_g"], (1, HEADS))

    h0 = _assemble("assemble_x", x, meta, lp)
    tgt = _assemble("assemble_target", target, jnp.zeros_like(meta), lp)

    (u1,), _ = rw("norm_mix_pre", lambda h, g: ([h * _rms_scale(h) * g], []),
                  ins=[("row", h0, d, 0), ("const", small["mix_pre_g"])], outs=[(d, bf)])
    p = _matmul("proj_in", u1, w["w_in"], out_dtype=F32, tm=tm, tn=1024, tk=1024)

    if comm is None:
        o_hg, z_a, states = _hgrn_fwd(p, lb, gh, batch=batch, lp=lp)
    else:
        o_hg, z_a, states, *gathered = _hgrn_fwd(p, lb, gh, batch=batch, lp=lp, ride=_Ride(comm.rest_payloads, True))
        w = {**w, **comm.rest_weights(gathered)}
    received = []
    scatter = lambda names: _Ride(comm.grad_parts(names, grads), False) if comm is not None else None
    y_a = _matmul("proj_hg_o", z_a, w["w_hg_o"], out_dtype=F32, tm=tm, tn=1024, tk=1024)

    def mla_pre(pc, gq, gkv, ct, st):
        cq, ckv = pc[:, 0:Q_LORA], pc[:, Q_LORA:Q_LORA + KV_LORA]
        kpe, kpe_sw = pc[:, 512:640], pc[:, 640:768]
        return [cq * _rms_scale(cq) * gq, ckv * _rms_scale(ckv) * gkv, kpe * ct + kpe_sw * st], []

    (cqn, ckvn, kp), _ = rw("mla_pre", mla_pre,
                            ins=[("row", p, 1024, CB_C), ("const", small["q_a_norm_g"]),
                                 ("const", small["kv_a_norm_g"]), ("pos", c128), ("pos", s128)],
                            outs=[(Q_LORA, bf), (KV_LORA, bf), (HEAD_DIM, bf)])
    q_cat = _proj_q_rope(cqn, w["w_q"], c128, s128, tm=tm, lp=lp)
    kv = _matmul("proj_kv_b", ckvn, w["w_kv"], out_dtype=bf, tm=tm, tn=1024, tk=KV_LORA)
    o_at, lse = _attn_fwd(q_cat, kv, kp, batch=batch, lp=lp)
    te, te_small = tm, lp // 4

    def merge(yb, pa, pb, ya, bg):
        ga, gb = _sigmoid(pa + bg[:, :d]), _sigmoid(pb + bg[:, d:])
        return [yb, ga * ya + gb * yb], []

    (y_b, mix), _ = _matmul_segments(
        "proj_mla_o", [o_at], w["w_mla_o"], tm=te_small, tn=d, tk=1024,
        epilogue=_Epilogue(merge, rows=[(p, 1024, CB_GA), (p, 1024, CB_GB), y_a], consts=[small["b_gate"]],
                           outs=[(d, F32), (d, bf)], lp=lp))
    def post_mix(mx_, h, g2, g3):
        h1_ = h + mx_ * _rms_scale(mx_) * g2
        return [mx_, h1_, h1_ * _rms_scale(h1_) * g3], []

    (mixed, h1, u2), _ = _matmul_segments(
        "proj_out", [mix], w["w_out"], tm=te, tn=d, tk=1024,
        epilogue=_Epilogue(post_mix, rows=[h0], consts=[small["mix_post_g"], small["ffn_pre_g"]],
                           outs=[(d, F32), (d, F32), (d, bf)], lp=lp))
    act, gt, up = _ffn_in_swiglu(u2, w["w_ffn_in"], tm=tm, tn=1408)

    def post_ffn(fo_, h1_, t_, g4, mask):
        r = _rms_scale(fo_)
        h2 = h1_ + fo_ * r * g4
        err = (h2 - t_) * mask[:, 0:1]
        dh2 = err * (1.0 / d)
        dfo, dg4 = _rms_bwd(fo_, g4, dh2)
        return [dh2, dfo], [err * err, dg4]

    (dh2, dfo), (loss_vec, dg_ffn_post) = _matmul_segments(
        "ffn_out", [act], w["w_ffn_out"], tm=te, tn=d, tk=1408,
        epilogue=_Epilogue(post_ffn, rows=[h1, tgt], consts=[small["ffn_post_g"]], pos=[real],
                           outs=[(d, F32), (d, bf)], accs=[d, d], lp=lp))
    loss = (0.5 / d) * jnp.sum(loss_vec)

    grads = {}
    dw_dt = F32 if comm is None else _WIRE_DTYPE
    dgt, dup = _d_ffn_out_swiglu(dfo, w["w_ffn_out"], gt, up, tm=tm, tn=1408)
    grads["w_ffn_out"] = _matmul_tn("dw_ffn_out", act, dfo, tk=1408, tn=1024, tr=tm, out_dtype=dw_dt)
    grads["w_ffn_in"] = jnp.concatenate([_matmul_tn("dw_ffn_in_gate", u2, dgt, tk=1024, tn=FFN_HIDDEN, tr=tm),
                                         _matmul_tn("dw_ffn_in_up", u2, dup, tk=1024, tn=FFN_HIDDEN, tr=tm)], axis=1)

    def post_mix_bwd(du2_, h1_, dh2_, mx_, g3, g2):
        dx, dg3 = _rms_bwd(h1_, g3, du2_)
        dh1_ = dh2_ + dx
        dmx, dg2 = _rms_bwd(mx_, g2, dh1_)
        return [dh1_, dmx], [dg3, dg2]

    (dh1, dmixed), (dg_ffn_pre, dg_mix_post) = _matmul_segments(
        "d_ffn_in", [dgt, dup], w["w_ffn_in"], tm=te_small, tn=d, tk=1408, b_transposed=True,
        epilogue=_Epilogue(post_mix_bwd, rows=[h1, dh2, mixed], consts=[small["ffn_pre_g"], small["mix_post_g"]],
                           outs=[(d, F32), (d, bf)], accs=[d, d], lp=lp))
    grads["w_out"] = _matmul_tn("dw_out", mix, dmixed, tk=1024, tn=1024, tr=tm, out_dtype=dw_dt)

    def merge_bwd(dm, pa, pb, ya, yb, bg):
        ga, gb = _sigmoid(pa + bg[:, :d]), _sigmoid(pb + bg[:, d:])
        dpg = jnp.concatenate([dm * ya * ga * (1.0 - ga), dm * yb * gb * (1.0 - gb)], axis=1)
        return [dpg, dm * ga, dm * gb], [dpg]

    (dpg, dya, dyb), (db_gate,) = _matmul_segments(
        "d_proj_out", [dmixed], w["w_out"], tm=te_small, tn=d, tk=1024, b_transposed=True,
        epilogue=_Epilogue(merge_bwd, rows=[(p, 1024, CB_GA), (p, 1024, CB_GB), y_a, y_b], consts=[small["b_gate"]],
                           outs=[(2 * d, bf), (d, bf), (d, bf)], accs=[2 * d], lp=lp))
    dz_a = _matmul("d_proj_hg_o", dya, w["w_hg_o"], out_dtype=F32, tm=tm, tn=1024, tk=1024, b_transposed=True)
    grads["w_hg_o"] = _matmul_tn("dw_hg_o", z_a, dya, tk=1024, tn=1024, tr=tm, out_dtype=dw_dt)
    do_at = _matmul("d_proj_mla_o", dyb, w["w_mla_o"], out_dtype=bf, tm=tm, tn=1024, tk=1024, b_transposed=True)
    grads["w_mla_o"] = _matmul_tn("dw_mla_o", o_at, dyb, tk=1024, tn=1024, tr=tm, out_dtype=dw_dt)

    dph, dlb, dgh, *got = _hgrn_bwd(p, o_hg, dz_a, states, lb, gh, batch=batch, lp=lp,
                                    ride=scatter(_GRAD_GROUPS[0]))
    received.append(got)

    res = _attn_bwd(q_cat, kv, kp, do_at, o_at, lse, c128, s128, batch=batch, lp=lp, ride=scatter(_GRAD_GROUPS[1]))
    dq_parts, (dkn, dkp, dvv) = list(res[:3]), res[3:6]
    received.append(list(res[6:]))
    dcqn = _matmul_segments("d_proj_q_b", dq_parts, w["w_q"], tm=tm, tn=Q_LORA, tk=1024, b_transposed=True)
    grads["w_q"] = jnp.concatenate([_matmul_tn(f"dw_q_b_{i}", cqn, part, tk=Q_LORA, tn=1024, tr=tm)
                                    for i, part in enumerate(dq_parts)], axis=1)
    dckvn = _matmul_segments("d_proj_kv_b", [dkn, dvv], w["w_kv"], tm=tm, tn=KV_LORA, tk=1024, b_transposed=True)
    grads["w_k"] = _matmul_tn("dw_k_b", ckvn, dkn, tk=KV_LORA, tn=1024, tr=tm)
    grads["w_v"] = _matmul_tn("dw_v_b", ckvn, dvv, tk=KV_LORA, tn=1024, tr=tm)

    def mla_pre_bwd(pc, dq_, dkv_, dkp_, gq, gkv, ct, st):
        cq, ckv = pc[:, 0:Q_LORA], pc[:, Q_LORA:Q_LORA + KV_LORA]
        dcq, dgq = _rms_bwd(cq, gq, dq_)
        dckv, dgkv = _rms_bwd(ckv, gkv, dkv_)
        dpc = jnp.concatenate([dcq, dckv, dkp_ * ct, dkp_ * st, jnp.zeros((pc.shape[0], 256), F32)], axis=1)
        return [dpc], [dgq, dgkv]

    (dpc,), (dg_q, dg_kv) = rw(
        "mla_pre_bwd", mla_pre_bwd,
        ins=[("row", p, 1024, CB_C), ("row", dcqn, Q_LORA, 0), ("row", dckvn, KV_LORA, 0), ("row", dkp, HEAD_DIM, 0),
             ("const", small["q_a_norm_g"]), ("const", small["kv_a_norm_g"]), ("pos", c128), ("pos", s128)],
        outs=[(1024, bf)], accs=[Q_LORA, KV_LORA])

    grads["w_in"] = (_matmul_tn("dw_in_h", u1, dph, tk=1024, tn=1024, tr=tm),
                     _matmul_tn("dw_in_c", u1, dpc, tk=1024, tn=1024, tr=tm),
                     _matmul_tn("dw_in_g", u1, dpg, tk=1024, tn=1024, tr=tm))
    def pre_bwd(du, h, dh, g1):
        dx, dg1 = _rms_bwd(h, g1, du)
        return [dh + dx], [dg1]

    (dh0,), (dg_mix_pre,), *got = _matmul_segments(
        "d_proj_in", [dph, dpc, dpg], w["w_in"], tm=te, tn=d, tk=1024, b_transposed=True,
        ride=scatter(_GRAD_GROUPS[2]),
        epilogue=_Epilogue(pre_bwd, rows=[h0, dh1], consts=[small["mix_pre_g"]], outs=[(d, F32)], accs=[d], lp=lp))
    if comm is not None:
        received.append(got)
    grad_x = dh0.reshape(batch, lp, d)[:, N_META:N_META + seq]
    partial = {"meta_tokens": _meta_grad(dh0, batch, lp), "lb_logits": dlb, "b_gate": db_gate, "hg_norm_g": dgh,
               "q_a_norm_g": dg_q, "kv_a_norm_g": dg_kv, "mix_pre_g": dg_mix_pre, "mix_post_g": dg_mix_post,
               "ffn_pre_g": dg_ffn_pre, "ffn_post_g": dg_ffn_post, "loss": loss_vec}
    return loss, grad_x, grads, partial, lb, received


_BIG = ["w_in", "w_hg_o", "w_q_b", "w_kv_b", "w_mla_o", "w_out", "w_ffn_in", "w_ffn_out"]
_COLUMN_SHARDED = {"w_in", "w_q_b", "w_kv_b", "w_ffn_in"}
_GRAD_GROUPS = [["w_ffn_in", "w_ffn_out"], ["w_out", "w_hg_o", "w_mla_o"], ["w_in", "w_q_b", "w_kv_b"]]
_SMALL = ["b_gate", "lb_logits", "hg_norm_g", "q_a_norm_g", "kv_a_norm_g", "mix_pre_g", "mix_post_g",
          "ffn_pre_g", "ffn_post_g"]


def _gathered_matrix(name, t):
    _, k, n = t.shape
    if name in _COLUMN_SHARDED:
        return t.transpose(1, 0, 2).reshape(k, N_DEV * n)
    return t.reshape(N_DEV * k, n)


def _scatter_layout(name, full):
    kk, nn = full.shape
    if name in _COLUMN_SHARDED:
        t = full.reshape(kk, N_DEV, nn // N_DEV).transpose(1, 0, 2)
    else:
        t = full.reshape(N_DEV, kk // N_DEV, nn)
    return t.astype(_WIRE_DTYPE)


def _model_w_in(wi):
    z = lambda *s: jnp.zeros(s, wi.dtype)
    kpe = wi[:, 4608:4672]
    c_blk = jnp.concatenate([wi[:, 4096:4608], kpe, z(1024, 64), _swap_halves(kpe), z(1024, 64), z(1024, 256)], axis=1)
    return {"w_in": jnp.concatenate([wi[:, :4096], c_blk, wi[:, 4672:]], axis=1).astype(_MXU_DTYPE)}


def _model_weights(full):
    return {**_model_w_in(full["w_in"]), **_model_rest(full)}


def _model_rest(full):
    wq3 = full["w_q_b"].reshape(Q_LORA, HEADS, HEAD_DIM + ROPE_DIM)
    pe = wq3[:, :, HEAD_DIM:]
    w_q = jnp.concatenate([wq3[:, :, :HEAD_DIM].reshape(Q_LORA, -1),
                           _pad_last(pe, HEAD_DIM).reshape(Q_LORA, -1),
                           _pad_last(_swap_halves(pe), HEAD_DIM).reshape(Q_LORA, -1)], axis=1)
    wkv3 = full["w_kv_b"].reshape(KV_LORA, HEADS, 2 * HEAD_DIM)
    w_k = wkv3[:, :, :HEAD_DIM].reshape(KV_LORA, -1)
    w_v = wkv3[:, :, HEAD_DIM:].reshape(KV_LORA, -1)
    w = {"w_q": w_q, "w_kv": jnp.concatenate([w_k, w_v], axis=1),
         "w_hg_o": full["w_hg_o"], "w_mla_o": full["w_mla_o"], "w_out": full["w_out"],
         "w_ffn_in": full["w_ffn_in"], "w_ffn_out": full["w_ffn_out"]}
    return {k: v.astype(_MXU_DTYPE) for k, v in w.items()}


def _reference_layout_grad(name, g):
    if name == "w_in":
        g_h, g_c, g_g = g["w_in"]
        d_kpe = g_c[:, 512:576] + _swap_halves(g_c[:, 640:704])
        return jnp.concatenate([g_h, g_c[:, :512], d_kpe, g_g], axis=1)
    if name == "w_q_b":
        gq = g["w_q"]
        d_pe = (gq[:, 1024:2048].reshape(Q_LORA, HEADS, HEAD_DIM)[:, :, :ROPE_DIM]
                + _swap_halves(gq[:, 2048:].reshape(Q_LORA, HEADS, HEAD_DIM)[:, :, :ROPE_DIM]))
        return jnp.concatenate([gq[:, :1024].reshape(Q_LORA, HEADS, HEAD_DIM), d_pe], axis=2).reshape(Q_LORA, -1)
    if name == "w_kv_b":
        return jnp.concatenate([g["w_k"].reshape(KV_LORA, HEADS, HEAD_DIM),
                                g["w_v"].reshape(KV_LORA, HEADS, HEAD_DIM)], axis=2).reshape(KV_LORA, -1)
    return g[name]


def _reference_layout_grads(g):
    return {n: _reference_layout_grad(n, g) for n in _BIG}


class _Comm:
    def __init__(self, shard):
        self.rest_payloads = [shard[n].astype(_WIRE_DTYPE) for n in _BIG[1:]]

    def rest_weights(self, gathered):
        return _model_rest({n: _gathered_matrix(n, t) for n, t in zip(_BIG[1:], gathered, strict=True)})

    def grad_parts(self, names, g):
        return [_scatter_layout(n, _reference_layout_grad(n, g)) for n in names]


def kernel(x, meta_tokens, w_in, b_gate, lb_logits, hg_norm_g, w_hg_o, q_a_norm_g, w_q_b, kv_a_norm_g, w_kv_b, w_mla_o, w_out, mix_pre_g, mix_post_g, ffn_pre_g, ffn_post_g, w_ffn_in, w_ffn_out, loss_target, m_meta_tokens, m_w_in, m_b_gate, m_lb_logits, m_hg_norm_g, m_w_hg_o, m_q_a_norm_g, m_w_q_b, m_kv_a_norm_g, m_w_kv_b, m_w_mla_o, m_w_out, m_mix_pre_g, m_mix_post_g, m_ffn_pre_g, m_ffn_post_g, m_w_ffn_in, m_w_ffn_out, v_meta_tokens, v_w_in, v_b_gate, v_lb_logits, v_hg_norm_g, v_w_hg_o, v_q_a_norm_g, v_w_q_b, v_kv_a_norm_g, v_w_kv_b, v_w_mla_o, v_w_out, v_mix_pre_g, v_mix_post_g, v_ffn_pre_g, v_ffn_post_g, v_w_ffn_in, v_w_ffn_out):
    args = dict(locals())
    batch, seq, d = x.shape
    lp = -(-(N_META + seq) // _LANES) * _LANES
    weight_names = ["meta_tokens", "w_in", "b_gate", "lb_logits", "hg_norm_g", "w_hg_o", "q_a_norm_g", "w_q_b",
                    "kv_a_norm_g", "w_kv_b", "w_mla_o", "w_out", "mix_pre_g", "mix_post_g", "ffn_pre_g",
                    "ffn_post_g", "w_ffn_in", "w_ffn_out"]
    shard = {n: args[n].reshape(args[n].shape[-2:]) for n in _BIG}
    comm = _Comm(shard)

    w_in_all, meta_all = _all_gather("gather_first", [shard["w_in"].astype(_WIRE_DTYPE), meta_tokens])
    w_first = _model_w_in(_gathered_matrix("w_in", w_in_all))
    meta_full = meta_all.transpose(1, 0, 2).reshape(N_META, d)
    small = {n: args[n] for n in _SMALL}

    _, grad_x, _, partial, lb, received = _forward_backward(x, loss_target, meta_full, w_first, small, lp=lp, comm=comm)
    out = {}
    for names, bufs in zip(_GRAD_GROUPS, received, strict=True):
        for n, buf in zip(names, bufs, strict=True):
            two_d = lambda t: t.reshape(t.shape[-2:])
            res = _sum_adamw("adamw_" + n, buf, shard[n], two_d(args["m_" + n]), two_d(args["v_" + n]))
            out[n] = [t.reshape(args[n].shape) for t in res]

    vec_names = _SMALL + ["meta_tokens"]
    *gathered, loss_parts = _all_gather("gather_vectors", [partial[n] for n in vec_names + ["loss"]])
    finished, loss = _finish_vectors(dict(zip(vec_names, gathered, strict=True)), lb,
                                     {n: (args[n], args["m_" + n], args["v_" + n]) for n in vec_names}, loss_parts)
    out.update(finished)
    return (loss, grad_x, *[out[n][i] for i in range(4) for n in weight_names])
```

```python
import functools

import jax
import jax.numpy as jnp
from jax import lax
from jax.experimental import pallas as pl
from jax.experimental.pallas import tpu as pltpu

F32 = jnp.float32
_MXU_DTYPE = jnp.bfloat16
_WIRE_DTYPE = jnp.bfloat16
_VMEM_LIMIT_BYTES = 56 * 1024 * 1024
_LANES = 128
_SUBLANES = 8

N_DEV = 8
N_META = 16
NORM_EPS = 1e-6
HEADS = 8
HEAD_DIM = 128
ROPE_DIM = 64
HG_CHUNK = 16
HG_BLOCK = 128
ROPE_THETA = 10000.0
D_MODEL = 1024
Q_LORA = 256
KV_LORA = 256
FFN_HIDDEN = 2816
ATTN_SCALE = (HEAD_DIM + ROPE_DIM) ** -0.5
NEG_BIG = -1e30

ADAM_LR = 0.001
ADAM_B1 = 0.9
ADAM_B2 = 0.999
ADAM_EPS = 1e-08
ADAM_WD = 0.01
ADAM_STEP = 10

CB_HQ, CB_HF, CB_HI, CB_HG, CB_C, CB_GA, CB_GB = range(7)
IN_COLS_PADDED = 7 * 1024


def _params(**kw):
    return pltpu.CompilerParams(vmem_limit_bytes=_VMEM_LIMIT_BYTES, **kw)


def _dot(a, b):
    return lax.dot_general(a, b, (((1,), (0,)), ((), ())), preferred_element_type=F32)


def _dot_nt(a, b):
    return lax.dot_general(a, b, (((1,), (1,)), ((), ())), preferred_element_type=F32)


def _dot_tn(a, b):
    return lax.dot_general(a, b, (((0,), (0,)), ((), ())), preferred_element_type=F32)


def _mx(x):
    return x.astype(_MXU_DTYPE)


def _exact_dot(m01, x, dot=_dot):
    if _MXU_DTYPE == jnp.float32:
        return dot(m01.astype(F32), x)
    m = m01.astype(jnp.bfloat16)
    x1 = x.astype(jnp.bfloat16)
    x2 = (x - x1.astype(F32)).astype(jnp.bfloat16)
    return dot(m, x1) + dot(m, x2)


def _exact_dot_nt(m01, x):
    return _exact_dot(m01, x, dot=_dot_nt)


def _sigmoid(x):
    return jax.nn.sigmoid(x)


def _silu_grad(x, s):
    return s * (1.0 + x * (1.0 - s))


def _rms_scale(x):
    return lax.rsqrt(jnp.mean(x * x, axis=-1, keepdims=True) + NORM_EPS)


def _rms_bwd(x, g, dy):
    r = _rms_scale(x)
    xh = x * r
    w = dy * g
    dx = r * (w - xh * jnp.mean(xh * w, axis=-1, keepdims=True))
    return dx, dy * xh


def _heads(fn, *arrays):
    outs = [fn(*[a[:, h * HEAD_DIM:(h + 1) * HEAD_DIM] for a in arrays]) for h in range(HEADS)]
    if isinstance(outs[0], tuple):
        return tuple(jnp.concatenate([o[i] for o in outs], axis=1) for i in range(len(outs[0])))
    return jnp.concatenate(outs, axis=1)


class _Ride:
    def __init__(self, payloads, gather):
        self.gather, self.args, self.n = gather, list(payloads), len(payloads)
        self.in_specs = [pl.BlockSpec(memory_space=pl.ANY)] * self.n
        self.out_shape = [jax.ShapeDtypeStruct((N_DEV, *p.shape[-2:]), p.dtype) for p in payloads]
        self.out_specs = [pl.BlockSpec(memory_space=pl.ANY)] * self.n
        self.scratch = [pltpu.SemaphoreType.DMA((self.n, N_DEV - 1)), pltpu.SemaphoreType.DMA((self.n, N_DEV - 1)),
                        pltpu.SemaphoreType.DMA((self.n,))]

    def split(self, rest, n_outs):
        n = self.n
        mine = (rest[:n], rest[n + n_outs:2 * n + n_outs], rest[-3:])
        return rest[n:n + n_outs] + rest[2 * n + n_outs:-3], mine

    def _copies(self, p_refs, out_refs, sems):
        send_sems, recv_sems, local_sems = sems
        x, y, c = lax.axis_index("x"), lax.axis_index("y"), lax.axis_index("c")
        me = 4 * x + 2 * y + c
        copies = []
        for i, (p_ref, out_ref) in enumerate(zip(p_refs, out_refs, strict=True)):
            part = (lambda j, p_ref=p_ref: p_ref) if self.gather else (lambda j, p_ref=p_ref: p_ref.at[j])
            copies.append(pltpu.make_async_copy(part(me), out_ref.at[me], local_sems.at[i]))
            for k in range(1, N_DEV):
                px, py, pc = x ^ (k >> 2), y ^ ((k >> 1) & 1), c ^ (k & 1)
                copies.append(pltpu.make_async_remote_copy(
                    src_ref=part(4 * px + 2 * py + pc), dst_ref=out_ref.at[me],
                    send_sem=send_sems.at[i, k - 1], recv_sem=recv_sems.at[i, k - 1],
                    device_id=(px, py, pc), device_id_type=pl.DeviceIdType.MESH))
        return copies

    def run(self, grid, refs):
        ids = [pl.program_id(i) for i in range(len(grid))]
        first = functools.reduce(jnp.logical_and, [i == 0 for i in ids])
        last = functools.reduce(jnp.logical_and, [i == g - 1 for i, g in zip(ids, grid)])

        @pl.when(first)
        def _():
            for cp in self._copies(*refs):
                cp.start()

        @pl.when(last)
        def _():
            for cp in self._copies(*refs):
                cp.wait()


class _NoRide:
    in_specs, out_shape, out_specs, scratch, args = [], [], [], [], []


def _matmul(name, a, b, *, out_dtype, tm, tn, tk, c_in=None, ride=None, b_transposed=False):
    m, k = a.shape
    n = b.shape[0] if b_transposed else b.shape[1]
    assert m % tm == 0 and n % tn == 0 and k % tk == 0, (name, a.shape, b.shape, tm, tn, tk)
    nk = k // tk
    has_c = c_in is not None
    dot = _dot_nt if b_transposed else _dot
    grid = (n // tn, m // tm, nk)
    n_in = 2 + has_c

    def body(*refs):
        a_ref, b_ref = refs[0], refs[1]
        c_ref = refs[2] if has_c else None
        rest = refs[n_in:]
        if ride is not None:
            rest, exchange = ride.split(rest, 1)
            ride.run(grid, exchange)
        o_ref = rest[0]
        acc_ref = rest[1] if nk > 1 else None

        def finish(r):
            if has_c:
                r = r + c_ref[...]
            o_ref[...] = r.astype(o_ref.dtype)

        if nk == 1:
            finish(dot(a_ref[...], b_ref[...]))
        else:
            kk = pl.program_id(2)

            @pl.when(kk == 0)
            def _():
                acc_ref[...] = jnp.zeros_like(acc_ref)

            acc_ref[...] += dot(a_ref[...], b_ref[...])

            @pl.when(kk == nk - 1)
            def _():
                finish(acc_ref[...])

    in_specs = [pl.BlockSpec((tm, tk), lambda j, i, kk: (i, kk)),
                pl.BlockSpec((tn, tk), lambda j, i, kk: (j, kk)) if b_transposed
                else pl.BlockSpec((tk, tn), lambda j, i, kk: (kk, j))]
    args = [a, b]
    aliases = {}
    if has_c:
        in_specs.append(pl.BlockSpec((tm, tn), lambda j, i, kk: (i, j)))
        args.append(c_in)
        aliases = {2: 0}
    out_shape = [jax.ShapeDtypeStruct((m, n), out_dtype)]
    out_specs = [pl.BlockSpec((tm, tn), lambda j, i, kk: (i, j))]
    scratch = [pltpu.VMEM((tm, tn), F32)] if nk > 1 else []
    if ride is not None:
        in_specs, args = in_specs + ride.in_specs, args + ride.args
        out_shape, out_specs, scratch = out_shape + ride.out_shape, out_specs + ride.out_specs, scratch + ride.scratch
    res = pl.pallas_call(
        body, name=name, out_shape=out_shape, grid=grid, in_specs=in_specs, out_specs=out_specs,
        scratch_shapes=scratch, input_output_aliases=aliases, compiler_params=_params(),
    )(*args)
    return res[0] if ride is None else res


EPILOGUE_ROWS = 272


class _Epilogue:
    def __init__(self, fn, *, rows=(), consts=(), pos=(), outs=(), accs=(), lp=None):
        self.fn, self.rows, self.consts, self.pos = fn, list(rows), list(consts), list(pos)
        self.outs, self.accs, self.lp = list(outs), list(accs), lp


def _matmul_segments(name, a_list, b, *, out_dtype=F32, tm, tn, tk, ride=None, b_transposed=False, epilogue=None):
    m = a_list[0].shape[0]
    n, k = b.shape if b_transposed else b.shape[::-1]
    steps = [a.shape[1] // tk for a in a_list]
    offs = [sum(steps[:s]) for s in range(len(steps))]
    nk = sum(steps)
    assert nk * tk == k and m % tm == 0 and n % tn == 0 and all(a.shape[1] % tk == 0 for a in a_list), name
    grid = (n // tn, m // tm, nk)
    n_seg = len(a_list)
    dot = _dot_nt if b_transposed else _dot
    ep = epilogue
    assert ep is None or tn == n, name
    n_extra = 0 if ep is None else len(ep.rows) + len(ep.consts) + len(ep.pos)
    n_outs = 1 if ep is None else len(ep.outs) + len(ep.accs)

    def body(*refs):
        a_refs, b_ref = refs[:n_seg], refs[n_seg]
        extra_refs, rest = refs[n_seg + 1:n_seg + 1 + n_extra], refs[n_seg + 1 + n_extra:]
        if ride is not None:
            rest, exchange = ride.split(rest, n_outs)
            ride.run(grid, exchange)
        out_refs, acc_ref = rest[:n_outs], rest[n_outs]
        i, kk = pl.program_id(1), pl.program_id(2)

        @pl.when(kk == 0)
        def _():
            acc_ref[...] = jnp.zeros_like(acc_ref)

        for s in range(n_seg):
            @pl.when((kk >= offs[s]) & (kk < offs[s] + steps[s]))
            def _(s=s):
                acc_ref[...] += dot(a_refs[s][...], b_ref[...])

        if ep is None:
            @pl.when(kk == nk - 1)
            def _():
                out_refs[0][...] = acc_ref[...].astype(out_refs[0].dtype)
        else:
            sum_refs = out_refs[len(ep.outs):]

            @pl.when((kk == 0) & (i == 0))
            def _():
                for ref in sum_refs:
                    ref[...] = jnp.zeros_like(ref)

            @pl.when(kk == nk - 1)
            def _():
                rs = EPILOGUE_ROWS if tm % EPILOGUE_ROWS == 0 else tm
                n_r, n_c = len(ep.rows), len(ep.consts)
                for r0 in range(0, tm, rs):
                    sl = slice(r0, r0 + rs)
                    tiles = ([r[sl, :] for r in extra_refs[:n_r]] + [c[...] for c in extra_refs[n_r:n_r + n_c]]
                             + [t[sl, :] for t in extra_refs[n_r + n_c:]])
                    res_outs, res_sums = ep.fn(acc_ref[sl, :], *tiles)
                    for ref, val in zip(out_refs[:len(ep.outs)], res_outs, strict=True):
                        ref[sl, :] = val.astype(ref.dtype)
                    for ref, val in zip(sum_refs, res_sums, strict=True):
                        ref[...] += val.reshape(rs // _SUBLANES, _SUBLANES, val.shape[-1]).sum(axis=0)

    seg_spec = lambda s: pl.BlockSpec(
        (tm, tk), functools.partial(lambda j, i, kk, off, ns: (i, jnp.clip(kk - off, 0, ns - 1)), off=offs[s], ns=steps[s]))
    b_spec = (pl.BlockSpec((tn, tk), lambda j, i, kk: (j, kk)) if b_transposed
              else pl.BlockSpec((tk, tn), lambda j, i, kk: (kk, j)))
    in_specs = [seg_spec(s) for s in range(n_seg)] + [b_spec]
    args = list(a_list) + [b]
    row_spec = lambda w: pl.BlockSpec((tm, w), lambda j, i, kk: (i, 0))
    if ep is None:
        out_shape = [jax.ShapeDtypeStruct((m, n), out_dtype)]
        out_specs = [pl.BlockSpec((tm, tn), lambda j, i, kk: (i, j))]
    else:
        tiles_per_example = ep.lp // tm
        row_ins = [r if isinstance(r, tuple) else (r, r.shape[1], 0) for r in ep.rows]
        in_specs += ([pl.BlockSpec((tm, wd), functools.partial(lambda j, i, kk, cb: (i, cb), cb=cb)) for _, wd, cb in row_ins]
                     + [pl.BlockSpec(c.shape, lambda j, i, kk: (0, 0)) for c in ep.consts]
                     + [pl.BlockSpec((tm, p.shape[1]), lambda j, i, kk: (i % tiles_per_example, 0)) for p in ep.pos])
        args += [arr for arr, _, _ in row_ins] + ep.consts + ep.pos
        out_shape = ([jax.ShapeDtypeStruct((m, w), dt) for w, dt in ep.outs]
                     + [jax.ShapeDtypeStruct((_SUBLANES, w), F32) for w in ep.accs])
        out_specs = ([row_spec(w) for w, _ in ep.outs]
                     + [pl.BlockSpec((_SUBLANES, w), lambda j, i, kk: (0, 0)) for w in ep.accs])
    scratch = [pltpu.VMEM((tm, tn), F32)]
    if ride is not None:
        in_specs, args = in_specs + ride.in_specs, args + ride.args
        out_shape, out_specs, scratch = out_shape + ride.out_shape, out_specs + ride.out_specs, scratch + ride.scratch
    res = pl.pallas_call(
        body, name=name, out_shape=out_shape, grid=grid, in_specs=in_specs, out_specs=out_specs,
        scratch_shapes=scratch, compiler_params=_params(),
    )(*args)
    if ep is None:
        return res[0] if ride is None else res
    n_o = len(ep.outs)
    return (res[:n_o], res[n_o:n_outs], *res[n_outs:])


def _matmul_tn(name, x, dy, *, tk, tn, tr, out_dtype=F32):
    r, k = x.shape
    _, n = dy.shape
    assert r % tr == 0 and k % tk == 0 and n % tn == 0, (name, x.shape, dy.shape)
    n_r = r // tr
    direct = out_dtype == F32

    def body(x_ref, dy_ref, o_ref, *scratch):
        acc_ref = o_ref if direct else scratch[0]

        @pl.when(pl.program_id(2) == 0)
        def _():
            acc_ref[...] = jnp.zeros_like(acc_ref)

        acc_ref[...] += _dot_tn(x_ref[...], dy_ref[...])
        if not direct:
            @pl.when(pl.program_id(2) == n_r - 1)
            def _():
                o_ref[...] = acc_ref[...].astype(o_ref.dtype)

    return pl.pallas_call(
        body, name=name,
        out_shape=jax.ShapeDtypeStruct((k, n), out_dtype),
        grid=(k // tk, n // tn, n_r),
        in_specs=[pl.BlockSpec((tr, tk), lambda kb, nb, rr: (rr, kb)),
                  pl.BlockSpec((tr, tn), lambda kb, nb, rr: (rr, nb))],
        out_specs=pl.BlockSpec((tk, tn), lambda kb, nb, rr: (kb, nb)),
        scratch_shapes=[] if direct else [pltpu.VMEM((tk, tn), F32)],
        compiler_params=_params(),
    )(x, dy)


def _ffn_in_swiglu(u, w, *, tm, tn):
    r, k = u.shape
    h = w.shape[1] // 2
    assert r % tm == 0 and h % tn == 0
    nj = h // tn

    def body(u_ref, wg_ref, wu_ref, act_ref, dgate_ref, dup_ref):
        uu = u_ref[...]
        gt, up = _dot(uu, wg_ref[...]), _dot(uu, wu_ref[...])
        s = _sigmoid(gt)
        silu = gt * s
        act_ref[...] = (silu * up).astype(act_ref.dtype)
        dgate_ref[...] = (up * _silu_grad(gt, s)).astype(dgate_ref.dtype)
        dup_ref[...] = silu.astype(dup_ref.dtype)

    tile = pl.BlockSpec((tm, tn), lambda j, i: (i, j))
    return pl.pallas_call(
        body, name="ffn_in_swiglu",
        out_shape=[jax.ShapeDtypeStruct((r, h), _MXU_DTYPE)] * 3,
        grid=(nj, r // tm),
        in_specs=[pl.BlockSpec((tm, k), lambda j, i: (i, 0)),
                  pl.BlockSpec((k, tn), lambda j, i: (0, j)),
                  pl.BlockSpec((k, tn), lambda j, i: (0, nj + j))],
        out_specs=[tile] * 3,
        compiler_params=_params(),
    )(u, w, w)


def _d_ffn_out_swiglu(dy, w, act_dgate, act_dup, *, tm, tn):
    r, k = dy.shape
    h = w.shape[0]
    assert r % tm == 0 and h % tn == 0

    def body(dy_ref, w_ref, pg_ref, pu_ref, dgt_ref, dup_ref):
        da = _dot_nt(dy_ref[...], w_ref[...])
        dgt_ref[...] = (da * pg_ref[...].astype(F32)).astype(dgt_ref.dtype)
        dup_ref[...] = (da * pu_ref[...].astype(F32)).astype(dup_ref.dtype)

    tile = pl.BlockSpec((tm, tn), lambda j, i: (i, j))
    return pl.pallas_call(
        body, name="d_ffn_out_swiglu",
        out_shape=[jax.ShapeDtypeStruct((r, h), _MXU_DTYPE)] * 2,
        grid=(h // tn, r // tm),
        in_specs=[pl.BlockSpec((tm, k), lambda j, i: (i, 0)), pl.BlockSpec((tn, k), lambda j, i: (j, 0)), tile, tile],
        out_specs=[tile] * 2,
        compiler_params=_params(),
    )(dy, w, act_dgate, act_dup)


def _proj_q_rope(cqn, w_q, c_tab, s_tab, *, tm, lp):
    r, k = cqn.shape
    tiles_per_example = lp // tm
    pair = 2 * HEAD_DIM

    def body(x_ref, wn_ref, wp_ref, ws_ref, c_ref, s_ref, o_ref):
        x = x_ref[...]
        c2, s2 = jnp.tile(c_ref[...], (1, 2)), jnp.tile(s_ref[...], (1, 2))
        nope = _dot(x, wn_ref[...])
        roped = _dot(x, wp_ref[...]) * c2 + _dot(x, ws_ref[...]) * s2
        hs = lambda t, h: t[:, h * HEAD_DIM:(h + 1) * HEAD_DIM]
        o_ref[...] = jnp.concatenate([hs(nope, 0), hs(roped, 0), hs(nope, 1), hs(roped, 1)], axis=1).astype(o_ref.dtype)

    w_blk = lambda part: pl.BlockSpec((k, pair), functools.partial(lambda h, i, part: (0, part * (HEADS // 2) + h), part=part))
    tab = pl.BlockSpec((tm, HEAD_DIM), lambda h, i: (i % tiles_per_example, 0))
    return pl.pallas_call(
        body, name="proj_q_rope",
        out_shape=jax.ShapeDtypeStruct((r, HEADS * QK_DIM), _MXU_DTYPE),
        grid=(HEADS // 2, r // tm),
        in_specs=[pl.BlockSpec((tm, k), lambda h, i: (i, 0)), w_blk(0), w_blk(1), w_blk(2), tab, tab],
        out_specs=pl.BlockSpec((tm, 2 * QK_DIM), lambda h, i: (i, h)),
        compiler_params=_params(),
    )(cqn, w_q, w_q, w_q, c_tab, s_tab)


def _rowwise(name, body, *, rows, tr, lp, ins, outs, accs=()):
    assert rows % tr == 0 and lp % tr == 0 and tr % 16 == 0
    tiles_per_example = lp // tr
    in_specs, arrays = [], []
    for spec in ins:
        if spec[0] == "row":
            _, arr, width, cb = spec
            in_specs.append(pl.BlockSpec((tr, width), functools.partial(lambda i, cb: (i, cb), cb=cb)))
        elif spec[0] == "const":
            arr = spec[1]
            in_specs.append(pl.BlockSpec(arr.shape, lambda i: (0, 0)))
        else:
            arr = spec[1]
            in_specs.append(pl.BlockSpec((tr, arr.shape[1]), lambda i: (i % tiles_per_example, 0)))
        arrays.append(arr)
    n_in, n_out = len(ins), len(outs)

    def kern(*refs):
        res_outs, res_accs = body(*[r[...] for r in refs[:n_in]])
        for ref, val in zip(refs[n_in:n_in + n_out], res_outs, strict=True):
            ref[...] = val.astype(ref.dtype)
        acc_refs = refs[n_in + n_out:]
        if acc_refs:
            @pl.when(pl.program_id(0) == 0)
            def _():
                for ref in acc_refs:
                    ref[...] = jnp.zeros_like(ref)

            for ref, val in zip(acc_refs, res_accs, strict=True):
                ref[...] += val.reshape(tr // _SUBLANES, _SUBLANES, val.shape[-1]).sum(axis=0)

    out_shape = ([jax.ShapeDtypeStruct((rows, w), dt) for w, dt in outs]
                 + [jax.ShapeDtypeStruct((_SUBLANES, w), F32) for w in accs])
    out_specs = ([pl.BlockSpec((tr, w), lambda i: (i, 0)) for w, _ in outs]
                 + [pl.BlockSpec((_SUBLANES, w), lambda i: (0, 0)) for w in accs])
    res = pl.pallas_call(
        kern, name=name, out_shape=out_shape, grid=(rows // tr,),
        in_specs=in_specs, out_specs=out_specs, compiler_params=_params(),
    )(*arrays)
    return res[:n_out], list(res[n_out:])


def _assemble(name, x, head_rows, lp):
    batch, seq, d = x.shape
    tc = 256

    def body(x_ref, m_ref, o_ref):
        o_ref[0:N_META, :] = m_ref[...]
        o_ref[N_META:N_META + seq, :] = x_ref[0]
        if lp > N_META + seq:
            o_ref[N_META + seq:, :] = jnp.zeros((lp - N_META - seq, tc), F32)

    return pl.pallas_call(
        body, name=name,
        out_shape=jax.ShapeDtypeStruct((batch * lp, d), F32),
        grid=(batch, d // tc),
        in_specs=[pl.BlockSpec((1, seq, tc), lambda b, j: (b, 0, j)),
                  pl.BlockSpec((N_META, tc), lambda b, j: (0, j))],
        out_specs=pl.BlockSpec((lp, tc), lambda b, j: (b, j)),
        compiler_params=_params(),
    )(x, head_rows)


def _meta_grad(dh0, batch, lp):
    d = dh0.shape[1]

    def body(g_ref, o_ref):
        @pl.when(pl.program_id(0) == 0)
        def _():
            o_ref[...] = jnp.zeros_like(o_ref)

        o_ref[...] += g_ref[...]

    return pl.pallas_call(
        body, name="meta_grad",
        out_shape=jax.ShapeDtypeStruct((N_META, d), F32),
        grid=(batch,),
        in_specs=[pl.BlockSpec((N_META, d), lambda b: (b * (lp // N_META), 0))],
        out_specs=pl.BlockSpec((N_META, d), lambda b: (0, 0)),
        compiler_params=_params(),
    )(dh0)


def _segment_masks():
    t = lax.broadcasted_iota(jnp.int32, (HG_BLOCK, HG_BLOCK), 0)
    s = lax.broadcasted_iota(jnp.int32, (HG_BLOCK, HG_BLOCK), 1)
    same = lax.shift_right_logical(t, 4) == lax.shift_right_logical(s, 4)
    lower = same & (s <= t)
    upper = same & (s >= t)
    first_half = same & ((s & 15) <= 7)
    return same, lower, upper, first_half


def _hgrn_gates(hq, hf, lb):
    sq = _sigmoid(hq)
    q = hq * sq
    sf = _sigmoid(hf)
    f = lb + (1.0 - lb) * sf
    return q, sq, sf, f


def _hgrn_decays(g, same, lower, first_half):
    b = _exact_dot(lower, g)
    b_last = _exact_dot(same, g)
    b_ref = _exact_dot(first_half, g)
    return b, b_last, b_ref


def _hgrn_fwd(p, lb, gh, *, batch, lp, ride=None):
    rows = batch * lp
    nb = lp // HG_BLOCK
    n_chunks = HG_BLOCK // HG_CHUNK

    def body(hq_ref, hf_ref, hi_ref, hg_ref, lb_ref, gh_ref, *rest):
        if ride is not None:
            rest, exchange = ride.split(rest, 3)
            ride.run((batch, nb), exchange)
        o_ref, z_ref, st_ref, s_scr = rest

        @pl.when(pl.program_id(1) == 0)
        def _():
            s_scr[...] = jnp.zeros_like(s_scr)

        same, lower, _, first_half = _segment_masks()
        v = hi_ref[...]
        q, _, _, f = _hgrn_gates(hq_ref[...], hf_ref[...], lb_ref[...])
        k = 1.0 - f
        b, b_last, b_ref = _hgrn_decays(jnp.log(f), same, lower, first_half)
        qt = _mx(q * jnp.exp(b))
        kh = _mx(k * jnp.exp(b_last - b))
        vm = _mx(v)
        el = jnp.exp(b_last)
        qc = _mx(q * jnp.exp(b - b_ref))
        kc = _mx(k * jnp.exp(b_ref - b))

        def intra(qc_h, kc_h, v_h):
            a = jnp.where(lower, _dot_nt(qc_h, kc_h), 0.0)
            return _dot(_mx(a), v_h)

        o_intra = _heads(intra, qc, kc, vm)

        states = [s_scr[h] for h in range(HEADS)]
        o_inter = [[None] * HEADS for _ in range(n_chunks)]
        for c in range(n_chunks):
            rs = slice(c * HG_CHUNK, (c + 1) * HG_CHUNK)
            for h in range(HEADS):
                cs = slice(h * HEAD_DIM, (h + 1) * HEAD_DIM)
                st_m = _mx(states[h])
                st_ref[c, h] = st_m
                o_inter[c][h] = _dot_nt(qt[rs, cs], st_m)
                states[h] = states[h] * el[c * HG_CHUNK:c * HG_CHUNK + 1, cs] + _dot_tn(vm[rs, cs], kh[rs, cs])
        for h in range(HEADS):
            s_scr[h] = states[h]

        o = o_intra + jnp.concatenate([jnp.concatenate(row, axis=1) for row in o_inter], axis=0)
        o_ref[...] = o
        hg = hg_ref[...]
        n = _heads(lambda o_h: o_h * _rms_scale(o_h), o) * gh_ref[...]
        z_ref[...] = (n * hg * _sigmoid(hg)).astype(z_ref.dtype)

    blk = lambda cb: pl.BlockSpec((HG_BLOCK, D_MODEL), functools.partial(lambda b, j, cb: (b * nb + j, cb), cb=cb))
    row_out = pl.BlockSpec((HG_BLOCK, D_MODEL), lambda b, j: (b * nb + j, 0))
    const = pl.BlockSpec((1, D_MODEL), lambda b, j: (0, 0))
    extra = ride if ride is not None else _NoRide
    return pl.pallas_call(
        body, name="hgrn_fwd",
        out_shape=[jax.ShapeDtypeStruct((rows, D_MODEL), F32),
                   jax.ShapeDtypeStruct((rows, D_MODEL), _MXU_DTYPE),
                   jax.ShapeDtypeStruct((rows // HG_CHUNK, HEADS, HEAD_DIM, HEAD_DIM), _MXU_DTYPE)] + extra.out_shape,
        grid=(batch, nb),
        in_specs=[blk(CB_HQ), blk(CB_HF), blk(CB_HI), blk(CB_HG), const, const] + extra.in_specs,
        out_specs=[row_out, row_out,
                   pl.BlockSpec((n_chunks, HEADS, HEAD_DIM, HEAD_DIM), lambda b, j: (b * nb + j, 0, 0, 0))]
        + extra.out_specs,
        scratch_shapes=[pltpu.VMEM((HEADS, HEAD_DIM, HEAD_DIM), F32)] + extra.scratch,
        compiler_params=_params(),
    )(p, p, p, p, lb, gh, *extra.args)


def _hgrn_bwd(p, o, dz, states, lb, gh, *, batch, lp, ride=None):
    rows = batch * lp
    nb = lp // HG_BLOCK
    n_chunks = HG_BLOCK // HG_CHUNK

    def body(hq_ref, hf_ref, hi_ref, hg_ref, o_ref, dz_ref, st_ref, lb_ref, gh_ref, *rest):
        if ride is not None:
            rest, exchange = ride.split(rest, 3)
            ride.run((batch, nb), exchange)
        dp_ref, dlb_ref, dgh_ref, ds_scr = rest
        first = (pl.program_id(0) == 0) & (pl.program_id(1) == 0)

        @pl.when(first)
        def _():
            dlb_ref[...] = jnp.zeros_like(dlb_ref)
            dgh_ref[...] = jnp.zeros_like(dgh_ref)

        @pl.when(pl.program_id(1) == 0)
        def _():
            ds_scr[...] = jnp.zeros_like(ds_scr)

        same, lower, upper, first_half = _segment_masks()
        lbv = lb_ref[...]
        hq, hf, v, hg = hq_ref[...], hf_ref[...], hi_ref[...], hg_ref[...]
        q, sq, sf, f = _hgrn_gates(hq, hf, lbv)
        k = 1.0 - f
        b, b_last, b_ref = _hgrn_decays(jnp.log(f), same, lower, first_half)
        e_b = jnp.exp(b)
        e_kh = jnp.exp(b_last - b)
        e_qc = jnp.exp(b - b_ref)
        e_kc = jnp.exp(b_ref - b)
        qt, kh, qc, kc = q * e_b, k * e_kh, q * e_qc, k * e_kc

        o = o_ref[...]
        dz = dz_ref[...].astype(F32)
        ghv = gh_ref[...]
        sg = _sigmoid(hg)
        r = _heads(lambda o_h: jnp.broadcast_to(_rms_scale(o_h), o_h.shape), o)
        oh = o * r
        dn = dz * hg * sg
        dhg = dz * oh * ghv * _silu_grad(hg, sg)
        w = dn * ghv
        do = r * (w - oh * _heads(lambda t: jnp.broadcast_to(jnp.mean(t, axis=-1, keepdims=True), t.shape), oh * w))
        dgh_ref[...] += (dn * oh).reshape(HG_BLOCK // _SUBLANES, _SUBLANES, D_MODEL).sum(axis=0)

        qt_m, kh_m, v_m, do_m = _mx(qt), _mx(kh), _mx(v), _mx(do)
        el_all = jnp.exp(b_last)

        def intra(qc_h, kc_h, v_h, do_h):
            a = _mx(jnp.where(lower, _dot_nt(qc_h, kc_h), 0.0))
            da = _mx(jnp.where(lower, _dot_nt(do_h, v_h), 0.0))
            return _dot(da, kc_h), _dot_tn(da, qc_h), _dot_tn(a, do_h)

        dqc, dkc, dv_intra = _heads(intra, _mx(qc), _mx(kc), v_m, do_m)

        d_states = [ds_scr[h] for h in range(HEADS)]
        grid_of = lambda: [[None] * HEADS for _ in range(n_chunks)]
        dkh_p, dv_p, dbl_p, dqt_p = grid_of(), grid_of(), grid_of(), grid_of()
        for c in reversed(range(n_chunks)):
            rs = slice(c * HG_CHUNK, (c + 1) * HG_CHUNK)
            for h in range(HEADS):
                cs = slice(h * HEAD_DIM, (h + 1) * HEAD_DIM)
                st = st_ref[c, h]
                ds_t = d_states[h]
                ds_m = _mx(ds_t)
                el = el_all[c * HG_CHUNK:c * HG_CHUNK + 1, cs]
                dkh_p[c][h] = _dot(v_m[rs, cs], ds_m)
                dv_p[c][h] = _dot_nt(kh_m[rs, cs], ds_m)
                dbl = jnp.sum(ds_t * st.astype(F32), axis=0, keepdims=True) * el
                dbl_p[c][h] = jnp.broadcast_to(dbl, (HG_CHUNK, HEAD_DIM))
                dqt_p[c][h] = _dot(do_m[rs, cs], st)
                d_states[h] = ds_t * el + _dot_tn(do_m[rs, cs], qt_m[rs, cs])
        for h in range(HEADS):
            ds_scr[h] = d_states[h]
        whole = lambda parts: jnp.concatenate([jnp.concatenate(row, axis=1) for row in parts], axis=0)

        dqt, dkh = whole(dqt_p), whole(dkh_p)
        dq = dqt * e_b + dqc * e_qc
        dk = dkh * e_kh + dkc * e_kc
        t_kh = dkh * kh
        db_rows = dqt * qt + dqc * qc - dkc * kc - t_kh
        dg = _exact_dot(upper, db_rows) + _exact_dot(same, t_kh) + whole(dbl_p)
        df = dg / f - dk
        dhf = df * (1.0 - lbv) * sf * (1.0 - sf)
        dlb_ref[...] += (df * (1.0 - sf)).reshape(HG_BLOCK // _SUBLANES, _SUBLANES, D_MODEL).sum(axis=0)
        dhq = dq * _silu_grad(hq, sq)
        dp_ref[...] = jnp.concatenate([dhq, dhf, dv_intra + whole(dv_p), dhg], axis=1).astype(dp_ref.dtype)

    rev = lambda b, j: b * nb + (nb - 1 - j)
    blk = lambda cb: pl.BlockSpec((HG_BLOCK, D_MODEL), functools.partial(lambda b, j, cb: (rev(b, j), cb), cb=cb))
    row = pl.BlockSpec((HG_BLOCK, D_MODEL), lambda b, j: (rev(b, j), 0))
    const = pl.BlockSpec((1, D_MODEL), lambda b, j: (0, 0))
    acc = pl.BlockSpec((_SUBLANES, D_MODEL), lambda b, j: (0, 0))
    extra = ride if ride is not None else _NoRide
    dp, dlb, dgh, *exchanged = pl.pallas_call(
        body, name="hgrn_bwd",
        out_shape=[jax.ShapeDtypeStruct((rows, 4 * D_MODEL), _MXU_DTYPE),
                   jax.ShapeDtypeStruct((_SUBLANES, D_MODEL), F32),
                   jax.ShapeDtypeStruct((_SUBLANES, D_MODEL), F32)] + extra.out_shape,
        grid=(batch, nb),
        in_specs=[blk(CB_HQ), blk(CB_HF), blk(CB_HI), blk(CB_HG), row, row,
                  pl.BlockSpec((n_chunks, HEADS, HEAD_DIM, HEAD_DIM), lambda b, j: (rev(b, j), 0, 0, 0)),
                  const, const] + extra.in_specs,
        out_specs=[pl.BlockSpec((HG_BLOCK, 4 * D_MODEL), lambda b, j: (rev(b, j), 0)), acc, acc] + extra.out_specs,
        scratch_shapes=[pltpu.VMEM((HEADS, HEAD_DIM, HEAD_DIM), F32)] + extra.scratch,
        compiler_params=_params(),
    )(p, p, p, p, o, dz, states, lb, gh, *extra.args)
    return (dp, dlb, dgh, *exchanged)


QK_DIM = 2 * HEAD_DIM
ATTN_TQ = 256
ATTN_KEY_CHUNK = 1024


def _query_tiles(lp):
    return [(r0, min(ATTN_TQ, lp - r0)) for r0 in range(0, lp, ATTN_TQ)]


def _attn_fwd(q_cat, kv, kp, *, batch, lp):
    rows = batch * lp

    def body(q_ref, kn_ref, kp_ref, v_ref, o_ref, lse_ref):
        k_cat = jnp.concatenate([kn_ref[...], kp_ref[...]], axis=1)
        for r0, tq in _query_tiles(lp):
            q_t = q_ref[r0:r0 + tq, :]
            i = lax.broadcasted_iota(jnp.int32, (tq, tq), 0)
            j = lax.broadcasted_iota(jnp.int32, (tq, tq), 1)
            s_diag = jnp.where(j <= i, _dot_nt(q_t, k_cat[r0:r0 + tq]) * ATTN_SCALE, NEG_BIG)
            m = jnp.max(s_diag, axis=1, keepdims=True)
            if r0:
                s_past = _dot_nt(q_t, k_cat[0:r0]) * ATTN_SCALE
                m = jnp.maximum(m, jnp.max(s_past, axis=1, keepdims=True))
            p_diag = jnp.exp(s_diag - m)
            l = jnp.sum(p_diag, axis=1, keepdims=True)
            acc = _dot(_mx(p_diag), v_ref[r0:r0 + tq, :])
            if r0:
                p_past = jnp.exp(s_past - m)
                l = l + jnp.sum(p_past, axis=1, keepdims=True)
                acc = acc + _dot(_mx(p_past), v_ref[0:r0, :])
            o_ref[r0:r0 + tq, :] = (acc / l).astype(o_ref.dtype)
            lse_ref[r0:r0 + tq, :] = jnp.broadcast_to(m + jnp.log(l), (tq, HEAD_DIM))

    head_blk = pl.BlockSpec((lp, HEAD_DIM), lambda b, h: (b, h))
    return pl.pallas_call(
        body, name="attn_fwd",
        out_shape=[jax.ShapeDtypeStruct((rows, D_MODEL), _MXU_DTYPE),
                   jax.ShapeDtypeStruct((rows, D_MODEL), F32)],
        grid=(batch, HEADS),
        in_specs=[pl.BlockSpec((lp, QK_DIM), lambda b, h: (b, h)), head_blk,
                  pl.BlockSpec((lp, HEAD_DIM), lambda b, h: (b, 0)),
                  pl.BlockSpec((lp, HEAD_DIM), lambda b, h: (b, HEADS + h))],
        out_specs=[head_blk, head_blk],
        compiler_params=_params(),
    )(q_cat, kv, kp, kv)


def _attn_bwd(q_cat, kv, kp, do, o, lse, c_tab, s_tab, *, batch, lp, ride=None):
    rows = batch * lp

    def body(q_ref, kn_ref, kp_ref, v_ref, do_ref, o_ref, lse_ref, c_ref, s_ref, *rest):
        if ride is not None:
            rest, exchange = ride.split(rest, 6)
            ride.run((batch, HEADS), exchange)
        dqn_ref, dqc_ref, dqs_ref, dkn_ref, dkp_ref, dv_ref, dk_acc, dv_acc = rest
        dk_acc[...] = jnp.zeros_like(dk_acc)
        dv_acc[...] = jnp.zeros_like(dv_acc)
        k_cat = jnp.concatenate([kn_ref[...], kp_ref[...]], axis=1)
        k_t = k_cat.T
        lane = lax.broadcasted_iota(jnp.int32, (_SUBLANES, HEAD_DIM), 1)
        lse_row = _exact_dot_nt(lane == 0, lse_ref[...])
        delta = _exact_dot_nt(lane >= 0, do_ref[...].astype(F32) * o_ref[...].astype(F32))
        for r0, tq in _query_tiles(lp):
            cols = slice(r0, r0 + tq)
            q_t_, do_t_ = q_ref[cols, :], do_ref[cols, :]
            lse_t, delta_t = lse_row[0:1, cols], delta[0:1, cols]
            chunks = [(c0, min(ATTN_KEY_CHUNK, r0 - c0), False) for c0 in range(0, r0, ATTN_KEY_CHUNK)] + [(r0, tq, True)]
            dq_t = jnp.zeros((QK_DIM, tq), F32)
            for c0, n, diagonal in chunks:
                keys = slice(c0, c0 + n)
                s = _dot_nt(k_cat[keys], q_t_) * ATTN_SCALE
                if diagonal:
                    jk = lax.broadcasted_iota(jnp.int32, (n, tq), 0)
                    iq = lax.broadcasted_iota(jnp.int32, (n, tq), 1)
                    s = jnp.where(jk <= iq, s, NEG_BIG)
                pexp = jnp.exp(s - lse_t)
                dp = _dot_nt(v_ref[keys, :], do_t_)
                ds = _mx(pexp * (dp - delta_t) * ATTN_SCALE)
                dk_acc[keys, :] += _dot(ds, q_t_)
                dv_acc[keys, :] += _dot(_mx(pexp), do_t_)
                dq_t = dq_t + _dot(k_t[:, keys], ds)
            dq = dq_t.T
            d_rope = dq[:, HEAD_DIM:]
            dqn_ref[cols, :] = dq[:, :HEAD_DIM].astype(dqn_ref.dtype)
            dqc_ref[cols, :] = (d_rope * c_ref[cols, :]).astype(dqc_ref.dtype)
            dqs_ref[cols, :] = (d_rope * s_ref[cols, :]).astype(dqs_ref.dtype)

        dkn_ref[...] = dk_acc[:, 0:HEAD_DIM].astype(dkn_ref.dtype)
        dv_ref[...] = dv_acc[...].astype(dv_ref.dtype)

        @pl.when(pl.program_id(1) == 0)
        def _():
            dkp_ref[...] = jnp.zeros_like(dkp_ref)

        dkp_ref[...] += dk_acc[:, HEAD_DIM:]

    head_blk = pl.BlockSpec((lp, HEAD_DIM), lambda b, h: (b, h))
    cat_blk = pl.BlockSpec((lp, QK_DIM), lambda b, h: (b, h))
    shared_blk = pl.BlockSpec((lp, HEAD_DIM), lambda b, h: (b, 0))
    table_blk = pl.BlockSpec((lp, HEAD_DIM), lambda b, h: (0, 0))
    extra = ride if ride is not None else _NoRide
    return pl.pallas_call(
        body, name="attn_bwd",
        out_shape=[jax.ShapeDtypeStruct((rows, D_MODEL), _MXU_DTYPE)] * 3 + [
                   jax.ShapeDtypeStruct((rows, D_MODEL), _MXU_DTYPE),
                   jax.ShapeDtypeStruct((rows, HEAD_DIM), F32),
                   jax.ShapeDtypeStruct((rows, D_MODEL), _MXU_DTYPE)] + extra.out_shape,
        grid=(batch, HEADS),
        in_specs=[cat_blk, head_blk, shared_blk, pl.BlockSpec((lp, HEAD_DIM), lambda b, h: (b, HEADS + h)),
                  head_blk, head_blk, head_blk, table_blk, table_blk] + extra.in_specs,
        out_specs=[head_blk, head_blk, head_blk, head_blk, shared_blk, head_blk] + extra.out_specs,
        scratch_shapes=[pltpu.VMEM((lp, QK_DIM), F32), pltpu.VMEM((lp, HEAD_DIM), F32)] + extra.scratch,
        compiler_params=_params(),
    )(q_cat, kv, kp, kv, do, o, lse, c_tab, s_tab, *extra.args)


def _all_gather(name, blocks):
    n = len(blocks)

    def body(*refs):
        x_refs, out_refs, (send_sems, recv_sems, local_sems) = refs[:n], refs[n:2 * n], refs[2 * n:]
        x, y, c = lax.axis_index("x"), lax.axis_index("y"), lax.axis_index("c")
        me, sibling = (x, y, c), (x, y, 1 - c)
        chips = [(1 - x, y), (x, 1 - y), (1 - x, 1 - y)]

        def slot(i, px, py, pc):
            return out_refs[i].at[4 * px + 2 * py + pc]

        def copy(i, k, blk, to, src=None):
            return pltpu.make_async_remote_copy(
                src_ref=slot(i, *blk) if src is None else src, dst_ref=slot(i, *blk),
                send_sem=send_sems.at[i, k], recv_sem=recv_sems.at[i, k],
                device_id=to, device_id_type=pl.DeviceIdType.MESH)

        mine = [pltpu.make_async_copy(x_refs[i], slot(i, *me), local_sems.at[i]) for i in range(n)]
        first = [copy(i, 0, me, sibling, src=x_refs[i]) for i in range(n)]
        first += [copy(i, 1 + j, me, (*chip, c), src=x_refs[i]) for i in range(n) for j, chip in enumerate(chips)]
        for cp in mine + first:
            cp.start()
        passed = []
        for i in range(n):
            for j, chip in enumerate(chips):
                copy(i, 1 + j, (*chip, c), me).wait_recv()
                passed.append(copy(i, 4 + j, (*chip, c), sibling))
                passed[-1].start()
        for i in range(n):
            copy(i, 0, sibling, me).wait_recv()
            for j, chip in enumerate(chips):
                copy(i, 4 + j, (*chip, 1 - c), me).wait_recv()
        for cp in first + passed:
            cp.wait_send()
        for cp in mine:
            cp.wait()

    return pl.pallas_call(
        body, name=name,
        out_shape=[jax.ShapeDtypeStruct((N_DEV, *b.shape), b.dtype) for b in blocks],
        in_specs=[pl.BlockSpec(memory_space=pl.ANY)] * n,
        out_specs=[pl.BlockSpec(memory_space=pl.ANY)] * n,
        scratch_shapes=[pltpu.SemaphoreType.DMA((n, 7)), pltpu.SemaphoreType.DMA((n, 7)),
                        pltpu.SemaphoreType.DMA((n,))],
    )(*blocks)


def _adamw_math(w, g, m, v):
    nm = ADAM_B1 * m + (1.0 - ADAM_B1) * g
    nv = ADAM_B2 * v + (1.0 - ADAM_B2) * (g * g)
    m_hat = nm / (1.0 - ADAM_B1 ** ADAM_STEP)
    v_hat = nv / (1.0 - ADAM_B2 ** ADAM_STEP)
    return -ADAM_LR * (m_hat / (jnp.sqrt(v_hat) + ADAM_EPS) + ADAM_WD * w), nm, nv


def _sum_adamw(name, parts, w, m, v):
    rows, cols = w.shape
    tr = rows // 4 if rows % 64 == 0 and rows * cols > (1 << 16) else rows

    def body(p_ref, w_ref, m_ref, v_ref, g_ref, d_ref, nm_ref, nv_ref):
        g = p_ref[0].astype(F32)
        for dev in range(1, N_DEV):
            g = g + p_ref[dev].astype(F32)
        g_ref[...] = g
        d_ref[...], nm_ref[...], nv_ref[...] = _adamw_math(w_ref[...], g, m_ref[...], v_ref[...])

    spec = pl.BlockSpec((tr, cols), lambda i: (i, 0))
    return pl.pallas_call(
        body, name=name,
        out_shape=[jax.ShapeDtypeStruct((rows, cols), F32)] * 4,
        grid=(rows // tr,),
        in_specs=[pl.BlockSpec((N_DEV, tr, cols), lambda i: (0, i, 0))] + [spec] * 3, out_specs=[spec] * 4,
        compiler_params=_params(),
    )(parts, w, m, v)


def _finish_vectors(gathered, lb, params, loss_parts):
    names = list(params)
    n = len(names)

    def body(*refs):
        g_refs, lb_ref, loss_ref = refs[:n], refs[n], refs[n + 1]
        wmv_refs = refs[n + 2:4 * n + 2]
        out_refs, loss_out = refs[4 * n + 2:-1], refs[-1]
        sq = loss_ref[0]
        for dev in range(1, N_DEV):
            sq = sq + loss_ref[dev]
        sq = jnp.sum(jnp.sum(sq, axis=0, keepdims=True), axis=1, keepdims=True)
        loss_out[...] = sq * (0.5 / D_MODEL)
        me = 4 * lax.axis_index("x") + 2 * lax.axis_index("y") + lax.axis_index("c")
        for i, name in enumerate(names):
            g_ref = g_refs[i]
            w_ref, m_ref, v_ref = wmv_refs[3 * i:3 * i + 3]
            if name == "meta_tokens":
                width = w_ref.shape[1]
                mine = pl.ds(pl.multiple_of(me * width, width), width)
                g = g_ref[0, :, mine]
                for dev in range(1, N_DEV):
                    g = g + g_ref[dev, :, mine]
            else:
                g = g_ref[0]
                for dev in range(1, N_DEV):
                    g = g + g_ref[dev]
                g = jnp.sum(g, axis=0, keepdims=True)
                if name == "hg_norm_g":
                    g = functools.reduce(jnp.add, [g[:, h * HEAD_DIM:(h + 1) * HEAD_DIM] for h in range(HEADS)])
                if name == "lb_logits":
                    lbv = lb_ref[...]
                    g = g * lbv * (1.0 - lbv)
                    g = jnp.concatenate([g, -g], axis=0)
            outs = (g, *_adamw_math(w_ref[...], g, m_ref[...], v_ref[...]))
            for ref, val in zip(out_refs[4 * i:4 * i + 4], outs, strict=True):
                ref[...] = val

    args = [gathered[k] for k in names] + [lb, loss_parts] + [t for k in names for t in params[k]]
    res = pl.pallas_call(
        body, name="finish_vectors",
        out_shape=[jax.ShapeDtypeStruct(params[k][0].shape, F32) for k in names for _ in range(4)]
        + [jax.ShapeDtypeStruct((1, 1), F32)],
        compiler_params=_params(),
    )(*args)
    return {k: res[4 * i:4 * i + 4] for i, k in enumerate(names)}, res[-1].reshape(())


def _swap_halves(t):
    half = t.shape[-1] // 2
    return jnp.concatenate([t[..., half:], t[..., :half]], axis=-1)


def _pad_last(t, width):
    return jnp.concatenate([t, jnp.zeros(t.shape[:-1] + (width - t.shape[-1],), t.dtype)], axis=-1)


def _rope_tables(lp):
    pos = jnp.arange(lp, dtype=F32)
    inv_freq = 1.0 / (ROPE_THETA ** (jnp.arange(0, ROPE_DIM, 2, dtype=F32) / ROPE_DIM))
    ang = pos[:, None] * inv_freq[None, :]
    cos, sin = jnp.cos(ang), jnp.sin(ang)
    c128 = _pad_last(jnp.concatenate([cos, cos], axis=1), HEAD_DIM)
    s128 = _pad_last(jnp.concatenate([-sin, sin], axis=1), HEAD_DIM)
    return c128, s128


def _forward_backward(x, target, meta, w, small, *, lp, comm=None):
    batch, seq, d = x.shape
    rows = batch * lp
    tr = 272 if lp % 272 == 0 else 128
    tm = lp // 2
    bf = _MXU_DTYPE
    rw = functools.partial(_rowwise, rows=rows, tr=tr, lp=lp)

    c128, s128 = _rope_tables(lp)
    t_idx = jnp.arange(lp)
    real = jnp.broadcast_to(((t_idx >= N_META) & (t_idx < N_META + seq)).astype(F32)[:, None], (lp, _LANES))

    lb_logits = small["lb_logits"]
    lb = jax.nn.softmax(lb_logits, axis=0)[0:1]
    gh = jnp.tile(small["hg_norm_g"], (1, HEADS))

    h0 = _assemble("assemble_x", x, meta, lp)
    tgt = _assemble("assemble_target", target, jnp.zeros_like(meta), lp)

    (u1,), _ = rw("norm_mix_pre", lambda h, g: ([h * _rms_scale(h) * g], []),
                  ins=[("row", h0, d, 0), ("const", small["mix_pre_g"])], outs=[(d, bf)])
    p = _matmul("proj_in", u1, w["w_in"], out_dtype=F32, tm=tm, tn=1024, tk=1024)

    if comm is None:
        o_hg, z_a, states = _hgrn_fwd(p, lb, gh, batch=batch, lp=lp)
    else:
        o_hg, z_a, states, *gathered = _hgrn_fwd(p, lb, gh, batch=batch, lp=lp, ride=_Ride(comm.rest_payloads, True))
        w = {**w, **comm.rest_weights(gathered)}
    received = []
    scatter = lambda names: _Ride(comm.grad_parts(names, grads), False) if comm is not None else None
    y_a = _matmul("proj_hg_o", z_a, w["w_hg_o"], out_dtype=F32, tm=tm, tn=1024, tk=1024)

    def mla_pre(pc, gq, gkv, ct, st):
        cq, ckv = pc[:, 0:Q_LORA], pc[:, Q_LORA:Q_LORA + KV_LORA]
        kpe, kpe_sw = pc[:, 512:640], pc[:, 640:768]
        return [cq * _rms_scale(cq) * gq, ckv * _rms_scale(ckv) * gkv, kpe * ct + kpe_sw * st], []

    (cqn, ckvn, kp), _ = rw("mla_pre", mla_pre,
                            ins=[("row", p, 1024, CB_C), ("const", small["q_a_norm_g"]),
                                 ("const", small["kv_a_norm_g"]), ("pos", c128), ("pos", s128)],
                            outs=[(Q_LORA, bf), (KV_LORA, bf), (HEAD_DIM, bf)])
    q_cat = _proj_q_rope(cqn, w["w_q"], c128, s128, tm=tm, lp=lp)
    kv = _matmul("proj_kv_b", ckvn, w["w_kv"], out_dtype=bf, tm=tm, tn=1024, tk=KV_LORA)
    o_at, lse = _attn_fwd(q_cat, kv, kp, batch=batch, lp=lp)
    te, te_small = tm, lp // 4

    def merge(yb, pa, pb, ya, bg):
        ga, gb = _sigmoid(pa + bg[:, :d]), _sigmoid(pb + bg[:, d:])
        return [yb, ga * ya + gb * yb], []

    (y_b, mix), _ = _matmul_segments(
        "proj_mla_o", [o_at], w["w_mla_o"], tm=te_small, tn=d, tk=1024,
        epilogue=_Epilogue(merge, rows=[(p, 1024, CB_GA), (p, 1024, CB_GB), y_a], consts=[small["b_gate"]],
                           outs=[(d, F32), (d, bf)], lp=lp))
    def post_mix(mx_, h, g2, g3):
        h1_ = h + mx_ * _rms_scale(mx_) * g2
        return [mx_, h1_, h1_ * _rms_scale(h1_) * g3], []

    (mixed, h1, u2), _ = _matmul_segments(
        "proj_out", [mix], w["w_out"], tm=te, tn=d, tk=1024,
        epilogue=_Epilogue(post_mix, rows=[h0], consts=[small["mix_post_g"], small["ffn_pre_g"]],
                           outs=[(d, F32), (d, F32), (d, bf)], lp=lp))
    act, act_dgate, act_dup = _ffn_in_swiglu(u2, w["w_ffn_in"], tm=tm, tn=1408)

    def post_ffn(fo_, h1_, t_, g4, mask):
        r = _rms_scale(fo_)
        h2 = h1_ + fo_ * r * g4
        err = (h2 - t_) * mask[:, 0:1]
        dh2 = err * (1.0 / d)
        dfo, dg4 = _rms_bwd(fo_, g4, dh2)
        return [dh2, dfo], [err * err, dg4]

    (dh2, dfo), (loss_vec, dg_ffn_post) = _matmul_segments(
        "ffn_out", [act], w["w_ffn_out"], tm=te, tn=d, tk=1408,
        epilogue=_Epilogue(post_ffn, rows=[h1, tgt], consts=[small["ffn_post_g"]], pos=[real],
                           outs=[(d, F32), (d, bf)], accs=[d, d], lp=lp))
    loss = (0.5 / d) * jnp.sum(loss_vec)

    grads = {}
    dw_dt = F32 if comm is None else _WIRE_DTYPE
    dgt, dup = _d_ffn_out_swiglu(dfo, w["w_ffn_out"], act_dgate, act_dup, tm=tm, tn=1408)
    grads["w_ffn_out"] = _matmul_tn("dw_ffn_out", act, dfo, tk=1408, tn=1024, tr=tm, out_dtype=dw_dt)
    grads["w_ffn_in"] = jnp.concatenate([_matmul_tn("dw_ffn_in_gate", u2, dgt, tk=1024, tn=FFN_HIDDEN, tr=tm),
                                         _matmul_tn("dw_ffn_in_up", u2, dup, tk=1024, tn=FFN_HIDDEN, tr=tm)], axis=1)

    def post_mix_bwd(du2_, h1_, dh2_, mx_, g3, g2):
        dx, dg3 = _rms_bwd(h1_, g3, du2_)
        dh1_ = dh2_ + dx
        dmx, dg2 = _rms_bwd(mx_, g2, dh1_)
        return [dh1_, dmx], [dg3, dg2]

    (dh1, dmixed), (dg_ffn_pre, dg_mix_post) = _matmul_segments(
        "d_ffn_in", [dgt, dup], w["w_ffn_in"], tm=te_small, tn=d, tk=1408, b_transposed=True,
        epilogue=_Epilogue(post_mix_bwd, rows=[h1, dh2, mixed], consts=[small["ffn_pre_g"], small["mix_post_g"]],
                           outs=[(d, F32), (d, bf)], accs=[d, d], lp=lp))
    grads["w_out"] = _matmul_tn("dw_out", mix, dmixed, tk=1024, tn=1024, tr=tm, out_dtype=dw_dt)

    def merge_bwd(dm, pa, pb, ya, yb, bg):
        ga, gb = _sigmoid(pa + bg[:, :d]), _sigmoid(pb + bg[:, d:])
        dpg = jnp.concatenate([dm * ya * ga * (1.0 - ga), dm * yb * gb * (1.0 - gb)], axis=1)
        return [dpg, dm * ga, dm * gb], [dpg]

    (dpg, dya, dyb), (db_gate,) = _matmul_segments(
        "d_proj_out", [dmixed], w["w_out"], tm=te_small, tn=d, tk=1024, b_transposed=True,
        epilogue=_Epilogue(merge_bwd, rows=[(p, 1024, CB_GA), (p, 1024, CB_GB), y_a, y_b], consts=[small["b_gate"]],
                           outs=[(2 * d, bf), (d, bf), (d, bf)], accs=[2 * d], lp=lp))
    dz_a = _matmul("d_proj_hg_o", dya, w["w_hg_o"], out_dtype=F32, tm=tm, tn=1024, tk=1024, b_transposed=True)
    grads["w_hg_o"] = _matmul_tn("dw_hg_o", z_a, dya, tk=1024, tn=1024, tr=tm, out_dtype=dw_dt)
    do_at = _matmul("d_proj_mla_o", dyb, w["w_mla_o"], out_dtype=bf, tm=tm, tn=1024, tk=1024, b_transposed=True)
    grads["w_mla_o"] = _matmul_tn("dw_mla_o", o_at, dyb, tk=1024, tn=1024, tr=tm, out_dtype=dw_dt)

    dph, dlb, dgh, *got = _hgrn_bwd(p, o_hg, dz_a, states, lb, gh, batch=batch, lp=lp,
                                    ride=scatter(_GRAD_GROUPS[0]))
    received.append(got)

    res = _attn_bwd(q_cat, kv, kp, do_at, o_at, lse, c128, s128, batch=batch, lp=lp, ride=scatter(_GRAD_GROUPS[1]))
    dq_parts, (dkn, dkp, dvv) = list(res[:3]), res[3:6]
    received.append(list(res[6:]))
    dcqn = _matmul_segments("d_proj_q_b", dq_parts, w["w_q"], tm=tm, tn=Q_LORA, tk=1024, b_transposed=True)
    grads["w_q"] = jnp.concatenate([_matmul_tn(f"dw_q_b_{i}", cqn, part, tk=Q_LORA, tn=1024, tr=tm)
                                    for i, part in enumerate(dq_parts)], axis=1)
    dckvn = _matmul_segments("d_proj_kv_b", [dkn, dvv], w["w_kv"], tm=tm, tn=KV_LORA, tk=1024, b_transposed=True)
    grads["w_k"] = _matmul_tn("dw_k_b", ckvn, dkn, tk=KV_LORA, tn=1024, tr=tm)
    grads["w_v"] = _matmul_tn("dw_v_b", ckvn, dvv, tk=KV_LORA, tn=1024, tr=tm)

    def mla_pre_bwd(pc, dq_, dkv_, dkp_, gq, gkv, ct, st):
        cq, ckv = pc[:, 0:Q_LORA], pc[:, Q_LORA:Q_LORA + KV_LORA]
        dcq, dgq = _rms_bwd(cq, gq, dq_)
        dckv, dgkv = _rms_bwd(ckv, gkv, dkv_)
        dpc = jnp.concatenate([dcq, dckv, dkp_ * ct, dkp_ * st, jnp.zeros((pc.shape[0], 256), F32)], axis=1)
        return [dpc], [dgq, dgkv]

    (dpc,), (dg_q, dg_kv) = rw(
        "mla_pre_bwd", mla_pre_bwd,
        ins=[("row", p, 1024, CB_C), ("row", dcqn, Q_LORA, 0), ("row", dckvn, KV_LORA, 0), ("row", dkp, HEAD_DIM, 0),
             ("const", small["q_a_norm_g"]), ("const", small["kv_a_norm_g"]), ("pos", c128), ("pos", s128)],
        outs=[(1024, bf)], accs=[Q_LORA, KV_LORA])

    grads["w_in"] = (_matmul_tn("dw_in_h", u1, dph, tk=1024, tn=1024, tr=tm),
                     _matmul_tn("dw_in_c", u1, dpc, tk=1024, tn=1024, tr=tm),
                     _matmul_tn("dw_in_g", u1, dpg, tk=1024, tn=1024, tr=tm))
    def pre_bwd(du, h, dh, g1):
        dx, dg1 = _rms_bwd(h, g1, du)
        return [dh + dx], [dg1]

    (dh0,), (dg_mix_pre,), *got = _matmul_segments(
        "d_proj_in", [dph, dpc, dpg], w["w_in"], tm=te, tn=d, tk=1024, b_transposed=True,
        ride=scatter(_GRAD_GROUPS[2]),
        epilogue=_Epilogue(pre_bwd, rows=[h0, dh1], consts=[small["mix_pre_g"]], outs=[(d, F32)], accs=[d], lp=lp))
    if comm is not None:
        received.append(got)
    grad_x = dh0.reshape(batch, lp, d)[:, N_META:N_META + seq]
    partial = {"meta_tokens": _meta_grad(dh0, batch, lp), "lb_logits": dlb, "b_gate": db_gate, "hg_norm_g": dgh,
               "q_a_norm_g": dg_q, "kv_a_norm_g": dg_kv, "mix_pre_g": dg_mix_pre, "mix_post_g": dg_mix_post,
               "ffn_pre_g": dg_ffn_pre, "ffn_post_g": dg_ffn_post, "loss": loss_vec}
    return loss, grad_x, grads, partial, lb, received


_BIG = ["w_in", "w_hg_o", "w_q_b", "w_kv_b", "w_mla_o", "w_out", "w_ffn_in", "w_ffn_out"]
_COLUMN_SHARDED = {"w_in", "w_q_b", "w_kv_b", "w_ffn_in"}
_GRAD_GROUPS = [["w_ffn_in", "w_ffn_out"], ["w_out", "w_hg_o", "w_mla_o"], ["w_in", "w_q_b", "w_kv_b"]]
_SMALL = ["b_gate", "lb_logits", "hg_norm_g", "q_a_norm_g", "kv_a_norm_g", "mix_pre_g", "mix_post_g",
          "ffn_pre_g", "ffn_post_g"]


def _gathered_matrix(name, t):
    _, k, n = t.shape
    if name in _COLUMN_SHARDED:
        return t.transpose(1, 0, 2).reshape(k, N_DEV * n)
    return t.reshape(N_DEV * k, n)


def _scatter_layout(name, full):
    kk, nn = full.shape
    if name in _COLUMN_SHARDED:
        t = full.reshape(kk, N_DEV, nn // N_DEV).transpose(1, 0, 2)
    else:
        t = full.reshape(N_DEV, kk // N_DEV, nn)
    return t.astype(_WIRE_DTYPE)


def _model_w_in(wi):
    z = lambda *s: jnp.zeros(s, wi.dtype)
    kpe = wi[:, 4608:4672]
    c_blk = jnp.concatenate([wi[:, 4096:4608], kpe, z(1024, 64), _swap_halves(kpe), z(1024, 64), z(1024, 256)], axis=1)
    return {"w_in": jnp.concatenate([wi[:, :4096], c_blk, wi[:, 4672:]], axis=1).astype(_MXU_DTYPE)}


def _model_weights(full):
    return {**_model_w_in(full["w_in"]), **_model_rest(full)}


def _model_rest(full):
    wq3 = full["w_q_b"].reshape(Q_LORA, HEADS, HEAD_DIM + ROPE_DIM)
    pe = wq3[:, :, HEAD_DIM:]
    w_q = jnp.concatenate([wq3[:, :, :HEAD_DIM].reshape(Q_LORA, -1),
                           _pad_last(pe, HEAD_DIM).reshape(Q_LORA, -1),
                           _pad_last(_swap_halves(pe), HEAD_DIM).reshape(Q_LORA, -1)], axis=1)
    wkv3 = full["w_kv_b"].reshape(KV_LORA, HEADS, 2 * HEAD_DIM)
    w_k = wkv3[:, :, :HEAD_DIM].reshape(KV_LORA, -1)
    w_v = wkv3[:, :, HEAD_DIM:].reshape(KV_LORA, -1)
    w = {"w_q": w_q, "w_kv": jnp.concatenate([w_k, w_v], axis=1),
         "w_hg_o": full["w_hg_o"], "w_mla_o": full["w_mla_o"], "w_out": full["w_out"],
         "w_ffn_in": full["w_ffn_in"], "w_ffn_out": full["w_ffn_out"]}
    return {k: v.astype(_MXU_DTYPE) for k, v in w.items()}


def _reference_layout_grad(name, g):
    if name == "w_in":
        g_h, g_c, g_g = g["w_in"]
        d_kpe = g_c[:, 512:576] + _swap_halves(g_c[:, 640:704])
        return jnp.concatenate([g_h, g_c[:, :512], d_kpe, g_g], axis=1)
    if name == "w_q_b":
        gq = g["w_q"]
        d_pe = (gq[:, 1024:2048].reshape(Q_LORA, HEADS, HEAD_DIM)[:, :, :ROPE_DIM]
                + _swap_halves(gq[:, 2048:].reshape(Q_LORA, HEADS, HEAD_DIM)[:, :, :ROPE_DIM]))
        return jnp.concatenate([gq[:, :1024].reshape(Q_LORA, HEADS, HEAD_DIM), d_pe], axis=2).reshape(Q_LORA, -1)
    if name == "w_kv_b":
        return jnp.concatenate([g["w_k"].reshape(KV_LORA, HEADS, HEAD_DIM),
                                g["w_v"].reshape(KV_LORA, HEADS, HEAD_DIM)], axis=2).reshape(KV_LORA, -1)
    return g[name]


def _reference_layout_grads(g):
    return {n: _reference_layout_grad(n, g) for n in _BIG}


class _Comm:
    def __init__(self, shard):
        self.rest_payloads = [shard[n].astype(_WIRE_DTYPE) for n in _BIG[1:]]

    def rest_weights(self, gathered):
        return _model_rest({n: _gathered_matrix(n, t) for n, t in zip(_BIG[1:], gathered, strict=True)})

    def grad_parts(self, names, g):
        return [_scatter_layout(n, _reference_layout_grad(n, g)) for n in names]


def kernel(x, meta_tokens, w_in, b_gate, lb_logits, hg_norm_g, w_hg_o, q_a_norm_g, w_q_b, kv_a_norm_g, w_kv_b, w_mla_o, w_out, mix_pre_g, mix_post_g, ffn_pre_g, ffn_post_g, w_ffn_in, w_ffn_out, loss_target, m_meta_tokens, m_w_in, m_b_gate, m_lb_logits, m_hg_norm_g, m_w_hg_o, m_q_a_norm_g, m_w_q_b, m_kv_a_norm_g, m_w_kv_b, m_w_mla_o, m_w_out, m_mix_pre_g, m_mix_post_g, m_ffn_pre_g, m_ffn_post_g, m_w_ffn_in, m_w_ffn_out, v_meta_tokens, v_w_in, v_b_gate, v_lb_logits, v_hg_norm_g, v_w_hg_o, v_q_a_norm_g, v_w_q_b, v_kv_a_norm_g, v_w_kv_b, v_w_mla_o, v_w_out, v_mix_pre_g, v_mix_post_g, v_ffn_pre_g, v_ffn_post_g, v_w_ffn_in, v_w_ffn_out):
    args = dict(locals())
    batch, seq, d = x.shape
    lp = -(-(N_META + seq) // _LANES) * _LANES
    weight_names = ["meta_tokens", "w_in", "b_gate", "lb_logits", "hg_norm_g", "w_hg_o", "q_a_norm_g", "w_q_b",
                    "kv_a_norm_g", "w_kv_b", "w_mla_o", "w_out", "mix_pre_g", "mix_post_g", "ffn_pre_g",
                    "ffn_post_g", "w_ffn_in", "w_ffn_out"]
    shard = {n: args[n].reshape(args[n].shape[-2:]) for n in _BIG}
    comm = _Comm(shard)

    w_in_all, meta_all = _all_gather("gather_first", [shard["w_in"].astype(_WIRE_DTYPE), meta_tokens])
    w_first = _model_w_in(_gathered_matrix("w_in", w_in_all))
    meta_full = meta_all.transpose(1, 0, 2).reshape(N_META, d)
    small = {n: args[n] for n in _SMALL}

    _, grad_x, _, partial, lb, received = _forward_backward(x, loss_target, meta_full, w_first, small, lp=lp, comm=comm)
    out = {}
    for names, bufs in zip(_GRAD_GROUPS, received, strict=True):
        for n, buf in zip(names, bufs, strict=True):
            two_d = lambda t: t.reshape(t.shape[-2:])
            res = _sum_adamw("adamw_" + n, buf, shard[n], two_d(args["m_" + n]), two_d(args["v_" + n]))
            out[n] = [t.reshape(args[n].shape) for t in res]

    vec_names = _SMALL + ["meta_tokens"]
    *gathered, loss_parts = _all_gather("gather_vectors", [partial[n] for n in vec_names + ["loss"]])
    finished, loss = _finish_vectors(dict(zip(vec_names, gathered, strict=True)), lb,
                                     {n: (args[n], args["m_" + n], args["v_" + n]) for n in vec_names}, loss_parts)
    out.update(finished)
    return (loss, grad_x, *[out[n][i] for i in range(4) for n in weight_names])
```

```python
import functools

import jax
import jax.numpy as jnp
from jax import lax
from jax.experimental import pallas as pl
from jax.experimental.pallas import tpu as pltpu

F32 = jnp.float32
_MXU_DTYPE = jnp.bfloat16
_WIRE_DTYPE = jnp.bfloat16
_VMEM_LIMIT_BYTES = 56 * 1024 * 1024
_LANES = 128
_SUBLANES = 8

N_DEV = 8
N_META = 16
NORM_EPS = 1e-6
HEADS = 8
HEAD_DIM = 128
ROPE_DIM = 64
HG_CHUNK = 16
HG_BLOCK = 128
ROPE_THETA = 10000.0
D_MODEL = 1024
Q_LORA = 256
KV_LORA = 256
FFN_HIDDEN = 2816
ATTN_SCALE = (HEAD_DIM + ROPE_DIM) ** -0.5
NEG_BIG = -1e30

ADAM_LR = 0.001
ADAM_B1 = 0.9
ADAM_B2 = 0.999
ADAM_EPS = 1e-08
ADAM_WD = 0.01
ADAM_STEP = 10

CB_HQ, CB_HF, CB_HI, CB_HG, CB_C, CB_GA, CB_GB = range(7)
IN_COLS_PADDED = 7 * 1024


def _params(**kw):
    return pltpu.CompilerParams(vmem_limit_bytes=_VMEM_LIMIT_BYTES, **kw)


def _dot(a, b):
    return lax.dot_general(a, b, (((1,), (0,)), ((), ())), preferred_element_type=F32)


def _dot_nt(a, b):
    return lax.dot_general(a, b, (((1,), (1,)), ((), ())), preferred_element_type=F32)


def _dot_tn(a, b):
    return lax.dot_general(a, b, (((0,), (0,)), ((), ())), preferred_element_type=F32)


def _mx(x):
    return x.astype(_MXU_DTYPE)


def _exact_dot(m01, x, dot=_dot):
    if _MXU_DTYPE == jnp.float32:
        return dot(m01.astype(F32), x)
    m = m01.astype(jnp.bfloat16)
    x1 = x.astype(jnp.bfloat16)
    x2 = (x - x1.astype(F32)).astype(jnp.bfloat16)
    return dot(m, x1) + dot(m, x2)


def _exact_dot_nt(m01, x):
    return _exact_dot(m01, x, dot=_dot_nt)


def _sigmoid(x):
    return jax.nn.sigmoid(x)


def _silu_grad(x, s):
    return s * (1.0 + x * (1.0 - s))


def _rms_scale(x):
    return lax.rsqrt(jnp.mean(x * x, axis=-1, keepdims=True) + NORM_EPS)


def _rms_bwd(x, g, dy):
    r = _rms_scale(x)
    xh = x * r
    w = dy * g
    dx = r * (w - xh * jnp.mean(xh * w, axis=-1, keepdims=True))
    return dx, dy * xh


def _heads(fn, *arrays):
    outs = [fn(*[a[:, h * HEAD_DIM:(h + 1) * HEAD_DIM] for a in arrays]) for h in range(HEADS)]
    if isinstance(outs[0], tuple):
        return tuple(jnp.concatenate([o[i] for o in outs], axis=1) for i in range(len(outs[0])))
    return jnp.concatenate(outs, axis=1)


class _Ride:
    def __init__(self, payloads, gather):
        self.gather, self.args, self.n = gather, list(payloads), len(payloads)
        self.in_specs = [pl.BlockSpec(memory_space=pl.ANY)] * self.n
        self.out_shape = [jax.ShapeDtypeStruct((N_DEV, *p.shape[-2:]), p.dtype) for p in payloads]
        self.out_specs = [pl.BlockSpec(memory_space=pl.ANY)] * self.n
        self.scratch = [pltpu.SemaphoreType.DMA((self.n, N_DEV - 1)), pltpu.SemaphoreType.DMA((self.n, N_DEV - 1)),
                        pltpu.SemaphoreType.DMA((self.n,))]

    def split(self, rest, n_outs):
        n = self.n
        mine = (rest[:n], rest[n + n_outs:2 * n + n_outs], rest[-3:])
        return rest[n:n + n_outs] + rest[2 * n + n_outs:-3], mine

    def _copies(self, p_refs, out_refs, sems):
        send_sems, recv_sems, local_sems = sems
        x, y, c = lax.axis_index("x"), lax.axis_index("y"), lax.axis_index("c")
        me = 4 * x + 2 * y + c
        copies = []
        for i, (p_ref, out_ref) in enumerate(zip(p_refs, out_refs, strict=True)):
            part = (lambda j, p_ref=p_ref: p_ref) if self.gather else (lambda j, p_ref=p_ref: p_ref.at[j])
            copies.append(pltpu.make_async_copy(part(me), out_ref.at[me], local_sems.at[i]))
            for k in range(1, N_DEV):
                px, py, pc = x ^ (k >> 2), y ^ ((k >> 1) & 1), c ^ (k & 1)
                copies.append(pltpu.make_async_remote_copy(
                    src_ref=part(4 * px + 2 * py + pc), dst_ref=out_ref.at[me],
                    send_sem=send_sems.at[i, k - 1], recv_sem=recv_sems.at[i, k - 1],
                    device_id=(px, py, pc), device_id_type=pl.DeviceIdType.MESH))
        return copies

    def run(self, grid, refs):
        ids = [pl.program_id(i) for i in range(len(grid))]
        first = functools.reduce(jnp.logical_and, [i == 0 for i in ids])
        last = functools.reduce(jnp.logical_and, [i == g - 1 for i, g in zip(ids, grid)])

        @pl.when(first)
        def _():
            for cp in self._copies(*refs):
                cp.start()

        @pl.when(last)
        def _():
            for cp in self._copies(*refs):
                cp.wait()


class _NoRide:
    in_specs, out_shape, out_specs, scratch, args = [], [], [], [], []


def _matmul(name, a, b, *, out_dtype, tm, tn, tk, c_in=None, ride=None, b_transposed=False):
    m, k = a.shape
    n = b.shape[0] if b_transposed else b.shape[1]
    assert m % tm == 0 and n % tn == 0 and k % tk == 0, (name, a.shape, b.shape, tm, tn, tk)
    nk = k // tk
    has_c = c_in is not None
    dot = _dot_nt if b_transposed else _dot
    grid = (n // tn, m // tm, nk)
    n_in = 2 + has_c

    def body(*refs):
        a_ref, b_ref = refs[0], refs[1]
        c_ref = refs[2] if has_c else None
        rest = refs[n_in:]
        if ride is not None:
            rest, exchange = ride.split(rest, 1)
            ride.run(grid, exchange)
        o_ref = rest[0]
        acc_ref = rest[1] if nk > 1 else None

        def finish(r):
            if has_c:
                r = r + c_ref[...]
            o_ref[...] = r.astype(o_ref.dtype)

        if nk == 1:
            finish(dot(a_ref[...], b_ref[...]))
        else:
            kk = pl.program_id(2)

            @pl.when(kk == 0)
            def _():
                acc_ref[...] = jnp.zeros_like(acc_ref)

            acc_ref[...] += dot(a_ref[...], b_ref[...])

            @pl.when(kk == nk - 1)
            def _():
                finish(acc_ref[...])

    in_specs = [pl.BlockSpec((tm, tk), lambda j, i, kk: (i, kk)),
                pl.BlockSpec((tn, tk), lambda j, i, kk: (j, kk)) if b_transposed
                else pl.BlockSpec((tk, tn), lambda j, i, kk: (kk, j))]
    args = [a, b]
    aliases = {}
    if has_c:
        in_specs.append(pl.BlockSpec((tm, tn), lambda j, i, kk: (i, j)))
        args.append(c_in)
        aliases = {2: 0}
    out_shape = [jax.ShapeDtypeStruct((m, n), out_dtype)]
    out_specs = [pl.BlockSpec((tm, tn), lambda j, i, kk: (i, j))]
    scratch = [pltpu.VMEM((tm, tn), F32)] if nk > 1 else []
    if ride is not None:
        in_specs, args = in_specs + ride.in_specs, args + ride.args
        out_shape, out_specs, scratch = out_shape + ride.out_shape, out_specs + ride.out_specs, scratch + ride.scratch
    res = pl.pallas_call(
        body, name=name, out_shape=out_shape, grid=grid, in_specs=in_specs, out_specs=out_specs,
        scratch_shapes=scratch, input_output_aliases=aliases, compiler_params=_params(),
    )(*args)
    return res[0] if ride is None else res


EPILOGUE_ROWS = 272


class _Epilogue:
    def __init__(self, fn, *, rows=(), consts=(), pos=(), outs=(), accs=(), lp=None):
        self.fn, self.rows, self.consts, self.pos = fn, list(rows), list(consts), list(pos)
        self.outs, self.accs, self.lp = list(outs), list(accs), lp


def _matmul_segments(name, a_list, b, *, out_dtype=F32, tm, tn, tk, ride=None, b_transposed=False, epilogue=None):
    m = a_list[0].shape[0]
    n, k = b.shape if b_transposed else b.shape[::-1]
    steps = [a.shape[1] // tk for a in a_list]
    offs = [sum(steps[:s]) for s in range(len(steps))]
    nk = sum(steps)
    assert nk * tk == k and m % tm == 0 and n % tn == 0 and all(a.shape[1] % tk == 0 for a in a_list), name
    grid = (n // tn, m // tm, nk)
    n_seg = len(a_list)
    dot = _dot_nt if b_transposed else _dot
    ep = epilogue
    assert ep is None or tn == n, name
    n_extra = 0 if ep is None else len(ep.rows) + len(ep.consts) + len(ep.pos)
    n_outs = 1 if ep is None else len(ep.outs) + len(ep.accs)

    def body(*refs):
        a_refs, b_ref = refs[:n_seg], refs[n_seg]
        extra_refs, rest = refs[n_seg + 1:n_seg + 1 + n_extra], refs[n_seg + 1 + n_extra:]
        if ride is not None:
            rest, exchange = ride.split(rest, n_outs)
            ride.run(grid, exchange)
        out_refs, acc_ref = rest[:n_outs], rest[n_outs]
        i, kk = pl.program_id(1), pl.program_id(2)

        @pl.when(kk == 0)
        def _():
            acc_ref[...] = jnp.zeros_like(acc_ref)

        for s in range(n_seg):
            @pl.when((kk >= offs[s]) & (kk < offs[s] + steps[s]))
            def _(s=s):
                acc_ref[...] += dot(a_refs[s][...], b_ref[...])

        if ep is None:
            @pl.when(kk == nk - 1)
            def _():
                out_refs[0][...] = acc_ref[...].astype(out_refs[0].dtype)
        else:
            sum_refs = out_refs[len(ep.outs):]

            @pl.when((kk == 0) & (i == 0))
            def _():
                for ref in sum_refs:
                    ref[...] = jnp.zeros_like(ref)

            @pl.when(kk == nk - 1)
            def _():
                rs = EPILOGUE_ROWS if tm % EPILOGUE_ROWS == 0 else tm
                n_r, n_c = len(ep.rows), len(ep.consts)
                for r0 in range(0, tm, rs):
                    sl = slice(r0, r0 + rs)
                    tiles = ([r[sl, :] for r in extra_refs[:n_r]] + [c[...] for c in extra_refs[n_r:n_r + n_c]]
                             + [t[sl, :] for t in extra_refs[n_r + n_c:]])
                    res_outs, res_sums = ep.fn(acc_ref[sl, :], *tiles)
                    for ref, val in zip(out_refs[:len(ep.outs)], res_outs, strict=True):
                        ref[sl, :] = val.astype(ref.dtype)
                    for ref, val in zip(sum_refs, res_sums, strict=True):
                        ref[...] += val.reshape(rs // _SUBLANES, _SUBLANES, val.shape[-1]).sum(axis=0)

    seg_spec = lambda s: pl.BlockSpec(
        (tm, tk), functools.partial(lambda j, i, kk, off, ns: (i, jnp.clip(kk - off, 0, ns - 1)), off=offs[s], ns=steps[s]))
    b_spec = (pl.BlockSpec((tn, tk), lambda j, i, kk: (j, kk)) if b_transposed
              else pl.BlockSpec((tk, tn), lambda j, i, kk: (kk, j)))
    in_specs = [seg_spec(s) for s in range(n_seg)] + [b_spec]
    args = list(a_list) + [b]
    row_spec = lambda w: pl.BlockSpec((tm, w), lambda j, i, kk: (i, 0))
    if ep is None:
        out_shape = [jax.ShapeDtypeStruct((m, n), out_dtype)]
        out_specs = [pl.BlockSpec((tm, tn), lambda j, i, kk: (i, j))]
    else:
        tiles_per_example = ep.lp // tm
        row_ins = [r if isinstance(r, tuple) else (r, r.shape[1], 0) for r in ep.rows]
        in_specs += ([pl.BlockSpec((tm, wd), functools.partial(lambda j, i, kk, cb: (i, cb), cb=cb)) for _, wd, cb in row_ins]
                     + [pl.BlockSpec(c.shape, lambda j, i, kk: (0, 0)) for c in ep.consts]
                     + [pl.BlockSpec((tm, p.shape[1]), lambda j, i, kk: (i % tiles_per_example, 0)) for p in ep.pos])
        args += [arr for arr, _, _ in row_ins] + ep.consts + ep.pos
        out_shape = ([jax.ShapeDtypeStruct((m, w), dt) for w, dt in ep.outs]
                     + [jax.ShapeDtypeStruct((_SUBLANES, w), F32) for w in ep.accs])
        out_specs = ([row_spec(w) for w, _ in ep.outs]
                     + [pl.BlockSpec((_SUBLANES, w), lambda j, i, kk: (0, 0)) for w in ep.accs])
    scratch = [pltpu.VMEM((tm, tn), F32)]
    if ride is not None:
        in_specs, args = in_specs + ride.in_specs, args + ride.args
        out_shape, out_specs, scratch = out_shape + ride.out_shape, out_specs + ride.out_specs, scratch + ride.scratch
    res = pl.pallas_call(
        body, name=name, out_shape=out_shape, grid=grid, in_specs=in_specs, out_specs=out_specs,
        scratch_shapes=scratch, compiler_params=_params(),
    )(*args)
    if ep is None:
        return res[0] if ride is None else res
    n_o = len(ep.outs)
    return (res[:n_o], res[n_o:n_outs], *res[n_outs:])


def _matmul_tn(name, x, dy, *, tk, tn, tr, out_dtype=F32):
    r, k = x.shape
    _, n = dy.shape
    assert r % tr == 0 and k % tk == 0 and n % tn == 0, (name, x.shape, dy.shape)
    n_r = r // tr
    direct = out_dtype == F32

    def body(x_ref, dy_ref, o_ref, *scratch):
        acc_ref = o_ref if direct else scratch[0]

        @pl.when(pl.program_id(2) == 0)
        def _():
            acc_ref[...] = jnp.zeros_like(acc_ref)

        acc_ref[...] += _dot_tn(x_ref[...], dy_ref[...])
        if not direct:
            @pl.when(pl.program_id(2) == n_r - 1)
            def _():
                o_ref[...] = acc_ref[...].astype(o_ref.dtype)

    return pl.pallas_call(
        body, name=name,
        out_shape=jax.ShapeDtypeStruct((k, n), out_dtype),
        grid=(k // tk, n // tn, n_r),
        in_specs=[pl.BlockSpec((tr, tk), lambda kb, nb, rr: (rr, kb)),
                  pl.BlockSpec((tr, tn), lambda kb, nb, rr: (rr, nb))],
        out_specs=pl.BlockSpec((tk, tn), lambda kb, nb, rr: (kb, nb)),
        scratch_shapes=[] if direct else [pltpu.VMEM((tk, tn), F32)],
        compiler_params=_params(),
    )(x, dy)


def _ffn_in_swiglu(u, w, *, tm, tn):
    r, k = u.shape
    h = w.shape[1] // 2
    assert r % tm == 0 and h % tn == 0
    nj = h // tn

    def body(u_ref, wg_ref, wu_ref, act_ref, dgate_ref, dup_ref):
        uu = u_ref[...]
        gt, up = _dot(uu, wg_ref[...]), _dot(uu, wu_ref[...])
        s = _sigmoid(gt)
        silu = gt * s
        act_ref[...] = (silu * up).astype(act_ref.dtype)
        dgate_ref[...] = (up * _silu_grad(gt, s)).astype(dgate_ref.dtype)
        dup_ref[...] = silu.astype(dup_ref.dtype)

    tile = pl.BlockSpec((tm, tn), lambda j, i: (i, j))
    return pl.pallas_call(
        body, name="ffn_in_swiglu",
        out_shape=[jax.ShapeDtypeStruct((r, h), _MXU_DTYPE)] * 3,
        grid=(nj, r // tm),
        in_specs=[pl.BlockSpec((tm, k), lambda j, i: (i, 0)),
                  pl.BlockSpec((k, tn), lambda j, i: (0, j)),
                  pl.BlockSpec((k, tn), lambda j, i: (0, nj + j))],
        out_specs=[tile] * 3,
        compiler_params=_params(),
    )(u, w, w)


def _d_ffn_out_swiglu(dy, w, act_dgate, act_dup, *, tm, tn):
    r, k = dy.shape
    h = w.shape[0]
    assert r % tm == 0 and h % tn == 0

    def body(dy_ref, w_ref, pg_ref, pu_ref, dgt_ref, dup_ref):
        da = _dot_nt(dy_ref[...], w_ref[...])
        dgt_ref[...] = (da * pg_ref[...].astype(F32)).astype(dgt_ref.dtype)
        dup_ref[...] = (da * pu_ref[...].astype(F32)).astype(dup_ref.dtype)

    tile = pl.BlockSpec((tm, tn), lambda j, i: (i, j))
    return pl.pallas_call(
        body, name="d_ffn_out_swiglu",
        out_shape=[jax.ShapeDtypeStruct((r, h), _MXU_DTYPE)] * 2,
        grid=(h // tn, r // tm),
        in_specs=[pl.BlockSpec((tm, k), lambda j, i: (i, 0)), pl.BlockSpec((tn, k), lambda j, i: (j, 0)), tile, tile],
        out_specs=[tile] * 2,
        compiler_params=_params(),
    )(dy, w, act_dgate, act_dup)


def _proj_q_rope(cqn, w_q, c_tab, s_tab, *, tm, lp):
    r, k = cqn.shape
    tiles_per_example = lp // tm
    pair = 2 * HEAD_DIM

    def body(x_ref, wn_ref, wp_ref, ws_ref, c_ref, s_ref, o_ref):
        x = x_ref[...]
        c2, s2 = jnp.tile(c_ref[...], (1, 2)), jnp.tile(s_ref[...], (1, 2))
        nope = _dot(x, wn_ref[...])
        roped = _dot(x, wp_ref[...]) * c2 + _dot(x, ws_ref[...]) * s2
        hs = lambda t, h: t[:, h * HEAD_DIM:(h + 1) * HEAD_DIM]
        o_ref[...] = jnp.concatenate([hs(nope, 0), hs(roped, 0), hs(nope, 1), hs(roped, 1)], axis=1).astype(o_ref.dtype)

    w_blk = lambda part: pl.BlockSpec((k, pair), functools.partial(lambda h, i, part: (0, part * (HEADS // 2) + h), part=part))
    tab = pl.BlockSpec((tm, HEAD_DIM), lambda h, i: (i % tiles_per_example, 0))
    return pl.pallas_call(
        body, name="proj_q_rope",
        out_shape=jax.ShapeDtypeStruct((r, HEADS * QK_DIM), _MXU_DTYPE),
        grid=(HEADS // 2, r // tm),
        in_specs=[pl.BlockSpec((tm, k), lambda h, i: (i, 0)), w_blk(0), w_blk(1), w_blk(2), tab, tab],
        out_specs=pl.BlockSpec((tm, 2 * QK_DIM), lambda h, i: (i, h)),
        compiler_params=_params(),
    )(cqn, w_q, w_q, w_q, c_tab, s_tab)


def _rowwise(name, body, *, rows, tr, lp, ins, outs, accs=()):
    assert rows % tr == 0 and lp % tr == 0 and tr % 16 == 0
    tiles_per_example = lp // tr
    in_specs, arrays = [], []
    for spec in ins:
        if spec[0] == "row":
            _, arr, width, cb = spec
            in_specs.append(pl.BlockSpec((tr, width), functools.partial(lambda i, cb: (i, cb), cb=cb)))
        elif spec[0] == "const":
            arr = spec[1]
            in_specs.append(pl.BlockSpec(arr.shape, lambda i: (0, 0)))
        else:
            arr = spec[1]
            in_specs.append(pl.BlockSpec((tr, arr.shape[1]), lambda i: (i % tiles_per_example, 0)))
        arrays.append(arr)
    n_in, n_out = len(ins), len(outs)

    def kern(*refs):
        res_outs, res_accs = body(*[r[...] for r in refs[:n_in]])
        for ref, val in zip(refs[n_in:n_in + n_out], res_outs, strict=True):
            ref[...] = val.astype(ref.dtype)
        acc_refs = refs[n_in + n_out:]
        if acc_refs:
            @pl.when(pl.program_id(0) == 0)
            def _():
                for ref in acc_refs:
                    ref[...] = jnp.zeros_like(ref)

            for ref, val in zip(acc_refs, res_accs, strict=True):
                ref[...] += val.reshape(tr // _SUBLANES, _SUBLANES, val.shape[-1]).sum(axis=0)

    out_shape = ([jax.ShapeDtypeStruct((rows, w), dt) for w, dt in outs]
                 + [jax.ShapeDtypeStruct((_SUBLANES, w), F32) for w in accs])
    out_specs = ([pl.BlockSpec((tr, w), lambda i: (i, 0)) for w, _ in outs]
                 + [pl.BlockSpec((_SUBLANES, w), lambda i: (0, 0)) for w in accs])
    res = pl.pallas_call(
        kern, name=name, out_shape=out_shape, grid=(rows // tr,),
        in_specs=in_specs, out_specs=out_specs, compiler_params=_params(),
    )(*arrays)
    return res[:n_out], list(res[n_out:])


def _assemble(name, x, head_rows, lp):
    batch, seq, d = x.shape
    tc = 256

    def body(x_ref, m_ref, o_ref):
        o_ref[0:N_META, :] = m_ref[...]
        o_ref[N_META:N_META + seq, :] = x_ref[0]
        if lp > N_META + seq:
            o_ref[N_META + seq:, :] = jnp.zeros((lp - N_META - seq, tc), F32)

    return pl.pallas_call(
        body, name=name,
        out_shape=jax.ShapeDtypeStruct((batch * lp, d), F32),
        grid=(batch, d // tc),
        in_specs=[pl.BlockSpec((1, seq, tc), lambda b, j: (b, 0, j)),
                  pl.BlockSpec((N_META, tc), lambda b, j: (0, j))],
        out_specs=pl.BlockSpec((lp, tc), lambda b, j: (b, j)),
        compiler_params=_params(),
    )(x, head_rows)


def _meta_grad(dh0, batch, lp):
    d = dh0.shape[1]

    def body(g_ref, o_ref):
        @pl.when(pl.program_id(0) == 0)
        def _():
            o_ref[...] = jnp.zeros_like(o_ref)

        o_ref[...] += g_ref[...]

    return pl.pallas_call(
        body, name="meta_grad",
        out_shape=jax.ShapeDtypeStruct((N_META, d), F32),
        grid=(batch,),
        in_specs=[pl.BlockSpec((N_META, d), lambda b: (b * (lp // N_META), 0))],
        out_specs=pl.BlockSpec((N_META, d), lambda b: (0, 0)),
        compiler_params=_params(),
    )(dh0)


def _segment_masks():
    t = lax.broadcasted_iota(jnp.int32, (HG_BLOCK, HG_BLOCK), 0)
    s = lax.broadcasted_iota(jnp.int32, (HG_BLOCK, HG_BLOCK), 1)
    same = lax.shift_right_logical(t, 4) == lax.shift_right_logical(s, 4)
    lower = same & (s <= t)
    upper = same & (s >= t)
    first_half = same & ((s & 15) <= 7)
    return same, lower, upper, first_half


def _hgrn_gates(hq, hf, lb):
    sq = _sigmoid(hq)
    q = hq * sq
    sf = _sigmoid(hf)
    f = lb + (1.0 - lb) * sf
    return q, sq, sf, f


def _hgrn_decays(g, same, lower, first_half):
    b = _exact_dot(lower, g)
    b_last = _exact_dot(same, g)
    b_ref = _exact_dot(first_half, g)
    return b, b_last, b_ref


def _hgrn_fwd(p, lb, gh, *, batch, lp, ride=None):
    rows = batch * lp
    nb = lp // HG_BLOCK
    n_chunks = HG_BLOCK // HG_CHUNK

    def body(hq_ref, hf_ref, hi_ref, hg_ref, lb_ref, gh_ref, *rest):
        if ride is not None:
            rest, exchange = ride.split(rest, 3)
            ride.run((batch, nb), exchange)
        o_ref, z_ref, st_ref, s_scr = rest

        @pl.when(pl.program_id(1) == 0)
        def _():
            s_scr[...] = jnp.zeros_like(s_scr)

        same, lower, _, first_half = _segment_masks()
        v = hi_ref[...]
        q, _, _, f = _hgrn_gates(hq_ref[...], hf_ref[...], lb_ref[...])
        k = 1.0 - f
        b, b_last, b_ref = _hgrn_decays(jnp.log(f), same, lower, first_half)
        qt = _mx(q * jnp.exp(b))
        kh = _mx(k * jnp.exp(b_last - b))
        vm = _mx(v)
        el = jnp.exp(b_last)
        qc = _mx(q * jnp.exp(b - b_ref))
        kc = _mx(k * jnp.exp(b_ref - b))

        def intra(qc_h, kc_h, v_h):
            a = jnp.where(lower, _dot_nt(qc_h, kc_h), 0.0)
            return _dot(_mx(a), v_h)

        o_intra = _heads(intra, qc, kc, vm)

        states = [s_scr[h] for h in range(HEADS)]
        o_inter = [[None] * HEADS for _ in range(n_chunks)]
        for c in range(n_chunks):
            rs = slice(c * HG_CHUNK, (c + 1) * HG_CHUNK)
            for h in range(HEADS):
                cs = slice(h * HEAD_DIM, (h + 1) * HEAD_DIM)
                st_m = _mx(states[h])
                st_ref[c, h] = st_m
                o_inter[c][h] = _dot_nt(qt[rs, cs], st_m)
                states[h] = states[h] * el[c * HG_CHUNK:c * HG_CHUNK + 1, cs] + _dot_tn(vm[rs, cs], kh[rs, cs])
        for h in range(HEADS):
            s_scr[h] = states[h]

        o = o_intra + jnp.concatenate([jnp.concatenate(row, axis=1) for row in o_inter], axis=0)
        o_ref[...] = o
        hg = hg_ref[...]
        n = _heads(lambda o_h: o_h * _rms_scale(o_h), o) * gh_ref[...]
        z_ref[...] = (n * hg * _sigmoid(hg)).astype(z_ref.dtype)

    blk = lambda cb: pl.BlockSpec((HG_BLOCK, D_MODEL), functools.partial(lambda b, j, cb: (b * nb + j, cb), cb=cb))
    row_out = pl.BlockSpec((HG_BLOCK, D_MODEL), lambda b, j: (b * nb + j, 0))
    const = pl.BlockSpec((1, D_MODEL), lambda b, j: (0, 0))
    extra = ride if ride is not None else _NoRide
    return pl.pallas_call(
        body, name="hgrn_fwd",
        out_shape=[jax.ShapeDtypeStruct((rows, D_MODEL), F32),
                   jax.ShapeDtypeStruct((rows, D_MODEL), _MXU_DTYPE),
                   jax.ShapeDtypeStruct((rows // HG_CHUNK, HEADS, HEAD_DIM, HEAD_DIM), _MXU_DTYPE)] + extra.out_shape,
        grid=(batch, nb),
        in_specs=[blk(CB_HQ), blk(CB_HF), blk(CB_HI), blk(CB_HG), const, const] + extra.in_specs,
        out_specs=[row_out, row_out,
                   pl.BlockSpec((n_chunks, HEADS, HEAD_DIM, HEAD_DIM), lambda b, j: (b * nb + j, 0, 0, 0))]
        + extra.out_specs,
        scratch_shapes=[pltpu.VMEM((HEADS, HEAD_DIM, HEAD_DIM), F32)] + extra.scratch,
        compiler_params=_params(),
    )(p, p, p, p, lb, gh, *extra.args)


def _hgrn_bwd(p, o, dz, states, lb, gh, *, batch, lp, ride=None):
    rows = batch * lp
    nb = lp // HG_BLOCK
    n_chunks = HG_BLOCK // HG_CHUNK

    def body(hq_ref, hf_ref, hi_ref, hg_ref, o_ref, dz_ref, st_ref, lb_ref, gh_ref, *rest):
        if ride is not None:
            rest, exchange = ride.split(rest, 3)
            ride.run((batch, nb), exchange)
        dp_ref, dlb_ref, dgh_ref, ds_scr = rest
        first = (pl.program_id(0) == 0) & (pl.program_id(1) == 0)

        @pl.when(first)
        def _():
            dlb_ref[...] = jnp.zeros_like(dlb_ref)
            dgh_ref[...] = jnp.zeros_like(dgh_ref)

        @pl.when(pl.program_id(1) == 0)
        def _():
            ds_scr[...] = jnp.zeros_like(ds_scr)

        same, lower, upper, first_half = _segment_masks()
        lbv = lb_ref[...]
        hq, hf, v, hg = hq_ref[...], hf_ref[...], hi_ref[...], hg_ref[...]
        q, sq, sf, f = _hgrn_gates(hq, hf, lbv)
        k = 1.0 - f
        b, b_last, b_ref = _hgrn_decays(jnp.log(f), same, lower, first_half)
        e_b = jnp.exp(b)
        e_kh = jnp.exp(b_last - b)
        e_qc = jnp.exp(b - b_ref)
        e_kc = jnp.exp(b_ref - b)
        qt, kh, qc, kc = q * e_b, k * e_kh, q * e_qc, k * e_kc

        o = o_ref[...]
        dz = dz_ref[...].astype(F32)
        ghv = gh_ref[...]
        sg = _sigmoid(hg)
        r = _heads(lambda o_h: jnp.broadcast_to(_rms_scale(o_h), o_h.shape), o)
        oh = o * r
        dn = dz * hg * sg
        dhg = dz * oh * ghv * _silu_grad(hg, sg)
        w = dn * ghv
        do = r * (w - oh * _heads(lambda t: jnp.broadcast_to(jnp.mean(t, axis=-1, keepdims=True), t.shape), oh * w))
        dgh_ref[...] += (dn * oh).reshape(HG_BLOCK // _SUBLANES, _SUBLANES, D_MODEL).sum(axis=0)

        qt_m, kh_m, v_m, do_m = _mx(qt), _mx(kh), _mx(v), _mx(do)
        el_all = jnp.exp(b_last)

        def intra(qc_h, kc_h, v_h, do_h):
            a = _mx(jnp.where(lower, _dot_nt(qc_h, kc_h), 0.0))
            da = _mx(jnp.where(lower, _dot_nt(do_h, v_h), 0.0))
            return _dot(da, kc_h), _dot_tn(da, qc_h), _dot_tn(a, do_h)

        dqc, dkc, dv_intra = _heads(intra, _mx(qc), _mx(kc), v_m, do_m)

        d_states = [ds_scr[h] for h in range(HEADS)]
        grid_of = lambda: [[None] * HEADS for _ in range(n_chunks)]
        dkh_p, dv_p, dbl_p, dqt_p = grid_of(), grid_of(), grid_of(), grid_of()
        for c in reversed(range(n_chunks)):
            rs = slice(c * HG_CHUNK, (c + 1) * HG_CHUNK)
            for h in range(HEADS):
                cs = slice(h * HEAD_DIM, (h + 1) * HEAD_DIM)
                st = st_ref[c, h]
                ds_t = d_states[h]
                ds_m = _mx(ds_t)
                el = el_all[c * HG_CHUNK:c * HG_CHUNK + 1, cs]
                dkh_p[c][h] = _dot(v_m[rs, cs], ds_m)
                dv_p[c][h] = _dot_nt(kh_m[rs, cs], ds_m)
                dbl = jnp.sum(ds_t * st.astype(F32), axis=0, keepdims=True) * el
                dbl_p[c][h] = jnp.broadcast_to(dbl, (HG_CHUNK, HEAD_DIM))
                dqt_p[c][h] = _dot(do_m[rs, cs], st)
                d_states[h] = ds_t * el + _dot_tn(do_m[rs, cs], qt_m[rs, cs])
        for h in range(HEADS):
            ds_scr[h] = d_states[h]
        whole = lambda parts: jnp.concatenate([jnp.concatenate(row, axis=1) for row in parts], axis=0)

        dqt, dkh = whole(dqt_p), whole(dkh_p)
        dq = dqt * e_b + dqc * e_qc
        dk = dkh * e_kh + dkc * e_kc
        t_kh = dkh * kh
        db_rows = dqt * qt + dqc * qc - dkc * kc - t_kh
        dg = _exact_dot(upper, db_rows) + _exact_dot(same, t_kh) + whole(dbl_p)
        df = dg / f - dk
        dhf = df * (1.0 - lbv) * sf * (1.0 - sf)
        dlb_ref[...] += (df * (1.0 - sf)).reshape(HG_BLOCK // _SUBLANES, _SUBLANES, D_MODEL).sum(axis=0)
        dhq = dq * _silu_grad(hq, sq)
        dp_ref[...] = jnp.concatenate([dhq, dhf, dv_intra + whole(dv_p), dhg], axis=1).astype(dp_ref.dtype)

    rev = lambda b, j: b * nb + (nb - 1 - j)
    blk = lambda cb: pl.BlockSpec((HG_BLOCK, D_MODEL), functools.partial(lambda b, j, cb: (rev(b, j), cb), cb=cb))
    row = pl.BlockSpec((HG_BLOCK, D_MODEL), lambda b, j: (rev(b, j), 0))
    const = pl.BlockSpec((1, D_MODEL), lambda b, j: (0, 0))
    acc = pl.BlockSpec((_SUBLANES, D_MODEL), lambda b, j: (0, 0))
    extra = ride if ride is not None else _NoRide
    dp, dlb, dgh, *exchanged = pl.pallas_call(
        body, name="hgrn_bwd",
        out_shape=[jax.ShapeDtypeStruct((rows, 4 * D_MODEL), _MXU_DTYPE),
                   jax.ShapeDtypeStruct((_SUBLANES, D_MODEL), F32),
                   jax.ShapeDtypeStruct((_SUBLANES, D_MODEL), F32)] + extra.out_shape,
        grid=(batch, nb),
        in_specs=[blk(CB_HQ), blk(CB_HF), blk(CB_HI), blk(CB_HG), row, row,
                  pl.BlockSpec((n_chunks, HEADS, HEAD_DIM, HEAD_DIM), lambda b, j: (rev(b, j), 0, 0, 0)),
                  const, const] + extra.in_specs,
        out_specs=[pl.BlockSpec((HG_BLOCK, 4 * D_MODEL), lambda b, j: (rev(b, j), 0)), acc, acc] + extra.out_specs,
        scratch_shapes=[pltpu.VMEM((HEADS, HEAD_DIM, HEAD_DIM), F32)] + extra.scratch,
        compiler_params=_params(),
    )(p, p, p, p, o, dz, states, lb, gh, *extra.args)
    return (dp, dlb, dgh, *exchanged)


QK_DIM = 2 * HEAD_DIM
ATTN_TQ_FWD = 512
ATTN_TQ_BWD = 256
ATTN_KEY_CHUNK = 1024


def _query_tiles(lp, tq):
    return [(r0, min(tq, lp - r0)) for r0 in range(0, lp, tq)]


def _attn_fwd(q_cat, kv, kp, *, batch, lp):
    rows = batch * lp

    def body(q_ref, kn_ref, kp_ref, v_ref, o_ref, lse_ref):
        k_cat = jnp.concatenate([kn_ref[...], kp_ref[...]], axis=1)
        for r0, tq in _query_tiles(lp, ATTN_TQ_FWD):
            q_t = q_ref[r0:r0 + tq, :]
            i = lax.broadcasted_iota(jnp.int32, (tq, tq), 0)
            j = lax.broadcasted_iota(jnp.int32, (tq, tq), 1)
            s_diag = jnp.where(j <= i, _dot_nt(q_t, k_cat[r0:r0 + tq]) * ATTN_SCALE, NEG_BIG)
            m = jnp.max(s_diag, axis=1, keepdims=True)
            if r0:
                s_past = _dot_nt(q_t, k_cat[0:r0]) * ATTN_SCALE
                m = jnp.maximum(m, jnp.max(s_past, axis=1, keepdims=True))
            p_diag = jnp.exp(s_diag - m)
            l = jnp.sum(p_diag, axis=1, keepdims=True)
            acc = _dot(_mx(p_diag), v_ref[r0:r0 + tq, :])
            if r0:
                p_past = jnp.exp(s_past - m)
                l = l + jnp.sum(p_past, axis=1, keepdims=True)
                acc = acc + _dot(_mx(p_past), v_ref[0:r0, :])
            o_ref[r0:r0 + tq, :] = (acc / l).astype(o_ref.dtype)
            lse_ref[r0:r0 + tq, :] = jnp.broadcast_to(m + jnp.log(l), (tq, HEAD_DIM))

    head_blk = pl.BlockSpec((lp, HEAD_DIM), lambda b, h: (b, h))
    return pl.pallas_call(
        body, name="attn_fwd",
        out_shape=[jax.ShapeDtypeStruct((rows, D_MODEL), _MXU_DTYPE),
                   jax.ShapeDtypeStruct((rows, D_MODEL), F32)],
        grid=(batch, HEADS),
        in_specs=[pl.BlockSpec((lp, QK_DIM), lambda b, h: (b, h)), head_blk,
                  pl.BlockSpec((lp, HEAD_DIM), lambda b, h: (b, 0)),
                  pl.BlockSpec((lp, HEAD_DIM), lambda b, h: (b, HEADS + h))],
        out_specs=[head_blk, head_blk],
        compiler_params=_params(),
    )(q_cat, kv, kp, kv)


def _attn_bwd(q_cat, kv, kp, do, o, lse, c_tab, s_tab, *, batch, lp, ride=None):
    rows = batch * lp

    def body(q_ref, kn_ref, kp_ref, v_ref, do_ref, o_ref, lse_ref, c_ref, s_ref, *rest):
        if ride is not None:
            rest, exchange = ride.split(rest, 6)
            ride.run((batch, HEADS), exchange)
        dqn_ref, dqc_ref, dqs_ref, dkn_ref, dkp_ref, dv_ref, dk_acc, dv_acc = rest
        dk_acc[...] = jnp.zeros_like(dk_acc)
        dv_acc[...] = jnp.zeros_like(dv_acc)
        k_cat = jnp.concatenate([kn_ref[...], kp_ref[...]], axis=1)
        k_t = k_cat.T
        lane = lax.broadcasted_iota(jnp.int32, (_SUBLANES, HEAD_DIM), 1)
        lse_row = _exact_dot_nt(lane == 0, lse_ref[...])
        delta = _exact_dot_nt(lane >= 0, do_ref[...].astype(F32) * o_ref[...].astype(F32))
        for r0, tq in _query_tiles(lp, ATTN_TQ_BWD):
            cols = slice(r0, r0 + tq)
            q_t_, do_t_ = q_ref[cols, :], do_ref[cols, :]
            lse_t, delta_t = lse_row[0:1, cols], delta[0:1, cols]
            chunks = [(c0, min(ATTN_KEY_CHUNK, r0 - c0), False) for c0 in range(0, r0, ATTN_KEY_CHUNK)] + [(r0, tq, True)]
            dq_t = jnp.zeros((QK_DIM, tq), F32)
            for c0, n, diagonal in chunks:
                keys = slice(c0, c0 + n)
                s = _dot_nt(k_cat[keys], q_t_) * ATTN_SCALE
                if diagonal:
                    jk = lax.broadcasted_iota(jnp.int32, (n, tq), 0)
                    iq = lax.broadcasted_iota(jnp.int32, (n, tq), 1)
                    s = jnp.where(jk <= iq, s, NEG_BIG)
                pexp = jnp.exp(s - lse_t)
                dp = _dot_nt(v_ref[keys, :], do_t_)
                ds = _mx(pexp * (dp - delta_t) * ATTN_SCALE)
                dk_acc[keys, :] += _dot(ds, q_t_)
                dv_acc[keys, :] += _dot(_mx(pexp), do_t_)
                dq_t = dq_t + _dot(k_t[:, keys], ds)
            dq = dq_t.T
            d_rope = dq[:, HEAD_DIM:]
            dqn_ref[cols, :] = dq[:, :HEAD_DIM].astype(dqn_ref.dtype)
            dqc_ref[cols, :] = (d_rope * c_ref[cols, :]).astype(dqc_ref.dtype)
            dqs_ref[cols, :] = (d_rope * s_ref[cols, :]).astype(dqs_ref.dtype)

        dkn_ref[...] = dk_acc[:, 0:HEAD_DIM].astype(dkn_ref.dtype)
        dv_ref[...] = dv_acc[...].astype(dv_ref.dtype)

        @pl.when(pl.program_id(1) == 0)
        def _():
            dkp_ref[...] = jnp.zeros_like(dkp_ref)

        dkp_ref[...] += dk_acc[:, HEAD_DIM:]

    head_blk = pl.BlockSpec((lp, HEAD_DIM), lambda b, h: (b, h))
    cat_blk = pl.BlockSpec((lp, QK_DIM), lambda b, h: (b, h))
    shared_blk = pl.BlockSpec((lp, HEAD_DIM), lambda b, h: (b, 0))
    table_blk = pl.BlockSpec((lp, HEAD_DIM), lambda b, h: (0, 0))
    extra = ride if ride is not None else _NoRide
    return pl.pallas_call(
        body, name="attn_bwd",
        out_shape=[jax.ShapeDtypeStruct((rows, D_MODEL), _MXU_DTYPE)] * 3 + [
                   jax.ShapeDtypeStruct((rows, D_MODEL), _MXU_DTYPE),
                   jax.ShapeDtypeStruct((rows, HEAD_DIM), F32),
                   jax.ShapeDtypeStruct((rows, D_MODEL), _MXU_DTYPE)] + extra.out_shape,
        grid=(batch, HEADS),
        in_specs=[cat_blk, head_blk, shared_blk, pl.BlockSpec((lp, HEAD_DIM), lambda b, h: (b, HEADS + h)),
                  head_blk, head_blk, head_blk, table_blk, table_blk] + extra.in_specs,
        out_specs=[head_blk, head_blk, head_blk, head_blk, shared_blk, head_blk] + extra.out_specs,
        scratch_shapes=[pltpu.VMEM((lp, QK_DIM), F32), pltpu.VMEM((lp, HEAD_DIM), F32)] + extra.scratch,
        compiler_params=_params(),
    )(q_cat, kv, kp, kv, do, o, lse, c_tab, s_tab, *extra.args)


def _all_gather(name, blocks):
    n = len(blocks)

    def body(*refs):
        x_refs, out_refs, (send_sems, recv_sems, local_sems) = refs[:n], refs[n:2 * n], refs[2 * n:]
        x, y, c = lax.axis_index("x"), lax.axis_index("y"), lax.axis_index("c")
        me, sibling = (x, y, c), (x, y, 1 - c)
        chips = [(1 - x, y), (x, 1 - y), (1 - x, 1 - y)]

        def slot(i, px, py, pc):
            return out_refs[i].at[4 * px + 2 * py + pc]

        def copy(i, k, blk, to, src=None):
            return pltpu.make_async_remote_copy(
                src_ref=slot(i, *blk) if src is None else src, dst_ref=slot(i, *blk),
                send_sem=send_sems.at[i, k], recv_sem=recv_sems.at[i, k],
                device_id=to, device_id_type=pl.DeviceIdType.MESH)

        mine = [pltpu.make_async_copy(x_refs[i], slot(i, *me), local_sems.at[i]) for i in range(n)]
        first = [copy(i, 0, me, sibling, src=x_refs[i]) for i in range(n)]
        first += [copy(i, 1 + j, me, (*chip, c), src=x_refs[i]) for i in range(n) for j, chip in enumerate(chips)]
        for cp in mine + first:
            cp.start()
        passed = []
        for i in range(n):
            for j, chip in enumerate(chips):
                copy(i, 1 + j, (*chip, c), me).wait_recv()
                passed.append(copy(i, 4 + j, (*chip, c), sibling))
                passed[-1].start()
        for i in range(n):
            copy(i, 0, sibling, me).wait_recv()
            for j, chip in enumerate(chips):
                copy(i, 4 + j, (*chip, 1 - c), me).wait_recv()
        for cp in first + passed:
            cp.wait_send()
        for cp in mine:
            cp.wait()

    return pl.pallas_call(
        body, name=name,
        out_shape=[jax.ShapeDtypeStruct((N_DEV, *b.shape), b.dtype) for b in blocks],
        in_specs=[pl.BlockSpec(memory_space=pl.ANY)] * n,
        out_specs=[pl.BlockSpec(memory_space=pl.ANY)] * n,
        scratch_shapes=[pltpu.SemaphoreType.DMA((n, 7)), pltpu.SemaphoreType.DMA((n, 7)),
                        pltpu.SemaphoreType.DMA((n,))],
    )(*blocks)


def _adamw_math(w, g, m, v):
    nm = ADAM_B1 * m + (1.0 - ADAM_B1) * g
    nv = ADAM_B2 * v + (1.0 - ADAM_B2) * (g * g)
    m_hat = nm / (1.0 - ADAM_B1 ** ADAM_STEP)
    v_hat = nv / (1.0 - ADAM_B2 ** ADAM_STEP)
    return -ADAM_LR * (m_hat / (jnp.sqrt(v_hat) + ADAM_EPS) + ADAM_WD * w), nm, nv


def _sum_adamw(name, parts, w, m, v):
    rows, cols = w.shape
    tr = rows // 4 if rows % 64 == 0 and rows * cols > (1 << 16) else rows

    def body(p_ref, w_ref, m_ref, v_ref, g_ref, d_ref, nm_ref, nv_ref):
        g = p_ref[0].astype(F32)
        for dev in range(1, N_DEV):
            g = g + p_ref[dev].astype(F32)
        g_ref[...] = g
        d_ref[...], nm_ref[...], nv_ref[...] = _adamw_math(w_ref[...], g, m_ref[...], v_ref[...])

    spec = pl.BlockSpec((tr, cols), lambda i: (i, 0))
    return pl.pallas_call(
        body, name=name,
        out_shape=[jax.ShapeDtypeStruct((rows, cols), F32)] * 4,
        grid=(rows // tr,),
        in_specs=[pl.BlockSpec((N_DEV, tr, cols), lambda i: (0, i, 0))] + [spec] * 3, out_specs=[spec] * 4,
        compiler_params=_params(),
    )(parts, w, m, v)


def _finish_vectors(gathered, lb, params, loss_parts):
    names = list(params)
    n = len(names)

    def body(*refs):
        g_refs, lb_ref, loss_ref = refs[:n], refs[n], refs[n + 1]
        wmv_refs = refs[n + 2:4 * n + 2]
        out_refs, loss_out = refs[4 * n + 2:-1], refs[-1]
        sq = loss_ref[0]
        for dev in range(1, N_DEV):
            sq = sq + loss_ref[dev]
        sq = jnp.sum(jnp.sum(sq, axis=0, keepdims=True), axis=1, keepdims=True)
        loss_out[...] = sq * (0.5 / D_MODEL)
        me = 4 * lax.axis_index("x") + 2 * lax.axis_index("y") + lax.axis_index("c")
        for i, name in enumerate(names):
            g_ref = g_refs[i]
            w_ref, m_ref, v_ref = wmv_refs[3 * i:3 * i + 3]
            if name == "meta_tokens":
                width = w_ref.shape[1]
                mine = pl.ds(pl.multiple_of(me * width, width), width)
                g = g_ref[0, :, mine]
                for dev in range(1, N_DEV):
                    g = g + g_ref[dev, :, mine]
            else:
                g = g_ref[0]
                for dev in range(1, N_DEV):
                    g = g + g_ref[dev]
                g = jnp.sum(g, axis=0, keepdims=True)
                if name == "hg_norm_g":
                    g = functools.reduce(jnp.add, [g[:, h * HEAD_DIM:(h + 1) * HEAD_DIM] for h in range(HEADS)])
                if name == "lb_logits":
                    lbv = lb_ref[...]
                    g = g * lbv * (1.0 - lbv)
                    g = jnp.concatenate([g, -g], axis=0)
            outs = (g, *_adamw_math(w_ref[...], g, m_ref[...], v_ref[...]))
            for ref, val in zip(out_refs[4 * i:4 * i + 4], outs, strict=True):
                ref[...] = val

    args = [gathered[k] for k in names] + [lb, loss_parts] + [t for k in names for t in params[k]]
    res = pl.pallas_call(
        body, name="finish_vectors",
        out_shape=[jax.ShapeDtypeStruct(params[k][0].shape, F32) for k in names for _ in range(4)]
        + [jax.ShapeDtypeStruct((1, 1), F32)],
        compiler_params=_params(),
    )(*args)
    return {k: res[4 * i:4 * i + 4] for i, k in enumerate(names)}, res[-1].reshape(())


def _swap_halves(t):
    half = t.shape[-1] // 2
    return jnp.concatenate([t[..., half:], t[..., :half]], axis=-1)


def _pad_last(t, width):
    return jnp.concatenate([t, jnp.zeros(t.shape[:-1] + (width - t.shape[-1],), t.dtype)], axis=-1)


def _rope_tables(lp):
    pos = jnp.arange(lp, dtype=F32)
    inv_freq = 1.0 / (ROPE_THETA ** (jnp.arange(0, ROPE_DIM, 2, dtype=F32) / ROPE_DIM))
    ang = pos[:, None] * inv_freq[None, :]
    cos, sin = jnp.cos(ang), jnp.sin(ang)
    c128 = _pad_last(jnp.concatenate([cos, cos], axis=1), HEAD_DIM)
    s128 = _pad_last(jnp.concatenate([-sin, sin], axis=1), HEAD_DIM)
    return c128, s128


def _forward_backward(x, target, meta, w, small, *, lp, comm=None):
    batch, seq, d = x.shape
    rows = batch * lp
    tr = 272 if lp % 272 == 0 else 128
    tm = lp // 2
    bf = _MXU_DTYPE
    rw = functools.partial(_rowwise, rows=rows, tr=tr, lp=lp)

    c128, s128 = _rope_tables(lp)
    t_idx = jnp.arange(lp)
    real = jnp.broadcast_to(((t_idx >= N_META) & (t_idx < N_META + seq)).astype(F32)[:, None], (lp, _LANES))

    lb_logits = small["lb_logits"]
    lb = jax.nn.softmax(lb_logits, axis=0)[0:1]
    gh = jnp.tile(small["hg_norm_g"], (1, HEADS))

    h0 = _assemble("assemble_x", x, meta, lp)
    tgt = _assemble("assemble_target", target, jnp.zeros_like(meta), lp)

    (u1,), _ = rw("norm_mix_pre", lambda h, g: ([h * _rms_scale(h) * g], []),
                  ins=[("row", h0, d, 0), ("const", small["mix_pre_g"])], outs=[(d, bf)])
    p = _matmul("proj_in", u1, w["w_in"], out_dtype=F32, tm=tm, tn=1024, tk=1024)

    if comm is None:
        o_hg, z_a, states = _hgrn_fwd(p, lb, gh, batch=batch, lp=lp)
    else:
        o_hg, z_a, states, *gathered = _hgrn_fwd(p, lb, gh, batch=batch, lp=lp, ride=_Ride(comm.rest_payloads, True))
        w = {**w, **comm.rest_weights(gathered)}
    received = []
    scatter = lambda names: _Ride(comm.grad_parts(names, grads), False) if comm is not None else None
    y_a = _matmul("proj_hg_o", z_a, w["w_hg_o"], out_dtype=F32, tm=tm, tn=1024, tk=1024)

    def mla_pre(pc, gq, gkv, ct, st):
        cq, ckv = pc[:, 0:Q_LORA], pc[:, Q_LORA:Q_LORA + KV_LORA]
        kpe, kpe_sw = pc[:, 512:640], pc[:, 640:768]
        return [cq * _rms_scale(cq) * gq, ckv * _rms_scale(ckv) * gkv, kpe * ct + kpe_sw * st], []

    (cqn, ckvn, kp), _ = rw("mla_pre", mla_pre,
                            ins=[("row", p, 1024, CB_C), ("const", small["q_a_norm_g"]),
                                 ("const", small["kv_a_norm_g"]), ("pos", c128), ("pos", s128)],
                            outs=[(Q_LORA, bf), (KV_LORA, bf), (HEAD_DIM, bf)])
    q_cat = _proj_q_rope(cqn, w["w_q"], c128, s128, tm=tm, lp=lp)
    kv = _matmul("proj_kv_b", ckvn, w["w_kv"], out_dtype=bf, tm=tm, tn=1024, tk=KV_LORA)
    o_at, lse = _attn_fwd(q_cat, kv, kp, batch=batch, lp=lp)
    te, te_small = tm, lp // 4

    def merge(yb, pa, pb, ya, bg):
        ga, gb = _sigmoid(pa + bg[:, :d]), _sigmoid(pb + bg[:, d:])
        return [yb, ga * ya + gb * yb], []

    (y_b, mix), _ = _matmul_segments(
        "proj_mla_o", [o_at], w["w_mla_o"], tm=te_small, tn=d, tk=1024,
        epilogue=_Epilogue(merge, rows=[(p, 1024, CB_GA), (p, 1024, CB_GB), y_a], consts=[small["b_gate"]],
                           outs=[(d, F32), (d, bf)], lp=lp))
    def post_mix(mx_, h, g2, g3):
        h1_ = h + mx_ * _rms_scale(mx_) * g2
        return [mx_, h1_, h1_ * _rms_scale(h1_) * g3], []

    (mixed, h1, u2), _ = _matmul_segments(
        "proj_out", [mix], w["w_out"], tm=te, tn=d, tk=1024,
        epilogue=_Epilogue(post_mix, rows=[h0], consts=[small["mix_post_g"], small["ffn_pre_g"]],
                           outs=[(d, F32), (d, F32), (d, bf)], lp=lp))
    act, act_dgate, act_dup = _ffn_in_swiglu(u2, w["w_ffn_in"], tm=tm, tn=1408)

    def post_ffn(fo_, h1_, t_, g4, mask):
        r = _rms_scale(fo_)
        h2 = h1_ + fo_ * r * g4
        err = (h2 - t_) * mask[:, 0:1]
        dh2 = err * (1.0 / d)
        dfo, dg4 = _rms_bwd(fo_, g4, dh2)
        return [dh2, dfo], [err * err, dg4]

    (dh2, dfo), (loss_vec, dg_ffn_post) = _matmul_segments(
        "ffn_out", [act], w["w_ffn_out"], tm=te, tn=d, tk=1408,
        epilogue=_Epilogue(post_ffn, rows=[h1, tgt], consts=[small["ffn_post_g"]], pos=[real],
                           outs=[(d, F32), (d, bf)], accs=[d, d], lp=lp))
    loss = (0.5 / d) * jnp.sum(loss_vec)

    grads = {}
    dw_dt = F32 if comm is None else _WIRE_DTYPE
    dgt, dup = _d_ffn_out_swiglu(dfo, w["w_ffn_out"], act_dgate, act_dup, tm=tm, tn=1408)
    grads["w_ffn_out"] = _matmul_tn("dw_ffn_out", act, dfo, tk=1408, tn=1024, tr=tm, out_dtype=dw_dt)
    grads["w_ffn_in"] = jnp.concatenate([_matmul_tn("dw_ffn_in_gate", u2, dgt, tk=1024, tn=FFN_HIDDEN, tr=tm),
                                         _matmul_tn("dw_ffn_in_up", u2, dup, tk=1024, tn=FFN_HIDDEN, tr=tm)], axis=1)

    def post_mix_bwd(du2_, h1_, dh2_, mx_, g3, g2):
        dx, dg3 = _rms_bwd(h1_, g3, du2_)
        dh1_ = dh2_ + dx
        dmx, dg2 = _rms_bwd(mx_, g2, dh1_)
        return [dh1_, dmx], [dg3, dg2]

    (dh1, dmixed), (dg_ffn_pre, dg_mix_post) = _matmul_segments(
        "d_ffn_in", [dgt, dup], w["w_ffn_in"], tm=te_small, tn=d, tk=1408, b_transposed=True,
        epilogue=_Epilogue(post_mix_bwd, rows=[h1, dh2, mixed], consts=[small["ffn_pre_g"], small["mix_post_g"]],
                           outs=[(d, F32), (d, bf)], accs=[d, d], lp=lp))
    grads["w_out"] = _matmul_tn("dw_out", mix, dmixed, tk=1024, tn=1024, tr=tm, out_dtype=dw_dt)

    def merge_bwd(dm, pa, pb, ya, yb, bg):
        ga, gb = _sigmoid(pa + bg[:, :d]), _sigmoid(pb + bg[:, d:])
        dpg = jnp.concatenate([dm * ya * ga * (1.0 - ga), dm * yb * gb * (1.0 - gb)], axis=1)
        return [dpg, dm * ga, dm * gb], [dpg]

    (dpg, dya, dyb), (db_gate,) = _matmul_segments(
        "d_proj_out", [dmixed], w["w_out"], tm=te_small, tn=d, tk=1024, b_transposed=True,
        epilogue=_Epilogue(merge_bwd, rows=[(p, 1024, CB_GA), (p, 1024, CB_GB), y_a, y_b], consts=[small["b_gate"]],
                           outs=[(2 * d, bf), (d, bf), (d, bf)], accs=[2 * d], lp=lp))
    dz_a = _matmul("d_proj_hg_o", dya, w["w_hg_o"], out_dtype=F32, tm=tm, tn=1024, tk=1024, b_transposed=True)
    grads["w_hg_o"] = _matmul_tn("dw_hg_o", z_a, dya, tk=1024, tn=1024, tr=tm, out_dtype=dw_dt)
    do_at = _matmul("d_proj_mla_o", dyb, w["w_mla_o"], out_dtype=bf, tm=tm, tn=1024, tk=1024, b_transposed=True)
    grads["w_mla_o"] = _matmul_tn("dw_mla_o", o_at, dyb, tk=1024, tn=1024, tr=tm, out_dtype=dw_dt)

    dph, dlb, dgh, *got = _hgrn_bwd(p, o_hg, dz_a, states, lb, gh, batch=batch, lp=lp,
                                    ride=scatter(_GRAD_GROUPS[0]))
    received.append(got)

    res = _attn_bwd(q_cat, kv, kp, do_at, o_at, lse, c128, s128, batch=batch, lp=lp, ride=scatter(_GRAD_GROUPS[1]))
    dq_parts, (dkn, dkp, dvv) = list(res[:3]), res[3:6]
    received.append(list(res[6:]))
    dcqn = _matmul_segments("d_proj_q_b", dq_parts, w["w_q"], tm=tm, tn=Q_LORA, tk=1024, b_transposed=True)
    grads["w_q"] = jnp.concatenate([_matmul_tn(f"dw_q_b_{i}", cqn, part, tk=Q_LORA, tn=1024, tr=tm)
                                    for i, part in enumerate(dq_parts)], axis=1)
    dckvn = _matmul_segments("d_proj_kv_b", [dkn, dvv], w["w_kv"], tm=tm, tn=KV_LORA, tk=1024, b_transposed=True)
    grads["w_k"] = _matmul_tn("dw_k_b", ckvn, dkn, tk=KV_LORA, tn=1024, tr=tm)
    grads["w_v"] = _matmul_tn("dw_v_b", ckvn, dvv, tk=KV_LORA, tn=1024, tr=tm)

    def mla_pre_bwd(pc, dq_, dkv_, dkp_, gq, gkv, ct, st):
        cq, ckv = pc[:, 0:Q_LORA], pc[:, Q_LORA:Q_LORA + KV_LORA]
        dcq, dgq = _rms_bwd(cq, gq, dq_)
        dckv, dgkv = _rms_bwd(ckv, gkv, dkv_)
        dpc = jnp.concatenate([dcq, dckv, dkp_ * ct, dkp_ * st, jnp.zeros((pc.shape[0], 256), F32)], axis=1)
        return [dpc], [dgq, dgkv]

    (dpc,), (dg_q, dg_kv) = rw(
        "mla_pre_bwd", mla_pre_bwd,
        ins=[("row", p, 1024, CB_C), ("row", dcqn, Q_LORA, 0), ("row", dckvn, KV_LORA, 0), ("row", dkp, HEAD_DIM, 0),
             ("const", small["q_a_norm_g"]), ("const", small["kv_a_norm_g"]), ("pos", c128), ("pos", s128)],
        outs=[(1024, bf)], accs=[Q_LORA, KV_LORA])

    grads["w_in"] = (_matmul_tn("dw_in_h", u1, dph, tk=1024, tn=1024, tr=tm),
                     _matmul_tn("dw_in_c", u1, dpc, tk=1024, tn=1024, tr=tm),
                     _matmul_tn("dw_in_g", u1, dpg, tk=1024, tn=1024, tr=tm))
    def pre_bwd(du, h, dh, g1):
        dx, dg1 = _rms_bwd(h, g1, du)
        return [dh + dx], [dg1]

    (dh0,), (dg_mix_pre,), *got = _matmul_segments(
        "d_proj_in", [dph, dpc, dpg], w["w_in"], tm=te, tn=d, tk=1024, b_transposed=True,
        ride=scatter(_GRAD_GROUPS[2]),
        epilogue=_Epilogue(pre_bwd, rows=[h0, dh1], consts=[small["mix_pre_g"]], outs=[(d, F32)], accs=[d], lp=lp))
    if comm is not None:
        received.append(got)
    grad_x = dh0.reshape(batch, lp, d)[:, N_META:N_META + seq]
    partial = {"meta_tokens": _meta_grad(dh0, batch, lp), "lb_logits": dlb, "b_gate": db_gate, "hg_norm_g": dgh,
               "q_a_norm_g": dg_q, "kv_a_norm_g": dg_kv, "mix_pre_g": dg_mix_pre, "mix_post_g": dg_mix_post,
               "ffn_pre_g": dg_ffn_pre, "ffn_post_g": dg_ffn_post, "loss": loss_vec}
    return loss, grad_x, grads, partial, lb, received


_BIG = ["w_in", "w_hg_o", "w_q_b", "w_kv_b", "w_mla_o", "w_out", "w_ffn_in", "w_ffn_out"]
_COLUMN_SHARDED = {"w_in", "w_q_b", "w_kv_b", "w_ffn_in"}
_GRAD_GROUPS = [["w_ffn_in", "w_ffn_out"], ["w_out", "w_hg_o", "w_mla_o"], ["w_in", "w_q_b", "w_kv_b"]]
_SMALL = ["b_gate", "lb_logits", "hg_norm_g", "q_a_norm_g", "kv_a_norm_g", "mix_pre_g", "mix_post_g",
          "ffn_pre_g", "ffn_post_g"]


def _gathered_matrix(name, t):
    _, k, n = t.shape
    if name in _COLUMN_SHARDED:
        return t.transpose(1, 0, 2).reshape(k, N_DEV * n)
    return t.reshape(N_DEV * k, n)


def _scatter_layout(name, full):
    kk, nn = full.shape
    if name in _COLUMN_SHARDED:
        t = full.reshape(kk, N_DEV, nn // N_DEV).transpose(1, 0, 2)
    else:
        t = full.reshape(N_DEV, kk // N_DEV, nn)
    return t.astype(_WIRE_DTYPE)


def _model_w_in(wi):
    z = lambda *s: jnp.zeros(s, wi.dtype)
    kpe = wi[:, 4608:4672]
    c_blk = jnp.concatenate([wi[:, 4096:4608], kpe, z(1024, 64), _swap_halves(kpe), z(1024, 64), z(1024, 256)], axis=1)
    return {"w_in": jnp.concatenate([wi[:, :4096], c_blk, wi[:, 4672:]], axis=1).astype(_MXU_DTYPE)}


def _model_weights(full):
    return {**_model_w_in(full["w_in"]), **_model_rest(full)}


def _model_rest(full):
    wq3 = full["w_q_b"].reshape(Q_LORA, HEADS, HEAD_DIM + ROPE_DIM)
    pe = wq3[:, :, HEAD_DIM:]
    w_q = jnp.concatenate([wq3[:, :, :HEAD_DIM].reshape(Q_LORA, -1),
                           _pad_last(pe, HEAD_DIM).reshape(Q_LORA, -1),
                           _pad_last(_swap_halves(pe), HEAD_DIM).reshape(Q_LORA, -1)], axis=1)
    wkv3 = full["w_kv_b"].reshape(KV_LORA, HEADS, 2 * HEAD_DIM)
    w_k = wkv3[:, :, :HEAD_DIM].reshape(KV_LORA, -1)
    w_v = wkv3[:, :, HEAD_DIM:].reshape(KV_LORA, -1)
    w = {"w_q": w_q, "w_kv": jnp.concatenate([w_k, w_v], axis=1),
         "w_hg_o": full["w_hg_o"], "w_mla_o": full["w_mla_o"], "w_out": full["w_out"],
         "w_ffn_in": full["w_ffn_in"], "w_ffn_out": full["w_ffn_out"]}
    return {k: v.astype(_MXU_DTYPE) for k, v in w.items()}


def _reference_layout_grad(name, g):
    if name == "w_in":
        g_h, g_c, g_g = g["w_in"]
        d_kpe = g_c[:, 512:576] + _swap_halves(g_c[:, 640:704])
        return jnp.concatenate([g_h, g_c[:, :512], d_kpe, g_g], axis=1)
    if name == "w_q_b":
        gq = g["w_q"]
        d_pe = (gq[:, 1024:2048].reshape(Q_LORA, HEADS, HEAD_DIM)[:, :, :ROPE_DIM]
                + _swap_halves(gq[:, 2048:].reshape(Q_LORA, HEADS, HEAD_DIM)[:, :, :ROPE_DIM]))
        return jnp.concatenate([gq[:, :1024].reshape(Q_LORA, HEADS, HEAD_DIM), d_pe], axis=2).reshape(Q_LORA, -1)
    if name == "w_kv_b":
        return jnp.concatenate([g["w_k"].reshape(KV_LORA, HEADS, HEAD_DIM),
                                g["w_v"].reshape(KV_LORA, HEADS, HEAD_DIM)], axis=2).reshape(KV_LORA, -1)
    return g[name]


def _reference_layout_grads(g):
    return {n: _reference_layout_grad(n, g) for n in _BIG}


class _Comm:
    def __init__(self, shard):
        self.rest_payloads = [shard[n].astype(_WIRE_DTYPE) for n in _BIG[1:]]

    def rest_weights(self, gathered):
        return _model_rest({n: _gathered_matrix(n, t) for n, t in zip(_BIG[1:], gathered, strict=True)})

    def grad_parts(self, names, g):
        return [_scatter_layout(n, _reference_layout_grad(n, g)) for n in names]


def kernel(x, meta_tokens, w_in, b_gate, lb_logits, hg_norm_g, w_hg_o, q_a_norm_g, w_q_b, kv_a_norm_g, w_kv_b, w_mla_o, w_out, mix_pre_g, mix_post_g, ffn_pre_g, ffn_post_g, w_ffn_in, w_ffn_out, loss_target, m_meta_tokens, m_w_in, m_b_gate, m_lb_logits, m_hg_norm_g, m_w_hg_o, m_q_a_norm_g, m_w_q_b, m_kv_a_norm_g, m_w_kv_b, m_w_mla_o, m_w_out, m_mix_pre_g, m_mix_post_g, m_ffn_pre_g, m_ffn_post_g, m_w_ffn_in, m_w_ffn_out, v_meta_tokens, v_w_in, v_b_gate, v_lb_logits, v_hg_norm_g, v_w_hg_o, v_q_a_norm_g, v_w_q_b, v_kv_a_norm_g, v_w_kv_b, v_w_mla_o, v_w_out, v_mix_pre_g, v_mix_post_g, v_ffn_pre_g, v_ffn_post_g, v_w_ffn_in, v_w_ffn_out):
    args = dict(locals())
    batch, seq, d = x.shape
    lp = -(-(N_META + seq) // _LANES) * _LANES
    weight_names = ["meta_tokens", "w_in", "b_gate", "lb_logits", "hg_norm_g", "w_hg_o", "q_a_norm_g", "w_q_b",
                    "kv_a_norm_g", "w_kv_b", "w_mla_o", "w_out", "mix_pre_g", "mix_post_g", "ffn_pre_g",
                    "ffn_post_g", "w_ffn_in", "w_ffn_out"]
    shard = {n: args[n].reshape(args[n].shape[-2:]) for n in _BIG}
    comm = _Comm(shard)

    w_in_all, meta_all = _all_gather("gather_first", [shard["w_in"].astype(_WIRE_DTYPE), meta_tokens])
    w_first = _model_w_in(_gathered_matrix("w_in", w_in_all))
    meta_full = meta_all.transpose(1, 0, 2).reshape(N_META, d)
    small = {n: args[n] for n in _SMALL}

    _, grad_x, _, partial, lb, received = _forward_backward(x, loss_target, meta_full, w_first, small, lp=lp, comm=comm)
    out = {}
    for names, bufs in zip(_GRAD_GROUPS, received, strict=True):
        for n, buf in zip(names, bufs, strict=True):
            two_d = lambda t: t.reshape(t.shape[-2:])
            res = _sum_adamw("adamw_" + n, buf, shard[n], two_d(args["m_" + n]), two_d(args["v_" + n]))
            out[n] = [t.reshape(args[n].shape) for t in res]

    vec_names = _SMALL + ["meta_tokens"]
    *gathered, loss_parts = _all_gather("gather_vectors", [partial[n] for n in vec_names + ["loss"]])
    finished, loss = _finish_vectors(dict(zip(vec_names, gathered, strict=True)), lb,
                                     {n: (args[n], args["m_" + n], args["v_" + n]) for n in vec_names}, loss_parts)
    out.update(finished)
    return (loss, grad_x, *[out[n][i] for i in range(4) for n in weight_names])
```

```python
import functools

import jax
import jax.numpy as jnp
from jax import lax
from jax.experimental import pallas as pl
from jax.experimental.pallas import tpu as pltpu

F32 = jnp.float32
_MXU_DTYPE = jnp.bfloat16
_WIRE_DTYPE = jnp.bfloat16
_VMEM_LIMIT_BYTES = 56 * 1024 * 1024
_LANES = 128
_SUBLANES = 8

N_DEV = 8
N_META = 16
NORM_EPS = 1e-6
HEADS = 8
HEAD_DIM = 128
ROPE_DIM = 64
HG_CHUNK = 16
HG_BLOCK = 128
ROPE_THETA = 10000.0
D_MODEL = 1024
Q_LORA = 256
KV_LORA = 256
FFN_HIDDEN = 2816
ATTN_SCALE = (HEAD_DIM + ROPE_DIM) ** -0.5
NEG_BIG = -1e30

ADAM_LR = 0.001
ADAM_B1 = 0.9
ADAM_B2 = 0.999
ADAM_EPS = 1e-08
ADAM_WD = 0.01
ADAM_STEP = 10

CB_HQ, CB_HF, CB_HI, CB_HG, CB_C, CB_GA, CB_GB = range(7)
IN_COLS_PADDED = 7 * 1024


def _params(**kw):
    return pltpu.CompilerParams(vmem_limit_bytes=_VMEM_LIMIT_BYTES, **kw)


def _dot(a, b):
    return lax.dot_general(a, b, (((1,), (0,)), ((), ())), preferred_element_type=F32)


def _dot_nt(a, b):
    return lax.dot_general(a, b, (((1,), (1,)), ((), ())), preferred_element_type=F32)


def _dot_tn(a, b):
    return lax.dot_general(a, b, (((0,), (0,)), ((), ())), preferred_element_type=F32)


def _mx(x):
    return x.astype(_MXU_DTYPE)


def _exact_dot(m01, x, dot=_dot):
    if _MXU_DTYPE == jnp.float32:
        return dot(m01.astype(F32), x)
    m = m01.astype(jnp.bfloat16)
    x1 = x.astype(jnp.bfloat16)
    x2 = (x - x1.astype(F32)).astype(jnp.bfloat16)
    return dot(m, x1) + dot(m, x2)


def _exact_dot_nt(m01, x):
    return _exact_dot(m01, x, dot=_dot_nt)


def _sigmoid(x):
    return jax.nn.sigmoid(x)


def _silu_grad(x, s):
    return s * (1.0 + x * (1.0 - s))


def _rms_scale(x):
    return lax.rsqrt(jnp.mean(x * x, axis=-1, keepdims=True) + NORM_EPS)


def _rms_bwd(x, g, dy):
    r = _rms_scale(x)
    xh = x * r
    w = dy * g
    dx = r * (w - xh * jnp.mean(xh * w, axis=-1, keepdims=True))
    return dx, dy * xh


def _heads(fn, *arrays):
    outs = [fn(*[a[:, h * HEAD_DIM:(h + 1) * HEAD_DIM] for a in arrays]) for h in range(HEADS)]
    if isinstance(outs[0], tuple):
        return tuple(jnp.concatenate([o[i] for o in outs], axis=1) for i in range(len(outs[0])))
    return jnp.concatenate(outs, axis=1)


class _Ride:
    def __init__(self, payloads, gather):
        self.gather, self.args, self.n = gather, list(payloads), len(payloads)
        self.in_specs = [pl.BlockSpec(memory_space=pl.ANY)] * self.n
        self.out_shape = [jax.ShapeDtypeStruct((N_DEV, *p.shape[-2:]), p.dtype) for p in payloads]
        self.out_specs = [pl.BlockSpec(memory_space=pl.ANY)] * self.n
        self.scratch = [pltpu.SemaphoreType.DMA((self.n, N_DEV - 1)), pltpu.SemaphoreType.DMA((self.n, N_DEV - 1)),
                        pltpu.SemaphoreType.DMA((self.n,))]

    def split(self, rest, n_outs):
        n = self.n
        mine = (rest[:n], rest[n + n_outs:2 * n + n_outs], rest[-3:])
        return rest[n:n + n_outs] + rest[2 * n + n_outs:-3], mine

    def _copies(self, p_refs, out_refs, sems):
        send_sems, recv_sems, local_sems = sems
        x, y, c = lax.axis_index("x"), lax.axis_index("y"), lax.axis_index("c")
        me = 4 * x + 2 * y + c
        copies = []
        for i, (p_ref, out_ref) in enumerate(zip(p_refs, out_refs, strict=True)):
            part = (lambda j, p_ref=p_ref: p_ref) if self.gather else (lambda j, p_ref=p_ref: p_ref.at[j])
            copies.append(pltpu.make_async_copy(part(me), out_ref.at[me], local_sems.at[i]))
            for k in range(1, N_DEV):
                px, py, pc = x ^ (k >> 2), y ^ ((k >> 1) & 1), c ^ (k & 1)
                copies.append(pltpu.make_async_remote_copy(
                    src_ref=part(4 * px + 2 * py + pc), dst_ref=out_ref.at[me],
                    send_sem=send_sems.at[i, k - 1], recv_sem=recv_sems.at[i, k - 1],
                    device_id=(px, py, pc), device_id_type=pl.DeviceIdType.MESH))
        return copies

    def run(self, grid, refs):
        ids = [pl.program_id(i) for i in range(len(grid))]
        first = functools.reduce(jnp.logical_and, [i == 0 for i in ids])
        last = functools.reduce(jnp.logical_and, [i == g - 1 for i, g in zip(ids, grid)])

        @pl.when(first)
        def _():
            for cp in self._copies(*refs):
                cp.start()

        @pl.when(last)
        def _():
            for cp in self._copies(*refs):
                cp.wait()


class _NoRide:
    in_specs, out_shape, out_specs, scratch, args = [], [], [], [], []


def _matmul(name, a, b, *, out_dtype, tm, tn, tk, c_in=None, ride=None, b_transposed=False):
    m, k = a.shape
    n = b.shape[0] if b_transposed else b.shape[1]
    assert m % tm == 0 and n % tn == 0 and k % tk == 0, (name, a.shape, b.shape, tm, tn, tk)
    nk = k // tk
    has_c = c_in is not None
    dot = _dot_nt if b_transposed else _dot
    grid = (n // tn, m // tm, nk)
    n_in = 2 + has_c

    def body(*refs):
        a_ref, b_ref = refs[0], refs[1]
        c_ref = refs[2] if has_c else None
        rest = refs[n_in:]
        if ride is not None:
            rest, exchange = ride.split(rest, 1)
            ride.run(grid, exchange)
        o_ref = rest[0]
        acc_ref = rest[1] if nk > 1 else None

        def finish(r):
            if has_c:
                r = r + c_ref[...]
            o_ref[...] = r.astype(o_ref.dtype)

        if nk == 1:
            finish(dot(a_ref[...], b_ref[...]))
        else:
            kk = pl.program_id(2)

            @pl.when(kk == 0)
            def _():
                acc_ref[...] = jnp.zeros_like(acc_ref)

            acc_ref[...] += dot(a_ref[...], b_ref[...])

            @pl.when(kk == nk - 1)
            def _():
                finish(acc_ref[...])

    in_specs = [pl.BlockSpec((tm, tk), lambda j, i, kk: (i, kk)),
                pl.BlockSpec((tn, tk), lambda j, i, kk: (j, kk)) if b_transposed
                else pl.BlockSpec((tk, tn), lambda j, i, kk: (kk, j))]
    args = [a, b]
    aliases = {}
    if has_c:
        in_specs.append(pl.BlockSpec((tm, tn), lambda j, i, kk: (i, j)))
        args.append(c_in)
        aliases = {2: 0}
    out_shape = [jax.ShapeDtypeStruct((m, n), out_dtype)]
    out_specs = [pl.BlockSpec((tm, tn), lambda j, i, kk: (i, j))]
    scratch = [pltpu.VMEM((tm, tn), F32)] if nk > 1 else []
    if ride is not None:
        in_specs, args = in_specs + ride.in_specs, args + ride.args
        out_shape, out_specs, scratch = out_shape + ride.out_shape, out_specs + ride.out_specs, scratch + ride.scratch
    res = pl.pallas_call(
        body, name=name, out_shape=out_shape, grid=grid, in_specs=in_specs, out_specs=out_specs,
        scratch_shapes=scratch, input_output_aliases=aliases, compiler_params=_params(),
    )(*args)
    return res[0] if ride is None else res


EPILOGUE_ROWS = 272


class _Epilogue:
    def __init__(self, fn, *, rows=(), consts=(), pos=(), outs=(), accs=(), lp=None):
        self.fn, self.rows, self.consts, self.pos = fn, list(rows), list(consts), list(pos)
        self.outs, self.accs, self.lp = list(outs), list(accs), lp


def _matmul_segments(name, a_list, b, *, out_dtype=F32, tm, tn, tk, ride=None, b_transposed=False, epilogue=None):
    m = a_list[0].shape[0]
    n, k = b.shape if b_transposed else b.shape[::-1]
    steps = [a.shape[1] // tk for a in a_list]
    offs = [sum(steps[:s]) for s in range(len(steps))]
    nk = sum(steps)
    assert nk * tk == k and m % tm == 0 and n % tn == 0 and all(a.shape[1] % tk == 0 for a in a_list), name
    grid = (n // tn, m // tm, nk)
    n_seg = len(a_list)
    dot = _dot_nt if b_transposed else _dot
    ep = epilogue
    assert ep is None or tn == n, name
    n_extra = 0 if ep is None else len(ep.rows) + len(ep.consts) + len(ep.pos)
    n_outs = 1 if ep is None else len(ep.outs) + len(ep.accs)

    def body(*refs):
        a_refs, b_ref = refs[:n_seg], refs[n_seg]
        extra_refs, rest = refs[n_seg + 1:n_seg + 1 + n_extra], refs[n_seg + 1 + n_extra:]
        if ride is not None:
            rest, exchange = ride.split(rest, n_outs)
            ride.run(grid, exchange)
        out_refs, acc_ref = rest[:n_outs], rest[n_outs]
        i, kk = pl.program_id(1), pl.program_id(2)

        @pl.when(kk == 0)
        def _():
            acc_ref[...] = jnp.zeros_like(acc_ref)

        for s in range(n_seg):
            @pl.when((kk >= offs[s]) & (kk < offs[s] + steps[s]))
            def _(s=s):
                acc_ref[...] += dot(a_refs[s][...], b_ref[...])

        if ep is None:
            @pl.when(kk == nk - 1)
            def _():
                out_refs[0][...] = acc_ref[...].astype(out_refs[0].dtype)
        else:
            sum_refs = out_refs[len(ep.outs):]

            @pl.when((kk == 0) & (i == 0))
            def _():
                for ref in sum_refs:
                    ref[...] = jnp.zeros_like(ref)

            @pl.when(kk == nk - 1)
            def _():
                rs = EPILOGUE_ROWS if tm % EPILOGUE_ROWS == 0 else tm
                n_r, n_c = len(ep.rows), len(ep.consts)
                for r0 in range(0, tm, rs):
                    sl = slice(r0, r0 + rs)
                    tiles = ([r[sl, :] for r in extra_refs[:n_r]] + [c[...] for c in extra_refs[n_r:n_r + n_c]]
                             + [t[sl, :] for t in extra_refs[n_r + n_c:]])
                    res_outs, res_sums = ep.fn(acc_ref[sl, :], *tiles)
                    for ref, val in zip(out_refs[:len(ep.outs)], res_outs, strict=True):
                        ref[sl, :] = val.astype(ref.dtype)
                    for ref, val in zip(sum_refs, res_sums, strict=True):
                        ref[...] += val.reshape(rs // _SUBLANES, _SUBLANES, val.shape[-1]).sum(axis=0)

    seg_spec = lambda s: pl.BlockSpec(
        (tm, tk), functools.partial(lambda j, i, kk, off, ns: (i, jnp.clip(kk - off, 0, ns - 1)), off=offs[s], ns=steps[s]))
    b_spec = (pl.BlockSpec((tn, tk), lambda j, i, kk: (j, kk)) if b_transposed
              else pl.BlockSpec((tk, tn), lambda j, i, kk: (kk, j)))
    in_specs = [seg_spec(s) for s in range(n_seg)] + [b_spec]
    args = list(a_list) + [b]
    row_spec = lambda w: pl.BlockSpec((tm, w), lambda j, i, kk: (i, 0))
    if ep is None:
        out_shape = [jax.ShapeDtypeStruct((m, n), out_dtype)]
        out_specs = [pl.BlockSpec((tm, tn), lambda j, i, kk: (i, j))]
    else:
        tiles_per_example = ep.lp // tm
        row_ins = [r if isinstance(r, tuple) else (r, r.shape[1], 0) for r in ep.rows]
        in_specs += ([pl.BlockSpec((tm, wd), functools.partial(lambda j, i, kk, cb: (i, cb), cb=cb)) for _, wd, cb in row_ins]
                     + [pl.BlockSpec(c.shape, lambda j, i, kk: (0, 0)) for c in ep.consts]
                     + [pl.BlockSpec((tm, p.shape[1]), lambda j, i, kk: (i % tiles_per_example, 0)) for p in ep.pos])
        args += [arr for arr, _, _ in row_ins] + ep.consts + ep.pos
        out_shape = ([jax.ShapeDtypeStruct((m, w), dt) for w, dt in ep.outs]
                     + [jax.ShapeDtypeStruct((_SUBLANES, w), F32) for w in ep.accs])
        out_specs = ([row_spec(w) for w, _ in ep.outs]
                     + [pl.BlockSpec((_SUBLANES, w), lambda j, i, kk: (0, 0)) for w in ep.accs])
    scratch = [pltpu.VMEM((tm, tn), F32)]
    if ride is not None:
        in_specs, args = in_specs + ride.in_specs, args + ride.args
        out_shape, out_specs, scratch = out_shape + ride.out_shape, out_specs + ride.out_specs, scratch + ride.scratch
    res = pl.pallas_call(
        body, name=name, out_shape=out_shape, grid=grid, in_specs=in_specs, out_specs=out_specs,
        scratch_shapes=scratch, compiler_params=_params(),
    )(*args)
    if ep is None:
        return res[0] if ride is None else res
    n_o = len(ep.outs)
    return (res[:n_o], res[n_o:n_outs], *res[n_outs:])


def _matmul_tn(name, x, dy, *, tk, tn, tr, out_dtype=F32):
    r, k = x.shape
    _, n = dy.shape
    assert r % tr == 0 and k % tk == 0 and n % tn == 0, (name, x.shape, dy.shape)
    n_r = r // tr
    direct = out_dtype == F32

    def body(x_ref, dy_ref, o_ref, *scratch):
        acc_ref = o_ref if direct else scratch[0]

        @pl.when(pl.program_id(2) == 0)
        def _():
            acc_ref[...] = jnp.zeros_like(acc_ref)

        acc_ref[...] += _dot_tn(x_ref[...], dy_ref[...])
        if not direct:
            @pl.when(pl.program_id(2) == n_r - 1)
            def _():
                o_ref[...] = acc_ref[...].astype(o_ref.dtype)

    return pl.pallas_call(
        body, name=name,
        out_shape=jax.ShapeDtypeStruct((k, n), out_dtype),
        grid=(k // tk, n // tn, n_r),
        in_specs=[pl.BlockSpec((tr, tk), lambda kb, nb, rr: (rr, kb)),
                  pl.BlockSpec((tr, tn), lambda kb, nb, rr: (rr, nb))],
        out_specs=pl.BlockSpec((tk, tn), lambda kb, nb, rr: (kb, nb)),
        scratch_shapes=[] if direct else [pltpu.VMEM((tk, tn), F32)],
        compiler_params=_params(),
    )(x, dy)


def _ffn_in_swiglu(u, w, *, tm, tn):
    r, k = u.shape
    h = w.shape[1] // 2
    assert r % tm == 0 and h % tn == 0
    nj = h // tn

    def body(u_ref, wg_ref, wu_ref, act_ref, dgate_ref, dup_ref):
        uu = u_ref[...]
        gt, up = _dot(uu, wg_ref[...]), _dot(uu, wu_ref[...])
        s = _sigmoid(gt)
        silu = gt * s
        act_ref[...] = (silu * up).astype(act_ref.dtype)
        dgate_ref[...] = (up * _silu_grad(gt, s)).astype(dgate_ref.dtype)
        dup_ref[...] = silu.astype(dup_ref.dtype)

    tile = pl.BlockSpec((tm, tn), lambda j, i: (i, j))
    return pl.pallas_call(
        body, name="ffn_in_swiglu",
        out_shape=[jax.ShapeDtypeStruct((r, h), _MXU_DTYPE)] * 3,
        grid=(nj, r // tm),
        in_specs=[pl.BlockSpec((tm, k), lambda j, i: (i, 0)),
                  pl.BlockSpec((k, tn), lambda j, i: (0, j)),
                  pl.BlockSpec((k, tn), lambda j, i: (0, nj + j))],
        out_specs=[tile] * 3,
        compiler_params=_params(),
    )(u, w, w)


def _d_ffn_out_swiglu(dy, w, act_dgate, act_dup, *, tm, tn):
    r, k = dy.shape
    h = w.shape[0]
    assert r % tm == 0 and h % tn == 0

    def body(dy_ref, w_ref, pg_ref, pu_ref, dgt_ref, dup_ref):
        da = _dot_nt(dy_ref[...], w_ref[...])
        dgt_ref[...] = (da * pg_ref[...].astype(F32)).astype(dgt_ref.dtype)
        dup_ref[...] = (da * pu_ref[...].astype(F32)).astype(dup_ref.dtype)

    tile = pl.BlockSpec((tm, tn), lambda j, i: (i, j))
    return pl.pallas_call(
        body, name="d_ffn_out_swiglu",
        out_shape=[jax.ShapeDtypeStruct((r, h), _MXU_DTYPE)] * 2,
        grid=(h // tn, r // tm),
        in_specs=[pl.BlockSpec((tm, k), lambda j, i: (i, 0)), pl.BlockSpec((tn, k), lambda j, i: (j, 0)), tile, tile],
        out_specs=[tile] * 2,
        compiler_params=_params(),
    )(dy, w, act_dgate, act_dup)


def _proj_q_rope(cqn, w_q, c_tab, s_tab, *, tm, lp):
    r, k = cqn.shape
    tiles_per_example = lp // tm
    pair = 2 * HEAD_DIM

    def body(x_ref, wn_ref, wp_ref, ws_ref, c_ref, s_ref, o_ref):
        x = x_ref[...]
        c2, s2 = jnp.tile(c_ref[...], (1, 2)), jnp.tile(s_ref[...], (1, 2))
        nope = _dot(x, wn_ref[...])
        roped = _dot(x, wp_ref[...]) * c2 + _dot(x, ws_ref[...]) * s2
        hs = lambda t, h: t[:, h * HEAD_DIM:(h + 1) * HEAD_DIM]
        o_ref[...] = jnp.concatenate([hs(nope, 0), hs(roped, 0), hs(nope, 1), hs(roped, 1)], axis=1).astype(o_ref.dtype)

    w_blk = lambda part: pl.BlockSpec((k, pair), functools.partial(lambda h, i, part: (0, part * (HEADS // 2) + h), part=part))
    tab = pl.BlockSpec((tm, HEAD_DIM), lambda h, i: (i % tiles_per_example, 0))
    return pl.pallas_call(
        body, name="proj_q_rope",
        out_shape=jax.ShapeDtypeStruct((r, HEADS * QK_DIM), _MXU_DTYPE),
        grid=(HEADS // 2, r // tm),
        in_specs=[pl.BlockSpec((tm, k), lambda h, i: (i, 0)), w_blk(0), w_blk(1), w_blk(2), tab, tab],
        out_specs=pl.BlockSpec((tm, 2 * QK_DIM), lambda h, i: (i, h)),
        compiler_params=_params(),
    )(cqn, w_q, w_q, w_q, c_tab, s_tab)


def _rowwise(name, body, *, rows, tr, lp, ins, outs, accs=()):
    assert rows % tr == 0 and lp % tr == 0 and tr % 16 == 0
    tiles_per_example = lp // tr
    in_specs, arrays = [], []
    for spec in ins:
        if spec[0] == "row":
            _, arr, width, cb = spec
            in_specs.append(pl.BlockSpec((tr, width), functools.partial(lambda i, cb: (i, cb), cb=cb)))
        elif spec[0] == "const":
            arr = spec[1]
            in_specs.append(pl.BlockSpec(arr.shape, lambda i: (0, 0)))
        else:
            arr = spec[1]
            in_specs.append(pl.BlockSpec((tr, arr.shape[1]), lambda i: (i % tiles_per_example, 0)))
        arrays.append(arr)
    n_in, n_out = len(ins), len(outs)

    def kern(*refs):
        res_outs, res_accs = body(*[r[...] for r in refs[:n_in]])
        for ref, val in zip(refs[n_in:n_in + n_out], res_outs, strict=True):
            ref[...] = val.astype(ref.dtype)
        acc_refs = refs[n_in + n_out:]
        if acc_refs:
            @pl.when(pl.program_id(0) == 0)
            def _():
                for ref in acc_refs:
                    ref[...] = jnp.zeros_like(ref)

            for ref, val in zip(acc_refs, res_accs, strict=True):
                ref[...] += val.reshape(tr // _SUBLANES, _SUBLANES, val.shape[-1]).sum(axis=0)

    out_shape = ([jax.ShapeDtypeStruct((rows, w), dt) for w, dt in outs]
                 + [jax.ShapeDtypeStruct((_SUBLANES, w), F32) for w in accs])
    out_specs = ([pl.BlockSpec((tr, w), lambda i: (i, 0)) for w, _ in outs]
                 + [pl.BlockSpec((_SUBLANES, w), lambda i: (0, 0)) for w in accs])
    res = pl.pallas_call(
        kern, name=name, out_shape=out_shape, grid=(rows // tr,),
        in_specs=in_specs, out_specs=out_specs, compiler_params=_params(),
    )(*arrays)
    return res[:n_out], list(res[n_out:])


def _assemble(name, x, head_rows, lp):
    batch, seq, d = x.shape
    tc = 256

    def body(x_ref, m_ref, o_ref):
        o_ref[0:N_META, :] = m_ref[...]
        o_ref[N_META:N_META + seq, :] = x_ref[0]
        if lp > N_META + seq:
            o_ref[N_META + seq:, :] = jnp.zeros((lp - N_META - seq, tc), F32)

    return pl.pallas_call(
        body, name=name,
        out_shape=jax.ShapeDtypeStruct((batch * lp, d), F32),
        grid=(batch, d // tc),
        in_specs=[pl.BlockSpec((1, seq, tc), lambda b, j: (b, 0, j)),
                  pl.BlockSpec((N_META, tc), lambda b, j: (0, j))],
        out_specs=pl.BlockSpec((lp, tc), lambda b, j: (b, j)),
        compiler_params=_params(),
    )(x, head_rows)


def _meta_grad(dh0, batch, lp):
    d = dh0.shape[1]

    def body(g_ref, o_ref):
        @pl.when(pl.program_id(0) == 0)
        def _():
            o_ref[...] = jnp.zeros_like(o_ref)

        o_ref[...] += g_ref[...]

    return pl.pallas_call(
        body, name="meta_grad",
        out_shape=jax.ShapeDtypeStruct((N_META, d), F32),
        grid=(batch,),
        in_specs=[pl.BlockSpec((N_META, d), lambda b: (b * (lp // N_META), 0))],
        out_specs=pl.BlockSpec((N_META, d), lambda b: (0, 0)),
        compiler_params=_params(),
    )(dh0)


def _segment_masks():
    t = lax.broadcasted_iota(jnp.int32, (HG_BLOCK, HG_BLOCK), 0)
    s = lax.broadcasted_iota(jnp.int32, (HG_BLOCK, HG_BLOCK), 1)
    same = lax.shift_right_logical(t, 4) == lax.shift_right_logical(s, 4)
    lower = same & (s <= t)
    upper = same & (s >= t)
    first_half = same & ((s & 15) <= 7)
    return same, lower, upper, first_half


def _hgrn_gates(hq, hf, lb):
    sq = _sigmoid(hq)
    q = hq * sq
    sf = _sigmoid(hf)
    f = lb + (1.0 - lb) * sf
    return q, sq, sf, f


def _hgrn_decays(g, same, lower, first_half):
    b = _exact_dot(lower, g)
    b_last = _exact_dot(same, g)
    b_ref = _exact_dot(first_half, g)
    return b, b_last, b_ref


def _hgrn_fwd(p, lb, gh, *, batch, lp, ride=None):
    rows = batch * lp
    nb = lp // HG_BLOCK
    n_chunks = HG_BLOCK // HG_CHUNK

    def body(hq_ref, hf_ref, hi_ref, hg_ref, lb_ref, gh_ref, *rest):
        if ride is not None:
            rest, exchange = ride.split(rest, 3)
            ride.run((batch, nb), exchange)
        o_ref, z_ref, st_ref, s_scr = rest

        @pl.when(pl.program_id(1) == 0)
        def _():
            s_scr[...] = jnp.zeros_like(s_scr)

        same, lower, _, first_half = _segment_masks()
        v = hi_ref[...]
        q, _, _, f = _hgrn_gates(hq_ref[...], hf_ref[...], lb_ref[...])
        k = 1.0 - f
        b, b_last, b_ref = _hgrn_decays(jnp.log(f), same, lower, first_half)
        qt = _mx(q * jnp.exp(b))
        kh = _mx(k * jnp.exp(b_last - b))
        vm = _mx(v)
        el = jnp.exp(b_last)
        qc = _mx(q * jnp.exp(b - b_ref))
        kc = _mx(k * jnp.exp(b_ref - b))

        def intra(qc_h, kc_h, v_h):
            a = jnp.where(lower, _dot_nt(qc_h, kc_h), 0.0)
            return _dot(_mx(a), v_h)

        o_intra = _heads(intra, qc, kc, vm)

        states = [s_scr[h] for h in range(HEADS)]
        o_inter = [[None] * HEADS for _ in range(n_chunks)]
        for c in range(n_chunks):
            rs = slice(c * HG_CHUNK, (c + 1) * HG_CHUNK)
            for h in range(HEADS):
                cs = slice(h * HEAD_DIM, (h + 1) * HEAD_DIM)
                st_m = _mx(states[h])
                st_ref[c, h] = st_m
                o_inter[c][h] = _dot_nt(qt[rs, cs], st_m)
                states[h] = states[h] * el[c * HG_CHUNK:c * HG_CHUNK + 1, cs] + _dot_tn(vm[rs, cs], kh[rs, cs])
        for h in range(HEADS):
            s_scr[h] = states[h]

        o = o_intra + jnp.concatenate([jnp.concatenate(row, axis=1) for row in o_inter], axis=0)
        o_ref[...] = o
        hg = hg_ref[...]
        n = _heads(lambda o_h: o_h * _rms_scale(o_h), o) * gh_ref[...]
        z_ref[...] = (n * hg * _sigmoid(hg)).astype(z_ref.dtype)

    blk = lambda cb: pl.BlockSpec((HG_BLOCK, D_MODEL), functools.partial(lambda b, j, cb: (b * nb + j, cb), cb=cb))
    row_out = pl.BlockSpec((HG_BLOCK, D_MODEL), lambda b, j: (b * nb + j, 0))
    const = pl.BlockSpec((1, D_MODEL), lambda b, j: (0, 0))
    extra = ride if ride is not None else _NoRide
    return pl.pallas_call(
        body, name="hgrn_fwd",
        out_shape=[jax.ShapeDtypeStruct((rows, D_MODEL), F32),
                   jax.ShapeDtypeStruct((rows, D_MODEL), _MXU_DTYPE),
                   jax.ShapeDtypeStruct((rows // HG_CHUNK, HEADS, HEAD_DIM, HEAD_DIM), _MXU_DTYPE)] + extra.out_shape,
        grid=(batch, nb),
        in_specs=[blk(CB_HQ), blk(CB_HF), blk(CB_HI), blk(CB_HG), const, const] + extra.in_specs,
        out_specs=[row_out, row_out,
                   pl.BlockSpec((n_chunks, HEADS, HEAD_DIM, HEAD_DIM), lambda b, j: (b * nb + j, 0, 0, 0))]
        + extra.out_specs,
        scratch_shapes=[pltpu.VMEM((HEADS, HEAD_DIM, HEAD_DIM), F32)] + extra.scratch,
        compiler_params=_params(),
    )(p, p, p, p, lb, gh, *extra.args)


def _hgrn_bwd(p, o, dz, states, lb, gh, *, batch, lp, ride=None):
    rows = batch * lp
    nb = lp // HG_BLOCK
    n_chunks = HG_BLOCK // HG_CHUNK

    def body(hq_ref, hf_ref, hi_ref, hg_ref, o_ref, dz_ref, st_ref, lb_ref, gh_ref, *rest):
        if ride is not None:
            rest, exchange = ride.split(rest, 3)
            ride.run((batch, nb), exchange)
        dp_ref, dlb_ref, dgh_ref, ds_scr = rest
        first = (pl.program_id(0) == 0) & (pl.program_id(1) == 0)

        @pl.when(first)
        def _():
            dlb_ref[...] = jnp.zeros_like(dlb_ref)
            dgh_ref[...] = jnp.zeros_like(dgh_ref)

        @pl.when(pl.program_id(1) == 0)
        def _():
            ds_scr[...] = jnp.zeros_like(ds_scr)

        same, lower, upper, first_half = _segment_masks()
        lbv = lb_ref[...]
        hq, hf, v, hg = hq_ref[...], hf_ref[...], hi_ref[...], hg_ref[...]
        q, sq, sf, f = _hgrn_gates(hq, hf, lbv)
        k = 1.0 - f
        b, b_last, b_ref = _hgrn_decays(jnp.log(f), same, lower, first_half)
        e_b = jnp.exp(b)
        e_kh = jnp.exp(b_last - b)
        e_qc = jnp.exp(b - b_ref)
        e_kc = jnp.exp(b_ref - b)
        qt, kh, qc, kc = q * e_b, k * e_kh, q * e_qc, k * e_kc

        o = o_ref[...]
        dz = dz_ref[...].astype(F32)
        ghv = gh_ref[...]
        sg = _sigmoid(hg)
        r = _heads(lambda o_h: jnp.broadcast_to(_rms_scale(o_h), o_h.shape), o)
        oh = o * r
        dn = dz * hg * sg
        dhg = dz * oh * ghv * _silu_grad(hg, sg)
        w = dn * ghv
        do = r * (w - oh * _heads(lambda t: jnp.broadcast_to(jnp.mean(t, axis=-1, keepdims=True), t.shape), oh * w))
        dgh_ref[...] += (dn * oh).reshape(HG_BLOCK // _SUBLANES, _SUBLANES, D_MODEL).sum(axis=0)

        qt_m, kh_m, v_m, do_m = _mx(qt), _mx(kh), _mx(v), _mx(do)
        el_all = jnp.exp(b_last)

        def intra(qc_h, kc_h, v_h, do_h):
            a = _mx(jnp.where(lower, _dot_nt(qc_h, kc_h), 0.0))
            da = _mx(jnp.where(lower, _dot_nt(do_h, v_h), 0.0))
            return _dot(da, kc_h), _dot_tn(da, qc_h), _dot_tn(a, do_h)

        dqc, dkc, dv_intra = _heads(intra, _mx(qc), _mx(kc), v_m, do_m)

        d_states = [ds_scr[h] for h in range(HEADS)]
        grid_of = lambda: [[None] * HEADS for _ in range(n_chunks)]
        dkh_p, dv_p, dbl_p, dqt_p = grid_of(), grid_of(), grid_of(), grid_of()
        for c in reversed(range(n_chunks)):
            rs = slice(c * HG_CHUNK, (c + 1) * HG_CHUNK)
            for h in range(HEADS):
                cs = slice(h * HEAD_DIM, (h + 1) * HEAD_DIM)
                st = st_ref[c, h]
                ds_t = d_states[h]
                ds_m = _mx(ds_t)
                el = el_all[c * HG_CHUNK:c * HG_CHUNK + 1, cs]
                dkh_p[c][h] = _dot(v_m[rs, cs], ds_m)
                dv_p[c][h] = _dot_nt(kh_m[rs, cs], ds_m)
                dbl = jnp.sum(ds_t * st.astype(F32), axis=0, keepdims=True) * el
                dbl_p[c][h] = jnp.broadcast_to(dbl, (HG_CHUNK, HEAD_DIM))
                dqt_p[c][h] = _dot(do_m[rs, cs], st)
                d_states[h] = ds_t * el + _dot_tn(do_m[rs, cs], qt_m[rs, cs])
        for h in range(HEADS):
            ds_scr[h] = d_states[h]
        whole = lambda parts: jnp.concatenate([jnp.concatenate(row, axis=1) for row in parts], axis=0)

        dqt, dkh = whole(dqt_p), whole(dkh_p)
        dq = dqt * e_b + dqc * e_qc
        dk = dkh * e_kh + dkc * e_kc
        t_kh = dkh * kh
        db_rows = dqt * qt + dqc * qc - dkc * kc - t_kh
        dg = _exact_dot(upper, db_rows) + _exact_dot(same, t_kh) + whole(dbl_p)
        df = dg / f - dk
        dhf = df * (1.0 - lbv) * sf * (1.0 - sf)
        dlb_ref[...] += (df * (1.0 - sf)).reshape(HG_BLOCK // _SUBLANES, _SUBLANES, D_MODEL).sum(axis=0)
        dhq = dq * _silu_grad(hq, sq)
        dp_ref[...] = jnp.concatenate([dhq, dhf, dv_intra + whole(dv_p), dhg], axis=1).astype(dp_ref.dtype)

    rev = lambda b, j: b * nb + (nb - 1 - j)
    blk = lambda cb: pl.BlockSpec((HG_BLOCK, D_MODEL), functools.partial(lambda b, j, cb: (rev(b, j), cb), cb=cb))
    row = pl.BlockSpec((HG_BLOCK, D_MODEL), lambda b, j: (rev(b, j), 0))
    const = pl.BlockSpec((1, D_MODEL), lambda b, j: (0, 0))
    acc = pl.BlockSpec((_SUBLANES, D_MODEL), lambda b, j: (0, 0))
    extra = ride if ride is not None else _NoRide
    dp, dlb, dgh, *exchanged = pl.pallas_call(
        body, name="hgrn_bwd",
        out_shape=[jax.ShapeDtypeStruct((rows, 4 * D_MODEL), _MXU_DTYPE),
                   jax.ShapeDtypeStruct((_SUBLANES, D_MODEL), F32),
                   jax.ShapeDtypeStruct((_SUBLANES, D_MODEL), F32)] + extra.out_shape,
        grid=(batch, nb),
        in_specs=[blk(CB_HQ), blk(CB_HF), blk(CB_HI), blk(CB_HG), row, row,
                  pl.BlockSpec((n_chunks, HEADS, HEAD_DIM, HEAD_DIM), lambda b, j: (rev(b, j), 0, 0, 0)),
                  const, const] + extra.in_specs,
        out_specs=[pl.BlockSpec((HG_BLOCK, 4 * D_MODEL), lambda b, j: (rev(b, j), 0)), acc, acc] + extra.out_specs,
        scratch_shapes=[pltpu.VMEM((HEADS, HEAD_DIM, HEAD_DIM), F32)] + extra.scratch,
        compiler_params=_params(),
    )(p, p, p, p, o, dz, states, lb, gh, *extra.args)
    return (dp, dlb, dgh, *exchanged)


QK_DIM = 2 * HEAD_DIM
ATTN_TQ_FWD = 512
ATTN_TQ_BWD = 256
ATTN_KEY_CHUNK = 1024


def _query_tiles(lp, tq):
    return [(r0, min(tq, lp - r0)) for r0 in range(0, lp, tq)]


def _attn_fwd(q_cat, kv, kp, *, batch, lp):
    rows = batch * lp

    def body(q_ref, kn_ref, kp_ref, v_ref, o_ref, lse_ref):
        k_cat = jnp.concatenate([kn_ref[...], kp_ref[...]], axis=1)
        for r0, tq in _query_tiles(lp, ATTN_TQ_FWD):
            q_t = q_ref[r0:r0 + tq, :]
            i = lax.broadcasted_iota(jnp.int32, (tq, tq), 0)
            j = lax.broadcasted_iota(jnp.int32, (tq, tq), 1)
            s_diag = jnp.where(j <= i, _dot_nt(q_t, k_cat[r0:r0 + tq]) * ATTN_SCALE, NEG_BIG)
            m = jnp.max(s_diag, axis=1, keepdims=True)
            if r0:
                s_past = _dot_nt(q_t, k_cat[0:r0]) * ATTN_SCALE
                m = jnp.maximum(m, jnp.max(s_past, axis=1, keepdims=True))
            p_diag = jnp.exp(s_diag - m)
            l = jnp.sum(p_diag, axis=1, keepdims=True)
            acc = _dot(_mx(p_diag), v_ref[r0:r0 + tq, :])
            if r0:
                p_past = jnp.exp(s_past - m)
                l = l + jnp.sum(p_past, axis=1, keepdims=True)
                acc = acc + _dot(_mx(p_past), v_ref[0:r0, :])
            o_ref[r0:r0 + tq, :] = (acc / l).astype(o_ref.dtype)
            lse_ref[r0:r0 + tq, :] = jnp.broadcast_to(m + jnp.log(l), (tq, HEAD_DIM))

    head_blk = pl.BlockSpec((lp, HEAD_DIM), lambda b, h: (b, h))
    return pl.pallas_call(
        body, name="attn_fwd",
        out_shape=[jax.ShapeDtypeStruct((rows, D_MODEL), _MXU_DTYPE),
                   jax.ShapeDtypeStruct((rows, D_MODEL), F32)],
        grid=(batch, HEADS),
        in_specs=[pl.BlockSpec((lp, QK_DIM), lambda b, h: (b, h)), head_blk,
                  pl.BlockSpec((lp, HEAD_DIM), lambda b, h: (b, 0)),
                  pl.BlockSpec((lp, HEAD_DIM), lambda b, h: (b, HEADS + h))],
        out_specs=[head_blk, head_blk],
        compiler_params=_params(),
    )(q_cat, kv, kp, kv)


def _attn_bwd(q_cat, kv, kp, do, o, lse, c_tab, s_tab, *, batch, lp, ride=None):
    rows = batch * lp

    def body(q_ref, kn_ref, kp_ref, v_ref, do_ref, o_ref, lse_ref, c_ref, s_ref, *rest):
        if ride is not None:
            rest, exchange = ride.split(rest, 6)
            ride.run((batch, HEADS), exchange)
        dqn_ref, dqc_ref, dqs_ref, dkn_ref, dkp_ref, dv_ref, dk_acc, dv_acc = rest
        dk_acc[...] = jnp.zeros_like(dk_acc)
        dv_acc[...] = jnp.zeros_like(dv_acc)
        k_cat = jnp.concatenate([kn_ref[...], kp_ref[...]], axis=1)
        k_t = k_cat.T
        lane = lax.broadcasted_iota(jnp.int32, (_SUBLANES, HEAD_DIM), 1)
        lse_row = _exact_dot_nt(lane == 0, lse_ref[...])
        delta = _exact_dot_nt(lane >= 0, do_ref[...].astype(F32) * o_ref[...].astype(F32))
        for r0, tq in _query_tiles(lp, ATTN_TQ_BWD):
            cols = slice(r0, r0 + tq)
            q_t_, do_t_ = q_ref[cols, :], do_ref[cols, :]
            lse_t, delta_t = lse_row[0:1, cols], delta[0:1, cols]
            chunks = [(c0, min(ATTN_KEY_CHUNK, r0 - c0), False) for c0 in range(0, r0, ATTN_KEY_CHUNK)] + [(r0, tq, True)]
            dq_t = jnp.zeros((QK_DIM, tq), F32)
            for c0, n, diagonal in chunks:
                keys = slice(c0, c0 + n)
                s = _dot_nt(k_cat[keys], q_t_) * ATTN_SCALE
                if diagonal:
                    jk = lax.broadcasted_iota(jnp.int32, (n, tq), 0)
                    iq = lax.broadcasted_iota(jnp.int32, (n, tq), 1)
                    s = jnp.where(jk <= iq, s, NEG_BIG)
                pexp = jnp.exp(s - lse_t)
                dp = _dot_nt(v_ref[keys, :], do_t_)
                ds = _mx(pexp * (dp - delta_t) * ATTN_SCALE)
                dk_acc[keys, :] += _dot(ds, q_t_)
                dv_acc[keys, :] += _dot(_mx(pexp), do_t_)
                dq_t = dq_t + _dot(k_t[:, keys], ds)
            dq = dq_t.T
            d_rope = dq[:, HEAD_DIM:]
            dqn_ref[cols, :] = dq[:, :HEAD_DIM].astype(dqn_ref.dtype)
            dqc_ref[cols, :] = (d_rope * c_ref[cols, :]).astype(dqc_ref.dtype)
            dqs_ref[cols, :] = (d_rope * s_ref[cols, :]).astype(dqs_ref.dtype)

        dkn_ref[...] = dk_acc[:, 0:HEAD_DIM].astype(dkn_ref.dtype)
        dv_ref[...] = dv_acc[...].astype(dv_ref.dtype)

        @pl.when(pl.program_id(1) == 0)
        def _():
            dkp_ref[...] = jnp.zeros_like(dkp_ref)

        dkp_ref[...] += dk_acc[:, HEAD_DIM:]

    head_blk = pl.BlockSpec((lp, HEAD_DIM), lambda b, h: (b, h))
    cat_blk = pl.BlockSpec((lp, QK_DIM), lambda b, h: (b, h))
    shared_blk = pl.BlockSpec((lp, HEAD_DIM), lambda b, h: (b, 0))
    table_blk = pl.BlockSpec((lp, HEAD_DIM), lambda b, h: (0, 0))
    extra = ride if ride is not None else _NoRide
    return pl.pallas_call(
        body, name="attn_bwd",
        out_shape=[jax.ShapeDtypeStruct((rows, D_MODEL), _MXU_DTYPE)] * 3 + [
                   jax.ShapeDtypeStruct((rows, D_MODEL), _MXU_DTYPE),
                   jax.ShapeDtypeStruct((rows, HEAD_DIM), F32),
                   jax.ShapeDtypeStruct((rows, D_MODEL), _MXU_DTYPE)] + extra.out_shape,
        grid=(batch, HEADS),
        in_specs=[cat_blk, head_blk, shared_blk, pl.BlockSpec((lp, HEAD_DIM), lambda b, h: (b, HEADS + h)),
                  head_blk, head_blk, head_blk, table_blk, table_blk] + extra.in_specs,
        out_specs=[head_blk, head_blk, head_blk, head_blk, shared_blk, head_blk] + extra.out_specs,
        scratch_shapes=[pltpu.VMEM((lp, QK_DIM), F32), pltpu.VMEM((lp, HEAD_DIM), F32)] + extra.scratch,
        compiler_params=_params(),
    )(q_cat, kv, kp, kv, do, o, lse, c_tab, s_tab, *extra.args)


def _all_gather(name, blocks):
    n = len(blocks)

    def body(*refs):
        x_refs, out_refs, (send_sems, recv_sems, local_sems) = refs[:n], refs[n:2 * n], refs[2 * n:]
        x, y, c = lax.axis_index("x"), lax.axis_index("y"), lax.axis_index("c")
        me, sibling = (x, y, c), (x, y, 1 - c)
        chips = [(1 - x, y), (x, 1 - y), (1 - x, 1 - y)]

        def slot(i, px, py, pc):
            return out_refs[i].at[4 * px + 2 * py + pc]

        def copy(i, k, blk, to, src=None):
            return pltpu.make_async_remote_copy(
                src_ref=slot(i, *blk) if src is None else src, dst_ref=slot(i, *blk),
                send_sem=send_sems.at[i, k], recv_sem=recv_sems.at[i, k],
                device_id=to, device_id_type=pl.DeviceIdType.MESH)

        mine = [pltpu.make_async_copy(x_refs[i], slot(i, *me), local_sems.at[i]) for i in range(n)]
        first = [copy(i, 0, me, sibling, src=x_refs[i]) for i in range(n)]
        first += [copy(i, 1 + j, me, (*chip, c), src=x_refs[i]) for i in range(n) for j, chip in enumerate(chips)]
        for cp in mine + first:
            cp.start()
        passed = []
        for i in range(n):
            for j, chip in enumerate(chips):
                copy(i, 1 + j, (*chip, c), me).wait_recv()
                passed.append(copy(i, 4 + j, (*chip, c), sibling))
                passed[-1].start()
        for i in range(n):
            copy(i, 0, sibling, me).wait_recv()
            for j, chip in enumerate(chips):
                copy(i, 4 + j, (*chip, 1 - c), me).wait_recv()
        for cp in first + passed:
            cp.wait_send()
        for cp in mine:
            cp.wait()

    return pl.pallas_call(
        body, name=name,
        out_shape=[jax.ShapeDtypeStruct((N_DEV, *b.shape), b.dtype) for b in blocks],
        in_specs=[pl.BlockSpec(memory_space=pl.ANY)] * n,
        out_specs=[pl.BlockSpec(memory_space=pl.ANY)] * n,
        scratch_shapes=[pltpu.SemaphoreType.DMA((n, 7)), pltpu.SemaphoreType.DMA((n, 7)),
                        pltpu.SemaphoreType.DMA((n,))],
    )(*blocks)


def _adamw_math(w, g, m, v):
    nm = ADAM_B1 * m + (1.0 - ADAM_B1) * g
    nv = ADAM_B2 * v + (1.0 - ADAM_B2) * (g * g)
    m_hat = nm / (1.0 - ADAM_B1 ** ADAM_STEP)
    v_hat = nv / (1.0 - ADAM_B2 ** ADAM_STEP)
    return -ADAM_LR * (m_hat / (jnp.sqrt(v_hat) + ADAM_EPS) + ADAM_WD * w), nm, nv


def _sum_adamw(name, parts, w, m, v):
    rows, cols = w.shape
    tr = rows // 4 if rows % 64 == 0 and rows * cols > (1 << 16) else rows

    def body(p_ref, w_ref, m_ref, v_ref, g_ref, d_ref, nm_ref, nv_ref):
        g = p_ref[0].astype(F32)
        for dev in range(1, N_DEV):
            g = g + p_ref[dev].astype(F32)
        g_ref[...] = g
        d_ref[...], nm_ref[...], nv_ref[...] = _adamw_math(w_ref[...], g, m_ref[...], v_ref[...])

    spec = pl.BlockSpec((tr, cols), lambda i: (i, 0))
    return pl.pallas_call(
        body, name=name,
        out_shape=[jax.ShapeDtypeStruct((rows, cols), F32)] * 4,
        grid=(rows // tr,),
        in_specs=[pl.BlockSpec((N_DEV, tr, cols), lambda i: (0, i, 0))] + [spec] * 3, out_specs=[spec] * 4,
        compiler_params=_params(),
    )(parts, w, m, v)


def _finish_vectors(gathered, lb, params, loss_parts):
    names = list(params)
    n = len(names)

    def body(*refs):
        g_refs, lb_ref, loss_ref = refs[:n], refs[n], refs[n + 1]
        wmv_refs = refs[n + 2:4 * n + 2]
        out_refs, loss_out = refs[4 * n + 2:-1], refs[-1]
        sq = loss_ref[0]
        for dev in range(1, N_DEV):
            sq = sq + loss_ref[dev]
        sq = jnp.sum(jnp.sum(sq, axis=0, keepdims=True), axis=1, keepdims=True)
        loss_out[...] = sq * (0.5 / D_MODEL)
        me = 4 * lax.axis_index("x") + 2 * lax.axis_index("y") + lax.axis_index("c")
        for i, name in enumerate(names):
            g_ref = g_refs[i]
            w_ref, m_ref, v_ref = wmv_refs[3 * i:3 * i + 3]
            if name == "meta_tokens":
                width = w_ref.shape[1]
                mine = pl.ds(pl.multiple_of(me * width, width), width)
                g = g_ref[0, :, mine]
                for dev in range(1, N_DEV):
                    g = g + g_ref[dev, :, mine]
            else:
                g = g_ref[0]
                for dev in range(1, N_DEV):
                    g = g + g_ref[dev]
                g = jnp.sum(g, axis=0, keepdims=True)
                if name == "hg_norm_g":
                    g = functools.reduce(jnp.add, [g[:, h * HEAD_DIM:(h + 1) * HEAD_DIM] for h in range(HEADS)])
                if name == "lb_logits":
                    lbv = lb_ref[...]
                    g = g * lbv * (1.0 - lbv)
                    g = jnp.concatenate([g, -g], axis=0)
            outs = (g, *_adamw_math(w_ref[...], g, m_ref[...], v_ref[...]))
            for ref, val in zip(out_refs[4 * i:4 * i + 4], outs, strict=True):
                ref[...] = val

    args = [gathered[k] for k in names] + [lb, loss_parts] + [t for k in names for t in params[k]]
    res = pl.pallas_call(
        body, name="finish_vectors",
        out_shape=[jax.ShapeDtypeStruct(params[k][0].shape, F32) for k in names for _ in range(4)]
        + [jax.ShapeDtypeStruct((1, 1), F32)],
        compiler_params=_params(),
    )(*args)
    return {k: res[4 * i:4 * i + 4] for i, k in enumerate(names)}, res[-1].reshape(())


def _swap_halves(t):
    half = t.shape[-1] // 2
    return jnp.concatenate([t[..., half:], t[..., :half]], axis=-1)


def _pad_last(t, width):
    return jnp.concatenate([t, jnp.zeros(t.shape[:-1] + (width - t.shape[-1],), t.dtype)], axis=-1)


def _rope_tables(lp):
    pos = jnp.arange(lp, dtype=F32)
    inv_freq = 1.0 / (ROPE_THETA ** (jnp.arange(0, ROPE_DIM, 2, dtype=F32) / ROPE_DIM))
    ang = pos[:, None] * inv_freq[None, :]
    cos, sin = jnp.cos(ang), jnp.sin(ang)
    c128 = _pad_last(jnp.concatenate([cos, cos], axis=1), HEAD_DIM)
    s128 = _pad_last(jnp.concatenate([-sin, sin], axis=1), HEAD_DIM)
    return c128, s128


def _forward_backward(x, target, meta, w, small, *, lp, comm=None):
    batch, seq, d = x.shape
    rows = batch * lp
    tr = 272 if lp % 272 == 0 else 128
    tm = lp // 2
    bf = _MXU_DTYPE
    rw = functools.partial(_rowwise, rows=rows, tr=tr, lp=lp)

    c128, s128 = _rope_tables(lp)
    t_idx = jnp.arange(lp)
    real = jnp.broadcast_to(((t_idx >= N_META) & (t_idx < N_META + seq)).astype(F32)[:, None], (lp, _LANES))

    lb_logits = small["lb_logits"]
    lb = jax.nn.softmax(lb_logits, axis=0)[0:1]
    gh = jnp.tile(small["hg_norm_g"], (1, HEADS))

    h0 = _assemble("assemble_x", x, meta, lp)
    tgt = _assemble("assemble_target", target, jnp.zeros_like(meta), lp)

    (u1,), _ = rw("norm_mix_pre", lambda h, g: ([h * _rms_scale(h) * g], []),
                  ins=[("row", h0, d, 0), ("const", small["mix_pre_g"])], outs=[(d, bf)])
    p = _matmul("proj_in", u1, w["w_in"], out_dtype=F32, tm=tm, tn=1024, tk=1024)

    if comm is None:
        o_hg, z_a, states = _hgrn_fwd(p, lb, gh, batch=batch, lp=lp)
    else:
        o_hg, z_a, states, *gathered = _hgrn_fwd(p, lb, gh, batch=batch, lp=lp, ride=_Ride(comm.rest_payloads, True))
        w = {**w, **comm.rest_weights(gathered)}
    received = []
    scatter = lambda names: _Ride(comm.grad_parts(names, grads), False) if comm is not None else None
    y_a = _matmul("proj_hg_o", z_a, w["w_hg_o"], out_dtype=bf, tm=tm, tn=1024, tk=1024)

    def mla_pre(pc, gq, gkv, ct, st):
        cq, ckv = pc[:, 0:Q_LORA], pc[:, Q_LORA:Q_LORA + KV_LORA]
        kpe, kpe_sw = pc[:, 512:640], pc[:, 640:768]
        return [cq * _rms_scale(cq) * gq, ckv * _rms_scale(ckv) * gkv, kpe * ct + kpe_sw * st], []

    (cqn, ckvn, kp), _ = rw("mla_pre", mla_pre,
                            ins=[("row", p, 1024, CB_C), ("const", small["q_a_norm_g"]),
                                 ("const", small["kv_a_norm_g"]), ("pos", c128), ("pos", s128)],
                            outs=[(Q_LORA, bf), (KV_LORA, bf), (HEAD_DIM, bf)])
    q_cat = _proj_q_rope(cqn, w["w_q"], c128, s128, tm=tm, lp=lp)
    kv = _matmul("proj_kv_b", ckvn, w["w_kv"], out_dtype=bf, tm=tm, tn=1024, tk=KV_LORA)
    o_at, lse = _attn_fwd(q_cat, kv, kp, batch=batch, lp=lp)
    te, te_small = tm, lp // 4

    def merge(yb, pa, pb, ya, bg):
        ga, gb = _sigmoid(pa + bg[:, :d]), _sigmoid(pb + bg[:, d:])
        return [yb, ga * ya.astype(F32) + gb * yb], []

    (y_b, mix), _ = _matmul_segments(
        "proj_mla_o", [o_at], w["w_mla_o"], tm=te, tn=d, tk=1024,
        epilogue=_Epilogue(merge, rows=[(p, 1024, CB_GA), (p, 1024, CB_GB), y_a], consts=[small["b_gate"]],
                           outs=[(d, bf), (d, bf)], lp=lp))
    def post_mix(mx_, h, g2, g3):
        h1_ = h + mx_ * _rms_scale(mx_) * g2
        return [mx_, h1_, h1_ * _rms_scale(h1_) * g3], []

    (mixed, h1, u2), _ = _matmul_segments(
        "proj_out", [mix], w["w_out"], tm=te, tn=d, tk=1024,
        epilogue=_Epilogue(post_mix, rows=[h0], consts=[small["mix_post_g"], small["ffn_pre_g"]],
                           outs=[(d, F32), (d, F32), (d, bf)], lp=lp))
    act, act_dgate, act_dup = _ffn_in_swiglu(u2, w["w_ffn_in"], tm=tm, tn=1408)

    def post_ffn(fo_, h1_, t_, g4, mask):
        r = _rms_scale(fo_)
        h2 = h1_ + fo_ * r * g4
        err = (h2 - t_) * mask[:, 0:1]
        dh2 = err * (1.0 / d)
        dfo, dg4 = _rms_bwd(fo_, g4, dh2)
        return [dh2, dfo], [err * err, dg4]

    (dh2, dfo), (loss_vec, dg_ffn_post) = _matmul_segments(
        "ffn_out", [act], w["w_ffn_out"], tm=te, tn=d, tk=1408,
        epilogue=_Epilogue(post_ffn, rows=[h1, tgt], consts=[small["ffn_post_g"]], pos=[real],
                           outs=[(d, F32), (d, bf)], accs=[d, d], lp=lp))
    loss = (0.5 / d) * jnp.sum(loss_vec)

    grads = {}
    dw_dt = F32 if comm is None else _WIRE_DTYPE
    dgt, dup = _d_ffn_out_swiglu(dfo, w["w_ffn_out"], act_dgate, act_dup, tm=tm, tn=1408)
    grads["w_ffn_out"] = _matmul_tn("dw_ffn_out", act, dfo, tk=1408, tn=1024, tr=tm, out_dtype=dw_dt)
    grads["w_ffn_in"] = jnp.concatenate([_matmul_tn("dw_ffn_in_gate", u2, dgt, tk=1024, tn=FFN_HIDDEN, tr=tm),
                                         _matmul_tn("dw_ffn_in_up", u2, dup, tk=1024, tn=FFN_HIDDEN, tr=tm)], axis=1)

    def post_mix_bwd(du2_, h1_, dh2_, mx_, g3, g2):
        dx, dg3 = _rms_bwd(h1_, g3, du2_)
        dh1_ = dh2_ + dx
        dmx, dg2 = _rms_bwd(mx_, g2, dh1_)
        return [dh1_, dmx], [dg3, dg2]

    (dh1, dmixed), (dg_ffn_pre, dg_mix_post) = _matmul_segments(
        "d_ffn_in", [dgt, dup], w["w_ffn_in"], tm=te_small, tn=d, tk=1408, b_transposed=True,
        epilogue=_Epilogue(post_mix_bwd, rows=[h1, dh2, mixed], consts=[small["ffn_pre_g"], small["mix_post_g"]],
                           outs=[(d, F32), (d, bf)], accs=[d, d], lp=lp))
    grads["w_out"] = _matmul_tn("dw_out", mix, dmixed, tk=1024, tn=1024, tr=tm, out_dtype=dw_dt)

    def merge_bwd(dm, pa, pb, ya, yb, bg):
        ya, yb = ya.astype(F32), yb.astype(F32)
        ga, gb = _sigmoid(pa + bg[:, :d]), _sigmoid(pb + bg[:, d:])
        dpg = jnp.concatenate([dm * ya * ga * (1.0 - ga), dm * yb * gb * (1.0 - gb)], axis=1)
        return [dpg, dm * ga, dm * gb], [dpg]

    (dpg, dya, dyb), (db_gate,) = _matmul_segments(
        "d_proj_out", [dmixed], w["w_out"], tm=te_small, tn=d, tk=1024, b_transposed=True,
        epilogue=_Epilogue(merge_bwd, rows=[(p, 1024, CB_GA), (p, 1024, CB_GB), y_a, y_b], consts=[small["b_gate"]],
                           outs=[(2 * d, bf), (d, bf), (d, bf)], accs=[2 * d], lp=lp))
    dz_a = _matmul("d_proj_hg_o", dya, w["w_hg_o"], out_dtype=F32, tm=tm, tn=1024, tk=1024, b_transposed=True)
    grads["w_hg_o"] = _matmul_tn("dw_hg_o", z_a, dya, tk=1024, tn=1024, tr=tm, out_dtype=dw_dt)
    do_at = _matmul("d_proj_mla_o", dyb, w["w_mla_o"], out_dtype=bf, tm=tm, tn=1024, tk=1024, b_transposed=True)
    grads["w_mla_o"] = _matmul_tn("dw_mla_o", o_at, dyb, tk=1024, tn=1024, tr=tm, out_dtype=dw_dt)

    dph, dlb, dgh, *got = _hgrn_bwd(p, o_hg, dz_a, states, lb, gh, batch=batch, lp=lp,
                                    ride=scatter(_GRAD_GROUPS[0]))
    received.append(got)

    res = _attn_bwd(q_cat, kv, kp, do_at, o_at, lse, c128, s128, batch=batch, lp=lp, ride=scatter(_GRAD_GROUPS[1]))
    dq_parts, (dkn, dkp, dvv) = list(res[:3]), res[3:6]
    received.append(list(res[6:]))
    dcqn = _matmul_segments("d_proj_q_b", dq_parts, w["w_q"], tm=tm, tn=Q_LORA, tk=1024, b_transposed=True)
    grads["w_q"] = jnp.concatenate([_matmul_tn(f"dw_q_b_{i}", cqn, part, tk=Q_LORA, tn=1024, tr=tm)
                                    for i, part in enumerate(dq_parts)], axis=1)
    dckvn = _matmul_segments("d_proj_kv_b", [dkn, dvv], w["w_kv"], tm=tm, tn=KV_LORA, tk=1024, b_transposed=True)
    grads["w_k"] = _matmul_tn("dw_k_b", ckvn, dkn, tk=KV_LORA, tn=1024, tr=tm)
    grads["w_v"] = _matmul_tn("dw_v_b", ckvn, dvv, tk=KV_LORA, tn=1024, tr=tm)

    def mla_pre_bwd(pc, dq_, dkv_, dkp_, gq, gkv, ct, st):
        cq, ckv = pc[:, 0:Q_LORA], pc[:, Q_LORA:Q_LORA + KV_LORA]
        dcq, dgq = _rms_bwd(cq, gq, dq_)
        dckv, dgkv = _rms_bwd(ckv, gkv, dkv_)
        dpc = jnp.concatenate([dcq, dckv, dkp_ * ct, dkp_ * st, jnp.zeros((pc.shape[0], 256), F32)], axis=1)
        return [dpc], [dgq, dgkv]

    (dpc,), (dg_q, dg_kv) = rw(
        "mla_pre_bwd", mla_pre_bwd,
        ins=[("row", p, 1024, CB_C), ("row", dcqn, Q_LORA, 0), ("row", dckvn, KV_LORA, 0), ("row", dkp, HEAD_DIM, 0),
             ("const", small["q_a_norm_g"]), ("const", small["kv_a_norm_g"]), ("pos", c128), ("pos", s128)],
        outs=[(1024, bf)], accs=[Q_LORA, KV_LORA])

    grads["w_in"] = (_matmul_tn("dw_in_h", u1, dph, tk=1024, tn=1024, tr=tm),
                     _matmul_tn("dw_in_c", u1, dpc, tk=1024, tn=1024, tr=tm),
                     _matmul_tn("dw_in_g", u1, dpg, tk=1024, tn=1024, tr=tm))
    def pre_bwd(du, h, dh, g1):
        dx, dg1 = _rms_bwd(h, g1, du)
        return [dh + dx], [dg1]

    (dh0,), (dg_mix_pre,), *got = _matmul_segments(
        "d_proj_in", [dph, dpc, dpg], w["w_in"], tm=te, tn=d, tk=1024, b_transposed=True,
        ride=scatter(_GRAD_GROUPS[2]),
        epilogue=_Epilogue(pre_bwd, rows=[h0, dh1], consts=[small["mix_pre_g"]], outs=[(d, F32)], accs=[d], lp=lp))
    if comm is not None:
        received.append(got)
    grad_x = dh0.reshape(batch, lp, d)[:, N_META:N_META + seq]
    partial = {"meta_tokens": _meta_grad(dh0, batch, lp), "lb_logits": dlb, "b_gate": db_gate, "hg_norm_g": dgh,
               "q_a_norm_g": dg_q, "kv_a_norm_g": dg_kv, "mix_pre_g": dg_mix_pre, "mix_post_g": dg_mix_post,
               "ffn_pre_g": dg_ffn_pre, "ffn_post_g": dg_ffn_post, "loss": loss_vec}
    return loss, grad_x, grads, partial, lb, received


_BIG = ["w_in", "w_hg_o", "w_q_b", "w_kv_b", "w_mla_o", "w_out", "w_ffn_in", "w_ffn_out"]
_COLUMN_SHARDED = {"w_in", "w_q_b", "w_kv_b", "w_ffn_in"}
_GRAD_GROUPS = [["w_ffn_in", "w_ffn_out"], ["w_out", "w_hg_o", "w_mla_o"], ["w_in", "w_q_b", "w_kv_b"]]
_SMALL = ["b_gate", "lb_logits", "hg_norm_g", "q_a_norm_g", "kv_a_norm_g", "mix_pre_g", "mix_post_g",
          "ffn_pre_g", "ffn_post_g"]


def _gathered_matrix(name, t):
    _, k, n = t.shape
    if name in _COLUMN_SHARDED:
        return t.transpose(1, 0, 2).reshape(k, N_DEV * n)
    return t.reshape(N_DEV * k, n)


def _scatter_layout(name, full):
    kk, nn = full.shape
    if name in _COLUMN_SHARDED:
        t = full.reshape(kk, N_DEV, nn // N_DEV).transpose(1, 0, 2)
    else:
        t = full.reshape(N_DEV, kk // N_DEV, nn)
    return t.astype(_WIRE_DTYPE)


def _model_w_in(wi):
    z = lambda *s: jnp.zeros(s, wi.dtype)
    kpe = wi[:, 4608:4672]
    c_blk = jnp.concatenate([wi[:, 4096:4608], kpe, z(1024, 64), _swap_halves(kpe), z(1024, 64), z(1024, 256)], axis=1)
    return {"w_in": jnp.concatenate([wi[:, :4096], c_blk, wi[:, 4672:]], axis=1).astype(_MXU_DTYPE)}


def _model_weights(full):
    return {**_model_w_in(full["w_in"]), **_model_rest(full)}


def _model_rest(full):
    wq3 = full["w_q_b"].reshape(Q_LORA, HEADS, HEAD_DIM + ROPE_DIM)
    pe = wq3[:, :, HEAD_DIM:]
    w_q = jnp.concatenate([wq3[:, :, :HEAD_DIM].reshape(Q_LORA, -1),
                           _pad_last(pe, HEAD_DIM).reshape(Q_LORA, -1),
                           _pad_last(_swap_halves(pe), HEAD_DIM).reshape(Q_LORA, -1)], axis=1)
    wkv3 = full["w_kv_b"].reshape(KV_LORA, HEADS, 2 * HEAD_DIM)
    w_k = wkv3[:, :, :HEAD_DIM].reshape(KV_LORA, -1)
    w_v = wkv3[:, :, HEAD_DIM:].reshape(KV_LORA, -1)
    w = {"w_q": w_q, "w_kv": jnp.concatenate([w_k, w_v], axis=1),
         "w_hg_o": full["w_hg_o"], "w_mla_o": full["w_mla_o"], "w_out": full["w_out"],
         "w_ffn_in": full["w_ffn_in"], "w_ffn_out": full["w_ffn_out"]}
    return {k: v.astype(_MXU_DTYPE) for k, v in w.items()}


def _reference_layout_grad(name, g):
    if name == "w_in":
        g_h, g_c, g_g = g["w_in"]
        d_kpe = g_c[:, 512:576] + _swap_halves(g_c[:, 640:704])
        return jnp.concatenate([g_h, g_c[:, :512], d_kpe, g_g], axis=1)
    if name == "w_q_b":
        gq = g["w_q"]
        d_pe = (gq[:, 1024:2048].reshape(Q_LORA, HEADS, HEAD_DIM)[:, :, :ROPE_DIM]
                + _swap_halves(gq[:, 2048:].reshape(Q_LORA, HEADS, HEAD_DIM)[:, :, :ROPE_DIM]))
        return jnp.concatenate([gq[:, :1024].reshape(Q_LORA, HEADS, HEAD_DIM), d_pe], axis=2).reshape(Q_LORA, -1)
    if name == "w_kv_b":
        return jnp.concatenate([g["w_k"].reshape(KV_LORA, HEADS, HEAD_DIM),
                                g["w_v"].reshape(KV_LORA, HEADS, HEAD_DIM)], axis=2).reshape(KV_LORA, -1)
    return g[name]


def _reference_layout_grads(g):
    return {n: _reference_layout_grad(n, g) for n in _BIG}


class _Comm:
    def __init__(self, shard):
        self.rest_payloads = [shard[n].astype(_WIRE_DTYPE) for n in _BIG[1:]]

    def rest_weights(self, gathered):
        return _model_rest({n: _gathered_matrix(n, t) for n, t in zip(_BIG[1:], gathered, strict=True)})

    def grad_parts(self, names, g):
        return [_scatter_layout(n, _reference_layout_grad(n, g)) for n in names]


def kernel(x, meta_tokens, w_in, b_gate, lb_logits, hg_norm_g, w_hg_o, q_a_norm_g, w_q_b, kv_a_norm_g, w_kv_b, w_mla_o, w_out, mix_pre_g, mix_post_g, ffn_pre_g, ffn_post_g, w_ffn_in, w_ffn_out, loss_target, m_meta_tokens, m_w_in, m_b_gate, m_lb_logits, m_hg_norm_g, m_w_hg_o, m_q_a_norm_g, m_w_q_b, m_kv_a_norm_g, m_w_kv_b, m_w_mla_o, m_w_out, m_mix_pre_g, m_mix_post_g, m_ffn_pre_g, m_ffn_post_g, m_w_ffn_in, m_w_ffn_out, v_meta_tokens, v_w_in, v_b_gate, v_lb_logits, v_hg_norm_g, v_w_hg_o, v_q_a_norm_g, v_w_q_b, v_kv_a_norm_g, v_w_kv_b, v_w_mla_o, v_w_out, v_mix_pre_g, v_mix_post_g, v_ffn_pre_g, v_ffn_post_g, v_w_ffn_in, v_w_ffn_out):
    args = dict(locals())
    batch, seq, d = x.shape
    lp = -(-(N_META + seq) // _LANES) * _LANES
    weight_names = ["meta_tokens", "w_in", "b_gate", "lb_logits", "hg_norm_g", "w_hg_o", "q_a_norm_g", "w_q_b",
                    "kv_a_norm_g", "w_kv_b", "w_mla_o", "w_out", "mix_pre_g", "mix_post_g", "ffn_pre_g",
                    "ffn_post_g", "w_ffn_in", "w_ffn_out"]
    shard = {n: args[n].reshape(args[n].shape[-2:]) for n in _BIG}
    comm = _Comm(shard)

    w_in_all, meta_all = _all_gather("gather_first", [shard["w_in"].astype(_WIRE_DTYPE), meta_tokens])
    w_first = _model_w_in(_gathered_matrix("w_in", w_in_all))
    meta_full = meta_all.transpose(1, 0, 2).reshape(N_META, d)
    small = {n: args[n] for n in _SMALL}

    _, grad_x, _, partial, lb, received = _forward_backward(x, loss_target, meta_full, w_first, small, lp=lp, comm=comm)
    out = {}
    for names, bufs in zip(_GRAD_GROUPS, received, strict=True):
        for n, buf in zip(names, bufs, strict=True):
            two_d = lambda t: t.reshape(t.shape[-2:])
            res = _sum_adamw("adamw_" + n, buf, shard[n], two_d(args["m_" + n]), two_d(args["v_" + n]))
            out[n] = [t.reshape(args[n].shape) for t in res]

    vec_names = _SMALL + ["meta_tokens"]
    *gathered, loss_parts = _all_gather("gather_vectors", [partial[n] for n in vec_names + ["loss"]])
    finished, loss = _finish_vectors(dict(zip(vec_names, gathered, strict=True)), lb,
                                     {n: (args[n], args["m_" + n], args["v_" + n]) for n in vec_names}, loss_parts)
    out.update(finished)
    return (loss, grad_x, *[out[n][i] for i in range(4) for n in weight_names])
```

```python
import functools

import jax
import jax.numpy as jnp
from jax import lax
from jax.experimental import pallas as pl
from jax.experimental.pallas import tpu as pltpu

F32 = jnp.float32
_MXU_DTYPE = jnp.bfloat16
_WIRE_DTYPE = jnp.bfloat16
_VMEM_LIMIT_BYTES = 56 * 1024 * 1024
_LANES = 128
_SUBLANES = 8

N_DEV = 8
N_META = 16
NORM_EPS = 1e-6
HEADS = 8
HEAD_DIM = 128
ROPE_DIM = 64
HG_CHUNK = 16
HG_BLOCK = 128
ROPE_THETA = 10000.0
D_MODEL = 1024
Q_LORA = 256
KV_LORA = 256
FFN_HIDDEN = 2816
ATTN_SCALE = (HEAD_DIM + ROPE_DIM) ** -0.5
NEG_BIG = -1e30

ADAM_LR = 0.001
ADAM_B1 = 0.9
ADAM_B2 = 0.999
ADAM_EPS = 1e-08
ADAM_WD = 0.01
ADAM_STEP = 10

CB_HQ, CB_HF, CB_HI, CB_HG, CB_C, CB_GA, CB_GB = range(7)
IN_COLS_PADDED = 7 * 1024


def _params(**kw):
    return pltpu.CompilerParams(vmem_limit_bytes=_VMEM_LIMIT_BYTES, **kw)


def _dot(a, b):
    return lax.dot_general(a, b, (((1,), (0,)), ((), ())), preferred_element_type=F32)


def _dot_nt(a, b):
    return lax.dot_general(a, b, (((1,), (1,)), ((), ())), preferred_element_type=F32)


def _dot_tn(a, b):
    return lax.dot_general(a, b, (((0,), (0,)), ((), ())), preferred_element_type=F32)


def _mx(x):
    return x.astype(_MXU_DTYPE)


def _exact_dot(m01, x, dot=_dot):
    if _MXU_DTYPE == jnp.float32:
        return dot(m01.astype(F32), x)
    m = m01.astype(jnp.bfloat16)
    x1 = x.astype(jnp.bfloat16)
    x2 = (x - x1.astype(F32)).astype(jnp.bfloat16)
    return dot(m, x1) + dot(m, x2)


def _exact_dot_nt(m01, x):
    return _exact_dot(m01, x, dot=_dot_nt)


def _sigmoid(x):
    return jax.nn.sigmoid(x)


def _silu_grad(x, s):
    return s * (1.0 + x * (1.0 - s))


def _rms_scale(x):
    return lax.rsqrt(jnp.mean(x * x, axis=-1, keepdims=True) + NORM_EPS)


def _rms_bwd(x, g, dy):
    r = _rms_scale(x)
    xh = x * r
    w = dy * g
    dx = r * (w - xh * jnp.mean(xh * w, axis=-1, keepdims=True))
    return dx, dy * xh


def _heads(fn, *arrays):
    outs = [fn(*[a[:, h * HEAD_DIM:(h + 1) * HEAD_DIM] for a in arrays]) for h in range(HEADS)]
    if isinstance(outs[0], tuple):
        return tuple(jnp.concatenate([o[i] for o in outs], axis=1) for i in range(len(outs[0])))
    return jnp.concatenate(outs, axis=1)


class _Ride:
    def __init__(self, payloads, gather):
        self.gather, self.args, self.n = gather, list(payloads), len(payloads)
        self.in_specs = [pl.BlockSpec(memory_space=pl.ANY)] * self.n
        self.out_shape = [jax.ShapeDtypeStruct((N_DEV, *p.shape[-2:]), p.dtype) for p in payloads]
        self.out_specs = [pl.BlockSpec(memory_space=pl.ANY)] * self.n
        self.scratch = [pltpu.SemaphoreType.DMA((self.n, N_DEV - 1)), pltpu.SemaphoreType.DMA((self.n, N_DEV - 1)),
                        pltpu.SemaphoreType.DMA((self.n,))]

    def split(self, rest, n_outs):
        n = self.n
        mine = (rest[:n], rest[n + n_outs:2 * n + n_outs], rest[-3:])
        return rest[n:n + n_outs] + rest[2 * n + n_outs:-3], mine

    def _copies(self, p_refs, out_refs, sems):
        send_sems, recv_sems, local_sems = sems
        x, y, c = lax.axis_index("x"), lax.axis_index("y"), lax.axis_index("c")
        me = 4 * x + 2 * y + c
        copies = []
        for i, (p_ref, out_ref) in enumerate(zip(p_refs, out_refs, strict=True)):
            part = (lambda j, p_ref=p_ref: p_ref) if self.gather else (lambda j, p_ref=p_ref: p_ref.at[j])
            copies.append(pltpu.make_async_copy(part(me), out_ref.at[me], local_sems.at[i]))
            for k in range(1, N_DEV):
                px, py, pc = x ^ (k >> 2), y ^ ((k >> 1) & 1), c ^ (k & 1)
                copies.append(pltpu.make_async_remote_copy(
                    src_ref=part(4 * px + 2 * py + pc), dst_ref=out_ref.at[me],
                    send_sem=send_sems.at[i, k - 1], recv_sem=recv_sems.at[i, k - 1],
                    device_id=(px, py, pc), device_id_type=pl.DeviceIdType.MESH))
        return copies

    def run(self, grid, refs):
        ids = [pl.program_id(i) for i in range(len(grid))]
        first = functools.reduce(jnp.logical_and, [i == 0 for i in ids])
        last = functools.reduce(jnp.logical_and, [i == g - 1 for i, g in zip(ids, grid)])

        @pl.when(first)
        def _():
            for cp in self._copies(*refs):
                cp.start()

        @pl.when(last)
        def _():
            for cp in self._copies(*refs):
                cp.wait()


class _NoRide:
    in_specs, out_shape, out_specs, scratch, args = [], [], [], [], []


def _matmul(name, a, b, *, out_dtype, tm, tn, tk, c_in=None, ride=None, b_transposed=False):
    m, k = a.shape
    n = b.shape[0] if b_transposed else b.shape[1]
    assert m % tm == 0 and n % tn == 0 and k % tk == 0, (name, a.shape, b.shape, tm, tn, tk)
    nk = k // tk
    has_c = c_in is not None
    dot = _dot_nt if b_transposed else _dot
    grid = (n // tn, m // tm, nk)
    n_in = 2 + has_c

    def body(*refs):
        a_ref, b_ref = refs[0], refs[1]
        c_ref = refs[2] if has_c else None
        rest = refs[n_in:]
        if ride is not None:
            rest, exchange = ride.split(rest, 1)
            ride.run(grid, exchange)
        o_ref = rest[0]
        acc_ref = rest[1] if nk > 1 else None

        def finish(r):
            if has_c:
                r = r + c_ref[...]
            o_ref[...] = r.astype(o_ref.dtype)

        if nk == 1:
            finish(dot(a_ref[...], b_ref[...]))
        else:
            kk = pl.program_id(2)

            @pl.when(kk == 0)
            def _():
                acc_ref[...] = jnp.zeros_like(acc_ref)

            acc_ref[...] += dot(a_ref[...], b_ref[...])

            @pl.when(kk == nk - 1)
            def _():
                finish(acc_ref[...])

    in_specs = [pl.BlockSpec((tm, tk), lambda j, i, kk: (i, kk)),
                pl.BlockSpec((tn, tk), lambda j, i, kk: (j, kk)) if b_transposed
                else pl.BlockSpec((tk, tn), lambda j, i, kk: (kk, j))]
    args = [a, b]
    aliases = {}
    if has_c:
        in_specs.append(pl.BlockSpec((tm, tn), lambda j, i, kk: (i, j)))
        args.append(c_in)
        aliases = {2: 0}
    out_shape = [jax.ShapeDtypeStruct((m, n), out_dtype)]
    out_specs = [pl.BlockSpec((tm, tn), lambda j, i, kk: (i, j))]
    scratch = [pltpu.VMEM((tm, tn), F32)] if nk > 1 else []
    if ride is not None:
        in_specs, args = in_specs + ride.in_specs, args + ride.args
        out_shape, out_specs, scratch = out_shape + ride.out_shape, out_specs + ride.out_specs, scratch + ride.scratch
    res = pl.pallas_call(
        body, name=name, out_shape=out_shape, grid=grid, in_specs=in_specs, out_specs=out_specs,
        scratch_shapes=scratch, input_output_aliases=aliases, compiler_params=_params(),
    )(*args)
    return res[0] if ride is None else res


EPILOGUE_ROWS = 272


class _Epilogue:
    def __init__(self, fn, *, rows=(), consts=(), pos=(), outs=(), accs=(), lp=None):
        self.fn, self.rows, self.consts, self.pos = fn, list(rows), list(consts), list(pos)
        self.outs, self.accs, self.lp = list(outs), list(accs), lp


def _matmul_segments(name, a_list, b, *, out_dtype=F32, tm, tn, tk, ride=None, b_transposed=False, epilogue=None):
    m = a_list[0].shape[0]
    n, k = b.shape if b_transposed else b.shape[::-1]
    steps = [a.shape[1] // tk for a in a_list]
    offs = [sum(steps[:s]) for s in range(len(steps))]
    nk = sum(steps)
    assert nk * tk == k and m % tm == 0 and n % tn == 0 and all(a.shape[1] % tk == 0 for a in a_list), name
    grid = (n // tn, m // tm, nk)
    n_seg = len(a_list)
    dot = _dot_nt if b_transposed else _dot
    ep = epilogue
    assert ep is None or tn == n, name
    n_extra = 0 if ep is None else len(ep.rows) + len(ep.consts) + len(ep.pos)
    n_outs = 1 if ep is None else len(ep.outs) + len(ep.accs)

    def body(*refs):
        a_refs, b_ref = refs[:n_seg], refs[n_seg]
        extra_refs, rest = refs[n_seg + 1:n_seg + 1 + n_extra], refs[n_seg + 1 + n_extra:]
        if ride is not None:
            rest, exchange = ride.split(rest, n_outs)
            ride.run(grid, exchange)
        out_refs, acc_ref = rest[:n_outs], rest[n_outs]
        i, kk = pl.program_id(1), pl.program_id(2)

        @pl.when(kk == 0)
        def _():
            acc_ref[...] = jnp.zeros_like(acc_ref)

        for s in range(n_seg):
            @pl.when((kk >= offs[s]) & (kk < offs[s] + steps[s]))
            def _(s=s):
                acc_ref[...] += dot(a_refs[s][...], b_ref[...])

        if ep is None:
            @pl.when(kk == nk - 1)
            def _():
                out_refs[0][...] = acc_ref[...].astype(out_refs[0].dtype)
        else:
            sum_refs = out_refs[len(ep.outs):]

            @pl.when((kk == 0) & (i == 0))
            def _():
                for ref in sum_refs:
                    ref[...] = jnp.zeros_like(ref)

            @pl.when(kk == nk - 1)
            def _():
                rs = EPILOGUE_ROWS if tm % EPILOGUE_ROWS == 0 else tm
                n_r, n_c = len(ep.rows), len(ep.consts)
                for r0 in range(0, tm, rs):
                    sl = slice(r0, r0 + rs)
                    tiles = ([r[sl, :] for r in extra_refs[:n_r]] + [c[...] for c in extra_refs[n_r:n_r + n_c]]
                             + [t[sl, :] for t in extra_refs[n_r + n_c:]])
                    res_outs, res_sums = ep.fn(acc_ref[sl, :], *tiles)
                    for ref, val in zip(out_refs[:len(ep.outs)], res_outs, strict=True):
                        ref[sl, :] = val.astype(ref.dtype)
                    for ref, val in zip(sum_refs, res_sums, strict=True):
                        ref[...] += val.reshape(rs // _SUBLANES, _SUBLANES, val.shape[-1]).sum(axis=0)

    seg_spec = lambda s: pl.BlockSpec(
        (tm, tk), functools.partial(lambda j, i, kk, off, ns: (i, jnp.clip(kk - off, 0, ns - 1)), off=offs[s], ns=steps[s]))
    b_spec = (pl.BlockSpec((tn, tk), lambda j, i, kk: (j, kk)) if b_transposed
              else pl.BlockSpec((tk, tn), lambda j, i, kk: (kk, j)))
    in_specs = [seg_spec(s) for s in range(n_seg)] + [b_spec]
    args = list(a_list) + [b]
    row_spec = lambda w: pl.BlockSpec((tm, w), lambda j, i, kk: (i, 0))
    if ep is None:
        out_shape = [jax.ShapeDtypeStruct((m, n), out_dtype)]
        out_specs = [pl.BlockSpec((tm, tn), lambda j, i, kk: (i, j))]
    else:
        tiles_per_example = ep.lp // tm
        row_ins = [r if isinstance(r, tuple) else (r, r.shape[1], 0) for r in ep.rows]
        in_specs += ([pl.BlockSpec((tm, wd), functools.partial(lambda j, i, kk, cb: (i, cb), cb=cb)) for _, wd, cb in row_ins]
                     + [pl.BlockSpec(c.shape, lambda j, i, kk: (0, 0)) for c in ep.consts]
                     + [pl.BlockSpec((tm, p.shape[1]), lambda j, i, kk: (i % tiles_per_example, 0)) for p in ep.pos])
        args += [arr for arr, _, _ in row_ins] + ep.consts + ep.pos
        out_shape = ([jax.ShapeDtypeStruct((m, w), dt) for w, dt in ep.outs]
                     + [jax.ShapeDtypeStruct((_SUBLANES, w), F32) for w in ep.accs])
        out_specs = ([row_spec(w) for w, _ in ep.outs]
                     + [pl.BlockSpec((_SUBLANES, w), lambda j, i, kk: (0, 0)) for w in ep.accs])
    scratch = [pltpu.VMEM((tm, tn), F32)]
    if ride is not None:
        in_specs, args = in_specs + ride.in_specs, args + ride.args
        out_shape, out_specs, scratch = out_shape + ride.out_shape, out_specs + ride.out_specs, scratch + ride.scratch
    res = pl.pallas_call(
        body, name=name, out_shape=out_shape, grid=grid, in_specs=in_specs, out_specs=out_specs,
        scratch_shapes=scratch, compiler_params=_params(),
    )(*args)
    if ep is None:
        return res[0] if ride is None else res
    n_o = len(ep.outs)
    return (res[:n_o], res[n_o:n_outs], *res[n_outs:])


def _matmul_tn(name, x, dy, *, tk, tn, tr, out_dtype=F32):
    r, k = x.shape
    _, n = dy.shape
    assert r % tr == 0 and k % tk == 0 and n % tn == 0, (name, x.shape, dy.shape)
    n_r = r // tr
    direct = out_dtype == F32

    def body(x_ref, dy_ref, o_ref, *scratch):
        acc_ref = o_ref if direct else scratch[0]

        @pl.when(pl.program_id(2) == 0)
        def _():
            acc_ref[...] = jnp.zeros_like(acc_ref)

        acc_ref[...] += _dot_tn(x_ref[...], dy_ref[...])
        if not direct:
            @pl.when(pl.program_id(2) == n_r - 1)
            def _():
                o_ref[...] = acc_ref[...].astype(o_ref.dtype)

    return pl.pallas_call(
        body, name=name,
        out_shape=jax.ShapeDtypeStruct((k, n), out_dtype),
        grid=(k // tk, n // tn, n_r),
        in_specs=[pl.BlockSpec((tr, tk), lambda kb, nb, rr: (rr, kb)),
                  pl.BlockSpec((tr, tn), lambda kb, nb, rr: (rr, nb))],
        out_specs=pl.BlockSpec((tk, tn), lambda kb, nb, rr: (kb, nb)),
        scratch_shapes=[] if direct else [pltpu.VMEM((tk, tn), F32)],
        compiler_params=_params(),
    )(x, dy)


def _ffn_in_swiglu(u, w, *, tm, tn):
    r, k = u.shape
    h = w.shape[1] // 2
    assert r % tm == 0 and h % tn == 0
    nj = h // tn

    def body(u_ref, wg_ref, wu_ref, act_ref, dgate_ref, dup_ref):
        uu = u_ref[...]
        gt, up = _dot(uu, wg_ref[...]), _dot(uu, wu_ref[...])
        s = _sigmoid(gt)
        silu = gt * s
        act_ref[...] = (silu * up).astype(act_ref.dtype)
        dgate_ref[...] = (up * _silu_grad(gt, s)).astype(dgate_ref.dtype)
        dup_ref[...] = silu.astype(dup_ref.dtype)

    tile = pl.BlockSpec((tm, tn), lambda j, i: (i, j))
    return pl.pallas_call(
        body, name="ffn_in_swiglu",
        out_shape=[jax.ShapeDtypeStruct((r, h), _MXU_DTYPE)] * 3,
        grid=(nj, r // tm),
        in_specs=[pl.BlockSpec((tm, k), lambda j, i: (i, 0)),
                  pl.BlockSpec((k, tn), lambda j, i: (0, j)),
                  pl.BlockSpec((k, tn), lambda j, i: (0, nj + j))],
        out_specs=[tile] * 3,
        compiler_params=_params(),
    )(u, w, w)


def _d_ffn_out_swiglu(dy, w, act_dgate, act_dup, *, tm, tn):
    r, k = dy.shape
    h = w.shape[0]
    assert r % tm == 0 and h % tn == 0

    def body(dy_ref, w_ref, pg_ref, pu_ref, dgt_ref, dup_ref):
        da = _dot_nt(dy_ref[...], w_ref[...])
        dgt_ref[...] = (da * pg_ref[...].astype(F32)).astype(dgt_ref.dtype)
        dup_ref[...] = (da * pu_ref[...].astype(F32)).astype(dup_ref.dtype)

    tile = pl.BlockSpec((tm, tn), lambda j, i: (i, j))
    return pl.pallas_call(
        body, name="d_ffn_out_swiglu",
        out_shape=[jax.ShapeDtypeStruct((r, h), _MXU_DTYPE)] * 2,
        grid=(h // tn, r // tm),
        in_specs=[pl.BlockSpec((tm, k), lambda j, i: (i, 0)), pl.BlockSpec((tn, k), lambda j, i: (j, 0)), tile, tile],
        out_specs=[tile] * 2,
        compiler_params=_params(),
    )(dy, w, act_dgate, act_dup)


def _proj_q_rope(cqn, w_q, c_tab, s_tab, *, tm, lp):
    r, k = cqn.shape
    tiles_per_example = lp // tm
    pair = 2 * HEAD_DIM

    def body(x_ref, wn_ref, wp_ref, ws_ref, c_ref, s_ref, o_ref):
        x = x_ref[...]
        c2, s2 = jnp.tile(c_ref[...], (1, 2)), jnp.tile(s_ref[...], (1, 2))
        nope = _dot(x, wn_ref[...])
        roped = _dot(x, wp_ref[...]) * c2 + _dot(x, ws_ref[...]) * s2
        hs = lambda t, h: t[:, h * HEAD_DIM:(h + 1) * HEAD_DIM]
        o_ref[...] = jnp.concatenate([hs(nope, 0), hs(roped, 0), hs(nope, 1), hs(roped, 1)], axis=1).astype(o_ref.dtype)

    w_blk = lambda part: pl.BlockSpec((k, pair), functools.partial(lambda h, i, part: (0, part * (HEADS // 2) + h), part=part))
    tab = pl.BlockSpec((tm, HEAD_DIM), lambda h, i: (i % tiles_per_example, 0))
    return pl.pallas_call(
        body, name="proj_q_rope",
        out_shape=jax.ShapeDtypeStruct((r, HEADS * QK_DIM), _MXU_DTYPE),
        grid=(HEADS // 2, r // tm),
        in_specs=[pl.BlockSpec((tm, k), lambda h, i: (i, 0)), w_blk(0), w_blk(1), w_blk(2), tab, tab],
        out_specs=pl.BlockSpec((tm, 2 * QK_DIM), lambda h, i: (i, h)),
        compiler_params=_params(),
    )(cqn, w_q, w_q, w_q, c_tab, s_tab)


def _rowwise(name, body, *, rows, tr, lp, ins, outs, accs=()):
    assert rows % tr == 0 and lp % tr == 0 and tr % 16 == 0
    tiles_per_example = lp // tr
    in_specs, arrays = [], []
    for spec in ins:
        if spec[0] == "row":
            _, arr, width, cb = spec
            in_specs.append(pl.BlockSpec((tr, width), functools.partial(lambda i, cb: (i, cb), cb=cb)))
        elif spec[0] == "const":
            arr = spec[1]
            in_specs.append(pl.BlockSpec(arr.shape, lambda i: (0, 0)))
        else:
            arr = spec[1]
            in_specs.append(pl.BlockSpec((tr, arr.shape[1]), lambda i: (i % tiles_per_example, 0)))
        arrays.append(arr)
    n_in, n_out = len(ins), len(outs)

    def kern(*refs):
        res_outs, res_accs = body(*[r[...] for r in refs[:n_in]])
        for ref, val in zip(refs[n_in:n_in + n_out], res_outs, strict=True):
            ref[...] = val.astype(ref.dtype)
        acc_refs = refs[n_in + n_out:]
        if acc_refs:
            @pl.when(pl.program_id(0) == 0)
            def _():
                for ref in acc_refs:
                    ref[...] = jnp.zeros_like(ref)

            for ref, val in zip(acc_refs, res_accs, strict=True):
                ref[...] += val.reshape(tr // _SUBLANES, _SUBLANES, val.shape[-1]).sum(axis=0)

    out_shape = ([jax.ShapeDtypeStruct((rows, w), dt) for w, dt in outs]
                 + [jax.ShapeDtypeStruct((_SUBLANES, w), F32) for w in accs])
    out_specs = ([pl.BlockSpec((tr, w), lambda i: (i, 0)) for w, _ in outs]
                 + [pl.BlockSpec((_SUBLANES, w), lambda i: (0, 0)) for w in accs])
    res = pl.pallas_call(
        kern, name=name, out_shape=out_shape, grid=(rows // tr,),
        in_specs=in_specs, out_specs=out_specs, compiler_params=_params(),
    )(*arrays)
    return res[:n_out], list(res[n_out:])


def _assemble(name, x, head_rows, lp):
    batch, seq, d = x.shape
    tc = 256

    def body(x_ref, m_ref, o_ref):
        o_ref[0:N_META, :] = m_ref[...]
        o_ref[N_META:N_META + seq, :] = x_ref[0]
        if lp > N_META + seq:
            o_ref[N_META + seq:, :] = jnp.zeros((lp - N_META - seq, tc), F32)

    return pl.pallas_call(
        body, name=name,
        out_shape=jax.ShapeDtypeStruct((batch * lp, d), F32),
        grid=(batch, d // tc),
        in_specs=[pl.BlockSpec((1, seq, tc), lambda b, j: (b, 0, j)),
                  pl.BlockSpec((N_META, tc), lambda b, j: (0, j))],
        out_specs=pl.BlockSpec((lp, tc), lambda b, j: (b, j)),
        compiler_params=_params(),
    )(x, head_rows)


def _meta_grad(dh0, batch, lp):
    d = dh0.shape[1]

    def body(g_ref, o_ref):
        @pl.when(pl.program_id(0) == 0)
        def _():
            o_ref[...] = jnp.zeros_like(o_ref)

        o_ref[...] += g_ref[...]

    return pl.pallas_call(
        body, name="meta_grad",
        out_shape=jax.ShapeDtypeStruct((N_META, d), F32),
        grid=(batch,),
        in_specs=[pl.BlockSpec((N_META, d), lambda b: (b * (lp // N_META), 0))],
        out_specs=pl.BlockSpec((N_META, d), lambda b: (0, 0)),
        compiler_params=_params(),
    )(dh0)


def _segment_masks():
    t = lax.broadcasted_iota(jnp.int32, (HG_BLOCK, HG_BLOCK), 0)
    s = lax.broadcasted_iota(jnp.int32, (HG_BLOCK, HG_BLOCK), 1)
    same = lax.shift_right_logical(t, 4) == lax.shift_right_logical(s, 4)
    lower = same & (s <= t)
    upper = same & (s >= t)
    first_half = same & ((s & 15) <= 7)
    return same, lower, upper, first_half


def _hgrn_gates(hq, hf, lb):
    sq = _sigmoid(hq)
    q = hq * sq
    sf = _sigmoid(hf)
    f = lb + (1.0 - lb) * sf
    return q, sq, sf, f


def _hgrn_decays(g, same, lower, first_half):
    b = _exact_dot(lower, g)
    b_last = _exact_dot(same, g)
    b_ref = _exact_dot(first_half, g)
    return b, b_last, b_ref


def _hgrn_fwd(p, lb, gh, *, batch, lp, ride=None):
    rows = batch * lp
    nb = lp // HG_BLOCK
    n_chunks = HG_BLOCK // HG_CHUNK

    def body(hq_ref, hf_ref, hi_ref, hg_ref, lb_ref, gh_ref, *rest):
        if ride is not None:
            rest, exchange = ride.split(rest, 3)
            ride.run((batch, nb), exchange)
        o_ref, z_ref, st_ref, s_scr = rest

        @pl.when(pl.program_id(1) == 0)
        def _():
            s_scr[...] = jnp.zeros_like(s_scr)

        same, lower, _, first_half = _segment_masks()
        v = hi_ref[...]
        q, _, _, f = _hgrn_gates(hq_ref[...], hf_ref[...], lb_ref[...])
        k = 1.0 - f
        b, b_last, b_ref = _hgrn_decays(jnp.log(f), same, lower, first_half)
        qt = _mx(q * jnp.exp(b))
        kh = _mx(k * jnp.exp(b_last - b))
        vm = _mx(v)
        el = jnp.exp(b_last)
        qc = _mx(q * jnp.exp(b - b_ref))
        kc = _mx(k * jnp.exp(b_ref - b))

        def intra(qc_h, kc_h, v_h):
            a = jnp.where(lower, _dot_nt(qc_h, kc_h), 0.0)
            return _dot(_mx(a), v_h)

        o_intra = _heads(intra, qc, kc, vm)

        states = [s_scr[h] for h in range(HEADS)]
        o_inter = [[None] * HEADS for _ in range(n_chunks)]
        for c in range(n_chunks):
            rs = slice(c * HG_CHUNK, (c + 1) * HG_CHUNK)
            for h in range(HEADS):
                cs = slice(h * HEAD_DIM, (h + 1) * HEAD_DIM)
                st_m = _mx(states[h])
                st_ref[c, h] = st_m
                o_inter[c][h] = _dot_nt(qt[rs, cs], st_m)
                states[h] = states[h] * el[c * HG_CHUNK:c * HG_CHUNK + 1, cs] + _dot_tn(vm[rs, cs], kh[rs, cs])
        for h in range(HEADS):
            s_scr[h] = states[h]

        o = o_intra + jnp.concatenate([jnp.concatenate(row, axis=1) for row in o_inter], axis=0)
        o_ref[...] = o
        hg = hg_ref[...]
        n = _heads(lambda o_h: o_h * _rms_scale(o_h), o) * gh_ref[...]
        z_ref[...] = (n * hg * _sigmoid(hg)).astype(z_ref.dtype)

    blk = lambda cb: pl.BlockSpec((HG_BLOCK, D_MODEL), functools.partial(lambda b, j, cb: (b * nb + j, cb), cb=cb))
    row_out = pl.BlockSpec((HG_BLOCK, D_MODEL), lambda b, j: (b * nb + j, 0))
    const = pl.BlockSpec((1, D_MODEL), lambda b, j: (0, 0))
    extra = ride if ride is not None else _NoRide
    return pl.pallas_call(
        body, name="hgrn_fwd",
        out_shape=[jax.ShapeDtypeStruct((rows, D_MODEL), F32),
                   jax.ShapeDtypeStruct((rows, D_MODEL), _MXU_DTYPE),
                   jax.ShapeDtypeStruct((rows // HG_CHUNK, HEADS, HEAD_DIM, HEAD_DIM), _MXU_DTYPE)] + extra.out_shape,
        grid=(batch, nb),
        in_specs=[blk(CB_HQ), blk(CB_HF), blk(CB_HI), blk(CB_HG), const, const] + extra.in_specs,
        out_specs=[row_out, row_out,
                   pl.BlockSpec((n_chunks, HEADS, HEAD_DIM, HEAD_DIM), lambda b, j: (b * nb + j, 0, 0, 0))]
        + extra.out_specs,
        scratch_shapes=[pltpu.VMEM((HEADS, HEAD_DIM, HEAD_DIM), F32)] + extra.scratch,
        compiler_params=_params(),
    )(p, p, p, p, lb, gh, *extra.args)


def _hgrn_bwd(p, o, dz, states, lb, gh, *, batch, lp, ride=None):
    rows = batch * lp
    nb = lp // HG_BLOCK
    n_chunks = HG_BLOCK // HG_CHUNK

    def body(hq_ref, hf_ref, hi_ref, hg_ref, o_ref, dz_ref, st_ref, lb_ref, gh_ref, *rest):
        if ride is not None:
            rest, exchange = ride.split(rest, 3)
            ride.run((batch, nb), exchange)
        dp_ref, dlb_ref, dgh_ref, ds_scr = rest
        first = (pl.program_id(0) == 0) & (pl.program_id(1) == 0)

        @pl.when(first)
        def _():
            dlb_ref[...] = jnp.zeros_like(dlb_ref)
            dgh_ref[...] = jnp.zeros_like(dgh_ref)

        @pl.when(pl.program_id(1) == 0)
        def _():
            ds_scr[...] = jnp.zeros_like(ds_scr)

        same, lower, upper, first_half = _segment_masks()
        lbv = lb_ref[...]
        hq, hf, v, hg = hq_ref[...], hf_ref[...], hi_ref[...], hg_ref[...]
        q, sq, sf, f = _hgrn_gates(hq, hf, lbv)
        k = 1.0 - f
        b, b_last, b_ref = _hgrn_decays(jnp.log(f), same, lower, first_half)
        e_b = jnp.exp(b)
        e_kh = jnp.exp(b_last - b)
        e_qc = jnp.exp(b - b_ref)
        e_kc = jnp.exp(b_ref - b)
        qt, kh, qc, kc = q * e_b, k * e_kh, q * e_qc, k * e_kc

        o = o_ref[...]
        dz = dz_ref[...].astype(F32)
        ghv = gh_ref[...]
        sg = _sigmoid(hg)
        r = _heads(lambda o_h: jnp.broadcast_to(_rms_scale(o_h), o_h.shape), o)
        oh = o * r
        dn = dz * hg * sg
        dhg = dz * oh * ghv * _silu_grad(hg, sg)
        w = dn * ghv
        do = r * (w - oh * _heads(lambda t: jnp.broadcast_to(jnp.mean(t, axis=-1, keepdims=True), t.shape), oh * w))
        dgh_ref[...] += (dn * oh).reshape(HG_BLOCK // _SUBLANES, _SUBLANES, D_MODEL).sum(axis=0)

        qt_m, kh_m, v_m, do_m = _mx(qt), _mx(kh), _mx(v), _mx(do)
        el_all = jnp.exp(b_last)

        def intra(qc_h, kc_h, v_h, do_h):
            a = _mx(jnp.where(lower, _dot_nt(qc_h, kc_h), 0.0))
            da = _mx(jnp.where(lower, _dot_nt(do_h, v_h), 0.0))
            return _dot(da, kc_h), _dot_tn(da, qc_h), _dot_tn(a, do_h)

        dqc, dkc, dv_intra = _heads(intra, _mx(qc), _mx(kc), v_m, do_m)

        d_states = [ds_scr[h] for h in range(HEADS)]
        grid_of = lambda: [[None] * HEADS for _ in range(n_chunks)]
        dkh_p, dv_p, dbl_p, dqt_p = grid_of(), grid_of(), grid_of(), grid_of()
        for c in reversed(range(n_chunks)):
            rs = slice(c * HG_CHUNK, (c + 1) * HG_CHUNK)
            for h in range(HEADS):
                cs = slice(h * HEAD_DIM, (h + 1) * HEAD_DIM)
                st = st_ref[c, h]
                ds_t = d_states[h]
                ds_m = _mx(ds_t)
                el = el_all[c * HG_CHUNK:c * HG_CHUNK + 1, cs]
                dkh_p[c][h] = _dot(v_m[rs, cs], ds_m)
                dv_p[c][h] = _dot_nt(kh_m[rs, cs], ds_m)
                dbl = jnp.sum(ds_t * st.astype(F32), axis=0, keepdims=True) * el
                dbl_p[c][h] = jnp.broadcast_to(dbl, (HG_CHUNK, HEAD_DIM))
                dqt_p[c][h] = _dot(do_m[rs, cs], st)
                d_states[h] = ds_t * el + _dot_tn(do_m[rs, cs], qt_m[rs, cs])
        for h in range(HEADS):
            ds_scr[h] = d_states[h]
        whole = lambda parts: jnp.concatenate([jnp.concatenate(row, axis=1) for row in parts], axis=0)

        dqt, dkh = whole(dqt_p), whole(dkh_p)
        dq = dqt * e_b + dqc * e_qc
        dk = dkh * e_kh + dkc * e_kc
        t_kh = dkh * kh
        db_rows = dqt * qt + dqc * qc - dkc * kc - t_kh
        dg = _exact_dot(upper, db_rows) + _exact_dot(same, t_kh) + whole(dbl_p)
        df = dg / f - dk
        dhf = df * (1.0 - lbv) * sf * (1.0 - sf)
        dlb_ref[...] += (df * (1.0 - sf)).reshape(HG_BLOCK // _SUBLANES, _SUBLANES, D_MODEL).sum(axis=0)
        dhq = dq * _silu_grad(hq, sq)
        dp_ref[...] = jnp.concatenate([dhq, dhf, dv_intra + whole(dv_p), dhg], axis=1).astype(dp_ref.dtype)

    rev = lambda b, j: b * nb + (nb - 1 - j)
    blk = lambda cb: pl.BlockSpec((HG_BLOCK, D_MODEL), functools.partial(lambda b, j, cb: (rev(b, j), cb), cb=cb))
    row = pl.BlockSpec((HG_BLOCK, D_MODEL), lambda b, j: (rev(b, j), 0))
    const = pl.BlockSpec((1, D_MODEL), lambda b, j: (0, 0))
    acc = pl.BlockSpec((_SUBLANES, D_MODEL), lambda b, j: (0, 0))
    extra = ride if ride is not None else _NoRide
    dp, dlb, dgh, *exchanged = pl.pallas_call(
        body, name="hgrn_bwd",
        out_shape=[jax.ShapeDtypeStruct((rows, 4 * D_MODEL), _MXU_DTYPE),
                   jax.ShapeDtypeStruct((_SUBLANES, D_MODEL), F32),
                   jax.ShapeDtypeStruct((_SUBLANES, D_MODEL), F32)] + extra.out_shape,
        grid=(batch, nb),
        in_specs=[blk(CB_HQ), blk(CB_HF), blk(CB_HI), blk(CB_HG), row, row,
                  pl.BlockSpec((n_chunks, HEADS, HEAD_DIM, HEAD_DIM), lambda b, j: (rev(b, j), 0, 0, 0)),
                  const, const] + extra.in_specs,
        out_specs=[pl.BlockSpec((HG_BLOCK, 4 * D_MODEL), lambda b, j: (rev(b, j), 0)), acc, acc] + extra.out_specs,
        scratch_shapes=[pltpu.VMEM((HEADS, HEAD_DIM, HEAD_DIM), F32)] + extra.scratch,
        compiler_params=_params(),
    )(p, p, p, p, o, dz, states, lb, gh, *extra.args)
    return (dp, dlb, dgh, *exchanged)


QK_DIM = 2 * HEAD_DIM
ATTN_TQ_FWD = 512
ATTN_TQ_BWD = 256
ATTN_KEY_CHUNK = 1024


def _query_tiles(lp, tq):
    return [(r0, min(tq, lp - r0)) for r0 in range(0, lp, tq)]


def _attn_fwd(q_cat, kv, kp, *, batch, lp):
    rows = batch * lp

    def body(q_ref, kn_ref, kp_ref, v_ref, o_ref, lse_ref):
        k_cat = jnp.concatenate([kn_ref[...], kp_ref[...]], axis=1)
        for r0, tq in _query_tiles(lp, ATTN_TQ_FWD):
            q_t = q_ref[r0:r0 + tq, :]
            i = lax.broadcasted_iota(jnp.int32, (tq, tq), 0)
            j = lax.broadcasted_iota(jnp.int32, (tq, tq), 1)
            s_diag = jnp.where(j <= i, _dot_nt(q_t, k_cat[r0:r0 + tq]) * ATTN_SCALE, NEG_BIG)
            m = jnp.max(s_diag, axis=1, keepdims=True)
            if r0:
                s_past = _dot_nt(q_t, k_cat[0:r0]) * ATTN_SCALE
                m = jnp.maximum(m, jnp.max(s_past, axis=1, keepdims=True))
            p_diag = jnp.exp(s_diag - m)
            l = jnp.sum(p_diag, axis=1, keepdims=True)
            acc = _dot(_mx(p_diag), v_ref[r0:r0 + tq, :])
            if r0:
                p_past = jnp.exp(s_past - m)
                l = l + jnp.sum(p_past, axis=1, keepdims=True)
                acc = acc + _dot(_mx(p_past), v_ref[0:r0, :])
            o_ref[r0:r0 + tq, :] = (acc / l).astype(o_ref.dtype)
            lse_ref[r0:r0 + tq, :] = jnp.broadcast_to(m + jnp.log(l), (tq, HEAD_DIM))

    head_blk = pl.BlockSpec((lp, HEAD_DIM), lambda b, h: (b, h))
    return pl.pallas_call(
        body, name="attn_fwd",
        out_shape=[jax.ShapeDtypeStruct((rows, D_MODEL), _MXU_DTYPE),
                   jax.ShapeDtypeStruct((rows, D_MODEL), F32)],
        grid=(batch, HEADS),
        in_specs=[pl.BlockSpec((lp, QK_DIM), lambda b, h: (b, h)), head_blk,
                  pl.BlockSpec((lp, HEAD_DIM), lambda b, h: (b, 0)),
                  pl.BlockSpec((lp, HEAD_DIM), lambda b, h: (b, HEADS + h))],
        out_specs=[head_blk, head_blk],
        compiler_params=_params(),
    )(q_cat, kv, kp, kv)


def _attn_bwd(q_cat, kv, kp, do, o, lse, c_tab, s_tab, *, batch, lp, ride=None):
    rows = batch * lp

    def body(q_ref, kn_ref, kp_ref, v_ref, do_ref, o_ref, lse_ref, c_ref, s_ref, *rest):
        if ride is not None:
            rest, exchange = ride.split(rest, 6)
            ride.run((batch, HEADS), exchange)
        dqn_ref, dqc_ref, dqs_ref, dkn_ref, dkp_ref, dv_ref, dk_acc, dv_acc = rest
        dk_acc[...] = jnp.zeros_like(dk_acc)
        dv_acc[...] = jnp.zeros_like(dv_acc)
        k_cat = jnp.concatenate([kn_ref[...], kp_ref[...]], axis=1)
        k_t = k_cat.T
        lane = lax.broadcasted_iota(jnp.int32, (_SUBLANES, HEAD_DIM), 1)
        lse_row = _exact_dot_nt(lane == 0, lse_ref[...])
        delta = _exact_dot_nt(lane >= 0, do_ref[...].astype(F32) * o_ref[...].astype(F32))
        for r0, tq in _query_tiles(lp, ATTN_TQ_BWD):
            cols = slice(r0, r0 + tq)
            q_t_, do_t_ = q_ref[cols, :], do_ref[cols, :]
            lse_t, delta_t = lse_row[0:1, cols], delta[0:1, cols]
            chunks = [(c0, min(ATTN_KEY_CHUNK, r0 - c0), False) for c0 in range(0, r0, ATTN_KEY_CHUNK)] + [(r0, tq, True)]
            dq_t = jnp.zeros((QK_DIM, tq), F32)
            for c0, n, diagonal in chunks:
                keys = slice(c0, c0 + n)
                s = _dot_nt(k_cat[keys], q_t_) * ATTN_SCALE
                if diagonal:
                    jk = lax.broadcasted_iota(jnp.int32, (n, tq), 0)
                    iq = lax.broadcasted_iota(jnp.int32, (n, tq), 1)
                    s = jnp.where(jk <= iq, s, NEG_BIG)
                pexp = jnp.exp(s - lse_t)
                dp = _dot_nt(v_ref[keys, :], do_t_)
                ds = _mx(pexp * (dp - delta_t) * ATTN_SCALE)
                dk_acc[keys, :] += _dot(ds, q_t_)
                dv_acc[keys, :] += _dot(_mx(pexp), do_t_)
                dq_t = dq_t + _dot(k_t[:, keys], ds)
            dq = dq_t.T
            d_rope = dq[:, HEAD_DIM:]
            dqn_ref[cols, :] = dq[:, :HEAD_DIM].astype(dqn_ref.dtype)
            dqc_ref[cols, :] = (d_rope * c_ref[cols, :]).astype(dqc_ref.dtype)
            dqs_ref[cols, :] = (d_rope * s_ref[cols, :]).astype(dqs_ref.dtype)

        dkn_ref[...] = dk_acc[:, 0:HEAD_DIM].astype(dkn_ref.dtype)
        dv_ref[...] = dv_acc[...].astype(dv_ref.dtype)

        @pl.when(pl.program_id(1) == 0)
        def _():
            dkp_ref[...] = jnp.zeros_like(dkp_ref)

        dkp_ref[...] += dk_acc[:, HEAD_DIM:]

    head_blk = pl.BlockSpec((lp, HEAD_DIM), lambda b, h: (b, h))
    cat_blk = pl.BlockSpec((lp, QK_DIM), lambda b, h: (b, h))
    shared_blk = pl.BlockSpec((lp, HEAD_DIM), lambda b, h: (b, 0))
    table_blk = pl.BlockSpec((lp, HEAD_DIM), lambda b, h: (0, 0))
    extra = ride if ride is not None else _NoRide
    return pl.pallas_call(
        body, name="attn_bwd",
        out_shape=[jax.ShapeDtypeStruct((rows, D_MODEL), _MXU_DTYPE)] * 3 + [
                   jax.ShapeDtypeStruct((rows, D_MODEL), _MXU_DTYPE),
                   jax.ShapeDtypeStruct((rows, HEAD_DIM), F32),
                   jax.ShapeDtypeStruct((rows, D_MODEL), _MXU_DTYPE)] + extra.out_shape,
        grid=(batch, HEADS),
        in_specs=[cat_blk, head_blk, shared_blk, pl.BlockSpec((lp, HEAD_DIM), lambda b, h: (b, HEADS + h)),
                  head_blk, head_blk, head_blk, table_blk, table_blk] + extra.in_specs,
        out_specs=[head_blk, head_blk, head_blk, head_blk, shared_blk, head_blk] + extra.out_specs,
        scratch_shapes=[pltpu.VMEM((lp, QK_DIM), F32), pltpu.VMEM((lp, HEAD_DIM), F32)] + extra.scratch,
        compiler_params=_params(),
    )(q_cat, kv, kp, kv, do, o, lse, c_tab, s_tab, *extra.args)


def _all_gather(name, blocks):
    n = len(blocks)

    def body(*refs):
        x_refs, out_refs, (send_sems, recv_sems, local_sems) = refs[:n], refs[n:2 * n], refs[2 * n:]
        x, y, c = lax.axis_index("x"), lax.axis_index("y"), lax.axis_index("c")
        me, sibling = (x, y, c), (x, y, 1 - c)
        chips = [(1 - x, y), (x, 1 - y), (1 - x, 1 - y)]

        def slot(i, px, py, pc):
            return out_refs[i].at[4 * px + 2 * py + pc]

        def copy(i, k, blk, to, src=None):
            return pltpu.make_async_remote_copy(
                src_ref=slot(i, *blk) if src is None else src, dst_ref=slot(i, *blk),
                send_sem=send_sems.at[i, k], recv_sem=recv_sems.at[i, k],
                device_id=to, device_id_type=pl.DeviceIdType.MESH)

        mine = [pltpu.make_async_copy(x_refs[i], slot(i, *me), local_sems.at[i]) for i in range(n)]
        first = [copy(i, 0, me, sibling, src=x_refs[i]) for i in range(n)]
        first += [copy(i, 1 + j, me, (*chip, c), src=x_refs[i]) for i in range(n) for j, chip in enumerate(chips)]
        for cp in mine + first:
            cp.start()
        passed = []
        for i in range(n):
            for j, chip in enumerate(chips):
                copy(i, 1 + j, (*chip, c), me).wait_recv()
                passed.append(copy(i, 4 + j, (*chip, c), sibling))
                passed[-1].start()
        for i in range(n):
            copy(i, 0, sibling, me).wait_recv()
            for j, chip in enumerate(chips):
                copy(i, 4 + j, (*chip, 1 - c), me).wait_recv()
        for cp in first + passed:
            cp.wait_send()
        for cp in mine:
            cp.wait()

    return pl.pallas_call(
        body, name=name,
        out_shape=[jax.ShapeDtypeStruct((N_DEV, *b.shape), b.dtype) for b in blocks],
        in_specs=[pl.BlockSpec(memory_space=pl.ANY)] * n,
        out_specs=[pl.BlockSpec(memory_space=pl.ANY)] * n,
        scratch_shapes=[pltpu.SemaphoreType.DMA((n, 7)), pltpu.SemaphoreType.DMA((n, 7)),
                        pltpu.SemaphoreType.DMA((n,))],
    )(*blocks)


def _adamw_math(w, g, m, v):
    nm = ADAM_B1 * m + (1.0 - ADAM_B1) * g
    nv = ADAM_B2 * v + (1.0 - ADAM_B2) * (g * g)
    m_hat = nm / (1.0 - ADAM_B1 ** ADAM_STEP)
    v_hat = nv / (1.0 - ADAM_B2 ** ADAM_STEP)
    return -ADAM_LR * (m_hat / (jnp.sqrt(v_hat) + ADAM_EPS) + ADAM_WD * w), nm, nv


def _sum_adamw(name, parts, w, m, v):
    rows, cols = w.shape
    tr = rows // 4 if rows % 64 == 0 and rows * cols > (1 << 16) else rows

    def body(p_ref, w_ref, m_ref, v_ref, g_ref, d_ref, nm_ref, nv_ref):
        g = p_ref[0].astype(F32)
        for dev in range(1, N_DEV):
            g = g + p_ref[dev].astype(F32)
        g_ref[...] = g
        d_ref[...], nm_ref[...], nv_ref[...] = _adamw_math(w_ref[...], g, m_ref[...], v_ref[...])

    spec = pl.BlockSpec((tr, cols), lambda i: (i, 0))
    return pl.pallas_call(
        body, name=name,
        out_shape=[jax.ShapeDtypeStruct((rows, cols), F32)] * 4,
        grid=(rows // tr,),
        in_specs=[pl.BlockSpec((N_DEV, tr, cols), lambda i: (0, i, 0))] + [spec] * 3, out_specs=[spec] * 4,
        compiler_params=_params(),
    )(parts, w, m, v)


def _finish_vectors(gathered, lb, params, loss_parts):
    names = list(params)
    n = len(names)

    def body(*refs):
        g_refs, lb_ref, loss_ref = refs[:n], refs[n], refs[n + 1]
        wmv_refs = refs[n + 2:4 * n + 2]
        out_refs, loss_out = refs[4 * n + 2:-1], refs[-1]
        sq = loss_ref[0]
        for dev in range(1, N_DEV):
            sq = sq + loss_ref[dev]
        sq = jnp.sum(jnp.sum(sq, axis=0, keepdims=True), axis=1, keepdims=True)
        loss_out[...] = sq * (0.5 / D_MODEL)
        me = 4 * lax.axis_index("x") + 2 * lax.axis_index("y") + lax.axis_index("c")
        for i, name in enumerate(names):
            g_ref = g_refs[i]
            w_ref, m_ref, v_ref = wmv_refs[3 * i:3 * i + 3]
            if name == "meta_tokens":
                width = w_ref.shape[1]
                mine = pl.ds(pl.multiple_of(me * width, width), width)
                g = g_ref[0, :, mine]
                for dev in range(1, N_DEV):
                    g = g + g_ref[dev, :, mine]
            else:
                g = g_ref[0]
                for dev in range(1, N_DEV):
                    g = g + g_ref[dev]
                g = jnp.sum(g, axis=0, keepdims=True)
                if name == "hg_norm_g":
                    g = functools.reduce(jnp.add, [g[:, h * HEAD_DIM:(h + 1) * HEAD_DIM] for h in range(HEADS)])
                if name == "lb_logits":
                    lbv = lb_ref[...]
                    g = g * lbv * (1.0 - lbv)
                    g = jnp.concatenate([g, -g], axis=0)
            outs = (g, *_adamw_math(w_ref[...], g, m_ref[...], v_ref[...]))
            for ref, val in zip(out_refs[4 * i:4 * i + 4], outs, strict=True):
                ref[...] = val

    args = [gathered[k] for k in names] + [lb, loss_parts] + [t for k in names for t in params[k]]
    res = pl.pallas_call(
        body, name="finish_vectors",
        out_shape=[jax.ShapeDtypeStruct(params[k][0].shape, F32) for k in names for _ in range(4)]
        + [jax.ShapeDtypeStruct((1, 1), F32)],
        compiler_params=_params(),
    )(*args)
    return {k: res[4 * i:4 * i + 4] for i, k in enumerate(names)}, res[-1].reshape(())


def _swap_halves(t):
    half = t.shape[-1] // 2
    return jnp.concatenate([t[..., half:], t[..., :half]], axis=-1)


def _pad_last(t, width):
    return jnp.concatenate([t, jnp.zeros(t.shape[:-1] + (width - t.shape[-1],), t.dtype)], axis=-1)


def _rope_tables(lp):
    pos = jnp.arange(lp, dtype=F32)
    inv_freq = 1.0 / (ROPE_THETA ** (jnp.arange(0, ROPE_DIM, 2, dtype=F32) / ROPE_DIM))
    ang = pos[:, None] * inv_freq[None, :]
    cos, sin = jnp.cos(ang), jnp.sin(ang)
    c128 = _pad_last(jnp.concatenate([cos, cos], axis=1), HEAD_DIM)
    s128 = _pad_last(jnp.concatenate([-sin, sin], axis=1), HEAD_DIM)
    return c128, s128


def _forward_backward(x, target, meta, w, small, *, lp, comm=None):
    batch, seq, d = x.shape
    rows = batch * lp
    tr = 272 if lp % 272 == 0 else 128
    tm = lp // 2
    bf = _MXU_DTYPE
    rw = functools.partial(_rowwise, rows=rows, tr=tr, lp=lp)

    c128, s128 = _rope_tables(lp)
    t_idx = jnp.arange(lp)
    real = jnp.broadcast_to(((t_idx >= N_META) & (t_idx < N_META + seq)).astype(F32)[:, None], (lp, _LANES))

    lb_logits = small["lb_logits"]
    lb = jax.nn.softmax(lb_logits, axis=0)[0:1]
    gh = jnp.tile(small["hg_norm_g"], (1, HEADS))

    h0 = _assemble("assemble_x", x, meta, lp)
    tgt = _assemble("assemble_target", target, jnp.zeros_like(meta), lp)

    (u1,), _ = rw("norm_mix_pre", lambda h, g: ([h * _rms_scale(h) * g], []),
                  ins=[("row", h0, d, 0), ("const", small["mix_pre_g"])], outs=[(d, bf)])
    p = _matmul("proj_in", u1, w["w_in"], out_dtype=F32, tm=tm, tn=IN_COLS_PADDED // 4, tk=1024)

    if comm is None:
        o_hg, z_a, states = _hgrn_fwd(p, lb, gh, batch=batch, lp=lp)
    else:
        o_hg, z_a, states, *gathered = _hgrn_fwd(p, lb, gh, batch=batch, lp=lp, ride=_Ride(comm.rest_payloads, True))
        w = {**w, **comm.rest_weights(gathered)}
    received = []
    scatter = lambda names: _Ride(comm.grad_parts(names, grads), False) if comm is not None else None
    y_a = _matmul("proj_hg_o", z_a, w["w_hg_o"], out_dtype=F32, tm=tm, tn=1024, tk=1024)

    def mla_pre(pc, gq, gkv, ct, st):
        cq, ckv = pc[:, 0:Q_LORA], pc[:, Q_LORA:Q_LORA + KV_LORA]
        kpe, kpe_sw = pc[:, 512:640], pc[:, 640:768]
        return [cq * _rms_scale(cq) * gq, ckv * _rms_scale(ckv) * gkv, kpe * ct + kpe_sw * st], []

    (cqn, ckvn, kp), _ = rw("mla_pre", mla_pre,
                            ins=[("row", p, 1024, CB_C), ("const", small["q_a_norm_g"]),
                                 ("const", small["kv_a_norm_g"]), ("pos", c128), ("pos", s128)],
                            outs=[(Q_LORA, bf), (KV_LORA, bf), (HEAD_DIM, bf)])
    q_cat = _proj_q_rope(cqn, w["w_q"], c128, s128, tm=tm, lp=lp)
    kv = _matmul("proj_kv_b", ckvn, w["w_kv"], out_dtype=bf, tm=tm, tn=1024, tk=KV_LORA)
    o_at, lse = _attn_fwd(q_cat, kv, kp, batch=batch, lp=lp)
    te, te_small = tm, lp // 4

    def merge(yb, pa, pb, ya, bg):
        ga, gb = _sigmoid(pa + bg[:, :d]), _sigmoid(pb + bg[:, d:])
        return [yb, ga * ya + gb * yb], []

    (y_b, mix), _ = _matmul_segments(
        "proj_mla_o", [o_at], w["w_mla_o"], tm=te_small, tn=d, tk=1024,
        epilogue=_Epilogue(merge, rows=[(p, 1024, CB_GA), (p, 1024, CB_GB), y_a], consts=[small["b_gate"]],
                           outs=[(d, F32), (d, bf)], lp=lp))
    def post_mix(mx_, h, g2, g3):
        h1_ = h + mx_ * _rms_scale(mx_) * g2
        return [mx_, h1_, h1_ * _rms_scale(h1_) * g3], []

    (mixed, h1, u2), _ = _matmul_segments(
        "proj_out", [mix], w["w_out"], tm=te, tn=d, tk=1024,
        epilogue=_Epilogue(post_mix, rows=[h0], consts=[small["mix_post_g"], small["ffn_pre_g"]],
                           outs=[(d, F32), (d, F32), (d, bf)], lp=lp))
    act, act_dgate, act_dup = _ffn_in_swiglu(u2, w["w_ffn_in"], tm=tm, tn=1408)

    def post_ffn(fo_, h1_, t_, g4, mask):
        r = _rms_scale(fo_)
        h2 = h1_ + fo_ * r * g4
        err = (h2 - t_) * mask[:, 0:1]
        dh2 = err * (1.0 / d)
        dfo, dg4 = _rms_bwd(fo_, g4, dh2)
        return [dh2, dfo], [err * err, dg4]

    (dh2, dfo), (loss_vec, dg_ffn_post) = _matmul_segments(
        "ffn_out", [act], w["w_ffn_out"], tm=te, tn=d, tk=1408,
        epilogue=_Epilogue(post_ffn, rows=[h1, tgt], consts=[small["ffn_post_g"]], pos=[real],
                           outs=[(d, F32), (d, bf)], accs=[d, d], lp=lp))
    loss = (0.5 / d) * jnp.sum(loss_vec)

    grads = {}
    dw_dt = F32 if comm is None else _WIRE_DTYPE
    dgt, dup = _d_ffn_out_swiglu(dfo, w["w_ffn_out"], act_dgate, act_dup, tm=tm, tn=1408)
    grads["w_ffn_out"] = _matmul_tn("dw_ffn_out", act, dfo, tk=1408, tn=1024, tr=tm, out_dtype=dw_dt)
    grads["w_ffn_in"] = jnp.concatenate([_matmul_tn("dw_ffn_in_gate", u2, dgt, tk=1024, tn=FFN_HIDDEN, tr=tm),
                                         _matmul_tn("dw_ffn_in_up", u2, dup, tk=1024, tn=FFN_HIDDEN, tr=tm)], axis=1)

    def post_mix_bwd(du2_, h1_, dh2_, mx_, g3, g2):
        dx, dg3 = _rms_bwd(h1_, g3, du2_)
        dh1_ = dh2_ + dx
        dmx, dg2 = _rms_bwd(mx_, g2, dh1_)
        return [dh1_, dmx], [dg3, dg2]

    (dh1, dmixed), (dg_ffn_pre, dg_mix_post) = _matmul_segments(
        "d_ffn_in", [dgt, dup], w["w_ffn_in"], tm=te_small, tn=d, tk=1408, b_transposed=True,
        epilogue=_Epilogue(post_mix_bwd, rows=[h1, dh2, mixed], consts=[small["ffn_pre_g"], small["mix_post_g"]],
                           outs=[(d, F32), (d, bf)], accs=[d, d], lp=lp))
    grads["w_out"] = _matmul_tn("dw_out", mix, dmixed, tk=1024, tn=1024, tr=tm, out_dtype=dw_dt)

    def merge_bwd(dm, pa, pb, ya, yb, bg):
        ga, gb = _sigmoid(pa + bg[:, :d]), _sigmoid(pb + bg[:, d:])
        dpg = jnp.concatenate([dm * ya * ga * (1.0 - ga), dm * yb * gb * (1.0 - gb)], axis=1)
        return [dpg, dm * ga, dm * gb], [dpg]

    (dpg, dya, dyb), (db_gate,) = _matmul_segments(
        "d_proj_out", [dmixed], w["w_out"], tm=te_small, tn=d, tk=1024, b_transposed=True,
        epilogue=_Epilogue(merge_bwd, rows=[(p, 1024, CB_GA), (p, 1024, CB_GB), y_a, y_b], consts=[small["b_gate"]],
                           outs=[(2 * d, bf), (d, bf), (d, bf)], accs=[2 * d], lp=lp))
    dz_a = _matmul("d_proj_hg_o", dya, w["w_hg_o"], out_dtype=F32, tm=tm, tn=1024, tk=1024, b_transposed=True)
    grads["w_hg_o"] = _matmul_tn("dw_hg_o", z_a, dya, tk=1024, tn=1024, tr=tm, out_dtype=dw_dt)
    do_at = _matmul("d_proj_mla_o", dyb, w["w_mla_o"], out_dtype=bf, tm=tm, tn=1024, tk=1024, b_transposed=True)
    grads["w_mla_o"] = _matmul_tn("dw_mla_o", o_at, dyb, tk=1024, tn=1024, tr=tm, out_dtype=dw_dt)

    dph, dlb, dgh, *got = _hgrn_bwd(p, o_hg, dz_a, states, lb, gh, batch=batch, lp=lp,
                                    ride=scatter(_GRAD_GROUPS[0]))
    received.append(got)

    res = _attn_bwd(q_cat, kv, kp, do_at, o_at, lse, c128, s128, batch=batch, lp=lp, ride=scatter(_GRAD_GROUPS[1]))
    dq_parts, (dkn, dkp, dvv) = list(res[:3]), res[3:6]
    received.append(list(res[6:]))
    dcqn = _matmul_segments("d_proj_q_b", dq_parts, w["w_q"], tm=tm, tn=Q_LORA, tk=1024, b_transposed=True)
    grads["w_q"] = jnp.concatenate([_matmul_tn(f"dw_q_b_{i}", cqn, part, tk=Q_LORA, tn=1024, tr=tm)
                                    for i, part in enumerate(dq_parts)], axis=1)
    dckvn = _matmul_segments("d_proj_kv_b", [dkn, dvv], w["w_kv"], tm=tm, tn=KV_LORA, tk=1024, b_transposed=True)
    grads["w_k"] = _matmul_tn("dw_k_b", ckvn, dkn, tk=KV_LORA, tn=1024, tr=tm)
    grads["w_v"] = _matmul_tn("dw_v_b", ckvn, dvv, tk=KV_LORA, tn=1024, tr=tm)

    def mla_pre_bwd(pc, dq_, dkv_, dkp_, gq, gkv, ct, st):
        cq, ckv = pc[:, 0:Q_LORA], pc[:, Q_LORA:Q_LORA + KV_LORA]
        dcq, dgq = _rms_bwd(cq, gq, dq_)
        dckv, dgkv = _rms_bwd(ckv, gkv, dkv_)
        dpc = jnp.concatenate([dcq, dckv, dkp_ * ct, dkp_ * st, jnp.zeros((pc.shape[0], 256), F32)], axis=1)
        return [dpc], [dgq, dgkv]

    (dpc,), (dg_q, dg_kv) = rw(
        "mla_pre_bwd", mla_pre_bwd,
        ins=[("row", p, 1024, CB_C), ("row", dcqn, Q_LORA, 0), ("row", dckvn, KV_LORA, 0), ("row", dkp, HEAD_DIM, 0),
             ("const", small["q_a_norm_g"]), ("const", small["kv_a_norm_g"]), ("pos", c128), ("pos", s128)],
        outs=[(1024, bf)], accs=[Q_LORA, KV_LORA])

    grads["w_in"] = (_matmul_tn("dw_in_h", u1, dph, tk=1024, tn=2048, tr=tm),
                     _matmul_tn("dw_in_c", u1, dpc, tk=1024, tn=1024, tr=tm),
                     _matmul_tn("dw_in_g", u1, dpg, tk=1024, tn=2048, tr=tm))
    def pre_bwd(du, h, dh, g1):
        dx, dg1 = _rms_bwd(h, g1, du)
        return [dh + dx], [dg1]

    (dh0,), (dg_mix_pre,), *got = _matmul_segments(
        "d_proj_in", [dph, dpc, dpg], w["w_in"], tm=te, tn=d, tk=1024, b_transposed=True,
        ride=scatter(_GRAD_GROUPS[2]),
        epilogue=_Epilogue(pre_bwd, rows=[h0, dh1], consts=[small["mix_pre_g"]], outs=[(d, F32)], accs=[d], lp=lp))
    if comm is not None:
        received.append(got)
    grad_x = dh0.reshape(batch, lp, d)[:, N_META:N_META + seq]
    partial = {"meta_tokens": _meta_grad(dh0, batch, lp), "lb_logits": dlb, "b_gate": db_gate, "hg_norm_g": dgh,
               "q_a_norm_g": dg_q, "kv_a_norm_g": dg_kv, "mix_pre_g": dg_mix_pre, "mix_post_g": dg_mix_post,
               "ffn_pre_g": dg_ffn_pre, "ffn_post_g": dg_ffn_post, "loss": loss_vec}
    return loss, grad_x, grads, partial, lb, received


_BIG = ["w_in", "w_hg_o", "w_q_b", "w_kv_b", "w_mla_o", "w_out", "w_ffn_in", "w_ffn_out"]
_COLUMN_SHARDED = {"w_in", "w_q_b", "w_kv_b", "w_ffn_in"}
_GRAD_GROUPS = [["w_ffn_in", "w_ffn_out"], ["w_out", "w_hg_o", "w_mla_o"], ["w_in", "w_q_b", "w_kv_b"]]
_SMALL = ["b_gate", "lb_logits", "hg_norm_g", "q_a_norm_g", "kv_a_norm_g", "mix_pre_g", "mix_post_g",
          "ffn_pre_g", "ffn_post_g"]


def _gathered_matrix(name, t):
    _, k, n = t.shape
    if name in _COLUMN_SHARDED:
        return t.transpose(1, 0, 2).reshape(k, N_DEV * n)
    return t.reshape(N_DEV * k, n)


def _scatter_layout(name, full):
    kk, nn = full.shape
    if name in _COLUMN_SHARDED:
        t = full.reshape(kk, N_DEV, nn // N_DEV).transpose(1, 0, 2)
    else:
        t = full.reshape(N_DEV, kk // N_DEV, nn)
    return t.astype(_WIRE_DTYPE)


def _model_w_in(wi):
    z = lambda *s: jnp.zeros(s, wi.dtype)
    kpe = wi[:, 4608:4672]
    c_blk = jnp.concatenate([wi[:, 4096:4608], kpe, z(1024, 64), _swap_halves(kpe), z(1024, 64), z(1024, 256)], axis=1)
    return {"w_in": jnp.concatenate([wi[:, :4096], c_blk, wi[:, 4672:]], axis=1).astype(_MXU_DTYPE)}


def _model_weights(full):
    return {**_model_w_in(full["w_in"]), **_model_rest(full)}


def _model_rest(full):
    wq3 = full["w_q_b"].reshape(Q_LORA, HEADS, HEAD_DIM + ROPE_DIM)
    pe = wq3[:, :, HEAD_DIM:]
    w_q = jnp.concatenate([wq3[:, :, :HEAD_DIM].reshape(Q_LORA, -1),
                           _pad_last(pe, HEAD_DIM).reshape(Q_LORA, -1),
                           _pad_last(_swap_halves(pe), HEAD_DIM).reshape(Q_LORA, -1)], axis=1)
    wkv3 = full["w_kv_b"].reshape(KV_LORA, HEADS, 2 * HEAD_DIM)
    w_k = wkv3[:, :, :HEAD_DIM].reshape(KV_LORA, -1)
    w_v = wkv3[:, :, HEAD_DIM:].reshape(KV_LORA, -1)
    w = {"w_q": w_q, "w_kv": jnp.concatenate([w_k, w_v], axis=1),
         "w_hg_o": full["w_hg_o"], "w_mla_o": full["w_mla_o"], "w_out": full["w_out"],
         "w_ffn_in": full["w_ffn_in"], "w_ffn_out": full["w_ffn_out"]}
    return {k: v.astype(_MXU_DTYPE) for k, v in w.items()}


def _reference_layout_grad(name, g):
    if name == "w_in":
        g_h, g_c, g_g = g["w_in"]
        d_kpe = g_c[:, 512:576] + _swap_halves(g_c[:, 640:704])
        return jnp.concatenate([g_h, g_c[:, :512], d_kpe, g_g], axis=1)
    if name == "w_q_b":
        gq = g["w_q"]
        d_pe = (gq[:, 1024:2048].reshape(Q_LORA, HEADS, HEAD_DIM)[:, :, :ROPE_DIM]
                + _swap_halves(gq[:, 2048:].reshape(Q_LORA, HEADS, HEAD_DIM)[:, :, :ROPE_DIM]))
        return jnp.concatenate([gq[:, :1024].reshape(Q_LORA, HEADS, HEAD_DIM), d_pe], axis=2).reshape(Q_LORA, -1)
    if name == "w_kv_b":
        return jnp.concatenate([g["w_k"].reshape(KV_LORA, HEADS, HEAD_DIM),
                                g["w_v"].reshape(KV_LORA, HEADS, HEAD_DIM)], axis=2).reshape(KV_LORA, -1)
    return g[name]


def _reference_layout_grads(g):
    return {n: _reference_layout_grad(n, g) for n in _BIG}


class _Comm:
    def __init__(self, shard):
        self.rest_payloads = [shard[n].astype(_WIRE_DTYPE) for n in _BIG[1:]]

    def rest_weights(self, gathered):
        return _model_rest({n: _gathered_matrix(n, t) for n, t in zip(_BIG[1:], gathered, strict=True)})

    def grad_parts(self, names, g):
        return [_scatter_layout(n, _reference_layout_grad(n, g)) for n in names]


def kernel(x, meta_tokens, w_in, b_gate, lb_logits, hg_norm_g, w_hg_o, q_a_norm_g, w_q_b, kv_a_norm_g, w_kv_b, w_mla_o, w_out, mix_pre_g, mix_post_g, ffn_pre_g, ffn_post_g, w_ffn_in, w_ffn_out, loss_target, m_meta_tokens, m_w_in, m_b_gate, m_lb_logits, m_hg_norm_g, m_w_hg_o, m_q_a_norm_g, m_w_q_b, m_kv_a_norm_g, m_w_kv_b, m_w_mla_o, m_w_out, m_mix_pre_g, m_mix_post_g, m_ffn_pre_g, m_ffn_post_g, m_w_ffn_in, m_w_ffn_out, v_meta_tokens, v_w_in, v_b_gate, v_lb_logits, v_hg_norm_g, v_w_hg_o, v_q_a_norm_g, v_w_q_b, v_kv_a_norm_g, v_w_kv_b, v_w_mla_o, v_w_out, v_mix_pre_g, v_mix_post_g, v_ffn_pre_g, v_ffn_post_g, v_w_ffn_in, v_w_ffn_out):
    args = dict(locals())
    batch, seq, d = x.shape
    lp = -(-(N_META + seq) // _LANES) * _LANES
    weight_names = ["meta_tokens", "w_in", "b_gate", "lb_logits", "hg_norm_g", "w_hg_o", "q_a_norm_g", "w_q_b",
                    "kv_a_norm_g", "w_kv_b", "w_mla_o", "w_out", "mix_pre_g", "mix_post_g", "ffn_pre_g",
                    "ffn_post_g", "w_ffn_in", "w_ffn_out"]
    shard = {n: args[n].reshape(args[n].shape[-2:]) for n in _BIG}
    comm = _Comm(shard)

    w_in_all, meta_all = _all_gather("gather_first", [shard["w_in"].astype(_WIRE_DTYPE), meta_tokens])
    w_first = _model_w_in(_gathered_matrix("w_in", w_in_all))
    meta_full = meta_all.transpose(1, 0, 2).reshape(N_META, d)
    small = {n: args[n] for n in _SMALL}

    _, grad_x, _, partial, lb, received = _forward_backward(x, loss_target, meta_full, w_first, small, lp=lp, comm=comm)
    out = {}
    for names, bufs in zip(_GRAD_GROUPS, received, strict=True):
        for n, buf in zip(names, bufs, strict=True):
            two_d = lambda t: t.reshape(t.shape[-2:])
            res = _sum_adamw("adamw_" + n, buf, shard[n], two_d(args["m_" + n]), two_d(args["v_" + n]))
            out[n] = [t.reshape(args[n].shape) for t in res]

    vec_names = _SMALL + ["meta_tokens"]
    *gathered, loss_parts = _all_gather("gather_vectors", [partial[n] for n in vec_names + ["loss"]])
    finished, loss = _finish_vectors(dict(zip(vec_names, gathered, strict=True)), lb,
                                     {n: (args[n], args["m_" + n], args["v_" + n]) for n in vec_names}, loss_parts)
    out.update(finished)
    return (loss, grad_x, *[out[n][i] for i in range(4) for n in weight_names])
```

```python
import functools

import jax
import jax.numpy as jnp
from jax import lax
from jax.experimental import pallas as pl
from jax.experimental.pallas import tpu as pltpu

F32 = jnp.float32
_MXU_DTYPE = jnp.bfloat16
_WIRE_DTYPE = jnp.bfloat16
_VMEM_LIMIT_BYTES = 56 * 1024 * 1024
_LANES = 128
_SUBLANES = 8

N_DEV = 8
N_META = 16
NORM_EPS = 1e-6
HEADS = 8
HEAD_DIM = 128
ROPE_DIM = 64
HG_CHUNK = 16
HG_BLOCK = 128
ROPE_THETA = 10000.0
D_MODEL = 1024
Q_LORA = 256
KV_LORA = 256
FFN_HIDDEN = 2816
ATTN_SCALE = (HEAD_DIM + ROPE_DIM) ** -0.5
NEG_BIG = -1e30

ADAM_LR = 0.001
ADAM_B1 = 0.9
ADAM_B2 = 0.999
ADAM_EPS = 1e-08
ADAM_WD = 0.01
ADAM_STEP = 10

CB_HQ, CB_HF, CB_HI, CB_HG, CB_C, CB_GA, CB_GB = range(7)
IN_COLS_PADDED = 7 * 1024


def _params(**kw):
    return pltpu.CompilerParams(vmem_limit_bytes=_VMEM_LIMIT_BYTES, **kw)


def _dot(a, b):
    return lax.dot_general(a, b, (((1,), (0,)), ((), ())), preferred_element_type=F32)


def _dot_nt(a, b):
    return lax.dot_general(a, b, (((1,), (1,)), ((), ())), preferred_element_type=F32)


def _dot_tn(a, b):
    return lax.dot_general(a, b, (((0,), (0,)), ((), ())), preferred_element_type=F32)


def _mx(x):
    return x.astype(_MXU_DTYPE)


def _exact_dot(m01, x, dot=_dot):
    if _MXU_DTYPE == jnp.float32:
        return dot(m01.astype(F32), x)
    m = m01.astype(jnp.bfloat16)
    x1 = x.astype(jnp.bfloat16)
    x2 = (x - x1.astype(F32)).astype(jnp.bfloat16)
    return dot(m, x1) + dot(m, x2)


def _exact_dot_nt(m01, x):
    return _exact_dot(m01, x, dot=_dot_nt)


def _sigmoid(x):
    return jax.nn.sigmoid(x)


def _silu_grad(x, s):
    return s * (1.0 + x * (1.0 - s))


def _rms_scale(x):
    return lax.rsqrt(jnp.mean(x * x, axis=-1, keepdims=True) + NORM_EPS)


def _rms_bwd(x, g, dy):
    r = _rms_scale(x)
    xh = x * r
    w = dy * g
    dx = r * (w - xh * jnp.mean(xh * w, axis=-1, keepdims=True))
    return dx, dy * xh


def _heads(fn, *arrays):
    outs = [fn(*[a[:, h * HEAD_DIM:(h + 1) * HEAD_DIM] for a in arrays]) for h in range(HEADS)]
    if isinstance(outs[0], tuple):
        return tuple(jnp.concatenate([o[i] for o in outs], axis=1) for i in range(len(outs[0])))
    return jnp.concatenate(outs, axis=1)


class _Ride:
    def __init__(self, payloads, gather):
        self.gather, self.args, self.n = gather, list(payloads), len(payloads)
        self.in_specs = [pl.BlockSpec(memory_space=pl.ANY)] * self.n
        self.out_shape = [jax.ShapeDtypeStruct((N_DEV, *p.shape[-2:]), p.dtype) for p in payloads]
        self.out_specs = [pl.BlockSpec(memory_space=pl.ANY)] * self.n
        self.scratch = [pltpu.SemaphoreType.DMA((self.n, N_DEV - 1)), pltpu.SemaphoreType.DMA((self.n, N_DEV - 1)),
                        pltpu.SemaphoreType.DMA((self.n,))]

    def split(self, rest, n_outs):
        n = self.n
        mine = (rest[:n], rest[n + n_outs:2 * n + n_outs], rest[-3:])
        return rest[n:n + n_outs] + rest[2 * n + n_outs:-3], mine

    def _copies(self, p_refs, out_refs, sems):
        send_sems, recv_sems, local_sems = sems
        x, y, c = lax.axis_index("x"), lax.axis_index("y"), lax.axis_index("c")
        me = 4 * x + 2 * y + c
        copies = []
        for i, (p_ref, out_ref) in enumerate(zip(p_refs, out_refs, strict=True)):
            part = (lambda j, p_ref=p_ref: p_ref) if self.gather else (lambda j, p_ref=p_ref: p_ref.at[j])
            copies.append(pltpu.make_async_copy(part(me), out_ref.at[me], local_sems.at[i]))
            for k in range(1, N_DEV):
                px, py, pc = x ^ (k >> 2), y ^ ((k >> 1) & 1), c ^ (k & 1)
                copies.append(pltpu.make_async_remote_copy(
                    src_ref=part(4 * px + 2 * py + pc), dst_ref=out_ref.at[me],
                    send_sem=send_sems.at[i, k - 1], recv_sem=recv_sems.at[i, k - 1],
                    device_id=(px, py, pc), device_id_type=pl.DeviceIdType.MESH))
        return copies

    def run(self, grid, refs):
        ids = [pl.program_id(i) for i in range(len(grid))]
        first = functools.reduce(jnp.logical_and, [i == 0 for i in ids])
        last = functools.reduce(jnp.logical_and, [i == g - 1 for i, g in zip(ids, grid)])

        @pl.when(first)
        def _():
            for cp in self._copies(*refs):
                cp.start()

        @pl.when(last)
        def _():
            for cp in self._copies(*refs):
                cp.wait()


class _NoRide:
    in_specs, out_shape, out_specs, scratch, args = [], [], [], [], []


def _matmul(name, a, b, *, out_dtype, tm, tn, tk, c_in=None, ride=None, b_transposed=False):
    m, k = a.shape
    n = b.shape[0] if b_transposed else b.shape[1]
    assert m % tm == 0 and n % tn == 0 and k % tk == 0, (name, a.shape, b.shape, tm, tn, tk)
    nk = k // tk
    has_c = c_in is not None
    dot = _dot_nt if b_transposed else _dot
    grid = (n // tn, m // tm, nk)
    n_in = 2 + has_c

    def body(*refs):
        a_ref, b_ref = refs[0], refs[1]
        c_ref = refs[2] if has_c else None
        rest = refs[n_in:]
        if ride is not None:
            rest, exchange = ride.split(rest, 1)
            ride.run(grid, exchange)
        o_ref = rest[0]
        acc_ref = rest[1] if nk > 1 else None

        def finish(r):
            if has_c:
                r = r + c_ref[...]
            o_ref[...] = r.astype(o_ref.dtype)

        if nk == 1:
            finish(dot(a_ref[...], b_ref[...]))
        else:
            kk = pl.program_id(2)

            @pl.when(kk == 0)
            def _():
                acc_ref[...] = jnp.zeros_like(acc_ref)

            acc_ref[...] += dot(a_ref[...], b_ref[...])

            @pl.when(kk == nk - 1)
            def _():
                finish(acc_ref[...])

    in_specs = [pl.BlockSpec((tm, tk), lambda j, i, kk: (i, kk)),
                pl.BlockSpec((tn, tk), lambda j, i, kk: (j, kk)) if b_transposed
                else pl.BlockSpec((tk, tn), lambda j, i, kk: (kk, j))]
    args = [a, b]
    aliases = {}
    if has_c:
        in_specs.append(pl.BlockSpec((tm, tn), lambda j, i, kk: (i, j)))
        args.append(c_in)
        aliases = {2: 0}
    out_shape = [jax.ShapeDtypeStruct((m, n), out_dtype)]
    out_specs = [pl.BlockSpec((tm, tn), lambda j, i, kk: (i, j))]
    scratch = [pltpu.VMEM((tm, tn), F32)] if nk > 1 else []
    if ride is not None:
        in_specs, args = in_specs + ride.in_specs, args + ride.args
        out_shape, out_specs, scratch = out_shape + ride.out_shape, out_specs + ride.out_specs, scratch + ride.scratch
    res = pl.pallas_call(
        body, name=name, out_shape=out_shape, grid=grid, in_specs=in_specs, out_specs=out_specs,
        scratch_shapes=scratch, input_output_aliases=aliases, compiler_params=_params(),
    )(*args)
    return res[0] if ride is None else res


EPILOGUE_ROWS = 272


class _Epilogue:
    def __init__(self, fn, *, rows=(), consts=(), pos=(), outs=(), accs=(), lp=None):
        self.fn, self.rows, self.consts, self.pos = fn, list(rows), list(consts), list(pos)
        self.outs, self.accs, self.lp = list(outs), list(accs), lp


def _matmul_segments(name, a_list, b, *, out_dtype=F32, tm, tn, tk, ride=None, b_transposed=False, epilogue=None):
    m = a_list[0].shape[0]
    n, k = b.shape if b_transposed else b.shape[::-1]
    steps = [a.shape[1] // tk for a in a_list]
    offs = [sum(steps[:s]) for s in range(len(steps))]
    nk = sum(steps)
    assert nk * tk == k and m % tm == 0 and n % tn == 0 and all(a.shape[1] % tk == 0 for a in a_list), name
    grid = (n // tn, m // tm, nk)
    n_seg = len(a_list)
    dot = _dot_nt if b_transposed else _dot
    ep = epilogue
    assert ep is None or tn == n, name
    n_extra = 0 if ep is None else len(ep.rows) + len(ep.consts) + len(ep.pos)
    n_outs = 1 if ep is None else len(ep.outs) + len(ep.accs)

    def body(*refs):
        a_refs, b_ref = refs[:n_seg], refs[n_seg]
        extra_refs, rest = refs[n_seg + 1:n_seg + 1 + n_extra], refs[n_seg + 1 + n_extra:]
        if ride is not None:
            rest, exchange = ride.split(rest, n_outs)
            ride.run(grid, exchange)
        out_refs, acc_ref = rest[:n_outs], rest[n_outs]
        i, kk = pl.program_id(1), pl.program_id(2)

        @pl.when(kk == 0)
        def _():
            acc_ref[...] = jnp.zeros_like(acc_ref)

        for s in range(n_seg):
            @pl.when((kk >= offs[s]) & (kk < offs[s] + steps[s]))
            def _(s=s):
                acc_ref[...] += dot(a_refs[s][...], b_ref[...])

        if ep is None:
            @pl.when(kk == nk - 1)
            def _():
                out_refs[0][...] = acc_ref[...].astype(out_refs[0].dtype)
        else:
            sum_refs = out_refs[len(ep.outs):]

            @pl.when((kk == 0) & (i == 0))
            def _():
                for ref in sum_refs:
                    ref[...] = jnp.zeros_like(ref)

            @pl.when(kk == nk - 1)
            def _():
                rs = EPILOGUE_ROWS if tm % EPILOGUE_ROWS == 0 else tm
                n_r, n_c = len(ep.rows), len(ep.consts)
                for r0 in range(0, tm, rs):
                    sl = slice(r0, r0 + rs)
                    tiles = ([r[sl, :] for r in extra_refs[:n_r]] + [c[...] for c in extra_refs[n_r:n_r + n_c]]
                             + [t[sl, :] for t in extra_refs[n_r + n_c:]])
                    res_outs, res_sums = ep.fn(acc_ref[sl, :], *tiles)
                    for ref, val in zip(out_refs[:len(ep.outs)], res_outs, strict=True):
                        ref[sl, :] = val.astype(ref.dtype)
                    for ref, val in zip(sum_refs, res_sums, strict=True):
                        ref[...] += val.reshape(rs // _SUBLANES, _SUBLANES, val.shape[-1]).sum(axis=0)

    seg_spec = lambda s: pl.BlockSpec(
        (tm, tk), functools.partial(lambda j, i, kk, off, ns: (i, jnp.clip(kk - off, 0, ns - 1)), off=offs[s], ns=steps[s]))
    b_spec = (pl.BlockSpec((tn, tk), lambda j, i, kk: (j, kk)) if b_transposed
              else pl.BlockSpec((tk, tn), lambda j, i, kk: (kk, j)))
    in_specs = [seg_spec(s) for s in range(n_seg)] + [b_spec]
    args = list(a_list) + [b]
    row_spec = lambda w: pl.BlockSpec((tm, w), lambda j, i, kk: (i, 0))
    if ep is None:
        out_shape = [jax.ShapeDtypeStruct((m, n), out_dtype)]
        out_specs = [pl.BlockSpec((tm, tn), lambda j, i, kk: (i, j))]
    else:
        tiles_per_example = ep.lp // tm
        row_ins = [r if isinstance(r, tuple) else (r, r.shape[1], 0) for r in ep.rows]
        in_specs += ([pl.BlockSpec((tm, wd), functools.partial(lambda j, i, kk, cb: (i, cb), cb=cb)) for _, wd, cb in row_ins]
                     + [pl.BlockSpec(c.shape, lambda j, i, kk: (0, 0)) for c in ep.consts]
                     + [pl.BlockSpec((tm, p.shape[1]), lambda j, i, kk: (i % tiles_per_example, 0)) for p in ep.pos])
        args += [arr for arr, _, _ in row_ins] + ep.consts + ep.pos
        out_shape = ([jax.ShapeDtypeStruct((m, w), dt) for w, dt in ep.outs]
                     + [jax.ShapeDtypeStruct((_SUBLANES, w), F32) for w in ep.accs])
        out_specs = ([row_spec(w) for w, _ in ep.outs]
                     + [pl.BlockSpec((_SUBLANES, w), lambda j, i, kk: (0, 0)) for w in ep.accs])
    scratch = [pltpu.VMEM((tm, tn), F32)]
    if ride is not None:
        in_specs, args = in_specs + ride.in_specs, args + ride.args
        out_shape, out_specs, scratch = out_shape + ride.out_shape, out_specs + ride.out_specs, scratch + ride.scratch
    res = pl.pallas_call(
        body, name=name, out_shape=out_shape, grid=grid, in_specs=in_specs, out_specs=out_specs,
        scratch_shapes=scratch, compiler_params=_params(),
    )(*args)
    if ep is None:
        return res[0] if ride is None else res
    n_o = len(ep.outs)
    return (res[:n_o], res[n_o:n_outs], *res[n_outs:])


def _matmul_tn(name, x, dy, *, tk, tn, tr, out_dtype=F32):
    r, k = x.shape
    _, n = dy.shape
    assert r % tr == 0 and k % tk == 0 and n % tn == 0, (name, x.shape, dy.shape)
    n_r = r // tr
    direct = out_dtype == F32

    def body(x_ref, dy_ref, o_ref, *scratch):
        acc_ref = o_ref if direct else scratch[0]

        @pl.when(pl.program_id(2) == 0)
        def _():
            acc_ref[...] = jnp.zeros_like(acc_ref)

        acc_ref[...] += _dot_tn(x_ref[...], dy_ref[...])
        if not direct:
            @pl.when(pl.program_id(2) == n_r - 1)
            def _():
                o_ref[...] = acc_ref[...].astype(o_ref.dtype)

    return pl.pallas_call(
        body, name=name,
        out_shape=jax.ShapeDtypeStruct((k, n), out_dtype),
        grid=(k // tk, n // tn, n_r),
        in_specs=[pl.BlockSpec((tr, tk), lambda kb, nb, rr: (rr, kb)),
                  pl.BlockSpec((tr, tn), lambda kb, nb, rr: (rr, nb))],
        out_specs=pl.BlockSpec((tk, tn), lambda kb, nb, rr: (kb, nb)),
        scratch_shapes=[] if direct else [pltpu.VMEM((tk, tn), F32)],
        compiler_params=_params(),
    )(x, dy)


def _ffn_in_swiglu(u, w, *, tm, tn):
    r, k = u.shape
    h = w.shape[1] // 2
    assert r % tm == 0 and h % tn == 0
    nj = h // tn

    def body(u_ref, wg_ref, wu_ref, act_ref, dgate_ref, dup_ref):
        uu = u_ref[...]
        gt, up = _dot(uu, wg_ref[...]), _dot(uu, wu_ref[...])
        s = _sigmoid(gt)
        silu = gt * s
        act_ref[...] = (silu * up).astype(act_ref.dtype)
        dgate_ref[...] = (up * _silu_grad(gt, s)).astype(dgate_ref.dtype)
        dup_ref[...] = silu.astype(dup_ref.dtype)

    tile = pl.BlockSpec((tm, tn), lambda j, i: (i, j))
    return pl.pallas_call(
        body, name="ffn_in_swiglu",
        out_shape=[jax.ShapeDtypeStruct((r, h), _MXU_DTYPE)] * 3,
        grid=(nj, r // tm),
        in_specs=[pl.BlockSpec((tm, k), lambda j, i: (i, 0)),
                  pl.BlockSpec((k, tn), lambda j, i: (0, j)),
                  pl.BlockSpec((k, tn), lambda j, i: (0, nj + j))],
        out_specs=[tile] * 3,
        compiler_params=_params(),
    )(u, w, w)


def _d_ffn_out_swiglu(dy, w, act_dgate, act_dup, *, tm, tn):
    r, k = dy.shape
    h = w.shape[0]
    assert r % tm == 0 and h % tn == 0

    def body(dy_ref, w_ref, pg_ref, pu_ref, dgt_ref, dup_ref):
        da = _dot_nt(dy_ref[...], w_ref[...])
        dgt_ref[...] = (da * pg_ref[...].astype(F32)).astype(dgt_ref.dtype)
        dup_ref[...] = (da * pu_ref[...].astype(F32)).astype(dup_ref.dtype)

    tile = pl.BlockSpec((tm, tn), lambda j, i: (i, j))
    return pl.pallas_call(
        body, name="d_ffn_out_swiglu",
        out_shape=[jax.ShapeDtypeStruct((r, h), _MXU_DTYPE)] * 2,
        grid=(h // tn, r // tm),
        in_specs=[pl.BlockSpec((tm, k), lambda j, i: (i, 0)), pl.BlockSpec((tn, k), lambda j, i: (j, 0)), tile, tile],
        out_specs=[tile] * 2,
        compiler_params=_params(),
    )(dy, w, act_dgate, act_dup)


def _proj_q_rope(cqn, w_q, c_tab, s_tab, *, tm, lp):
    r, k = cqn.shape
    tiles_per_example = lp // tm
    pair = 2 * HEAD_DIM

    def body(x_ref, wn_ref, wp_ref, ws_ref, c_ref, s_ref, o_ref):
        x = x_ref[...]
        c2, s2 = jnp.tile(c_ref[...], (1, 2)), jnp.tile(s_ref[...], (1, 2))
        nope = _dot(x, wn_ref[...])
        roped = _dot(x, wp_ref[...]) * c2 + _dot(x, ws_ref[...]) * s2
        hs = lambda t, h: t[:, h * HEAD_DIM:(h + 1) * HEAD_DIM]
        o_ref[...] = jnp.concatenate([hs(nope, 0), hs(roped, 0), hs(nope, 1), hs(roped, 1)], axis=1).astype(o_ref.dtype)

    w_blk = lambda part: pl.BlockSpec((k, pair), functools.partial(lambda h, i, part: (0, part * (HEADS // 2) + h), part=part))
    tab = pl.BlockSpec((tm, HEAD_DIM), lambda h, i: (i % tiles_per_example, 0))
    return pl.pallas_call(
        body, name="proj_q_rope",
        out_shape=jax.ShapeDtypeStruct((r, HEADS * QK_DIM), _MXU_DTYPE),
        grid=(HEADS // 2, r // tm),
        in_specs=[pl.BlockSpec((tm, k), lambda h, i: (i, 0)), w_blk(0), w_blk(1), w_blk(2), tab, tab],
        out_specs=pl.BlockSpec((tm, 2 * QK_DIM), lambda h, i: (i, h)),
        compiler_params=_params(),
    )(cqn, w_q, w_q, w_q, c_tab, s_tab)


def _rowwise(name, body, *, rows, tr, lp, ins, outs, accs=()):
    assert rows % tr == 0 and lp % tr == 0 and tr % 16 == 0
    tiles_per_example = lp // tr
    in_specs, arrays = [], []
    for spec in ins:
        if spec[0] == "row":
            _, arr, width, cb = spec
            in_specs.append(pl.BlockSpec((tr, width), functools.partial(lambda i, cb: (i, cb), cb=cb)))
        elif spec[0] == "const":
            arr = spec[1]
            in_specs.append(pl.BlockSpec(arr.shape, lambda i: (0, 0)))
        else:
            arr = spec[1]
            in_specs.append(pl.BlockSpec((tr, arr.shape[1]), lambda i: (i % tiles_per_example, 0)))
        arrays.append(arr)
    n_in, n_out = len(ins), len(outs)

    def kern(*refs):
        res_outs, res_accs = body(*[r[...] for r in refs[:n_in]])
        for ref, val in zip(refs[n_in:n_in + n_out], res_outs, strict=True):
            ref[...] = val.astype(ref.dtype)
        acc_refs = refs[n_in + n_out:]
        if acc_refs:
            @pl.when(pl.program_id(0) == 0)
            def _():
                for ref in acc_refs:
                    ref[...] = jnp.zeros_like(ref)

            for ref, val in zip(acc_refs, res_accs, strict=True):
                ref[...] += val.reshape(tr // _SUBLANES, _SUBLANES, val.shape[-1]).sum(axis=0)

    out_shape = ([jax.ShapeDtypeStruct((rows, w), dt) for w, dt in outs]
                 + [jax.ShapeDtypeStruct((_SUBLANES, w), F32) for w in accs])
    out_specs = ([pl.BlockSpec((tr, w), lambda i: (i, 0)) for w, _ in outs]
                 + [pl.BlockSpec((_SUBLANES, w), lambda i: (0, 0)) for w in accs])
    res = pl.pallas_call(
        kern, name=name, out_shape=out_shape, grid=(rows // tr,),
        in_specs=in_specs, out_specs=out_specs, compiler_params=_params(),
    )(*arrays)
    return res[:n_out], list(res[n_out:])


def _assemble(name, x, head_rows, lp):
    batch, seq, d = x.shape
    tc = 256

    def body(x_ref, m_ref, o_ref):
        o_ref[0:N_META, :] = m_ref[...]
        o_ref[N_META:N_META + seq, :] = x_ref[0]
        if lp > N_META + seq:
            o_ref[N_META + seq:, :] = jnp.zeros((lp - N_META - seq, tc), F32)

    return pl.pallas_call(
        body, name=name,
        out_shape=jax.ShapeDtypeStruct((batch * lp, d), F32),
        grid=(batch, d // tc),
        in_specs=[pl.BlockSpec((1, seq, tc), lambda b, j: (b, 0, j)),
                  pl.BlockSpec((N_META, tc), lambda b, j: (0, j))],
        out_specs=pl.BlockSpec((lp, tc), lambda b, j: (b, j)),
        compiler_params=_params(),
    )(x, head_rows)


def _meta_grad(dh0, batch, lp):
    d = dh0.shape[1]

    def body(g_ref, o_ref):
        @pl.when(pl.program_id(0) == 0)
        def _():
            o_ref[...] = jnp.zeros_like(o_ref)

        o_ref[...] += g_ref[...]

    return pl.pallas_call(
        body, name="meta_grad",
        out_shape=jax.ShapeDtypeStruct((N_META, d), F32),
        grid=(batch,),
        in_specs=[pl.BlockSpec((N_META, d), lambda b: (b * (lp // N_META), 0))],
        out_specs=pl.BlockSpec((N_META, d), lambda b: (0, 0)),
        compiler_params=_params(),
    )(dh0)


def _segment_masks():
    t = lax.broadcasted_iota(jnp.int32, (HG_BLOCK, HG_BLOCK), 0)
    s = lax.broadcasted_iota(jnp.int32, (HG_BLOCK, HG_BLOCK), 1)
    same = lax.shift_right_logical(t, 4) == lax.shift_right_logical(s, 4)
    lower = same & (s <= t)
    upper = same & (s >= t)
    first_half = same & ((s & 15) <= 7)
    return same, lower, upper, first_half


def _hgrn_gates(hq, hf, lb):
    sq = _sigmoid(hq)
    q = hq * sq
    sf = _sigmoid(hf)
    f = lb + (1.0 - lb) * sf
    return q, sq, sf, f


def _hgrn_decays(g, same, lower, first_half):
    b = _exact_dot(lower, g)
    b_last = _exact_dot(same, g)
    b_ref = _exact_dot(first_half, g)
    return b, b_last, b_ref


def _hgrn_fwd(p, lb, gh, *, batch, lp, ride=None):
    rows = batch * lp
    nb = lp // HG_BLOCK
    n_chunks = HG_BLOCK // HG_CHUNK

    def body(hq_ref, hf_ref, hi_ref, hg_ref, lb_ref, gh_ref, *rest):
        if ride is not None:
            rest, exchange = ride.split(rest, 3)
            ride.run((batch, nb), exchange)
        o_ref, z_ref, st_ref, s_scr = rest

        @pl.when(pl.program_id(1) == 0)
        def _():
            s_scr[...] = jnp.zeros_like(s_scr)

        same, lower, _, first_half = _segment_masks()
        v = hi_ref[...]
        q, _, _, f = _hgrn_gates(hq_ref[...], hf_ref[...], lb_ref[...])
        k = 1.0 - f
        b, b_last, b_ref = _hgrn_decays(jnp.log(f), same, lower, first_half)
        qt = _mx(q * jnp.exp(b))
        kh = _mx(k * jnp.exp(b_last - b))
        vm = _mx(v)
        el = jnp.exp(b_last)
        qc = _mx(q * jnp.exp(b - b_ref))
        kc = _mx(k * jnp.exp(b_ref - b))

        def intra(qc_h, kc_h, v_h):
            a = jnp.where(lower, _dot_nt(qc_h, kc_h), 0.0)
            return _dot(_mx(a), v_h)

        o_intra = _heads(intra, qc, kc, vm)

        states = [s_scr[h] for h in range(HEADS)]
        o_inter = [[None] * HEADS for _ in range(n_chunks)]
        for c in range(n_chunks):
            rs = slice(c * HG_CHUNK, (c + 1) * HG_CHUNK)
            for h in range(HEADS):
                cs = slice(h * HEAD_DIM, (h + 1) * HEAD_DIM)
                st_m = _mx(states[h])
                st_ref[c, h] = st_m
                o_inter[c][h] = _dot_nt(qt[rs, cs], st_m)
                states[h] = states[h] * el[c * HG_CHUNK:c * HG_CHUNK + 1, cs] + _dot_tn(vm[rs, cs], kh[rs, cs])
        for h in range(HEADS):
            s_scr[h] = states[h]

        o = o_intra + jnp.concatenate([jnp.concatenate(row, axis=1) for row in o_inter], axis=0)
        o_ref[...] = o
        hg = hg_ref[...]
        n = _heads(lambda o_h: o_h * _rms_scale(o_h), o) * gh_ref[...]
        z_ref[...] = (n * hg * _sigmoid(hg)).astype(z_ref.dtype)

    blk = lambda cb: pl.BlockSpec((HG_BLOCK, D_MODEL), functools.partial(lambda b, j, cb: (b * nb + j, cb), cb=cb))
    row_out = pl.BlockSpec((HG_BLOCK, D_MODEL), lambda b, j: (b * nb + j, 0))
    const = pl.BlockSpec((1, D_MODEL), lambda b, j: (0, 0))
    extra = ride if ride is not None else _NoRide
    return pl.pallas_call(
        body, name="hgrn_fwd",
        out_shape=[jax.ShapeDtypeStruct((rows, D_MODEL), F32),
                   jax.ShapeDtypeStruct((rows, D_MODEL), _MXU_DTYPE),
                   jax.ShapeDtypeStruct((rows // HG_CHUNK, HEADS, HEAD_DIM, HEAD_DIM), _MXU_DTYPE)] + extra.out_shape,
        grid=(batch, nb),
        in_specs=[blk(CB_HQ), blk(CB_HF), blk(CB_HI), blk(CB_HG), const, const] + extra.in_specs,
        out_specs=[row_out, row_out,
                   pl.BlockSpec((n_chunks, HEADS, HEAD_DIM, HEAD_DIM), lambda b, j: (b * nb + j, 0, 0, 0))]
        + extra.out_specs,
        scratch_shapes=[pltpu.VMEM((HEADS, HEAD_DIM, HEAD_DIM), F32)] + extra.scratch,
        compiler_params=_params(),
    )(p, p, p, p, lb, gh, *extra.args)


def _hgrn_bwd(p, o, dz, states, lb, gh, *, batch, lp, ride=None):
    rows = batch * lp
    nb = lp // HG_BLOCK
    n_chunks = HG_BLOCK // HG_CHUNK

    def body(hq_ref, hf_ref, hi_ref, hg_ref, o_ref, dz_ref, st_ref, lb_ref, gh_ref, *rest):
        if ride is not None:
            rest, exchange = ride.split(rest, 3)
            ride.run((batch, nb), exchange)
        dp_ref, dlb_ref, dgh_ref, ds_scr = rest
        first = (pl.program_id(0) == 0) & (pl.program_id(1) == 0)

        @pl.when(first)
        def _():
            dlb_ref[...] = jnp.zeros_like(dlb_ref)
            dgh_ref[...] = jnp.zeros_like(dgh_ref)

        @pl.when(pl.program_id(1) == 0)
        def _():
            ds_scr[...] = jnp.zeros_like(ds_scr)

        same, lower, upper, first_half = _segment_masks()
        lbv = lb_ref[...]
        hq, hf, v, hg = hq_ref[...], hf_ref[...], hi_ref[...], hg_ref[...]
        q, sq, sf, f = _hgrn_gates(hq, hf, lbv)
        k = 1.0 - f
        b, b_last, b_ref = _hgrn_decays(jnp.log(f), same, lower, first_half)
        e_b = jnp.exp(b)
        e_kh = jnp.exp(b_last - b)
        e_qc = jnp.exp(b - b_ref)
        e_kc = jnp.exp(b_ref - b)
        qt, kh, qc, kc = q * e_b, k * e_kh, q * e_qc, k * e_kc

        o = o_ref[...]
        dz = dz_ref[...].astype(F32)
        ghv = gh_ref[...]
        sg = _sigmoid(hg)
        r = _heads(lambda o_h: jnp.broadcast_to(_rms_scale(o_h), o_h.shape), o)
        oh = o * r
        dn = dz * hg * sg
        dhg = dz * oh * ghv * _silu_grad(hg, sg)
        w = dn * ghv
        do = r * (w - oh * _heads(lambda t: jnp.broadcast_to(jnp.mean(t, axis=-1, keepdims=True), t.shape), oh * w))
        dgh_ref[...] += (dn * oh).reshape(HG_BLOCK // _SUBLANES, _SUBLANES, D_MODEL).sum(axis=0)

        qt_m, kh_m, v_m, do_m = _mx(qt), _mx(kh), _mx(v), _mx(do)
        el_all = jnp.exp(b_last)

        def intra(qc_h, kc_h, v_h, do_h):
            a = _mx(jnp.where(lower, _dot_nt(qc_h, kc_h), 0.0))
            da = _mx(jnp.where(lower, _dot_nt(do_h, v_h), 0.0))
            return _dot(da, kc_h), _dot_tn(da, qc_h), _dot_tn(a, do_h)

        dqc, dkc, dv_intra = _heads(intra, _mx(qc), _mx(kc), v_m, do_m)

        d_states = [ds_scr[h] for h in range(HEADS)]
        grid_of = lambda: [[None] * HEADS for _ in range(n_chunks)]
        dkh_p, dv_p, dbl_p, dqt_p = grid_of(), grid_of(), grid_of(), grid_of()
        for c in reversed(range(n_chunks)):
            rs = slice(c * HG_CHUNK, (c + 1) * HG_CHUNK)
            for h in range(HEADS):
                cs = slice(h * HEAD_DIM, (h + 1) * HEAD_DIM)
                st = st_ref[c, h]
                ds_t = d_states[h]
                ds_m = _mx(ds_t)
                el = el_all[c * HG_CHUNK:c * HG_CHUNK + 1, cs]
                dkh_p[c][h] = _dot(v_m[rs, cs], ds_m)
                dv_p[c][h] = _dot_nt(kh_m[rs, cs], ds_m)
                dbl = jnp.sum(ds_t * st.astype(F32), axis=0, keepdims=True) * el
                dbl_p[c][h] = jnp.broadcast_to(dbl, (HG_CHUNK, HEAD_DIM))
                dqt_p[c][h] = _dot(do_m[rs, cs], st)
                d_states[h] = ds_t * el + _dot_tn(do_m[rs, cs], qt_m[rs, cs])
        for h in range(HEADS):
            ds_scr[h] = d_states[h]
        whole = lambda parts: jnp.concatenate([jnp.concatenate(row, axis=1) for row in parts], axis=0)

        dqt, dkh = whole(dqt_p), whole(dkh_p)
        dq = dqt * e_b + dqc * e_qc
        dk = dkh * e_kh + dkc * e_kc
        t_kh = dkh * kh
        db_rows = dqt * qt + dqc * qc - dkc * kc - t_kh
        dg = _exact_dot(upper, db_rows) + _exact_dot(same, t_kh) + whole(dbl_p)
        df = dg / f - dk
        dhf = df * (1.0 - lbv) * sf * (1.0 - sf)
        dlb_ref[...] += (df * (1.0 - sf)).reshape(HG_BLOCK // _SUBLANES, _SUBLANES, D_MODEL).sum(axis=0)
        dhq = dq * _silu_grad(hq, sq)
        dp_ref[...] = jnp.concatenate([dhq, dhf, dv_intra + whole(dv_p), dhg], axis=1).astype(dp_ref.dtype)

    rev = lambda b, j: b * nb + (nb - 1 - j)
    blk = lambda cb: pl.BlockSpec((HG_BLOCK, D_MODEL), functools.partial(lambda b, j, cb: (rev(b, j), cb), cb=cb))
    row = pl.BlockSpec((HG_BLOCK, D_MODEL), lambda b, j: (rev(b, j), 0))
    const = pl.BlockSpec((1, D_MODEL), lambda b, j: (0, 0))
    acc = pl.BlockSpec((_SUBLANES, D_MODEL), lambda b, j: (0, 0))
    extra = ride if ride is not None else _NoRide
    dp, dlb, dgh, *exchanged = pl.pallas_call(
        body, name="hgrn_bwd",
        out_shape=[jax.ShapeDtypeStruct((rows, 4 * D_MODEL), _MXU_DTYPE),
                   jax.ShapeDtypeStruct((_SUBLANES, D_MODEL), F32),
                   jax.ShapeDtypeStruct((_SUBLANES, D_MODEL), F32)] + extra.out_shape,
        grid=(batch, nb),
        in_specs=[blk(CB_HQ), blk(CB_HF), blk(CB_HI), blk(CB_HG), row, row,
                  pl.BlockSpec((n_chunks, HEADS, HEAD_DIM, HEAD_DIM), lambda b, j: (rev(b, j), 0, 0, 0)),
                  const, const] + extra.in_specs,
        out_specs=[pl.BlockSpec((HG_BLOCK, 4 * D_MODEL), lambda b, j: (rev(b, j), 0)), acc, acc] + extra.out_specs,
        scratch_shapes=[pltpu.VMEM((HEADS, HEAD_DIM, HEAD_DIM), F32)] + extra.scratch,
        compiler_params=_params(),
    )(p, p, p, p, o, dz, states, lb, gh, *extra.args)
    return (dp, dlb, dgh, *exchanged)


QK_DIM = 2 * HEAD_DIM
ATTN_TQ_FWD = 512
ATTN_TQ_BWD = 256
ATTN_KEY_CHUNK = 1024


def _query_tiles(lp, tq):
    return [(r0, min(tq, lp - r0)) for r0 in range(0, lp, tq)]


def _attn_fwd(q_cat, kv, kp, *, batch, lp):
    rows = batch * lp

    def body(q_ref, kn_ref, kp_ref, v_ref, o_ref, lse_ref):
        k_cat = jnp.concatenate([kn_ref[...], kp_ref[...]], axis=1)
        for r0, tq in _query_tiles(lp, ATTN_TQ_FWD):
            q_t = q_ref[r0:r0 + tq, :]
            i = lax.broadcasted_iota(jnp.int32, (tq, tq), 0)
            j = lax.broadcasted_iota(jnp.int32, (tq, tq), 1)
            s_diag = jnp.where(j <= i, _dot_nt(q_t, k_cat[r0:r0 + tq]) * ATTN_SCALE, NEG_BIG)
            m = jnp.max(s_diag, axis=1, keepdims=True)
            if r0:
                s_past = _dot_nt(q_t, k_cat[0:r0]) * ATTN_SCALE
                m = jnp.maximum(m, jnp.max(s_past, axis=1, keepdims=True))
            p_diag = jnp.exp(s_diag - m)
            l = jnp.sum(p_diag, axis=1, keepdims=True)
            acc = _dot(_mx(p_diag), v_ref[r0:r0 + tq, :])
            if r0:
                p_past = jnp.exp(s_past - m)
                l = l + jnp.sum(p_past, axis=1, keepdims=True)
                acc = acc + _dot(_mx(p_past), v_ref[0:r0, :])
            o_ref[r0:r0 + tq, :] = (acc / l).astype(o_ref.dtype)
            lse_ref[r0:r0 + tq, :] = jnp.broadcast_to(m + jnp.log(l), (tq, HEAD_DIM))

    head_blk = pl.BlockSpec((lp, HEAD_DIM), lambda b, h: (b, h))
    return pl.pallas_call(
        body, name="attn_fwd",
        out_shape=[jax.ShapeDtypeStruct((rows, D_MODEL), _MXU_DTYPE),
                   jax.ShapeDtypeStruct((rows, D_MODEL), F32)],
        grid=(batch, HEADS),
        in_specs=[pl.BlockSpec((lp, QK_DIM), lambda b, h: (b, h)), head_blk,
                  pl.BlockSpec((lp, HEAD_DIM), lambda b, h: (b, 0)),
                  pl.BlockSpec((lp, HEAD_DIM), lambda b, h: (b, HEADS + h))],
        out_specs=[head_blk, head_blk],
        compiler_params=_params(),
    )(q_cat, kv, kp, kv)


def _attn_bwd(q_cat, kv, kp, do, o, lse, c_tab, s_tab, *, batch, lp, ride=None):
    rows = batch * lp

    def body(q_ref, kn_ref, kp_ref, v_ref, do_ref, o_ref, lse_ref, c_ref, s_ref, *rest):
        if ride is not None:
            rest, exchange = ride.split(rest, 6)
            ride.run((batch, HEADS), exchange)
        dqn_ref, dqc_ref, dqs_ref, dkn_ref, dkp_ref, dv_ref, dk_acc, dv_acc = rest
        dk_acc[...] = jnp.zeros_like(dk_acc)
        dv_acc[...] = jnp.zeros_like(dv_acc)
        k_cat = jnp.concatenate([kn_ref[...], kp_ref[...]], axis=1)
        k_t = k_cat.T
        lane = lax.broadcasted_iota(jnp.int32, (_SUBLANES, HEAD_DIM), 1)
        lse_row = _exact_dot_nt(lane == 0, lse_ref[...])
        delta = _exact_dot_nt(lane >= 0, do_ref[...].astype(F32) * o_ref[...].astype(F32))
        for r0, tq in _query_tiles(lp, ATTN_TQ_BWD):
            cols = slice(r0, r0 + tq)
            q_t_, do_t_ = q_ref[cols, :], do_ref[cols, :]
            lse_t, delta_t = lse_row[0:1, cols], delta[0:1, cols]
            chunks = [(c0, min(ATTN_KEY_CHUNK, r0 - c0), False) for c0 in range(0, r0, ATTN_KEY_CHUNK)] + [(r0, tq, True)]
            dq_t = jnp.zeros((QK_DIM, tq), F32)
            for c0, n, diagonal in chunks:
                keys = slice(c0, c0 + n)
                s = _dot_nt(k_cat[keys], q_t_) * ATTN_SCALE
                if diagonal:
                    jk = lax.broadcasted_iota(jnp.int32, (n, tq), 0)
                    iq = lax.broadcasted_iota(jnp.int32, (n, tq), 1)
                    s = jnp.where(jk <= iq, s, NEG_BIG)
                pexp = jnp.exp(s - lse_t)
                dp = _dot_nt(v_ref[keys, :], do_t_)
                ds = _mx(pexp * (dp - delta_t) * ATTN_SCALE)
                dk_acc[keys, :] += _dot(ds, q_t_)
                dv_acc[keys, :] += _dot(_mx(pexp), do_t_)
                dq_t = dq_t + _dot(k_t[:, keys], ds)
            dq = dq_t.T
            d_rope = dq[:, HEAD_DIM:]
            dqn_ref[cols, :] = dq[:, :HEAD_DIM].astype(dqn_ref.dtype)
            dqc_ref[cols, :] = (d_rope * c_ref[cols, :]).astype(dqc_ref.dtype)
            dqs_ref[cols, :] = (d_rope * s_ref[cols, :]).astype(dqs_ref.dtype)

        dkn_ref[...] = dk_acc[:, 0:HEAD_DIM].astype(dkn_ref.dtype)
        dv_ref[...] = dv_acc[...].astype(dv_ref.dtype)

        @pl.when(pl.program_id(1) == 0)
        def _():
            dkp_ref[...] = jnp.zeros_like(dkp_ref)

        dkp_ref[...] += dk_acc[:, HEAD_DIM:]

    head_blk = pl.BlockSpec((lp, HEAD_DIM), lambda b, h: (b, h))
    cat_blk = pl.BlockSpec((lp, QK_DIM), lambda b, h: (b, h))
    shared_blk = pl.BlockSpec((lp, HEAD_DIM), lambda b, h: (b, 0))
    table_blk = pl.BlockSpec((lp, HEAD_DIM), lambda b, h: (0, 0))
    extra = ride if ride is not None else _NoRide
    return pl.pallas_call(
        body, name="attn_bwd",
        out_shape=[jax.ShapeDtypeStruct((rows, D_MODEL), _MXU_DTYPE)] * 3 + [
                   jax.ShapeDtypeStruct((rows, D_MODEL), _MXU_DTYPE),
                   jax.ShapeDtypeStruct((rows, HEAD_DIM), F32),
                   jax.ShapeDtypeStruct((rows, D_MODEL), _MXU_DTYPE)] + extra.out_shape,
        grid=(batch, HEADS),
        in_specs=[cat_blk, head_blk, shared_blk, pl.BlockSpec((lp, HEAD_DIM), lambda b, h: (b, HEADS + h)),
                  head_blk, head_blk, head_blk, table_blk, table_blk] + extra.in_specs,
        out_specs=[head_blk, head_blk, head_blk, head_blk, shared_blk, head_blk] + extra.out_specs,
        scratch_shapes=[pltpu.VMEM((lp, QK_DIM), F32), pltpu.VMEM((lp, HEAD_DIM), F32)] + extra.scratch,
        compiler_params=_params(),
    )(q_cat, kv, kp, kv, do, o, lse, c_tab, s_tab, *extra.args)


def _all_gather(name, blocks):
    n = len(blocks)

    def body(*refs):
        x_refs, out_refs, (send_sems, recv_sems, local_sems) = refs[:n], refs[n:2 * n], refs[2 * n:]
        x, y, c = lax.axis_index("x"), lax.axis_index("y"), lax.axis_index("c")
        me, sibling = (x, y, c), (x, y, 1 - c)
        chips = [(1 - x, y), (x, 1 - y), (1 - x, 1 - y)]

        def slot(i, px, py, pc):
            return out_refs[i].at[4 * px + 2 * py + pc]

        def copy(i, k, blk, to, src=None):
            return pltpu.make_async_remote_copy(
                src_ref=slot(i, *blk) if src is None else src, dst_ref=slot(i, *blk),
                send_sem=send_sems.at[i, k], recv_sem=recv_sems.at[i, k],
                device_id=to, device_id_type=pl.DeviceIdType.MESH)

        mine = [pltpu.make_async_copy(x_refs[i], slot(i, *me), local_sems.at[i]) for i in range(n)]
        first = [copy(i, 0, me, sibling, src=x_refs[i]) for i in range(n)]
        first += [copy(i, 1 + j, me, (*chip, c), src=x_refs[i]) for i in range(n) for j, chip in enumerate(chips)]
        for cp in mine + first:
            cp.start()
        passed = []
        for i in range(n):
            for j, chip in enumerate(chips):
                copy(i, 1 + j, (*chip, c), me).wait_recv()
                passed.append(copy(i, 4 + j, (*chip, c), sibling))
                passed[-1].start()
        for i in range(n):
            copy(i, 0, sibling, me).wait_recv()
            for j, chip in enumerate(chips):
                copy(i, 4 + j, (*chip, 1 - c), me).wait_recv()
        for cp in first + passed:
            cp.wait_send()
        for cp in mine:
            cp.wait()

    return pl.pallas_call(
        body, name=name,
        out_shape=[jax.ShapeDtypeStruct((N_DEV, *b.shape), b.dtype) for b in blocks],
        in_specs=[pl.BlockSpec(memory_space=pl.ANY)] * n,
        out_specs=[pl.BlockSpec(memory_space=pl.ANY)] * n,
        scratch_shapes=[pltpu.SemaphoreType.DMA((n, 7)), pltpu.SemaphoreType.DMA((n, 7)),
                        pltpu.SemaphoreType.DMA((n,))],
    )(*blocks)


def _adamw_math(w, g, m, v):
    nm = ADAM_B1 * m + (1.0 - ADAM_B1) * g
    nv = ADAM_B2 * v + (1.0 - ADAM_B2) * (g * g)
    m_hat = nm / (1.0 - ADAM_B1 ** ADAM_STEP)
    v_hat = nv / (1.0 - ADAM_B2 ** ADAM_STEP)
    return -ADAM_LR * (m_hat / (jnp.sqrt(v_hat) + ADAM_EPS) + ADAM_WD * w), nm, nv


def _sum_adamw(name, parts, w, m, v):
    rows, cols = w.shape
    tr = rows // 4 if rows % 64 == 0 and rows * cols > (1 << 16) else rows

    def body(p_ref, w_ref, m_ref, v_ref, g_ref, d_ref, nm_ref, nv_ref):
        g = p_ref[0].astype(F32)
        for dev in range(1, N_DEV):
            g = g + p_ref[dev].astype(F32)
        g_ref[...] = g
        d_ref[...], nm_ref[...], nv_ref[...] = _adamw_math(w_ref[...], g, m_ref[...], v_ref[...])

    spec = pl.BlockSpec((tr, cols), lambda i: (i, 0))
    return pl.pallas_call(
        body, name=name,
        out_shape=[jax.ShapeDtypeStruct((rows, cols), F32)] * 4,
        grid=(rows // tr,),
        in_specs=[pl.BlockSpec((N_DEV, tr, cols), lambda i: (0, i, 0))] + [spec] * 3, out_specs=[spec] * 4,
        compiler_params=_params(),
    )(parts, w, m, v)


def _finish_vectors(gathered, lb, params, loss_parts):
    names = list(params)
    n = len(names)

    def body(*refs):
        g_refs, lb_ref, loss_ref = refs[:n], refs[n], refs[n + 1]
        wmv_refs = refs[n + 2:4 * n + 2]
        out_refs, loss_out = refs[4 * n + 2:-1], refs[-1]
        sq = loss_ref[0]
        for dev in range(1, N_DEV):
            sq = sq + loss_ref[dev]
        sq = jnp.sum(jnp.sum(sq, axis=0, keepdims=True), axis=1, keepdims=True)
        loss_out[...] = sq * (0.5 / D_MODEL)
        me = 4 * lax.axis_index("x") + 2 * lax.axis_index("y") + lax.axis_index("c")
        for i, name in enumerate(names):
            g_ref = g_refs[i]
            w_ref, m_ref, v_ref = wmv_refs[3 * i:3 * i + 3]
            if name == "meta_tokens":
                width = w_ref.shape[1]
                mine = pl.ds(pl.multiple_of(me * width, width), width)
                g = g_ref[0, :, mine]
                for dev in range(1, N_DEV):
                    g = g + g_ref[dev, :, mine]
            else:
                g = g_ref[0]
                for dev in range(1, N_DEV):
                    g = g + g_ref[dev]
                g = jnp.sum(g, axis=0, keepdims=True)
                if name == "hg_norm_g":
                    g = functools.reduce(jnp.add, [g[:, h * HEAD_DIM:(h + 1) * HEAD_DIM] for h in range(HEADS)])
                if name == "lb_logits":
                    lbv = lb_ref[...]
                    g = g * lbv * (1.0 - lbv)
                    g = jnp.concatenate([g, -g], axis=0)
            outs = (g, *_adamw_math(w_ref[...], g, m_ref[...], v_ref[...]))
            for ref, val in zip(out_refs[4 * i:4 * i + 4], outs, strict=True):
                ref[...] = val

    args = [gathered[k] for k in names] + [lb, loss_parts] + [t for k in names for t in params[k]]
    res = pl.pallas_call(
        body, name="finish_vectors",
        out_shape=[jax.ShapeDtypeStruct(params[k][0].shape, F32) for k in names for _ in range(4)]
        + [jax.ShapeDtypeStruct((1, 1), F32)],
        compiler_params=_params(),
    )(*args)
    return {k: res[4 * i:4 * i + 4] for i, k in enumerate(names)}, res[-1].reshape(())


def _swap_halves(t):
    half = t.shape[-1] // 2
    return jnp.concatenate([t[..., half:], t[..., :half]], axis=-1)


def _pad_last(t, width):
    return jnp.concatenate([t, jnp.zeros(t.shape[:-1] + (width - t.shape[-1],), t.dtype)], axis=-1)


def _rope_tables(lp):
    pos = jnp.arange(lp, dtype=F32)
    inv_freq = 1.0 / (ROPE_THETA ** (jnp.arange(0, ROPE_DIM, 2, dtype=F32) / ROPE_DIM))
    ang = pos[:, None] * inv_freq[None, :]
    cos, sin = jnp.cos(ang), jnp.sin(ang)
    c128 = _pad_last(jnp.concatenate([cos, cos], axis=1), HEAD_DIM)
    s128 = _pad_last(jnp.concatenate([-sin, sin], axis=1), HEAD_DIM)
    return c128, s128


def _forward_backward(x, target, meta, w, small, *, lp, comm=None):
    batch, seq, d = x.shape
    rows = batch * lp
    tr = 272 if lp % 272 == 0 else 128
    tm = lp // 2
    bf = _MXU_DTYPE
    rw = functools.partial(_rowwise, rows=rows, tr=tr, lp=lp)

    c128, s128 = _rope_tables(lp)
    t_idx = jnp.arange(lp)
    real = jnp.broadcast_to(((t_idx >= N_META) & (t_idx < N_META + seq)).astype(F32)[:, None], (lp, _LANES))

    lb_logits = small["lb_logits"]
    lb = jax.nn.softmax(lb_logits, axis=0)[0:1]
    gh = jnp.tile(small["hg_norm_g"], (1, HEADS))

    h0 = _assemble("assemble_x", x, meta, lp)
    tgt = _assemble("assemble_target", target, jnp.zeros_like(meta), lp)

    (u1,), _ = rw("norm_mix_pre", lambda h, g: ([h * _rms_scale(h) * g], []),
                  ins=[("row", h0, d, 0), ("const", small["mix_pre_g"])], outs=[(d, bf)])
    p = _matmul("proj_in", u1, w["w_in"], out_dtype=F32, tm=tm, tn=1024, tk=1024)

    if comm is None:
        o_hg, z_a, states = _hgrn_fwd(p, lb, gh, batch=batch, lp=lp)
    else:
        o_hg, z_a, states, *gathered = _hgrn_fwd(p, lb, gh, batch=batch, lp=lp, ride=_Ride(comm.rest_payloads, True))
        w = {**w, **comm.rest_weights(gathered)}
    received = []
    scatter = lambda names: _Ride(comm.grad_parts(names, grads), False) if comm is not None else None
    y_a = _matmul("proj_hg_o", z_a, w["w_hg_o"], out_dtype=F32, tm=tm, tn=1024, tk=1024)

    def mla_pre(pc, gq, gkv, ct, st):
        cq, ckv = pc[:, 0:Q_LORA], pc[:, Q_LORA:Q_LORA + KV_LORA]
        kpe, kpe_sw = pc[:, 512:640], pc[:, 640:768]
        return [cq * _rms_scale(cq) * gq, ckv * _rms_scale(ckv) * gkv, kpe * ct + kpe_sw * st], []

    (cqn, ckvn, kp), _ = rw("mla_pre", mla_pre,
                            ins=[("row", p, 1024, CB_C), ("const", small["q_a_norm_g"]),
                                 ("const", small["kv_a_norm_g"]), ("pos", c128), ("pos", s128)],
                            outs=[(Q_LORA, bf), (KV_LORA, bf), (HEAD_DIM, bf)])
    q_cat = _proj_q_rope(cqn, w["w_q"], c128, s128, tm=tm, lp=lp)
    kv = _matmul("proj_kv_b", ckvn, w["w_kv"], out_dtype=bf, tm=tm, tn=1024, tk=KV_LORA)
    o_at, lse = _attn_fwd(q_cat, kv, kp, batch=batch, lp=lp)
    te, te_small = tm, lp // 4

    def merge(yb, pa, pb, ya, bg):
        ga, gb = _sigmoid(pa + bg[:, :d]), _sigmoid(pb + bg[:, d:])
        return [yb, ga * ya + gb * yb], []

    (y_b, mix), _ = _matmul_segments(
        "proj_mla_o", [o_at], w["w_mla_o"], tm=te_small, tn=d, tk=1024,
        epilogue=_Epilogue(merge, rows=[(p, 1024, CB_GA), (p, 1024, CB_GB), y_a], consts=[small["b_gate"]],
                           outs=[(d, F32), (d, bf)], lp=lp))
    def post_mix(mx_, h, g2, g3):
        h1_ = h + mx_ * _rms_scale(mx_) * g2
        return [mx_, h1_, h1_ * _rms_scale(h1_) * g3], []

    (mixed, h1, u2), _ = _matmul_segments(
        "proj_out", [mix], w["w_out"], tm=te, tn=d, tk=1024,
        epilogue=_Epilogue(post_mix, rows=[h0], consts=[small["mix_post_g"], small["ffn_pre_g"]],
                           outs=[(d, F32), (d, F32), (d, bf)], lp=lp))
    act, act_dgate, act_dup = _ffn_in_swiglu(u2, w["w_ffn_in"], tm=tm, tn=1408)

    def post_ffn(fo_, h1_, t_, g4, mask):
        r = _rms_scale(fo_)
        h2 = h1_ + fo_ * r * g4
        err = (h2 - t_) * mask[:, 0:1]
        dh2 = err * (1.0 / d)
        dfo, dg4 = _rms_bwd(fo_, g4, dh2)
        return [dh2, dfo], [err * err, dg4]

    (dh2, dfo), (loss_vec, dg_ffn_post) = _matmul_segments(
        "ffn_out", [act], w["w_ffn_out"], tm=te, tn=d, tk=1408,
        epilogue=_Epilogue(post_ffn, rows=[h1, tgt], consts=[small["ffn_post_g"]], pos=[real],
                           outs=[(d, F32), (d, bf)], accs=[d, d], lp=lp))
    loss = (0.5 / d) * jnp.sum(loss_vec)

    grads = {}
    dw_dt = F32 if comm is None else _WIRE_DTYPE
    dgt, dup = _d_ffn_out_swiglu(dfo, w["w_ffn_out"], act_dgate, act_dup, tm=tm, tn=1408)
    grads["w_ffn_out"] = _matmul_tn("dw_ffn_out", act, dfo, tk=1408, tn=1024, tr=tm, out_dtype=dw_dt)
    grads["w_ffn_in"] = jnp.concatenate([_matmul_tn("dw_ffn_in_gate", u2, dgt, tk=1024, tn=FFN_HIDDEN, tr=tm),
                                         _matmul_tn("dw_ffn_in_up", u2, dup, tk=1024, tn=FFN_HIDDEN, tr=tm)], axis=1)

    def post_mix_bwd(du2_, h1_, dh2_, mx_, g3, g2):
        dx, dg3 = _rms_bwd(h1_, g3, du2_)
        dh1_ = dh2_ + dx
        dmx, dg2 = _rms_bwd(mx_, g2, dh1_)
        return [dh1_, dmx], [dg3, dg2]

    (dh1, dmixed), (dg_ffn_pre, dg_mix_post) = _matmul_segments(
        "d_ffn_in", [dgt, dup], w["w_ffn_in"], tm=te_small, tn=d, tk=1408, b_transposed=True,
        epilogue=_Epilogue(post_mix_bwd, rows=[h1, dh2, mixed], consts=[small["ffn_pre_g"], small["mix_post_g"]],
                           outs=[(d, F32), (d, bf)], accs=[d, d], lp=lp))
    grads["w_out"] = _matmul_tn("dw_out", mix, dmixed, tk=1024, tn=1024, tr=tm, out_dtype=dw_dt)

    def merge_bwd(dm, pa, pb, ya, yb, bg):
        ga, gb = _sigmoid(pa + bg[:, :d]), _sigmoid(pb + bg[:, d:])
        dpg = jnp.concatenate([dm * ya * ga * (1.0 - ga), dm * yb * gb * (1.0 - gb)], axis=1)
        return [dpg, dm * ga, dm * gb], [dpg]

    (dpg, dya, dyb), (db_gate,) = _matmul_segments(
        "d_proj_out", [dmixed], w["w_out"], tm=te_small, tn=d, tk=1024, b_transposed=True,
        epilogue=_Epilogue(merge_bwd, rows=[(p, 1024, CB_GA), (p, 1024, CB_GB), y_a, y_b], consts=[small["b_gate"]],
                           outs=[(2 * d, bf), (d, bf), (d, bf)], accs=[2 * d], lp=lp))
    dz_a = _matmul("d_proj_hg_o", dya, w["w_hg_o"], out_dtype=F32, tm=tm, tn=1024, tk=1024, b_transposed=True)
    grads["w_hg_o"] = _matmul_tn("dw_hg_o", z_a, dya, tk=1024, tn=1024, tr=tm, out_dtype=dw_dt)
    do_at = _matmul("d_proj_mla_o", dyb, w["w_mla_o"], out_dtype=bf, tm=tm, tn=1024, tk=1024, b_transposed=True)
    grads["w_mla_o"] = _matmul_tn("dw_mla_o", o_at, dyb, tk=1024, tn=1024, tr=tm, out_dtype=dw_dt)

    dph, dlb, dgh, *got = _hgrn_bwd(p, o_hg, dz_a, states, lb, gh, batch=batch, lp=lp,
                                    ride=scatter(_GRAD_GROUPS[0]))
    received.append(got)

    res = _attn_bwd(q_cat, kv, kp, do_at, o_at, lse, c128, s128, batch=batch, lp=lp, ride=scatter(_GRAD_GROUPS[1]))
    dq_parts, (dkn, dkp, dvv) = list(res[:3]), res[3:6]
    received.append(list(res[6:]))
    dcqn = _matmul_segments("d_proj_q_b", dq_parts, w["w_q"], tm=tm, tn=Q_LORA, tk=1024, b_transposed=True)
    grads["w_q"] = jnp.concatenate([_matmul_tn(f"dw_q_b_{i}", cqn, part, tk=Q_LORA, tn=1024, tr=tm)
                                    for i, part in enumerate(dq_parts)], axis=1)
    grads["w_k"] = _matmul_tn("dw_k_b", ckvn, dkn, tk=KV_LORA, tn=1024, tr=tm)
    grads["w_v"] = _matmul_tn("dw_v_b", ckvn, dvv, tk=KV_LORA, tn=1024, tr=tm)
    dckvn = _matmul_segments("d_proj_kv_b", [dkn, dvv], w["w_kv"], tm=tm, tn=KV_LORA, tk=1024, b_transposed=True,
                             ride=scatter(_GRAD_GROUPS[2]))
    if comm is not None:
        dckvn, *got = dckvn
        received.append(got)

    def mla_pre_bwd(pc, dq_, dkv_, dkp_, gq, gkv, ct, st):
        cq, ckv = pc[:, 0:Q_LORA], pc[:, Q_LORA:Q_LORA + KV_LORA]
        dcq, dgq = _rms_bwd(cq, gq, dq_)
        dckv, dgkv = _rms_bwd(ckv, gkv, dkv_)
        dpc = jnp.concatenate([dcq, dckv, dkp_ * ct, dkp_ * st, jnp.zeros((pc.shape[0], 256), F32)], axis=1)
        return [dpc], [dgq, dgkv]

    (dpc,), (dg_q, dg_kv) = rw(
        "mla_pre_bwd", mla_pre_bwd,
        ins=[("row", p, 1024, CB_C), ("row", dcqn, Q_LORA, 0), ("row", dckvn, KV_LORA, 0), ("row", dkp, HEAD_DIM, 0),
             ("const", small["q_a_norm_g"]), ("const", small["kv_a_norm_g"]), ("pos", c128), ("pos", s128)],
        outs=[(1024, bf)], accs=[Q_LORA, KV_LORA])

    grads["w_in"] = (_matmul_tn("dw_in_h", u1, dph, tk=1024, tn=1024, tr=tm),
                     _matmul_tn("dw_in_c", u1, dpc, tk=1024, tn=1024, tr=tm),
                     _matmul_tn("dw_in_g", u1, dpg, tk=1024, tn=1024, tr=tm))
    def pre_bwd(du, h, dh, g1):
        dx, dg1 = _rms_bwd(h, g1, du)
        return [dh + dx], [dg1]

    (dh0,), (dg_mix_pre,), *got = _matmul_segments(
        "d_proj_in", [dph, dpc, dpg], w["w_in"], tm=te, tn=d, tk=1024, b_transposed=True,
        ride=scatter(_GRAD_GROUPS[3]),
        epilogue=_Epilogue(pre_bwd, rows=[h0, dh1], consts=[small["mix_pre_g"]], outs=[(d, F32)], accs=[d], lp=lp))
    if comm is not None:
        received.append(got)
    grad_x = dh0.reshape(batch, lp, d)[:, N_META:N_META + seq]
    partial = {"meta_tokens": _meta_grad(dh0, batch, lp), "lb_logits": dlb, "b_gate": db_gate, "hg_norm_g": dgh,
               "q_a_norm_g": dg_q, "kv_a_norm_g": dg_kv, "mix_pre_g": dg_mix_pre, "mix_post_g": dg_mix_post,
               "ffn_pre_g": dg_ffn_pre, "ffn_post_g": dg_ffn_post, "loss": loss_vec}
    return loss, grad_x, grads, partial, lb, received


_BIG = ["w_in", "w_hg_o", "w_q_b", "w_kv_b", "w_mla_o", "w_out", "w_ffn_in", "w_ffn_out"]
_COLUMN_SHARDED = {"w_in", "w_q_b", "w_kv_b", "w_ffn_in"}
_GRAD_GROUPS = [["w_ffn_in", "w_ffn_out"], ["w_out", "w_hg_o", "w_mla_o"], ["w_q_b", "w_kv_b"], ["w_in"]]
_SMALL = ["b_gate", "lb_logits", "hg_norm_g", "q_a_norm_g", "kv_a_norm_g", "mix_pre_g", "mix_post_g",
          "ffn_pre_g", "ffn_post_g"]


def _gathered_matrix(name, t):
    _, k, n = t.shape
    if name in _COLUMN_SHARDED:
        return t.transpose(1, 0, 2).reshape(k, N_DEV * n)
    return t.reshape(N_DEV * k, n)


def _scatter_layout(name, full):
    kk, nn = full.shape
    if name in _COLUMN_SHARDED:
        t = full.reshape(kk, N_DEV, nn // N_DEV).transpose(1, 0, 2)
    else:
        t = full.reshape(N_DEV, kk // N_DEV, nn)
    return t.astype(_WIRE_DTYPE)


def _model_w_in(wi):
    z = lambda *s: jnp.zeros(s, wi.dtype)
    kpe = wi[:, 4608:4672]
    c_blk = jnp.concatenate([wi[:, 4096:4608], kpe, z(1024, 64), _swap_halves(kpe), z(1024, 64), z(1024, 256)], axis=1)
    return {"w_in": jnp.concatenate([wi[:, :4096], c_blk, wi[:, 4672:]], axis=1).astype(_MXU_DTYPE)}


def _model_weights(full):
    return {**_model_w_in(full["w_in"]), **_model_rest(full)}


def _model_rest(full):
    wq3 = full["w_q_b"].reshape(Q_LORA, HEADS, HEAD_DIM + ROPE_DIM)
    pe = wq3[:, :, HEAD_DIM:]
    w_q = jnp.concatenate([wq3[:, :, :HEAD_DIM].reshape(Q_LORA, -1),
                           _pad_last(pe, HEAD_DIM).reshape(Q_LORA, -1),
                           _pad_last(_swap_halves(pe), HEAD_DIM).reshape(Q_LORA, -1)], axis=1)
    wkv3 = full["w_kv_b"].reshape(KV_LORA, HEADS, 2 * HEAD_DIM)
    w_k = wkv3[:, :, :HEAD_DIM].reshape(KV_LORA, -1)
    w_v = wkv3[:, :, HEAD_DIM:].reshape(KV_LORA, -1)
    w = {"w_q": w_q, "w_kv": jnp.concatenate([w_k, w_v], axis=1),
         "w_hg_o": full["w_hg_o"], "w_mla_o": full["w_mla_o"], "w_out": full["w_out"],
         "w_ffn_in": full["w_ffn_in"], "w_ffn_out": full["w_ffn_out"]}
    return {k: v.astype(_MXU_DTYPE) for k, v in w.items()}


def _reference_layout_grad(name, g):
    if name == "w_in":
        g_h, g_c, g_g = g["w_in"]
        d_kpe = g_c[:, 512:576] + _swap_halves(g_c[:, 640:704])
        return jnp.concatenate([g_h, g_c[:, :512], d_kpe, g_g], axis=1)
    if name == "w_q_b":
        gq = g["w_q"]
        d_pe = (gq[:, 1024:2048].reshape(Q_LORA, HEADS, HEAD_DIM)[:, :, :ROPE_DIM]
                + _swap_halves(gq[:, 2048:].reshape(Q_LORA, HEADS, HEAD_DIM)[:, :, :ROPE_DIM]))
        return jnp.concatenate([gq[:, :1024].reshape(Q_LORA, HEADS, HEAD_DIM), d_pe], axis=2).reshape(Q_LORA, -1)
    if name == "w_kv_b":
        return jnp.concatenate([g["w_k"].reshape(KV_LORA, HEADS, HEAD_DIM),
                                g["w_v"].reshape(KV_LORA, HEADS, HEAD_DIM)], axis=2).reshape(KV_LORA, -1)
    return g[name]


def _reference_layout_grads(g):
    return {n: _reference_layout_grad(n, g) for n in _BIG}


class _Comm:
    def __init__(self, shard):
        self.rest_payloads = [shard[n].astype(_WIRE_DTYPE) for n in _BIG[1:]]

    def rest_weights(self, gathered):
        return _model_rest({n: _gathered_matrix(n, t) for n, t in zip(_BIG[1:], gathered, strict=True)})

    def grad_parts(self, names, g):
        return [_scatter_layout(n, _reference_layout_grad(n, g)) for n in names]


def kernel(x, meta_tokens, w_in, b_gate, lb_logits, hg_norm_g, w_hg_o, q_a_norm_g, w_q_b, kv_a_norm_g, w_kv_b, w_mla_o, w_out, mix_pre_g, mix_post_g, ffn_pre_g, ffn_post_g, w_ffn_in, w_ffn_out, loss_target, m_meta_tokens, m_w_in, m_b_gate, m_lb_logits, m_hg_norm_g, m_w_hg_o, m_q_a_norm_g, m_w_q_b, m_kv_a_norm_g, m_w_kv_b, m_w_mla_o, m_w_out, m_mix_pre_g, m_mix_post_g, m_ffn_pre_g, m_ffn_post_g, m_w_ffn_in, m_w_ffn_out, v_meta_tokens, v_w_in, v_b_gate, v_lb_logits, v_hg_norm_g, v_w_hg_o, v_q_a_norm_g, v_w_q_b, v_kv_a_norm_g, v_w_kv_b, v_w_mla_o, v_w_out, v_mix_pre_g, v_mix_post_g, v_ffn_pre_g, v_ffn_post_g, v_w_ffn_in, v_w_ffn_out):
    args = dict(locals())
    batch, seq, d = x.shape
    lp = -(-(N_META + seq) // _LANES) * _LANES
    weight_names = ["meta_tokens", "w_in", "b_gate", "lb_logits", "hg_norm_g", "w_hg_o", "q_a_norm_g", "w_q_b",
                    "kv_a_norm_g", "w_kv_b", "w_mla_o", "w_out", "mix_pre_g", "mix_post_g", "ffn_pre_g",
                    "ffn_post_g", "w_ffn_in", "w_ffn_out"]
    shard = {n: args[n].reshape(args[n].shape[-2:]) for n in _BIG}
    comm = _Comm(shard)

    w_in_all, meta_all = _all_gather("gather_first", [shard["w_in"].astype(_WIRE_DTYPE), meta_tokens])
    w_first = _model_w_in(_gathered_matrix("w_in", w_in_all))
    meta_full = meta_all.transpose(1, 0, 2).reshape(N_META, d)
    small = {n: args[n] for n in _SMALL}

    _, grad_x, _, partial, lb, received = _forward_backward(x, loss_target, meta_full, w_first, small, lp=lp, comm=comm)
    out = {}
    for names, bufs in zip(_GRAD_GROUPS, received, strict=True):
        for n, buf in zip(names, bufs, strict=True):
            two_d = lambda t: t.reshape(t.shape[-2:])
            res = _sum_adamw("adamw_" + n, buf, shard[n], two_d(args["m_" + n]), two_d(args["v_" + n]))
            out[n] = [t.reshape(args[n].shape) for t in res]

    vec_names = _SMALL + ["meta_tokens"]
    *gathered, loss_parts = _all_gather("gather_vectors", [partial[n] for n in vec_names + ["loss"]])
    finished, loss = _finish_vectors(dict(zip(vec_names, gathered, strict=True)), lb,
                                     {n: (args[n], args["m_" + n], args["v_" + n]) for n in vec_names}, loss_parts)
    out.update(finished)
    return (loss, grad_x, *[out[n][i] for i in range(4) for n in weight_names])
```
